```python
import jax, jax.numpy as jnp
from jax import lax
import numpy as np

D_MODEL = 1024
BATCH = 16
SEQ = 2048
DEPTH = 2

N_MIXERS = 2
N_A_LAYERS = (DEPTH + 1) // 2
N_B_LAYERS = DEPTH // 2
CONV_KERNEL = 31
POOL_WINDOWS = (2, 4, 8, 16)
N_POOL_GROUPS = len(POOL_WINDOWS)
POOL_GROUP_DIM = D_MODEL // N_POOL_GROUPS
D_FF = ((8 * D_MODEL // 3 + 127) // 128) * 128
FFN_CONV_KERNEL = 3
RMS_EPS = 1e-6
LN_EPS = 1e-5

kernel_name = "hybrid_conformerconv_msnpool_convffn"


def rmsnorm(x, g):
    xf = x.astype(jnp.float32)
    y = xf * lax.rsqrt(jnp.mean(xf * xf, axis=-1, keepdims=True) + RMS_EPS)
    return (y * g.astype(jnp.float32)).astype(x.dtype)


def layernorm(x, g, b):
    xf = x.astype(jnp.float32)
    mu = jnp.mean(xf, axis=-1, keepdims=True)
    var = jnp.mean(jnp.square(xf - mu), axis=-1, keepdims=True)
    y = (xf - mu) * lax.rsqrt(var + LN_EPS)
    return (y * g.astype(jnp.float32) + b.astype(jnp.float32)).astype(x.dtype)


def causal_dwconv(x, w, b):
    k, c = w.shape
    y = lax.conv_general_dilated(
        x, w[:, None, :].astype(x.dtype),
        window_strides=(1,), padding=((k - 1, 0),),
        dimension_numbers=("NWC", "WIO", "NWC"),
        feature_group_count=c)
    return y + b


def conformer_conv_module(h, w_pw1, b_pw1, w_dw, b_dw, ln_g, ln_b, w_pw2, b_pw2):
    d = h.shape[-1]
    a = jnp.einsum("bsd,de->bse", h, w_pw1) + b_pw1
    u = a[..., :d] * jax.nn.sigmoid(a[..., d:])
    u = causal_dwconv(u, w_dw, b_dw)
    u = jax.nn.silu(layernorm(u, ln_g, ln_b))
    return jnp.einsum("bsd,de->bse", u, w_pw2) + b_pw2


def multiscale_pool_mixer(h, w_grp, b_grp, scale):
    bsz, s, d = h.shape
    hf = h.astype(jnp.float32)
    cs = jnp.concatenate(
        [jnp.zeros((bsz, 1, d), jnp.float32), lax.cumsum(hf, axis=1)], axis=1)
    t = jnp.arange(s)
    outs = []
    for g, w in enumerate(POOL_WINDOWS):
        sl = slice(g * POOL_GROUP_DIM, (g + 1) * POOL_GROUP_DIM)
        c = cs[..., sl]
        lagged = jnp.pad(c[:, : s + 1 - w], ((0, 0), (w - 1, 0), (0, 0)))
        win_sum = c[:, 1:] - lagged
        cnt = jnp.minimum(t + 1, w).astype(jnp.float32)[None, :, None]
        outs.append(win_sum / cnt - hf[..., sl])
    pooled = jnp.stack(outs, axis=2).astype(h.dtype)
    mixed = jnp.einsum("bsgc,gce->bsge", pooled, w_grp).reshape(bsz, s, d)
    return scale * (mixed + b_grp)


def conv_ffn(h, w_up, w_dw, b_dw, w_down):
    up = jnp.einsum("bsd,df->bsf", h, w_up)
    act, gate = up[..., :D_FF], up[..., D_FF:]
    act = causal_dwconv(act, w_dw, b_dw)
    return jnp.einsum("bsf,fd->bsd", jax.nn.silu(act) * gate, w_down)


def _fwd_setup_inputs(seed: int = 0) -> dict:
    key = jax.random.key(seed)
    ks = jax.random.split(key, 24)
    f32 = jnp.float32
    D, G, Cg = D_MODEL, N_POOL_GROUPS, POOL_GROUP_DIM
    nrm = lambda k, shape, sc: jax.random.normal(k, shape, f32) * sc
    gain = lambda k, shape: 1.0 + 0.05 * jax.random.normal(k, shape, f32)
    return {
        "x": jax.random.normal(ks[0], (BATCH, SEQ, D), f32),
        "norm_mix": gain(ks[1], (DEPTH, D)),
        "norm_ffn": gain(ks[2], (DEPTH, D)),
        "conv_w_pw1": nrm(ks[3], (N_A_LAYERS, D, 2 * D), D ** -0.5),
        "conv_b_pw1": nrm(ks[4], (N_A_LAYERS, 2 * D), 0.02),
        "conv_w_dw": nrm(ks[5], (N_A_LAYERS, CONV_KERNEL, D), CONV_KERNEL ** -0.5),
        "conv_b_dw": nrm(ks[6], (N_A_LAYERS, D), 0.02),
        "conv_ln_g": gain(ks[7], (N_A_LAYERS, D)),
        "conv_ln_b": nrm(ks[8], (N_A_LAYERS, D), 0.02),
        "conv_w_pw2": nrm(ks[9], (N_A_LAYERS, D, D), D ** -0.5),
        "conv_b_pw2": nrm(ks[10], (N_A_LAYERS, D), 0.02),
        "pool_w": nrm(ks[11], (N_B_LAYERS, G, Cg, Cg), Cg ** -0.5),
        "pool_b": nrm(ks[12], (N_B_LAYERS, D), 0.02),
        "pool_scale": gain(ks[13], (N_B_LAYERS, D)),
        "ffn_w_up": nrm(ks[14], (DEPTH, D, 2 * D_FF), D ** -0.5),
        "ffn_w_dw": nrm(ks[15], (DEPTH, FFN_CONV_KERNEL, D_FF), FFN_CONV_KERNEL ** -0.5),
        "ffn_b_dw": nrm(ks[16], (DEPTH, D_FF), 0.02),
        "ffn_w_down": nrm(ks[17], (DEPTH, D_FF, D), D_FF ** -0.5),
        "final_norm": gain(ks[18], (D,)),
    }


def _fwd_reference(x, norm_mix, norm_ffn, conv_w_pw1, conv_b_pw1, conv_w_dw, conv_b_dw,
              conv_ln_g, conv_ln_b, conv_w_pw2, conv_b_pw2, pool_w, pool_b, pool_scale,
              ffn_w_up, ffn_w_dw, ffn_b_dw, ffn_w_down, final_norm):
    ia, ib = 0, 0
    for i in range(DEPTH):
        h = rmsnorm(x, norm_mix[i])
        if i % N_MIXERS == 0:
            x = x + conformer_conv_module(
                h, conv_w_pw1[ia], conv_b_pw1[ia], conv_w_dw[ia], conv_b_dw[ia],
                conv_ln_g[ia], conv_ln_b[ia], conv_w_pw2[ia], conv_b_pw2[ia])
            ia += 1
        else:
            x = x + multiscale_pool_mixer(h, pool_w[ib], pool_b[ib], pool_scale[ib])
            ib += 1
        h = rmsnorm(x, norm_ffn[i])
        x = x + conv_ffn(h, ffn_w_up[i], ffn_w_dw[i], ffn_b_dw[i], ffn_w_down[i])
    return rmsnorm(x, final_norm)


import jax as _jax
import jax.numpy as _jnp

TWIN_FORMAT = 'train_step'
FWD_PARAMS = ['x', 'norm_mix', 'norm_ffn', 'conv_w_pw1', 'conv_b_pw1', 'conv_w_dw', 'conv_b_dw', 'conv_ln_g', 'conv_ln_b', 'conv_w_pw2', 'conv_b_pw2', 'pool_w', 'pool_b', 'pool_scale', 'ffn_w_up', 'ffn_w_dw', 'ffn_b_dw', 'ffn_w_down', 'final_norm']
TWIN_WEIGHTS = ['norm_mix', 'norm_ffn', 'conv_w_pw1', 'conv_b_pw1', 'conv_w_dw', 'conv_b_dw', 'conv_ln_g', 'conv_ln_b', 'conv_w_pw2', 'conv_b_pw2', 'pool_w', 'pool_b', 'pool_scale', 'ffn_w_up', 'ffn_w_dw', 'ffn_b_dw', 'ffn_w_down', 'final_norm']
TWIN_DIFF_INPUT = 'x'
TWIN_INPUTS = ['x', 'norm_mix', 'norm_ffn', 'conv_w_pw1', 'conv_b_pw1', 'conv_w_dw', 'conv_b_dw', 'conv_ln_g', 'conv_ln_b', 'conv_w_pw2', 'conv_b_pw2', 'pool_w', 'pool_b', 'pool_scale', 'ffn_w_up', 'ffn_w_dw', 'ffn_b_dw', 'ffn_w_down', 'final_norm', 'loss_target', 'm_norm_mix', 'm_norm_ffn', 'm_conv_w_pw1', 'm_conv_b_pw1', 'm_conv_w_dw', 'm_conv_b_dw', 'm_conv_ln_g', 'm_conv_ln_b', 'm_conv_w_pw2', 'm_conv_b_pw2', 'm_pool_w', 'm_pool_b', 'm_pool_scale', 'm_ffn_w_up', 'm_ffn_w_dw', 'm_ffn_b_dw', 'm_ffn_w_down', 'm_final_norm', 'v_norm_mix', 'v_norm_ffn', 'v_conv_w_pw1', 'v_conv_b_pw1', 'v_conv_w_dw', 'v_conv_b_dw', 'v_conv_ln_g', 'v_conv_ln_b', 'v_conv_w_pw2', 'v_conv_b_pw2', 'v_pool_w', 'v_pool_b', 'v_pool_scale', 'v_ffn_w_up', 'v_ffn_w_dw', 'v_ffn_b_dw', 'v_ffn_w_down', 'v_final_norm']
TWIN_OUTPUTS = ['loss', 'grad_x', 'grad_norm_mix', 'grad_norm_ffn', 'grad_conv_w_pw1', 'grad_conv_b_pw1', 'grad_conv_w_dw', 'grad_conv_b_dw', 'grad_conv_ln_g', 'grad_conv_ln_b', 'grad_conv_w_pw2', 'grad_conv_b_pw2', 'grad_pool_w', 'grad_pool_b', 'grad_pool_scale', 'grad_ffn_w_up', 'grad_ffn_w_dw', 'grad_ffn_b_dw', 'grad_ffn_w_down', 'grad_final_norm', 'delta_norm_mix', 'delta_norm_ffn', 'delta_conv_w_pw1', 'delta_conv_b_pw1', 'delta_conv_w_dw', 'delta_conv_b_dw', 'delta_conv_ln_g', 'delta_conv_ln_b', 'delta_conv_w_pw2', 'delta_conv_b_pw2', 'delta_pool_w', 'delta_pool_b', 'delta_pool_scale', 'delta_ffn_w_up', 'delta_ffn_w_dw', 'delta_ffn_b_dw', 'delta_ffn_w_down', 'delta_final_norm', 'new_m_norm_mix', 'new_m_norm_ffn', 'new_m_conv_w_pw1', 'new_m_conv_b_pw1', 'new_m_conv_w_dw', 'new_m_conv_b_dw', 'new_m_conv_ln_g', 'new_m_conv_ln_b', 'new_m_conv_w_pw2', 'new_m_conv_b_pw2', 'new_m_pool_w', 'new_m_pool_b', 'new_m_pool_scale', 'new_m_ffn_w_up', 'new_m_ffn_w_dw', 'new_m_ffn_b_dw', 'new_m_ffn_w_down', 'new_m_final_norm', 'new_v_norm_mix', 'new_v_norm_ffn', 'new_v_conv_w_pw1', 'new_v_conv_b_pw1', 'new_v_conv_w_dw', 'new_v_conv_b_dw', 'new_v_conv_ln_g', 'new_v_conv_ln_b', 'new_v_conv_w_pw2', 'new_v_conv_b_pw2', 'new_v_pool_w', 'new_v_pool_b', 'new_v_pool_scale', 'new_v_ffn_w_up', 'new_v_ffn_w_dw', 'new_v_ffn_b_dw', 'new_v_ffn_w_down', 'new_v_final_norm']
TWIN_LEAF_KINDS = {'loss': 'loss', 'grad_x': 'grad_x', 'grad_norm_mix': 'grad_w', 'grad_norm_ffn': 'grad_w', 'grad_conv_w_pw1': 'grad_w', 'grad_conv_b_pw1': 'grad_w', 'grad_conv_w_dw': 'grad_w', 'grad_conv_b_dw': 'grad_w', 'grad_conv_ln_g': 'grad_w', 'grad_conv_ln_b': 'grad_w', 'grad_conv_w_pw2': 'grad_w', 'grad_conv_b_pw2': 'grad_w', 'grad_pool_w': 'grad_w', 'grad_pool_b': 'grad_w', 'grad_pool_scale': 'grad_w', 'grad_ffn_w_up': 'grad_w', 'grad_ffn_w_dw': 'grad_w', 'grad_ffn_b_dw': 'grad_w', 'grad_ffn_w_down': 'grad_w', 'grad_final_norm': 'grad_w', 'delta_norm_mix': 'delta_w', 'delta_norm_ffn': 'delta_w', 'delta_conv_w_pw1': 'delta_w', 'delta_conv_b_pw1': 'delta_w', 'delta_conv_w_dw': 'delta_w', 'delta_conv_b_dw': 'delta_w', 'delta_conv_ln_g': 'delta_w', 'delta_conv_ln_b': 'delta_w', 'delta_conv_w_pw2': 'delta_w', 'delta_conv_b_pw2': 'delta_w', 'delta_pool_w': 'delta_w', 'delta_pool_b': 'delta_w', 'delta_pool_scale': 'delta_w', 'delta_ffn_w_up': 'delta_w', 'delta_ffn_w_dw': 'delta_w', 'delta_ffn_b_dw': 'delta_w', 'delta_ffn_w_down': 'delta_w', 'delta_final_norm': 'delta_w', 'new_m_norm_mix': 'new_m', 'new_m_norm_ffn': 'new_m', 'new_m_conv_w_pw1': 'new_m', 'new_m_conv_b_pw1': 'new_m', 'new_m_conv_w_dw': 'new_m', 'new_m_conv_b_dw': 'new_m', 'new_m_conv_ln_g': 'new_m', 'new_m_conv_ln_b': 'new_m', 'new_m_conv_w_pw2': 'new_m', 'new_m_conv_b_pw2': 'new_m', 'new_m_pool_w': 'new_m', 'new_m_pool_b': 'new_m', 'new_m_pool_scale': 'new_m', 'new_m_ffn_w_up': 'new_m', 'new_m_ffn_w_dw': 'new_m', 'new_m_ffn_b_dw': 'new_m', 'new_m_ffn_w_down': 'new_m', 'new_m_final_norm': 'new_m', 'new_v_norm_mix': 'new_v', 'new_v_norm_ffn': 'new_v', 'new_v_conv_w_pw1': 'new_v', 'new_v_conv_b_pw1': 'new_v', 'new_v_conv_w_dw': 'new_v', 'new_v_conv_b_dw': 'new_v', 'new_v_conv_ln_g': 'new_v', 'new_v_conv_ln_b': 'new_v', 'new_v_conv_w_pw2': 'new_v', 'new_v_conv_b_pw2': 'new_v', 'new_v_pool_w': 'new_v', 'new_v_pool_b': 'new_v', 'new_v_pool_scale': 'new_v', 'new_v_ffn_w_up': 'new_v', 'new_v_ffn_w_dw': 'new_v', 'new_v_ffn_b_dw': 'new_v', 'new_v_ffn_w_down': 'new_v', 'new_v_final_norm': 'new_v'}


def _forward(args):
    return _fwd_reference(*[args[k] for k in FWD_PARAMS])


def _output_shape():
    out = _jax.eval_shape(lambda: _forward(_fwd_setup_inputs(0)))
    return out.shape, out.dtype

N_MICROBATCH = 1
ADAM_LR = 0.001
ADAM_B1 = 0.9
ADAM_B2 = 0.999
ADAM_EPS = 1e-08
ADAM_WD = 0.01
ADAM_STEP = 10
PER_EXAMPLE_BATCH_AXIS = {'x': 0, 'loss_target': 0}
SHARED_INPUTS = []
_WEIGHT_DTYPES = {'norm_mix': _jnp.float32, 'norm_ffn': _jnp.float32, 'conv_w_pw1': _jnp.float32, 'conv_b_pw1': _jnp.float32, 'conv_w_dw': _jnp.float32, 'conv_b_dw': _jnp.float32, 'conv_ln_g': _jnp.float32, 'conv_ln_b': _jnp.float32, 'conv_w_pw2': _jnp.float32, 'conv_b_pw2': _jnp.float32, 'pool_w': _jnp.float32, 'pool_b': _jnp.float32, 'pool_scale': _jnp.float32, 'ffn_w_up': _jnp.float32, 'ffn_w_dw': _jnp.float32, 'ffn_b_dw': _jnp.float32, 'ffn_w_down': _jnp.float32, 'final_norm': _jnp.float32}
MOMENT_SCALE = {'norm_mix': 1.269409e-01, 'norm_ffn': 1.142444e-01, 'conv_w_pw1': 8.530176e-02, 'conv_b_pw1': 8.802595e-02, 'conv_w_dw': 1.135699e-01, 'conv_b_dw': 2.218652e-01, 'conv_ln_g': 1.361602e-01, 'conv_ln_b': 1.208345e-01, 'conv_w_pw2': 1.117901e-01, 'conv_b_pw2': 2.239065e-01, 'pool_w': 1.143792e-01, 'pool_b': 1.962149e-01, 'pool_scale': 7.297389e-01, 'ffn_w_up': 4.813202e-02, 'ffn_w_dw': 5.022843e-02, 'ffn_b_dw': 4.628222e-02, 'ffn_w_down': 7.918851e-02, 'final_norm': 3.213877e+01}


def _to_microbatches(a, axis):
    t = _jnp.moveaxis(a, axis, 0)
    t = t.reshape((N_MICROBATCH, t.shape[0] // N_MICROBATCH) + t.shape[1:])
    return _jnp.moveaxis(t, 1, axis + 1)


def setup_inputs(seed: int = 0) -> dict:
    inp = _fwd_setup_inputs(seed)
    key = _jax.random.fold_in(_jax.random.key(seed), 7919)
    shape, _ = _output_shape()
    out = dict(inp)
    out["loss_target"] = _jax.random.normal(_jax.random.fold_in(key, 0), shape, _jnp.float32)
    for i, name in enumerate(TWIN_WEIGHTS):
        w = inp[name].astype(_jnp.float32)
        if MOMENT_SCALE is None:
            s = _jnp.sqrt(_jnp.mean(_jnp.square(w)) + 1e-30)
        else:
            s = MOMENT_SCALE[name]
        km, kv = _jax.random.split(_jax.random.fold_in(key, i + 1))
        out[name] = w
        out["m_" + name] = s * _jax.random.normal(km, w.shape, _jnp.float32)
        out["v_" + name] = (s * s) * _jax.random.uniform(kv, w.shape, _jnp.float32, 0.5, 1.5)
    if N_MICROBATCH > 1:
        for name, axis in PER_EXAMPLE_BATCH_AXIS.items():
            out[name] = _to_microbatches(out[name], axis)
    return {'x': out['x'], 'norm_mix': out['norm_mix'], 'norm_ffn': out['norm_ffn'], 'conv_w_pw1': out['conv_w_pw1'], 'conv_b_pw1': out['conv_b_pw1'], 'conv_w_dw': out['conv_w_dw'], 'conv_b_dw': out['conv_b_dw'], 'conv_ln_g': out['conv_ln_g'], 'conv_ln_b': out['conv_ln_b'], 'conv_w_pw2': out['conv_w_pw2'], 'conv_b_pw2': out['conv_b_pw2'], 'pool_w': out['pool_w'], 'pool_b': out['pool_b'], 'pool_scale': out['pool_scale'], 'ffn_w_up': out['ffn_w_up'], 'ffn_w_dw': out['ffn_w_dw'], 'ffn_b_dw': out['ffn_b_dw'], 'ffn_w_down': out['ffn_w_down'], 'final_norm': out['final_norm'], 'loss_target': out['loss_target'], 'm_norm_mix': out['m_norm_mix'], 'm_norm_ffn': out['m_norm_ffn'], 'm_conv_w_pw1': out['m_conv_w_pw1'], 'm_conv_b_pw1': out['m_conv_b_pw1'], 'm_conv_w_dw': out['m_conv_w_dw'], 'm_conv_b_dw': out['m_conv_b_dw'], 'm_conv_ln_g': out['m_conv_ln_g'], 'm_conv_ln_b': out['m_conv_ln_b'], 'm_conv_w_pw2': out['m_conv_w_pw2'], 'm_conv_b_pw2': out['m_conv_b_pw2'], 'm_pool_w': out['m_pool_w'], 'm_pool_b': out['m_pool_b'], 'm_pool_scale': out['m_pool_scale'], 'm_ffn_w_up': out['m_ffn_w_up'], 'm_ffn_w_dw': out['m_ffn_w_dw'], 'm_ffn_b_dw': out['m_ffn_b_dw'], 'm_ffn_w_down': out['m_ffn_w_down'], 'm_final_norm': out['m_final_norm'], 'v_norm_mix': out['v_norm_mix'], 'v_norm_ffn': out['v_norm_ffn'], 'v_conv_w_pw1': out['v_conv_w_pw1'], 'v_conv_b_pw1': out['v_conv_b_pw1'], 'v_conv_w_dw': out['v_conv_w_dw'], 'v_conv_b_dw': out['v_conv_b_dw'], 'v_conv_ln_g': out['v_conv_ln_g'], 'v_conv_ln_b': out['v_conv_ln_b'], 'v_conv_w_pw2': out['v_conv_w_pw2'], 'v_conv_b_pw2': out['v_conv_b_pw2'], 'v_pool_w': out['v_pool_w'], 'v_pool_b': out['v_pool_b'], 'v_pool_scale': out['v_pool_scale'], 'v_ffn_w_up': out['v_ffn_w_up'], 'v_ffn_w_dw': out['v_ffn_w_dw'], 'v_ffn_b_dw': out['v_ffn_b_dw'], 'v_ffn_w_down': out['v_ffn_w_down'], 'v_final_norm': out['v_final_norm']}


def _loss(weights, diff, rest, loss_target):
    with _jax.named_scope("forward"):
        args = {**rest, TWIN_DIFF_INPUT: diff, **{k: w.astype(_WEIGHT_DTYPES[k]) for k, w in weights.items()}}
        y = _forward(args)
    with _jax.named_scope("loss_head"):
        err = _jnp.square(y.astype(_jnp.float32) - loss_target)
        return 0.5 * _jnp.sum(_jnp.mean(err, axis=-1)) if err.ndim else 0.5 * err


def _adamw(w, g, m, v):
    m = ADAM_B1 * m + (1.0 - ADAM_B1) * g
    v = ADAM_B2 * v + (1.0 - ADAM_B2) * _jnp.square(g)
    m_hat = m / (1.0 - ADAM_B1 ** ADAM_STEP)
    v_hat = v / (1.0 - ADAM_B2 ** ADAM_STEP)
    delta = -ADAM_LR * (m_hat / (_jnp.sqrt(v_hat) + ADAM_EPS) + ADAM_WD * w)
    return delta, m, v


def reference(x, norm_mix, norm_ffn, conv_w_pw1, conv_b_pw1, conv_w_dw, conv_b_dw, conv_ln_g, conv_ln_b, conv_w_pw2, conv_b_pw2, pool_w, pool_b, pool_scale, ffn_w_up, ffn_w_dw, ffn_b_dw, ffn_w_down, final_norm, loss_target, m_norm_mix, m_norm_ffn, m_conv_w_pw1, m_conv_b_pw1, m_conv_w_dw, m_conv_b_dw, m_conv_ln_g, m_conv_ln_b, m_conv_w_pw2, m_conv_b_pw2, m_pool_w, m_pool_b, m_pool_scale, m_ffn_w_up, m_ffn_w_dw, m_ffn_b_dw, m_ffn_w_down, m_final_norm, v_norm_mix, v_norm_ffn, v_conv_w_pw1, v_conv_b_pw1, v_conv_w_dw, v_conv_b_dw, v_conv_ln_g, v_conv_ln_b, v_conv_w_pw2, v_conv_b_pw2, v_pool_w, v_pool_b, v_pool_scale, v_ffn_w_up, v_ffn_w_dw, v_ffn_b_dw, v_ffn_w_down, v_final_norm):
    given = dict(x=x, norm_mix=norm_mix, norm_ffn=norm_ffn, conv_w_pw1=conv_w_pw1, conv_b_pw1=conv_b_pw1, conv_w_dw=conv_w_dw, conv_b_dw=conv_b_dw, conv_ln_g=conv_ln_g, conv_ln_b=conv_ln_b, conv_w_pw2=conv_w_pw2, conv_b_pw2=conv_b_pw2, pool_w=pool_w, pool_b=pool_b, pool_scale=pool_scale, ffn_w_up=ffn_w_up, ffn_w_dw=ffn_w_dw, ffn_b_dw=ffn_b_dw, ffn_w_down=ffn_w_down, final_norm=final_norm, loss_target=loss_target, m_norm_mix=m_norm_mix, m_norm_ffn=m_norm_ffn, m_conv_w_pw1=m_conv_w_pw1, m_conv_b_pw1=m_conv_b_pw1, m_conv_w_dw=m_conv_w_dw, m_conv_b_dw=m_conv_b_dw, m_conv_ln_g=m_conv_ln_g, m_conv_ln_b=m_conv_ln_b, m_conv_w_pw2=m_conv_w_pw2, m_conv_b_pw2=m_conv_b_pw2, m_pool_w=m_pool_w, m_pool_b=m_pool_b, m_pool_scale=m_pool_scale, m_ffn_w_up=m_ffn_w_up, m_ffn_w_dw=m_ffn_w_dw, m_ffn_b_dw=m_ffn_b_dw, m_ffn_w_down=m_ffn_w_down, m_final_norm=m_final_norm, v_norm_mix=v_norm_mix, v_norm_ffn=v_norm_ffn, v_conv_w_pw1=v_conv_w_pw1, v_conv_b_pw1=v_conv_b_pw1, v_conv_w_dw=v_conv_w_dw, v_conv_b_dw=v_conv_b_dw, v_conv_ln_g=v_conv_ln_g, v_conv_ln_b=v_conv_ln_b, v_conv_w_pw2=v_conv_w_pw2, v_conv_b_pw2=v_conv_b_pw2, v_pool_w=v_pool_w, v_pool_b=v_pool_b, v_pool_scale=v_pool_scale, v_ffn_w_up=v_ffn_w_up, v_ffn_w_dw=v_ffn_w_dw, v_ffn_b_dw=v_ffn_b_dw, v_ffn_w_down=v_ffn_w_down, v_final_norm=v_final_norm)
    weights = {n: given[n] for n in TWIN_WEIGHTS}
    shared = {n: given[n] for n in SHARED_INPUTS}
    per_example = {n: given[n] for n in ['x']}
    grad_fn = _jax.value_and_grad(_loss, argnums=(0, 1))

    def one_microbatch(ex, loss_target):
        ex = dict(ex)
        diff = ex.pop(TWIN_DIFF_INPUT)
        return grad_fn(weights, diff, {**shared, **ex}, loss_target)

    if N_MICROBATCH == 1:
        loss, (grad_w, grad_x) = one_microbatch(per_example, given["loss_target"])
    else:
        def body(carry, xs):
            loss_sum, grad_sum = carry
            l_k, (gw_k, gx_k) = one_microbatch(xs[0], xs[1])
            with _jax.named_scope("update"):
                return (loss_sum + l_k, _jax.tree.map(_jnp.add, grad_sum, gw_k)), gx_k

        init = (_jnp.zeros((), _jnp.float32), _jax.tree.map(_jnp.zeros_like, weights))
        (loss, grad_w), grad_x = _jax.lax.scan(body, init, (per_example, given["loss_target"]))
    with _jax.named_scope("update"):
        delta_w, new_m, new_v = {}, {}, {}
        for n in TWIN_WEIGHTS:
            delta_w[n], new_m[n], new_v[n] = _adamw(weights[n], grad_w[n], given["m_" + n], given["v_" + n])
    return (loss, grad_x, *[grad_w[n] for n in TWIN_WEIGHTS], *[delta_w[n] for n in TWIN_WEIGHTS],
            *[new_m[n] for n in TWIN_WEIGHTS], *[new_v[n] for n in TWIN_WEIGHTS])
```

```python
import functools

import jax
import jax.numpy as jnp
from jax import lax
from jax.experimental import pallas as pl
from jax.experimental.pallas import tpu as pltpu

F32, BF16 = jnp.float32, jnp.bfloat16
S_ = jax.ShapeDtypeStruct
MESH = pl.DeviceIdType.MESH

RMS_EPS, LN_EPS = 1e-6, 1e-5
POOL_WINDOWS = (2, 4, 8, 16)
ADAM_LR, ADAM_B1, ADAM_B2, ADAM_EPS, ADAM_WD, ADAM_STEP = 0.001, 0.9, 0.999, 1e-08, 0.01, 10

VMEM_LIMIT_BYTES = 60 * 1024 * 1024
N_CHIPS = 4
CHUNK = 16
HALO_CONV = 32
HALO_POOL = 16
HALO_FFN = 8
TILE_D = 512
TILE_F = 256


def _pcall(body, *, name, grid, in_specs, out_specs, out_shape, args, scratch=(), aliases=None, prefetch=0):
    params = pltpu.CompilerParams(dimension_semantics=("arbitrary",) * len(grid), vmem_limit_bytes=VMEM_LIMIT_BYTES)
    if prefetch:
        spec = pltpu.PrefetchScalarGridSpec(num_scalar_prefetch=prefetch, grid=grid, in_specs=in_specs,
                                            out_specs=out_specs, scratch_shapes=list(scratch))
        return pl.pallas_call(body, name=name, grid_spec=spec, out_shape=out_shape,
                              input_output_aliases=aliases or {}, compiler_params=params)(*args)
    return pl.pallas_call(body, name=name, grid=grid, in_specs=in_specs, out_specs=out_specs, out_shape=out_shape,
                          scratch_shapes=list(scratch), input_output_aliases=aliases or {},
                          compiler_params=params)(*args)


def _full(shape):
    n = len(shape)
    return pl.BlockSpec(tuple(shape), lambda *_: (0,) * n, pipeline_mode=pl.Buffered(1))


def _rows(tm, c, col=0):
    return pl.BlockSpec((tm, c), lambda i, *_: (i, col))


def _prev_halo(tm, hb, c):
    return pl.BlockSpec((hb, c), lambda i, *_: (jnp.maximum(i * (tm // hb) - 1, 0), 0))


def _next_halo(tm, hb, c, total_rows):
    last = total_rows // hb - 1
    return pl.BlockSpec((hb, c), lambda i, *_: (jnp.minimum((i + 1) * (tm // hb), last), 0))


def _acc_spec(r, c):
    return pl.BlockSpec((r, c), lambda *_: (0, 0))


def _rms(x, g):
    r = lax.rsqrt(jnp.mean(x * x, axis=-1, keepdims=True) + RMS_EPS)
    xh = x * r
    return xh * g, xh, r


def _rms_bwd(dy, xh, r, g):
    dxh = dy * g
    dx = r * (dxh - xh * jnp.mean(dxh * xh, axis=-1, keepdims=True))
    return dx, jnp.sum(dy * xh, axis=0, keepdims=True)


def _colsum(v):
    return jnp.sum(v, axis=0, keepdims=True)


def _fold8(v):
    out = v[0:8]
    for j in range(1, v.shape[0] // 8):
        out = out + v[8 * j:8 * j + 8]
    return out


def _chunks(n_rows, fn):
    def step(j, carry):
        fn(pl.multiple_of(j * CHUNK, CHUNK))
        return carry
    lax.fori_loop(0, n_rows // CHUNK, step, 0)


def _dot(a, b):
    return jnp.dot(a, b, preferred_element_type=F32)


def _dot_nt(a, b):
    return lax.dot_general(a, b, (((1,), (1,)), ((), ())), preferred_element_type=F32)


def _dot_tn(a, b):
    return lax.dot_general(a, b, (((0,), (0,)), ((), ())), preferred_element_type=F32)


def _sig(v):
    return jax.nn.sigmoid(v)


def _accumulate(ref, val, first):
    @pl.when(first)
    def _():
        ref[...] = val

    @pl.when(jnp.logical_not(first))
    def _():
        ref[...] = ref[...] + val


def _conv_in(x, g, w1g, b1, tm):
    T, D = x.shape
    ns, _, nc = w1g.shape

    def body(x_ref, g_ref, w_ref, b_ref, hb_ref, a_ref):
        hb = _rms(x_ref[...], g_ref[...])[0].astype(BF16)
        hb_ref[...] = hb
        for s in range(ns):
            a_ref[:, s * nc:(s + 1) * nc] = _dot(hb, w_ref[s]) + b_ref[:, s * nc:(s + 1) * nc]

    return _pcall(body, name="conv_in", grid=(T // tm,),
                  in_specs=[_rows(tm, D), _full((1, D)), _full(w1g.shape), _full((1, ns * nc))],
                  out_specs=[_rows(tm, D), _rows(tm, ns * nc)],
                  out_shape=[S_((T, D), BF16), S_((T, ns * nc), F32)], args=(x, g, w1g, b1))


def _conv_mid(a, x, wdw, bdw, lng, lnb, w2, b2, gn, tm, tps, taps):
    T, D = x.shape
    hc = HALO_CONV

    def body(a_ref, ah_ref, x_ref, wdw_ref, bdw_ref, lng_ref, lnb_ref, w2_ref, b2_ref, gn_ref,
             c_ref, sb_ref, x1_ref, hb_ref, uext, win):
        i = pl.program_id(0)
        ah = ah_ref[...]
        uext[0:hc, :] = jnp.where(i % tps == 0, 0.0, ah[:, :D] * _sig(ah[:, D:]))
        av = a_ref[...]
        uext[hc:hc + tm, :] = av[:, :D] * _sig(av[:, D:])

        def conv(r0):
            win[...] = uext[pl.ds(r0, CHUNK + hc), :]
            acc = jnp.broadcast_to(bdw_ref[...], (CHUNK, D))
            for k in range(taps):
                o = hc - (taps - 1) + k
                acc = acc + wdw_ref[k:k + 1, :] * win[o:o + CHUNK, :]
            c_ref[pl.ds(r0, CHUNK), :] = acc
        _chunks(tm, conv)

        cv = c_ref[...]
        xc = cv - jnp.mean(cv, axis=-1, keepdims=True)
        nh = xc * lax.rsqrt(jnp.mean(xc * xc, axis=-1, keepdims=True) + LN_EPS)
        n = nh * lng_ref[...] + lnb_ref[...]
        sb = (n * _sig(n)).astype(BF16)
        sb_ref[...] = sb
        x1 = x_ref[...] + _dot(sb, w2_ref[...]) + b2_ref[...]
        x1_ref[...] = x1
        hb_ref[...] = _rms(x1, gn_ref[...])[0].astype(BF16)

    return _pcall(body, name="conv_mid", grid=(T // tm,),
                  in_specs=[_rows(tm, 2 * D), _prev_halo(tm, hc, 2 * D), _rows(tm, D), _full(wdw.shape), _full((1, D)),
                            _full((1, D)), _full((1, D)), _full((D, D)), _full((1, D)), _full((1, D))],
                  out_specs=[_rows(tm, D), _rows(tm, D), _rows(tm, D), _rows(tm, D)],
                  out_shape=[S_((T, D), F32), S_((T, D), BF16), S_((T, D), F32), S_((T, D), BF16)],
                  scratch=[pltpu.VMEM((tm + hc, D), F32), pltpu.VMEM((CHUNK + hc, D), F32)],
                  args=(a, a, x, wdw, bdw, lng, lnb, w2, b2, gn))


def _mm_up(hb, wg, tm, name):
    T, D = hb.shape
    ns, _, nc = wg.shape

    def body(h_ref, w_ref, o_ref):
        o_ref[...] = _dot(h_ref[...], w_ref[...])

    return _pcall(body, name=name, grid=(ns, T // tm),
                  in_specs=[pl.BlockSpec((tm, D), lambda s, i: (i, 0)), pl.BlockSpec((None, D, nc), lambda s, i: (s, 0, 0))],
                  out_specs=pl.BlockSpec((tm, nc), lambda s, i: (i, s)),
                  out_shape=S_((T, ns * nc), F32), args=(hb, wg))


def _ffn_conv(aext, win, wdw_ref, bdw_ref, dst_ref, tm, ffn_taps):
    hf = HALO_FFN
    f = dst_ref.shape[-1]

    def conv(r0):
        win[...] = aext[pl.ds(r0, CHUNK + hf), :]
        acc = jnp.broadcast_to(bdw_ref[...], (CHUNK, f))
        for k in range(ffn_taps):
            o = hf - (ffn_taps - 1) + k
            acc = acc + wdw_ref[k:k + 1, :] * win[o:o + CHUNK, :]
        dst_ref[pl.ds(r0, CHUNK), :] = acc
    _chunks(tm, conv)


def _ffn_down(up, xin, wdw, bdw, wd, gn, tm, tps, ffn_taps, name, target=None):
    T, D = xin.shape
    F = up.shape[1] // 2
    hf = HALO_FFN
    final = target is not None

    def body(*refs):
        if final:
            (up_ref, uph_ref, x_ref, wdw_ref, bdw_ref, wd_ref, gn_ref, tg_ref,
             gb_ref, dx_ref, loss_ref, dgn_ref, aext, win, ac) = refs
        else:
            (up_ref, uph_ref, x_ref, wdw_ref, bdw_ref, wd_ref, gn_ref,
             gb_ref, xo_ref, h_ref, aext, win, ac) = refs
        i = pl.program_id(0)
        aext[0:hf, :] = jnp.where(i % tps == 0, 0.0, uph_ref[...])
        aext[hf:hf + tm, :] = up_ref[:, 0:F]
        _ffn_conv(aext, win, wdw_ref, bdw_ref, ac, tm, ffn_taps)
        acv = ac[...]
        gb = (acv * _sig(acv) * up_ref[:, F:2 * F]).astype(BF16)
        gb_ref[...] = gb
        xo = x_ref[...] + _dot(gb, wd_ref[...])
        if not final:
            xo_ref[...] = xo
            h_ref[...] = _rms(xo, gn_ref[...])[0]
        else:
            out, xh, r = _rms(xo, gn_ref[...])
            diff = out - tg_ref[...]
            part = 0.5 / D * jnp.sum(jnp.sum(diff * diff, axis=1, keepdims=True), axis=0, keepdims=True)
            dx, dg = _rms_bwd(diff * (1.0 / D), xh, r, gn_ref[...])
            dx_ref[...] = dx
            _accumulate(loss_ref, part, i == 0)
            _accumulate(dgn_ref, dg, i == 0)

    in_specs = [_rows(tm, 2 * F), _prev_halo(tm, hf, F), _rows(tm, D), _full(wdw.shape), _full((1, F)),
                _full((F, D)), _full((1, D))]
    args = [up, up, xin, wdw, bdw, wd, gn]
    if final:
        in_specs.append(_rows(tm, D))
        args.append(target)
        out_specs = [_rows(tm, F), _rows(tm, D), _acc_spec(1, 1), _acc_spec(1, D)]
        out_shape = [S_((T, F), BF16), S_((T, D), F32), S_((1, 1), F32), S_((1, D), F32)]
    else:
        out_specs = [_rows(tm, F), _rows(tm, D), _rows(tm, D)]
        out_shape = [S_((T, F), BF16), S_((T, D), F32), S_((T, D), F32)]
    return _pcall(body, name=name, grid=(T // tm,), in_specs=in_specs, out_specs=out_specs, out_shape=out_shape,
                  scratch=[pltpu.VMEM((tm + hf, F), F32), pltpu.VMEM((CHUNK + hf, F), F32), pltpu.VMEM((tm, F), F32)],
                  args=args)


def _pool_counts(i, tps, tm, r0, w):
    pos = (i % tps) * tm + r0 + lax.broadcasted_iota(jnp.int32, (CHUNK, 1), 0)
    return jnp.minimum(pos + 1, w).astype(F32)


def _pool_fwd(h, x2, pw, pb, ps, gn, tm, tps):
    T, D = h.shape
    ng, cg, _ = pw.shape
    hp = HALO_POOL

    def body(h_ref, hh_ref, x_ref, pw_ref, pb_ref, ps_ref, gn_ref, pl_ref, xo_ref, hb_ref, hext, win):
        i = pl.program_id(0)
        hext[0:hp, :] = jnp.where(i % tps == 0, 0.0, hh_ref[...])
        hext[hp:hp + tm, :] = h_ref[...]

        def pool(r0):
            win[...] = hext[pl.ds(r0, CHUNK + hp), :]
            for g, w in enumerate(POOL_WINDOWS):
                cs = slice(g * cg, (g + 1) * cg)
                acc = win[hp:hp + CHUNK, cs]
                cur = acc
                for j in range(1, w):
                    acc = acc + win[hp - j:hp - j + CHUNK, cs]
                pl_ref[pl.ds(r0, CHUNK), cs] = (acc / _pool_counts(i, tps, tm, r0, w) - cur).astype(BF16)
        _chunks(tm, pool)

        pv = pl_ref[...]
        mixed = jnp.concatenate([_dot(pv[:, g * cg:(g + 1) * cg], pw_ref[g]) for g in range(ng)], axis=1)
        xo = x_ref[...] + ps_ref[...] * (mixed + pb_ref[...])
        xo_ref[...] = xo
        hb_ref[...] = _rms(xo, gn_ref[...])[0].astype(BF16)

    return _pcall(body, name="pool_fwd", grid=(T // tm,),
                  in_specs=[_rows(tm, D), _prev_halo(tm, hp, D), _rows(tm, D), _full(pw.shape), _full((1, D)), _full((1, D)),
                            _full((1, D))],
                  out_specs=[_rows(tm, D), _rows(tm, D), _rows(tm, D)],
                  out_shape=[S_((T, D), BF16), S_((T, D), F32), S_((T, D), BF16)],
                  scratch=[pltpu.VMEM((tm + hp, D), F32), pltpu.VMEM((CHUNK + hp, D), F32)],
                  args=(h, h, x2, pw, pb, ps, gn))


def _bwd_down(dx, up, wdw, bdw, wd, tm, tps, ffn_taps, name):
    T, D = dx.shape
    F = up.shape[1] // 2
    hf = HALO_FFN
    nt = T // tm

    def body(dx_ref, up_ref, uph_ref, wdw_ref, bdw_ref, wd_ref, dac_ref, dgate_ref, wsum_ref, aext, win, ac, wacc):
        i = pl.program_id(0)
        aext[0:hf, :] = jnp.where(i % tps == 0, 0.0, uph_ref[...])
        aext[hf:hf + tm, :] = up_ref[:, 0:F]
        _ffn_conv(aext, win, wdw_ref, bdw_ref, ac, tm, ffn_taps)
        dg = _dot_nt(dx_ref[...].astype(BF16), wd_ref[...])
        acv = ac[...]
        sg = _sig(acv)
        dgate_ref[...] = (dg * acv * sg).astype(BF16)
        dac_ref[...] = dg * up_ref[:, F:2 * F] * (sg * (1.0 + acv * (1.0 - sg)))

        @pl.when(i == 0)
        def _():
            wacc[...] = jnp.zeros_like(wacc)

        def wgrad(r0):
            win[...] = aext[pl.ds(r0, CHUNK + hf), :]
            d = dac_ref[pl.ds(r0, CHUNK), :]
            for k in range(ffn_taps):
                o = hf - (ffn_taps - 1) + k
                wacc[8 * k:8 * k + 8, :] += _fold8(d * win[o:o + CHUNK, :])
            wacc[8 * ffn_taps:8 * ffn_taps + 8, :] += _fold8(d)
        _chunks(tm, wgrad)

        @pl.when(i == nt - 1)
        def _():
            wsum_ref[...] = jnp.zeros_like(wsum_ref)
            for k in range(ffn_taps + 1):
                wsum_ref[k:k + 1, :] = _colsum(wacc[8 * k:8 * k + 8, :])

    return _pcall(body, name=name, grid=(nt,),
                  in_specs=[_rows(tm, D), _rows(tm, 2 * F), _prev_halo(tm, hf, F), _full(wdw.shape), _full((1, F)),
                            _full((F, D))],
                  out_specs=[_rows(tm, F), _rows(tm, F, col=1), _acc_spec(8, F)],
                  out_shape=[S_((T, F), F32), S_((T, 2 * F), BF16), S_((8, F), F32)],
                  scratch=[pltpu.VMEM((tm + hf, F), F32), pltpu.VMEM((CHUNK + hf, F), F32), pltpu.VMEM((tm, F), F32),
                           pltpu.VMEM((8 * (ffn_taps + 1), F), F32)],
                  args=(dx, up, up, wdw, bdw, wd))


def _bwd_up(dac, dup, wg, wdw, xin, gn, dx, tm, tps, ffn_taps, name):
    T, D = xin.shape
    F = dac.shape[1]
    ns, _, nc = wg.shape
    hf = HALO_FFN
    nt = T // tm

    def body(dac_ref, dach_ref, dgate_ref, wg_ref, wdw_ref, x_ref, gn_ref, dx_ref, dact_ref, dxo_ref, dgn_ref, dext, win):
        i = pl.program_id(0)
        dext[0:tm, :] = dac_ref[...]
        dext[tm:tm + hf, :] = jnp.where(i % tps == tps - 1, 0.0, dach_ref[...])

        def convt(r0):
            win[...] = dext[pl.ds(r0, CHUNK + hf), :]
            acc = jnp.zeros((CHUNK, F), F32)
            for k in range(ffn_taps):
                o = ffn_taps - 1 - k
                acc = acc + wdw_ref[k:k + 1, :] * win[o:o + CHUNK, :]
            dact_ref[pl.ds(r0, CHUNK), :] = acc.astype(BF16)
        _chunks(tm, convt)

        dh = jnp.zeros((tm, D), F32)
        for s in range(ns):
            src = dact_ref if s < ns // 2 else dgate_ref
            o = (s % (ns // 2)) * nc
            dh = dh + _dot_nt(src[:, o:o + nc], wg_ref[s])
        _, xh, r = _rms(x_ref[...], gn_ref[...])
        dxn, dg = _rms_bwd(dh, xh, r, gn_ref[...])
        dxo_ref[...] = dx_ref[...] + dxn
        _accumulate(dgn_ref, dg, i == 0)

    return _pcall(body, name=name, grid=(nt,),
                  in_specs=[_rows(tm, F), _next_halo(tm, hf, F, T), _rows(tm, F, col=1), _full(wg.shape), _full(wdw.shape),
                            _rows(tm, D), _full((1, D)), _rows(tm, D)],
                  out_specs=[_rows(tm, F), _rows(tm, D), _acc_spec(1, D)],
                  out_shape=[S_((T, 2 * F), BF16), S_((T, D), F32), S_((1, D), F32)],
                  scratch=[pltpu.VMEM((tm + hf, F), F32), pltpu.VMEM((CHUNK + hf, F), F32)],
                  aliases={2: 0}, args=(dac, dac, dup, wg, wdw, xin, gn, dx))


def _mm_tn(x, dy, bk, bn, tt, shard_major, name):
    T, K = x.shape
    N = dy.shape[1]
    nt = T // tt

    def body(x_ref, dy_ref, o_ref, acc):
        t = pl.program_id(2)
        part = _dot_tn(x_ref[...].astype(BF16), dy_ref[...].astype(BF16))
        _accumulate(acc, part, t == 0)

        @pl.when(t == nt - 1)
        def _():
            o_ref[...] = acc[...].astype(o_ref.dtype)

    if shard_major:
        out_spec = pl.BlockSpec((None, bk, bn), lambda i, j, t: (j, i, 0))
        out_shape = S_((N // bn, K, bn), BF16)
    else:
        out_spec = pl.BlockSpec((bk, bn), lambda i, j, t: (i, j))
        out_shape = S_((K, N), BF16)
    return _pcall(body, name=name, grid=(K // bk, N // bn, nt),
                  in_specs=[pl.BlockSpec((tt, bk), lambda i, j, t: (t, i)), pl.BlockSpec((tt, bn), lambda i, j, t: (t, j))],
                  out_specs=out_spec, out_shape=out_shape, scratch=[pltpu.VMEM((bk, bn), F32)], args=(x, dy))


def _pool_bwd(dx3, plb, x2, pw, pb, ps, gn, tm, tps):
    T, D = x2.shape
    ng, cg, _ = pw.shape
    hp = HALO_POOL
    nt = T // tm

    def body(do_ref, doh_ref, pl_ref, x_ref, pw_ref, pb_ref, ps_ref, gn_ref, dxo_ref, dpw_ref, sm_ref, qext, win, dh_s):
        i = pl.program_id(0)
        do = do_ref[...]
        dm = do * ps_ref[...]
        dmh = jnp.where(i % tps == tps - 1, 0.0, doh_ref[...]) * ps_ref[...]
        pv = pl_ref[...]
        mixed = jnp.concatenate([_dot(pv[:, g * cg:(g + 1) * cg], pw_ref[g]) for g in range(ng)], axis=1)
        dscale = _colsum(do * (mixed + pb_ref[...]))
        dbias = _colsum(dm)
        dmb = dm.astype(BF16)
        dmhb = dmh.astype(BF16)
        for g in range(ng):
            cs = slice(g * cg, (g + 1) * cg)
            _accumulate(dpw_ref.at[g], _dot_tn(pv[:, cs], dmb[:, cs]), i == 0)
            qext[0:tm, cs] = _dot_nt(dmb[:, cs], pw_ref[g])
            qext[tm:tm + hp, cs] = _dot_nt(dmhb[:, cs], pw_ref[g])

        def poolt(r0):
            win[...] = qext[pl.ds(r0, CHUNK + hp), :]
            for g, w in enumerate(POOL_WINDOWS):
                cs = slice(g * cg, (g + 1) * cg)
                cur = win[0:CHUNK, cs]
                acc = cur / _pool_counts(i, tps, tm, r0, w)
                for j in range(1, w):
                    cnt = _pool_counts(i, tps, tm, r0 + j, w)
                    acc = acc + win[j:j + CHUNK, cs] / cnt
                dh_s[pl.ds(r0, CHUNK), cs] = acc - cur
        _chunks(tm, poolt)

        _, xh, r = _rms(x_ref[...], gn_ref[...])
        dxn, dg = _rms_bwd(dh_s[...], xh, r, gn_ref[...])
        dxo_ref[...] = do + dxn
        sm = jnp.concatenate([dscale, dbias, dg, jnp.zeros((5, D), F32)], axis=0)
        _accumulate(sm_ref, sm, i == 0)

    return _pcall(body, name="pool_bwd", grid=(nt,),
                  in_specs=[_rows(tm, D), _next_halo(tm, hp, D, T), _rows(tm, D), _rows(tm, D), _full(pw.shape), _full((1, D)),
                            _full((1, D)), _full((1, D))],
                  out_specs=[_rows(tm, D), pl.BlockSpec((ng, cg, cg), lambda i: (0, 0, 0)), _acc_spec(8, D)],
                  out_shape=[S_((T, D), F32), S_((ng, cg, cg), F32), S_((8, D), F32)],
                  scratch=[pltpu.VMEM((tm + hp, D), F32), pltpu.VMEM((CHUNK + hp, D), F32), pltpu.VMEM((tm, D), F32)],
                  args=(dx3, dx3, plb, x2, pw, pb, ps, gn))


def _conv_bwd1(dx1, c, w2, lng, lnb, tm):
    T, D = dx1.shape

    def body(do_ref, c_ref, w2_ref, lng_ref, lnb_ref, dc_ref, sm_ref):
        i = pl.program_id(0)
        do = do_ref[...]
        ds = _dot_nt(do.astype(BF16), w2_ref[...])
        cv = c_ref[...]
        xc = cv - jnp.mean(cv, axis=-1, keepdims=True)
        rstd = lax.rsqrt(jnp.mean(xc * xc, axis=-1, keepdims=True) + LN_EPS)
        nh = xc * rstd
        n = nh * lng_ref[...] + lnb_ref[...]
        sg = _sig(n)
        dn = ds * (sg * (1.0 + n * (1.0 - sg)))
        dnh = dn * lng_ref[...]
        dc = rstd * (dnh - jnp.mean(dnh, axis=-1, keepdims=True) - nh * jnp.mean(dnh * nh, axis=-1, keepdims=True))
        dc_ref[...] = dc
        sm = jnp.concatenate([_colsum(dn * nh), _colsum(dn), _colsum(do), _colsum(dc), jnp.zeros((4, D), F32)], axis=0)
        _accumulate(sm_ref, sm, i == 0)

    return _pcall(body, name="conv_bwd1", grid=(T // tm,),
                  in_specs=[_rows(tm, D), _rows(tm, D), _full((D, D)), _full((1, D)), _full((1, D))],
                  out_specs=[_rows(tm, D), _acc_spec(8, D)],
                  out_shape=[S_((T, D), F32), S_((8, D), F32)], args=(dx1, c, w2, lng, lnb))


def _conv_bwd2(dc, a, x, w1g, wdw, gn, dx1, tm, tps, taps):
    T, D = x.shape
    ns, _, nc = w1g.shape
    hc = HALO_CONV
    nt = T // tm
    tp = wdw.shape[0]

    def body(dc_ref, dch_ref, a_ref, ah_ref, x_ref, w_ref, wdw_ref, gn_ref, dx1_ref,
             gx_ref, dab_ref, dw_ref, db1_ref, dgn_ref, uext, dext, win, du_s, wacc):
        i = pl.program_id(0)
        ah = ah_ref[...]
        uext[0:hc, :] = jnp.where(i % tps == 0, 0.0, ah[:, :D] * _sig(ah[:, D:]))
        av = a_ref[...]
        sg2 = _sig(av[:, D:])
        uext[hc:hc + tm, :] = av[:, :D] * sg2
        dext[0:tm, :] = dc_ref[...]
        dext[tm:tm + hc, :] = jnp.where(i % tps == tps - 1, 0.0, dch_ref[...])

        @pl.when(i == 0)
        def _():
            wacc[...] = jnp.zeros_like(wacc)

        def wgrad(r0):
            win[...] = uext[pl.ds(r0, CHUNK + hc), :]
            d = dc_ref[pl.ds(r0, CHUNK), :]
            for k in range(taps):
                o = hc - (taps - 1) + k
                wacc[8 * k:8 * k + 8, :] += _fold8(d * win[o:o + CHUNK, :])
        _chunks(tm, wgrad)

        def convt(r0):
            win[...] = dext[pl.ds(r0, CHUNK + hc), :]
            acc = jnp.zeros((CHUNK, D), F32)
            for k in range(taps):
                o = taps - 1 - k
                acc = acc + wdw_ref[k:k + 1, :] * win[o:o + CHUNK, :]
            du_s[pl.ds(r0, CHUNK), :] = acc
        _chunks(tm, convt)

        du = du_s[...]
        da1 = du * sg2
        da2 = du * av[:, :D] * (sg2 * (1.0 - sg2))
        dab_ref[:, 0:D] = da1.astype(BF16)
        dab_ref[:, D:2 * D] = da2.astype(BF16)
        _accumulate(db1_ref, jnp.concatenate([_colsum(da1), _colsum(da2)], axis=1), i == 0)
        dh = jnp.zeros((tm, D), F32)
        for s in range(ns):
            dh = dh + _dot_nt(dab_ref[:, s * nc:(s + 1) * nc], w_ref[s])
        _, xh, r = _rms(x_ref[...], gn_ref[...])
        dxn, dg = _rms_bwd(dh, xh, r, gn_ref[...])
        gx_ref[...] = dx1_ref[...] + dxn
        _accumulate(dgn_ref, dg, i == 0)

        @pl.when(i == nt - 1)
        def _():
            dw_ref[...] = jnp.zeros_like(dw_ref)
            for k in range(taps):
                dw_ref[k:k + 1, :] = _colsum(wacc[8 * k:8 * k + 8, :])

    return _pcall(body, name="conv_bwd2", grid=(nt,),
                  in_specs=[_rows(tm, D), _next_halo(tm, hc, D, T), _rows(tm, 2 * D), _prev_halo(tm, hc, 2 * D), _rows(tm, D),
                            _full(w1g.shape), _full(wdw.shape), _full((1, D)), _rows(tm, D)],
                  out_specs=[_rows(tm, D), _rows(tm, 2 * D), _acc_spec(tp, D), _acc_spec(1, 2 * D), _acc_spec(1, D)],
                  out_shape=[S_((T, D), F32), S_((T, 2 * D), BF16), S_((tp, D), F32), S_((1, 2 * D), F32), S_((1, D), F32)],
                  scratch=[pltpu.VMEM((tm + hc, D), F32), pltpu.VMEM((tm + hc, D), F32), pltpu.VMEM((CHUNK + hc, D), F32),
                           pltpu.VMEM((tm, D), F32), pltpu.VMEM((8 * taps, D), F32)],
                  args=(dc, dc, a, a, x, w1g, wdw, gn, dx1))


def _place():
    x, y, c = lax.axis_index("x"), lax.axis_index("y"), lax.axis_index("c")
    chips = [(1 - x, y), (x, 1 - y), (1 - x, 1 - y)]
    return x, y, c, chips


def _gather(shards):
    n = len(shards)

    def body(*refs):
        srcs, outs = refs[:n], refs[n:2 * n]
        loc, s_ici, r_ici, s_d2d, r_d2d = refs[2 * n:]
        x, y, c, chips = _place()
        q = 2 * x + y
        local = [pltpu.make_async_copy(srcs[i], outs[i].at[q], loc.at[i]) for i in range(n)]
        for cp in local:
            cp.start()

        def half(i, h):
            rh = shards[i].shape[0] // 2
            return pl.ds(pl.multiple_of(h * rh, 8), rh)

        def ici(i, j, src_chip, to):
            src = srcs[i].at[half(i, c)] if src_chip is None else outs[i].at[src_chip, half(i, c)]
            chip = q if src_chip is None else src_chip
            return pltpu.make_async_remote_copy(src_ref=src, dst_ref=outs[i].at[chip, half(i, c)],
                                                send_sem=s_ici.at[i, j], recv_sem=r_ici.at[i, j],
                                                device_id=to, device_id_type=MESH)

        def d2d(i, j, chip, h):
            blk = outs[i].at[chip, half(i, h)]
            return pltpu.make_async_remote_copy(src_ref=blk, dst_ref=blk, send_sem=s_d2d.at[i, j], recv_sem=r_d2d.at[i, j],
                                                device_id=(x, y, 1 - c), device_id_type=MESH)

        first = [ici(i, j, None, (*chip, c)) for i in range(n) for j, chip in enumerate(chips)]
        for cp in first:
            cp.start()
        passed = []
        for i in range(n):
            for j, chip in enumerate(chips):
                qj = 2 * chip[0] + chip[1]
                ici(i, j, qj, (x, y, c)).wait_recv()
                fw = d2d(i, j, qj, c)
                fw.start()
                passed.append(fw)
        for i in range(n):
            for j, chip in enumerate(chips):
                d2d(i, j, 2 * chip[0] + chip[1], 1 - c).wait_recv()
        for cp in first + passed:
            cp.wait_send()
        for cp in local:
            cp.wait()

    any_spec = pl.BlockSpec(memory_space=pl.ANY)
    return pl.pallas_call(
        body, name="gather_weights", in_specs=[any_spec] * n, out_specs=[any_spec] * n,
        out_shape=[S_((N_CHIPS,) + s.shape, s.dtype) for s in shards],
        scratch_shapes=[pltpu.SemaphoreType.DMA((n,))] + [pltpu.SemaphoreType.DMA((n, 3))] * 4,
    )(*shards)


def _pair_exchange(parts):
    n = len(parts)

    def body(*refs):
        srcs, outs = refs[:n], refs[n:2 * n]
        s_sem, r_sem = refs[2 * n:]
        x, y, c, _ = _place()
        cps = []
        for i in range(n):
            for s in range(N_CHIPS):
                cp = pltpu.make_async_remote_copy(src_ref=srcs[i].at[s, 1 - c], dst_ref=outs[i].at[s], send_sem=s_sem.at[i, s],
                                                  recv_sem=r_sem.at[i, s], device_id=(x, y, 1 - c), device_id_type=MESH)
                cp.start()
                cps.append(cp)
        for cp in cps:
            cp.wait()

    any_spec = pl.BlockSpec(memory_space=pl.ANY)
    return pl.pallas_call(
        body, name="grad_pair_exchange", in_specs=[any_spec] * n, out_specs=[any_spec] * n,
        out_shape=[S_((N_CHIPS,) + p.shape[2:], p.dtype) for p in parts],
        scratch_shapes=[pltpu.SemaphoreType.DMA((n, N_CHIPS))] * 2,
    )(*parts)


def _chip_exchange(sums):
    n = len(sums)

    def body(*refs):
        srcs, outs = refs[:n], refs[n:2 * n]
        s_sem, r_sem = refs[2 * n:]
        x, y, c, chips = _place()
        cps = []
        for i in range(n):
            for j, chip in enumerate(chips):
                cp = pltpu.make_async_remote_copy(src_ref=srcs[i].at[2 * chip[0] + chip[1]], dst_ref=outs[i].at[j],
                                                  send_sem=s_sem.at[i, j], recv_sem=r_sem.at[i, j],
                                                  device_id=(*chip, c), device_id_type=MESH)
                cp.start()
                cps.append(cp)
        for cp in cps:
            cp.wait()

    any_spec = pl.BlockSpec(memory_space=pl.ANY)
    return pl.pallas_call(
        body, name="grad_chip_exchange", in_specs=[any_spec] * n, out_specs=[any_spec] * n,
        out_shape=[S_((3,) + s.shape[1:], s.dtype) for s in sums],
        scratch_shapes=[pltpu.SemaphoreType.DMA((n, 3))] * 2,
    )(*sums)


def _share_halves(halves):
    n = len(halves)

    def body(*refs):
        srcs, outs = refs[:n], refs[n:2 * n]
        loc, s_sem, r_sem = refs[2 * n:]
        x, y, c, _ = _place()
        cps = []
        for i in range(n):
            lc = pltpu.make_async_copy(srcs[i], outs[i].at[c], loc.at[i])
            lc.start()
            cp = pltpu.make_async_remote_copy(src_ref=srcs[i], dst_ref=outs[i].at[c], send_sem=s_sem.at[i],
                                              recv_sem=r_sem.at[i], device_id=(x, y, 1 - c), device_id_type=MESH)
            cp.start()
            cps += [lc, cp]
        for cp in cps:
            cp.wait()

    any_spec = pl.BlockSpec(memory_space=pl.ANY)
    return pl.pallas_call(
        body, name="grad_share_halves", in_specs=[any_spec] * n, out_specs=[any_spec] * n,
        out_shape=[S_((2,) + h.shape, h.dtype) for h in halves],
        scratch_shapes=[pltpu.SemaphoreType.DMA((n,))] * 3,
    )(*halves)


def _small_allreduce(v):
    rows = v.shape[0]

    def body(v_ref, o_ref, sib, cs, got, s_sem, r_sem):
        x, y, c, chips = _place()
        q = 2 * x + y
        to_sib = pltpu.make_async_remote_copy(src_ref=v_ref, dst_ref=sib, send_sem=s_sem.at[3], recv_sem=r_sem.at[3],
                                              device_id=(x, y, 1 - c), device_id_type=MESH)
        to_sib.start()
        to_sib.wait()
        mine, other = v_ref[...], sib[...]
        cs[...] = jnp.where(c == 0, mine, other) + jnp.where(c == 0, other, mine)
        cps = []
        for j, chip in enumerate(chips):
            cp = pltpu.make_async_remote_copy(src_ref=cs, dst_ref=got.at[j], send_sem=s_sem.at[j], recv_sem=r_sem.at[j],
                                              device_id=(*chip, c), device_id_type=MESH)
            cp.start()
            cps.append(cp)
        for cp in cps:
            cp.wait()
        flips = [2, 1, 3]
        total = None
        for k in range(N_CHIPS):
            d = q ^ k
            term = jnp.where(d == 0, cs[...], 0.0)
            for j in range(3):
                term = jnp.where(d == flips[j], got[j], term)
            total = term if total is None else total + term
        o_ref[...] = total

    vm = pl.BlockSpec(memory_space=pltpu.VMEM)
    return pl.pallas_call(
        body, name="small_allreduce", in_specs=[vm], out_specs=vm, out_shape=S_(v.shape, F32),
        scratch_shapes=[pltpu.VMEM((rows, 128), F32), pltpu.VMEM((rows, 128), F32), pltpu.VMEM((3, rows, 128), F32),
                        pltpu.SemaphoreType.DMA((4,)), pltpu.SemaphoreType.DMA((4,))],
    )(v)


def _row_block(r, c, target_bytes=1 << 20):
    br = r
    while br % 32 == 0 and br * c * 4 > target_bytes:
        br //= 2
    return br


def _chip_sum(part, sib, ids, name):
    _, _, r, c = part.shape
    br = _row_block(r, c)

    def body(ids_ref, p_ref, s_ref, o_ref):
        o_ref[...] = (p_ref[...].astype(F32) + s_ref[...].astype(F32)).astype(BF16)

    return _pcall(body, name=name, grid=(N_CHIPS, r // br), prefetch=1,
                  in_specs=[pl.BlockSpec((None, None, br, c), lambda s, i, ids: (s, ids[1], i, 0)),
                            pl.BlockSpec((None, br, c), lambda s, i, ids: (s, i, 0))],
                  out_specs=pl.BlockSpec((None, br, c), lambda s, i, ids: (s, i, 0)),
                  out_shape=S_((N_CHIPS, r, c), BF16), args=(ids, part, sib))


def _final_sum(part, sib, got, ids, name):
    _, r, c = got.shape
    br = _row_block(r, c)

    def body(ids_ref, p_ref, s_ref, got_ref, o_ref):
        v = p_ref[...].astype(F32) + s_ref[...].astype(F32)
        for j in range(3):
            v = v + got_ref[j].astype(F32)
        o_ref[...] = v

    return _pcall(body, name=name, grid=(r // br,), prefetch=1,
                  in_specs=[pl.BlockSpec((None, None, br, c), lambda i, ids: (ids[0], ids[1], i, 0)),
                            pl.BlockSpec((None, br, c), lambda i, ids: (ids[0], i, 0)),
                            pl.BlockSpec((3, br, c), lambda i, ids: (0, i, 0))],
                  out_specs=pl.BlockSpec((br, c), lambda i, ids: (i, 0)), out_shape=S_((r, c), F32),
                  args=(ids, part, sib, got))


def _adamw(w, g, m, v, name):
    r, c = w.shape
    br = _row_block(r, c, 1 << 19)
    c1 = 1.0 / (1.0 - ADAM_B1 ** ADAM_STEP)
    c2 = 1.0 / (1.0 - ADAM_B2 ** ADAM_STEP)

    def body(w_ref, g_ref, m_ref, v_ref, d_ref, mo_ref, vo_ref):
        gv = g_ref[...]
        mn = ADAM_B1 * m_ref[...] + (1.0 - ADAM_B1) * gv
        vn = ADAM_B2 * v_ref[...] + (1.0 - ADAM_B2) * (gv * gv)
        mo_ref[...] = mn
        vo_ref[...] = vn
        d_ref[...] = -ADAM_LR * ((mn * c1) / (jnp.sqrt(vn * c2) + ADAM_EPS) + ADAM_WD * w_ref[...])

    blk = pl.BlockSpec((br, c), lambda i: (i, 0))
    return _pcall(body, name=name, grid=(r // br,), in_specs=[blk] * 4, out_specs=[blk] * 3,
                  out_shape=[S_((r, c), F32)] * 3, args=(w, g, m, v))


def _pack(pieces, rows):
    flat = jnp.concatenate([p.reshape(-1).astype(F32) for p in pieces])
    return jnp.pad(flat, (0, rows * 128 - flat.shape[0])).reshape(rows, 128)


def _unpack(packed, shapes):
    flat = packed.reshape(-1)
    out, o = [], 0
    for s in shapes:
        n = 1
        for d in s:
            n *= d
        out.append(flat[o:o + n].reshape(s))
        o += n
    return out


def _rows_for(pieces_or_shapes):
    n = 0
    for p in pieces_or_shapes:
        k = 1
        for d in (p if isinstance(p, tuple) else p.shape):
            k *= d
        n += k
    return -(-n // 1024) * 8


def kernel(x, norm_mix, norm_ffn, conv_w_pw1, conv_b_pw1, conv_w_dw, conv_b_dw, conv_ln_g, conv_ln_b, conv_w_pw2, conv_b_pw2, pool_w, pool_b, pool_scale, ffn_w_up, ffn_w_dw, ffn_b_dw, ffn_w_down, final_norm, loss_target, m_norm_mix, m_norm_ffn, m_conv_w_pw1, m_conv_b_pw1, m_conv_w_dw, m_conv_b_dw, m_conv_ln_g, m_conv_ln_b, m_conv_w_pw2, m_conv_b_pw2, m_pool_w, m_pool_b, m_pool_scale, m_ffn_w_up, m_ffn_w_dw, m_ffn_b_dw, m_ffn_w_down, m_final_norm, v_norm_mix, v_norm_ffn, v_conv_w_pw1, v_conv_b_pw1, v_conv_w_dw, v_conv_b_dw, v_conv_ln_g, v_conv_ln_b, v_conv_w_pw2, v_conv_b_pw2, v_pool_w, v_pool_b, v_pool_scale, v_ffn_w_up, v_ffn_w_dw, v_ffn_b_dw, v_ffn_w_down, v_final_norm):
    nb, seq, D = x.shape
    T = nb * seq
    F = ffn_w_down.shape[1] * N_CHIPS
    taps = conv_w_dw.shape[1]
    ffn_taps = ffn_w_dw.shape[1]
    ng = pool_w.shape[1]
    cg = pool_w.shape[3]
    dsh = D // N_CHIPS
    fsh = F // N_CHIPS
    assert taps - 1 <= HALO_CONV and ffn_taps - 1 <= HALO_FFN and max(POOL_WINDOWS) <= HALO_POOL
    tm = min(TILE_D, seq)
    tf = min(TILE_F, seq)
    assert seq % tm == 0 and seq % tf == 0 and tm % HALO_CONV == 0 and tf % CHUNK == 0
    tps, tpf = seq // tm, seq // tf

    xi, yi, ci = lax.axis_index("x"), lax.axis_index("y"), lax.axis_index("c")
    qi = 2 * xi + yi
    ids = jnp.stack([qi, ci]).astype(jnp.int32)
    x2d = x.reshape(T, D)
    tg2d = loss_target.reshape(T, D)

    small_sharded = [conv_w_dw[0], ffn_w_dw, pool_b, pool_scale]
    srows = 2 * _rows_for(small_sharded)
    gathered = _gather([
        conv_w_pw1[0].astype(BF16), conv_w_pw2[0].astype(BF16), pool_w[0].reshape(ng * (cg // N_CHIPS), cg).astype(BF16),
        ffn_w_up[0].astype(BF16), ffn_w_up[1].astype(BF16), ffn_w_down[0].astype(BF16), ffn_w_down[1].astype(BF16),
        _pack(small_sharded, srows)])
    w1g, w2g, pwg, wup0, wup1, wd0, wd1, smallg = gathered
    w2f = w2g.reshape(D, D)
    pwf = pwg.reshape(N_CHIPS, ng, cg // N_CHIPS, cg).transpose(1, 0, 2, 3).reshape(ng, cg, cg)
    wup = [wup0, wup1]
    wdn = [wd0.reshape(F, D), wd1.reshape(F, D)]
    sm_parts = [_unpack(smallg[s], [(taps, dsh), (2, ffn_taps, fsh), (1, dsh), (1, dsh)]) for s in range(N_CHIPS)]
    wdw_f = jnp.concatenate([p[0] for p in sm_parts], axis=1)
    wdw_f = jnp.pad(wdw_f, ((0, HALO_CONV - taps), (0, 0)))
    fdw_f = jnp.concatenate([p[1] for p in sm_parts], axis=2)
    fdw_f = jnp.pad(fdw_f, ((0, 0), (0, 8 - ffn_taps), (0, 0)))
    pb_f = jnp.concatenate([p[2] for p in sm_parts], axis=1)
    ps_f = jnp.concatenate([p[3] for p in sm_parts], axis=1)

    hb0, a = _conv_in(x2d, norm_mix[0:1], w1g, conv_b_pw1, tm)
    c, sb, x1, hb1 = _conv_mid(a, x2d, wdw_f, conv_b_dw, conv_ln_g, conv_ln_b, w2f, conv_b_pw2, norm_ffn[0:1], tm, tps, taps)
    up0 = _mm_up(hb1, wup[0], tm, "ffn_up0")
    gb0, x2, h2 = _ffn_down(up0, x1, fdw_f[0], ffn_b_dw[0:1], wdn[0], norm_mix[1:2], tf, tpf, ffn_taps, "ffn_down0")
    plb, x3, hb3 = _pool_fwd(h2, x2, pwf, pb_f, ps_f, norm_ffn[1:2], tm, tps)
    up1 = _mm_up(hb3, wup[1], tm, "ffn_up1")
    gb1, dx4, loss_part, d_final = _ffn_down(up1, x3, fdw_f[1], ffn_b_dw[1:2], wdn[1], final_norm.reshape(1, D), tf, tpf,
                                             ffn_taps, "ffn_down1", target=tg2d)

    dac1, dup1, fw1 = _bwd_down(dx4, up1, fdw_f[1], ffn_b_dw[1:2], wdn[1], tf, tpf, ffn_taps, "bwd_down1")
    p_down1 = _mm_tn(gb1, dx4, F // 2, D, tm, False, "dw_down1")
    dup1, dx3, dnf1 = _bwd_up(dac1, dup1, wup[1], fdw_f[1], x3, norm_ffn[1:2], dx4, tf, tpf, ffn_taps, "bwd_up1")
    p_up1 = _mm_tn(hb3, dup1, D, 2 * fsh, tm, True, "dw_up1")
    dx2, dpw, pool_sm = _pool_bwd(dx3, plb, x2, pwf, pb_f, ps_f, norm_mix[1:2], tm, tps)
    dac0, dup0, fw0 = _bwd_down(dx2, up0, fdw_f[0], ffn_b_dw[0:1], wdn[0], tf, tpf, ffn_taps, "bwd_down0")
    p_down0 = _mm_tn(gb0, dx2, F // 2, D, tm, False, "dw_down0")
    dup0, dx1, dnf0 = _bwd_up(dac0, dup0, wup[0], fdw_f[0], x1, norm_ffn[0:1], dx2, tf, tpf, ffn_taps, "bwd_up0")
    p_up0 = _mm_tn(hb1, dup0, D, 2 * fsh, tm, True, "dw_up0")
    dc, conv_sm = _conv_bwd1(dx1, c, w2f, conv_ln_g, conv_ln_b, tm)
    p_pw2 = _mm_tn(sb, dx1, D, D, tm, False, "dw_pw2")
    grad_x, dab, dwdw, db1, dnm0 = _conv_bwd2(dc, a, x2d, w1g, wdw_f, norm_mix[0:1], dx1, tm, tps, taps)
    p_pw1 = _mm_tn(hb0, dab, D, 2 * dsh, tm, True, "dw_pw1")

    csh = cg // N_CHIPS
    p_pool = dpw.reshape(2, ng // 2, N_CHIPS, csh, cg).transpose(2, 0, 1, 3, 4).reshape(N_CHIPS, 2, (ng // 2) * csh, cg)
    parts = [p_pw1.reshape(N_CHIPS, 2, D // 2, 2 * dsh), p_pw2.reshape(N_CHIPS, 2, dsh // 2, D), p_pool.astype(BF16),
             p_up0.reshape(N_CHIPS, 2, D // 2, 2 * fsh), p_up1.reshape(N_CHIPS, 2, D // 2, 2 * fsh),
             p_down0.reshape(N_CHIPS, 2, fsh // 2, D), p_down1.reshape(N_CHIPS, 2, fsh // 2, D)]
    names = ["pw1", "pw2", "pool", "up0", "up1", "down0", "down1"]
    from_sib = _pair_exchange(parts)
    sums = [_chip_sum(p, s, ids, "chip_sum_" + nm) for p, s, nm in zip(parts, from_sib, names)]
    got = _chip_exchange(sums)
    halves = [_final_sum(p, s, g, ids, "final_sum_" + nm) for p, s, g, nm in zip(parts, from_sib, got, names)]
    both = _share_halves(halves)
    g_pw1, g_pw2, g_pool, g_up0, g_up1, g_down0, g_down1 = [b.reshape(2 * b.shape[1], b.shape[2]) for b in both]

    d_norm_mix = jnp.concatenate([dnm0, pool_sm[2:3]], axis=0)
    d_norm_ffn = jnp.concatenate([dnf0, dnf1], axis=0)
    d_ffn_b_dw = jnp.stack([fw0[ffn_taps], fw1[ffn_taps]])
    d_ffn_w_dw = jnp.stack([fw0[:ffn_taps], fw1[:ffn_taps]])
    small = [d_norm_mix, d_norm_ffn, db1, conv_sm[3:4], conv_sm[0:1], conv_sm[1:2], conv_sm[2:3], d_ffn_b_dw, d_final,
             dwdw[:taps], d_ffn_w_dw, pool_sm[1:2], pool_sm[0:1]]
    small_shapes = [tuple(s.shape) for s in small]
    red = _unpack(_small_allreduce(_pack(small, _rows_for(small))), small_shapes)
    (g_norm_mix, g_norm_ffn, g_b_pw1, g_b_dw, g_ln_g, g_ln_b, g_b_pw2, g_ffn_b_dw, g_final, g_wdw_full, g_fdw_full, g_pb_full,
     g_ps_full) = red
    g_wdw = lax.dynamic_slice_in_dim(g_wdw_full, qi * dsh, dsh, axis=1)[None]
    g_fdw = lax.dynamic_slice_in_dim(g_fdw_full, qi * fsh, fsh, axis=2)
    g_pb = lax.dynamic_slice_in_dim(g_pb_full, qi * dsh, dsh, axis=1)
    g_ps = lax.dynamic_slice_in_dim(g_ps_full, qi * dsh, dsh, axis=1)
    g_final = g_final.reshape(D)

    def big(w, g, m, v, nm):
        d, mn, vn = _adamw(w.reshape(g.shape), g, m.reshape(g.shape), v.reshape(g.shape), "adamw_" + nm)
        return g.reshape(w.shape), d.reshape(w.shape), mn.reshape(w.shape), vn.reshape(w.shape)

    def layers(w, g0, g1, m, v, nm):
        r0 = big(w[0], g0, m[0], v[0], nm + "0")
        r1 = big(w[1], g1, m[1], v[1], nm + "1")
        return tuple(jnp.stack([u, t]) for u, t in zip(r0, r1))

    o_pw1 = big(conv_w_pw1, g_pw1, m_conv_w_pw1, v_conv_w_pw1, "pw1")
    o_pw2 = big(conv_w_pw2, g_pw2, m_conv_w_pw2, v_conv_w_pw2, "pw2")
    o_pool = big(pool_w, g_pool, m_pool_w, v_pool_w, "pool")
    o_up = layers(ffn_w_up, g_up0, g_up1, m_ffn_w_up, v_ffn_w_up, "up")
    o_down = layers(ffn_w_down, g_down0, g_down1, m_ffn_w_down, v_ffn_w_down, "down")

    sw = [norm_mix, norm_ffn, conv_b_pw1, conv_w_dw, conv_b_dw, conv_ln_g, conv_ln_b, conv_b_pw2, pool_b, pool_scale,
          ffn_w_dw, ffn_b_dw, final_norm]
    sg = [g_norm_mix, g_norm_ffn, g_b_pw1, g_wdw, g_b_dw, g_ln_g, g_ln_b, g_b_pw2, g_pb, g_ps, g_fdw, g_ffn_b_dw, g_final]
    smm = [m_norm_mix, m_norm_ffn, m_conv_b_pw1, m_conv_w_dw, m_conv_b_dw, m_conv_ln_g, m_conv_ln_b, m_conv_b_pw2, m_pool_b,
           m_pool_scale, m_ffn_w_dw, m_ffn_b_dw, m_final_norm]
    svv = [v_norm_mix, v_norm_ffn, v_conv_b_pw1, v_conv_w_dw, v_conv_b_dw, v_conv_ln_g, v_conv_ln_b, v_conv_b_pw2, v_pool_b,
           v_pool_scale, v_ffn_w_dw, v_ffn_b_dw, v_final_norm]
    sshapes = [tuple(w.shape) for w in sw]
    rws = _rows_for(sw)
    sd, sm_new, sv_new = _adamw(_pack(sw, rws), _pack(sg, rws), _pack(smm, rws), _pack(svv, rws), "adamw_small")
    sd, sm_new, sv_new = _unpack(sd, sshapes), _unpack(sm_new, sshapes), _unpack(sv_new, sshapes)
    sg = [g.reshape(s) for g, s in zip(sg, sshapes)]
    small_out = {n: (sg[k], sd[k], sm_new[k], sv_new[k]) for k, n in enumerate(
        ["norm_mix", "norm_ffn", "b_pw1", "w_dw", "b_dw", "ln_g", "ln_b", "b_pw2", "pool_b", "pool_scale", "ffn_w_dw",
         "ffn_b_dw", "final_norm"])}

    order = [small_out["norm_mix"], small_out["norm_ffn"], o_pw1, small_out["b_pw1"], small_out["w_dw"], small_out["b_dw"],
             small_out["ln_g"], small_out["ln_b"], o_pw2, small_out["b_pw2"], o_pool, small_out["pool_b"],
             small_out["pool_scale"], o_up, small_out["ffn_w_dw"], small_out["ffn_b_dw"], o_down, small_out["final_norm"]]
    loss = lax.psum(loss_part[0, 0], ("x", "y", "c"))
    return (loss, grad_x.reshape(nb, seq, D), *[o[0] for o in order], *[o[1] for o in order], *[o[2] for o in order],
            *[o[3] for o in order])
```

```python
import functools

import jax
import jax.numpy as jnp
from jax import lax
from jax.experimental import pallas as pl
from jax.experimental.pallas import tpu as pltpu

F32, BF16 = jnp.float32, jnp.bfloat16
S_ = jax.ShapeDtypeStruct
MESH = pl.DeviceIdType.MESH

RMS_EPS, LN_EPS = 1e-6, 1e-5
POOL_WINDOWS = (2, 4, 8, 16)
ADAM_LR, ADAM_B1, ADAM_B2, ADAM_EPS, ADAM_WD, ADAM_STEP = 0.001, 0.9, 0.999, 1e-08, 0.01, 10

VMEM_LIMIT_BYTES = 60 * 1024 * 1024
N_CHIPS = 4
CHUNK = 16
HALO_CONV = 32
HALO_POOL = 16
HALO_FFN = 8
TILE_D = 512
TILE_F = 256


def _pcall(body, *, name, grid, in_specs, out_specs, out_shape, args, scratch=(), aliases=None, prefetch=0):
    params = pltpu.CompilerParams(dimension_semantics=("arbitrary",) * len(grid), vmem_limit_bytes=VMEM_LIMIT_BYTES)
    if prefetch:
        spec = pltpu.PrefetchScalarGridSpec(num_scalar_prefetch=prefetch, grid=grid, in_specs=in_specs,
                                            out_specs=out_specs, scratch_shapes=list(scratch))
        return pl.pallas_call(body, name=name, grid_spec=spec, out_shape=out_shape,
                              input_output_aliases=aliases or {}, compiler_params=params)(*args)
    return pl.pallas_call(body, name=name, grid=grid, in_specs=in_specs, out_specs=out_specs, out_shape=out_shape,
                          scratch_shapes=list(scratch), input_output_aliases=aliases or {},
                          compiler_params=params)(*args)


def _full(shape):
    n = len(shape)
    return pl.BlockSpec(tuple(shape), lambda *_: (0,) * n, pipeline_mode=pl.Buffered(1))


def _rows(tm, c, col=0):
    return pl.BlockSpec((tm, c), lambda i, *_: (i, col))


def _prev_halo(tm, hb, c):
    return pl.BlockSpec((hb, c), lambda i, *_: (jnp.maximum(i * (tm // hb) - 1, 0), 0))


def _next_halo(tm, hb, c, total_rows):
    last = total_rows // hb - 1
    return pl.BlockSpec((hb, c), lambda i, *_: (jnp.minimum((i + 1) * (tm // hb), last), 0))


def _acc_spec(r, c):
    return pl.BlockSpec((r, c), lambda *_: (0, 0))


def _rms(x, g):
    r = lax.rsqrt(jnp.mean(x * x, axis=-1, keepdims=True) + RMS_EPS)
    xh = x * r
    return xh * g, xh, r


def _rms_bwd(dy, xh, r, g):
    dxh = dy * g
    dx = r * (dxh - xh * jnp.mean(dxh * xh, axis=-1, keepdims=True))
    return dx, jnp.sum(dy * xh, axis=0, keepdims=True)


def _colsum(v):
    return jnp.sum(v, axis=0, keepdims=True)


def _fold8(v):
    out = v[0:8]
    for j in range(1, v.shape[0] // 8):
        out = out + v[8 * j:8 * j + 8]
    return out


def _chunks(n_rows, fn):
    def step(j, carry):
        fn(pl.multiple_of(j * CHUNK, CHUNK))
        return carry
    lax.fori_loop(0, n_rows // CHUNK, step, 0)


def _dot(a, b):
    return jnp.dot(a, b, preferred_element_type=F32)


def _dot_nt(a, b):
    return lax.dot_general(a, b, (((1,), (1,)), ((), ())), preferred_element_type=F32)


def _dot_tn(a, b):
    return lax.dot_general(a, b, (((0,), (0,)), ((), ())), preferred_element_type=F32)


def _sig(v):
    return jax.nn.sigmoid(v)


def _accumulate(ref, val, first):
    @pl.when(first)
    def _():
        ref[...] = val

    @pl.when(jnp.logical_not(first))
    def _():
        ref[...] = ref[...] + val


def _conv_in(x, g, w1g, b1, tm):
    T, D = x.shape
    ns, _, nc = w1g.shape

    def body(x_ref, g_ref, w_ref, b_ref, hb_ref, a_ref):
        hb = _rms(x_ref[...], g_ref[...])[0].astype(BF16)
        hb_ref[...] = hb
        for s in range(ns):
            a_ref[:, s * nc:(s + 1) * nc] = _dot(hb, w_ref[s]) + b_ref[:, s * nc:(s + 1) * nc]

    return _pcall(body, name="conv_in", grid=(T // tm,),
                  in_specs=[_rows(tm, D), _full((1, D)), _full(w1g.shape), _full((1, ns * nc))],
                  out_specs=[_rows(tm, D), _rows(tm, ns * nc)],
                  out_shape=[S_((T, D), BF16), S_((T, ns * nc), F32)], args=(x, g, w1g, b1))


def _conv_mid(a, x, wdw, bdw, lng, lnb, w2, b2, gn, tm, tps, taps):
    T, D = x.shape
    hc = HALO_CONV

    def body(a_ref, ah_ref, x_ref, wdw_ref, bdw_ref, lng_ref, lnb_ref, w2_ref, b2_ref, gn_ref,
             c_ref, sb_ref, x1_ref, hb_ref, uext, win):
        i = pl.program_id(0)
        ah = ah_ref[...]
        uext[0:hc, :] = jnp.where(i % tps == 0, 0.0, ah[:, :D] * _sig(ah[:, D:]))
        av = a_ref[...]
        uext[hc:hc + tm, :] = av[:, :D] * _sig(av[:, D:])

        def conv(r0):
            win[...] = uext[pl.ds(r0, CHUNK + hc), :]
            acc = jnp.broadcast_to(bdw_ref[...], (CHUNK, D))
            for k in range(taps):
                o = hc - (taps - 1) + k
                acc = acc + wdw_ref[k:k + 1, :] * win[o:o + CHUNK, :]
            c_ref[pl.ds(r0, CHUNK), :] = acc
        _chunks(tm, conv)

        cv = c_ref[...]
        xc = cv - jnp.mean(cv, axis=-1, keepdims=True)
        nh = xc * lax.rsqrt(jnp.mean(xc * xc, axis=-1, keepdims=True) + LN_EPS)
        n = nh * lng_ref[...] + lnb_ref[...]
        sb = (n * _sig(n)).astype(BF16)
        sb_ref[...] = sb
        x1 = x_ref[...] + _dot(sb, w2_ref[...]) + b2_ref[...]
        x1_ref[...] = x1
        hb_ref[...] = _rms(x1, gn_ref[...])[0].astype(BF16)

    return _pcall(body, name="conv_mid", grid=(T // tm,),
                  in_specs=[_rows(tm, 2 * D), _prev_halo(tm, hc, 2 * D), _rows(tm, D), _full(wdw.shape), _full((1, D)),
                            _full((1, D)), _full((1, D)), _full((D, D)), _full((1, D)), _full((1, D))],
                  out_specs=[_rows(tm, D), _rows(tm, D), _rows(tm, D), _rows(tm, D)],
                  out_shape=[S_((T, D), F32), S_((T, D), BF16), S_((T, D), F32), S_((T, D), BF16)],
                  scratch=[pltpu.VMEM((tm + hc, D), F32), pltpu.VMEM((CHUNK + hc, D), F32)],
                  args=(a, a, x, wdw, bdw, lng, lnb, w2, b2, gn))


def _mm_up(hb, wg, tm, name):
    T, D = hb.shape
    ns, _, nc = wg.shape

    def body(h_ref, w_ref, o_ref):
        o_ref[...] = _dot(h_ref[...], w_ref[...])

    return _pcall(body, name=name, grid=(ns, T // tm),
                  in_specs=[pl.BlockSpec((tm, D), lambda s, i: (i, 0)), pl.BlockSpec((None, D, nc), lambda s, i: (s, 0, 0))],
                  out_specs=pl.BlockSpec((tm, nc), lambda s, i: (i, s)),
                  out_shape=S_((T, ns * nc), F32), args=(hb, wg))


def _ffn_conv(aext, win, wdw_ref, bdw_ref, dst_ref, tm, ffn_taps):
    hf = HALO_FFN
    f = dst_ref.shape[-1]

    def conv(r0):
        win[...] = aext[pl.ds(r0, CHUNK + hf), :]
        acc = jnp.broadcast_to(bdw_ref[...], (CHUNK, f))
        for k in range(ffn_taps):
            o = hf - (ffn_taps - 1) + k
            acc = acc + wdw_ref[k:k + 1, :] * win[o:o + CHUNK, :]
        dst_ref[pl.ds(r0, CHUNK), :] = acc
    _chunks(tm, conv)


def _ffn_down(up, xin, wdw, bdw, wd, gn, tm, tps, ffn_taps, name, target=None):
    T, D = xin.shape
    F = up.shape[1] // 2
    hf = HALO_FFN
    final = target is not None

    def body(*refs):
        if final:
            (up_ref, uph_ref, x_ref, wdw_ref, bdw_ref, wd_ref, gn_ref, tg_ref,
             gb_ref, dx_ref, loss_ref, dgn_ref, aext, win, ac) = refs
        else:
            (up_ref, uph_ref, x_ref, wdw_ref, bdw_ref, wd_ref, gn_ref,
             gb_ref, xo_ref, h_ref, aext, win, ac) = refs
        i = pl.program_id(0)
        aext[0:hf, :] = jnp.where(i % tps == 0, 0.0, uph_ref[...])
        aext[hf:hf + tm, :] = up_ref[:, 0:F]
        _ffn_conv(aext, win, wdw_ref, bdw_ref, ac, tm, ffn_taps)
        acv = ac[...]
        gb = (acv * _sig(acv) * up_ref[:, F:2 * F]).astype(BF16)
        gb_ref[...] = gb
        xo = x_ref[...] + _dot(gb, wd_ref[...])
        if not final:
            xo_ref[...] = xo
            h_ref[...] = _rms(xo, gn_ref[...])[0]
        else:
            out, xh, r = _rms(xo, gn_ref[...])
            diff = out - tg_ref[...]
            part = 0.5 / D * jnp.sum(jnp.sum(diff * diff, axis=1, keepdims=True), axis=0, keepdims=True)
            dx, dg = _rms_bwd(diff * (1.0 / D), xh, r, gn_ref[...])
            dx_ref[...] = dx
            _accumulate(loss_ref, part, i == 0)
            _accumulate(dgn_ref, dg, i == 0)

    in_specs = [_rows(tm, 2 * F), _prev_halo(tm, hf, F), _rows(tm, D), _full(wdw.shape), _full((1, F)),
                _full((F, D)), _full((1, D))]
    args = [up, up, xin, wdw, bdw, wd, gn]
    if final:
        in_specs.append(_rows(tm, D))
        args.append(target)
        out_specs = [_rows(tm, F), _rows(tm, D), _acc_spec(1, 1), _acc_spec(1, D)]
        out_shape = [S_((T, F), BF16), S_((T, D), F32), S_((1, 1), F32), S_((1, D), F32)]
    else:
        out_specs = [_rows(tm, F), _rows(tm, D), _rows(tm, D)]
        out_shape = [S_((T, F), BF16), S_((T, D), F32), S_((T, D), F32)]
    return _pcall(body, name=name, grid=(T // tm,), in_specs=in_specs, out_specs=out_specs, out_shape=out_shape,
                  scratch=[pltpu.VMEM((tm + hf, F), F32), pltpu.VMEM((CHUNK + hf, F), F32), pltpu.VMEM((tm, F), F32)],
                  args=args)


def _pool_counts(i, tps, tm, r0, w):
    pos = (i % tps) * tm + r0 + lax.broadcasted_iota(jnp.int32, (CHUNK, 1), 0)
    return jnp.minimum(pos + 1, w).astype(F32)


def _pool_fwd(h, x2, pw, pb, ps, gn, tm, tps):
    T, D = h.shape
    ng, cg, _ = pw.shape
    hp = HALO_POOL

    def body(h_ref, hh_ref, x_ref, pw_ref, pb_ref, ps_ref, gn_ref, pl_ref, xo_ref, hb_ref, hext, win):
        i = pl.program_id(0)
        hext[0:hp, :] = jnp.where(i % tps == 0, 0.0, hh_ref[...])
        hext[hp:hp + tm, :] = h_ref[...]

        def pool(r0):
            win[...] = hext[pl.ds(r0, CHUNK + hp), :]
            for g, w in enumerate(POOL_WINDOWS):
                cs = slice(g * cg, (g + 1) * cg)
                acc = win[hp:hp + CHUNK, cs]
                cur = acc
                for j in range(1, w):
                    acc = acc + win[hp - j:hp - j + CHUNK, cs]
                pl_ref[pl.ds(r0, CHUNK), cs] = (acc / _pool_counts(i, tps, tm, r0, w) - cur).astype(BF16)
        _chunks(tm, pool)

        pv = pl_ref[...]
        mixed = jnp.concatenate([_dot(pv[:, g * cg:(g + 1) * cg], pw_ref[g]) for g in range(ng)], axis=1)
        xo = x_ref[...] + ps_ref[...] * (mixed + pb_ref[...])
        xo_ref[...] = xo
        hb_ref[...] = _rms(xo, gn_ref[...])[0].astype(BF16)

    return _pcall(body, name="pool_fwd", grid=(T // tm,),
                  in_specs=[_rows(tm, D), _prev_halo(tm, hp, D), _rows(tm, D), _full(pw.shape), _full((1, D)), _full((1, D)),
                            _full((1, D))],
                  out_specs=[_rows(tm, D), _rows(tm, D), _rows(tm, D)],
                  out_shape=[S_((T, D), BF16), S_((T, D), F32), S_((T, D), BF16)],
                  scratch=[pltpu.VMEM((tm + hp, D), F32), pltpu.VMEM((CHUNK + hp, D), F32)],
                  args=(h, h, x2, pw, pb, ps, gn))


def _bwd_down(dx, up, wdw, bdw, wd, tm, tps, ffn_taps, name):
    T, D = dx.shape
    F = up.shape[1] // 2
    hf = HALO_FFN
    nt = T // tm

    def body(dx_ref, up_ref, uph_ref, wdw_ref, bdw_ref, wd_ref, dac_ref, dgate_ref, wsum_ref, aext, win, ac, wacc):
        i = pl.program_id(0)
        aext[0:hf, :] = jnp.where(i % tps == 0, 0.0, uph_ref[...])
        aext[hf:hf + tm, :] = up_ref[:, 0:F]
        _ffn_conv(aext, win, wdw_ref, bdw_ref, ac, tm, ffn_taps)
        dg = _dot_nt(dx_ref[...].astype(BF16), wd_ref[...])
        acv = ac[...]
        sg = _sig(acv)
        dgate_ref[...] = (dg * acv * sg).astype(BF16)
        dac_ref[...] = dg * up_ref[:, F:2 * F] * (sg * (1.0 + acv * (1.0 - sg)))

        @pl.when(i == 0)
        def _():
            wacc[...] = jnp.zeros_like(wacc)

        def wgrad(r0):
            win[...] = aext[pl.ds(r0, CHUNK + hf), :]
            d = dac_ref[pl.ds(r0, CHUNK), :]
            for k in range(ffn_taps):
                o = hf - (ffn_taps - 1) + k
                wacc[8 * k:8 * k + 8, :] += _fold8(d * win[o:o + CHUNK, :])
            wacc[8 * ffn_taps:8 * ffn_taps + 8, :] += _fold8(d)
        _chunks(tm, wgrad)

        @pl.when(i == nt - 1)
        def _():
            wsum_ref[...] = jnp.zeros_like(wsum_ref)
            for k in range(ffn_taps + 1):
                wsum_ref[k:k + 1, :] = _colsum(wacc[8 * k:8 * k + 8, :])

    return _pcall(body, name=name, grid=(nt,),
                  in_specs=[_rows(tm, D), _rows(tm, 2 * F), _prev_halo(tm, hf, F), _full(wdw.shape), _full((1, F)),
                            _full((F, D))],
                  out_specs=[_rows(tm, F), _rows(tm, F, col=1), _acc_spec(8, F)],
                  out_shape=[S_((T, F), F32), S_((T, 2 * F), BF16), S_((8, F), F32)],
                  scratch=[pltpu.VMEM((tm + hf, F), F32), pltpu.VMEM((CHUNK + hf, F), F32), pltpu.VMEM((tm, F), F32),
                           pltpu.VMEM((8 * (ffn_taps + 1), F), F32)],
                  args=(dx, up, up, wdw, bdw, wd))


def _bwd_up(dac, dup, wg, wdw, xin, gn, dx, tm, tps, ffn_taps, name):
    T, D = xin.shape
    F = dac.shape[1]
    ns, _, nc = wg.shape
    hf = HALO_FFN
    nt = T // tm

    def body(dac_ref, dach_ref, dgate_ref, wg_ref, wdw_ref, x_ref, gn_ref, dx_ref, dact_ref, dxo_ref, dgn_ref, dext, win):
        i = pl.program_id(0)
        dext[0:tm, :] = dac_ref[...]
        dext[tm:tm + hf, :] = jnp.where(i % tps == tps - 1, 0.0, dach_ref[...])

        def convt(r0):
            win[...] = dext[pl.ds(r0, CHUNK + hf), :]
            acc = jnp.zeros((CHUNK, F), F32)
            for k in range(ffn_taps):
                o = ffn_taps - 1 - k
                acc = acc + wdw_ref[k:k + 1, :] * win[o:o + CHUNK, :]
            dact_ref[pl.ds(r0, CHUNK), :] = acc.astype(BF16)
        _chunks(tm, convt)

        dh = jnp.zeros((tm, D), F32)
        for s in range(ns):
            src = dact_ref if s < ns // 2 else dgate_ref
            o = (s % (ns // 2)) * nc
            dh = dh + _dot_nt(src[:, o:o + nc], wg_ref[s])
        _, xh, r = _rms(x_ref[...], gn_ref[...])
        dxn, dg = _rms_bwd(dh, xh, r, gn_ref[...])
        dxo_ref[...] = dx_ref[...] + dxn
        _accumulate(dgn_ref, dg, i == 0)

    return _pcall(body, name=name, grid=(nt,),
                  in_specs=[_rows(tm, F), _next_halo(tm, hf, F, T), _rows(tm, F, col=1), _full(wg.shape), _full(wdw.shape),
                            _rows(tm, D), _full((1, D)), _rows(tm, D)],
                  out_specs=[_rows(tm, F), _rows(tm, D), _acc_spec(1, D)],
                  out_shape=[S_((T, 2 * F), BF16), S_((T, D), F32), S_((1, D), F32)],
                  scratch=[pltpu.VMEM((tm + hf, F), F32), pltpu.VMEM((CHUNK + hf, F), F32)],
                  aliases={2: 0}, args=(dac, dac, dup, wg, wdw, xin, gn, dx))


def _mm_tn(x, dy, bk, bn, tt, shard_major, name):
    T, K = x.shape
    N = dy.shape[1]
    nt = T // tt

    def body(x_ref, dy_ref, o_ref, acc):
        t = pl.program_id(2)
        part = _dot_tn(x_ref[...].astype(BF16), dy_ref[...].astype(BF16))
        _accumulate(acc, part, t == 0)

        @pl.when(t == nt - 1)
        def _():
            o_ref[...] = acc[...].astype(o_ref.dtype)

    if shard_major:
        out_spec = pl.BlockSpec((None, bk, bn), lambda i, j, t: (j, i, 0))
        out_shape = S_((N // bn, K, bn), BF16)
    else:
        out_spec = pl.BlockSpec((bk, bn), lambda i, j, t: (i, j))
        out_shape = S_((K, N), BF16)
    return _pcall(body, name=name, grid=(K // bk, N // bn, nt),
                  in_specs=[pl.BlockSpec((tt, bk), lambda i, j, t: (t, i)), pl.BlockSpec((tt, bn), lambda i, j, t: (t, j))],
                  out_specs=out_spec, out_shape=out_shape, scratch=[pltpu.VMEM((bk, bn), F32)], args=(x, dy))


def _pool_bwd(dx3, plb, x2, pw, pb, ps, gn, tm, tps):
    T, D = x2.shape
    ng, cg, _ = pw.shape
    hp = HALO_POOL
    nt = T // tm

    def body(do_ref, doh_ref, pl_ref, x_ref, pw_ref, pb_ref, ps_ref, gn_ref, dxo_ref, dpw_ref, sm_ref, qext, win, dh_s):
        i = pl.program_id(0)
        do = do_ref[...]
        dm = do * ps_ref[...]
        dmh = jnp.where(i % tps == tps - 1, 0.0, doh_ref[...]) * ps_ref[...]
        pv = pl_ref[...]
        mixed = jnp.concatenate([_dot(pv[:, g * cg:(g + 1) * cg], pw_ref[g]) for g in range(ng)], axis=1)
        dscale = _colsum(do * (mixed + pb_ref[...]))
        dbias = _colsum(dm)
        dmb = dm.astype(BF16)
        dmhb = dmh.astype(BF16)
        for g in range(ng):
            cs = slice(g * cg, (g + 1) * cg)
            _accumulate(dpw_ref.at[g], _dot_tn(pv[:, cs], dmb[:, cs]), i == 0)
            qext[0:tm, cs] = _dot_nt(dmb[:, cs], pw_ref[g])
            qext[tm:tm + hp, cs] = _dot_nt(dmhb[:, cs], pw_ref[g])

        def poolt(r0):
            win[...] = qext[pl.ds(r0, CHUNK + hp), :]
            for g, w in enumerate(POOL_WINDOWS):
                cs = slice(g * cg, (g + 1) * cg)
                cur = win[0:CHUNK, cs]
                acc = cur / _pool_counts(i, tps, tm, r0, w)
                for j in range(1, w):
                    cnt = _pool_counts(i, tps, tm, r0 + j, w)
                    acc = acc + win[j:j + CHUNK, cs] / cnt
                dh_s[pl.ds(r0, CHUNK), cs] = acc - cur
        _chunks(tm, poolt)

        _, xh, r = _rms(x_ref[...], gn_ref[...])
        dxn, dg = _rms_bwd(dh_s[...], xh, r, gn_ref[...])
        dxo_ref[...] = do + dxn
        sm = jnp.concatenate([dscale, dbias, dg, jnp.zeros((5, D), F32)], axis=0)
        _accumulate(sm_ref, sm, i == 0)

    return _pcall(body, name="pool_bwd", grid=(nt,),
                  in_specs=[_rows(tm, D), _next_halo(tm, hp, D, T), _rows(tm, D), _rows(tm, D), _full(pw.shape), _full((1, D)),
                            _full((1, D)), _full((1, D))],
                  out_specs=[_rows(tm, D), pl.BlockSpec((ng, cg, cg), lambda i: (0, 0, 0)), _acc_spec(8, D)],
                  out_shape=[S_((T, D), F32), S_((ng, cg, cg), F32), S_((8, D), F32)],
                  scratch=[pltpu.VMEM((tm + hp, D), F32), pltpu.VMEM((CHUNK + hp, D), F32), pltpu.VMEM((tm, D), F32)],
                  args=(dx3, dx3, plb, x2, pw, pb, ps, gn))


def _conv_bwd1(dx1, c, w2, lng, lnb, tm):
    T, D = dx1.shape

    def body(do_ref, c_ref, w2_ref, lng_ref, lnb_ref, dc_ref, sm_ref):
        i = pl.program_id(0)
        do = do_ref[...]
        ds = _dot_nt(do.astype(BF16), w2_ref[...])
        cv = c_ref[...]
        xc = cv - jnp.mean(cv, axis=-1, keepdims=True)
        rstd = lax.rsqrt(jnp.mean(xc * xc, axis=-1, keepdims=True) + LN_EPS)
        nh = xc * rstd
        n = nh * lng_ref[...] + lnb_ref[...]
        sg = _sig(n)
        dn = ds * (sg * (1.0 + n * (1.0 - sg)))
        dnh = dn * lng_ref[...]
        dc = rstd * (dnh - jnp.mean(dnh, axis=-1, keepdims=True) - nh * jnp.mean(dnh * nh, axis=-1, keepdims=True))
        dc_ref[...] = dc
        sm = jnp.concatenate([_colsum(dn * nh), _colsum(dn), _colsum(do), _colsum(dc), jnp.zeros((4, D), F32)], axis=0)
        _accumulate(sm_ref, sm, i == 0)

    return _pcall(body, name="conv_bwd1", grid=(T // tm,),
                  in_specs=[_rows(tm, D), _rows(tm, D), _full((D, D)), _full((1, D)), _full((1, D))],
                  out_specs=[_rows(tm, D), _acc_spec(8, D)],
                  out_shape=[S_((T, D), F32), S_((8, D), F32)], args=(dx1, c, w2, lng, lnb))


def _conv_bwd2(dc, a, x, w1g, wdw, gn, dx1, tm, tps, taps):
    T, D = x.shape
    ns, _, nc = w1g.shape
    hc = HALO_CONV
    nt = T // tm
    tp = wdw.shape[0]

    def body(dc_ref, dch_ref, a_ref, ah_ref, x_ref, w_ref, wdw_ref, gn_ref, dx1_ref,
             gx_ref, dab_ref, dw_ref, db1_ref, dgn_ref, uext, dext, win, du_s, wacc):
        i = pl.program_id(0)
        ah = ah_ref[...]
        uext[0:hc, :] = jnp.where(i % tps == 0, 0.0, ah[:, :D] * _sig(ah[:, D:]))
        av = a_ref[...]
        sg2 = _sig(av[:, D:])
        uext[hc:hc + tm, :] = av[:, :D] * sg2
        dext[0:tm, :] = dc_ref[...]
        dext[tm:tm + hc, :] = jnp.where(i % tps == tps - 1, 0.0, dch_ref[...])

        @pl.when(i == 0)
        def _():
            wacc[...] = jnp.zeros_like(wacc)

        def wgrad(r0):
            win[...] = uext[pl.ds(r0, CHUNK + hc), :]
            d = dc_ref[pl.ds(r0, CHUNK), :]
            for k in range(taps):
                o = hc - (taps - 1) + k
                wacc[8 * k:8 * k + 8, :] += _fold8(d * win[o:o + CHUNK, :])
        _chunks(tm, wgrad)

        def convt(r0):
            win[...] = dext[pl.ds(r0, CHUNK + hc), :]
            acc = jnp.zeros((CHUNK, D), F32)
            for k in range(taps):
                o = taps - 1 - k
                acc = acc + wdw_ref[k:k + 1, :] * win[o:o + CHUNK, :]
            du_s[pl.ds(r0, CHUNK), :] = acc
        _chunks(tm, convt)

        du = du_s[...]
        da1 = du * sg2
        da2 = du * av[:, :D] * (sg2 * (1.0 - sg2))
        dab_ref[:, 0:D] = da1.astype(BF16)
        dab_ref[:, D:2 * D] = da2.astype(BF16)
        _accumulate(db1_ref, jnp.concatenate([_colsum(da1), _colsum(da2)], axis=0), i == 0)
        dh = jnp.zeros((tm, D), F32)
        for s in range(ns):
            dh = dh + _dot_nt(dab_ref[:, s * nc:(s + 1) * nc], w_ref[s])
        _, xh, r = _rms(x_ref[...], gn_ref[...])
        dxn, dg = _rms_bwd(dh, xh, r, gn_ref[...])
        gx_ref[...] = dx1_ref[...] + dxn
        _accumulate(dgn_ref, dg, i == 0)

        @pl.when(i == nt - 1)
        def _():
            dw_ref[...] = jnp.zeros_like(dw_ref)
            for k in range(taps):
                dw_ref[k:k + 1, :] = _colsum(wacc[8 * k:8 * k + 8, :])

    return _pcall(body, name="conv_bwd2", grid=(nt,),
                  in_specs=[_rows(tm, D), _next_halo(tm, hc, D, T), _rows(tm, 2 * D), _prev_halo(tm, hc, 2 * D), _rows(tm, D),
                            _full(w1g.shape), _full(wdw.shape), _full((1, D)), _rows(tm, D)],
                  out_specs=[_rows(tm, D), _rows(tm, 2 * D), _acc_spec(tp, D), _acc_spec(2, D), _acc_spec(1, D)],
                  out_shape=[S_((T, D), F32), S_((T, 2 * D), BF16), S_((tp, D), F32), S_((2, D), F32), S_((1, D), F32)],
                  scratch=[pltpu.VMEM((tm + hc, D), F32), pltpu.VMEM((tm + hc, D), F32), pltpu.VMEM((CHUNK + hc, D), F32),
                           pltpu.VMEM((tm, D), F32), pltpu.VMEM((8 * taps, D), F32)],
                  args=(dc, dc, a, a, x, w1g, wdw, gn, dx1))


def _place():
    x, y, c = lax.axis_index("x"), lax.axis_index("y"), lax.axis_index("c")
    chips = [(1 - x, y), (x, 1 - y), (1 - x, 1 - y)]
    return x, y, c, chips


def _own_slot(w, layer, ids, dtype, name):
    _, r, c = w.shape
    br = _row_block(r, c)

    def body(ids_ref, w_ref, o_ref):
        o_ref[...] = w_ref[...].astype(dtype)

    return _pcall(body, name=name, grid=(r // br,), prefetch=1,
                  in_specs=[pl.BlockSpec((None, br, c), lambda i, ids: (layer, i, 0))],
                  out_specs=pl.BlockSpec((None, br, c), lambda i, ids: (ids[0], i, 0)),
                  out_shape=S_((N_CHIPS, r, c), dtype), args=(ids, w))


def _gather(slots):
    n = len(slots)

    def body(*refs):
        outs = refs[n:2 * n]
        s_ici, r_ici, s_d2d, r_d2d = refs[2 * n:]
        x, y, c, chips = _place()
        q = 2 * x + y

        def half(i, h):
            rh = slots[i].shape[1] // 2
            return pl.ds(pl.multiple_of(h * rh, 8), rh)

        def ici(i, j, chip, to):
            blk = outs[i].at[chip, half(i, c)]
            return pltpu.make_async_remote_copy(src_ref=blk, dst_ref=blk, send_sem=s_ici.at[i, j], recv_sem=r_ici.at[i, j],
                                                device_id=to, device_id_type=MESH)

        def d2d(i, j, chip, h):
            blk = outs[i].at[chip, half(i, h)]
            return pltpu.make_async_remote_copy(src_ref=blk, dst_ref=blk, send_sem=s_d2d.at[i, j], recv_sem=r_d2d.at[i, j],
                                                device_id=(x, y, 1 - c), device_id_type=MESH)

        first = [ici(i, j, q, (*chip, c)) for i in range(n) for j, chip in enumerate(chips)]
        for cp in first:
            cp.start()
        passed = []
        for i in range(n):
            for j, chip in enumerate(chips):
                qj = 2 * chip[0] + chip[1]
                ici(i, j, qj, (x, y, c)).wait_recv()
                fw = d2d(i, j, qj, c)
                fw.start()
                passed.append(fw)
        for i in range(n):
            for j, chip in enumerate(chips):
                d2d(i, j, 2 * chip[0] + chip[1], 1 - c).wait_recv()
        for cp in first + passed:
            cp.wait_send()

    any_spec = pl.BlockSpec(memory_space=pl.ANY)
    return pl.pallas_call(
        body, name="gather_weights", in_specs=[any_spec] * n, out_specs=[any_spec] * n,
        out_shape=[S_(s.shape, s.dtype) for s in slots], input_output_aliases={i: i for i in range(n)},
        scratch_shapes=[pltpu.SemaphoreType.DMA((n, 3))] * 4,
    )(*slots)


def _pair_exchange(parts):
    n = len(parts)

    def body(*refs):
        srcs, outs = refs[:n], refs[n:2 * n]
        s_sem, r_sem = refs[2 * n:]
        x, y, c, _ = _place()
        cps = []
        for i in range(n):
            for s in range(N_CHIPS):
                cp = pltpu.make_async_remote_copy(src_ref=srcs[i].at[s, 1 - c], dst_ref=outs[i].at[s], send_sem=s_sem.at[i, s],
                                                  recv_sem=r_sem.at[i, s], device_id=(x, y, 1 - c), device_id_type=MESH)
                cp.start()
                cps.append(cp)
        for cp in cps:
            cp.wait()

    any_spec = pl.BlockSpec(memory_space=pl.ANY)
    return pl.pallas_call(
        body, name="grad_pair_exchange", in_specs=[any_spec] * n, out_specs=[any_spec] * n,
        out_shape=[S_((N_CHIPS,) + p.shape[2:], p.dtype) for p in parts],
        scratch_shapes=[pltpu.SemaphoreType.DMA((n, N_CHIPS))] * 2,
    )(*parts)


def _chip_exchange(sums):
    n = len(sums)

    def body(*refs):
        srcs, outs = refs[:n], refs[n:2 * n]
        s_sem, r_sem = refs[2 * n:]
        x, y, c, chips = _place()
        cps = []
        for i in range(n):
            for j, chip in enumerate(chips):
                cp = pltpu.make_async_remote_copy(src_ref=srcs[i].at[2 * chip[0] + chip[1]], dst_ref=outs[i].at[j],
                                                  send_sem=s_sem.at[i, j], recv_sem=r_sem.at[i, j],
                                                  device_id=(*chip, c), device_id_type=MESH)
                cp.start()
                cps.append(cp)
        for cp in cps:
            cp.wait()

    any_spec = pl.BlockSpec(memory_space=pl.ANY)
    return pl.pallas_call(
        body, name="grad_chip_exchange", in_specs=[any_spec] * n, out_specs=[any_spec] * n,
        out_shape=[S_((3,) + s.shape[1:], s.dtype) for s in sums],
        scratch_shapes=[pltpu.SemaphoreType.DMA((n, 3))] * 2,
    )(*sums)


def _share_halves(bufs):
    n = len(bufs)
    nl = max(b.shape[0] for b in bufs)

    def body(*refs):
        outs = refs[n:2 * n]
        s_sem, r_sem = refs[2 * n:]
        x, y, c, _ = _place()
        cps = []
        for i in range(n):
            for l in range(bufs[i].shape[0]):
                blk = outs[i].at[l, c]
                cp = pltpu.make_async_remote_copy(src_ref=blk, dst_ref=blk, send_sem=s_sem.at[i, l], recv_sem=r_sem.at[i, l],
                                                  device_id=(x, y, 1 - c), device_id_type=MESH)
                cp.start()
                cps.append(cp)
        for cp in cps:
            cp.wait()

    any_spec = pl.BlockSpec(memory_space=pl.ANY)
    return pl.pallas_call(
        body, name="grad_share_halves", in_specs=[any_spec] * n, out_specs=[any_spec] * n,
        out_shape=[S_(b.shape, b.dtype) for b in bufs], input_output_aliases={i: i for i in range(n)},
        scratch_shapes=[pltpu.SemaphoreType.DMA((n, nl))] * 2,
    )(*bufs)


def _small_allreduce(groups):
    n = len(groups)
    arrays = []
    for grp in groups:
        for p in grp:
            if not isinstance(p, int) and not any(p[0] is a for a in arrays):
                arrays.append(p[0])
    shapes = []
    for grp in groups:
        rows = sum(p if isinstance(p, int) else p[2] - p[1] for p in grp)
        width = next(p[0].shape[1] for p in grp if not isinstance(p, int))
        assert rows % 8 == 0
        shapes.append((rows, width))
    na = len(arrays)

    def body(*refs):
        a_refs, o_refs = refs[:na], refs[na:na + n]
        rest = refs[na + n:]
        v_refs, sibs, css, gots = rest[:n], rest[n:2 * n], rest[2 * n:3 * n], rest[3 * n:4 * n]
        s_sem, r_sem = rest[4 * n:]
        for i, grp in enumerate(groups):
            o = 0
            for p in grp:
                if isinstance(p, int):
                    v_refs[i][o:o + p, :] = jnp.zeros((p, shapes[i][1]), F32)
                    o += p
                else:
                    k = next(t for t, a in enumerate(arrays) if a is p[0])
                    v_refs[i][o:o + p[2] - p[1], :] = a_refs[k][p[1]:p[2], :]
                    o += p[2] - p[1]
        x, y, c, chips = _place()
        q = 2 * x + y
        to_sib = [pltpu.make_async_remote_copy(src_ref=v_refs[i], dst_ref=sibs[i], send_sem=s_sem.at[i, 3], recv_sem=r_sem.at[i, 3],
                                               device_id=(x, y, 1 - c), device_id_type=MESH) for i in range(n)]
        for cp in to_sib:
            cp.start()
        cps = []
        for i in range(n):
            to_sib[i].wait()
            mine, other = v_refs[i][...], sibs[i][...]
            css[i][...] = jnp.where(c == 0, mine, other) + jnp.where(c == 0, other, mine)
            for j, chip in enumerate(chips):
                cp = pltpu.make_async_remote_copy(src_ref=css[i], dst_ref=gots[i].at[j], send_sem=s_sem.at[i, j],
                                                  recv_sem=r_sem.at[i, j], device_id=(*chip, c), device_id_type=MESH)
                cp.start()
                cps.append(cp)
        for cp in cps:
            cp.wait()
        flips = [2, 1, 3]
        for i in range(n):
            total = None
            for k in range(N_CHIPS):
                d = q ^ k
                term = jnp.where(d == 0, css[i][...], 0.0)
                for j in range(3):
                    term = jnp.where(d == flips[j], gots[i][j], term)
                total = term if total is None else total + term
            o_refs[i][...] = total

    vm = pl.BlockSpec(memory_space=pltpu.VMEM)
    return pl.pallas_call(
        body, name="small_allreduce", in_specs=[vm] * na, out_specs=[vm] * n, out_shape=[S_(s, F32) for s in shapes],
        scratch_shapes=[pltpu.VMEM(s, F32) for s in shapes] * 3 + [pltpu.VMEM((3,) + s, F32) for s in shapes]
        + [pltpu.SemaphoreType.DMA((n, 4)), pltpu.SemaphoreType.DMA((n, 4))],
    )(*arrays)


def _row_block(r, c, target_bytes=1 << 20):
    br = r
    while br % 32 == 0 and br * c * 4 > target_bytes:
        br //= 2
    return br


def _chip_sum(part, sib, ids, name):
    _, _, r, c = part.shape
    br = _row_block(r, c)

    def body(ids_ref, p_ref, s_ref, o_ref):
        o_ref[...] = (p_ref[...].astype(F32) + s_ref[...].astype(F32)).astype(BF16)

    return _pcall(body, name=name, grid=(N_CHIPS, r // br), prefetch=1,
                  in_specs=[pl.BlockSpec((None, None, br, c), lambda s, i, ids: (s, ids[1], i, 0)),
                            pl.BlockSpec((None, br, c), lambda s, i, ids: (s, i, 0))],
                  out_specs=pl.BlockSpec((None, br, c), lambda s, i, ids: (s, i, 0)),
                  out_shape=S_((N_CHIPS, r, c), BF16), args=(ids, part, sib))


def _final_sum(part, sib, got, ids, name, layer=0, n_layers=1, buf=None):
    _, r, c = got.shape
    br = _row_block(r, c)

    def body(ids_ref, p_ref, s_ref, got_ref, *rest):
        v = p_ref[...].astype(F32) + s_ref[...].astype(F32)
        for j in range(3):
            v = v + got_ref[j].astype(F32)
        rest[-1][...] = v

    in_specs = [pl.BlockSpec((None, None, br, c), lambda i, ids: (ids[0], ids[1], i, 0)),
                pl.BlockSpec((None, br, c), lambda i, ids: (ids[0], i, 0)),
                pl.BlockSpec((3, br, c), lambda i, ids: (0, i, 0))]
    args = [ids, part, sib, got]
    if buf is not None:
        in_specs.append(pl.BlockSpec(memory_space=pl.ANY))
        args.append(buf)
    return _pcall(body, name=name, grid=(r // br,), prefetch=1, in_specs=in_specs,
                  out_specs=pl.BlockSpec((None, None, br, c), lambda i, ids: (layer, ids[1], i, 0)),
                  out_shape=S_((n_layers, 2, r, c), F32), args=args, aliases={4: 0} if buf is not None else None)


def _adam_math(w, g, m, v):
    c1 = 1.0 / (1.0 - ADAM_B1 ** ADAM_STEP)
    c2 = 1.0 / (1.0 - ADAM_B2 ** ADAM_STEP)
    mn = ADAM_B1 * m + (1.0 - ADAM_B1) * g
    vn = ADAM_B2 * v + (1.0 - ADAM_B2) * (g * g)
    return -ADAM_LR * ((mn * c1) / (jnp.sqrt(vn * c2) + ADAM_EPS) + ADAM_WD * w), mn, vn


def _adamw(w, g, m, v, name):
    nl, r, c = w.shape
    br = _row_block(r, c, 1 << 19)

    def body(w_ref, g_ref, m_ref, v_ref, d_ref, mo_ref, vo_ref):
        d_ref[...], mo_ref[...], vo_ref[...] = _adam_math(w_ref[...], g_ref[...], m_ref[...], v_ref[...])

    blk = pl.BlockSpec((None, br, c), lambda l, i: (l, i, 0))
    return _pcall(body, name=name, grid=(nl, r // br), in_specs=[blk] * 4, out_specs=[blk] * 3,
                  out_shape=[S_((nl, r, c), F32)] * 3, args=(w, g, m, v))


def _adamw_small(rd, rf, sharded_g, ws, ms, vs, rows_d, rows_f):
    n = len(ws)
    ns = len(sharded_g)
    rep = [k for k in range(n) if rows_d[k] is not None or rows_f[k] is not None]

    def body(*refs):
        rd_ref, rf_ref = refs[0], refs[1]
        sg = refs[2:2 + ns]
        w_refs, m_refs, v_refs = (refs[2 + ns + t * n:2 + ns + (t + 1) * n] for t in range(3))
        outs = refs[2 + ns + 3 * n:]
        g_out, d_out, m_out, v_out = outs[:len(rep)], outs[len(rep):len(rep) + n], outs[len(rep) + n:len(rep) + 2 * n], \
            outs[len(rep) + 2 * n:]
        si = 0
        for k in range(n):
            shape = w_refs[k].shape
            if rows_d[k] is not None or rows_f[k] is not None:
                src, rws = (rd_ref, rows_d[k]) if rows_d[k] is not None else (rf_ref, rows_f[k])
                axis = 0 if shape[0] == len(rws) else 1
                g = src[rws[0]:rws[0] + 1, :]
                if len(rws) > 1:
                    g = jnp.concatenate([src[r:r + 1, :] for r in rws], axis=axis)
                g_out[rep.index(k)][...] = g
            else:
                g = sg[si][...]
                si += 1
            d_out[k][...], m_out[k][...], v_out[k][...] = _adam_math(w_refs[k][...], g, m_refs[k][...], v_refs[k][...])

    vm = pl.BlockSpec(memory_space=pltpu.VMEM)
    shapes = [S_(w.shape, F32) for w in ws]
    outs = pl.pallas_call(
        body, name="adamw_small", in_specs=[vm] * (2 + ns + 3 * n), out_specs=[vm] * (len(rep) + 3 * n),
        out_shape=[shapes[k] for k in rep] + shapes * 3,
    )(rd, rf, *sharded_g, *ws, *ms, *vs)
    g_rep = dict(zip(rep, outs[:len(rep)]))
    o = outs[len(rep):]
    return g_rep, o[:n], o[n:2 * n], o[2 * n:]


def _pack(pieces, rows):
    flat = jnp.concatenate([p.reshape(-1).astype(F32) for p in pieces])
    return jnp.pad(flat, (0, rows * 128 - flat.shape[0])).reshape(rows, 128)


def _unpack(packed, shapes):
    flat = packed.reshape(-1)
    out, o = [], 0
    for s in shapes:
        n = 1
        for d in s:
            n *= d
        out.append(flat[o:o + n].reshape(s))
        o += n
    return out


def _rows_for(pieces_or_shapes):
    n = 0
    for p in pieces_or_shapes:
        k = 1
        for d in (p if isinstance(p, tuple) else p.shape):
            k *= d
        n += k
    return -(-n // 1024) * 8


def kernel(x, norm_mix, norm_ffn, conv_w_pw1, conv_b_pw1, conv_w_dw, conv_b_dw, conv_ln_g, conv_ln_b, conv_w_pw2, conv_b_pw2, pool_w, pool_b, pool_scale, ffn_w_up, ffn_w_dw, ffn_b_dw, ffn_w_down, final_norm, loss_target, m_norm_mix, m_norm_ffn, m_conv_w_pw1, m_conv_b_pw1, m_conv_w_dw, m_conv_b_dw, m_conv_ln_g, m_conv_ln_b, m_conv_w_pw2, m_conv_b_pw2, m_pool_w, m_pool_b, m_pool_scale, m_ffn_w_up, m_ffn_w_dw, m_ffn_b_dw, m_ffn_w_down, m_final_norm, v_norm_mix, v_norm_ffn, v_conv_w_pw1, v_conv_b_pw1, v_conv_w_dw, v_conv_b_dw, v_conv_ln_g, v_conv_ln_b, v_conv_w_pw2, v_conv_b_pw2, v_pool_w, v_pool_b, v_pool_scale, v_ffn_w_up, v_ffn_w_dw, v_ffn_b_dw, v_ffn_w_down, v_final_norm):
    nb, seq, D = x.shape
    T = nb * seq
    F = ffn_w_down.shape[1] * N_CHIPS
    taps = conv_w_dw.shape[1]
    ffn_taps = ffn_w_dw.shape[1]
    ng = pool_w.shape[1]
    cg = pool_w.shape[3]
    dsh = D // N_CHIPS
    fsh = F // N_CHIPS
    assert taps - 1 <= HALO_CONV and ffn_taps - 1 <= HALO_FFN and max(POOL_WINDOWS) <= HALO_POOL
    tm = min(TILE_D, seq)
    tf = min(TILE_F, seq)
    assert seq % tm == 0 and seq % tf == 0 and tm % HALO_CONV == 0 and tf % CHUNK == 0
    tps, tpf = seq // tm, seq // tf

    xi, yi, ci = lax.axis_index("x"), lax.axis_index("y"), lax.axis_index("c")
    qi = 2 * xi + yi
    ids = jnp.stack([qi, ci]).astype(jnp.int32)
    x2d = x.reshape(T, D)
    tg2d = loss_target.reshape(T, D)

    small_sharded = [conv_w_dw[0], ffn_w_dw, pool_b, pool_scale]
    srows = 2 * _rows_for(small_sharded)
    gathered = _gather([
        _own_slot(conv_w_pw1, 0, ids, BF16, "slot_pw1"), _own_slot(conv_w_pw2, 0, ids, BF16, "slot_pw2"),
        _own_slot(pool_w.reshape(1, ng * (cg // N_CHIPS), cg), 0, ids, BF16, "slot_pool"),
        _own_slot(ffn_w_up, 0, ids, BF16, "slot_up0"), _own_slot(ffn_w_up, 1, ids, BF16, "slot_up1"),
        _own_slot(ffn_w_down, 0, ids, BF16, "slot_down0"), _own_slot(ffn_w_down, 1, ids, BF16, "slot_down1"),
        _own_slot(_pack(small_sharded, srows)[None], 0, ids, F32, "slot_small")])
    w1g, w2g, pwg, wup0, wup1, wd0, wd1, smallg = gathered
    w2f = w2g.reshape(D, D)
    pwf = pwg.reshape(N_CHIPS, ng, cg // N_CHIPS, cg).transpose(1, 0, 2, 3).reshape(ng, cg, cg)
    wup = [wup0, wup1]
    wdn = [wd0.reshape(F, D), wd1.reshape(F, D)]
    sm_parts = [_unpack(smallg[s], [(taps, dsh), (2, ffn_taps, fsh), (1, dsh), (1, dsh)]) for s in range(N_CHIPS)]
    wdw_f = jnp.concatenate([p[0] for p in sm_parts], axis=1)
    wdw_f = jnp.pad(wdw_f, ((0, HALO_CONV - taps), (0, 0)))
    fdw_f = jnp.concatenate([p[1] for p in sm_parts], axis=2)
    fdw_f = jnp.pad(fdw_f, ((0, 0), (0, 8 - ffn_taps), (0, 0)))
    pb_f = jnp.concatenate([p[2] for p in sm_parts], axis=1)
    ps_f = jnp.concatenate([p[3] for p in sm_parts], axis=1)

    hb0, a = _conv_in(x2d, norm_mix[0:1], w1g, conv_b_pw1, tm)
    c, sb, x1, hb1 = _conv_mid(a, x2d, wdw_f, conv_b_dw, conv_ln_g, conv_ln_b, w2f, conv_b_pw2, norm_ffn[0:1], tm, tps, taps)
    up0 = _mm_up(hb1, wup[0], tm, "ffn_up0")
    gb0, x2, h2 = _ffn_down(up0, x1, fdw_f[0], ffn_b_dw[0:1], wdn[0], norm_mix[1:2], tf, tpf, ffn_taps, "ffn_down0")
    plb, x3, hb3 = _pool_fwd(h2, x2, pwf, pb_f, ps_f, norm_ffn[1:2], tm, tps)
    up1 = _mm_up(hb3, wup[1], tm, "ffn_up1")
    gb1, dx4, loss_part, d_final = _ffn_down(up1, x3, fdw_f[1], ffn_b_dw[1:2], wdn[1], final_norm.reshape(1, D), tf, tpf,
                                             ffn_taps, "ffn_down1", target=tg2d)

    dac1, dup1, fw1 = _bwd_down(dx4, up1, fdw_f[1], ffn_b_dw[1:2], wdn[1], tf, tpf, ffn_taps, "bwd_down1")
    p_down1 = _mm_tn(gb1, dx4, F // 2, D, tm, False, "dw_down1")
    dup1, dx3, dnf1 = _bwd_up(dac1, dup1, wup[1], fdw_f[1], x3, norm_ffn[1:2], dx4, tf, tpf, ffn_taps, "bwd_up1")
    p_up1 = _mm_tn(hb3, dup1, D, 2 * fsh, tm, True, "dw_up1")
    dx2, dpw, pool_sm = _pool_bwd(dx3, plb, x2, pwf, pb_f, ps_f, norm_mix[1:2], tm, tps)
    dac0, dup0, fw0 = _bwd_down(dx2, up0, fdw_f[0], ffn_b_dw[0:1], wdn[0], tf, tpf, ffn_taps, "bwd_down0")
    p_down0 = _mm_tn(gb0, dx2, F // 2, D, tm, False, "dw_down0")
    dup0, dx1, dnf0 = _bwd_up(dac0, dup0, wup[0], fdw_f[0], x1, norm_ffn[0:1], dx2, tf, tpf, ffn_taps, "bwd_up0")
    p_up0 = _mm_tn(hb1, dup0, D, 2 * fsh, tm, True, "dw_up0")
    dc, conv_sm = _conv_bwd1(dx1, c, w2f, conv_ln_g, conv_ln_b, tm)
    p_pw2 = _mm_tn(sb, dx1, D, D, tm, False, "dw_pw2")
    grad_x, dab, dwdw, db1, dnm0 = _conv_bwd2(dc, a, x2d, w1g, wdw_f, norm_mix[0:1], dx1, tm, tps, taps)
    p_pw1 = _mm_tn(hb0, dab, D, 2 * dsh, tm, True, "dw_pw1")

    csh = cg // N_CHIPS
    p_pool = dpw.reshape(2, ng // 2, N_CHIPS, csh, cg).transpose(2, 0, 1, 3, 4).reshape(N_CHIPS, 2, (ng // 2) * csh, cg)
    parts = [p_pw1.reshape(N_CHIPS, 2, D // 2, 2 * dsh), p_pw2.reshape(N_CHIPS, 2, dsh // 2, D), p_pool.astype(BF16),
             p_up0.reshape(N_CHIPS, 2, D // 2, 2 * fsh), p_up1.reshape(N_CHIPS, 2, D // 2, 2 * fsh),
             p_down0.reshape(N_CHIPS, 2, fsh // 2, D), p_down1.reshape(N_CHIPS, 2, fsh // 2, D)]
    names = ["pw1", "pw2", "pool", "up0", "up1", "down0", "down1"]
    from_sib = _pair_exchange(parts)
    sums = [_chip_sum(p, s, ids, "chip_sum_" + nm) for p, s, nm in zip(parts, from_sib, names)]
    got = _chip_exchange(sums)
    fs = {nm: (p, s, g) for p, s, g, nm in zip(parts, from_sib, got, names)}
    bufs = [_final_sum(*fs[nm], ids, "final_sum_" + nm) for nm in ("pw1", "pw2", "pool")]
    for nm in ("up", "down"):
        b = _final_sum(*fs[nm + "0"], ids, "final_sum_" + nm + "0", layer=0, n_layers=2)
        bufs.append(_final_sum(*fs[nm + "1"], ids, "final_sum_" + nm + "1", layer=1, n_layers=2, buf=b))
    g_pw1, g_pw2, g_pool, g_up, g_down = [b.reshape(b.shape[0], 2 * b.shape[2], b.shape[3]) for b in _share_halves(bufs)]

    rd, rf = _small_allreduce([
        [(dnm0, 0, 1), (pool_sm, 2, 3), (dnf0, 0, 1), (dnf1, 0, 1), (db1, 0, 2), (conv_sm, 0, 4), (d_final, 0, 1),
         (pool_sm, 0, 2), 3, (dwdw, 0, dwdw.shape[0])],
        [(fw0, 0, 4), (fw1, 0, 4)]])
    g_wdw = lax.dynamic_slice_in_dim(rd[16:16 + taps], qi * dsh, dsh, axis=1)
    g_ps = lax.dynamic_slice_in_dim(rd[11:12], qi * dsh, dsh, axis=1)
    g_pb = lax.dynamic_slice_in_dim(rd[12:13], qi * dsh, dsh, axis=1)
    g_fdw = lax.dynamic_slice_in_dim(rf.reshape(2, 4, F)[:, :ffn_taps], qi * fsh, fsh, axis=2).reshape(2 * ffn_taps, fsh)

    def big(w, g, m, v, nm):
        d, mn, vn = _adamw(w.reshape(g.shape), g, m.reshape(g.shape), v.reshape(g.shape), "adamw_" + nm)
        return g.reshape(w.shape), d.reshape(w.shape), mn.reshape(w.shape), vn.reshape(w.shape)

    o_pw1 = big(conv_w_pw1, g_pw1, m_conv_w_pw1, v_conv_w_pw1, "pw1")
    o_pw2 = big(conv_w_pw2, g_pw2, m_conv_w_pw2, v_conv_w_pw2, "pw2")
    o_pool = big(pool_w, g_pool, m_pool_w, v_pool_w, "pool")
    o_up = big(ffn_w_up, g_up, m_ffn_w_up, v_ffn_w_up, "up")
    o_down = big(ffn_w_down, g_down, m_ffn_w_down, v_ffn_w_down, "down")

    snames = ["norm_mix", "norm_ffn", "b_pw1", "w_dw", "b_dw", "ln_g", "ln_b", "b_pw2", "pool_b", "pool_scale", "ffn_w_dw",
              "ffn_b_dw", "final_norm"]
    sw = [norm_mix, norm_ffn, conv_b_pw1, conv_w_dw, conv_b_dw, conv_ln_g, conv_ln_b, conv_b_pw2, pool_b, pool_scale,
          ffn_w_dw, ffn_b_dw, final_norm]
    smm = [m_norm_mix, m_norm_ffn, m_conv_b_pw1, m_conv_w_dw, m_conv_b_dw, m_conv_ln_g, m_conv_ln_b, m_conv_b_pw2, m_pool_b,
           m_pool_scale, m_ffn_w_dw, m_ffn_b_dw, m_final_norm]
    svv = [v_norm_mix, v_norm_ffn, v_conv_b_pw1, v_conv_w_dw, v_conv_b_dw, v_conv_ln_g, v_conv_ln_b, v_conv_b_pw2, v_pool_b,
           v_pool_scale, v_ffn_w_dw, v_ffn_b_dw, v_final_norm]
    sshapes = [tuple(w.shape) for w in sw]
    two_d = [(2, D), (2, D), (1, 2 * D), (taps, dsh), (1, D), (1, D), (1, D), (1, D), (1, dsh), (1, dsh), (2 * ffn_taps, fsh),
             (2, F), (1, D)]
    rows_d = [[0, 1], [2, 3], [4, 5], None, [9], [6], [7], [8], None, None, None, None, [10]]
    rows_f = [None] * 11 + [[3, 7], None]
    as2d = lambda ts: [t.reshape(s) for t, s in zip(ts, two_d)]
    g_rep, sd, sm_new, sv_new = _adamw_small(rd, rf, [g_wdw, g_pb, g_ps, g_fdw], as2d(sw), as2d(smm), as2d(svv), rows_d, rows_f)
    sg = dict(g_rep)
    sg.update({3: g_wdw, 8: g_pb, 9: g_ps, 10: g_fdw})
    small_out = {n: tuple(t.reshape(sshapes[k]) for t in (sg[k], sd[k], sm_new[k], sv_new[k])) for k, n in enumerate(snames)}

    order = [small_out["norm_mix"], small_out["norm_ffn"], o_pw1, small_out["b_pw1"], small_out["w_dw"], small_out["b_dw"],
             small_out["ln_g"], small_out["ln_b"], o_pw2, small_out["b_pw2"], o_pool, small_out["pool_b"],
             small_out["pool_scale"], o_up, small_out["ffn_w_dw"], small_out["ffn_b_dw"], o_down, small_out["final_norm"]]
    loss = lax.psum(loss_part[0, 0], ("x", "y", "c"))
    return (loss, grad_x.reshape(nb, seq, D), *[o[0] for o in order], *[o[1] for o in order], *[o[2] for o in order],
            *[o[3] for o in order])
```

```python
import functools

import jax
import jax.numpy as jnp
from jax import lax
from jax.experimental import pallas as pl
from jax.experimental.pallas import tpu as pltpu

F32, BF16 = jnp.float32, jnp.bfloat16
S_ = jax.ShapeDtypeStruct
MESH = pl.DeviceIdType.MESH

RMS_EPS, LN_EPS = 1e-6, 1e-5
POOL_WINDOWS = (2, 4, 8, 16)
ADAM_LR, ADAM_B1, ADAM_B2, ADAM_EPS, ADAM_WD, ADAM_STEP = 0.001, 0.9, 0.999, 1e-08, 0.01, 10

VMEM_LIMIT_BYTES = 60 * 1024 * 1024
N_CHIPS = 4
CHUNK = 16
HALO_CONV = 32
HALO_POOL = 32
HALO_FFN = 8
TILE_D = 512
TILE_F = 256


def _pcall(body, *, name, grid, in_specs, out_specs, out_shape, args, scratch=(), aliases=None, prefetch=0):
    params = pltpu.CompilerParams(dimension_semantics=("arbitrary",) * len(grid), vmem_limit_bytes=VMEM_LIMIT_BYTES)
    if prefetch:
        spec = pltpu.PrefetchScalarGridSpec(num_scalar_prefetch=prefetch, grid=grid, in_specs=in_specs,
                                            out_specs=out_specs, scratch_shapes=list(scratch))
        return pl.pallas_call(body, name=name, grid_spec=spec, out_shape=out_shape,
                              input_output_aliases=aliases or {}, compiler_params=params)(*args)
    return pl.pallas_call(body, name=name, grid=grid, in_specs=in_specs, out_specs=out_specs, out_shape=out_shape,
                          scratch_shapes=list(scratch), input_output_aliases=aliases or {},
                          compiler_params=params)(*args)


def _full(shape):
    n = len(shape)
    return pl.BlockSpec(tuple(shape), lambda *_: (0,) * n, pipeline_mode=pl.Buffered(1))


def _rows(tm, c, col=0):
    return pl.BlockSpec((tm, c), lambda i, *_: (i, col))


def _cols(r, tm):
    return pl.BlockSpec((r, tm), lambda i, *_: (0, i))


def _prev_halo(tm, hb, c):
    return pl.BlockSpec((hb, c), lambda i, *_: (jnp.maximum(i * (tm // hb) - 1, 0), 0))


def _next_halo(tm, hb, c, total_rows):
    last = total_rows // hb - 1
    return pl.BlockSpec((hb, c), lambda i, *_: (jnp.minimum((i + 1) * (tm // hb), last), 0))


def _acc_spec(r, c):
    return pl.BlockSpec((r, c), lambda *_: (0, 0))


def _rms(x, g):
    r = lax.rsqrt(jnp.mean(x * x, axis=-1, keepdims=True) + RMS_EPS)
    xh = x * r
    return xh * g, xh, r


def _rms_bwd(dy, xh, r, g):
    dxh = dy * g
    dx = r * (dxh - xh * jnp.mean(dxh * xh, axis=-1, keepdims=True))
    return dx, jnp.sum(dy * xh, axis=0, keepdims=True)


def _colsum(v):
    return jnp.sum(v, axis=0, keepdims=True)


def _fold8(v):
    out = v[0:8]
    for j in range(1, v.shape[0] // 8):
        out = out + v[8 * j:8 * j + 8]
    return out


def _chunks(n_rows, fn):
    def step(j, carry):
        fn(pl.multiple_of(j * CHUNK, CHUNK))
        return carry
    lax.fori_loop(0, n_rows // CHUNK, step, 0)


def _dot(a, b):
    return jnp.dot(a, b, preferred_element_type=F32)


def _dot_nt(a, b):
    return lax.dot_general(a, b, (((1,), (1,)), ((), ())), preferred_element_type=F32)


def _dot_tn(a, b):
    return lax.dot_general(a, b, (((0,), (0,)), ((), ())), preferred_element_type=F32)


def _sig(v):
    return jax.nn.sigmoid(v)


def _accumulate(ref, val, first):
    @pl.when(first)
    def _():
        ref[...] = val

    @pl.when(jnp.logical_not(first))
    def _():
        ref[...] = ref[...] + val


def _conv_in(x, g, w1g, b1, tm):
    T, D = x.shape
    ns, _, nc = w1g.shape

    def body(x_ref, g_ref, w_ref, b_ref, hbt_ref, a_ref):
        hb = _rms(x_ref[...], g_ref[...])[0].astype(BF16)
        hbt_ref[...] = hb.T
        for s in range(ns):
            a_ref[:, s * nc:(s + 1) * nc] = _dot(hb, w_ref[s]) + b_ref[:, s * nc:(s + 1) * nc]

    return _pcall(body, name="conv_in", grid=(T // tm,),
                  in_specs=[_rows(tm, D), _full((1, D)), _full(w1g.shape), _full((1, ns * nc))],
                  out_specs=[_cols(D, tm), _rows(tm, ns * nc)],
                  out_shape=[S_((D, T), BF16), S_((T, ns * nc), F32)], args=(x, g, w1g, b1))


def _conv_mid(a, x, wdw, bdw, lng, lnb, w2, b2, gn, tm, tps, taps):
    T, D = x.shape
    hc = HALO_CONV

    def body(a_ref, ah_ref, x_ref, wdw_ref, bdw_ref, lng_ref, lnb_ref, w2_ref, b2_ref, gn_ref,
             c_ref, sbt_ref, x1_ref, hb_ref, hbt_ref, uext, shifted):
        i = pl.program_id(0)
        ah = ah_ref[...]
        uext[0:hc, :] = jnp.where(i % tps == 0, 0.0, ah[:, :D] * _sig(ah[:, D:]))
        av = a_ref[...]
        uext[hc:hc + tm, :] = av[:, :D] * _sig(av[:, D:])

        def conv(r0, src, ks, first):
            acc = jnp.broadcast_to(bdw_ref[...], (CHUNK, D)) if first else c_ref[pl.ds(r0, CHUNK), :]
            for k, o in ks:
                acc = acc + wdw_ref[k:k + 1, :] * src[pl.ds(pl.multiple_of(r0 + o, 8), CHUNK), :]
            c_ref[pl.ds(r0, CHUNK), :] = acc
        _by_sublane_shift(uext, shifted, tm, [hc - (taps - 1) + k for k in range(taps)], conv)

        cv = c_ref[...]
        xc = cv - jnp.mean(cv, axis=-1, keepdims=True)
        nh = xc * lax.rsqrt(jnp.mean(xc * xc, axis=-1, keepdims=True) + LN_EPS)
        n = nh * lng_ref[...] + lnb_ref[...]
        sb = (n * _sig(n)).astype(BF16)
        sbt_ref[...] = sb.T
        x1 = x_ref[...] + _dot(sb, w2_ref[...]) + b2_ref[...]
        x1_ref[...] = x1
        hb = _rms(x1, gn_ref[...])[0].astype(BF16)
        hb_ref[...] = hb
        hbt_ref[...] = hb.T

    return _pcall(body, name="conv_mid", grid=(T // tm,),
                  in_specs=[_rows(tm, 2 * D), _prev_halo(tm, hc, 2 * D), _rows(tm, D), _full(wdw.shape), _full((1, D)),
                            _full((1, D)), _full((1, D)), _full((D, D)), _full((1, D)), _full((1, D))],
                  out_specs=[_rows(tm, D), _cols(D, tm), _rows(tm, D), _rows(tm, D), _cols(D, tm)],
                  out_shape=[S_((T, D), F32), S_((D, T), BF16), S_((T, D), F32), S_((T, D), BF16), S_((D, T), BF16)],
                  scratch=[pltpu.VMEM((tm + hc, D), F32), pltpu.VMEM((tm + hc - 8, D), F32)],
                  args=(a, a, x, wdw, bdw, lng, lnb, w2, b2, gn))


def _mm_up(hb, wg, tm, name):
    T, D = hb.shape
    ns, _, nc = wg.shape

    def body(h_ref, w_ref, o_ref):
        o_ref[...] = _dot(h_ref[...], w_ref[...])

    return _pcall(body, name=name, grid=(ns, T // tm),
                  in_specs=[pl.BlockSpec((tm, D), lambda s, i: (i, 0)), pl.BlockSpec((None, D, nc), lambda s, i: (s, 0, 0))],
                  out_specs=pl.BlockSpec((tm, nc), lambda s, i: (i, s)),
                  out_shape=S_((T, ns * nc), F32), args=(hb, wg))


def _ffn_conv(aext, wdw_ref, bdw_ref, tm, ffn_taps):
    acc = bdw_ref[...]
    for k in range(ffn_taps):
        acc = acc + wdw_ref[k:k + 1, :] * aext[pl.ds(HALO_FFN - (ffn_taps - 1) + k, tm), :]
    return acc


def _by_sublane_shift(ext, shifted, tm, offsets, per_chunk):
    first = True
    for b in range(8):
        ks = [(k, o - b) for k, o in enumerate(offsets) if o % 8 == b]
        if not ks:
            continue
        if b:
            shifted[...] = ext[pl.ds(b, shifted.shape[0]), :]
        _chunks(tm, functools.partial(per_chunk, src=shifted if b else ext, ks=ks, first=first))
        first = False


def _ffn_down(up, xin, wdw, bdw, wd, gn, tm, tps, ffn_taps, name, target=None):
    T, D = xin.shape
    F = up.shape[1] // 2
    hf = HALO_FFN
    final = target is not None

    def body(*refs):
        if final:
            (up_ref, uph_ref, x_ref, wdw_ref, bdw_ref, wd_ref, gn_ref, tg_ref,
             gb_ref, dx_ref, loss_ref, dgn_ref, aext) = refs
        else:
            (up_ref, uph_ref, x_ref, wdw_ref, bdw_ref, wd_ref, gn_ref,
             gb_ref, xo_ref, h_ref, aext) = refs
        i = pl.program_id(0)
        aext[0:hf, :] = jnp.where(i % tps == 0, 0.0, uph_ref[...])
        aext[hf:hf + tm, :] = up_ref[:, 0:F]
        acv = _ffn_conv(aext, wdw_ref, bdw_ref, tm, ffn_taps)
        gb = (acv * _sig(acv) * up_ref[:, F:2 * F]).astype(BF16)
        gb_ref[...] = gb.T
        xo = x_ref[...] + _dot(gb, wd_ref[...])
        if not final:
            xo_ref[...] = xo
            h_ref[...] = _rms(xo, gn_ref[...])[0]
        else:
            out, xh, r = _rms(xo, gn_ref[...])
            diff = out - tg_ref[...]
            part = 0.5 / D * jnp.sum(jnp.sum(diff * diff, axis=1, keepdims=True), axis=0, keepdims=True)
            dx, dg = _rms_bwd(diff * (1.0 / D), xh, r, gn_ref[...])
            dx_ref[...] = dx
            _accumulate(loss_ref, part, i == 0)
            _accumulate(dgn_ref, dg, i == 0)

    in_specs = [_rows(tm, 2 * F), _prev_halo(tm, hf, F), _rows(tm, D), _full(wdw.shape), _full((1, F)),
                _full((F, D)), _full((1, D))]
    args = [up, up, xin, wdw, bdw, wd, gn]
    if final:
        in_specs.append(_rows(tm, D))
        args.append(target)
        out_specs = [_cols(F, tm), _rows(tm, D), _acc_spec(1, 1), _acc_spec(1, D)]
        out_shape = [S_((F, T), BF16), S_((T, D), F32), S_((1, 1), F32), S_((1, D), F32)]
    else:
        out_specs = [_cols(F, tm), _rows(tm, D), _rows(tm, D)]
        out_shape = [S_((F, T), BF16), S_((T, D), F32), S_((T, D), F32)]
    return _pcall(body, name=name, grid=(T // tm,), in_specs=in_specs, out_specs=out_specs, out_shape=out_shape,
                  scratch=[pltpu.VMEM((tm + hf, F), F32)], args=args)


def _pool_counts(i, tps, tm, w):
    pos = (i % tps) * tm + lax.broadcasted_iota(jnp.int32, (tm, 1), 0)
    return jnp.minimum(pos + 1, w).astype(F32)


def _window_sums(src, buf_a, buf_b, cg, causal):
    assert POOL_WINDOWS == (2, 4, 8, 16)
    n = src.shape[0]
    levels = len(POOL_WINDOWS)
    cur, outs = src, []
    for l in range(levels):
        dst = buf_b if l % 2 else buf_a
        cols = slice(l * cg, levels * cg)
        lo, m = 8 * (l + 1), n - 8 * (l + 1)
        if causal:
            dst[lo:n, cols] = cur[lo:n, cols] + cur[pl.ds(lo - (1 << l), m), cols]
        else:
            dst[0:m, cols] = cur[0:m, cols] + cur[pl.ds(1 << l, m), cols]
        outs.append(dst)
        cur = dst
    return outs


def _pool_fwd(h, x2, pw, pb, ps, gn, tm, tps):
    T, D = h.shape
    ng, cg, _ = pw.shape
    hp = HALO_POOL

    def body(h_ref, hh_ref, x_ref, pw_ref, pb_ref, ps_ref, gn_ref, pl_ref, xo_ref, hb_ref, hbt_ref, hext, buf_a, buf_b):
        i = pl.program_id(0)
        hext[0:hp, :] = jnp.where(i % tps == 0, 0.0, hh_ref[...])
        hext[hp:hp + tm, :] = h_ref[...]
        sums = _window_sums(hext, buf_a, buf_b, cg, True)
        for g, w in enumerate(POOL_WINDOWS):
            cs = slice(g * cg, (g + 1) * cg)
            pl_ref[:, cs] = (sums[g][hp:hp + tm, cs] / _pool_counts(i, tps, tm, w) - h_ref[:, cs]).astype(BF16)

        pv = pl_ref[...]
        mixed = jnp.concatenate([_dot(pv[:, g * cg:(g + 1) * cg], pw_ref[g]) for g in range(ng)], axis=1)
        xo = x_ref[...] + ps_ref[...] * (mixed + pb_ref[...])
        xo_ref[...] = xo
        hb = _rms(xo, gn_ref[...])[0].astype(BF16)
        hb_ref[...] = hb
        hbt_ref[...] = hb.T

    return _pcall(body, name="pool_fwd", grid=(T // tm,),
                  in_specs=[_rows(tm, D), _prev_halo(tm, hp, D), _rows(tm, D), _full(pw.shape), _full((1, D)), _full((1, D)),
                            _full((1, D))],
                  out_specs=[_rows(tm, D), _rows(tm, D), _rows(tm, D), _cols(D, tm)],
                  out_shape=[S_((T, D), BF16), S_((T, D), F32), S_((T, D), BF16), S_((D, T), BF16)],
                  scratch=[pltpu.VMEM((tm + hp, D), F32)] * 3, args=(h, h, x2, pw, pb, ps, gn))


def _bwd_down(dx, up, wdw, bdw, wd, tm, tps, ffn_taps, name):
    T, D = dx.shape
    F = up.shape[1] // 2
    hf = HALO_FFN
    nt = T // tm

    def body(dx_ref, up_ref, uph_ref, wdw_ref, bdw_ref, wd_ref, dac_ref, dgate_ref, wsum_ref, aext):
        i = pl.program_id(0)
        aext[0:hf, :] = jnp.where(i % tps == 0, 0.0, uph_ref[...])
        aext[hf:hf + tm, :] = up_ref[:, 0:F]
        acv = _ffn_conv(aext, wdw_ref, bdw_ref, tm, ffn_taps)
        dg = _dot_nt(dx_ref[...].astype(BF16), wd_ref[...])
        sg = _sig(acv)
        dgate_ref[...] = (dg * acv * sg).astype(BF16)
        dac = dg * up_ref[:, F:2 * F] * (sg * (1.0 + acv * (1.0 - sg)))
        dac_ref[...] = dac
        rows = [_colsum(dac * aext[pl.ds(hf - (ffn_taps - 1) + k, tm), :]) for k in range(ffn_taps)]
        rows += [_colsum(dac), jnp.zeros((8 - ffn_taps - 1, F), F32)]
        _accumulate(wsum_ref, jnp.concatenate(rows, axis=0), i == 0)

    return _pcall(body, name=name, grid=(nt,),
                  in_specs=[_rows(tm, D), _rows(tm, 2 * F), _prev_halo(tm, hf, F), _full(wdw.shape), _full((1, F)),
                            _full((F, D))],
                  out_specs=[_rows(tm, F), _rows(tm, F, col=1), _acc_spec(8, F)],
                  out_shape=[S_((T, F), F32), S_((T, 2 * F), BF16), S_((8, F), F32)],
                  scratch=[pltpu.VMEM((tm + hf, F), F32)], args=(dx, up, up, wdw, bdw, wd))


def _bwd_up(dac, dup, wg, wdw, xin, gn, dx, tm, tps, ffn_taps, name):
    T, D = xin.shape
    F = dac.shape[1]
    ns, _, nc = wg.shape
    hf = HALO_FFN
    nt = T // tm

    def body(dac_ref, dach_ref, dgate_ref, wg_ref, wdw_ref, x_ref, gn_ref, dx_ref, dact_ref, dxo_ref, dgn_ref, dext):
        i = pl.program_id(0)
        dext[0:tm, :] = dac_ref[...]
        dext[tm:tm + hf, :] = jnp.where(i % tps == tps - 1, 0.0, dach_ref[...])
        dact = wdw_ref[ffn_taps - 1:ffn_taps, :] * dac_ref[...]
        for k in range(ffn_taps - 1):
            dact = dact + wdw_ref[k:k + 1, :] * dext[pl.ds(ffn_taps - 1 - k, tm), :]
        dact_ref[...] = dact.astype(BF16)

        dh = jnp.zeros((tm, D), F32)
        for s in range(ns):
            src = dact_ref if s < ns // 2 else dgate_ref
            o = (s % (ns // 2)) * nc
            dh = dh + _dot_nt(src[:, o:o + nc], wg_ref[s])
        _, xh, r = _rms(x_ref[...], gn_ref[...])
        dxn, dg = _rms_bwd(dh, xh, r, gn_ref[...])
        dxo_ref[...] = dx_ref[...] + dxn
        _accumulate(dgn_ref, dg, i == 0)

    return _pcall(body, name=name, grid=(nt,),
                  in_specs=[_rows(tm, F), _next_halo(tm, hf, F, T), _rows(tm, F, col=1), _full(wg.shape), _full(wdw.shape),
                            _rows(tm, D), _full((1, D)), _rows(tm, D)],
                  out_specs=[_rows(tm, F), _rows(tm, D), _acc_spec(1, D)],
                  out_shape=[S_((T, 2 * F), BF16), S_((T, D), F32), S_((1, D), F32)],
                  scratch=[pltpu.VMEM((tm + hf, F), F32)],
                  aliases={2: 0}, args=(dac, dac, dup, wg, wdw, xin, gn, dx))


def _mm_tn(xt, dy, bk, bn, tt, shard_major, name):
    K, T = xt.shape
    N = dy.shape[1]
    nt = T // tt

    def body(x_ref, dy_ref, o_ref, acc):
        t = pl.program_id(2)
        part = _dot(x_ref[...], dy_ref[...].astype(BF16))
        _accumulate(acc, part, t == 0)

        @pl.when(t == nt - 1)
        def _():
            o_ref[...] = acc[...].astype(o_ref.dtype)

    if shard_major:
        out_spec = pl.BlockSpec((None, bk, bn), lambda i, j, t: (j, i, 0))
        out_shape = S_((N // bn, K, bn), BF16)
    else:
        out_spec = pl.BlockSpec((bk, bn), lambda i, j, t: (i, j))
        out_shape = S_((K, N), BF16)
    return _pcall(body, name=name, grid=(K // bk, N // bn, nt),
                  in_specs=[pl.BlockSpec((bk, tt), lambda i, j, t: (i, t)), pl.BlockSpec((tt, bn), lambda i, j, t: (t, j))],
                  out_specs=out_spec, out_shape=out_shape, scratch=[pltpu.VMEM((bk, bn), F32)], args=(xt, dy))


def _pool_bwd(dx3, plb, x2, pw, pb, ps, gn, tm, tps):
    T, D = x2.shape
    ng, cg, _ = pw.shape
    hp = HALO_POOL
    nt = T // tm

    def body(do_ref, doh_ref, pl_ref, x_ref, pw_ref, pb_ref, ps_ref, gn_ref, dxo_ref, dpw_ref, sm_ref, qext, buf_a, buf_b,
             dh_s):
        i = pl.program_id(0)
        do = do_ref[...]
        dm = do * ps_ref[...]
        dmh = jnp.where(i % tps == tps - 1, 0.0, doh_ref[...]) * ps_ref[...]
        pv = pl_ref[...]
        mixed = jnp.concatenate([_dot(pv[:, g * cg:(g + 1) * cg], pw_ref[g]) for g in range(ng)], axis=1)
        dscale = _colsum(do * (mixed + pb_ref[...]))
        dbias = _colsum(dm)
        dmb = dm.astype(BF16)
        dmhb = dmh.astype(BF16)
        for g, w in enumerate(POOL_WINDOWS):
            cs = slice(g * cg, (g + 1) * cg)
            _accumulate(dpw_ref.at[g], _dot_tn(pv[:, cs], dmb[:, cs]), i == 0)
            dpo = _dot_nt(dmb[:, cs], pw_ref[g])
            dh_s[:, cs] = dpo
            qext[0:tm, cs] = dpo / _pool_counts(i, tps, tm, w)
            qext[tm:tm + hp, cs] = _dot_nt(dmhb[:, cs], pw_ref[g]) * (1.0 / w)
        sums = _window_sums(qext, buf_a, buf_b, cg, False)
        for g in range(ng):
            cs = slice(g * cg, (g + 1) * cg)
            dh_s[:, cs] = sums[g][0:tm, cs] - dh_s[:, cs]

        _, xh, r = _rms(x_ref[...], gn_ref[...])
        dxn, dg = _rms_bwd(dh_s[...], xh, r, gn_ref[...])
        dxo_ref[...] = do + dxn
        sm = jnp.concatenate([dscale, dbias, dg, jnp.zeros((5, D), F32)], axis=0)
        _accumulate(sm_ref, sm, i == 0)

    return _pcall(body, name="pool_bwd", grid=(nt,),
                  in_specs=[_rows(tm, D), _next_halo(tm, hp, D, T), _rows(tm, D), _rows(tm, D), _full(pw.shape), _full((1, D)),
                            _full((1, D)), _full((1, D))],
                  out_specs=[_rows(tm, D), pl.BlockSpec((ng, cg, cg), lambda i: (0, 0, 0)), _acc_spec(8, D)],
                  out_shape=[S_((T, D), F32), S_((ng, cg, cg), F32), S_((8, D), F32)],
                  scratch=[pltpu.VMEM((tm + hp, D), F32)] * 3 + [pltpu.VMEM((tm, D), F32)],
                  args=(dx3, dx3, plb, x2, pw, pb, ps, gn))


def _conv_bwd1(dx1, c, w2, lng, lnb, tm):
    T, D = dx1.shape

    def body(do_ref, c_ref, w2_ref, lng_ref, lnb_ref, dc_ref, sm_ref):
        i = pl.program_id(0)
        do = do_ref[...]
        ds = _dot_nt(do.astype(BF16), w2_ref[...])
        cv = c_ref[...]
        xc = cv - jnp.mean(cv, axis=-1, keepdims=True)
        rstd = lax.rsqrt(jnp.mean(xc * xc, axis=-1, keepdims=True) + LN_EPS)
        nh = xc * rstd
        n = nh * lng_ref[...] + lnb_ref[...]
        sg = _sig(n)
        dn = ds * (sg * (1.0 + n * (1.0 - sg)))
        dnh = dn * lng_ref[...]
        dc = rstd * (dnh - jnp.mean(dnh, axis=-1, keepdims=True) - nh * jnp.mean(dnh * nh, axis=-1, keepdims=True))
        dc_ref[...] = dc
        sm = jnp.concatenate([_colsum(dn * nh), _colsum(dn), _colsum(do), _colsum(dc), jnp.zeros((4, D), F32)], axis=0)
        _accumulate(sm_ref, sm, i == 0)

    return _pcall(body, name="conv_bwd1", grid=(T // tm,),
                  in_specs=[_rows(tm, D), _rows(tm, D), _full((D, D)), _full((1, D)), _full((1, D))],
                  out_specs=[_rows(tm, D), _acc_spec(8, D)],
                  out_shape=[S_((T, D), F32), S_((8, D), F32)], args=(dx1, c, w2, lng, lnb))


def _conv_bwd2(dc, a, x, w1g, wdw, gn, dx1, tm, tps, taps):
    T, D = x.shape
    ns, _, nc = w1g.shape
    hc = HALO_CONV
    nt = T // tm
    tp = wdw.shape[0]

    def body(dc_ref, dch_ref, a_ref, ah_ref, x_ref, w_ref, wdw_ref, gn_ref, dx1_ref,
             gx_ref, dab_ref, dw_ref, db1_ref, dgn_ref, uext, dext, shifted, du_s, wacc):
        i = pl.program_id(0)
        ah = ah_ref[...]
        uext[0:hc, :] = jnp.where(i % tps == 0, 0.0, ah[:, :D] * _sig(ah[:, D:]))
        av = a_ref[...]
        sg2 = _sig(av[:, D:])
        uext[hc:hc + tm, :] = av[:, :D] * sg2
        dext[0:tm, :] = dc_ref[...]
        dext[tm:tm + hc, :] = jnp.where(i % tps == tps - 1, 0.0, dch_ref[...])

        @pl.when(i == 0)
        def _():
            wacc[...] = jnp.zeros_like(wacc)

        def wgrad(r0, src, ks, first):
            d = dc_ref[pl.ds(r0, CHUNK), :]
            for k, o in ks:
                wacc[8 * k:8 * k + 8, :] += _fold8(d * src[pl.ds(pl.multiple_of(r0 + o, 8), CHUNK), :])
        _by_sublane_shift(uext, shifted, tm, [hc - (taps - 1) + k for k in range(taps)], wgrad)

        def convt(r0, src, ks, first):
            acc = jnp.zeros((CHUNK, D), F32) if first else du_s[pl.ds(r0, CHUNK), :]
            for k, o in ks:
                acc = acc + wdw_ref[k:k + 1, :] * src[pl.ds(pl.multiple_of(r0 + o, 8), CHUNK), :]
            du_s[pl.ds(r0, CHUNK), :] = acc
        _by_sublane_shift(dext, shifted, tm, [taps - 1 - k for k in range(taps)], convt)

        du = du_s[...]
        da1 = du * sg2
        da2 = du * av[:, :D] * (sg2 * (1.0 - sg2))
        dab_ref[:, 0:D] = da1.astype(BF16)
        dab_ref[:, D:2 * D] = da2.astype(BF16)
        _accumulate(db1_ref, jnp.concatenate([_colsum(da1), _colsum(da2)], axis=0), i == 0)
        dh = jnp.zeros((tm, D), F32)
        for s in range(ns):
            dh = dh + _dot_nt(dab_ref[:, s * nc:(s + 1) * nc], w_ref[s])
        _, xh, r = _rms(x_ref[...], gn_ref[...])
        dxn, dg = _rms_bwd(dh, xh, r, gn_ref[...])
        gx_ref[...] = dx1_ref[...] + dxn
        _accumulate(dgn_ref, dg, i == 0)

        @pl.when(i == nt - 1)
        def _():
            dw_ref[...] = jnp.zeros_like(dw_ref)
            for k in range(taps):
                dw_ref[k:k + 1, :] = _colsum(wacc[8 * k:8 * k + 8, :])

    return _pcall(body, name="conv_bwd2", grid=(nt,),
                  in_specs=[_rows(tm, D), _next_halo(tm, hc, D, T), _rows(tm, 2 * D), _prev_halo(tm, hc, 2 * D), _rows(tm, D),
                            _full(w1g.shape), _full(wdw.shape), _full((1, D)), _rows(tm, D)],
                  out_specs=[_rows(tm, D), _rows(tm, 2 * D), _acc_spec(tp, D), _acc_spec(2, D), _acc_spec(1, D)],
                  out_shape=[S_((T, D), F32), S_((T, 2 * D), BF16), S_((tp, D), F32), S_((2, D), F32), S_((1, D), F32)],
                  scratch=[pltpu.VMEM((tm + hc, D), F32), pltpu.VMEM((tm + hc, D), F32), pltpu.VMEM((tm + hc - 8, D), F32),
                           pltpu.VMEM((tm, D), F32), pltpu.VMEM((8 * taps, D), F32)],
                  args=(dc, dc, a, a, x, w1g, wdw, gn, dx1))


def _place():
    x, y, c = lax.axis_index("x"), lax.axis_index("y"), lax.axis_index("c")
    chips = [(1 - x, y), (x, 1 - y), (1 - x, 1 - y)]
    return x, y, c, chips


def _own_slot(w, layer, ids, dtype, name):
    _, r, c = w.shape
    br = _row_block(r, c)

    def body(ids_ref, w_ref, o_ref):
        o_ref[...] = w_ref[...].astype(dtype)

    return _pcall(body, name=name, grid=(r // br,), prefetch=1,
                  in_specs=[pl.BlockSpec((None, br, c), lambda i, ids: (layer, i, 0))],
                  out_specs=pl.BlockSpec((None, br, c), lambda i, ids: (ids[0], i, 0)),
                  out_shape=S_((N_CHIPS, r, c), dtype), args=(ids, w))


def _gather(slots):
    n = len(slots)

    def body(*refs):
        outs = refs[n:2 * n]
        s_ici, r_ici, s_d2d, r_d2d = refs[2 * n:]
        x, y, c, chips = _place()
        q = 2 * x + y

        def half(i, h):
            rh = slots[i].shape[1] // 2
            return pl.ds(pl.multiple_of(h * rh, 8), rh)

        def ici(i, j, chip, to):
            blk = outs[i].at[chip, half(i, c)]
            return pltpu.make_async_remote_copy(src_ref=blk, dst_ref=blk, send_sem=s_ici.at[i, j], recv_sem=r_ici.at[i, j],
                                                device_id=to, device_id_type=MESH)

        def d2d(i, j, chip, h):
            blk = outs[i].at[chip, half(i, h)]
            return pltpu.make_async_remote_copy(src_ref=blk, dst_ref=blk, send_sem=s_d2d.at[i, j], recv_sem=r_d2d.at[i, j],
                                                device_id=(x, y, 1 - c), device_id_type=MESH)

        first = [ici(i, j, q, (*chip, c)) for i in range(n) for j, chip in enumerate(chips)]
        for cp in first:
            cp.start()
        passed = []
        for i in range(n):
            for j, chip in enumerate(chips):
                qj = 2 * chip[0] + chip[1]
                ici(i, j, qj, (x, y, c)).wait_recv()
                fw = d2d(i, j, qj, c)
                fw.start()
                passed.append(fw)
        for i in range(n):
            for j, chip in enumerate(chips):
                d2d(i, j, 2 * chip[0] + chip[1], 1 - c).wait_recv()
        for cp in first + passed:
            cp.wait_send()

    any_spec = pl.BlockSpec(memory_space=pl.ANY)
    return pl.pallas_call(
        body, name="gather_weights", in_specs=[any_spec] * n, out_specs=[any_spec] * n,
        out_shape=[S_(s.shape, s.dtype) for s in slots], input_output_aliases={i: i for i in range(n)},
        scratch_shapes=[pltpu.SemaphoreType.DMA((n, 3))] * 4,
    )(*slots)


def _pair_exchange(parts):
    n = len(parts)

    def body(*refs):
        srcs, outs = refs[:n], refs[n:2 * n]
        s_sem, r_sem = refs[2 * n:]
        x, y, c, _ = _place()
        cps = []
        for i in range(n):
            for s in range(N_CHIPS):
                cp = pltpu.make_async_remote_copy(src_ref=srcs[i].at[s, 1 - c], dst_ref=outs[i].at[s], send_sem=s_sem.at[i, s],
                                                  recv_sem=r_sem.at[i, s], device_id=(x, y, 1 - c), device_id_type=MESH)
                cp.start()
                cps.append(cp)
        for cp in cps:
            cp.wait()

    any_spec = pl.BlockSpec(memory_space=pl.ANY)
    return pl.pallas_call(
        body, name="grad_pair_exchange", in_specs=[any_spec] * n, out_specs=[any_spec] * n,
        out_shape=[S_((N_CHIPS,) + p.shape[2:], p.dtype) for p in parts],
        scratch_shapes=[pltpu.SemaphoreType.DMA((n, N_CHIPS))] * 2,
    )(*parts)


def _chip_exchange(sums):
    n = len(sums)

    def body(*refs):
        srcs, outs = refs[:n], refs[n:2 * n]
        s_sem, r_sem = refs[2 * n:]
        x, y, c, chips = _place()
        cps = []
        for i in range(n):
            for j, chip in enumerate(chips):
                cp = pltpu.make_async_remote_copy(src_ref=srcs[i].at[2 * chip[0] + chip[1]], dst_ref=outs[i].at[j],
                                                  send_sem=s_sem.at[i, j], recv_sem=r_sem.at[i, j],
                                                  device_id=(*chip, c), device_id_type=MESH)
                cp.start()
                cps.append(cp)
        for cp in cps:
            cp.wait()

    any_spec = pl.BlockSpec(memory_space=pl.ANY)
    return pl.pallas_call(
        body, name="grad_chip_exchange", in_specs=[any_spec] * n, out_specs=[any_spec] * n,
        out_shape=[S_((3,) + s.shape[1:], s.dtype) for s in sums],
        scratch_shapes=[pltpu.SemaphoreType.DMA((n, 3))] * 2,
    )(*sums)


def _share_halves(bufs):
    n = len(bufs)
    nl = max(b.shape[0] for b in bufs)

    def body(*refs):
        outs = refs[n:2 * n]
        s_sem, r_sem = refs[2 * n:]
        x, y, c, _ = _place()
        cps = []
        for i in range(n):
            for l in range(bufs[i].shape[0]):
                blk = outs[i].at[l, c]
                cp = pltpu.make_async_remote_copy(src_ref=blk, dst_ref=blk, send_sem=s_sem.at[i, l], recv_sem=r_sem.at[i, l],
                                                  device_id=(x, y, 1 - c), device_id_type=MESH)
                cp.start()
                cps.append(cp)
        for cp in cps:
            cp.wait()

    any_spec = pl.BlockSpec(memory_space=pl.ANY)
    return pl.pallas_call(
        body, name="grad_share_halves", in_specs=[any_spec] * n, out_specs=[any_spec] * n,
        out_shape=[S_(b.shape, b.dtype) for b in bufs], input_output_aliases={i: i for i in range(n)},
        scratch_shapes=[pltpu.SemaphoreType.DMA((n, nl))] * 2,
    )(*bufs)


def _small_allreduce(groups):
    n = len(groups)
    arrays = []
    for grp in groups:
        for p in grp:
            if not isinstance(p, int) and not any(p[0] is a for a in arrays):
                arrays.append(p[0])
    shapes = []
    for grp in groups:
        rows = sum(p if isinstance(p, int) else p[2] - p[1] for p in grp)
        width = next(p[0].shape[1] for p in grp if not isinstance(p, int))
        assert rows % 8 == 0
        shapes.append((rows, width))
    na = len(arrays)

    def body(*refs):
        a_refs, o_refs = refs[:na], refs[na:na + n]
        rest = refs[na + n:]
        v_refs, sibs, css, gots = rest[:n], rest[n:2 * n], rest[2 * n:3 * n], rest[3 * n:4 * n]
        s_sem, r_sem = rest[4 * n:]
        for i, grp in enumerate(groups):
            o = 0
            for p in grp:
                if isinstance(p, int):
                    v_refs[i][o:o + p, :] = jnp.zeros((p, shapes[i][1]), F32)
                    o += p
                else:
                    k = next(t for t, a in enumerate(arrays) if a is p[0])
                    v_refs[i][o:o + p[2] - p[1], :] = a_refs[k][p[1]:p[2], :]
                    o += p[2] - p[1]
        x, y, c, chips = _place()
        q = 2 * x + y
        to_sib = [pltpu.make_async_remote_copy(src_ref=v_refs[i], dst_ref=sibs[i], send_sem=s_sem.at[i, 3], recv_sem=r_sem.at[i, 3],
                                               device_id=(x, y, 1 - c), device_id_type=MESH) for i in range(n)]
        for cp in to_sib:
            cp.start()
        cps = []
        for i in range(n):
            to_sib[i].wait()
            mine, other = v_refs[i][...], sibs[i][...]
            css[i][...] = jnp.where(c == 0, mine, other) + jnp.where(c == 0, other, mine)
            for j, chip in enumerate(chips):
                cp = pltpu.make_async_remote_copy(src_ref=css[i], dst_ref=gots[i].at[j], send_sem=s_sem.at[i, j],
                                                  recv_sem=r_sem.at[i, j], device_id=(*chip, c), device_id_type=MESH)
                cp.start()
                cps.append(cp)
        for cp in cps:
            cp.wait()
        flips = [2, 1, 3]
        for i in range(n):
            total = None
            for k in range(N_CHIPS):
                d = q ^ k
                term = jnp.where(d == 0, css[i][...], 0.0)
                for j in range(3):
                    term = jnp.where(d == flips[j], gots[i][j], term)
                total = term if total is None else total + term
            o_refs[i][...] = total

    vm = pl.BlockSpec(memory_space=pltpu.VMEM)
    return pl.pallas_call(
        body, name="small_allreduce", in_specs=[vm] * na, out_specs=[vm] * n, out_shape=[S_(s, F32) for s in shapes],
        scratch_shapes=[pltpu.VMEM(s, F32) for s in shapes] * 3 + [pltpu.VMEM((3,) + s, F32) for s in shapes]
        + [pltpu.SemaphoreType.DMA((n, 4)), pltpu.SemaphoreType.DMA((n, 4))],
    )(*arrays)


def _row_block(r, c, target_bytes=1 << 20):
    br = r
    while br % 32 == 0 and br * c * 4 > target_bytes:
        br //= 2
    return br


def _chip_sum(part, sib, ids, name):
    _, _, r, c = part.shape
    br = _row_block(r, c)

    def body(ids_ref, p_ref, s_ref, o_ref):
        o_ref[...] = (p_ref[...].astype(F32) + s_ref[...].astype(F32)).astype(BF16)

    return _pcall(body, name=name, grid=(N_CHIPS, r // br), prefetch=1,
                  in_specs=[pl.BlockSpec((None, None, br, c), lambda s, i, ids: (s, ids[1], i, 0)),
                            pl.BlockSpec((None, br, c), lambda s, i, ids: (s, i, 0))],
                  out_specs=pl.BlockSpec((None, br, c), lambda s, i, ids: (s, i, 0)),
                  out_shape=S_((N_CHIPS, r, c), BF16), args=(ids, part, sib))


def _final_sum(part, sib, got, ids, name, layer=0, n_layers=1, buf=None):
    _, r, c = got.shape
    br = _row_block(r, c)

    def body(ids_ref, p_ref, s_ref, got_ref, *rest):
        v = p_ref[...].astype(F32) + s_ref[...].astype(F32)
        for j in range(3):
            v = v + got_ref[j].astype(F32)
        rest[-1][...] = v

    in_specs = [pl.BlockSpec((None, None, br, c), lambda i, ids: (ids[0], ids[1], i, 0)),
                pl.BlockSpec((None, br, c), lambda i, ids: (ids[0], i, 0)),
                pl.BlockSpec((3, br, c), lambda i, ids: (0, i, 0))]
    args = [ids, part, sib, got]
    if buf is not None:
        in_specs.append(pl.BlockSpec(memory_space=pl.ANY))
        args.append(buf)
    return _pcall(body, name=name, grid=(r // br,), prefetch=1, in_specs=in_specs,
                  out_specs=pl.BlockSpec((None, None, br, c), lambda i, ids: (layer, ids[1], i, 0)),
                  out_shape=S_((n_layers, 2, r, c), F32), args=args, aliases={4: 0} if buf is not None else None)


def _adam_math(w, g, m, v):
    c1 = 1.0 / (1.0 - ADAM_B1 ** ADAM_STEP)
    c2 = 1.0 / (1.0 - ADAM_B2 ** ADAM_STEP)
    mn = ADAM_B1 * m + (1.0 - ADAM_B1) * g
    vn = ADAM_B2 * v + (1.0 - ADAM_B2) * (g * g)
    return -ADAM_LR * ((mn * c1) / (jnp.sqrt(vn * c2) + ADAM_EPS) + ADAM_WD * w), mn, vn


def _adamw(w, g, m, v, name):
    nl, r, c = w.shape
    br = _row_block(r, c, 1 << 19)

    def body(w_ref, g_ref, m_ref, v_ref, d_ref, mo_ref, vo_ref):
        d_ref[...], mo_ref[...], vo_ref[...] = _adam_math(w_ref[...], g_ref[...], m_ref[...], v_ref[...])

    blk = pl.BlockSpec((None, br, c), lambda l, i: (l, i, 0))
    return _pcall(body, name=name, grid=(nl, r // br), in_specs=[blk] * 4, out_specs=[blk] * 3,
                  out_shape=[S_((nl, r, c), F32)] * 3, args=(w, g, m, v))


def _adamw_small(rd, rf, sharded_g, ws, ms, vs, rows_d, rows_f):
    n = len(ws)
    ns = len(sharded_g)
    rep = [k for k in range(n) if rows_d[k] is not None or rows_f[k] is not None]

    def body(*refs):
        rd_ref, rf_ref = refs[0], refs[1]
        sg = refs[2:2 + ns]
        w_refs, m_refs, v_refs = (refs[2 + ns + t * n:2 + ns + (t + 1) * n] for t in range(3))
        outs = refs[2 + ns + 3 * n:]
        g_out, d_out, m_out, v_out = outs[:len(rep)], outs[len(rep):len(rep) + n], outs[len(rep) + n:len(rep) + 2 * n], \
            outs[len(rep) + 2 * n:]
        si = 0
        for k in range(n):
            shape = w_refs[k].shape
            if rows_d[k] is not None or rows_f[k] is not None:
                src, rws = (rd_ref, rows_d[k]) if rows_d[k] is not None else (rf_ref, rows_f[k])
                axis = 0 if shape[0] == len(rws) else 1
                g = src[rws[0]:rws[0] + 1, :]
                if len(rws) > 1:
                    g = jnp.concatenate([src[r:r + 1, :] for r in rws], axis=axis)
                g_out[rep.index(k)][...] = g
            else:
                g = sg[si][...]
                si += 1
            d_out[k][...], m_out[k][...], v_out[k][...] = _adam_math(w_refs[k][...], g, m_refs[k][...], v_refs[k][...])

    vm = pl.BlockSpec(memory_space=pltpu.VMEM)
    shapes = [S_(w.shape, F32) for w in ws]
    outs = pl.pallas_call(
        body, name="adamw_small", in_specs=[vm] * (2 + ns + 3 * n), out_specs=[vm] * (len(rep) + 3 * n),
        out_shape=[shapes[k] for k in rep] + shapes * 3,
    )(rd, rf, *sharded_g, *ws, *ms, *vs)
    g_rep = dict(zip(rep, outs[:len(rep)]))
    o = outs[len(rep):]
    return g_rep, o[:n], o[n:2 * n], o[2 * n:]


def _pack(pieces, rows):
    flat = jnp.concatenate([p.reshape(-1).astype(F32) for p in pieces])
    return jnp.pad(flat, (0, rows * 128 - flat.shape[0])).reshape(rows, 128)


def _unpack(packed, shapes):
    flat = packed.reshape(-1)
    out, o = [], 0
    for s in shapes:
        n = 1
        for d in s:
            n *= d
        out.append(flat[o:o + n].reshape(s))
        o += n
    return out


def _rows_for(pieces_or_shapes):
    n = 0
    for p in pieces_or_shapes:
        k = 1
        for d in (p if isinstance(p, tuple) else p.shape):
            k *= d
        n += k
    return -(-n // 1024) * 8


def kernel(x, norm_mix, norm_ffn, conv_w_pw1, conv_b_pw1, conv_w_dw, conv_b_dw, conv_ln_g, conv_ln_b, conv_w_pw2, conv_b_pw2, pool_w, pool_b, pool_scale, ffn_w_up, ffn_w_dw, ffn_b_dw, ffn_w_down, final_norm, loss_target, m_norm_mix, m_norm_ffn, m_conv_w_pw1, m_conv_b_pw1, m_conv_w_dw, m_conv_b_dw, m_conv_ln_g, m_conv_ln_b, m_conv_w_pw2, m_conv_b_pw2, m_pool_w, m_pool_b, m_pool_scale, m_ffn_w_up, m_ffn_w_dw, m_ffn_b_dw, m_ffn_w_down, m_final_norm, v_norm_mix, v_norm_ffn, v_conv_w_pw1, v_conv_b_pw1, v_conv_w_dw, v_conv_b_dw, v_conv_ln_g, v_conv_ln_b, v_conv_w_pw2, v_conv_b_pw2, v_pool_w, v_pool_b, v_pool_scale, v_ffn_w_up, v_ffn_w_dw, v_ffn_b_dw, v_ffn_w_down, v_final_norm):
    nb, seq, D = x.shape
    T = nb * seq
    F = ffn_w_down.shape[1] * N_CHIPS
    taps = conv_w_dw.shape[1]
    ffn_taps = ffn_w_dw.shape[1]
    ng = pool_w.shape[1]
    cg = pool_w.shape[3]
    dsh = D // N_CHIPS
    fsh = F // N_CHIPS
    assert taps - 1 <= HALO_CONV and ffn_taps - 1 <= HALO_FFN and max(POOL_WINDOWS) <= HALO_POOL
    tm = min(TILE_D, seq)
    tf = min(TILE_F, seq)
    assert seq % tm == 0 and seq % tf == 0 and tm % HALO_CONV == 0 and tf % CHUNK == 0
    tps, tpf = seq // tm, seq // tf

    xi, yi, ci = lax.axis_index("x"), lax.axis_index("y"), lax.axis_index("c")
    qi = 2 * xi + yi
    ids = jnp.stack([qi, ci]).astype(jnp.int32)
    x2d = x.reshape(T, D)
    tg2d = loss_target.reshape(T, D)

    small_sharded = [conv_w_dw[0], ffn_w_dw, pool_b, pool_scale]
    srows = 2 * _rows_for(small_sharded)
    gathered = _gather([
        _own_slot(conv_w_pw1, 0, ids, BF16, "slot_pw1"), _own_slot(conv_w_pw2, 0, ids, BF16, "slot_pw2"),
        _own_slot(pool_w.reshape(1, ng * (cg // N_CHIPS), cg), 0, ids, BF16, "slot_pool"),
        _own_slot(ffn_w_up, 0, ids, BF16, "slot_up0"), _own_slot(ffn_w_up, 1, ids, BF16, "slot_up1"),
        _own_slot(ffn_w_down, 0, ids, BF16, "slot_down0"), _own_slot(ffn_w_down, 1, ids, BF16, "slot_down1"),
        _own_slot(_pack(small_sharded, srows)[None], 0, ids, F32, "slot_small")])
    w1g, w2g, pwg, wup0, wup1, wd0, wd1, smallg = gathered
    w2f = w2g.reshape(D, D)
    pwf = pwg.reshape(N_CHIPS, ng, cg // N_CHIPS, cg).transpose(1, 0, 2, 3).reshape(ng, cg, cg)
    wup = [wup0, wup1]
    wdn = [wd0.reshape(F, D), wd1.reshape(F, D)]
    sm_parts = [_unpack(smallg[s], [(taps, dsh), (2, ffn_taps, fsh), (1, dsh), (1, dsh)]) for s in range(N_CHIPS)]
    wdw_f = jnp.concatenate([p[0] for p in sm_parts], axis=1)
    wdw_f = jnp.pad(wdw_f, ((0, HALO_CONV - taps), (0, 0)))
    fdw_f = jnp.concatenate([p[1] for p in sm_parts], axis=2)
    fdw_f = jnp.pad(fdw_f, ((0, 0), (0, 8 - ffn_taps), (0, 0)))
    pb_f = jnp.concatenate([p[2] for p in sm_parts], axis=1)
    ps_f = jnp.concatenate([p[3] for p in sm_parts], axis=1)

    hb0t, a = _conv_in(x2d, norm_mix[0:1], w1g, conv_b_pw1, tm)
    c, sbt, x1, hb1, hb1t = _conv_mid(a, x2d, wdw_f, conv_b_dw, conv_ln_g, conv_ln_b, w2f, conv_b_pw2, norm_ffn[0:1], tm, tps,
                                      taps)
    up0 = _mm_up(hb1, wup[0], tm, "ffn_up0")
    gb0t, x2, h2 = _ffn_down(up0, x1, fdw_f[0], ffn_b_dw[0:1], wdn[0], norm_mix[1:2], tf, tpf, ffn_taps, "ffn_down0")
    plb, x3, hb3, hb3t = _pool_fwd(h2, x2, pwf, pb_f, ps_f, norm_ffn[1:2], tm, tps)
    up1 = _mm_up(hb3, wup[1], tm, "ffn_up1")
    gb1t, dx4, loss_part, d_final = _ffn_down(up1, x3, fdw_f[1], ffn_b_dw[1:2], wdn[1], final_norm.reshape(1, D), tf, tpf,
                                              ffn_taps, "ffn_down1", target=tg2d)

    dac1, dup1, fw1 = _bwd_down(dx4, up1, fdw_f[1], ffn_b_dw[1:2], wdn[1], tf, tpf, ffn_taps, "bwd_down1")
    p_down1 = _mm_tn(gb1t, dx4, F // 2, D, tm, False, "dw_down1")
    dup1, dx3, dnf1 = _bwd_up(dac1, dup1, wup[1], fdw_f[1], x3, norm_ffn[1:2], dx4, tf, tpf, ffn_taps, "bwd_up1")
    p_up1 = _mm_tn(hb3t, dup1, D, 2 * fsh, tm, True, "dw_up1")
    dx2, dpw, pool_sm = _pool_bwd(dx3, plb, x2, pwf, pb_f, ps_f, norm_mix[1:2], tm, tps)
    dac0, dup0, fw0 = _bwd_down(dx2, up0, fdw_f[0], ffn_b_dw[0:1], wdn[0], tf, tpf, ffn_taps, "bwd_down0")
    p_down0 = _mm_tn(gb0t, dx2, F // 2, D, tm, False, "dw_down0")
    dup0, dx1, dnf0 = _bwd_up(dac0, dup0, wup[0], fdw_f[0], x1, norm_ffn[0:1], dx2, tf, tpf, ffn_taps, "bwd_up0")
    p_up0 = _mm_tn(hb1t, dup0, D, 2 * fsh, tm, True, "dw_up0")
    dc, conv_sm = _conv_bwd1(dx1, c, w2f, conv_ln_g, conv_ln_b, tm)
    p_pw2 = _mm_tn(sbt, dx1, D, D, tm, False, "dw_pw2")
    grad_x, dab, dwdw, db1, dnm0 = _conv_bwd2(dc, a, x2d, w1g, wdw_f, norm_mix[0:1], dx1, tm, tps, taps)
    p_pw1 = _mm_tn(hb0t, dab, D, 2 * dsh, tm, True, "dw_pw1")

    csh = cg // N_CHIPS
    p_pool = dpw.reshape(2, ng // 2, N_CHIPS, csh, cg).transpose(2, 0, 1, 3, 4).reshape(N_CHIPS, 2, (ng // 2) * csh, cg)
    parts = [p_pw1.reshape(N_CHIPS, 2, D // 2, 2 * dsh), p_pw2.reshape(N_CHIPS, 2, dsh // 2, D), p_pool.astype(BF16),
             p_up0.reshape(N_CHIPS, 2, D // 2, 2 * fsh), p_up1.reshape(N_CHIPS, 2, D // 2, 2 * fsh),
             p_down0.reshape(N_CHIPS, 2, fsh // 2, D), p_down1.reshape(N_CHIPS, 2, fsh // 2, D)]
    names = ["pw1", "pw2", "pool", "up0", "up1", "down0", "down1"]
    from_sib = _pair_exchange(parts)
    sums = [_chip_sum(p, s, ids, "chip_sum_" + nm) for p, s, nm in zip(parts, from_sib, names)]
    got = _chip_exchange(sums)
    fs = {nm: (p, s, g) for p, s, g, nm in zip(parts, from_sib, got, names)}
    bufs = [_final_sum(*fs[nm], ids, "final_sum_" + nm) for nm in ("pw1", "pw2", "pool")]
    for nm in ("up", "down"):
        b = _final_sum(*fs[nm + "0"], ids, "final_sum_" + nm + "0", layer=0, n_layers=2)
        bufs.append(_final_sum(*fs[nm + "1"], ids, "final_sum_" + nm + "1", layer=1, n_layers=2, buf=b))
    g_pw1, g_pw2, g_pool, g_up, g_down = [b.reshape(b.shape[0], 2 * b.shape[2], b.shape[3]) for b in _share_halves(bufs)]

    rd, rf = _small_allreduce([
        [(dnm0, 0, 1), (pool_sm, 2, 3), (dnf0, 0, 1), (dnf1, 0, 1), (db1, 0, 2), (conv_sm, 0, 4), (d_final, 0, 1),
         (pool_sm, 0, 2), 3, (dwdw, 0, dwdw.shape[0])],
        [(fw0, 0, 4), (fw1, 0, 4)]])
    g_wdw = lax.dynamic_slice_in_dim(rd[16:16 + taps], qi * dsh, dsh, axis=1)
    g_ps = lax.dynamic_slice_in_dim(rd[11:12], qi * dsh, dsh, axis=1)
    g_pb = lax.dynamic_slice_in_dim(rd[12:13], qi * dsh, dsh, axis=1)
    g_fdw = lax.dynamic_slice_in_dim(rf.reshape(2, 4, F)[:, :ffn_taps], qi * fsh, fsh, axis=2).reshape(2 * ffn_taps, fsh)

    def big(w, g, m, v, nm):
        d, mn, vn = _adamw(w.reshape(g.shape), g, m.reshape(g.shape), v.reshape(g.shape), "adamw_" + nm)
        return g.reshape(w.shape), d.reshape(w.shape), mn.reshape(w.shape), vn.reshape(w.shape)

    o_pw1 = big(conv_w_pw1, g_pw1, m_conv_w_pw1, v_conv_w_pw1, "pw1")
    o_pw2 = big(conv_w_pw2, g_pw2, m_conv_w_pw2, v_conv_w_pw2, "pw2")
    o_pool = big(pool_w, g_pool, m_pool_w, v_pool_w, "pool")
    o_up = big(ffn_w_up, g_up, m_ffn_w_up, v_ffn_w_up, "up")
    o_down = big(ffn_w_down, g_down, m_ffn_w_down, v_ffn_w_down, "down")

    snames = ["norm_mix", "norm_ffn", "b_pw1", "w_dw", "b_dw", "ln_g", "ln_b", "b_pw2", "pool_b", "pool_scale", "ffn_w_dw",
              "ffn_b_dw", "final_norm"]
    sw = [norm_mix, norm_ffn, conv_b_pw1, conv_w_dw, conv_b_dw, conv_ln_g, conv_ln_b, conv_b_pw2, pool_b, pool_scale,
          ffn_w_dw, ffn_b_dw, final_norm]
    smm = [m_norm_mix, m_norm_ffn, m_conv_b_pw1, m_conv_w_dw, m_conv_b_dw, m_conv_ln_g, m_conv_ln_b, m_conv_b_pw2, m_pool_b,
           m_pool_scale, m_ffn_w_dw, m_ffn_b_dw, m_final_norm]
    svv = [v_norm_mix, v_norm_ffn, v_conv_b_pw1, v_conv_w_dw, v_conv_b_dw, v_conv_ln_g, v_conv_ln_b, v_conv_b_pw2, v_pool_b,
           v_pool_scale, v_ffn_w_dw, v_ffn_b_dw, v_final_norm]
    sshapes = [tuple(w.shape) for w in sw]
    two_d = [(2, D), (2, D), (1, 2 * D), (taps, dsh), (1, D), (1, D), (1, D), (1, D), (1, dsh), (1, dsh), (2 * ffn_taps, fsh),
             (2, F), (1, D)]
    rows_d = [[0, 1], [2, 3], [4, 5], None, [9], [6], [7], [8], None, None, None, None, [10]]
    rows_f = [None] * 11 + [[3, 7], None]
    as2d = lambda ts: [t.reshape(s) for t, s in zip(ts, two_d)]
    g_rep, sd, sm_new, sv_new = _adamw_small(rd, rf, [g_wdw, g_pb, g_ps, g_fdw], as2d(sw), as2d(smm), as2d(svv), rows_d, rows_f)
    sg = dict(g_rep)
    sg.update({3: g_wdw, 8: g_pb, 9: g_ps, 10: g_fdw})
    small_out = {n: tuple(t.reshape(sshapes[k]) for t in (sg[k], sd[k], sm_new[k], sv_new[k])) for k, n in enumerate(snames)}

    order = [small_out["norm_mix"], small_out["norm_ffn"], o_pw1, small_out["b_pw1"], small_out["w_dw"], small_out["b_dw"],
             small_out["ln_g"], small_out["ln_b"], o_pw2, small_out["b_pw2"], o_pool, small_out["pool_b"],
             small_out["pool_scale"], o_up, small_out["ffn_w_dw"], small_out["ffn_b_dw"], o_down, small_out["final_norm"]]
    loss = lax.psum(loss_part[0, 0], ("x", "y", "c"))
    return (loss, grad_x.reshape(nb, seq, D), *[o[0] for o in order], *[o[1] for o in order], *[o[2] for o in order],
            *[o[3] for o in order])
```

```python
import functools

import jax
import jax.numpy as jnp
from jax import lax
from jax.experimental import pallas as pl
from jax.experimental.pallas import tpu as pltpu

F32, BF16 = jnp.float32, jnp.bfloat16
S_ = jax.ShapeDtypeStruct
MESH = pl.DeviceIdType.MESH

RMS_EPS, LN_EPS = 1e-6, 1e-5
POOL_WINDOWS = (2, 4, 8, 16)
ADAM_LR, ADAM_B1, ADAM_B2, ADAM_EPS, ADAM_WD, ADAM_STEP = 0.001, 0.9, 0.999, 1e-08, 0.01, 10

VMEM_LIMIT_BYTES = 60 * 1024 * 1024
N_CHIPS = 4
CHUNK = 16
HALO_CONV = 32
HALO_POOL = 32
HALO_FFN = 8
TILE_D = 512
TILE_F = 256


def _pcall(body, *, name, grid, in_specs, out_specs, out_shape, args, scratch=(), aliases=None, prefetch=0, comm=None):
    params = pltpu.CompilerParams(dimension_semantics=("arbitrary",) * len(grid), vmem_limit_bytes=VMEM_LIMIT_BYTES)
    if comm is not None:
        assert not prefetch
        single = not isinstance(out_shape, (list, tuple))
        in_specs, args, scratch = list(in_specs), list(args), list(scratch)
        out_specs, out_shape = ([out_specs], [out_shape]) if single else (list(out_specs), list(out_shape))
        n_in, n_out, n_scr, n_cin, n_cout = len(in_specs), len(out_specs), len(scratch), len(comm.inputs), len(comm.out_shape)
        any_spec = pl.BlockSpec(memory_space=pl.ANY)
        aliases = dict(aliases or {})
        aliases.update({n_in + a: n_out + b for a, b in comm.aliases.items()})
        inner = body

        def body(*refs):
            ins, cin = refs[:n_in], refs[n_in:n_in + n_cin]
            outs = refs[n_in + n_cin:n_in + n_cin + n_out]
            cout = refs[n_in + n_cin + n_out:n_in + n_cin + n_out + n_cout]
            scr = refs[n_in + n_cin + n_out + n_cout:n_in + n_cin + n_out + n_cout + n_scr]
            sems = refs[n_in + n_cin + n_out + n_cout + n_scr:]
            step, total = 0, 1
            for d, g in enumerate(grid):
                step = step * g + pl.program_id(d)
                total *= g
            comm.before(cin, cout, sems, step, total)
            inner(*ins, *outs, *scr)
            comm.after(cin, cout, sems, step, total)

        in_specs += [any_spec] * n_cin
        args += list(comm.inputs)
        out_specs += [any_spec] * n_cout
        out_shape += list(comm.out_shape)
        scratch += list(comm.sems)
    if prefetch:
        spec = pltpu.PrefetchScalarGridSpec(num_scalar_prefetch=prefetch, grid=grid, in_specs=in_specs,
                                            out_specs=out_specs, scratch_shapes=list(scratch))
        return pl.pallas_call(body, name=name, grid_spec=spec, out_shape=out_shape,
                              input_output_aliases=aliases or {}, compiler_params=params)(*args)
    return pl.pallas_call(body, name=name, grid=grid, in_specs=in_specs, out_specs=out_specs, out_shape=out_shape,
                          scratch_shapes=list(scratch), input_output_aliases=aliases or {},
                          compiler_params=params)(*args)


def _full(shape):
    n = len(shape)
    return pl.BlockSpec(tuple(shape), lambda *_: (0,) * n, pipeline_mode=pl.Buffered(1))


def _rows(tm, c, col=0):
    return pl.BlockSpec((tm, c), lambda i, *_: (i, col))


def _cols(r, tm):
    return pl.BlockSpec((r, tm), lambda i, *_: (0, i))


def _prev_halo(tm, hb, c):
    return pl.BlockSpec((hb, c), lambda i, *_: (jnp.maximum(i * (tm // hb) - 1, 0), 0))


def _next_halo(tm, hb, c, total_rows):
    last = total_rows // hb - 1
    return pl.BlockSpec((hb, c), lambda i, *_: (jnp.minimum((i + 1) * (tm // hb), last), 0))


def _acc_spec(r, c):
    return pl.BlockSpec((r, c), lambda *_: (0, 0))


def _rms(x, g):
    r = lax.rsqrt(jnp.mean(x * x, axis=-1, keepdims=True) + RMS_EPS)
    xh = x * r
    return xh * g, xh, r


def _rms_bwd(dy, xh, r, g):
    dxh = dy * g
    dx = r * (dxh - xh * jnp.mean(dxh * xh, axis=-1, keepdims=True))
    return dx, jnp.sum(dy * xh, axis=0, keepdims=True)


def _colsum(v):
    return jnp.sum(v, axis=0, keepdims=True)


def _fold8(v):
    out = v[0:8]
    for j in range(1, v.shape[0] // 8):
        out = out + v[8 * j:8 * j + 8]
    return out


def _chunks(n_rows, fn):
    def step(j, carry):
        fn(pl.multiple_of(j * CHUNK, CHUNK))
        return carry
    lax.fori_loop(0, n_rows // CHUNK, step, 0)


def _dot(a, b):
    return jnp.dot(a, b, preferred_element_type=F32)


def _dot_nt(a, b):
    return lax.dot_general(a, b, (((1,), (1,)), ((), ())), preferred_element_type=F32)


def _dot_tn(a, b):
    return lax.dot_general(a, b, (((0,), (0,)), ((), ())), preferred_element_type=F32)


def _sig(v):
    return jax.nn.sigmoid(v)


def _accumulate(ref, val, first):
    @pl.when(first)
    def _():
        ref[...] = val

    @pl.when(jnp.logical_not(first))
    def _():
        ref[...] = ref[...] + val


def _conv_in(x, g, w1g, b1, tm):
    T, D = x.shape
    ns, _, nc = w1g.shape

    def body(x_ref, g_ref, w_ref, b_ref, hbt_ref, a_ref):
        hb = _rms(x_ref[...], g_ref[...])[0].astype(BF16)
        hbt_ref[...] = hb.T
        for s in range(ns):
            a_ref[:, s * nc:(s + 1) * nc] = _dot(hb, w_ref[s]) + b_ref[:, s * nc:(s + 1) * nc]

    return _pcall(body, name="conv_in", grid=(T // tm,),
                  in_specs=[_rows(tm, D), _full((1, D)), _full(w1g.shape), _full((1, ns * nc))],
                  out_specs=[_cols(D, tm), _rows(tm, ns * nc)],
                  out_shape=[S_((D, T), BF16), S_((T, ns * nc), F32)], args=(x, g, w1g, b1))


def _conv_mid(a, x, wdw, bdw, lng, lnb, w2, b2, gn, tm, tps, taps, comm=None):
    T, D = x.shape
    hc = HALO_CONV

    def body(a_ref, ah_ref, x_ref, wdw_ref, bdw_ref, lng_ref, lnb_ref, w2_ref, b2_ref, gn_ref,
             c_ref, sbt_ref, x1_ref, hb_ref, hbt_ref, uext, shifted):
        i = pl.program_id(0)
        ah = ah_ref[...]
        uext[0:hc, :] = jnp.where(i % tps == 0, 0.0, ah[:, :D] * _sig(ah[:, D:]))
        av = a_ref[...]
        uext[hc:hc + tm, :] = av[:, :D] * _sig(av[:, D:])

        def conv(r0, src, ks, first):
            acc = jnp.broadcast_to(bdw_ref[...], (CHUNK, D)) if first else c_ref[pl.ds(r0, CHUNK), :]
            for k, o in ks:
                acc = acc + wdw_ref[k:k + 1, :] * src[pl.ds(pl.multiple_of(r0 + o, 8), CHUNK), :]
            c_ref[pl.ds(r0, CHUNK), :] = acc
        _by_sublane_shift(uext, shifted, tm, [hc - (taps - 1) + k for k in range(taps)], conv)

        cv = c_ref[...]
        xc = cv - jnp.mean(cv, axis=-1, keepdims=True)
        nh = xc * lax.rsqrt(jnp.mean(xc * xc, axis=-1, keepdims=True) + LN_EPS)
        n = nh * lng_ref[...] + lnb_ref[...]
        sb = (n * _sig(n)).astype(BF16)
        sbt_ref[...] = sb.T
        x1 = x_ref[...] + _dot(sb, w2_ref[...]) + b2_ref[...]
        x1_ref[...] = x1
        hb = _rms(x1, gn_ref[...])[0].astype(BF16)
        hb_ref[...] = hb
        hbt_ref[...] = hb.T

    return _pcall(body, name="conv_mid", grid=(T // tm,),
                  in_specs=[_rows(tm, 2 * D), _prev_halo(tm, hc, 2 * D), _rows(tm, D), _full(wdw.shape), _full((1, D)),
                            _full((1, D)), _full((1, D)), _full((D, D)), _full((1, D)), _full((1, D))],
                  out_specs=[_rows(tm, D), _cols(D, tm), _rows(tm, D), _rows(tm, D), _cols(D, tm)],
                  out_shape=[S_((T, D), F32), S_((D, T), BF16), S_((T, D), F32), S_((T, D), BF16), S_((D, T), BF16)],
                  scratch=[pltpu.VMEM((tm + hc, D), F32), pltpu.VMEM((tm + hc - 8, D), F32)],
                  args=(a, a, x, wdw, bdw, lng, lnb, w2, b2, gn), comm=comm)


def _mm_up(hb, wg, tm, name, comm=None):
    T, D = hb.shape
    ns, _, nc = wg.shape

    def body(h_ref, w_ref, o_ref):
        o_ref[...] = _dot(h_ref[...], w_ref[...])

    return _pcall(body, name=name, grid=(ns, T // tm),
                  in_specs=[pl.BlockSpec((tm, D), lambda s, i: (i, 0)), pl.BlockSpec((None, D, nc), lambda s, i: (s, 0, 0))],
                  out_specs=pl.BlockSpec((tm, nc), lambda s, i: (i, s)),
                  out_shape=S_((T, ns * nc), F32), args=(hb, wg), comm=comm)


def _ffn_conv(aext, wdw_ref, bdw_ref, tm, ffn_taps):
    acc = bdw_ref[...]
    for k in range(ffn_taps):
        acc = acc + wdw_ref[k:k + 1, :] * aext[pl.ds(HALO_FFN - (ffn_taps - 1) + k, tm), :]
    return acc


def _by_sublane_shift(ext, shifted, tm, offsets, per_chunk):
    first = True
    for b in range(8):
        ks = [(k, o - b) for k, o in enumerate(offsets) if o % 8 == b]
        if not ks:
            continue
        if b:
            shifted[...] = ext[pl.ds(b, shifted.shape[0]), :]
        _chunks(tm, functools.partial(per_chunk, src=shifted if b else ext, ks=ks, first=first))
        first = False


def _ffn_down(up, xin, wdw, bdw, wd, gn, tm, tps, ffn_taps, name, target=None, comm=None):
    T, D = xin.shape
    F = up.shape[1] // 2
    hf = HALO_FFN
    final = target is not None

    def body(*refs):
        if final:
            (up_ref, uph_ref, x_ref, wdw_ref, bdw_ref, wd_ref, gn_ref, tg_ref,
             gb_ref, dx_ref, loss_ref, dgn_ref, aext) = refs
        else:
            (up_ref, uph_ref, x_ref, wdw_ref, bdw_ref, wd_ref, gn_ref,
             gb_ref, xo_ref, h_ref, aext) = refs
        i = pl.program_id(0)
        aext[0:hf, :] = jnp.where(i % tps == 0, 0.0, uph_ref[...])
        aext[hf:hf + tm, :] = up_ref[:, 0:F]
        acv = _ffn_conv(aext, wdw_ref, bdw_ref, tm, ffn_taps)
        gb = (acv * _sig(acv) * up_ref[:, F:2 * F]).astype(BF16)
        gb_ref[...] = gb.T
        xo = x_ref[...] + _dot(gb, wd_ref[...])
        if not final:
            xo_ref[...] = xo
            h_ref[...] = _rms(xo, gn_ref[...])[0]
        else:
            out, xh, r = _rms(xo, gn_ref[...])
            diff = out - tg_ref[...]
            part = 0.5 / D * jnp.sum(jnp.sum(diff * diff, axis=1, keepdims=True), axis=0, keepdims=True)
            dx, dg = _rms_bwd(diff * (1.0 / D), xh, r, gn_ref[...])
            dx_ref[...] = dx
            _accumulate(loss_ref, part, i == 0)
            _accumulate(dgn_ref, dg, i == 0)

    in_specs = [_rows(tm, 2 * F), _prev_halo(tm, hf, F), _rows(tm, D), _full(wdw.shape), _full((1, F)),
                _full((F, D)), _full((1, D))]
    args = [up, up, xin, wdw, bdw, wd, gn]
    if final:
        in_specs.append(_rows(tm, D))
        args.append(target)
        out_specs = [_cols(F, tm), _rows(tm, D), _acc_spec(1, 1), _acc_spec(1, D)]
        out_shape = [S_((F, T), BF16), S_((T, D), F32), S_((1, 1), F32), S_((1, D), F32)]
    else:
        out_specs = [_cols(F, tm), _rows(tm, D), _rows(tm, D)]
        out_shape = [S_((F, T), BF16), S_((T, D), F32), S_((T, D), F32)]
    return _pcall(body, name=name, grid=(T // tm,), in_specs=in_specs, out_specs=out_specs, out_shape=out_shape,
                  scratch=[pltpu.VMEM((tm + hf, F), F32)], args=args, comm=comm)


def _pool_counts(i, tps, tm, w):
    pos = (i % tps) * tm + lax.broadcasted_iota(jnp.int32, (tm, 1), 0)
    return jnp.minimum(pos + 1, w).astype(F32)


def _window_sums(src, buf_a, buf_b, cg, causal):
    assert POOL_WINDOWS == (2, 4, 8, 16)
    n = src.shape[0]
    levels = len(POOL_WINDOWS)
    cur, outs = src, []
    for l in range(levels):
        dst = buf_b if l % 2 else buf_a
        cols = slice(l * cg, levels * cg)
        lo, m = 8 * (l + 1), n - 8 * (l + 1)
        if causal:
            dst[lo:n, cols] = cur[lo:n, cols] + cur[pl.ds(lo - (1 << l), m), cols]
        else:
            dst[0:m, cols] = cur[0:m, cols] + cur[pl.ds(1 << l, m), cols]
        outs.append(dst)
        cur = dst
    return outs


def _pool_fwd(h, x2, pw, pb, ps, gn, tm, tps):
    T, D = h.shape
    ng, cg, _ = pw.shape
    hp = HALO_POOL

    def body(h_ref, hh_ref, x_ref, pw_ref, pb_ref, ps_ref, gn_ref, pl_ref, xo_ref, hb_ref, hbt_ref, hext, buf_a, buf_b):
        i = pl.program_id(0)
        hext[0:hp, :] = jnp.where(i % tps == 0, 0.0, hh_ref[...])
        hext[hp:hp + tm, :] = h_ref[...]
        sums = _window_sums(hext, buf_a, buf_b, cg, True)
        for g, w in enumerate(POOL_WINDOWS):
            cs = slice(g * cg, (g + 1) * cg)
            pl_ref[:, cs] = (sums[g][hp:hp + tm, cs] / _pool_counts(i, tps, tm, w) - h_ref[:, cs]).astype(BF16)

        pv = pl_ref[...]
        mixed = jnp.concatenate([_dot(pv[:, g * cg:(g + 1) * cg], pw_ref[g]) for g in range(ng)], axis=1)
        xo = x_ref[...] + ps_ref[...] * (mixed + pb_ref[...])
        xo_ref[...] = xo
        hb = _rms(xo, gn_ref[...])[0].astype(BF16)
        hb_ref[...] = hb
        hbt_ref[...] = hb.T

    return _pcall(body, name="pool_fwd", grid=(T // tm,),
                  in_specs=[_rows(tm, D), _prev_halo(tm, hp, D), _rows(tm, D), _full(pw.shape), _full((1, D)), _full((1, D)),
                            _full((1, D))],
                  out_specs=[_rows(tm, D), _rows(tm, D), _rows(tm, D), _cols(D, tm)],
                  out_shape=[S_((T, D), BF16), S_((T, D), F32), S_((T, D), BF16), S_((D, T), BF16)],
                  scratch=[pltpu.VMEM((tm + hp, D), F32)] * 3, args=(h, h, x2, pw, pb, ps, gn))


def _bwd_down(dx, up, wdw, bdw, wd, tm, tps, ffn_taps, name, comm=None):
    T, D = dx.shape
    F = up.shape[1] // 2
    hf = HALO_FFN
    nt = T // tm

    def body(dx_ref, up_ref, uph_ref, wdw_ref, bdw_ref, wd_ref, dac_ref, dgate_ref, wsum_ref, aext):
        i = pl.program_id(0)
        aext[0:hf, :] = jnp.where(i % tps == 0, 0.0, uph_ref[...])
        aext[hf:hf + tm, :] = up_ref[:, 0:F]
        acv = _ffn_conv(aext, wdw_ref, bdw_ref, tm, ffn_taps)
        dg = _dot_nt(dx_ref[...].astype(BF16), wd_ref[...])
        sg = _sig(acv)
        dgate_ref[...] = (dg * acv * sg).astype(BF16)
        dac = dg * up_ref[:, F:2 * F] * (sg * (1.0 + acv * (1.0 - sg)))
        dac_ref[...] = dac
        rows = [_colsum(dac * aext[pl.ds(hf - (ffn_taps - 1) + k, tm), :]) for k in range(ffn_taps)]
        rows += [_colsum(dac), jnp.zeros((8 - ffn_taps - 1, F), F32)]
        _accumulate(wsum_ref, jnp.concatenate(rows, axis=0), i == 0)

    return _pcall(body, name=name, grid=(nt,),
                  in_specs=[_rows(tm, D), _rows(tm, 2 * F), _prev_halo(tm, hf, F), _full(wdw.shape), _full((1, F)),
                            _full((F, D))],
                  out_specs=[_rows(tm, F), _rows(tm, F, col=1), _acc_spec(8, F)],
                  out_shape=[S_((T, F), F32), S_((T, 2 * F), BF16), S_((8, F), F32)],
                  scratch=[pltpu.VMEM((tm + hf, F), F32)], args=(dx, up, up, wdw, bdw, wd), comm=comm)


def _bwd_up(dac, dup, wg, wdw, xin, gn, dx, tm, tps, ffn_taps, name):
    T, D = xin.shape
    F = dac.shape[1]
    ns, _, nc = wg.shape
    hf = HALO_FFN
    nt = T // tm

    def body(dac_ref, dach_ref, dgate_ref, wg_ref, wdw_ref, x_ref, gn_ref, dx_ref, dact_ref, dxo_ref, dgn_ref, dext):
        i = pl.program_id(0)
        dext[0:tm, :] = dac_ref[...]
        dext[tm:tm + hf, :] = jnp.where(i % tps == tps - 1, 0.0, dach_ref[...])
        dact = wdw_ref[ffn_taps - 1:ffn_taps, :] * dac_ref[...]
        for k in range(ffn_taps - 1):
            dact = dact + wdw_ref[k:k + 1, :] * dext[pl.ds(ffn_taps - 1 - k, tm), :]
        dact_ref[...] = dact.astype(BF16)

        dh = jnp.zeros((tm, D), F32)
        for s in range(ns):
            src = dact_ref if s < ns // 2 else dgate_ref
            o = (s % (ns // 2)) * nc
            dh = dh + _dot_nt(src[:, o:o + nc], wg_ref[s])
        _, xh, r = _rms(x_ref[...], gn_ref[...])
        dxn, dg = _rms_bwd(dh, xh, r, gn_ref[...])
        dxo_ref[...] = dx_ref[...] + dxn
        _accumulate(dgn_ref, dg, i == 0)

    return _pcall(body, name=name, grid=(nt,),
                  in_specs=[_rows(tm, F), _next_halo(tm, hf, F, T), _rows(tm, F, col=1), _full(wg.shape), _full(wdw.shape),
                            _rows(tm, D), _full((1, D)), _rows(tm, D)],
                  out_specs=[_rows(tm, F), _rows(tm, D), _acc_spec(1, D)],
                  out_shape=[S_((T, 2 * F), BF16), S_((T, D), F32), S_((1, D), F32)],
                  scratch=[pltpu.VMEM((tm + hf, F), F32)],
                  aliases={2: 0}, args=(dac, dac, dup, wg, wdw, xin, gn, dx))


def _mm_tn(xt, dy, bk, bn, tt, shard_major, name):
    K, T = xt.shape
    N = dy.shape[1]
    nt = T // tt

    def body(x_ref, dy_ref, o_ref, acc):
        t = pl.program_id(2)
        part = _dot(x_ref[...], dy_ref[...].astype(BF16))
        _accumulate(acc, part, t == 0)

        @pl.when(t == nt - 1)
        def _():
            o_ref[...] = acc[...].astype(o_ref.dtype)

    if shard_major:
        out_spec = pl.BlockSpec((None, bk, bn), lambda i, j, t: (j, i, 0))
        out_shape = S_((N // bn, K, bn), BF16)
    else:
        out_spec = pl.BlockSpec((bk, bn), lambda i, j, t: (i, j))
        out_shape = S_((K, N), BF16)
    return _pcall(body, name=name, grid=(K // bk, N // bn, nt),
                  in_specs=[pl.BlockSpec((bk, tt), lambda i, j, t: (i, t)), pl.BlockSpec((tt, bn), lambda i, j, t: (t, j))],
                  out_specs=out_spec, out_shape=out_shape, scratch=[pltpu.VMEM((bk, bn), F32)], args=(xt, dy))


def _pool_bwd(dx3, plb, x2, pw, pb, ps, gn, tm, tps):
    T, D = x2.shape
    ng, cg, _ = pw.shape
    hp = HALO_POOL
    nt = T // tm

    def body(do_ref, doh_ref, pl_ref, x_ref, pw_ref, pb_ref, ps_ref, gn_ref, dxo_ref, dpw_ref, sm_ref, qext, buf_a, buf_b,
             dh_s):
        i = pl.program_id(0)
        do = do_ref[...]
        dm = do * ps_ref[...]
        dmh = jnp.where(i % tps == tps - 1, 0.0, doh_ref[...]) * ps_ref[...]
        pv = pl_ref[...]
        mixed = jnp.concatenate([_dot(pv[:, g * cg:(g + 1) * cg], pw_ref[g]) for g in range(ng)], axis=1)
        dscale = _colsum(do * (mixed + pb_ref[...]))
        dbias = _colsum(dm)
        dmb = dm.astype(BF16)
        dmhb = dmh.astype(BF16)
        for g, w in enumerate(POOL_WINDOWS):
            cs = slice(g * cg, (g + 1) * cg)
            _accumulate(dpw_ref.at[g], _dot_tn(pv[:, cs], dmb[:, cs]), i == 0)
            dpo = _dot_nt(dmb[:, cs], pw_ref[g])
            dh_s[:, cs] = dpo
            qext[0:tm, cs] = dpo / _pool_counts(i, tps, tm, w)
            qext[tm:tm + hp, cs] = _dot_nt(dmhb[:, cs], pw_ref[g]) * (1.0 / w)
        sums = _window_sums(qext, buf_a, buf_b, cg, False)
        for g in range(ng):
            cs = slice(g * cg, (g + 1) * cg)
            dh_s[:, cs] = sums[g][0:tm, cs] - dh_s[:, cs]

        _, xh, r = _rms(x_ref[...], gn_ref[...])
        dxn, dg = _rms_bwd(dh_s[...], xh, r, gn_ref[...])
        dxo_ref[...] = do + dxn
        sm = jnp.concatenate([dscale, dbias, dg, jnp.zeros((5, D), F32)], axis=0)
        _accumulate(sm_ref, sm, i == 0)

    return _pcall(body, name="pool_bwd", grid=(nt,),
                  in_specs=[_rows(tm, D), _next_halo(tm, hp, D, T), _rows(tm, D), _rows(tm, D), _full(pw.shape), _full((1, D)),
                            _full((1, D)), _full((1, D))],
                  out_specs=[_rows(tm, D), pl.BlockSpec((ng, cg, cg), lambda i: (0, 0, 0)), _acc_spec(8, D)],
                  out_shape=[S_((T, D), F32), S_((ng, cg, cg), F32), S_((8, D), F32)],
                  scratch=[pltpu.VMEM((tm + hp, D), F32)] * 3 + [pltpu.VMEM((tm, D), F32)],
                  args=(dx3, dx3, plb, x2, pw, pb, ps, gn))


def _conv_bwd1(dx1, c, w2, lng, lnb, tm):
    T, D = dx1.shape

    def body(do_ref, c_ref, w2_ref, lng_ref, lnb_ref, dc_ref, sm_ref):
        i = pl.program_id(0)
        do = do_ref[...]
        ds = _dot_nt(do.astype(BF16), w2_ref[...])
        cv = c_ref[...]
        xc = cv - jnp.mean(cv, axis=-1, keepdims=True)
        rstd = lax.rsqrt(jnp.mean(xc * xc, axis=-1, keepdims=True) + LN_EPS)
        nh = xc * rstd
        n = nh * lng_ref[...] + lnb_ref[...]
        sg = _sig(n)
        dn = ds * (sg * (1.0 + n * (1.0 - sg)))
        dnh = dn * lng_ref[...]
        dc = rstd * (dnh - jnp.mean(dnh, axis=-1, keepdims=True) - nh * jnp.mean(dnh * nh, axis=-1, keepdims=True))
        dc_ref[...] = dc
        sm = jnp.concatenate([_colsum(dn * nh), _colsum(dn), _colsum(do), _colsum(dc), jnp.zeros((4, D), F32)], axis=0)
        _accumulate(sm_ref, sm, i == 0)

    return _pcall(body, name="conv_bwd1", grid=(T // tm,),
                  in_specs=[_rows(tm, D), _rows(tm, D), _full((D, D)), _full((1, D)), _full((1, D))],
                  out_specs=[_rows(tm, D), _acc_spec(8, D)],
                  out_shape=[S_((T, D), F32), S_((8, D), F32)], args=(dx1, c, w2, lng, lnb))


def _conv_bwd2(dc, a, x, w1g, wdw, gn, dx1, tm, tps, taps, comm=None):
    T, D = x.shape
    ns, _, nc = w1g.shape
    hc = HALO_CONV
    nt = T // tm
    tp = wdw.shape[0]

    def body(dc_ref, dch_ref, a_ref, ah_ref, x_ref, w_ref, wdw_ref, gn_ref, dx1_ref,
             gx_ref, dab_ref, dw_ref, db1_ref, dgn_ref, uext, dext, shifted, du_s, wacc):
        i = pl.program_id(0)
        ah = ah_ref[...]
        uext[0:hc, :] = jnp.where(i % tps == 0, 0.0, ah[:, :D] * _sig(ah[:, D:]))
        av = a_ref[...]
        sg2 = _sig(av[:, D:])
        uext[hc:hc + tm, :] = av[:, :D] * sg2
        dext[0:tm, :] = dc_ref[...]
        dext[tm:tm + hc, :] = jnp.where(i % tps == tps - 1, 0.0, dch_ref[...])

        @pl.when(i == 0)
        def _():
            wacc[...] = jnp.zeros_like(wacc)

        def wgrad(r0, src, ks, first):
            d = dc_ref[pl.ds(r0, CHUNK), :]
            for k, o in ks:
                wacc[8 * k:8 * k + 8, :] += _fold8(d * src[pl.ds(pl.multiple_of(r0 + o, 8), CHUNK), :])
        _by_sublane_shift(uext, shifted, tm, [hc - (taps - 1) + k for k in range(taps)], wgrad)

        def convt(r0, src, ks, first):
            acc = jnp.zeros((CHUNK, D), F32) if first else du_s[pl.ds(r0, CHUNK), :]
            for k, o in ks:
                acc = acc + wdw_ref[k:k + 1, :] * src[pl.ds(pl.multiple_of(r0 + o, 8), CHUNK), :]
            du_s[pl.ds(r0, CHUNK), :] = acc
        _by_sublane_shift(dext, shifted, tm, [taps - 1 - k for k in range(taps)], convt)

        du = du_s[...]
        da1 = du * sg2
        da2 = du * av[:, :D] * (sg2 * (1.0 - sg2))
        dab_ref[:, 0:D] = da1.astype(BF16)
        dab_ref[:, D:2 * D] = da2.astype(BF16)
        _accumulate(db1_ref, jnp.concatenate([_colsum(da1), _colsum(da2)], axis=0), i == 0)
        dh = jnp.zeros((tm, D), F32)
        for s in range(ns):
            dh = dh + _dot_nt(dab_ref[:, s * nc:(s + 1) * nc], w_ref[s])
        _, xh, r = _rms(x_ref[...], gn_ref[...])
        dxn, dg = _rms_bwd(dh, xh, r, gn_ref[...])
        gx_ref[...] = dx1_ref[...] + dxn
        _accumulate(dgn_ref, dg, i == 0)

        @pl.when(i == nt - 1)
        def _():
            dw_ref[...] = jnp.zeros_like(dw_ref)
            for k in range(taps):
                dw_ref[k:k + 1, :] = _colsum(wacc[8 * k:8 * k + 8, :])

    return _pcall(body, name="conv_bwd2", grid=(nt,),
                  in_specs=[_rows(tm, D), _next_halo(tm, hc, D, T), _rows(tm, 2 * D), _prev_halo(tm, hc, 2 * D), _rows(tm, D),
                            _full(w1g.shape), _full(wdw.shape), _full((1, D)), _rows(tm, D)],
                  out_specs=[_rows(tm, D), _rows(tm, 2 * D), _acc_spec(tp, D), _acc_spec(2, D), _acc_spec(1, D)],
                  out_shape=[S_((T, D), F32), S_((T, 2 * D), BF16), S_((tp, D), F32), S_((2, D), F32), S_((1, D), F32)],
                  scratch=[pltpu.VMEM((tm + hc, D), F32), pltpu.VMEM((tm + hc, D), F32), pltpu.VMEM((tm + hc - 8, D), F32),
                           pltpu.VMEM((tm, D), F32), pltpu.VMEM((8 * taps, D), F32)],
                  args=(dc, dc, a, a, x, w1g, wdw, gn, dx1), comm=comm)


def _place():
    x, y, c = lax.axis_index("x"), lax.axis_index("y"), lax.axis_index("c")
    chips = [(1 - x, y), (x, 1 - y), (1 - x, 1 - y)]
    return x, y, c, chips


def _own_slot(w, layer, ids, dtype, name):
    _, r, c = w.shape
    br = _row_block(r, c)

    def body(ids_ref, w_ref, o_ref):
        o_ref[...] = w_ref[...].astype(dtype)

    return _pcall(body, name=name, grid=(r // br,), prefetch=1,
                  in_specs=[pl.BlockSpec((None, br, c), lambda i, ids: (layer, i, 0))],
                  out_specs=pl.BlockSpec((None, br, c), lambda i, ids: (ids[0], i, 0)),
                  out_shape=S_((N_CHIPS, r, c), dtype), args=(ids, w))


class _GatherComm:
    def __init__(self, slots):
        n = len(slots)
        self.inputs = list(slots)
        self.out_shape = [S_(s.shape, s.dtype) for s in slots]
        self.aliases = {i: i for i in range(n)}
        self.sems = [pltpu.SemaphoreType.DMA((n, 3))] * 4
        self.halves = [s.shape[1] // 2 for s in slots]

    def _copies(self, outs, sems):
        s_ici, r_ici, s_d2d, r_d2d = sems
        x, y, c, chips = _place()
        peers = [(j, chip, 2 * chip[0] + chip[1]) for j, chip in enumerate(chips)]

        def half(i, h):
            return pl.ds(pl.multiple_of(h * self.halves[i], 8), self.halves[i])

        def ici(i, j, chip, to):
            blk = outs[i].at[chip, half(i, c)]
            return pltpu.make_async_remote_copy(src_ref=blk, dst_ref=blk, send_sem=s_ici.at[i, j], recv_sem=r_ici.at[i, j],
                                                device_id=to, device_id_type=MESH)

        def d2d(i, j, chip, h):
            blk = outs[i].at[chip, half(i, h)]
            return pltpu.make_async_remote_copy(src_ref=blk, dst_ref=blk, send_sem=s_d2d.at[i, j], recv_sem=r_d2d.at[i, j],
                                                device_id=(x, y, 1 - c), device_id_type=MESH)
        return (x, y, c, 2 * x + y), peers, ici, d2d

    def before(self, cin, outs, sems, step, total):
        (x, y, c, q), peers, ici, d2d = self._copies(outs, sems)

        @pl.when(step == 0)
        def _():
            for i in range(len(outs)):
                for j, chip, _ in peers:
                    ici(i, j, q, (*chip, c)).start()

    def after(self, cin, outs, sems, step, total):
        (x, y, c, q), peers, ici, d2d = self._copies(outs, sems)

        @pl.when(step == max(total - 2, 0))
        def _():
            for i in range(len(outs)):
                for j, chip, qj in peers:
                    ici(i, j, qj, (x, y, c)).wait_recv()
                    d2d(i, j, qj, c).start()

        @pl.when(step == total - 1)
        def _():
            for i in range(len(outs)):
                for j, chip, qj in peers:
                    d2d(i, j, qj, 1 - c).wait_recv()
            for i in range(len(outs)):
                for j, chip, qj in peers:
                    ici(i, j, q, (*chip, c)).wait_send()
                    d2d(i, j, qj, c).wait_send()


class _ExchangeComm:
    def __init__(self, sums):
        n = len(sums)
        self.inputs = list(sums)
        self.out_shape = [S_((3,) + s.shape[1:], s.dtype) for s in sums]
        self.aliases = {}
        self.sems = [pltpu.SemaphoreType.DMA((n, 3))] * 2

    def _copies(self, srcs, outs, sems):
        s_sem, r_sem = sems
        x, y, c, chips = _place()
        return [pltpu.make_async_remote_copy(src_ref=srcs[i].at[2 * chip[0] + chip[1]], dst_ref=outs[i].at[j],
                                             send_sem=s_sem.at[i, j], recv_sem=r_sem.at[i, j],
                                             device_id=(*chip, c), device_id_type=MESH)
                for i in range(len(srcs)) for j, chip in enumerate(chips)]

    def before(self, srcs, outs, sems, step, total):
        @pl.when(step == 0)
        def _():
            for cp in self._copies(srcs, outs, sems):
                cp.start()

    def after(self, srcs, outs, sems, step, total):
        @pl.when(step == total - 1)
        def _():
            for cp in self._copies(srcs, outs, sems):
                cp.wait()


def _comm_only(comm, name):
    def body():
        pass
    outs = _pcall(body, name=name, grid=(1,), in_specs=[], out_specs=[], out_shape=[], args=[], comm=comm)
    return list(outs)


def _pair_exchange(parts, name):
    n = len(parts)

    def body(*refs):
        srcs, outs = refs[:n], refs[n:2 * n]
        s_sem, r_sem = refs[2 * n:]
        x, y, c, _ = _place()
        cps = []
        for i in range(n):
            for s in range(N_CHIPS):
                cp = pltpu.make_async_remote_copy(src_ref=srcs[i].at[s, 1 - c], dst_ref=outs[i].at[s], send_sem=s_sem.at[i, s],
                                                  recv_sem=r_sem.at[i, s], device_id=(x, y, 1 - c), device_id_type=MESH)
                cp.start()
                cps.append(cp)
        for cp in cps:
            cp.wait()

    any_spec = pl.BlockSpec(memory_space=pl.ANY)
    outs = pl.pallas_call(
        body, name=name, in_specs=[any_spec] * n, out_specs=[any_spec] * n,
        out_shape=[S_((N_CHIPS,) + p.shape[2:], p.dtype) for p in parts],
        scratch_shapes=[pltpu.SemaphoreType.DMA((n, N_CHIPS))] * 2,
    )(*parts)
    return list(outs)


def _share_halves(bufs):
    n = len(bufs)
    nl = max(b.shape[0] for b in bufs)

    def body(*refs):
        outs = refs[n:2 * n]
        s_sem, r_sem = refs[2 * n:]
        x, y, c, _ = _place()
        cps = []
        for i in range(n):
            for l in range(bufs[i].shape[0]):
                blk = outs[i].at[l, c]
                cp = pltpu.make_async_remote_copy(src_ref=blk, dst_ref=blk, send_sem=s_sem.at[i, l], recv_sem=r_sem.at[i, l],
                                                  device_id=(x, y, 1 - c), device_id_type=MESH)
                cp.start()
                cps.append(cp)
        for cp in cps:
            cp.wait()

    any_spec = pl.BlockSpec(memory_space=pl.ANY)
    return pl.pallas_call(
        body, name="grad_share_halves", in_specs=[any_spec] * n, out_specs=[any_spec] * n,
        out_shape=[S_(b.shape, b.dtype) for b in bufs], input_output_aliases={i: i for i in range(n)},
        scratch_shapes=[pltpu.SemaphoreType.DMA((n, nl))] * 2,
    )(*bufs)


def _small_allreduce(groups):
    n = len(groups)
    arrays = []
    for grp in groups:
        for p in grp:
            if not isinstance(p, int) and not any(p[0] is a for a in arrays):
                arrays.append(p[0])
    shapes = []
    for grp in groups:
        rows = sum(p if isinstance(p, int) else p[2] - p[1] for p in grp)
        width = next(p[0].shape[1] for p in grp if not isinstance(p, int))
        assert rows % 8 == 0
        shapes.append((rows, width))
    na = len(arrays)

    def body(*refs):
        a_refs, o_refs = refs[:na], refs[na:na + n]
        rest = refs[na + n:]
        v_refs, sibs, css, gots = rest[:n], rest[n:2 * n], rest[2 * n:3 * n], rest[3 * n:4 * n]
        s_sem, r_sem = rest[4 * n:]
        for i, grp in enumerate(groups):
            o = 0
            for p in grp:
                if isinstance(p, int):
                    v_refs[i][o:o + p, :] = jnp.zeros((p, shapes[i][1]), F32)
                    o += p
                else:
                    k = next(t for t, a in enumerate(arrays) if a is p[0])
                    v_refs[i][o:o + p[2] - p[1], :] = a_refs[k][p[1]:p[2], :]
                    o += p[2] - p[1]
        x, y, c, chips = _place()
        q = 2 * x + y
        to_sib = [pltpu.make_async_remote_copy(src_ref=v_refs[i], dst_ref=sibs[i], send_sem=s_sem.at[i, 3], recv_sem=r_sem.at[i, 3],
                                               device_id=(x, y, 1 - c), device_id_type=MESH) for i in range(n)]
        for cp in to_sib:
            cp.start()
        cps = []
        for i in range(n):
            to_sib[i].wait()
            mine, other = v_refs[i][...], sibs[i][...]
            css[i][...] = jnp.where(c == 0, mine, other) + jnp.where(c == 0, other, mine)
            for j, chip in enumerate(chips):
                cp = pltpu.make_async_remote_copy(src_ref=css[i], dst_ref=gots[i].at[j], send_sem=s_sem.at[i, j],
                                                  recv_sem=r_sem.at[i, j], device_id=(*chip, c), device_id_type=MESH)
                cp.start()
                cps.append(cp)
        for cp in cps:
            cp.wait()
        flips = [2, 1, 3]
        for i in range(n):
            total = None
            for k in range(N_CHIPS):
                d = q ^ k
                term = jnp.where(d == 0, css[i][...], 0.0)
                for j in range(3):
                    term = jnp.where(d == flips[j], gots[i][j], term)
                total = term if total is None else total + term
            o_refs[i][...] = total

    vm = pl.BlockSpec(memory_space=pltpu.VMEM)
    return pl.pallas_call(
        body, name="small_allreduce", in_specs=[vm] * na, out_specs=[vm] * n, out_shape=[S_(s, F32) for s in shapes],
        scratch_shapes=[pltpu.VMEM(s, F32) for s in shapes] * 3 + [pltpu.VMEM((3,) + s, F32) for s in shapes]
        + [pltpu.SemaphoreType.DMA((n, 4)), pltpu.SemaphoreType.DMA((n, 4))],
    )(*arrays)


def _row_block(r, c, target_bytes=1 << 20):
    br = r
    while br % 32 == 0 and br * c * 4 > target_bytes:
        br //= 2
    return br


def _chip_sum(part, sib, ids, name):
    _, _, r, c = part.shape
    br = _row_block(r, c)

    def body(ids_ref, p_ref, s_ref, o_ref):
        o_ref[...] = (p_ref[...].astype(F32) + s_ref[...].astype(F32)).astype(BF16)

    return _pcall(body, name=name, grid=(N_CHIPS, r // br), prefetch=1,
                  in_specs=[pl.BlockSpec((None, None, br, c), lambda s, i, ids: (s, ids[1], i, 0)),
                            pl.BlockSpec((None, br, c), lambda s, i, ids: (s, i, 0))],
                  out_specs=pl.BlockSpec((None, br, c), lambda s, i, ids: (s, i, 0)),
                  out_shape=S_((N_CHIPS, r, c), BF16), args=(ids, part, sib))


def _final_sum(part, sib, got, ids, name, layer=0, n_layers=1, buf=None):
    _, r, c = got.shape
    br = _row_block(r, c)

    def body(ids_ref, p_ref, s_ref, got_ref, *rest):
        v = p_ref[...].astype(F32) + s_ref[...].astype(F32)
        for j in range(3):
            v = v + got_ref[j].astype(F32)
        rest[-1][...] = v

    in_specs = [pl.BlockSpec((None, None, br, c), lambda i, ids: (ids[0], ids[1], i, 0)),
                pl.BlockSpec((None, br, c), lambda i, ids: (ids[0], i, 0)),
                pl.BlockSpec((3, br, c), lambda i, ids: (0, i, 0))]
    args = [ids, part, sib, got]
    if buf is not None:
        in_specs.append(pl.BlockSpec(memory_space=pl.ANY))
        args.append(buf)
    return _pcall(body, name=name, grid=(r // br,), prefetch=1, in_specs=in_specs,
                  out_specs=pl.BlockSpec((None, None, br, c), lambda i, ids: (layer, ids[1], i, 0)),
                  out_shape=S_((n_layers, 2, r, c), F32), args=args, aliases={4: 0} if buf is not None else None)


def _adam_math(w, g, m, v):
    c1 = 1.0 / (1.0 - ADAM_B1 ** ADAM_STEP)
    c2 = 1.0 / (1.0 - ADAM_B2 ** ADAM_STEP)
    mn = ADAM_B1 * m + (1.0 - ADAM_B1) * g
    vn = ADAM_B2 * v + (1.0 - ADAM_B2) * (g * g)
    return -ADAM_LR * ((mn * c1) / (jnp.sqrt(vn * c2) + ADAM_EPS) + ADAM_WD * w), mn, vn


def _adamw(w, g, m, v, name):
    nl, r, c = w.shape
    br = _row_block(r, c, 1 << 19)

    def body(w_ref, g_ref, m_ref, v_ref, d_ref, mo_ref, vo_ref):
        d_ref[...], mo_ref[...], vo_ref[...] = _adam_math(w_ref[...], g_ref[...], m_ref[...], v_ref[...])

    blk = pl.BlockSpec((None, br, c), lambda l, i: (l, i, 0))
    return _pcall(body, name=name, grid=(nl, r // br), in_specs=[blk] * 4, out_specs=[blk] * 3,
                  out_shape=[S_((nl, r, c), F32)] * 3, args=(w, g, m, v))


def _adamw_small(rd, rf, sharded_g, ws, ms, vs, rows_d, rows_f):
    n = len(ws)
    ns = len(sharded_g)
    rep = [k for k in range(n) if rows_d[k] is not None or rows_f[k] is not None]

    def body(*refs):
        rd_ref, rf_ref = refs[0], refs[1]
        sg = refs[2:2 + ns]
        w_refs, m_refs, v_refs = (refs[2 + ns + t * n:2 + ns + (t + 1) * n] for t in range(3))
        outs = refs[2 + ns + 3 * n:]
        g_out, d_out, m_out, v_out = outs[:len(rep)], outs[len(rep):len(rep) + n], outs[len(rep) + n:len(rep) + 2 * n], \
            outs[len(rep) + 2 * n:]
        si = 0
        for k in range(n):
            shape = w_refs[k].shape
            if rows_d[k] is not None or rows_f[k] is not None:
                src, rws = (rd_ref, rows_d[k]) if rows_d[k] is not None else (rf_ref, rows_f[k])
                axis = 0 if shape[0] == len(rws) else 1
                g = src[rws[0]:rws[0] + 1, :]
                if len(rws) > 1:
                    g = jnp.concatenate([src[r:r + 1, :] for r in rws], axis=axis)
                g_out[rep.index(k)][...] = g
            else:
                g = sg[si][...]
                si += 1
            d_out[k][...], m_out[k][...], v_out[k][...] = _adam_math(w_refs[k][...], g, m_refs[k][...], v_refs[k][...])

    vm = pl.BlockSpec(memory_space=pltpu.VMEM)
    shapes = [S_(w.shape, F32) for w in ws]
    outs = pl.pallas_call(
        body, name="adamw_small", in_specs=[vm] * (2 + ns + 3 * n), out_specs=[vm] * (len(rep) + 3 * n),
        out_shape=[shapes[k] for k in rep] + shapes * 3,
    )(rd, rf, *sharded_g, *ws, *ms, *vs)
    g_rep = dict(zip(rep, outs[:len(rep)]))
    o = outs[len(rep):]
    return g_rep, o[:n], o[n:2 * n], o[2 * n:]


def _pack(pieces, rows):
    flat = jnp.concatenate([p.reshape(-1).astype(F32) for p in pieces])
    return jnp.pad(flat, (0, rows * 128 - flat.shape[0])).reshape(rows, 128)


def _unpack(packed, shapes):
    flat = packed.reshape(-1)
    out, o = [], 0
    for s in shapes:
        n = 1
        for d in s:
            n *= d
        out.append(flat[o:o + n].reshape(s))
        o += n
    return out


def _rows_for(pieces_or_shapes):
    n = 0
    for p in pieces_or_shapes:
        k = 1
        for d in (p if isinstance(p, tuple) else p.shape):
            k *= d
        n += k
    return -(-n // 1024) * 8


def kernel(x, norm_mix, norm_ffn, conv_w_pw1, conv_b_pw1, conv_w_dw, conv_b_dw, conv_ln_g, conv_ln_b, conv_w_pw2, conv_b_pw2, pool_w, pool_b, pool_scale, ffn_w_up, ffn_w_dw, ffn_b_dw, ffn_w_down, final_norm, loss_target, m_norm_mix, m_norm_ffn, m_conv_w_pw1, m_conv_b_pw1, m_conv_w_dw, m_conv_b_dw, m_conv_ln_g, m_conv_ln_b, m_conv_w_pw2, m_conv_b_pw2, m_pool_w, m_pool_b, m_pool_scale, m_ffn_w_up, m_ffn_w_dw, m_ffn_b_dw, m_ffn_w_down, m_final_norm, v_norm_mix, v_norm_ffn, v_conv_w_pw1, v_conv_b_pw1, v_conv_w_dw, v_conv_b_dw, v_conv_ln_g, v_conv_ln_b, v_conv_w_pw2, v_conv_b_pw2, v_pool_w, v_pool_b, v_pool_scale, v_ffn_w_up, v_ffn_w_dw, v_ffn_b_dw, v_ffn_w_down, v_final_norm):
    nb, seq, D = x.shape
    T = nb * seq
    F = ffn_w_down.shape[1] * N_CHIPS
    taps = conv_w_dw.shape[1]
    ffn_taps = ffn_w_dw.shape[1]
    ng = pool_w.shape[1]
    cg = pool_w.shape[3]
    dsh = D // N_CHIPS
    fsh = F // N_CHIPS
    assert taps - 1 <= HALO_CONV and ffn_taps - 1 <= HALO_FFN and max(POOL_WINDOWS) <= HALO_POOL
    tm = min(TILE_D, seq)
    tf = min(TILE_F, seq)
    assert seq % tm == 0 and seq % tf == 0 and tm % HALO_CONV == 0 and tf % CHUNK == 0
    tps, tpf = seq // tm, seq // tf

    xi, yi, ci = lax.axis_index("x"), lax.axis_index("y"), lax.axis_index("c")
    qi = 2 * xi + yi
    ids = jnp.stack([qi, ci]).astype(jnp.int32)
    x2d = x.reshape(T, D)
    tg2d = loss_target.reshape(T, D)

    small_sharded = [conv_w_dw[0], ffn_w_dw, pool_b, pool_scale]
    srows = 2 * _rows_for(small_sharded)
    s_pw1, s_pw2 = _own_slot(conv_w_pw1, 0, ids, BF16, "slot_pw1"), _own_slot(conv_w_pw2, 0, ids, BF16, "slot_pw2")
    s_small = _own_slot(_pack(small_sharded, srows)[None], 0, ids, F32, "slot_small")
    s_pool = _own_slot(pool_w.reshape(1, ng * (cg // N_CHIPS), cg), 0, ids, BF16, "slot_pool")
    s_up = [_own_slot(ffn_w_up, l, ids, BF16, "slot_up%d" % l) for l in range(2)]
    s_down = [_own_slot(ffn_w_down, l, ids, BF16, "slot_down%d" % l) for l in range(2)]
    w1g, w2g, smallg = _comm_only(_GatherComm([s_pw1, s_pw2, s_small]), "gather_conv")
    w2f = w2g.reshape(D, D)
    sm_parts = [_unpack(smallg[s], [(taps, dsh), (2, ffn_taps, fsh), (1, dsh), (1, dsh)]) for s in range(N_CHIPS)]
    wdw_f = jnp.concatenate([p[0] for p in sm_parts], axis=1)
    wdw_f = jnp.pad(wdw_f, ((0, HALO_CONV - taps), (0, 0)))
    fdw_f = jnp.concatenate([p[1] for p in sm_parts], axis=2)
    fdw_f = jnp.pad(fdw_f, ((0, 0), (0, 8 - ffn_taps), (0, 0)))
    pb_f = jnp.concatenate([p[2] for p in sm_parts], axis=1)
    ps_f = jnp.concatenate([p[3] for p in sm_parts], axis=1)

    hb0t, a = _conv_in(x2d, norm_mix[0:1], w1g, conv_b_pw1, tm)
    c, sbt, x1, hb1, hb1t, wup0, wd0, pwg = _conv_mid(a, x2d, wdw_f, conv_b_dw, conv_ln_g, conv_ln_b, w2f, conv_b_pw2,
                                                      norm_ffn[0:1], tm, tps, taps,
                                                      comm=_GatherComm([s_up[0], s_down[0], s_pool]))
    up0, wup1 = _mm_up(hb1, wup0, tm, "ffn_up0", comm=_GatherComm([s_up[1]]))
    gb0t, x2, h2, wd1 = _ffn_down(up0, x1, fdw_f[0], ffn_b_dw[0:1], wd0.reshape(F, D), norm_mix[1:2], tf, tpf, ffn_taps,
                                  "ffn_down0", comm=_GatherComm([s_down[1]]))
    wup = [wup0, wup1]
    wdn = [wd0.reshape(F, D), wd1.reshape(F, D)]
    pwf = pwg.reshape(N_CHIPS, ng, cg // N_CHIPS, cg).transpose(1, 0, 2, 3).reshape(ng, cg, cg)
    plb, x3, hb3, hb3t = _pool_fwd(h2, x2, pwf, pb_f, ps_f, norm_ffn[1:2], tm, tps)
    up1 = _mm_up(hb3, wup[1], tm, "ffn_up1")
    gb1t, dx4, loss_part, d_final = _ffn_down(up1, x3, fdw_f[1], ffn_b_dw[1:2], wdn[1], final_norm.reshape(1, D), tf, tpf,
                                              ffn_taps, "ffn_down1", target=tg2d)

    dac1, dup1, fw1 = _bwd_down(dx4, up1, fdw_f[1], ffn_b_dw[1:2], wdn[1], tf, tpf, ffn_taps, "bwd_down1")
    p_down1 = _mm_tn(gb1t, dx4, F // 2, D, tm, False, "dw_down1")
    dup1, dx3, dnf1 = _bwd_up(dac1, dup1, wup[1], fdw_f[1], x3, norm_ffn[1:2], dx4, tf, tpf, ffn_taps, "bwd_up1")
    p_up1 = _mm_tn(hb3t, dup1, D, 2 * fsh, tm, True, "dw_up1")
    fs = {}

    def chip_sums(group, tag):
        nms = list(group)
        from_sib = _pair_exchange([group[nm] for nm in nms], "grad_pair_exchange_" + tag)
        sums = [_chip_sum(group[nm], s, ids, "chip_sum_" + nm) for nm, s in zip(nms, from_sib)]
        for nm, s in zip(nms, from_sib):
            fs[nm] = (group[nm], s)
        return nms, sums

    nms1, sums1 = chip_sums({"down1": p_down1.reshape(N_CHIPS, 2, fsh // 2, D),
                             "up1": p_up1.reshape(N_CHIPS, 2, D // 2, 2 * fsh)}, "ffn1")
    dx2, dpw, pool_sm = _pool_bwd(dx3, plb, x2, pwf, pb_f, ps_f, norm_mix[1:2], tm, tps)
    dac0, dup0, fw0, *got1 = _bwd_down(dx2, up0, fdw_f[0], ffn_b_dw[0:1], wdn[0], tf, tpf, ffn_taps, "bwd_down0",
                                       comm=_ExchangeComm(sums1))
    p_down0 = _mm_tn(gb0t, dx2, F // 2, D, tm, False, "dw_down0")
    dup0, dx1, dnf0 = _bwd_up(dac0, dup0, wup[0], fdw_f[0], x1, norm_ffn[0:1], dx2, tf, tpf, ffn_taps, "bwd_up0")
    p_up0 = _mm_tn(hb1t, dup0, D, 2 * fsh, tm, True, "dw_up0")
    dc, conv_sm = _conv_bwd1(dx1, c, w2f, conv_ln_g, conv_ln_b, tm)
    p_pw2 = _mm_tn(sbt, dx1, D, D, tm, False, "dw_pw2")
    csh = cg // N_CHIPS
    p_pool = dpw.reshape(2, ng // 2, N_CHIPS, csh, cg).transpose(2, 0, 1, 3, 4).reshape(N_CHIPS, 2, (ng // 2) * csh, cg)
    nms2, sums2 = chip_sums({"down0": p_down0.reshape(N_CHIPS, 2, fsh // 2, D),
                             "up0": p_up0.reshape(N_CHIPS, 2, D // 2, 2 * fsh), "pool": p_pool.astype(BF16),
                             "pw2": p_pw2.reshape(N_CHIPS, 2, dsh // 2, D)}, "ffn0")
    grad_x, dab, dwdw, db1, dnm0, *got2 = _conv_bwd2(dc, a, x2d, w1g, wdw_f, norm_mix[0:1], dx1, tm, tps, taps,
                                                     comm=_ExchangeComm(sums2))
    p_pw1 = _mm_tn(hb0t, dab, D, 2 * dsh, tm, True, "dw_pw1")
    nms3, sums3 = chip_sums({"pw1": p_pw1.reshape(N_CHIPS, 2, D // 2, 2 * dsh)}, "conv")
    got3 = _comm_only(_ExchangeComm(sums3), "grad_chip_exchange_conv")
    for nm, g in zip(nms1 + nms2 + nms3, list(got1) + list(got2) + got3):
        fs[nm] = fs[nm] + (g,)
    bufs = [_final_sum(*fs[nm], ids, "final_sum_" + nm) for nm in ("pw1", "pw2", "pool")]
    for nm in ("up", "down"):
        b = _final_sum(*fs[nm + "0"], ids, "final_sum_" + nm + "0", layer=0, n_layers=2)
        bufs.append(_final_sum(*fs[nm + "1"], ids, "final_sum_" + nm + "1", layer=1, n_layers=2, buf=b))
    g_pw1, g_pw2, g_pool, g_up, g_down = [b.reshape(b.shape[0], 2 * b.shape[2], b.shape[3]) for b in _share_halves(bufs)]

    rd, rf = _small_allreduce([
        [(dnm0, 0, 1), (pool_sm, 2, 3), (dnf0, 0, 1), (dnf1, 0, 1), (db1, 0, 2), (conv_sm, 0, 4), (d_final, 0, 1),
         (pool_sm, 0, 2), 3, (dwdw, 0, dwdw.shape[0])],
        [(fw0, 0, 4), (fw1, 0, 4)]])
    g_wdw = lax.dynamic_slice_in_dim(rd[16:16 + taps], qi * dsh, dsh, axis=1)
    g_ps = lax.dynamic_slice_in_dim(rd[11:12], qi * dsh, dsh, axis=1)
    g_pb = lax.dynamic_slice_in_dim(rd[12:13], qi * dsh, dsh, axis=1)
    g_fdw = lax.dynamic_slice_in_dim(rf.reshape(2, 4, F)[:, :ffn_taps], qi * fsh, fsh, axis=2).reshape(2 * ffn_taps, fsh)

    def big(w, g, m, v, nm):
        d, mn, vn = _adamw(w.reshape(g.shape), g, m.reshape(g.shape), v.reshape(g.shape), "adamw_" + nm)
        return g.reshape(w.shape), d.reshape(w.shape), mn.reshape(w.shape), vn.reshape(w.shape)

    o_pw1 = big(conv_w_pw1, g_pw1, m_conv_w_pw1, v_conv_w_pw1, "pw1")
    o_pw2 = big(conv_w_pw2, g_pw2, m_conv_w_pw2, v_conv_w_pw2, "pw2")
    o_pool = big(pool_w, g_pool, m_pool_w, v_pool_w, "pool")
    o_up = big(ffn_w_up, g_up, m_ffn_w_up, v_ffn_w_up, "up")
    o_down = big(ffn_w_down, g_down, m_ffn_w_down, v_ffn_w_down, "down")

    snames = ["norm_mix", "norm_ffn", "b_pw1", "w_dw", "b_dw", "ln_g", "ln_b", "b_pw2", "pool_b", "pool_scale", "ffn_w_dw",
              "ffn_b_dw", "final_norm"]
    sw = [norm_mix, norm_ffn, conv_b_pw1, conv_w_dw, conv_b_dw, conv_ln_g, conv_ln_b, conv_b_pw2, pool_b, pool_scale,
          ffn_w_dw, ffn_b_dw, final_norm]
    smm = [m_norm_mix, m_norm_ffn, m_conv_b_pw1, m_conv_w_dw, m_conv_b_dw, m_conv_ln_g, m_conv_ln_b, m_conv_b_pw2, m_pool_b,
           m_pool_scale, m_ffn_w_dw, m_ffn_b_dw, m_final_norm]
    svv = [v_norm_mix, v_norm_ffn, v_conv_b_pw1, v_conv_w_dw, v_conv_b_dw, v_conv_ln_g, v_conv_ln_b, v_conv_b_pw2, v_pool_b,
           v_pool_scale, v_ffn_w_dw, v_ffn_b_dw, v_final_norm]
    sshapes = [tuple(w.shape) for w in sw]
    two_d = [(2, D), (2, D), (1, 2 * D), (taps, dsh), (1, D), (1, D), (1, D), (1, D), (1, dsh), (1, dsh), (2 * ffn_taps, fsh),
             (2, F), (1, D)]
    rows_d = [[0, 1], [2, 3], [4, 5], None, [9], [6], [7], [8], None, None, None, None, [10]]
    rows_f = [None] * 11 + [[3, 7], None]
    as2d = lambda ts: [t.reshape(s) for t, s in zip(ts, two_d)]
    g_rep, sd, sm_new, sv_new = _adamw_small(rd, rf, [g_wdw, g_pb, g_ps, g_fdw], as2d(sw), as2d(smm), as2d(svv), rows_d, rows_f)
    sg = dict(g_rep)
    sg.update({3: g_wdw, 8: g_pb, 9: g_ps, 10: g_fdw})
    small_out = {n: tuple(t.reshape(sshapes[k]) for t in (sg[k], sd[k], sm_new[k], sv_new[k])) for k, n in enumerate(snames)}

    order = [small_out["norm_mix"], small_out["norm_ffn"], o_pw1, small_out["b_pw1"], small_out["w_dw"], small_out["b_dw"],
             small_out["ln_g"], small_out["ln_b"], o_pw2, small_out["b_pw2"], o_pool, small_out["pool_b"],
             small_out["pool_scale"], o_up, small_out["ffn_w_dw"], small_out["ffn_b_dw"], o_down, small_out["final_norm"]]
    loss = lax.psum(loss_part[0, 0], ("x", "y", "c"))
    return (loss, grad_x.reshape(nb, seq, D), *[o[0] for o in order], *[o[1] for o in order], *[o[2] for o in order],
            *[o[3] for o in order])
```

```python
import functools

import jax
import jax.numpy as jnp
from jax import lax
from jax.experimental import pallas as pl
from jax.experimental.pallas import tpu as pltpu

F32, BF16 = jnp.float32, jnp.bfloat16
S_ = jax.ShapeDtypeStruct
MESH = pl.DeviceIdType.MESH

RMS_EPS, LN_EPS = 1e-6, 1e-5
POOL_WINDOWS = (2, 4, 8, 16)
ADAM_LR, ADAM_B1, ADAM_B2, ADAM_EPS, ADAM_WD, ADAM_STEP = 0.001, 0.9, 0.999, 1e-08, 0.01, 10

VMEM_LIMIT_BYTES = 60 * 1024 * 1024
N_CHIPS = 4
CHUNK = 16
HALO_CONV = 32
HALO_POOL = 32
HALO_FFN = 8
TILE_D = 512
TILE_F = 256
TILE_MM = 1024


def _pcall(body, *, name, grid, in_specs, out_specs, out_shape, args, scratch=(), aliases=None, prefetch=0, comm=None):
    params = pltpu.CompilerParams(dimension_semantics=("arbitrary",) * len(grid), vmem_limit_bytes=VMEM_LIMIT_BYTES)
    if comm is not None:
        assert not prefetch
        single = not isinstance(out_shape, (list, tuple))
        in_specs, args, scratch = list(in_specs), list(args), list(scratch)
        out_specs, out_shape = ([out_specs], [out_shape]) if single else (list(out_specs), list(out_shape))
        n_in, n_out, n_scr, n_cin, n_cout = len(in_specs), len(out_specs), len(scratch), len(comm.inputs), len(comm.out_shape)
        any_spec = pl.BlockSpec(memory_space=pl.ANY)
        aliases = dict(aliases or {})
        aliases.update({n_in + a: n_out + b for a, b in comm.aliases.items()})
        inner = body

        def body(*refs):
            ins, cin = refs[:n_in], refs[n_in:n_in + n_cin]
            outs = refs[n_in + n_cin:n_in + n_cin + n_out]
            cout = refs[n_in + n_cin + n_out:n_in + n_cin + n_out + n_cout]
            scr = refs[n_in + n_cin + n_out + n_cout:n_in + n_cin + n_out + n_cout + n_scr]
            sems = refs[n_in + n_cin + n_out + n_cout + n_scr:]
            step, total = 0, 1
            for d, g in enumerate(grid):
                step = step * g + pl.program_id(d)
                total *= g
            comm.before(cin, cout, sems, step, total)
            inner(*ins, *outs, *scr)
            comm.after(cin, cout, sems, step, total)

        in_specs += [any_spec] * n_cin
        args += list(comm.inputs)
        out_specs += [any_spec] * n_cout
        out_shape += list(comm.out_shape)
        scratch += list(comm.sems)
    if prefetch:
        spec = pltpu.PrefetchScalarGridSpec(num_scalar_prefetch=prefetch, grid=grid, in_specs=in_specs,
                                            out_specs=out_specs, scratch_shapes=list(scratch))
        return pl.pallas_call(body, name=name, grid_spec=spec, out_shape=out_shape,
                              input_output_aliases=aliases or {}, compiler_params=params)(*args)
    return pl.pallas_call(body, name=name, grid=grid, in_specs=in_specs, out_specs=out_specs, out_shape=out_shape,
                          scratch_shapes=list(scratch), input_output_aliases=aliases or {},
                          compiler_params=params)(*args)


def _full(shape):
    n = len(shape)
    return pl.BlockSpec(tuple(shape), lambda *_: (0,) * n, pipeline_mode=pl.Buffered(1))


def _rows(tm, c, col=0):
    return pl.BlockSpec((tm, c), lambda i, *_: (i, col))


def _cols(r, tm):
    return pl.BlockSpec((r, tm), lambda i, *_: (0, i))


def _prev_halo(tm, hb, c):
    return pl.BlockSpec((hb, c), lambda i, *_: (jnp.maximum(i * (tm // hb) - 1, 0), 0))


def _next_halo(tm, hb, c, total_rows):
    last = total_rows // hb - 1
    return pl.BlockSpec((hb, c), lambda i, *_: (jnp.minimum((i + 1) * (tm // hb), last), 0))


def _acc_spec(r, c):
    return pl.BlockSpec((r, c), lambda *_: (0, 0))


def _rms(x, g):
    r = lax.rsqrt(jnp.mean(x * x, axis=-1, keepdims=True) + RMS_EPS)
    xh = x * r
    return xh * g, xh, r


def _rms_bwd(dy, xh, r, g):
    dxh = dy * g
    dx = r * (dxh - xh * jnp.mean(dxh * xh, axis=-1, keepdims=True))
    return dx, jnp.sum(dy * xh, axis=0, keepdims=True)


def _colsum(v):
    return jnp.sum(v, axis=0, keepdims=True)


def _fold8(v):
    out = v[0:8]
    for j in range(1, v.shape[0] // 8):
        out = out + v[8 * j:8 * j + 8]
    return out


def _chunks(n_rows, fn):
    def step(j, carry):
        fn(pl.multiple_of(j * CHUNK, CHUNK))
        return carry
    lax.fori_loop(0, n_rows // CHUNK, step, 0)


def _dot(a, b):
    return jnp.dot(a, b, preferred_element_type=F32)


def _dot_nt(a, b):
    return lax.dot_general(a, b, (((1,), (1,)), ((), ())), preferred_element_type=F32)


def _dot_tn(a, b):
    return lax.dot_general(a, b, (((0,), (0,)), ((), ())), preferred_element_type=F32)


def _sig(v):
    return jax.nn.sigmoid(v)


def _accumulate(ref, val, first):
    @pl.when(first)
    def _():
        ref[...] = val

    @pl.when(jnp.logical_not(first))
    def _():
        ref[...] = ref[...] + val


def _conv_in(x, g, w1g, b1, tm):
    T, D = x.shape
    ns, _, nc = w1g.shape

    def body(x_ref, g_ref, w_ref, b_ref, hbt_ref, a_ref):
        hb = _rms(x_ref[...], g_ref[...])[0].astype(BF16)
        hbt_ref[...] = hb.T
        for s in range(ns):
            a_ref[:, s * nc:(s + 1) * nc] = _dot(hb, w_ref[s]) + b_ref[:, s * nc:(s + 1) * nc]

    return _pcall(body, name="conv_in", grid=(T // tm,),
                  in_specs=[_rows(tm, D), _full((1, D)), _full(w1g.shape), _full((1, ns * nc))],
                  out_specs=[_cols(D, tm), _rows(tm, ns * nc)],
                  out_shape=[S_((D, T), BF16), S_((T, ns * nc), F32)], args=(x, g, w1g, b1))


def _conv_mid(a, x, wdw, bdw, lng, lnb, w2, b2, gn, tm, tps, taps, comm=None):
    T, D = x.shape
    hc = HALO_CONV

    def body(a_ref, ah_ref, x_ref, wdw_ref, bdw_ref, lng_ref, lnb_ref, w2_ref, b2_ref, gn_ref,
             c_ref, sbt_ref, x1_ref, hb_ref, hbt_ref, uext, shifted):
        i = pl.program_id(0)
        ah = ah_ref[...]
        uext[0:hc, :] = jnp.where(i % tps == 0, 0.0, ah[:, :D] * _sig(ah[:, D:]))
        av = a_ref[...]
        uext[hc:hc + tm, :] = av[:, :D] * _sig(av[:, D:])

        def conv(r0, src, ks, first):
            acc = jnp.broadcast_to(bdw_ref[...], (CHUNK, D)) if first else c_ref[pl.ds(r0, CHUNK), :]
            for k, o in ks:
                acc = acc + wdw_ref[k:k + 1, :] * src[pl.ds(pl.multiple_of(r0 + o, 8), CHUNK), :]
            c_ref[pl.ds(r0, CHUNK), :] = acc
        _by_sublane_shift(uext, shifted, tm, [hc - (taps - 1) + k for k in range(taps)], conv)

        cv = c_ref[...]
        xc = cv - jnp.mean(cv, axis=-1, keepdims=True)
        nh = xc * lax.rsqrt(jnp.mean(xc * xc, axis=-1, keepdims=True) + LN_EPS)
        n = nh * lng_ref[...] + lnb_ref[...]
        sb = (n * _sig(n)).astype(BF16)
        sbt_ref[...] = sb.T
        x1 = x_ref[...] + _dot(sb, w2_ref[...]) + b2_ref[...]
        x1_ref[...] = x1
        hb = _rms(x1, gn_ref[...])[0].astype(BF16)
        hb_ref[...] = hb
        hbt_ref[...] = hb.T

    return _pcall(body, name="conv_mid", grid=(T // tm,),
                  in_specs=[_rows(tm, 2 * D), _prev_halo(tm, hc, 2 * D), _rows(tm, D), _full(wdw.shape), _full((1, D)),
                            _full((1, D)), _full((1, D)), _full((D, D)), _full((1, D)), _full((1, D))],
                  out_specs=[_rows(tm, D), _cols(D, tm), _rows(tm, D), _rows(tm, D), _cols(D, tm)],
                  out_shape=[S_((T, D), F32), S_((D, T), BF16), S_((T, D), F32), S_((T, D), BF16), S_((D, T), BF16)],
                  scratch=[pltpu.VMEM((tm + hc, D), F32), pltpu.VMEM((tm + hc - 8, D), F32)],
                  args=(a, a, x, wdw, bdw, lng, lnb, w2, b2, gn), comm=comm)


def _mm_up(hb, wg, tm, name, comm=None):
    T, D = hb.shape
    ns, _, nc = wg.shape

    def body(h_ref, w_ref, o_ref):
        o_ref[...] = _dot(h_ref[...], w_ref[...])

    return _pcall(body, name=name, grid=(ns, T // tm),
                  in_specs=[pl.BlockSpec((tm, D), lambda s, i: (i, 0)), pl.BlockSpec((None, D, nc), lambda s, i: (s, 0, 0))],
                  out_specs=pl.BlockSpec((tm, nc), lambda s, i: (i, s)),
                  out_shape=S_((T, ns * nc), F32), args=(hb, wg), comm=comm)


def _ffn_taps(aext, realigned, tm, ffn_taps):
    for k in range(ffn_taps - 1):
        realigned[k] = aext[pl.ds(HALO_FFN - (ffn_taps - 1) + k, tm), :]
    return [realigned.at[k] for k in range(ffn_taps - 1)] + [aext.at[pl.ds(HALO_FFN, tm)]]


def _ffn_conv(taps, wdw_ref, bdw_ref):
    acc = bdw_ref[...]
    for k, tap in enumerate(taps):
        acc = acc + wdw_ref[k:k + 1, :] * tap[...]
    return acc


def _by_sublane_shift(ext, shifted, tm, offsets, per_chunk):
    first = True
    for b in range(8):
        ks = [(k, o - b) for k, o in enumerate(offsets) if o % 8 == b]
        if not ks:
            continue
        if b:
            shifted[...] = ext[pl.ds(b, shifted.shape[0]), :]
        _chunks(tm, functools.partial(per_chunk, src=shifted if b else ext, ks=ks, first=first))
        first = False


def _ffn_down(up, xin, wdw, bdw, wd, gn, tm, tps, ffn_taps, name, target=None, comm=None):
    T, D = xin.shape
    F = up.shape[1] // 2
    hf = HALO_FFN
    final = target is not None

    def body(*refs):
        if final:
            (up_ref, uph_ref, x_ref, wdw_ref, bdw_ref, wd_ref, gn_ref, tg_ref,
             gb_ref, dx_ref, loss_ref, dgn_ref, aext, realigned) = refs
        else:
            (up_ref, uph_ref, x_ref, wdw_ref, bdw_ref, wd_ref, gn_ref,
             gb_ref, xo_ref, h_ref, aext, realigned) = refs
        i = pl.program_id(0)
        aext[0:hf, :] = jnp.where(i % tps == 0, 0.0, uph_ref[...])
        aext[hf:hf + tm, :] = up_ref[:, 0:F]
        acv = _ffn_conv(_ffn_taps(aext, realigned, tm, ffn_taps), wdw_ref, bdw_ref)
        gb = (acv * _sig(acv) * up_ref[:, F:2 * F]).astype(BF16)
        gb_ref[...] = gb.T
        xo = x_ref[...] + _dot(gb, wd_ref[...])
        if not final:
            xo_ref[...] = xo
            h_ref[...] = _rms(xo, gn_ref[...])[0]
        else:
            out, xh, r = _rms(xo, gn_ref[...])
            diff = out - tg_ref[...]
            part = 0.5 / D * _colsum(diff * diff)
            dx, dg = _rms_bwd(diff * (1.0 / D), xh, r, gn_ref[...])
            dx_ref[...] = dx
            _accumulate(loss_ref, part, i == 0)
            _accumulate(dgn_ref, dg, i == 0)

    in_specs = [_rows(tm, 2 * F), _prev_halo(tm, hf, F), _rows(tm, D), _full(wdw.shape), _full((1, F)),
                _full((F, D)), _full((1, D))]
    args = [up, up, xin, wdw, bdw, wd, gn]
    if final:
        in_specs.append(_rows(tm, D))
        args.append(target)
        out_specs = [_cols(F, tm), _rows(tm, D), _acc_spec(1, D), _acc_spec(1, D)]
        out_shape = [S_((F, T), BF16), S_((T, D), F32), S_((1, D), F32), S_((1, D), F32)]
    else:
        out_specs = [_cols(F, tm), _rows(tm, D), _rows(tm, D)]
        out_shape = [S_((F, T), BF16), S_((T, D), F32), S_((T, D), F32)]
    return _pcall(body, name=name, grid=(T // tm,), in_specs=in_specs, out_specs=out_specs, out_shape=out_shape,
                  scratch=[pltpu.VMEM((tm + hf, F), F32), pltpu.VMEM((ffn_taps - 1, tm, F), F32)], args=args, comm=comm)


def _pool_counts(i, tps, tm, w):
    pos = (i % tps) * tm + lax.broadcasted_iota(jnp.int32, (tm, 1), 0)
    return jnp.minimum(pos + 1, w).astype(F32)


def _window_sums(src, buf_a, buf_b, cg, causal):
    assert POOL_WINDOWS == (2, 4, 8, 16)
    n = src.shape[0]
    levels = len(POOL_WINDOWS)
    cur, outs = src, []
    for l in range(levels):
        dst = buf_b if l % 2 else buf_a
        cols = slice(l * cg, levels * cg)
        lo, m = 8 * (l + 1), n - 8 * (l + 1)
        if causal:
            dst[lo:n, cols] = cur[lo:n, cols] + cur[pl.ds(lo - (1 << l), m), cols]
        else:
            dst[0:m, cols] = cur[0:m, cols] + cur[pl.ds(1 << l, m), cols]
        outs.append(dst)
        cur = dst
    return outs


def _pool_fwd(h, x2, pw, pb, ps, gn, tm, tps):
    T, D = h.shape
    ng, cg, _ = pw.shape
    hp = HALO_POOL

    def body(h_ref, hh_ref, x_ref, pw_ref, pb_ref, ps_ref, gn_ref, pl_ref, xo_ref, hb_ref, hbt_ref, hext, buf_a, buf_b):
        i = pl.program_id(0)
        hext[0:hp, :] = jnp.where(i % tps == 0, 0.0, hh_ref[...])
        hext[hp:hp + tm, :] = h_ref[...]
        sums = _window_sums(hext, buf_a, buf_b, cg, True)
        for g, w in enumerate(POOL_WINDOWS):
            cs = slice(g * cg, (g + 1) * cg)
            pl_ref[:, cs] = (sums[g][hp:hp + tm, cs] / _pool_counts(i, tps, tm, w) - h_ref[:, cs]).astype(BF16)

        pv = pl_ref[...]
        mixed = jnp.concatenate([_dot(pv[:, g * cg:(g + 1) * cg], pw_ref[g]) for g in range(ng)], axis=1)
        xo = x_ref[...] + ps_ref[...] * (mixed + pb_ref[...])
        xo_ref[...] = xo
        hb = _rms(xo, gn_ref[...])[0].astype(BF16)
        hb_ref[...] = hb
        hbt_ref[...] = hb.T

    return _pcall(body, name="pool_fwd", grid=(T // tm,),
                  in_specs=[_rows(tm, D), _prev_halo(tm, hp, D), _rows(tm, D), _full(pw.shape), _full((1, D)), _full((1, D)),
                            _full((1, D))],
                  out_specs=[_rows(tm, D), _rows(tm, D), _rows(tm, D), _cols(D, tm)],
                  out_shape=[S_((T, D), BF16), S_((T, D), F32), S_((T, D), BF16), S_((D, T), BF16)],
                  scratch=[pltpu.VMEM((tm + hp, D), F32)] * 3, args=(h, h, x2, pw, pb, ps, gn))


def _bwd_down(dx, up, wdw, bdw, wd, tm, tps, ffn_taps, name, comm=None):
    T, D = dx.shape
    F = up.shape[1] // 2
    hf = HALO_FFN
    nt = T // tm

    def body(dx_ref, up_ref, uph_ref, wdw_ref, bdw_ref, wd_ref, dac_ref, dgate_ref, wsum_ref, aext, realigned):
        i = pl.program_id(0)
        aext[0:hf, :] = jnp.where(i % tps == 0, 0.0, uph_ref[...])
        aext[hf:hf + tm, :] = up_ref[:, 0:F]
        taps = _ffn_taps(aext, realigned, tm, ffn_taps)
        acv = _ffn_conv(taps, wdw_ref, bdw_ref)
        dg = _dot_nt(dx_ref[...].astype(BF16), wd_ref[...])
        sg = _sig(acv)
        dgate_ref[...] = (dg * acv * sg).astype(BF16)
        dac = dg * up_ref[:, F:2 * F] * (sg * (1.0 + acv * (1.0 - sg)))
        dac_ref[...] = dac
        rows = [_colsum(dac * tap[...]) for tap in taps]
        rows += [_colsum(dac), jnp.zeros((8 - ffn_taps - 1, F), F32)]
        _accumulate(wsum_ref, jnp.concatenate(rows, axis=0), i == 0)

    return _pcall(body, name=name, grid=(nt,),
                  in_specs=[_rows(tm, D), _rows(tm, 2 * F), _prev_halo(tm, hf, F), _full(wdw.shape), _full((1, F)),
                            _full((F, D))],
                  out_specs=[_rows(tm, F), _rows(tm, F, col=1), _acc_spec(8, F)],
                  out_shape=[S_((T, F), F32), S_((T, 2 * F), BF16), S_((8, F), F32)],
                  scratch=[pltpu.VMEM((tm + hf, F), F32), pltpu.VMEM((ffn_taps - 1, tm, F), F32)],
                  args=(dx, up, up, wdw, bdw, wd), comm=comm)


def _bwd_up(dac, dup, wg, wdw, xin, gn, dx, tm, tps, ffn_taps, name):
    T, D = xin.shape
    F = dac.shape[1]
    ns, _, nc = wg.shape
    hf = HALO_FFN
    nt = T // tm

    def body(dac_ref, dach_ref, dgate_ref, wg_ref, wdw_ref, x_ref, gn_ref, dx_ref, dact_ref, dxo_ref, dgn_ref, dext):
        i = pl.program_id(0)
        dext[0:tm, :] = dac_ref[...]
        dext[tm:tm + hf, :] = jnp.where(i % tps == tps - 1, 0.0, dach_ref[...])
        dact = wdw_ref[ffn_taps - 1:ffn_taps, :] * dac_ref[...]
        for k in range(ffn_taps - 1):
            dact = dact + wdw_ref[k:k + 1, :] * dext[pl.ds(ffn_taps - 1 - k, tm), :]
        dact_ref[...] = dact.astype(BF16)

        dh = jnp.zeros((tm, D), F32)
        for s in range(ns):
            src = dact_ref if s < ns // 2 else dgate_ref
            o = (s % (ns // 2)) * nc
            dh = dh + _dot_nt(src[:, o:o + nc], wg_ref[s])
        _, xh, r = _rms(x_ref[...], gn_ref[...])
        dxn, dg = _rms_bwd(dh, xh, r, gn_ref[...])
        dxo_ref[...] = dx_ref[...] + dxn
        _accumulate(dgn_ref, dg, i == 0)

    return _pcall(body, name=name, grid=(nt,),
                  in_specs=[_rows(tm, F), _next_halo(tm, hf, F, T), _rows(tm, F, col=1), _full(wg.shape), _full(wdw.shape),
                            _rows(tm, D), _full((1, D)), _rows(tm, D)],
                  out_specs=[_rows(tm, F), _rows(tm, D), _acc_spec(1, D)],
                  out_shape=[S_((T, 2 * F), BF16), S_((T, D), F32), S_((1, D), F32)],
                  scratch=[pltpu.VMEM((tm + hf, F), F32)],
                  aliases={2: 0}, args=(dac, dac, dup, wg, wdw, xin, gn, dx))


def _mm_tn(xt, dy, bk, bn, tt, shard_major, name):
    K, T = xt.shape
    N = dy.shape[1]
    nt = T // tt

    def body(x_ref, dy_ref, o_ref, acc):
        t = pl.program_id(2)
        part = _dot(x_ref[...], dy_ref[...].astype(BF16))
        _accumulate(acc, part, t == 0)

        @pl.when(t == nt - 1)
        def _():
            o_ref[...] = acc[...].astype(o_ref.dtype)

    if shard_major:
        out_spec = pl.BlockSpec((None, bk, bn), lambda i, j, t: (j, i, 0))
        out_shape = S_((N // bn, K, bn), BF16)
    else:
        out_spec = pl.BlockSpec((bk, bn), lambda i, j, t: (i, j))
        out_shape = S_((K, N), BF16)
    return _pcall(body, name=name, grid=(K // bk, N // bn, nt),
                  in_specs=[pl.BlockSpec((bk, tt), lambda i, j, t: (i, t)), pl.BlockSpec((tt, bn), lambda i, j, t: (t, j))],
                  out_specs=out_spec, out_shape=out_shape, scratch=[pltpu.VMEM((bk, bn), F32)], args=(xt, dy))


def _pool_bwd(dx3, plb, x2, pw, pb, ps, gn, tm, tps):
    T, D = x2.shape
    ng, cg, _ = pw.shape
    hp = HALO_POOL
    nt = T // tm

    def body(do_ref, doh_ref, pl_ref, x_ref, pw_ref, pb_ref, ps_ref, gn_ref, dxo_ref, dpw_ref, sm_ref, qext, buf_a, buf_b,
             dh_s):
        i = pl.program_id(0)
        do = do_ref[...]
        dm = do * ps_ref[...]
        dmh = jnp.where(i % tps == tps - 1, 0.0, doh_ref[...]) * ps_ref[...]
        pv = pl_ref[...]
        mixed = jnp.concatenate([_dot(pv[:, g * cg:(g + 1) * cg], pw_ref[g]) for g in range(ng)], axis=1)
        dscale = _colsum(do * (mixed + pb_ref[...]))
        dbias = _colsum(dm)
        dmb = dm.astype(BF16)
        dmhb = dmh.astype(BF16)
        for g, w in enumerate(POOL_WINDOWS):
            cs = slice(g * cg, (g + 1) * cg)
            _accumulate(dpw_ref.at[g], _dot_tn(pv[:, cs], dmb[:, cs]), i == 0)
            dpo = _dot_nt(dmb[:, cs], pw_ref[g])
            dh_s[:, cs] = dpo
            qext[0:tm, cs] = dpo / _pool_counts(i, tps, tm, w)
            qext[tm:tm + hp, cs] = _dot_nt(dmhb[:, cs], pw_ref[g]) * (1.0 / w)
        sums = _window_sums(qext, buf_a, buf_b, cg, False)
        for g in range(ng):
            cs = slice(g * cg, (g + 1) * cg)
            dh_s[:, cs] = sums[g][0:tm, cs] - dh_s[:, cs]

        _, xh, r = _rms(x_ref[...], gn_ref[...])
        dxn, dg = _rms_bwd(dh_s[...], xh, r, gn_ref[...])
        dxo_ref[...] = do + dxn
        sm = jnp.concatenate([dscale, dbias, dg, jnp.zeros((5, D), F32)], axis=0)
        _accumulate(sm_ref, sm, i == 0)

    return _pcall(body, name="pool_bwd", grid=(nt,),
                  in_specs=[_rows(tm, D), _next_halo(tm, hp, D, T), _rows(tm, D), _rows(tm, D), _full(pw.shape), _full((1, D)),
                            _full((1, D)), _full((1, D))],
                  out_specs=[_rows(tm, D), pl.BlockSpec((ng, cg, cg), lambda i: (0, 0, 0)), _acc_spec(8, D)],
                  out_shape=[S_((T, D), F32), S_((ng, cg, cg), F32), S_((8, D), F32)],
                  scratch=[pltpu.VMEM((tm + hp, D), F32)] * 3 + [pltpu.VMEM((tm, D), F32)],
                  args=(dx3, dx3, plb, x2, pw, pb, ps, gn))


def _conv_bwd1(dx1, c, w2, lng, lnb, tm):
    T, D = dx1.shape

    def body(do_ref, c_ref, w2_ref, lng_ref, lnb_ref, dc_ref, sm_ref):
        i = pl.program_id(0)
        do = do_ref[...]
        ds = _dot_nt(do.astype(BF16), w2_ref[...])
        cv = c_ref[...]
        xc = cv - jnp.mean(cv, axis=-1, keepdims=True)
        rstd = lax.rsqrt(jnp.mean(xc * xc, axis=-1, keepdims=True) + LN_EPS)
        nh = xc * rstd
        n = nh * lng_ref[...] + lnb_ref[...]
        sg = _sig(n)
        dn = ds * (sg * (1.0 + n * (1.0 - sg)))
        dnh = dn * lng_ref[...]
        dc = rstd * (dnh - jnp.mean(dnh, axis=-1, keepdims=True) - nh * jnp.mean(dnh * nh, axis=-1, keepdims=True))
        dc_ref[...] = dc
        sm = jnp.concatenate([_colsum(dn * nh), _colsum(dn), _colsum(do), _colsum(dc), jnp.zeros((4, D), F32)], axis=0)
        _accumulate(sm_ref, sm, i == 0)

    return _pcall(body, name="conv_bwd1", grid=(T // tm,),
                  in_specs=[_rows(tm, D), _rows(tm, D), _full((D, D)), _full((1, D)), _full((1, D))],
                  out_specs=[_rows(tm, D), _acc_spec(8, D)],
                  out_shape=[S_((T, D), F32), S_((8, D), F32)], args=(dx1, c, w2, lng, lnb))


def _conv_bwd2(dc, a, x, w1g, wdw, gn, dx1, tm, tps, taps, comm=None):
    T, D = x.shape
    ns, _, nc = w1g.shape
    hc = HALO_CONV
    nt = T // tm
    tp = wdw.shape[0]

    def body(dc_ref, dch_ref, a_ref, ah_ref, x_ref, w_ref, wdw_ref, gn_ref, dx1_ref,
             gx_ref, dab_ref, dw_ref, db1_ref, dgn_ref, uext, dext, shifted, du_s, wacc):
        i = pl.program_id(0)
        ah = ah_ref[...]
        uext[0:hc, :] = jnp.where(i % tps == 0, 0.0, ah[:, :D] * _sig(ah[:, D:]))
        av = a_ref[...]
        sg2 = _sig(av[:, D:])
        uext[hc:hc + tm, :] = av[:, :D] * sg2
        dext[0:tm, :] = dc_ref[...]
        dext[tm:tm + hc, :] = jnp.where(i % tps == tps - 1, 0.0, dch_ref[...])

        @pl.when(i == 0)
        def _():
            wacc[...] = jnp.zeros_like(wacc)

        def wgrad(r0, src, ks, first):
            d = dc_ref[pl.ds(r0, CHUNK), :]
            for k, o in ks:
                wacc[8 * k:8 * k + 8, :] += _fold8(d * src[pl.ds(pl.multiple_of(r0 + o, 8), CHUNK), :])
        _by_sublane_shift(uext, shifted, tm, [hc - (taps - 1) + k for k in range(taps)], wgrad)

        def convt(r0, src, ks, first):
            acc = jnp.zeros((CHUNK, D), F32) if first else du_s[pl.ds(r0, CHUNK), :]
            for k, o in ks:
                acc = acc + wdw_ref[k:k + 1, :] * src[pl.ds(pl.multiple_of(r0 + o, 8), CHUNK), :]
            du_s[pl.ds(r0, CHUNK), :] = acc
        _by_sublane_shift(dext, shifted, tm, [taps - 1 - k for k in range(taps)], convt)

        du = du_s[...]
        da1 = du * sg2
        da2 = du * av[:, :D] * (sg2 * (1.0 - sg2))
        dab_ref[:, 0:D] = da1.astype(BF16)
        dab_ref[:, D:2 * D] = da2.astype(BF16)
        _accumulate(db1_ref, jnp.concatenate([_colsum(da1), _colsum(da2)], axis=0), i == 0)
        dh = jnp.zeros((tm, D), F32)
        for s in range(ns):
            dh = dh + _dot_nt(dab_ref[:, s * nc:(s + 1) * nc], w_ref[s])
        _, xh, r = _rms(x_ref[...], gn_ref[...])
        dxn, dg = _rms_bwd(dh, xh, r, gn_ref[...])
        gx_ref[...] = dx1_ref[...] + dxn
        _accumulate(dgn_ref, dg, i == 0)

        @pl.when(i == nt - 1)
        def _():
            dw_ref[...] = jnp.zeros_like(dw_ref)
            for k in range(taps):
                dw_ref[k:k + 1, :] = _colsum(wacc[8 * k:8 * k + 8, :])

    return _pcall(body, name="conv_bwd2", grid=(nt,),
                  in_specs=[_rows(tm, D), _next_halo(tm, hc, D, T), _rows(tm, 2 * D), _prev_halo(tm, hc, 2 * D), _rows(tm, D),
                            _full(w1g.shape), _full(wdw.shape), _full((1, D)), _rows(tm, D)],
                  out_specs=[_rows(tm, D), _rows(tm, 2 * D), _acc_spec(tp, D), _acc_spec(2, D), _acc_spec(1, D)],
                  out_shape=[S_((T, D), F32), S_((T, 2 * D), BF16), S_((tp, D), F32), S_((2, D), F32), S_((1, D), F32)],
                  scratch=[pltpu.VMEM((tm + hc, D), F32), pltpu.VMEM((tm + hc, D), F32), pltpu.VMEM((tm + hc - 8, D), F32),
                           pltpu.VMEM((tm, D), F32), pltpu.VMEM((8 * taps, D), F32)],
                  args=(dc, dc, a, a, x, w1g, wdw, gn, dx1), comm=comm)


def _place():
    x, y, c = lax.axis_index("x"), lax.axis_index("y"), lax.axis_index("c")
    chips = [(1 - x, y), (x, 1 - y), (1 - x, 1 - y)]
    return x, y, c, chips


def _own_slot(w, layer, ids, dtype, name):
    _, r, c = w.shape
    br = _row_block(r, c)

    def body(ids_ref, w_ref, o_ref):
        o_ref[...] = w_ref[...].astype(dtype)

    return _pcall(body, name=name, grid=(r // br,), prefetch=1,
                  in_specs=[pl.BlockSpec((None, br, c), lambda i, ids: (layer, i, 0))],
                  out_specs=pl.BlockSpec((None, br, c), lambda i, ids: (ids[0], i, 0)),
                  out_shape=S_((N_CHIPS, r, c), dtype), args=(ids, w))


class _GatherComm:
    def __init__(self, slots):
        n = len(slots)
        self.inputs = list(slots)
        self.out_shape = [S_(s.shape, s.dtype) for s in slots]
        self.aliases = {i: i for i in range(n)}
        self.sems = [pltpu.SemaphoreType.DMA((n, 3))] * 4
        self.halves = [s.shape[1] // 2 for s in slots]

    def _copies(self, outs, sems):
        s_ici, r_ici, s_d2d, r_d2d = sems
        x, y, c, chips = _place()
        peers = [(j, chip, 2 * chip[0] + chip[1]) for j, chip in enumerate(chips)]

        def half(i, h):
            return pl.ds(pl.multiple_of(h * self.halves[i], 8), self.halves[i])

        def ici(i, j, chip, to):
            blk = outs[i].at[chip, half(i, c)]
            return pltpu.make_async_remote_copy(src_ref=blk, dst_ref=blk, send_sem=s_ici.at[i, j], recv_sem=r_ici.at[i, j],
                                                device_id=to, device_id_type=MESH)

        def d2d(i, j, chip, h):
            blk = outs[i].at[chip, half(i, h)]
            return pltpu.make_async_remote_copy(src_ref=blk, dst_ref=blk, send_sem=s_d2d.at[i, j], recv_sem=r_d2d.at[i, j],
                                                device_id=(x, y, 1 - c), device_id_type=MESH)
        return (x, y, c, 2 * x + y), peers, ici, d2d

    def before(self, cin, outs, sems, step, total):
        (x, y, c, q), peers, ici, d2d = self._copies(outs, sems)

        @pl.when(step == 0)
        def _():
            for i in range(len(outs)):
                for j, chip, _ in peers:
                    ici(i, j, q, (*chip, c)).start()

    def after(self, cin, outs, sems, step, total):
        (x, y, c, q), peers, ici, d2d = self._copies(outs, sems)

        @pl.when(step == max(total - 2, 0))
        def _():
            for i in range(len(outs)):
                for j, chip, qj in peers:
                    ici(i, j, qj, (x, y, c)).wait_recv()
                    d2d(i, j, qj, c).start()

        @pl.when(step == total - 1)
        def _():
            for i in range(len(outs)):
                for j, chip, qj in peers:
                    d2d(i, j, qj, 1 - c).wait_recv()
            for i in range(len(outs)):
                for j, chip, qj in peers:
                    ici(i, j, q, (*chip, c)).wait_send()
                    d2d(i, j, qj, c).wait_send()


class _ExchangeComm:
    def __init__(self, sums):
        n = len(sums)
        self.inputs = list(sums)
        self.out_shape = [S_((3,) + s.shape[1:], s.dtype) for s in sums]
        self.aliases = {}
        self.sems = [pltpu.SemaphoreType.DMA((n, 3))] * 2

    def _copies(self, srcs, outs, sems):
        s_sem, r_sem = sems
        x, y, c, chips = _place()
        return [pltpu.make_async_remote_copy(src_ref=srcs[i].at[2 * chip[0] + chip[1]], dst_ref=outs[i].at[j],
                                             send_sem=s_sem.at[i, j], recv_sem=r_sem.at[i, j],
                                             device_id=(*chip, c), device_id_type=MESH)
                for i in range(len(srcs)) for j, chip in enumerate(chips)]

    def before(self, srcs, outs, sems, step, total):
        @pl.when(step == 0)
        def _():
            for cp in self._copies(srcs, outs, sems):
                cp.start()

    def after(self, srcs, outs, sems, step, total):
        @pl.when(step == total - 1)
        def _():
            for cp in self._copies(srcs, outs, sems):
                cp.wait()


def _comm_only(comm, name):
    def body():
        pass
    outs = _pcall(body, name=name, grid=(1,), in_specs=[], out_specs=[], out_shape=[], args=[], comm=comm)
    return list(outs)


def _pair_exchange(parts, name):
    n = len(parts)

    def body(*refs):
        srcs, outs = refs[:n], refs[n:2 * n]
        s_sem, r_sem = refs[2 * n:]
        x, y, c, _ = _place()
        cps = []
        for i in range(n):
            for s in range(N_CHIPS):
                cp = pltpu.make_async_remote_copy(src_ref=srcs[i].at[s, 1 - c], dst_ref=outs[i].at[s], send_sem=s_sem.at[i, s],
                                                  recv_sem=r_sem.at[i, s], device_id=(x, y, 1 - c), device_id_type=MESH)
                cp.start()
                cps.append(cp)
        for cp in cps:
            cp.wait()

    any_spec = pl.BlockSpec(memory_space=pl.ANY)
    outs = pl.pallas_call(
        body, name=name, in_specs=[any_spec] * n, out_specs=[any_spec] * n,
        out_shape=[S_((N_CHIPS,) + p.shape[2:], p.dtype) for p in parts],
        scratch_shapes=[pltpu.SemaphoreType.DMA((n, N_CHIPS))] * 2,
    )(*parts)
    return list(outs)


def _share_halves(bufs):
    n = len(bufs)
    nl = max(b.shape[0] for b in bufs)

    def body(*refs):
        outs = refs[n:2 * n]
        s_sem, r_sem = refs[2 * n:]
        x, y, c, _ = _place()
        cps = []
        for i in range(n):
            for l in range(bufs[i].shape[0]):
                blk = outs[i].at[l, c]
                cp = pltpu.make_async_remote_copy(src_ref=blk, dst_ref=blk, send_sem=s_sem.at[i, l], recv_sem=r_sem.at[i, l],
                                                  device_id=(x, y, 1 - c), device_id_type=MESH)
                cp.start()
                cps.append(cp)
        for cp in cps:
            cp.wait()

    any_spec = pl.BlockSpec(memory_space=pl.ANY)
    return pl.pallas_call(
        body, name="grad_share_halves", in_specs=[any_spec] * n, out_specs=[any_spec] * n,
        out_shape=[S_(b.shape, b.dtype) for b in bufs], input_output_aliases={i: i for i in range(n)},
        scratch_shapes=[pltpu.SemaphoreType.DMA((n, nl))] * 2,
    )(*bufs)


def _small_allreduce(groups):
    n = len(groups)
    arrays = []
    for grp in groups:
        for p in grp:
            if not isinstance(p, int) and not any(p[0] is a for a in arrays):
                arrays.append(p[0])
    shapes = []
    for grp in groups:
        rows = sum(p if isinstance(p, int) else p[2] - p[1] for p in grp)
        width = next(p[0].shape[1] for p in grp if not isinstance(p, int))
        assert rows % 8 == 0
        shapes.append((rows, width))
    na = len(arrays)

    def body(*refs):
        a_refs, o_refs = refs[:na], refs[na:na + n]
        rest = refs[na + n:]
        v_refs, sibs, css, gots = rest[:n], rest[n:2 * n], rest[2 * n:3 * n], rest[3 * n:4 * n]
        s_sem, r_sem = rest[4 * n:]
        for i, grp in enumerate(groups):
            o = 0
            for p in grp:
                if isinstance(p, int):
                    v_refs[i][o:o + p, :] = jnp.zeros((p, shapes[i][1]), F32)
                    o += p
                else:
                    k = next(t for t, a in enumerate(arrays) if a is p[0])
                    v_refs[i][o:o + p[2] - p[1], :] = a_refs[k][p[1]:p[2], :]
                    o += p[2] - p[1]
        x, y, c, chips = _place()
        q = 2 * x + y
        to_sib = [pltpu.make_async_remote_copy(src_ref=v_refs[i], dst_ref=sibs[i], send_sem=s_sem.at[i, 3], recv_sem=r_sem.at[i, 3],
                                               device_id=(x, y, 1 - c), device_id_type=MESH) for i in range(n)]
        for cp in to_sib:
            cp.start()
        cps = []
        for i in range(n):
            to_sib[i].wait()
            mine, other = v_refs[i][...], sibs[i][...]
            css[i][...] = jnp.where(c == 0, mine, other) + jnp.where(c == 0, other, mine)
            for j, chip in enumerate(chips):
                cp = pltpu.make_async_remote_copy(src_ref=css[i], dst_ref=gots[i].at[j], send_sem=s_sem.at[i, j],
                                                  recv_sem=r_sem.at[i, j], device_id=(*chip, c), device_id_type=MESH)
                cp.start()
                cps.append(cp)
        for cp in cps:
            cp.wait()
        flips = [2, 1, 3]
        for i in range(n):
            total = None
            for k in range(N_CHIPS):
                d = q ^ k
                term = jnp.where(d == 0, css[i][...], 0.0)
                for j in range(3):
                    term = jnp.where(d == flips[j], gots[i][j], term)
                total = term if total is None else total + term
            o_refs[i][...] = total

    vm = pl.BlockSpec(memory_space=pltpu.VMEM)
    return pl.pallas_call(
        body, name="small_allreduce", in_specs=[vm] * na, out_specs=[vm] * n, out_shape=[S_(s, F32) for s in shapes],
        scratch_shapes=[pltpu.VMEM(s, F32) for s in shapes] * 3 + [pltpu.VMEM((3,) + s, F32) for s in shapes]
        + [pltpu.SemaphoreType.DMA((n, 4)), pltpu.SemaphoreType.DMA((n, 4))],
    )(*arrays)


def _row_block(r, c, target_bytes=1 << 20):
    br = r
    while br % 32 == 0 and br * c * 4 > target_bytes:
        br //= 2
    return br


def _chip_sum(part, sib, ids, name):
    _, _, r, c = part.shape
    br = _row_block(r, c)

    def body(ids_ref, p_ref, s_ref, o_ref):
        o_ref[...] = (p_ref[...].astype(F32) + s_ref[...].astype(F32)).astype(BF16)

    return _pcall(body, name=name, grid=(N_CHIPS, r // br), prefetch=1,
                  in_specs=[pl.BlockSpec((None, None, br, c), lambda s, i, ids: (s, ids[1], i, 0)),
                            pl.BlockSpec((None, br, c), lambda s, i, ids: (s, i, 0))],
                  out_specs=pl.BlockSpec((None, br, c), lambda s, i, ids: (s, i, 0)),
                  out_shape=S_((N_CHIPS, r, c), BF16), args=(ids, part, sib))


def _final_sum(part, sib, got, ids, name, layer=0, n_layers=1, buf=None):
    _, r, c = got.shape
    br = _row_block(r, c)

    def body(ids_ref, p_ref, s_ref, got_ref, *rest):
        v = p_ref[...].astype(F32) + s_ref[...].astype(F32)
        for j in range(3):
            v = v + got_ref[j].astype(F32)
        rest[-1][...] = v

    in_specs = [pl.BlockSpec((None, None, br, c), lambda i, ids: (ids[0], ids[1], i, 0)),
                pl.BlockSpec((None, br, c), lambda i, ids: (ids[0], i, 0)),
                pl.BlockSpec((3, br, c), lambda i, ids: (0, i, 0))]
    args = [ids, part, sib, got]
    if buf is not None:
        in_specs.append(pl.BlockSpec(memory_space=pl.ANY))
        args.append(buf)
    return _pcall(body, name=name, grid=(r // br,), prefetch=1, in_specs=in_specs,
                  out_specs=pl.BlockSpec((None, None, br, c), lambda i, ids: (layer, ids[1], i, 0)),
                  out_shape=S_((n_layers, 2, r, c), F32), args=args, aliases={4: 0} if buf is not None else None)


def _adam_math(w, g, m, v):
    c1 = 1.0 / (1.0 - ADAM_B1 ** ADAM_STEP)
    c2 = 1.0 / (1.0 - ADAM_B2 ** ADAM_STEP)
    mn = ADAM_B1 * m + (1.0 - ADAM_B1) * g
    vn = ADAM_B2 * v + (1.0 - ADAM_B2) * (g * g)
    return -ADAM_LR * ((mn * c1) / (jnp.sqrt(vn * c2) + ADAM_EPS) + ADAM_WD * w), mn, vn


def _adamw(w, g, m, v, name):
    nl, r, c = w.shape
    br = _row_block(r, c, 1 << 19)

    def body(w_ref, g_ref, m_ref, v_ref, d_ref, mo_ref, vo_ref):
        d_ref[...], mo_ref[...], vo_ref[...] = _adam_math(w_ref[...], g_ref[...], m_ref[...], v_ref[...])

    blk = pl.BlockSpec((None, br, c), lambda l, i: (l, i, 0))
    return _pcall(body, name=name, grid=(nl, r // br), in_specs=[blk] * 4, out_specs=[blk] * 3,
                  out_shape=[S_((nl, r, c), F32)] * 3, args=(w, g, m, v))


def _adamw_small(rd, rf, sharded_g, ws, ms, vs, rows_d, rows_f, loss_row):
    n = len(ws)
    ns = len(sharded_g)
    rep = [k for k in range(n) if rows_d[k] is not None or rows_f[k] is not None]

    def body(*refs):
        rd_ref, rf_ref = refs[0], refs[1]
        sg = refs[2:2 + ns]
        w_refs, m_refs, v_refs = (refs[2 + ns + t * n:2 + ns + (t + 1) * n] for t in range(3))
        loss_ref, outs = refs[2 + ns + 3 * n], refs[3 + ns + 3 * n:]
        g_out, d_out, m_out, v_out = outs[:len(rep)], outs[len(rep):len(rep) + n], outs[len(rep) + n:len(rep) + 2 * n], \
            outs[len(rep) + 2 * n:]
        loss_ref[...] = jnp.sum(rd_ref[loss_row:loss_row + 1, :], axis=1, keepdims=True)
        si = 0
        for k in range(n):
            shape = w_refs[k].shape
            if rows_d[k] is not None or rows_f[k] is not None:
                src, rws = (rd_ref, rows_d[k]) if rows_d[k] is not None else (rf_ref, rows_f[k])
                axis = 0 if shape[0] == len(rws) else 1
                g = src[rws[0]:rws[0] + 1, :]
                if len(rws) > 1:
                    g = jnp.concatenate([src[r:r + 1, :] for r in rws], axis=axis)
                g_out[rep.index(k)][...] = g
            else:
                g = sg[si][...]
                si += 1
            d_out[k][...], m_out[k][...], v_out[k][...] = _adam_math(w_refs[k][...], g, m_refs[k][...], v_refs[k][...])

    vm = pl.BlockSpec(memory_space=pltpu.VMEM)
    shapes = [S_(w.shape, F32) for w in ws]
    outs = pl.pallas_call(
        body, name="adamw_small", in_specs=[vm] * (2 + ns + 3 * n), out_specs=[vm] * (1 + len(rep) + 3 * n),
        out_shape=[S_((1, 1), F32)] + [shapes[k] for k in rep] + shapes * 3,
    )(rd, rf, *sharded_g, *ws, *ms, *vs)
    g_rep = dict(zip(rep, outs[1:1 + len(rep)]))
    o = outs[1 + len(rep):]
    return outs[0], g_rep, o[:n], o[n:2 * n], o[2 * n:]


def _pack(pieces, rows):
    flat = jnp.concatenate([p.reshape(-1).astype(F32) for p in pieces])
    return jnp.pad(flat, (0, rows * 128 - flat.shape[0])).reshape(rows, 128)


def _unpack(packed, shapes):
    flat = packed.reshape(-1)
    out, o = [], 0
    for s in shapes:
        n = 1
        for d in s:
            n *= d
        out.append(flat[o:o + n].reshape(s))
        o += n
    return out


def _rows_for(pieces_or_shapes):
    n = 0
    for p in pieces_or_shapes:
        k = 1
        for d in (p if isinstance(p, tuple) else p.shape):
            k *= d
        n += k
    return -(-n // 1024) * 8


def kernel(x, norm_mix, norm_ffn, conv_w_pw1, conv_b_pw1, conv_w_dw, conv_b_dw, conv_ln_g, conv_ln_b, conv_w_pw2, conv_b_pw2, pool_w, pool_b, pool_scale, ffn_w_up, ffn_w_dw, ffn_b_dw, ffn_w_down, final_norm, loss_target, m_norm_mix, m_norm_ffn, m_conv_w_pw1, m_conv_b_pw1, m_conv_w_dw, m_conv_b_dw, m_conv_ln_g, m_conv_ln_b, m_conv_w_pw2, m_conv_b_pw2, m_pool_w, m_pool_b, m_pool_scale, m_ffn_w_up, m_ffn_w_dw, m_ffn_b_dw, m_ffn_w_down, m_final_norm, v_norm_mix, v_norm_ffn, v_conv_w_pw1, v_conv_b_pw1, v_conv_w_dw, v_conv_b_dw, v_conv_ln_g, v_conv_ln_b, v_conv_w_pw2, v_conv_b_pw2, v_pool_w, v_pool_b, v_pool_scale, v_ffn_w_up, v_ffn_w_dw, v_ffn_b_dw, v_ffn_w_down, v_final_norm):
    nb, seq, D = x.shape
    T = nb * seq
    F = ffn_w_down.shape[1] * N_CHIPS
    taps = conv_w_dw.shape[1]
    ffn_taps = ffn_w_dw.shape[1]
    ng = pool_w.shape[1]
    cg = pool_w.shape[3]
    dsh = D // N_CHIPS
    fsh = F // N_CHIPS
    assert taps - 1 <= HALO_CONV and ffn_taps - 1 <= HALO_FFN and max(POOL_WINDOWS) <= HALO_POOL
    tm = min(TILE_D, seq)
    tf = min(TILE_F, seq)
    assert seq % tm == 0 and seq % tf == 0 and tm % HALO_CONV == 0 and tf % CHUNK == 0
    tps, tpf = seq // tm, seq // tf
    tw = min(TILE_MM, T)
    assert T % tw == 0

    xi, yi, ci = lax.axis_index("x"), lax.axis_index("y"), lax.axis_index("c")
    qi = 2 * xi + yi
    ids = jnp.stack([qi, ci]).astype(jnp.int32)
    x2d = x.reshape(T, D)
    tg2d = loss_target.reshape(T, D)

    small_sharded = [conv_w_dw[0], ffn_w_dw, pool_b, pool_scale]
    srows = 2 * _rows_for(small_sharded)
    s_pw1, s_pw2 = _own_slot(conv_w_pw1, 0, ids, BF16, "slot_pw1"), _own_slot(conv_w_pw2, 0, ids, BF16, "slot_pw2")
    s_small = _own_slot(_pack(small_sharded, srows)[None], 0, ids, F32, "slot_small")
    s_pool = _own_slot(pool_w.reshape(1, ng * (cg // N_CHIPS), cg), 0, ids, BF16, "slot_pool")
    s_up = [_own_slot(ffn_w_up, l, ids, BF16, "slot_up%d" % l) for l in range(2)]
    s_down = [_own_slot(ffn_w_down, l, ids, BF16, "slot_down%d" % l) for l in range(2)]
    w1g, w2g, smallg = _comm_only(_GatherComm([s_pw1, s_pw2, s_small]), "gather_conv")
    w2f = w2g.reshape(D, D)
    sm_parts = [_unpack(smallg[s], [(taps, dsh), (2, ffn_taps, fsh), (1, dsh), (1, dsh)]) for s in range(N_CHIPS)]
    wdw_f = jnp.concatenate([p[0] for p in sm_parts], axis=1)
    wdw_f = jnp.pad(wdw_f, ((0, HALO_CONV - taps), (0, 0)))
    fdw_f = jnp.concatenate([p[1] for p in sm_parts], axis=2)
    fdw_f = jnp.pad(fdw_f, ((0, 0), (0, 8 - ffn_taps), (0, 0)))
    pb_f = jnp.concatenate([p[2] for p in sm_parts], axis=1)
    ps_f = jnp.concatenate([p[3] for p in sm_parts], axis=1)

    hb0t, a = _conv_in(x2d, norm_mix[0:1], w1g, conv_b_pw1, tm)
    c, sbt, x1, hb1, hb1t, wup0, wd0, pwg = _conv_mid(a, x2d, wdw_f, conv_b_dw, conv_ln_g, conv_ln_b, w2f, conv_b_pw2,
                                                      norm_ffn[0:1], tm, tps, taps,
                                                      comm=_GatherComm([s_up[0], s_down[0], s_pool]))
    up0, wup1 = _mm_up(hb1, wup0, tw, "ffn_up0", comm=_GatherComm([s_up[1]]))
    gb0t, x2, h2, wd1 = _ffn_down(up0, x1, fdw_f[0], ffn_b_dw[0:1], wd0.reshape(F, D), norm_mix[1:2], tf, tpf, ffn_taps,
                                  "ffn_down0", comm=_GatherComm([s_down[1]]))
    wup = [wup0, wup1]
    wdn = [wd0.reshape(F, D), wd1.reshape(F, D)]
    pwf = pwg.reshape(N_CHIPS, ng, cg // N_CHIPS, cg).transpose(1, 0, 2, 3).reshape(ng, cg, cg)
    plb, x3, hb3, hb3t = _pool_fwd(h2, x2, pwf, pb_f, ps_f, norm_ffn[1:2], tm, tps)
    up1 = _mm_up(hb3, wup[1], tw, "ffn_up1")
    gb1t, dx4, loss_part, d_final = _ffn_down(up1, x3, fdw_f[1], ffn_b_dw[1:2], wdn[1], final_norm.reshape(1, D), tf, tpf,
                                              ffn_taps, "ffn_down1", target=tg2d)

    dac1, dup1, fw1 = _bwd_down(dx4, up1, fdw_f[1], ffn_b_dw[1:2], wdn[1], tf, tpf, ffn_taps, "bwd_down1")
    p_down1 = _mm_tn(gb1t, dx4, F // 2, D, tw, False, "dw_down1")
    dup1, dx3, dnf1 = _bwd_up(dac1, dup1, wup[1], fdw_f[1], x3, norm_ffn[1:2], dx4, tf, tpf, ffn_taps, "bwd_up1")
    p_up1 = _mm_tn(hb3t, dup1, D, 2 * fsh, tw, True, "dw_up1")
    fs = {}

    def chip_sums(group, tag):
        nms = list(group)
        from_sib = _pair_exchange([group[nm] for nm in nms], "grad_pair_exchange_" + tag)
        sums = [_chip_sum(group[nm], s, ids, "chip_sum_" + nm) for nm, s in zip(nms, from_sib)]
        for nm, s in zip(nms, from_sib):
            fs[nm] = (group[nm], s)
        return nms, sums

    nms1, sums1 = chip_sums({"down1": p_down1.reshape(N_CHIPS, 2, fsh // 2, D),
                             "up1": p_up1.reshape(N_CHIPS, 2, D // 2, 2 * fsh)}, "ffn1")
    dx2, dpw, pool_sm = _pool_bwd(dx3, plb, x2, pwf, pb_f, ps_f, norm_mix[1:2], tm, tps)
    dac0, dup0, fw0, *got1 = _bwd_down(dx2, up0, fdw_f[0], ffn_b_dw[0:1], wdn[0], tf, tpf, ffn_taps, "bwd_down0",
                                       comm=_ExchangeComm(sums1))
    p_down0 = _mm_tn(gb0t, dx2, F // 2, D, tw, False, "dw_down0")
    dup0, dx1, dnf0 = _bwd_up(dac0, dup0, wup[0], fdw_f[0], x1, norm_ffn[0:1], dx2, tf, tpf, ffn_taps, "bwd_up0")
    p_up0 = _mm_tn(hb1t, dup0, D, 2 * fsh, tw, True, "dw_up0")
    dc, conv_sm = _conv_bwd1(dx1, c, w2f, conv_ln_g, conv_ln_b, tm)
    p_pw2 = _mm_tn(sbt, dx1, D, D, tw, False, "dw_pw2")
    csh = cg // N_CHIPS
    p_pool = dpw.reshape(2, ng // 2, N_CHIPS, csh, cg).transpose(2, 0, 1, 3, 4).reshape(N_CHIPS, 2, (ng // 2) * csh, cg)
    nms2, sums2 = chip_sums({"down0": p_down0.reshape(N_CHIPS, 2, fsh // 2, D),
                             "up0": p_up0.reshape(N_CHIPS, 2, D // 2, 2 * fsh), "pool": p_pool.astype(BF16),
                             "pw2": p_pw2.reshape(N_CHIPS, 2, dsh // 2, D)}, "ffn0")
    grad_x, dab, dwdw, db1, dnm0, *got2 = _conv_bwd2(dc, a, x2d, w1g, wdw_f, norm_mix[0:1], dx1, tm, tps, taps,
                                                     comm=_ExchangeComm(sums2))
    p_pw1 = _mm_tn(hb0t, dab, D, 2 * dsh, tw, True, "dw_pw1")
    nms3, sums3 = chip_sums({"pw1": p_pw1.reshape(N_CHIPS, 2, D // 2, 2 * dsh)}, "conv")
    got3 = _comm_only(_ExchangeComm(sums3), "grad_chip_exchange_conv")
    for nm, g in zip(nms1 + nms2 + nms3, list(got1) + list(got2) + got3):
        fs[nm] = fs[nm] + (g,)
    bufs = [_final_sum(*fs[nm], ids, "final_sum_" + nm) for nm in ("pw1", "pw2", "pool")]
    for nm in ("up", "down"):
        b = _final_sum(*fs[nm + "0"], ids, "final_sum_" + nm + "0", layer=0, n_layers=2)
        bufs.append(_final_sum(*fs[nm + "1"], ids, "final_sum_" + nm + "1", layer=1, n_layers=2, buf=b))
    g_pw1, g_pw2, g_pool, g_up, g_down = [b.reshape(b.shape[0], 2 * b.shape[2], b.shape[3]) for b in _share_halves(bufs)]

    rd, rf = _small_allreduce([
        [(dnm0, 0, 1), (pool_sm, 2, 3), (dnf0, 0, 1), (dnf1, 0, 1), (db1, 0, 2), (conv_sm, 0, 4), (d_final, 0, 1),
         (pool_sm, 0, 2), (loss_part, 0, 1), 2, (dwdw, 0, dwdw.shape[0])],
        [(fw0, 0, 4), (fw1, 0, 4)]])
    g_wdw = lax.dynamic_slice_in_dim(rd[16:16 + taps], qi * dsh, dsh, axis=1)
    g_ps = lax.dynamic_slice_in_dim(rd[11:12], qi * dsh, dsh, axis=1)
    g_pb = lax.dynamic_slice_in_dim(rd[12:13], qi * dsh, dsh, axis=1)
    g_fdw = lax.dynamic_slice_in_dim(rf.reshape(2, 4, F)[:, :ffn_taps], qi * fsh, fsh, axis=2).reshape(2 * ffn_taps, fsh)

    def big(w, g, m, v, nm):
        d, mn, vn = _adamw(w.reshape(g.shape), g, m.reshape(g.shape), v.reshape(g.shape), "adamw_" + nm)
        return g.reshape(w.shape), d.reshape(w.shape), mn.reshape(w.shape), vn.reshape(w.shape)

    o_pw1 = big(conv_w_pw1, g_pw1, m_conv_w_pw1, v_conv_w_pw1, "pw1")
    o_pw2 = big(conv_w_pw2, g_pw2, m_conv_w_pw2, v_conv_w_pw2, "pw2")
    o_pool = big(pool_w, g_pool, m_pool_w, v_pool_w, "pool")
    o_up = big(ffn_w_up, g_up, m_ffn_w_up, v_ffn_w_up, "up")
    o_down = big(ffn_w_down, g_down, m_ffn_w_down, v_ffn_w_down, "down")

    snames = ["norm_mix", "norm_ffn", "b_pw1", "w_dw", "b_dw", "ln_g", "ln_b", "b_pw2", "pool_b", "pool_scale", "ffn_w_dw",
              "ffn_b_dw", "final_norm"]
    sw = [norm_mix, norm_ffn, conv_b_pw1, conv_w_dw, conv_b_dw, conv_ln_g, conv_ln_b, conv_b_pw2, pool_b, pool_scale,
          ffn_w_dw, ffn_b_dw, final_norm]
    smm = [m_norm_mix, m_norm_ffn, m_conv_b_pw1, m_conv_w_dw, m_conv_b_dw, m_conv_ln_g, m_conv_ln_b, m_conv_b_pw2, m_pool_b,
           m_pool_scale, m_ffn_w_dw, m_ffn_b_dw, m_final_norm]
    svv = [v_norm_mix, v_norm_ffn, v_conv_b_pw1, v_conv_w_dw, v_conv_b_dw, v_conv_ln_g, v_conv_ln_b, v_conv_b_pw2, v_pool_b,
           v_pool_scale, v_ffn_w_dw, v_ffn_b_dw, v_final_norm]
    sshapes = [tuple(w.shape) for w in sw]
    two_d = [(2, D), (2, D), (1, 2 * D), (taps, dsh), (1, D), (1, D), (1, D), (1, D), (1, dsh), (1, dsh), (2 * ffn_taps, fsh),
             (2, F), (1, D)]
    rows_d = [[0, 1], [2, 3], [4, 5], None, [9], [6], [7], [8], None, None, None, None, [10]]
    rows_f = [None] * 11 + [[3, 7], None]
    as2d = lambda ts: [t.reshape(s) for t, s in zip(ts, two_d)]
    loss, g_rep, sd, sm_new, sv_new = _adamw_small(rd, rf, [g_wdw, g_pb, g_ps, g_fdw], as2d(sw), as2d(smm), as2d(svv), rows_d,
                                                   rows_f, loss_row=13)
    sg = dict(g_rep)
    sg.update({3: g_wdw, 8: g_pb, 9: g_ps, 10: g_fdw})
    small_out = {n: tuple(t.reshape(sshapes[k]) for t in (sg[k], sd[k], sm_new[k], sv_new[k])) for k, n in enumerate(snames)}

    order = [small_out["norm_mix"], small_out["norm_ffn"], o_pw1, small_out["b_pw1"], small_out["w_dw"], small_out["b_dw"],
             small_out["ln_g"], small_out["ln_b"], o_pw2, small_out["b_pw2"], o_pool, small_out["pool_b"],
             small_out["pool_scale"], o_up, small_out["ffn_w_dw"], small_out["ffn_b_dw"], o_down, small_out["final_norm"]]
    return (loss[0, 0], grad_x.reshape(nb, seq, D), *[o[0] for o in order], *[o[1] for o in order], *[o[2] for o in order],
            *[o[3] for o in order])
```

```python
import functools

import jax
import jax.numpy as jnp
from jax import lax
from jax.experimental import pallas as pl
from jax.experimental.pallas import tpu as pltpu

F32, BF16 = jnp.float32, jnp.bfloat16
S_ = jax.ShapeDtypeStruct
MESH = pl.DeviceIdType.MESH

RMS_EPS, LN_EPS = 1e-6, 1e-5
POOL_WINDOWS = (2, 4, 8, 16)
ADAM_LR, ADAM_B1, ADAM_B2, ADAM_EPS, ADAM_WD, ADAM_STEP = 0.001, 0.9, 0.999, 1e-08, 0.01, 10

VMEM_LIMIT_BYTES = 60 * 1024 * 1024
N_CHIPS = 4
CHUNK = 16
HALO_CONV = 32
HALO_POOL = 32
HALO_FFN = 8
TILE_D = 512
TILE_F = 256
TILE_MM = 1024


def _pcall(body, *, name, grid, in_specs, out_specs, out_shape, args, scratch=(), aliases=None, prefetch=0, comm=None):
    params = pltpu.CompilerParams(dimension_semantics=("arbitrary",) * len(grid), vmem_limit_bytes=VMEM_LIMIT_BYTES)
    if comm is not None:
        assert not prefetch
        single = not isinstance(out_shape, (list, tuple))
        in_specs, args, scratch = list(in_specs), list(args), list(scratch)
        out_specs, out_shape = ([out_specs], [out_shape]) if single else (list(out_specs), list(out_shape))
        n_in, n_out, n_scr, n_cin, n_cout = len(in_specs), len(out_specs), len(scratch), len(comm.inputs), len(comm.out_shape)
        any_spec = pl.BlockSpec(memory_space=pl.ANY)
        aliases = dict(aliases or {})
        aliases.update({n_in + a: n_out + b for a, b in comm.aliases.items()})
        inner = body

        def body(*refs):
            ins, cin = refs[:n_in], refs[n_in:n_in + n_cin]
            outs = refs[n_in + n_cin:n_in + n_cin + n_out]
            cout = refs[n_in + n_cin + n_out:n_in + n_cin + n_out + n_cout]
            scr = refs[n_in + n_cin + n_out + n_cout:n_in + n_cin + n_out + n_cout + n_scr]
            sems = refs[n_in + n_cin + n_out + n_cout + n_scr:]
            step, total = 0, 1
            for d, g in enumerate(grid):
                step = step * g + pl.program_id(d)
                total *= g
            comm.before(cin, cout, sems, step, total)
            inner(*ins, *outs, *scr)
            comm.after(cin, cout, sems, step, total)

        in_specs += [any_spec] * n_cin
        args += list(comm.inputs)
        out_specs += [any_spec] * n_cout
        out_shape += list(comm.out_shape)
        scratch += list(comm.sems)
    if prefetch:
        spec = pltpu.PrefetchScalarGridSpec(num_scalar_prefetch=prefetch, grid=grid, in_specs=in_specs,
                                            out_specs=out_specs, scratch_shapes=list(scratch))
        return pl.pallas_call(body, name=name, grid_spec=spec, out_shape=out_shape,
                              input_output_aliases=aliases or {}, compiler_params=params)(*args)
    return pl.pallas_call(body, name=name, grid=grid, in_specs=in_specs, out_specs=out_specs, out_shape=out_shape,
                          scratch_shapes=list(scratch), input_output_aliases=aliases or {},
                          compiler_params=params)(*args)


def _full(shape):
    n = len(shape)
    return pl.BlockSpec(tuple(shape), lambda *_: (0,) * n, pipeline_mode=pl.Buffered(1))


def _rows(tm, c, col=0):
    return pl.BlockSpec((tm, c), lambda i, *_: (i, col))


def _cols(r, tm):
    return pl.BlockSpec((r, tm), lambda i, *_: (0, i))


def _prev_halo(tm, hb, c):
    return pl.BlockSpec((hb, c), lambda i, *_: (jnp.maximum(i * (tm // hb) - 1, 0), 0))


def _next_halo(tm, hb, c, total_rows):
    last = total_rows // hb - 1
    return pl.BlockSpec((hb, c), lambda i, *_: (jnp.minimum((i + 1) * (tm // hb), last), 0))


def _acc_spec(r, c):
    return pl.BlockSpec((r, c), lambda *_: (0, 0))


def _rms(x, g):
    r = lax.rsqrt(jnp.mean(x * x, axis=-1, keepdims=True) + RMS_EPS)
    xh = x * r
    return xh * g, xh, r


def _rms_bwd(dy, xh, r, g):
    dxh = dy * g
    dx = r * (dxh - xh * jnp.mean(dxh * xh, axis=-1, keepdims=True))
    return dx, jnp.sum(dy * xh, axis=0, keepdims=True)


def _colsum(v):
    return jnp.sum(v, axis=0, keepdims=True)


def _fold8(v):
    out = v[0:8]
    for j in range(1, v.shape[0] // 8):
        out = out + v[8 * j:8 * j + 8]
    return out


def _chunks(n_rows, fn):
    def step(j, carry):
        fn(pl.multiple_of(j * CHUNK, CHUNK))
        return carry
    lax.fori_loop(0, n_rows // CHUNK, step, 0)


def _dot(a, b):
    return jnp.dot(a, b, preferred_element_type=F32)


def _dot_nt(a, b):
    return lax.dot_general(a, b, (((1,), (1,)), ((), ())), preferred_element_type=F32)


def _dot_tn(a, b):
    return lax.dot_general(a, b, (((0,), (0,)), ((), ())), preferred_element_type=F32)


def _sig(v):
    return jax.nn.sigmoid(v)


def _accumulate(ref, val, first):
    @pl.when(first)
    def _():
        ref[...] = val

    @pl.when(jnp.logical_not(first))
    def _():
        ref[...] = ref[...] + val


def _conv_in(x, g, w1g, b1, tm, comm=None):
    T, D = x.shape
    ns, _, nc = w1g.shape

    def body(x_ref, g_ref, w_ref, b_ref, hbt_ref, a_ref):
        hb = _rms(x_ref[...], g_ref[...])[0].astype(BF16)
        hbt_ref[...] = hb.T
        for s in range(ns):
            a_ref[:, s * nc:(s + 1) * nc] = _dot(hb, w_ref[s]) + b_ref[:, s * nc:(s + 1) * nc]

    return _pcall(body, name="conv_in", grid=(T // tm,),
                  in_specs=[_rows(tm, D), _full((1, D)), _full(w1g.shape), _full((1, ns * nc))],
                  out_specs=[_cols(D, tm), _rows(tm, ns * nc)],
                  out_shape=[S_((D, T), BF16), S_((T, ns * nc), F32)], args=(x, g, w1g, b1), comm=comm)


def _conv_mid(a, x, wdw, bdw, lng, lnb, w2, b2, gn, tm, tps, taps, comm=None):
    T, D = x.shape
    hc = HALO_CONV

    def body(a_ref, ah_ref, x_ref, wdw_ref, bdw_ref, lng_ref, lnb_ref, w2_ref, b2_ref, gn_ref,
             c_ref, sbt_ref, x1_ref, hb_ref, hbt_ref, uext, shifted):
        i = pl.program_id(0)
        ah = ah_ref[...]
        uext[0:hc, :] = jnp.where(i % tps == 0, 0.0, ah[:, :D] * _sig(ah[:, D:]))
        av = a_ref[...]
        uext[hc:hc + tm, :] = av[:, :D] * _sig(av[:, D:])

        def conv(r0, src, ks, first):
            acc = jnp.broadcast_to(bdw_ref[...], (CHUNK, D)) if first else c_ref[pl.ds(r0, CHUNK), :]
            for k, o in ks:
                acc = acc + wdw_ref[k:k + 1, :] * src[pl.ds(pl.multiple_of(r0 + o, 8), CHUNK), :]
            c_ref[pl.ds(r0, CHUNK), :] = acc
        _by_sublane_shift(uext, shifted, tm, [hc - (taps - 1) + k for k in range(taps)], conv)

        cv = c_ref[...]
        xc = cv - jnp.mean(cv, axis=-1, keepdims=True)
        nh = xc * lax.rsqrt(jnp.mean(xc * xc, axis=-1, keepdims=True) + LN_EPS)
        n = nh * lng_ref[...] + lnb_ref[...]
        sb = (n * _sig(n)).astype(BF16)
        sbt_ref[...] = sb.T
        x1 = x_ref[...] + _dot(sb, w2_ref[...]) + b2_ref[...]
        x1_ref[...] = x1
        hb = _rms(x1, gn_ref[...])[0].astype(BF16)
        hb_ref[...] = hb
        hbt_ref[...] = hb.T

    return _pcall(body, name="conv_mid", grid=(T // tm,),
                  in_specs=[_rows(tm, 2 * D), _prev_halo(tm, hc, 2 * D), _rows(tm, D), _full(wdw.shape), _full((1, D)),
                            _full((1, D)), _full((1, D)), _full((D, D)), _full((1, D)), _full((1, D))],
                  out_specs=[_rows(tm, D), _cols(D, tm), _rows(tm, D), _rows(tm, D), _cols(D, tm)],
                  out_shape=[S_((T, D), F32), S_((D, T), BF16), S_((T, D), F32), S_((T, D), BF16), S_((D, T), BF16)],
                  scratch=[pltpu.VMEM((tm + hc, D), F32), pltpu.VMEM((tm + hc - 8, D), F32)],
                  args=(a, a, x, wdw, bdw, lng, lnb, w2, b2, gn), comm=comm)


def _mm_up(hb, wg, tm, name, comm=None):
    T, D = hb.shape
    ns, _, nc = wg.shape

    def body(h_ref, w_ref, o_ref):
        o_ref[...] = _dot(h_ref[...], w_ref[...])

    return _pcall(body, name=name, grid=(ns, T // tm),
                  in_specs=[pl.BlockSpec((tm, D), lambda s, i: (i, 0)), pl.BlockSpec((None, D, nc), lambda s, i: (s, 0, 0))],
                  out_specs=pl.BlockSpec((tm, nc), lambda s, i: (i, s)),
                  out_shape=S_((T, ns * nc), F32), args=(hb, wg), comm=comm)


def _ffn_taps(aext, realigned, tm, ffn_taps):
    for k in range(ffn_taps - 1):
        realigned[k] = aext[pl.ds(HALO_FFN - (ffn_taps - 1) + k, tm), :]
    return [realigned.at[k] for k in range(ffn_taps - 1)] + [aext.at[pl.ds(HALO_FFN, tm)]]


def _ffn_conv(taps, wdw_ref, bdw_ref):
    acc = bdw_ref[...]
    for k, tap in enumerate(taps):
        acc = acc + wdw_ref[k:k + 1, :] * tap[...]
    return acc


def _by_sublane_shift(ext, shifted, tm, offsets, per_chunk):
    first = True
    for b in range(8):
        ks = [(k, o - b) for k, o in enumerate(offsets) if o % 8 == b]
        if not ks:
            continue
        if b:
            shifted[...] = ext[pl.ds(b, shifted.shape[0]), :]
        _chunks(tm, functools.partial(per_chunk, src=shifted if b else ext, ks=ks, first=first))
        first = False


def _ffn_down(up, xin, wdw, bdw, wd, gn, tm, tps, ffn_taps, name, target=None, comm=None):
    T, D = xin.shape
    F = up.shape[1] // 2
    hf = HALO_FFN
    final = target is not None

    def body(*refs):
        if final:
            (up_ref, uph_ref, x_ref, wdw_ref, bdw_ref, wd_ref, gn_ref, tg_ref,
             gb_ref, dx_ref, loss_ref, dgn_ref, aext, realigned) = refs
        else:
            (up_ref, uph_ref, x_ref, wdw_ref, bdw_ref, wd_ref, gn_ref,
             gb_ref, xo_ref, h_ref, aext, realigned) = refs
        i = pl.program_id(0)
        aext[0:hf, :] = jnp.where(i % tps == 0, 0.0, uph_ref[...])
        aext[hf:hf + tm, :] = up_ref[:, 0:F]
        acv = _ffn_conv(_ffn_taps(aext, realigned, tm, ffn_taps), wdw_ref, bdw_ref)
        gb = (acv * _sig(acv) * up_ref[:, F:2 * F]).astype(BF16)
        gb_ref[...] = gb.T
        xo = x_ref[...] + _dot(gb, wd_ref[...])
        if not final:
            xo_ref[...] = xo
            h_ref[...] = _rms(xo, gn_ref[...])[0]
        else:
            out, xh, r = _rms(xo, gn_ref[...])
            diff = out - tg_ref[...]
            part = 0.5 / D * _colsum(diff * diff)
            dx, dg = _rms_bwd(diff * (1.0 / D), xh, r, gn_ref[...])
            dx_ref[...] = dx
            _accumulate(loss_ref, part, i == 0)
            _accumulate(dgn_ref, dg, i == 0)

    in_specs = [_rows(tm, 2 * F), _prev_halo(tm, hf, F), _rows(tm, D), _full(wdw.shape), _full((1, F)),
                _full((F, D)), _full((1, D))]
    args = [up, up, xin, wdw, bdw, wd, gn]
    if final:
        in_specs.append(_rows(tm, D))
        args.append(target)
        out_specs = [_cols(F, tm), _rows(tm, D), _acc_spec(1, D), _acc_spec(1, D)]
        out_shape = [S_((F, T), BF16), S_((T, D), F32), S_((1, D), F32), S_((1, D), F32)]
    else:
        out_specs = [_cols(F, tm), _rows(tm, D), _rows(tm, D)]
        out_shape = [S_((F, T), BF16), S_((T, D), F32), S_((T, D), F32)]
    return _pcall(body, name=name, grid=(T // tm,), in_specs=in_specs, out_specs=out_specs, out_shape=out_shape,
                  scratch=[pltpu.VMEM((tm + hf, F), F32), pltpu.VMEM((ffn_taps - 1, tm, F), F32)], args=args, comm=comm)


def _pool_counts(i, tps, tm, w):
    pos = (i % tps) * tm + lax.broadcasted_iota(jnp.int32, (tm, 1), 0)
    return jnp.minimum(pos + 1, w).astype(F32)


def _window_sums(src, buf_a, buf_b, cg, causal):
    assert POOL_WINDOWS == (2, 4, 8, 16)
    n = src.shape[0]
    levels = len(POOL_WINDOWS)
    cur, outs = src, []
    for l in range(levels):
        dst = buf_b if l % 2 else buf_a
        cols = slice(l * cg, levels * cg)
        lo, m = 8 * (l + 1), n - 8 * (l + 1)
        if causal:
            dst[lo:n, cols] = cur[lo:n, cols] + cur[pl.ds(lo - (1 << l), m), cols]
        else:
            dst[0:m, cols] = cur[0:m, cols] + cur[pl.ds(1 << l, m), cols]
        outs.append(dst)
        cur = dst
    return outs


def _pool_fwd(h, x2, pw, pb, ps, gn, tm, tps):
    T, D = h.shape
    ng, cg, _ = pw.shape
    hp = HALO_POOL

    def body(h_ref, hh_ref, x_ref, pw_ref, pb_ref, ps_ref, gn_ref, pl_ref, xo_ref, hb_ref, hbt_ref, hext, buf_a, buf_b):
        i = pl.program_id(0)
        hext[0:hp, :] = jnp.where(i % tps == 0, 0.0, hh_ref[...])
        hext[hp:hp + tm, :] = h_ref[...]
        sums = _window_sums(hext, buf_a, buf_b, cg, True)
        for g, w in enumerate(POOL_WINDOWS):
            cs = slice(g * cg, (g + 1) * cg)
            pl_ref[:, cs] = (sums[g][hp:hp + tm, cs] / _pool_counts(i, tps, tm, w) - h_ref[:, cs]).astype(BF16)

        pv = pl_ref[...]
        mixed = jnp.concatenate([_dot(pv[:, g * cg:(g + 1) * cg], pw_ref[g]) for g in range(ng)], axis=1)
        xo = x_ref[...] + ps_ref[...] * (mixed + pb_ref[...])
        xo_ref[...] = xo
        hb = _rms(xo, gn_ref[...])[0].astype(BF16)
        hb_ref[...] = hb
        hbt_ref[...] = hb.T

    return _pcall(body, name="pool_fwd", grid=(T // tm,),
                  in_specs=[_rows(tm, D), _prev_halo(tm, hp, D), _rows(tm, D), _full(pw.shape), _full((1, D)), _full((1, D)),
                            _full((1, D))],
                  out_specs=[_rows(tm, D), _rows(tm, D), _rows(tm, D), _cols(D, tm)],
                  out_shape=[S_((T, D), BF16), S_((T, D), F32), S_((T, D), BF16), S_((D, T), BF16)],
                  scratch=[pltpu.VMEM((tm + hp, D), F32)] * 3, args=(h, h, x2, pw, pb, ps, gn))


def _bwd_down(dx, up, wdw, bdw, wd, tm, tps, ffn_taps, name, comm=None):
    T, D = dx.shape
    F = up.shape[1] // 2
    hf = HALO_FFN
    nt = T // tm

    def body(dx_ref, up_ref, uph_ref, wdw_ref, bdw_ref, wd_ref, dac_ref, dgate_ref, aext, realigned):
        i = pl.program_id(0)
        aext[0:hf, :] = jnp.where(i % tps == 0, 0.0, uph_ref[...])
        aext[hf:hf + tm, :] = up_ref[:, 0:F]
        acv = _ffn_conv(_ffn_taps(aext, realigned, tm, ffn_taps), wdw_ref, bdw_ref)
        dg = _dot_nt(dx_ref[...].astype(BF16), wd_ref[...])
        sg = _sig(acv)
        dgate_ref[...] = (dg * acv * sg).astype(BF16)
        dac_ref[...] = dg * up_ref[:, F:2 * F] * (sg * (1.0 + acv * (1.0 - sg)))

    return _pcall(body, name=name, grid=(nt,),
                  in_specs=[_rows(tm, D), _rows(tm, 2 * F), _prev_halo(tm, hf, F), _full(wdw.shape), _full((1, F)),
                            _full((F, D))],
                  out_specs=[_rows(tm, F), _rows(tm, F, col=1)],
                  out_shape=[S_((T, F), F32), S_((T, 2 * F), BF16)],
                  scratch=[pltpu.VMEM((tm + hf, F), F32), pltpu.VMEM((ffn_taps - 1, tm, F), F32)],
                  args=(dx, up, up, wdw, bdw, wd), comm=comm)


def _bwd_up(dac, dup, wg, wdw, xin, gn, dx, up, tm, tps, ffn_taps, name, comm=None):
    T, D = xin.shape
    F = dac.shape[1]
    ns, _, nc = wg.shape
    hf = HALO_FFN
    nt = T // tm

    def body(dac_ref, dach_ref, dgate_ref, wg_ref, wdw_ref, x_ref, gn_ref, dx_ref, act_ref, acth_ref,
             dact_ref, dxo_ref, dgn_ref, wsum_ref, dext, aext):
        i = pl.program_id(0)
        dac = dac_ref[...]
        dext[0:tm, :] = dac
        dext[tm:tm + hf, :] = jnp.where(i % tps == tps - 1, 0.0, dach_ref[...])
        dact = wdw_ref[ffn_taps - 1:ffn_taps, :] * dac
        for k in range(ffn_taps - 1):
            dact = dact + wdw_ref[k:k + 1, :] * dext[pl.ds(ffn_taps - 1 - k, tm), :]
        dact_ref[...] = dact.astype(BF16)
        aext[0:hf, :] = jnp.where(i % tps == 0, 0.0, acth_ref[...])
        aext[hf:hf + tm, :] = act_ref[...]
        rows = [_colsum(dac * aext[pl.ds(hf - (ffn_taps - 1) + k, tm), :]) for k in range(ffn_taps)]
        rows += [_colsum(dac), jnp.zeros((8 - ffn_taps - 1, F), F32)]
        _accumulate(wsum_ref, jnp.concatenate(rows, axis=0), i == 0)

        dh = jnp.zeros((tm, D), F32)
        for s in range(ns):
            src = dact_ref if s < ns // 2 else dgate_ref
            o = (s % (ns // 2)) * nc
            dh = dh + _dot_nt(src[:, o:o + nc], wg_ref[s])
        _, xh, r = _rms(x_ref[...], gn_ref[...])
        dxn, dg = _rms_bwd(dh, xh, r, gn_ref[...])
        dxo_ref[...] = dx_ref[...] + dxn
        _accumulate(dgn_ref, dg, i == 0)

    return _pcall(body, name=name, grid=(nt,),
                  in_specs=[_rows(tm, F), _next_halo(tm, hf, F, T), _rows(tm, F, col=1), _full(wg.shape), _full(wdw.shape),
                            _rows(tm, D), _full((1, D)), _rows(tm, D), _rows(tm, F), _prev_halo(tm, hf, F)],
                  out_specs=[_rows(tm, F), _rows(tm, D), _acc_spec(1, D), _acc_spec(8, F)],
                  out_shape=[S_((T, 2 * F), BF16), S_((T, D), F32), S_((1, D), F32), S_((8, F), F32)],
                  scratch=[pltpu.VMEM((tm + hf, F), F32)] * 2,
                  aliases={2: 0}, args=(dac, dac, dup, wg, wdw, xin, gn, dx, up, up), comm=comm)


def _mm_tn(xt, dy, bk, bn, tt, shard_major, name):
    K, T = xt.shape
    N = dy.shape[1]
    nt = T // tt

    def body(x_ref, dy_ref, o_ref, acc):
        t = pl.program_id(2)
        part = _dot(x_ref[...], dy_ref[...].astype(BF16))
        _accumulate(acc, part, t == 0)

        @pl.when(t == nt - 1)
        def _():
            o_ref[...] = acc[...].astype(o_ref.dtype)

    if shard_major:
        out_spec = pl.BlockSpec((None, bk, bn), lambda i, j, t: (j, i, 0))
        out_shape = S_((N // bn, K, bn), BF16)
    else:
        out_spec = pl.BlockSpec((bk, bn), lambda i, j, t: (i, j))
        out_shape = S_((K, N), BF16)
    return _pcall(body, name=name, grid=(K // bk, N // bn, nt),
                  in_specs=[pl.BlockSpec((bk, tt), lambda i, j, t: (i, t)), pl.BlockSpec((tt, bn), lambda i, j, t: (t, j))],
                  out_specs=out_spec, out_shape=out_shape, scratch=[pltpu.VMEM((bk, bn), F32)], args=(xt, dy))


def _pool_bwd(dx3, plb, x2, pw, pb, ps, gn, tm, tps):
    T, D = x2.shape
    ng, cg, _ = pw.shape
    hp = HALO_POOL
    nt = T // tm

    def body(do_ref, doh_ref, pl_ref, x_ref, pw_ref, pb_ref, ps_ref, gn_ref, dxo_ref, dpw_ref, sm_ref, qext, buf_a, buf_b,
             dh_s):
        i = pl.program_id(0)
        do = do_ref[...]
        dm = do * ps_ref[...]
        dmh = jnp.where(i % tps == tps - 1, 0.0, doh_ref[...]) * ps_ref[...]
        pv = pl_ref[...]
        mixed = jnp.concatenate([_dot(pv[:, g * cg:(g + 1) * cg], pw_ref[g]) for g in range(ng)], axis=1)
        dscale = _colsum(do * (mixed + pb_ref[...]))
        dbias = _colsum(dm)
        dmb = dm.astype(BF16)
        dmhb = dmh.astype(BF16)
        for g, w in enumerate(POOL_WINDOWS):
            cs = slice(g * cg, (g + 1) * cg)
            _accumulate(dpw_ref.at[g], _dot_tn(pv[:, cs], dmb[:, cs]), i == 0)
            dpo = _dot_nt(dmb[:, cs], pw_ref[g])
            dh_s[:, cs] = dpo
            qext[0:tm, cs] = dpo / _pool_counts(i, tps, tm, w)
            qext[tm:tm + hp, cs] = _dot_nt(dmhb[:, cs], pw_ref[g]) * (1.0 / w)
        sums = _window_sums(qext, buf_a, buf_b, cg, False)
        for g in range(ng):
            cs = slice(g * cg, (g + 1) * cg)
            dh_s[:, cs] = sums[g][0:tm, cs] - dh_s[:, cs]

        _, xh, r = _rms(x_ref[...], gn_ref[...])
        dxn, dg = _rms_bwd(dh_s[...], xh, r, gn_ref[...])
        dxo_ref[...] = do + dxn
        sm = jnp.concatenate([dscale, dbias, dg, jnp.zeros((5, D), F32)], axis=0)
        _accumulate(sm_ref, sm, i == 0)

    return _pcall(body, name="pool_bwd", grid=(nt,),
                  in_specs=[_rows(tm, D), _next_halo(tm, hp, D, T), _rows(tm, D), _rows(tm, D), _full(pw.shape), _full((1, D)),
                            _full((1, D)), _full((1, D))],
                  out_specs=[_rows(tm, D), pl.BlockSpec((ng, cg, cg), lambda i: (0, 0, 0)), _acc_spec(8, D)],
                  out_shape=[S_((T, D), F32), S_((ng, cg, cg), F32), S_((8, D), F32)],
                  scratch=[pltpu.VMEM((tm + hp, D), F32)] * 3 + [pltpu.VMEM((tm, D), F32)],
                  args=(dx3, dx3, plb, x2, pw, pb, ps, gn))


def _conv_bwd1(dx1, c, w2, lng, lnb, tm):
    T, D = dx1.shape

    def body(do_ref, c_ref, w2_ref, lng_ref, lnb_ref, dc_ref, sm_ref):
        i = pl.program_id(0)
        do = do_ref[...]
        ds = _dot_nt(do.astype(BF16), w2_ref[...])
        cv = c_ref[...]
        xc = cv - jnp.mean(cv, axis=-1, keepdims=True)
        rstd = lax.rsqrt(jnp.mean(xc * xc, axis=-1, keepdims=True) + LN_EPS)
        nh = xc * rstd
        n = nh * lng_ref[...] + lnb_ref[...]
        sg = _sig(n)
        dn = ds * (sg * (1.0 + n * (1.0 - sg)))
        dnh = dn * lng_ref[...]
        dc = rstd * (dnh - jnp.mean(dnh, axis=-1, keepdims=True) - nh * jnp.mean(dnh * nh, axis=-1, keepdims=True))
        dc_ref[...] = dc
        sm = jnp.concatenate([_colsum(dn * nh), _colsum(dn), _colsum(do), _colsum(dc), jnp.zeros((4, D), F32)], axis=0)
        _accumulate(sm_ref, sm, i == 0)

    return _pcall(body, name="conv_bwd1", grid=(T // tm,),
                  in_specs=[_rows(tm, D), _rows(tm, D), _full((D, D)), _full((1, D)), _full((1, D))],
                  out_specs=[_rows(tm, D), _acc_spec(8, D)],
                  out_shape=[S_((T, D), F32), S_((8, D), F32)], args=(dx1, c, w2, lng, lnb))


def _conv_bwd2(dc, a, x, w1g, wdw, gn, dx1, tm, tps, taps, comm=None):
    T, D = x.shape
    ns, _, nc = w1g.shape
    hc = HALO_CONV
    nt = T // tm
    tp = wdw.shape[0]

    def body(dc_ref, dch_ref, a_ref, ah_ref, x_ref, w_ref, wdw_ref, gn_ref, dx1_ref,
             gx_ref, dab_ref, dw_ref, db1_ref, dgn_ref, uext, dext, shifted, du_s, wacc):
        i = pl.program_id(0)
        ah = ah_ref[...]
        uext[0:hc, :] = jnp.where(i % tps == 0, 0.0, ah[:, :D] * _sig(ah[:, D:]))
        av = a_ref[...]
        sg2 = _sig(av[:, D:])
        uext[hc:hc + tm, :] = av[:, :D] * sg2
        dext[0:tm, :] = dc_ref[...]
        dext[tm:tm + hc, :] = jnp.where(i % tps == tps - 1, 0.0, dch_ref[...])

        @pl.when(i == 0)
        def _():
            wacc[...] = jnp.zeros_like(wacc)

        def wgrad(r0, src, ks, first):
            d = dc_ref[pl.ds(r0, CHUNK), :]
            for k, o in ks:
                wacc[8 * k:8 * k + 8, :] += _fold8(d * src[pl.ds(pl.multiple_of(r0 + o, 8), CHUNK), :])
        _by_sublane_shift(uext, shifted, tm, [hc - (taps - 1) + k for k in range(taps)], wgrad)

        def convt(r0, src, ks, first):
            acc = jnp.zeros((CHUNK, D), F32) if first else du_s[pl.ds(r0, CHUNK), :]
            for k, o in ks:
                acc = acc + wdw_ref[k:k + 1, :] * src[pl.ds(pl.multiple_of(r0 + o, 8), CHUNK), :]
            du_s[pl.ds(r0, CHUNK), :] = acc
        _by_sublane_shift(dext, shifted, tm, [taps - 1 - k for k in range(taps)], convt)

        du = du_s[...]
        da1 = du * sg2
        da2 = du * av[:, :D] * (sg2 * (1.0 - sg2))
        dab_ref[:, 0:D] = da1.astype(BF16)
        dab_ref[:, D:2 * D] = da2.astype(BF16)
        _accumulate(db1_ref, jnp.concatenate([_colsum(da1), _colsum(da2)], axis=0), i == 0)
        dh = jnp.zeros((tm, D), F32)
        for s in range(ns):
            dh = dh + _dot_nt(dab_ref[:, s * nc:(s + 1) * nc], w_ref[s])
        _, xh, r = _rms(x_ref[...], gn_ref[...])
        dxn, dg = _rms_bwd(dh, xh, r, gn_ref[...])
        gx_ref[...] = dx1_ref[...] + dxn
        _accumulate(dgn_ref, dg, i == 0)

        @pl.when(i == nt - 1)
        def _():
            dw_ref[...] = jnp.zeros_like(dw_ref)
            for k in range(taps):
                dw_ref[k:k + 1, :] = _colsum(wacc[8 * k:8 * k + 8, :])

    return _pcall(body, name="conv_bwd2", grid=(nt,),
                  in_specs=[_rows(tm, D), _next_halo(tm, hc, D, T), _rows(tm, 2 * D), _prev_halo(tm, hc, 2 * D), _rows(tm, D),
                            _full(w1g.shape), _full(wdw.shape), _full((1, D)), _rows(tm, D)],
                  out_specs=[_rows(tm, D), _rows(tm, 2 * D), _acc_spec(tp, D), _acc_spec(2, D), _acc_spec(1, D)],
                  out_shape=[S_((T, D), F32), S_((T, 2 * D), BF16), S_((tp, D), F32), S_((2, D), F32), S_((1, D), F32)],
                  scratch=[pltpu.VMEM((tm + hc, D), F32), pltpu.VMEM((tm + hc, D), F32), pltpu.VMEM((tm + hc - 8, D), F32),
                           pltpu.VMEM((tm, D), F32), pltpu.VMEM((8 * taps, D), F32)],
                  args=(dc, dc, a, a, x, w1g, wdw, gn, dx1), comm=comm)


def _place():
    x, y, c = lax.axis_index("x"), lax.axis_index("y"), lax.axis_index("c")
    chips = [(1 - x, y), (x, 1 - y), (1 - x, 1 - y)]
    return x, y, c, chips


def _own_slot(w, layer, ids, dtype, name):
    _, r, c = w.shape
    br = _row_block(r, c)

    def body(ids_ref, w_ref, o_ref):
        o_ref[...] = w_ref[...].astype(dtype)

    return _pcall(body, name=name, grid=(r // br,), prefetch=1,
                  in_specs=[pl.BlockSpec((None, br, c), lambda i, ids: (layer, i, 0))],
                  out_specs=pl.BlockSpec((None, br, c), lambda i, ids: (ids[0], i, 0)),
                  out_shape=S_((N_CHIPS, r, c), dtype), args=(ids, w))


class _GatherComm:
    def __init__(self, slots):
        n = len(slots)
        self.inputs = list(slots)
        self.out_shape = [S_(s.shape, s.dtype) for s in slots]
        self.aliases = {i: i for i in range(n)}
        self.sems = [pltpu.SemaphoreType.DMA((n, 3))] * 4
        self.halves = [s.shape[1] // 2 for s in slots]

    def _copies(self, outs, sems):
        s_ici, r_ici, s_d2d, r_d2d = sems
        x, y, c, chips = _place()
        peers = [(j, chip, 2 * chip[0] + chip[1]) for j, chip in enumerate(chips)]

        def half(i, h):
            return pl.ds(pl.multiple_of(h * self.halves[i], 8), self.halves[i])

        def ici(i, j, chip, to):
            blk = outs[i].at[chip, half(i, c)]
            return pltpu.make_async_remote_copy(src_ref=blk, dst_ref=blk, send_sem=s_ici.at[i, j], recv_sem=r_ici.at[i, j],
                                                device_id=to, device_id_type=MESH)

        def d2d(i, j, chip, h):
            blk = outs[i].at[chip, half(i, h)]
            return pltpu.make_async_remote_copy(src_ref=blk, dst_ref=blk, send_sem=s_d2d.at[i, j], recv_sem=r_d2d.at[i, j],
                                                device_id=(x, y, 1 - c), device_id_type=MESH)
        return (x, y, c, 2 * x + y), peers, ici, d2d

    def before(self, cin, outs, sems, step, total):
        (x, y, c, q), peers, ici, d2d = self._copies(outs, sems)

        @pl.when(step == 0)
        def _():
            for i in range(len(outs)):
                for j, chip, _ in peers:
                    ici(i, j, q, (*chip, c)).start()

    def after(self, cin, outs, sems, step, total):
        (x, y, c, q), peers, ici, d2d = self._copies(outs, sems)

        @pl.when(step == max(total - 2, 0))
        def _():
            for i in range(len(outs)):
                for j, chip, qj in peers:
                    ici(i, j, qj, (x, y, c)).wait_recv()
                    d2d(i, j, qj, c).start()

        @pl.when(step == total - 1)
        def _():
            for i in range(len(outs)):
                for j, chip, qj in peers:
                    d2d(i, j, qj, 1 - c).wait_recv()
            for i in range(len(outs)):
                for j, chip, qj in peers:
                    ici(i, j, q, (*chip, c)).wait_send()
                    d2d(i, j, qj, c).wait_send()


class _ExchangeComm:
    def __init__(self, sums):
        n = len(sums)
        self.inputs = list(sums)
        self.out_shape = [S_((3,) + s.shape[1:], s.dtype) for s in sums]
        self.aliases = {}
        self.sems = [pltpu.SemaphoreType.DMA((n, 3))] * 2

    def _copies(self, srcs, outs, sems):
        s_sem, r_sem = sems
        x, y, c, chips = _place()
        return [pltpu.make_async_remote_copy(src_ref=srcs[i].at[2 * chip[0] + chip[1]], dst_ref=outs[i].at[j],
                                             send_sem=s_sem.at[i, j], recv_sem=r_sem.at[i, j],
                                             device_id=(*chip, c), device_id_type=MESH)
                for i in range(len(srcs)) for j, chip in enumerate(chips)]

    def before(self, srcs, outs, sems, step, total):
        @pl.when(step == 0)
        def _():
            for cp in self._copies(srcs, outs, sems):
                cp.start()

    def after(self, srcs, outs, sems, step, total):
        @pl.when(step == total - 1)
        def _():
            for cp in self._copies(srcs, outs, sems):
                cp.wait()


def _comm_only(comm, name):
    def body():
        pass
    outs = _pcall(body, name=name, grid=(1,), in_specs=[], out_specs=[], out_shape=[], args=[], comm=comm)
    return list(outs)


def _pair_exchange(parts, name):
    n = len(parts)

    def body(*refs):
        srcs, outs = refs[:n], refs[n:2 * n]
        s_sem, r_sem = refs[2 * n:]
        x, y, c, _ = _place()
        cps = []
        for i in range(n):
            for s in range(N_CHIPS):
                cp = pltpu.make_async_remote_copy(src_ref=srcs[i].at[s, 1 - c], dst_ref=outs[i].at[s], send_sem=s_sem.at[i, s],
                                                  recv_sem=r_sem.at[i, s], device_id=(x, y, 1 - c), device_id_type=MESH)
                cp.start()
                cps.append(cp)
        for cp in cps:
            cp.wait()

    any_spec = pl.BlockSpec(memory_space=pl.ANY)
    outs = pl.pallas_call(
        body, name=name, in_specs=[any_spec] * n, out_specs=[any_spec] * n,
        out_shape=[S_((N_CHIPS,) + p.shape[2:], p.dtype) for p in parts],
        scratch_shapes=[pltpu.SemaphoreType.DMA((n, N_CHIPS))] * 2,
    )(*parts)
    return list(outs)


class _ShareComm:
    def __init__(self, bufs):
        n = len(bufs)
        self.inputs = list(bufs)
        self.out_shape = [S_(b.shape, b.dtype) for b in bufs]
        self.aliases = {i: i for i in range(n)}
        self.sems = [pltpu.SemaphoreType.DMA((n, max(b.shape[0] for b in bufs)))] * 2
        self.layers = [b.shape[0] for b in bufs]

    def _copies(self, outs, sems):
        s_sem, r_sem = sems
        x, y, c, _ = _place()
        cps = []
        for i, nl in enumerate(self.layers):
            for l in range(nl):
                blk = outs[i].at[l, c]
                cps.append(pltpu.make_async_remote_copy(src_ref=blk, dst_ref=blk, send_sem=s_sem.at[i, l],
                                                        recv_sem=r_sem.at[i, l], device_id=(x, y, 1 - c),
                                                        device_id_type=MESH))
        return cps

    def before(self, cin, outs, sems, step, total):
        @pl.when(step == 0)
        def _():
            for cp in self._copies(outs, sems):
                cp.start()

    def after(self, cin, outs, sems, step, total):
        @pl.when(step == total - 1)
        def _():
            for cp in self._copies(outs, sems):
                cp.wait()


class _Both:
    def __init__(self, a, b):
        self.parts = (a, b)
        self.inputs = a.inputs + b.inputs
        self.out_shape = a.out_shape + b.out_shape
        self.aliases = dict(a.aliases)
        self.aliases.update({len(a.inputs) + i: len(a.out_shape) + o for i, o in b.aliases.items()})
        self.sems = a.sems + b.sems

    def _each(self, phase, cin, cout, sems, step, total):
        i = o = s = 0
        for p in self.parts:
            ni, no, ns = len(p.inputs), len(p.out_shape), len(p.sems)
            getattr(p, phase)(cin[i:i + ni], cout[o:o + no], sems[s:s + ns], step, total)
            i, o, s = i + ni, o + no, s + ns

    def before(self, cin, cout, sems, step, total):
        self._each("before", cin, cout, sems, step, total)

    def after(self, cin, cout, sems, step, total):
        self._each("after", cin, cout, sems, step, total)


def _small_allreduce(groups):
    n = len(groups)
    arrays = []
    for grp in groups:
        for p in grp:
            if not isinstance(p, int) and not any(p[0] is a for a in arrays):
                arrays.append(p[0])
    shapes = []
    for grp in groups:
        rows = sum(p if isinstance(p, int) else p[2] - p[1] for p in grp)
        width = next(p[0].shape[1] for p in grp if not isinstance(p, int))
        assert rows % 8 == 0
        shapes.append((rows, width))
    na = len(arrays)

    def body(*refs):
        a_refs, o_refs = refs[:na], refs[na:na + n]
        rest = refs[na + n:]
        v_refs, sibs, css, gots = rest[:n], rest[n:2 * n], rest[2 * n:3 * n], rest[3 * n:4 * n]
        s_sem, r_sem = rest[4 * n:]
        for i, grp in enumerate(groups):
            o = 0
            for p in grp:
                if isinstance(p, int):
                    v_refs[i][o:o + p, :] = jnp.zeros((p, shapes[i][1]), F32)
                    o += p
                else:
                    k = next(t for t, a in enumerate(arrays) if a is p[0])
                    v_refs[i][o:o + p[2] - p[1], :] = a_refs[k][p[1]:p[2], :]
                    o += p[2] - p[1]
        x, y, c, chips = _place()
        q = 2 * x + y
        to_sib = [pltpu.make_async_remote_copy(src_ref=v_refs[i], dst_ref=sibs[i], send_sem=s_sem.at[i, 3], recv_sem=r_sem.at[i, 3],
                                               device_id=(x, y, 1 - c), device_id_type=MESH) for i in range(n)]
        for cp in to_sib:
            cp.start()
        cps = []
        for i in range(n):
            to_sib[i].wait()
            mine, other = v_refs[i][...], sibs[i][...]
            css[i][...] = jnp.where(c == 0, mine, other) + jnp.where(c == 0, other, mine)
            for j, chip in enumerate(chips):
                cp = pltpu.make_async_remote_copy(src_ref=css[i], dst_ref=gots[i].at[j], send_sem=s_sem.at[i, j],
                                                  recv_sem=r_sem.at[i, j], device_id=(*chip, c), device_id_type=MESH)
                cp.start()
                cps.append(cp)
        for cp in cps:
            cp.wait()
        flips = [2, 1, 3]
        for i in range(n):
            total = None
            for k in range(N_CHIPS):
                d = q ^ k
                term = jnp.where(d == 0, css[i][...], 0.0)
                for j in range(3):
                    term = jnp.where(d == flips[j], gots[i][j], term)
                total = term if total is None else total + term
            o_refs[i][...] = total

    vm = pl.BlockSpec(memory_space=pltpu.VMEM)
    return pl.pallas_call(
        body, name="small_allreduce", in_specs=[vm] * na, out_specs=[vm] * n, out_shape=[S_(s, F32) for s in shapes],
        scratch_shapes=[pltpu.VMEM(s, F32) for s in shapes] * 3 + [pltpu.VMEM((3,) + s, F32) for s in shapes]
        + [pltpu.SemaphoreType.DMA((n, 4)), pltpu.SemaphoreType.DMA((n, 4))],
    )(*arrays)


def _row_block(r, c, target_bytes=1 << 20):
    br = r
    while br % 32 == 0 and br * c * 4 > target_bytes:
        br //= 2
    return br


def _chip_sum(part, sib, ids, name):
    _, _, r, c = part.shape
    br = _row_block(r, c)

    def body(ids_ref, p_ref, s_ref, o_ref):
        o_ref[...] = (p_ref[...].astype(F32) + s_ref[...].astype(F32)).astype(BF16)

    return _pcall(body, name=name, grid=(N_CHIPS, r // br), prefetch=1,
                  in_specs=[pl.BlockSpec((None, None, br, c), lambda s, i, ids: (s, ids[1], i, 0)),
                            pl.BlockSpec((None, br, c), lambda s, i, ids: (s, i, 0))],
                  out_specs=pl.BlockSpec((None, br, c), lambda s, i, ids: (s, i, 0)),
                  out_shape=S_((N_CHIPS, r, c), BF16), args=(ids, part, sib))


def _final_sum(part, sib, got, ids, name, layer=0, n_layers=1, buf=None):
    _, r, c = got.shape
    br = _row_block(r, c)

    def body(ids_ref, p_ref, s_ref, got_ref, *rest):
        v = p_ref[...].astype(F32) + s_ref[...].astype(F32)
        for j in range(3):
            v = v + got_ref[j].astype(F32)
        rest[-1][...] = v

    in_specs = [pl.BlockSpec((None, None, br, c), lambda i, ids: (ids[0], ids[1], i, 0)),
                pl.BlockSpec((None, br, c), lambda i, ids: (ids[0], i, 0)),
                pl.BlockSpec((3, br, c), lambda i, ids: (0, i, 0))]
    args = [ids, part, sib, got]
    if buf is not None:
        in_specs.append(pl.BlockSpec(memory_space=pl.ANY))
        args.append(buf)
    return _pcall(body, name=name, grid=(r // br,), prefetch=1, in_specs=in_specs,
                  out_specs=pl.BlockSpec((None, None, br, c), lambda i, ids: (layer, ids[1], i, 0)),
                  out_shape=S_((n_layers, 2, r, c), F32), args=args, aliases={4: 0} if buf is not None else None)


def _adam_math(w, g, m, v):
    c1 = 1.0 / (1.0 - ADAM_B1 ** ADAM_STEP)
    c2 = 1.0 / (1.0 - ADAM_B2 ** ADAM_STEP)
    mn = ADAM_B1 * m + (1.0 - ADAM_B1) * g
    vn = ADAM_B2 * v + (1.0 - ADAM_B2) * (g * g)
    return -ADAM_LR * ((mn * c1) / (jnp.sqrt(vn * c2) + ADAM_EPS) + ADAM_WD * w), mn, vn


def _adamw(w, g, m, v, name):
    nl, r, c = w.shape
    br = _row_block(r, c, 1 << 19)

    def body(w_ref, g_ref, m_ref, v_ref, d_ref, mo_ref, vo_ref):
        d_ref[...], mo_ref[...], vo_ref[...] = _adam_math(w_ref[...], g_ref[...], m_ref[...], v_ref[...])

    blk = pl.BlockSpec((None, br, c), lambda l, i: (l, i, 0))
    return _pcall(body, name=name, grid=(nl, r // br), in_specs=[blk] * 4, out_specs=[blk] * 3,
                  out_shape=[S_((nl, r, c), F32)] * 3, args=(w, g, m, v))


def _adamw_small(rd, rf, sharded_g, ws, ms, vs, rows_d, rows_f, loss_row):
    n = len(ws)
    ns = len(sharded_g)
    rep = [k for k in range(n) if rows_d[k] is not None or rows_f[k] is not None]

    def body(*refs):
        rd_ref, rf_ref = refs[0], refs[1]
        sg = refs[2:2 + ns]
        w_refs, m_refs, v_refs = (refs[2 + ns + t * n:2 + ns + (t + 1) * n] for t in range(3))
        loss_ref, outs = refs[2 + ns + 3 * n], refs[3 + ns + 3 * n:]
        g_out, d_out, m_out, v_out = outs[:len(rep)], outs[len(rep):len(rep) + n], outs[len(rep) + n:len(rep) + 2 * n], \
            outs[len(rep) + 2 * n:]
        loss_ref[...] = jnp.sum(rd_ref[loss_row:loss_row + 1, :], axis=1, keepdims=True)
        si = 0
        for k in range(n):
            shape = w_refs[k].shape
            if rows_d[k] is not None or rows_f[k] is not None:
                src, rws = (rd_ref, rows_d[k]) if rows_d[k] is not None else (rf_ref, rows_f[k])
                axis = 0 if shape[0] == len(rws) else 1
                g = src[rws[0]:rws[0] + 1, :]
                if len(rws) > 1:
                    g = jnp.concatenate([src[r:r + 1, :] for r in rws], axis=axis)
                g_out[rep.index(k)][...] = g
            else:
                g = sg[si][...]
                si += 1
            d_out[k][...], m_out[k][...], v_out[k][...] = _adam_math(w_refs[k][...], g, m_refs[k][...], v_refs[k][...])

    vm = pl.BlockSpec(memory_space=pltpu.VMEM)
    shapes = [S_(w.shape, F32) for w in ws]
    outs = pl.pallas_call(
        body, name="adamw_small", in_specs=[vm] * (2 + ns + 3 * n), out_specs=[vm] * (1 + len(rep) + 3 * n),
        out_shape=[S_((1, 1), F32)] + [shapes[k] for k in rep] + shapes * 3,
    )(rd, rf, *sharded_g, *ws, *ms, *vs)
    g_rep = dict(zip(rep, outs[1:1 + len(rep)]))
    o = outs[1 + len(rep):]
    return outs[0], g_rep, o[:n], o[n:2 * n], o[2 * n:]


def _pack(pieces, rows):
    flat = jnp.concatenate([p.reshape(-1).astype(F32) for p in pieces])
    return jnp.pad(flat, (0, rows * 128 - flat.shape[0])).reshape(rows, 128)


def _unpack(packed, shapes):
    flat = packed.reshape(-1)
    out, o = [], 0
    for s in shapes:
        n = 1
        for d in s:
            n *= d
        out.append(flat[o:o + n].reshape(s))
        o += n
    return out


def _rows_for(pieces_or_shapes):
    n = 0
    for p in pieces_or_shapes:
        k = 1
        for d in (p if isinstance(p, tuple) else p.shape):
            k *= d
        n += k
    return -(-n // 1024) * 8


def kernel(x, norm_mix, norm_ffn, conv_w_pw1, conv_b_pw1, conv_w_dw, conv_b_dw, conv_ln_g, conv_ln_b, conv_w_pw2, conv_b_pw2, pool_w, pool_b, pool_scale, ffn_w_up, ffn_w_dw, ffn_b_dw, ffn_w_down, final_norm, loss_target, m_norm_mix, m_norm_ffn, m_conv_w_pw1, m_conv_b_pw1, m_conv_w_dw, m_conv_b_dw, m_conv_ln_g, m_conv_ln_b, m_conv_w_pw2, m_conv_b_pw2, m_pool_w, m_pool_b, m_pool_scale, m_ffn_w_up, m_ffn_w_dw, m_ffn_b_dw, m_ffn_w_down, m_final_norm, v_norm_mix, v_norm_ffn, v_conv_w_pw1, v_conv_b_pw1, v_conv_w_dw, v_conv_b_dw, v_conv_ln_g, v_conv_ln_b, v_conv_w_pw2, v_conv_b_pw2, v_pool_w, v_pool_b, v_pool_scale, v_ffn_w_up, v_ffn_w_dw, v_ffn_b_dw, v_ffn_w_down, v_final_norm):
    nb, seq, D = x.shape
    T = nb * seq
    F = ffn_w_down.shape[1] * N_CHIPS
    taps = conv_w_dw.shape[1]
    ffn_taps = ffn_w_dw.shape[1]
    ng = pool_w.shape[1]
    cg = pool_w.shape[3]
    dsh = D // N_CHIPS
    fsh = F // N_CHIPS
    assert taps - 1 <= HALO_CONV and ffn_taps - 1 <= HALO_FFN and max(POOL_WINDOWS) <= HALO_POOL
    tm = min(TILE_D, seq)
    tf = min(TILE_F, seq)
    assert seq % tm == 0 and seq % tf == 0 and tm % HALO_CONV == 0 and tf % CHUNK == 0
    tps, tpf = seq // tm, seq // tf
    tw = min(TILE_MM, T)
    assert T % tw == 0

    xi, yi, ci = lax.axis_index("x"), lax.axis_index("y"), lax.axis_index("c")
    qi = 2 * xi + yi
    ids = jnp.stack([qi, ci]).astype(jnp.int32)
    x2d = x.reshape(T, D)
    tg2d = loss_target.reshape(T, D)

    small_sharded = [conv_w_dw[0], ffn_w_dw, pool_b, pool_scale]
    srows = 2 * _rows_for(small_sharded)
    s_pw1, s_pw2 = _own_slot(conv_w_pw1, 0, ids, BF16, "slot_pw1"), _own_slot(conv_w_pw2, 0, ids, BF16, "slot_pw2")
    s_small = _own_slot(_pack(small_sharded, srows)[None], 0, ids, F32, "slot_small")
    s_pool = _own_slot(pool_w.reshape(1, ng * (cg // N_CHIPS), cg), 0, ids, BF16, "slot_pool")
    s_up = [_own_slot(ffn_w_up, l, ids, BF16, "slot_up%d" % l) for l in range(2)]
    s_down = [_own_slot(ffn_w_down, l, ids, BF16, "slot_down%d" % l) for l in range(2)]
    (w1g,) = _comm_only(_GatherComm([s_pw1]), "gather_pw1")
    hb0t, a, w2g, smallg = _conv_in(x2d, norm_mix[0:1], w1g, conv_b_pw1, tm, comm=_GatherComm([s_pw2, s_small]))
    w2f = w2g.reshape(D, D)
    sm_parts = [_unpack(smallg[s], [(taps, dsh), (2, ffn_taps, fsh), (1, dsh), (1, dsh)]) for s in range(N_CHIPS)]
    wdw_f = jnp.concatenate([p[0] for p in sm_parts], axis=1)
    wdw_f = jnp.pad(wdw_f, ((0, HALO_CONV - taps), (0, 0)))
    fdw_f = jnp.concatenate([p[1] for p in sm_parts], axis=2)
    fdw_f = jnp.pad(fdw_f, ((0, 0), (0, 8 - ffn_taps), (0, 0)))
    pb_f = jnp.concatenate([p[2] for p in sm_parts], axis=1)
    ps_f = jnp.concatenate([p[3] for p in sm_parts], axis=1)

    c, sbt, x1, hb1, hb1t, wup0, wd0, pwg = _conv_mid(a, x2d, wdw_f, conv_b_dw, conv_ln_g, conv_ln_b, w2f, conv_b_pw2,
                                                      norm_ffn[0:1], tm, tps, taps,
                                                      comm=_GatherComm([s_up[0], s_down[0], s_pool]))
    up0, wup1 = _mm_up(hb1, wup0, tw, "ffn_up0", comm=_GatherComm([s_up[1]]))
    gb0t, x2, h2, wd1 = _ffn_down(up0, x1, fdw_f[0], ffn_b_dw[0:1], wd0.reshape(F, D), norm_mix[1:2], tf, tpf, ffn_taps,
                                  "ffn_down0", comm=_GatherComm([s_down[1]]))
    wup = [wup0, wup1]
    wdn = [wd0.reshape(F, D), wd1.reshape(F, D)]
    pwf = pwg.reshape(N_CHIPS, ng, cg // N_CHIPS, cg).transpose(1, 0, 2, 3).reshape(ng, cg, cg)
    plb, x3, hb3, hb3t = _pool_fwd(h2, x2, pwf, pb_f, ps_f, norm_ffn[1:2], tm, tps)
    up1 = _mm_up(hb3, wup[1], tw, "ffn_up1")
    gb1t, dx4, loss_part, d_final = _ffn_down(up1, x3, fdw_f[1], ffn_b_dw[1:2], wdn[1], final_norm.reshape(1, D), tf, tpf,
                                              ffn_taps, "ffn_down1", target=tg2d)

    fs = {}

    def chip_sums(group, tag):
        nms = list(group)
        from_sib = _pair_exchange([group[nm] for nm in nms], "grad_pair_exchange_" + tag)
        sums = [_chip_sum(group[nm], s, ids, "chip_sum_" + nm) for nm, s in zip(nms, from_sib)]
        for nm, s in zip(nms, from_sib):
            fs[nm] = (group[nm], s)
        return nms, _ExchangeComm(sums)

    def arrived(nms, got):
        for nm, g in zip(nms, got):
            fs[nm] = fs[nm] + (g,)

    dac1, dup1 = _bwd_down(dx4, up1, fdw_f[1], ffn_b_dw[1:2], wdn[1], tf, tpf, ffn_taps, "bwd_down1")
    p_down1 = _mm_tn(gb1t, dx4, F // 2, D, tw, False, "dw_down1")
    nms, xchg = chip_sums({"down1": p_down1.reshape(N_CHIPS, 2, fsh // 2, D)}, "down1")
    dup1, dx3, dnf1, fw1, *got = _bwd_up(dac1, dup1, wup[1], fdw_f[1], x3, norm_ffn[1:2], dx4, up1, tf, tpf, ffn_taps,
                                         "bwd_up1", comm=xchg)
    arrived(nms, got)
    p_up1 = _mm_tn(hb3t, dup1, D, 2 * fsh, tw, True, "dw_up1")
    nms, xchg = chip_sums({"up1": p_up1.reshape(N_CHIPS, 2, D // 2, 2 * fsh)}, "up1")
    dx2, dpw, pool_sm = _pool_bwd(dx3, plb, x2, pwf, pb_f, ps_f, norm_mix[1:2], tm, tps)
    dac0, dup0, *got = _bwd_down(dx2, up0, fdw_f[0], ffn_b_dw[0:1], wdn[0], tf, tpf, ffn_taps, "bwd_down0", comm=xchg)
    arrived(nms, got)
    p_down0 = _mm_tn(gb0t, dx2, F // 2, D, tw, False, "dw_down0")
    nms, xchg = chip_sums({"down0": p_down0.reshape(N_CHIPS, 2, fsh // 2, D)}, "down0")
    dup0, dx1, dnf0, fw0, *got = _bwd_up(dac0, dup0, wup[0], fdw_f[0], x1, norm_ffn[0:1], dx2, up0, tf, tpf, ffn_taps,
                                         "bwd_up0", comm=xchg)
    arrived(nms, got)
    p_up0 = _mm_tn(hb1t, dup0, D, 2 * fsh, tw, True, "dw_up0")
    dc, conv_sm = _conv_bwd1(dx1, c, w2f, conv_ln_g, conv_ln_b, tm)
    p_pw2 = _mm_tn(sbt, dx1, D, D, tw, False, "dw_pw2")
    csh = cg // N_CHIPS
    p_pool = dpw.reshape(2, ng // 2, N_CHIPS, csh, cg).transpose(2, 0, 1, 3, 4).reshape(N_CHIPS, 2, (ng // 2) * csh, cg)
    nms, xchg = chip_sums({"up0": p_up0.reshape(N_CHIPS, 2, D // 2, 2 * fsh), "pool": p_pool.astype(BF16),
                           "pw2": p_pw2.reshape(N_CHIPS, 2, dsh // 2, D)}, "mix")
    grad_x, dab, dwdw, db1, dnm0, *got = _conv_bwd2(dc, a, x2d, w1g, wdw_f, norm_mix[0:1], dx1, tm, tps, taps, comm=xchg)
    arrived(nms, got)
    p_pw1 = _mm_tn(hb0t, dab, D, 2 * dsh, tw, True, "dw_pw1")
    nms, xchg = chip_sums({"pw1": p_pw1.reshape(N_CHIPS, 2, D // 2, 2 * dsh)}, "pw1")
    bufs = [_final_sum(*fs[nm], ids, "final_sum_" + nm) for nm in ("pw2", "pool")]
    for nm in ("up", "down"):
        b = _final_sum(*fs[nm + "0"], ids, "final_sum_" + nm + "0", layer=0, n_layers=2)
        bufs.append(_final_sum(*fs[nm + "1"], ids, "final_sum_" + nm + "1", layer=1, n_layers=2, buf=b))
    *shared, got_pw1 = _comm_only(_Both(_ShareComm(bufs), xchg), "grad_share_halves")
    arrived(nms, [got_pw1])
    shared += _comm_only(_ShareComm([_final_sum(*fs["pw1"], ids, "final_sum_pw1")]), "grad_share_halves_pw1")
    g_pw2, g_pool, g_up, g_down, g_pw1 = [b.reshape(b.shape[0], 2 * b.shape[2], b.shape[3]) for b in shared]

    rd, rf = _small_allreduce([
        [(dnm0, 0, 1), (pool_sm, 2, 3), (dnf0, 0, 1), (dnf1, 0, 1), (db1, 0, 2), (conv_sm, 0, 4), (d_final, 0, 1),
         (pool_sm, 0, 2), (loss_part, 0, 1), 2, (dwdw, 0, dwdw.shape[0])],
        [(fw0, 0, 4), (fw1, 0, 4)]])
    g_wdw = lax.dynamic_slice_in_dim(rd[16:16 + taps], qi * dsh, dsh, axis=1)
    g_ps = lax.dynamic_slice_in_dim(rd[11:12], qi * dsh, dsh, axis=1)
    g_pb = lax.dynamic_slice_in_dim(rd[12:13], qi * dsh, dsh, axis=1)
    g_fdw = lax.dynamic_slice_in_dim(rf.reshape(2, 4, F)[:, :ffn_taps], qi * fsh, fsh, axis=2).reshape(2 * ffn_taps, fsh)

    def big(w, g, m, v, nm):
        d, mn, vn = _adamw(w.reshape(g.shape), g, m.reshape(g.shape), v.reshape(g.shape), "adamw_" + nm)
        return g.reshape(w.shape), d.reshape(w.shape), mn.reshape(w.shape), vn.reshape(w.shape)

    o_pw1 = big(conv_w_pw1, g_pw1, m_conv_w_pw1, v_conv_w_pw1, "pw1")
    o_pw2 = big(conv_w_pw2, g_pw2, m_conv_w_pw2, v_conv_w_pw2, "pw2")
    o_pool = big(pool_w, g_pool, m_pool_w, v_pool_w, "pool")
    o_up = big(ffn_w_up, g_up, m_ffn_w_up, v_ffn_w_up, "up")
    o_down = big(ffn_w_down, g_down, m_ffn_w_down, v_ffn_w_down, "down")

    snames = ["norm_mix", "norm_ffn", "b_pw1", "w_dw", "b_dw", "ln_g", "ln_b", "b_pw2", "pool_b", "pool_scale", "ffn_w_dw",
              "ffn_b_dw", "final_norm"]
    sw = [norm_mix, norm_ffn, conv_b_pw1, conv_w_dw, conv_b_dw, conv_ln_g, conv_ln_b, conv_b_pw2, pool_b, pool_scale,
          ffn_w_dw, ffn_b_dw, final_norm]
    smm = [m_norm_mix, m_norm_ffn, m_conv_b_pw1, m_conv_w_dw, m_conv_b_dw, m_conv_ln_g, m_conv_ln_b, m_conv_b_pw2, m_pool_b,
           m_pool_scale, m_ffn_w_dw, m_ffn_b_dw, m_final_norm]
    svv = [v_norm_mix, v_norm_ffn, v_conv_b_pw1, v_conv_w_dw, v_conv_b_dw, v_conv_ln_g, v_conv_ln_b, v_conv_b_pw2, v_pool_b,
           v_pool_scale, v_ffn_w_dw, v_ffn_b_dw, v_final_norm]
    sshapes = [tuple(w.shape) for w in sw]
    two_d = [(2, D), (2, D), (1, 2 * D), (taps, dsh), (1, D), (1, D), (1, D), (1, D), (1, dsh), (1, dsh), (2 * ffn_taps, fsh),
             (2, F), (1, D)]
    rows_d = [[0, 1], [2, 3], [4, 5], None, [9], [6], [7], [8], None, None, None, None, [10]]
    rows_f = [None] * 11 + [[3, 7], None]
    as2d = lambda ts: [t.reshape(s) for t, s in zip(ts, two_d)]
    loss, g_rep, sd, sm_new, sv_new = _adamw_small(rd, rf, [g_wdw, g_pb, g_ps, g_fdw], as2d(sw), as2d(smm), as2d(svv), rows_d,
                                                   rows_f, loss_row=13)
    sg = dict(g_rep)
    sg.update({3: g_wdw, 8: g_pb, 9: g_ps, 10: g_fdw})
    small_out = {n: tuple(t.reshape(sshapes[k]) for t in (sg[k], sd[k], sm_new[k], sv_new[k])) for k, n in enumerate(snames)}

    order = [small_out["norm_mix"], small_out["norm_ffn"], o_pw1, small_out["b_pw1"], small_out["w_dw"], small_out["b_dw"],
             small_out["ln_g"], small_out["ln_b"], o_pw2, small_out["b_pw2"], o_pool, small_out["pool_b"],
             small_out["pool_scale"], o_up, small_out["ffn_w_dw"], small_out["ffn_b_dw"], o_down, small_out["final_norm"]]
    return (loss[0, 0], grad_x.reshape(nb, seq, D), *[o[0] for o in order], *[o[1] for o in order], *[o[2] for o in order],
            *[o[3] for o in order])
```

```python
import functools

import jax
import jax.numpy as jnp
from jax import lax
from jax.experimental import pallas as pl
from jax.experimental.pallas import tpu as pltpu

F32, BF16 = jnp.float32, jnp.bfloat16
S_ = jax.ShapeDtypeStruct
MESH = pl.DeviceIdType.MESH

RMS_EPS, LN_EPS = 1e-6, 1e-5
POOL_WINDOWS = (2, 4, 8, 16)
ADAM_LR, ADAM_B1, ADAM_B2, ADAM_EPS, ADAM_WD, ADAM_STEP = 0.001, 0.9, 0.999, 1e-08, 0.01, 10

VMEM_LIMIT_BYTES = 60 * 1024 * 1024
N_CHIPS = 4
CHUNK = 16
HALO_CONV = 32
HALO_POOL = 32
HALO_FFN = 8
TILE_D = 512
TILE_F = 256
TILE_MM = 1024


def _pcall(body, *, name, grid, in_specs, out_specs, out_shape, args, scratch=(), aliases=None, prefetch=0, comm=None):
    params = pltpu.CompilerParams(dimension_semantics=("arbitrary",) * len(grid), vmem_limit_bytes=VMEM_LIMIT_BYTES)
    if comm is not None:
        assert not prefetch
        single = not isinstance(out_shape, (list, tuple))
        in_specs, args, scratch = list(in_specs), list(args), list(scratch)
        out_specs, out_shape = ([out_specs], [out_shape]) if single else (list(out_specs), list(out_shape))
        n_in, n_out, n_scr, n_cin, n_cout = len(in_specs), len(out_specs), len(scratch), len(comm.inputs), len(comm.out_shape)
        any_spec = pl.BlockSpec(memory_space=pl.ANY)
        aliases = dict(aliases or {})
        aliases.update({n_in + a: n_out + b for a, b in comm.aliases.items()})
        inner = body

        def body(*refs):
            ins, cin = refs[:n_in], refs[n_in:n_in + n_cin]
            outs = refs[n_in + n_cin:n_in + n_cin + n_out]
            cout = refs[n_in + n_cin + n_out:n_in + n_cin + n_out + n_cout]
            scr = refs[n_in + n_cin + n_out + n_cout:n_in + n_cin + n_out + n_cout + n_scr]
            sems = refs[n_in + n_cin + n_out + n_cout + n_scr:]
            step, total = 0, 1
            for d, g in enumerate(grid):
                step = step * g + pl.program_id(d)
                total *= g
            comm.before(cin, cout, sems, step, total)
            inner(*ins, *outs, *scr)
            comm.after(cin, cout, sems, step, total)

        in_specs += [any_spec] * n_cin
        args += list(comm.inputs)
        out_specs += [any_spec] * n_cout
        out_shape += list(comm.out_shape)
        scratch += list(comm.sems)
    if prefetch:
        spec = pltpu.PrefetchScalarGridSpec(num_scalar_prefetch=prefetch, grid=grid, in_specs=in_specs,
                                            out_specs=out_specs, scratch_shapes=list(scratch))
        return pl.pallas_call(body, name=name, grid_spec=spec, out_shape=out_shape,
                              input_output_aliases=aliases or {}, compiler_params=params)(*args)
    return pl.pallas_call(body, name=name, grid=grid, in_specs=in_specs, out_specs=out_specs, out_shape=out_shape,
                          scratch_shapes=list(scratch), input_output_aliases=aliases or {},
                          compiler_params=params)(*args)


def _full(shape):
    n = len(shape)
    return pl.BlockSpec(tuple(shape), lambda *_: (0,) * n, pipeline_mode=pl.Buffered(1))


def _rows(tm, c, col=0):
    return pl.BlockSpec((tm, c), lambda i, *_: (i, col))


def _cols(r, tm):
    return pl.BlockSpec((r, tm), lambda i, *_: (0, i))


def _prev_halo(tm, hb, c):
    return pl.BlockSpec((hb, c), lambda i, *_: (jnp.maximum(i * (tm // hb) - 1, 0), 0))


def _next_halo(tm, hb, c, total_rows):
    last = total_rows // hb - 1
    return pl.BlockSpec((hb, c), lambda i, *_: (jnp.minimum((i + 1) * (tm // hb), last), 0))


def _acc_spec(r, c):
    return pl.BlockSpec((r, c), lambda *_: (0, 0))


def _rms(x, g):
    r = lax.rsqrt(jnp.mean(x * x, axis=-1, keepdims=True) + RMS_EPS)
    xh = x * r
    return xh * g, xh, r


def _rms_bwd(dy, xh, r, g):
    dxh = dy * g
    dx = r * (dxh - xh * jnp.mean(dxh * xh, axis=-1, keepdims=True))
    return dx, jnp.sum(dy * xh, axis=0, keepdims=True)


def _colsum(v):
    return jnp.sum(v, axis=0, keepdims=True)


def _fold8(v):
    out = v[0:8]
    for j in range(1, v.shape[0] // 8):
        out = out + v[8 * j:8 * j + 8]
    return out


def _chunks(n_rows, fn):
    def step(j, carry):
        fn(pl.multiple_of(j * CHUNK, CHUNK))
        return carry
    lax.fori_loop(0, n_rows // CHUNK, step, 0)


def _dot(a, b):
    return jnp.dot(a, b, preferred_element_type=F32)


def _dot_nt(a, b):
    return lax.dot_general(a, b, (((1,), (1,)), ((), ())), preferred_element_type=F32)


def _dot_tn(a, b):
    return lax.dot_general(a, b, (((0,), (0,)), ((), ())), preferred_element_type=F32)


def _sig(v):
    return jax.nn.sigmoid(v)


def _accumulate(ref, val, first):
    @pl.when(first)
    def _():
        ref[...] = val

    @pl.when(jnp.logical_not(first))
    def _():
        ref[...] = ref[...] + val


def _conv_in(x, g, w1g, b1, tm, comm=None):
    T, D = x.shape
    ns, _, nc = w1g.shape

    def body(x_ref, g_ref, w_ref, b_ref, hbt_ref, a_ref):
        hb = _rms(x_ref[...], g_ref[...])[0].astype(BF16)
        hbt_ref[...] = hb.T
        for s in range(ns):
            a_ref[:, s * nc:(s + 1) * nc] = _dot(hb, w_ref[s]) + b_ref[:, s * nc:(s + 1) * nc]

    return _pcall(body, name="conv_in", grid=(T // tm,),
                  in_specs=[_rows(tm, D), _full((1, D)), _full(w1g.shape), _full((1, ns * nc))],
                  out_specs=[_cols(D, tm), _rows(tm, ns * nc)],
                  out_shape=[S_((D, T), BF16), S_((T, ns * nc), F32)], args=(x, g, w1g, b1), comm=comm)


def _conv_mid(a, x, wdw, bdw, lng, lnb, w2, b2, gn, tm, tps, taps, comm=None):
    T, D = x.shape
    hc = HALO_CONV

    def body(a_ref, ah_ref, x_ref, wdw_ref, bdw_ref, lng_ref, lnb_ref, w2_ref, b2_ref, gn_ref,
             c_ref, sbt_ref, x1_ref, hb_ref, hbt_ref, uext, shifted):
        i = pl.program_id(0)
        ah = ah_ref[...]
        uext[0:hc, :] = jnp.where(i % tps == 0, 0.0, ah[:, :D] * _sig(ah[:, D:]))
        av = a_ref[...]
        uext[hc:hc + tm, :] = av[:, :D] * _sig(av[:, D:])

        def conv(r0, src, ks, first):
            acc = jnp.broadcast_to(bdw_ref[...], (CHUNK, D)) if first else c_ref[pl.ds(r0, CHUNK), :]
            for k, o in ks:
                acc = acc + wdw_ref[k:k + 1, :] * src[pl.ds(pl.multiple_of(r0 + o, 8), CHUNK), :]
            c_ref[pl.ds(r0, CHUNK), :] = acc
        _by_sublane_shift(uext, shifted, tm, [hc - (taps - 1) + k for k in range(taps)], conv)

        cv = c_ref[...]
        xc = cv - jnp.mean(cv, axis=-1, keepdims=True)
        nh = xc * lax.rsqrt(jnp.mean(xc * xc, axis=-1, keepdims=True) + LN_EPS)
        n = nh * lng_ref[...] + lnb_ref[...]
        sb = (n * _sig(n)).astype(BF16)
        sbt_ref[...] = sb.T
        x1 = x_ref[...] + _dot(sb, w2_ref[...]) + b2_ref[...]
        x1_ref[...] = x1
        hb = _rms(x1, gn_ref[...])[0].astype(BF16)
        hb_ref[...] = hb
        hbt_ref[...] = hb.T

    return _pcall(body, name="conv_mid", grid=(T // tm,),
                  in_specs=[_rows(tm, 2 * D), _prev_halo(tm, hc, 2 * D), _rows(tm, D), _full(wdw.shape), _full((1, D)),
                            _full((1, D)), _full((1, D)), _full((D, D)), _full((1, D)), _full((1, D))],
                  out_specs=[_rows(tm, D), _cols(D, tm), _rows(tm, D), _rows(tm, D), _cols(D, tm)],
                  out_shape=[S_((T, D), F32), S_((D, T), BF16), S_((T, D), F32), S_((T, D), BF16), S_((D, T), BF16)],
                  scratch=[pltpu.VMEM((tm + hc, D), F32), pltpu.VMEM((tm + hc - 8, D), F32)],
                  args=(a, a, x, wdw, bdw, lng, lnb, w2, b2, gn), comm=comm)


def _mm_up(hb, wg, tm, name, comm=None):
    T, D = hb.shape
    ns, _, nc = wg.shape

    def body(h_ref, w_ref, o_ref):
        o_ref[...] = _dot(h_ref[...], w_ref[...])

    return _pcall(body, name=name, grid=(ns, T // tm),
                  in_specs=[pl.BlockSpec((tm, D), lambda s, i: (i, 0)), pl.BlockSpec((None, D, nc), lambda s, i: (s, 0, 0))],
                  out_specs=pl.BlockSpec((tm, nc), lambda s, i: (i, s)),
                  out_shape=S_((T, ns * nc), F32), args=(hb, wg), comm=comm)


def _ffn_taps(aext, realigned, tm, ffn_taps):
    for k in range(ffn_taps - 1):
        realigned[k] = aext[pl.ds(HALO_FFN - (ffn_taps - 1) + k, tm), :]
    return [realigned.at[k] for k in range(ffn_taps - 1)] + [aext.at[pl.ds(HALO_FFN, tm)]]


def _ffn_conv(taps, wdw_ref, bdw_ref):
    acc = bdw_ref[...]
    for k, tap in enumerate(taps):
        acc = acc + wdw_ref[k:k + 1, :] * tap[...]
    return acc


def _by_sublane_shift(ext, shifted, tm, offsets, per_chunk):
    first = True
    for b in range(8):
        ks = [(k, o - b) for k, o in enumerate(offsets) if o % 8 == b]
        if not ks:
            continue
        if b:
            shifted[...] = ext[pl.ds(b, shifted.shape[0]), :]
        _chunks(tm, functools.partial(per_chunk, src=shifted if b else ext, ks=ks, first=first))
        first = False


def _ffn_down(up, xin, wdw, bdw, wd, gn, tm, tps, ffn_taps, name, target=None, comm=None):
    T, D = xin.shape
    F = up.shape[1] // 2
    hf = HALO_FFN
    final = target is not None

    def body(*refs):
        if final:
            (up_ref, uph_ref, x_ref, wdw_ref, bdw_ref, wd_ref, gn_ref, tg_ref,
             gb_ref, dx_ref, loss_ref, dgn_ref, aext, realigned) = refs
        else:
            (up_ref, uph_ref, x_ref, wdw_ref, bdw_ref, wd_ref, gn_ref,
             gb_ref, xo_ref, h_ref, aext, realigned) = refs
        i = pl.program_id(0)
        aext[0:hf, :] = jnp.where(i % tps == 0, 0.0, uph_ref[...])
        aext[hf:hf + tm, :] = up_ref[:, 0:F]
        acv = _ffn_conv(_ffn_taps(aext, realigned, tm, ffn_taps), wdw_ref, bdw_ref)
        gb = (acv * _sig(acv) * up_ref[:, F:2 * F]).astype(BF16)
        gb_ref[...] = gb.T
        xo = x_ref[...] + _dot(gb, wd_ref[...])
        if not final:
            xo_ref[...] = xo
            h_ref[...] = _rms(xo, gn_ref[...])[0]
        else:
            out, xh, r = _rms(xo, gn_ref[...])
            diff = out - tg_ref[...]
            part = 0.5 / D * _colsum(diff * diff)
            dx, dg = _rms_bwd(diff * (1.0 / D), xh, r, gn_ref[...])
            dx_ref[...] = dx
            _accumulate(loss_ref, part, i == 0)
            _accumulate(dgn_ref, dg, i == 0)

    in_specs = [_rows(tm, 2 * F), _prev_halo(tm, hf, F), _rows(tm, D), _full(wdw.shape), _full((1, F)),
                _full((F, D)), _full((1, D))]
    args = [up, up, xin, wdw, bdw, wd, gn]
    if final:
        in_specs.append(_rows(tm, D))
        args.append(target)
        out_specs = [_cols(F, tm), _rows(tm, D), _acc_spec(1, D), _acc_spec(1, D)]
        out_shape = [S_((F, T), BF16), S_((T, D), F32), S_((1, D), F32), S_((1, D), F32)]
    else:
        out_specs = [_cols(F, tm), _rows(tm, D), _rows(tm, D)]
        out_shape = [S_((F, T), BF16), S_((T, D), F32), S_((T, D), F32)]
    return _pcall(body, name=name, grid=(T // tm,), in_specs=in_specs, out_specs=out_specs, out_shape=out_shape,
                  scratch=[pltpu.VMEM((tm + hf, F), F32), pltpu.VMEM((ffn_taps - 1, tm, F), F32)], args=args, comm=comm)


def _pool_counts(i, tps, tm, w):
    pos = (i % tps) * tm + lax.broadcasted_iota(jnp.int32, (tm, 1), 0)
    return jnp.minimum(pos + 1, w).astype(F32)


def _window_sums(src, buf_a, buf_b, cg, causal):
    assert POOL_WINDOWS == (2, 4, 8, 16)
    n = src.shape[0]
    levels = len(POOL_WINDOWS)
    cur, outs = src, []
    for l in range(levels):
        dst = buf_b if l % 2 else buf_a
        cols = slice(l * cg, levels * cg)
        lo, m = 8 * (l + 1), n - 8 * (l + 1)
        if causal:
            dst[lo:n, cols] = cur[lo:n, cols] + cur[pl.ds(lo - (1 << l), m), cols]
        else:
            dst[0:m, cols] = cur[0:m, cols] + cur[pl.ds(1 << l, m), cols]
        outs.append(dst)
        cur = dst
    return outs


def _pool_fwd(h, x2, pw, pb, ps, gn, tm, tps):
    T, D = h.shape
    ng, cg, _ = pw.shape
    hp = HALO_POOL

    def body(h_ref, hh_ref, x_ref, pw_ref, pb_ref, ps_ref, gn_ref, pl_ref, xo_ref, hb_ref, hbt_ref, hext, buf_a, buf_b):
        i = pl.program_id(0)
        hext[0:hp, :] = jnp.where(i % tps == 0, 0.0, hh_ref[...])
        hext[hp:hp + tm, :] = h_ref[...]
        sums = _window_sums(hext, buf_a, buf_b, cg, True)
        for g, w in enumerate(POOL_WINDOWS):
            cs = slice(g * cg, (g + 1) * cg)
            pl_ref[:, cs] = (sums[g][hp:hp + tm, cs] / _pool_counts(i, tps, tm, w) - h_ref[:, cs]).astype(BF16)

        pv = pl_ref[...]
        mixed = jnp.concatenate([_dot(pv[:, g * cg:(g + 1) * cg], pw_ref[g]) for g in range(ng)], axis=1)
        xo = x_ref[...] + ps_ref[...] * (mixed + pb_ref[...])
        xo_ref[...] = xo
        hb = _rms(xo, gn_ref[...])[0].astype(BF16)
        hb_ref[...] = hb
        hbt_ref[...] = hb.T

    return _pcall(body, name="pool_fwd", grid=(T // tm,),
                  in_specs=[_rows(tm, D), _prev_halo(tm, hp, D), _rows(tm, D), _full(pw.shape), _full((1, D)), _full((1, D)),
                            _full((1, D))],
                  out_specs=[_rows(tm, D), _rows(tm, D), _rows(tm, D), _cols(D, tm)],
                  out_shape=[S_((T, D), BF16), S_((T, D), F32), S_((T, D), BF16), S_((D, T), BF16)],
                  scratch=[pltpu.VMEM((tm + hp, D), F32)] * 3, args=(h, h, x2, pw, pb, ps, gn))


def _bwd_down(dx, up, wdw, bdw, wd, tm, tps, ffn_taps, name, comm=None):
    T, D = dx.shape
    F = up.shape[1] // 2
    hf = HALO_FFN
    nt = T // tm

    def body(dx_ref, up_ref, uph_ref, wdw_ref, bdw_ref, wd_ref, dac_ref, dgate_ref, wsum_ref, aext, realigned):
        i = pl.program_id(0)
        aext[0:hf, :] = jnp.where(i % tps == 0, 0.0, uph_ref[...])
        aext[hf:hf + tm, :] = up_ref[:, 0:F]
        taps = _ffn_taps(aext, realigned, tm, ffn_taps)
        acv = _ffn_conv(taps, wdw_ref, bdw_ref)
        dg = _dot_nt(dx_ref[...].astype(BF16), wd_ref[...])
        sg = _sig(acv)
        dgate_ref[...] = (dg * acv * sg).astype(BF16)
        dac = dg * up_ref[:, F:2 * F] * (sg * (1.0 + acv * (1.0 - sg)))
        dac_ref[...] = dac
        rows = [_colsum(dac * tap[...]) for tap in taps]
        rows += [_colsum(dac), jnp.zeros((8 - ffn_taps - 1, F), F32)]
        _accumulate(wsum_ref, jnp.concatenate(rows, axis=0), i == 0)

    return _pcall(body, name=name, grid=(nt,),
                  in_specs=[_rows(tm, D), _rows(tm, 2 * F), _prev_halo(tm, hf, F), _full(wdw.shape), _full((1, F)),
                            _full((F, D))],
                  out_specs=[_rows(tm, F), _rows(tm, F, col=1), _acc_spec(8, F)],
                  out_shape=[S_((T, F), F32), S_((T, 2 * F), BF16), S_((8, F), F32)],
                  scratch=[pltpu.VMEM((tm + hf, F), F32), pltpu.VMEM((ffn_taps - 1, tm, F), F32)],
                  args=(dx, up, up, wdw, bdw, wd), comm=comm)


def _bwd_up(dac, dup, wg, wdw, xin, gn, dx, tm, tps, ffn_taps, name, comm=None):
    T, D = xin.shape
    F = dac.shape[1]
    ns, _, nc = wg.shape
    hf = HALO_FFN
    nt = T // tm

    def body(dac_ref, dach_ref, dgate_ref, wg_ref, wdw_ref, x_ref, gn_ref, dx_ref, dact_ref, dxo_ref, dgn_ref, dext):
        i = pl.program_id(0)
        dext[0:tm, :] = dac_ref[...]
        dext[tm:tm + hf, :] = jnp.where(i % tps == tps - 1, 0.0, dach_ref[...])
        dact = wdw_ref[ffn_taps - 1:ffn_taps, :] * dac_ref[...]
        for k in range(ffn_taps - 1):
            dact = dact + wdw_ref[k:k + 1, :] * dext[pl.ds(ffn_taps - 1 - k, tm), :]
        dact_ref[...] = dact.astype(BF16)

        dh = jnp.zeros((tm, D), F32)
        for s in range(ns):
            src = dact_ref if s < ns // 2 else dgate_ref
            o = (s % (ns // 2)) * nc
            dh = dh + _dot_nt(src[:, o:o + nc], wg_ref[s])
        _, xh, r = _rms(x_ref[...], gn_ref[...])
        dxn, dg = _rms_bwd(dh, xh, r, gn_ref[...])
        dxo_ref[...] = dx_ref[...] + dxn
        _accumulate(dgn_ref, dg, i == 0)

    return _pcall(body, name=name, grid=(nt,),
                  in_specs=[_rows(tm, F), _next_halo(tm, hf, F, T), _rows(tm, F, col=1), _full(wg.shape), _full(wdw.shape),
                            _rows(tm, D), _full((1, D)), _rows(tm, D)],
                  out_specs=[_rows(tm, F), _rows(tm, D), _acc_spec(1, D)],
                  out_shape=[S_((T, 2 * F), BF16), S_((T, D), F32), S_((1, D), F32)],
                  scratch=[pltpu.VMEM((tm + hf, F), F32)],
                  aliases={2: 0}, args=(dac, dac, dup, wg, wdw, xin, gn, dx), comm=comm)


def _mm_tn(xt, dy, bk, bn, tt, shard_major, name):
    K, T = xt.shape
    N = dy.shape[1]
    nt = T // tt

    def body(x_ref, dy_ref, o_ref, acc):
        t = pl.program_id(2)
        part = _dot(x_ref[...], dy_ref[...].astype(BF16))
        _accumulate(acc, part, t == 0)

        @pl.when(t == nt - 1)
        def _():
            o_ref[...] = acc[...].astype(o_ref.dtype)

    if shard_major:
        out_spec = pl.BlockSpec((None, bk, bn), lambda i, j, t: (j, i, 0))
        out_shape = S_((N // bn, K, bn), BF16)
    else:
        out_spec = pl.BlockSpec((bk, bn), lambda i, j, t: (i, j))
        out_shape = S_((K, N), BF16)
    return _pcall(body, name=name, grid=(K // bk, N // bn, nt),
                  in_specs=[pl.BlockSpec((bk, tt), lambda i, j, t: (i, t)), pl.BlockSpec((tt, bn), lambda i, j, t: (t, j))],
                  out_specs=out_spec, out_shape=out_shape, scratch=[pltpu.VMEM((bk, bn), F32)], args=(xt, dy))


def _pool_bwd(dx3, plb, x2, pw, pb, ps, gn, tm, tps):
    T, D = x2.shape
    ng, cg, _ = pw.shape
    hp = HALO_POOL
    nt = T // tm

    def body(do_ref, doh_ref, pl_ref, x_ref, pw_ref, pb_ref, ps_ref, gn_ref, dxo_ref, dpw_ref, sm_ref, qext, buf_a, buf_b,
             dh_s):
        i = pl.program_id(0)
        do = do_ref[...]
        dm = do * ps_ref[...]
        dmh = jnp.where(i % tps == tps - 1, 0.0, doh_ref[...]) * ps_ref[...]
        pv = pl_ref[...]
        mixed = jnp.concatenate([_dot(pv[:, g * cg:(g + 1) * cg], pw_ref[g]) for g in range(ng)], axis=1)
        dscale = _colsum(do * (mixed + pb_ref[...]))
        dbias = _colsum(dm)
        dmb = dm.astype(BF16)
        dmhb = dmh.astype(BF16)
        for g, w in enumerate(POOL_WINDOWS):
            cs = slice(g * cg, (g + 1) * cg)
            _accumulate(dpw_ref.at[g], _dot_tn(pv[:, cs], dmb[:, cs]), i == 0)
            dpo = _dot_nt(dmb[:, cs], pw_ref[g])
            dh_s[:, cs] = dpo
            qext[0:tm, cs] = dpo / _pool_counts(i, tps, tm, w)
            qext[tm:tm + hp, cs] = _dot_nt(dmhb[:, cs], pw_ref[g]) * (1.0 / w)
        sums = _window_sums(qext, buf_a, buf_b, cg, False)
        for g in range(ng):
            cs = slice(g * cg, (g + 1) * cg)
            dh_s[:, cs] = sums[g][0:tm, cs] - dh_s[:, cs]

        _, xh, r = _rms(x_ref[...], gn_ref[...])
        dxn, dg = _rms_bwd(dh_s[...], xh, r, gn_ref[...])
        dxo_ref[...] = do + dxn
        sm = jnp.concatenate([dscale, dbias, dg, jnp.zeros((5, D), F32)], axis=0)
        _accumulate(sm_ref, sm, i == 0)

    return _pcall(body, name="pool_bwd", grid=(nt,),
                  in_specs=[_rows(tm, D), _next_halo(tm, hp, D, T), _rows(tm, D), _rows(tm, D), _full(pw.shape), _full((1, D)),
                            _full((1, D)), _full((1, D))],
                  out_specs=[_rows(tm, D), pl.BlockSpec((ng, cg, cg), lambda i: (0, 0, 0)), _acc_spec(8, D)],
                  out_shape=[S_((T, D), F32), S_((ng, cg, cg), F32), S_((8, D), F32)],
                  scratch=[pltpu.VMEM((tm + hp, D), F32)] * 3 + [pltpu.VMEM((tm, D), F32)],
                  args=(dx3, dx3, plb, x2, pw, pb, ps, gn))


def _conv_bwd1(dx1, c, w2, lng, lnb, tm):
    T, D = dx1.shape

    def body(do_ref, c_ref, w2_ref, lng_ref, lnb_ref, dc_ref, sm_ref):
        i = pl.program_id(0)
        do = do_ref[...]
        ds = _dot_nt(do.astype(BF16), w2_ref[...])
        cv = c_ref[...]
        xc = cv - jnp.mean(cv, axis=-1, keepdims=True)
        rstd = lax.rsqrt(jnp.mean(xc * xc, axis=-1, keepdims=True) + LN_EPS)
        nh = xc * rstd
        n = nh * lng_ref[...] + lnb_ref[...]
        sg = _sig(n)
        dn = ds * (sg * (1.0 + n * (1.0 - sg)))
        dnh = dn * lng_ref[...]
        dc = rstd * (dnh - jnp.mean(dnh, axis=-1, keepdims=True) - nh * jnp.mean(dnh * nh, axis=-1, keepdims=True))
        dc_ref[...] = dc
        sm = jnp.concatenate([_colsum(dn * nh), _colsum(dn), _colsum(do), _colsum(dc), jnp.zeros((4, D), F32)], axis=0)
        _accumulate(sm_ref, sm, i == 0)

    return _pcall(body, name="conv_bwd1", grid=(T // tm,),
                  in_specs=[_rows(tm, D), _rows(tm, D), _full((D, D)), _full((1, D)), _full((1, D))],
                  out_specs=[_rows(tm, D), _acc_spec(8, D)],
                  out_shape=[S_((T, D), F32), S_((8, D), F32)], args=(dx1, c, w2, lng, lnb))


def _conv_bwd2(dc, a, x, w1g, wdw, gn, dx1, tm, tps, taps, comm=None):
    T, D = x.shape
    ns, _, nc = w1g.shape
    hc = HALO_CONV
    nt = T // tm
    tp = wdw.shape[0]

    def body(dc_ref, dch_ref, a_ref, ah_ref, x_ref, w_ref, wdw_ref, gn_ref, dx1_ref,
             gx_ref, dab_ref, dw_ref, db1_ref, dgn_ref, uext, dext, shifted, du_s, wacc):
        i = pl.program_id(0)
        ah = ah_ref[...]
        uext[0:hc, :] = jnp.where(i % tps == 0, 0.0, ah[:, :D] * _sig(ah[:, D:]))
        av = a_ref[...]
        sg2 = _sig(av[:, D:])
        uext[hc:hc + tm, :] = av[:, :D] * sg2
        dext[0:tm, :] = dc_ref[...]
        dext[tm:tm + hc, :] = jnp.where(i % tps == tps - 1, 0.0, dch_ref[...])

        @pl.when(i == 0)
        def _():
            wacc[...] = jnp.zeros_like(wacc)

        def wgrad(r0, src, ks, first):
            d = dc_ref[pl.ds(r0, CHUNK), :]
            for k, o in ks:
                wacc[8 * k:8 * k + 8, :] += _fold8(d * src[pl.ds(pl.multiple_of(r0 + o, 8), CHUNK), :])
        _by_sublane_shift(uext, shifted, tm, [hc - (taps - 1) + k for k in range(taps)], wgrad)

        def convt(r0, src, ks, first):
            acc = jnp.zeros((CHUNK, D), F32) if first else du_s[pl.ds(r0, CHUNK), :]
            for k, o in ks:
                acc = acc + wdw_ref[k:k + 1, :] * src[pl.ds(pl.multiple_of(r0 + o, 8), CHUNK), :]
            du_s[pl.ds(r0, CHUNK), :] = acc
        _by_sublane_shift(dext, shifted, tm, [taps - 1 - k for k in range(taps)], convt)

        du = du_s[...]
        da1 = du * sg2
        da2 = du * av[:, :D] * (sg2 * (1.0 - sg2))
        dab_ref[:, 0:D] = da1.astype(BF16)
        dab_ref[:, D:2 * D] = da2.astype(BF16)
        _accumulate(db1_ref, jnp.concatenate([_colsum(da1), _colsum(da2)], axis=0), i == 0)
        dh = jnp.zeros((tm, D), F32)
        for s in range(ns):
            dh = dh + _dot_nt(dab_ref[:, s * nc:(s + 1) * nc], w_ref[s])
        _, xh, r = _rms(x_ref[...], gn_ref[...])
        dxn, dg = _rms_bwd(dh, xh, r, gn_ref[...])
        gx_ref[...] = dx1_ref[...] + dxn
        _accumulate(dgn_ref, dg, i == 0)

        @pl.when(i == nt - 1)
        def _():
            dw_ref[...] = jnp.zeros_like(dw_ref)
            for k in range(taps):
                dw_ref[k:k + 1, :] = _colsum(wacc[8 * k:8 * k + 8, :])

    return _pcall(body, name="conv_bwd2", grid=(nt,),
                  in_specs=[_rows(tm, D), _next_halo(tm, hc, D, T), _rows(tm, 2 * D), _prev_halo(tm, hc, 2 * D), _rows(tm, D),
                            _full(w1g.shape), _full(wdw.shape), _full((1, D)), _rows(tm, D)],
                  out_specs=[_rows(tm, D), _rows(tm, 2 * D), _acc_spec(tp, D), _acc_spec(2, D), _acc_spec(1, D)],
                  out_shape=[S_((T, D), F32), S_((T, 2 * D), BF16), S_((tp, D), F32), S_((2, D), F32), S_((1, D), F32)],
                  scratch=[pltpu.VMEM((tm + hc, D), F32), pltpu.VMEM((tm + hc, D), F32), pltpu.VMEM((tm + hc - 8, D), F32),
                           pltpu.VMEM((tm, D), F32), pltpu.VMEM((8 * taps, D), F32)],
                  args=(dc, dc, a, a, x, w1g, wdw, gn, dx1), comm=comm)


def _place():
    x, y, c = lax.axis_index("x"), lax.axis_index("y"), lax.axis_index("c")
    chips = [(1 - x, y), (x, 1 - y), (1 - x, 1 - y)]
    return x, y, c, chips


def _own_slot(w, layer, ids, dtype, name):
    _, r, c = w.shape
    br = _row_block(r, c)

    def body(ids_ref, w_ref, o_ref):
        o_ref[...] = w_ref[...].astype(dtype)

    return _pcall(body, name=name, grid=(r // br,), prefetch=1,
                  in_specs=[pl.BlockSpec((None, br, c), lambda i, ids: (layer, i, 0))],
                  out_specs=pl.BlockSpec((None, br, c), lambda i, ids: (ids[0], i, 0)),
                  out_shape=S_((N_CHIPS, r, c), dtype), args=(ids, w))


class _GatherComm:
    def __init__(self, slots):
        n = len(slots)
        self.inputs = list(slots)
        self.out_shape = [S_(s.shape, s.dtype) for s in slots]
        self.aliases = {i: i for i in range(n)}
        self.sems = [pltpu.SemaphoreType.DMA((n, 3))] * 4
        self.halves = [s.shape[1] // 2 for s in slots]

    def _copies(self, outs, sems):
        s_ici, r_ici, s_d2d, r_d2d = sems
        x, y, c, chips = _place()
        peers = [(j, chip, 2 * chip[0] + chip[1]) for j, chip in enumerate(chips)]

        def half(i, h):
            return pl.ds(pl.multiple_of(h * self.halves[i], 8), self.halves[i])

        def ici(i, j, chip, to):
            blk = outs[i].at[chip, half(i, c)]
            return pltpu.make_async_remote_copy(src_ref=blk, dst_ref=blk, send_sem=s_ici.at[i, j], recv_sem=r_ici.at[i, j],
                                                device_id=to, device_id_type=MESH)

        def d2d(i, j, chip, h):
            blk = outs[i].at[chip, half(i, h)]
            return pltpu.make_async_remote_copy(src_ref=blk, dst_ref=blk, send_sem=s_d2d.at[i, j], recv_sem=r_d2d.at[i, j],
                                                device_id=(x, y, 1 - c), device_id_type=MESH)
        return (x, y, c, 2 * x + y), peers, ici, d2d

    def before(self, cin, outs, sems, step, total):
        (x, y, c, q), peers, ici, d2d = self._copies(outs, sems)

        @pl.when(step == 0)
        def _():
            for i in range(len(outs)):
                for j, chip, _ in peers:
                    ici(i, j, q, (*chip, c)).start()

    def after(self, cin, outs, sems, step, total):
        (x, y, c, q), peers, ici, d2d = self._copies(outs, sems)

        @pl.when(step == max(total - 2, 0))
        def _():
            for i in range(len(outs)):
                for j, chip, qj in peers:
                    ici(i, j, qj, (x, y, c)).wait_recv()
                    d2d(i, j, qj, c).start()

        @pl.when(step == total - 1)
        def _():
            for i in range(len(outs)):
                for j, chip, qj in peers:
                    d2d(i, j, qj, 1 - c).wait_recv()
            for i in range(len(outs)):
                for j, chip, qj in peers:
                    ici(i, j, q, (*chip, c)).wait_send()
                    d2d(i, j, qj, c).wait_send()


class _ExchangeComm:
    def __init__(self, sums):
        n = len(sums)
        self.inputs = list(sums)
        self.out_shape = [S_((3,) + s.shape[1:], s.dtype) for s in sums]
        self.aliases = {}
        self.sems = [pltpu.SemaphoreType.DMA((n, 3))] * 2

    def _copies(self, srcs, outs, sems):
        s_sem, r_sem = sems
        x, y, c, chips = _place()
        return [pltpu.make_async_remote_copy(src_ref=srcs[i].at[2 * chip[0] + chip[1]], dst_ref=outs[i].at[j],
                                             send_sem=s_sem.at[i, j], recv_sem=r_sem.at[i, j],
                                             device_id=(*chip, c), device_id_type=MESH)
                for i in range(len(srcs)) for j, chip in enumerate(chips)]

    def before(self, srcs, outs, sems, step, total):
        @pl.when(step == 0)
        def _():
            for cp in self._copies(srcs, outs, sems):
                cp.start()

    def after(self, srcs, outs, sems, step, total):
        @pl.when(step == total - 1)
        def _():
            for cp in self._copies(srcs, outs, sems):
                cp.wait()


def _comm_only(comm, name):
    def body():
        pass
    outs = _pcall(body, name=name, grid=(1,), in_specs=[], out_specs=[], out_shape=[], args=[], comm=comm)
    return list(outs)


def _pair_exchange(parts, name):
    n = len(parts)

    def body(*refs):
        srcs, outs = refs[:n], refs[n:2 * n]
        s_sem, r_sem = refs[2 * n:]
        x, y, c, _ = _place()
        cps = []
        for i in range(n):
            for s in range(N_CHIPS):
                cp = pltpu.make_async_remote_copy(src_ref=srcs[i].at[s, 1 - c], dst_ref=outs[i].at[s], send_sem=s_sem.at[i, s],
                                                  recv_sem=r_sem.at[i, s], device_id=(x, y, 1 - c), device_id_type=MESH)
                cp.start()
                cps.append(cp)
        for cp in cps:
            cp.wait()

    any_spec = pl.BlockSpec(memory_space=pl.ANY)
    outs = pl.pallas_call(
        body, name=name, in_specs=[any_spec] * n, out_specs=[any_spec] * n,
        out_shape=[S_((N_CHIPS,) + p.shape[2:], p.dtype) for p in parts],
        scratch_shapes=[pltpu.SemaphoreType.DMA((n, N_CHIPS))] * 2,
    )(*parts)
    return list(outs)


class _ShareComm:
    def __init__(self, bufs):
        n = len(bufs)
        self.inputs = list(bufs)
        self.out_shape = [S_(b.shape, b.dtype) for b in bufs]
        self.aliases = {i: i for i in range(n)}
        self.sems = [pltpu.SemaphoreType.DMA((n, max(b.shape[0] for b in bufs)))] * 2
        self.layers = [b.shape[0] for b in bufs]

    def _copies(self, outs, sems):
        s_sem, r_sem = sems
        x, y, c, _ = _place()
        cps = []
        for i, nl in enumerate(self.layers):
            for l in range(nl):
                blk = outs[i].at[l, c]
                cps.append(pltpu.make_async_remote_copy(src_ref=blk, dst_ref=blk, send_sem=s_sem.at[i, l],
                                                        recv_sem=r_sem.at[i, l], device_id=(x, y, 1 - c),
                                                        device_id_type=MESH))
        return cps

    def before(self, cin, outs, sems, step, total):
        @pl.when(step == 0)
        def _():
            for cp in self._copies(outs, sems):
                cp.start()

    def after(self, cin, outs, sems, step, total):
        @pl.when(step == total - 1)
        def _():
            for cp in self._copies(outs, sems):
                cp.wait()


class _Both:
    def __init__(self, a, b):
        self.parts = (a, b)
        self.inputs = a.inputs + b.inputs
        self.out_shape = a.out_shape + b.out_shape
        self.aliases = dict(a.aliases)
        self.aliases.update({len(a.inputs) + i: len(a.out_shape) + o for i, o in b.aliases.items()})
        self.sems = a.sems + b.sems

    def _each(self, phase, cin, cout, sems, step, total):
        i = o = s = 0
        for p in self.parts:
            ni, no, ns = len(p.inputs), len(p.out_shape), len(p.sems)
            getattr(p, phase)(cin[i:i + ni], cout[o:o + no], sems[s:s + ns], step, total)
            i, o, s = i + ni, o + no, s + ns

    def before(self, cin, cout, sems, step, total):
        self._each("before", cin, cout, sems, step, total)

    def after(self, cin, cout, sems, step, total):
        self._each("after", cin, cout, sems, step, total)


def _small_allreduce(groups):
    n = len(groups)
    arrays = []
    for grp in groups:
        for p in grp:
            if not isinstance(p, int) and not any(p[0] is a for a in arrays):
                arrays.append(p[0])
    shapes = []
    for grp in groups:
        rows = sum(p if isinstance(p, int) else p[2] - p[1] for p in grp)
        width = next(p[0].shape[1] for p in grp if not isinstance(p, int))
        assert rows % 8 == 0
        shapes.append((rows, width))
    na = len(arrays)

    def body(*refs):
        a_refs, o_refs = refs[:na], refs[na:na + n]
        rest = refs[na + n:]
        v_refs, sibs, css, gots = rest[:n], rest[n:2 * n], rest[2 * n:3 * n], rest[3 * n:4 * n]
        s_sem, r_sem = rest[4 * n:]
        for i, grp in enumerate(groups):
            o = 0
            for p in grp:
                if isinstance(p, int):
                    v_refs[i][o:o + p, :] = jnp.zeros((p, shapes[i][1]), F32)
                    o += p
                else:
                    k = next(t for t, a in enumerate(arrays) if a is p[0])
                    v_refs[i][o:o + p[2] - p[1], :] = a_refs[k][p[1]:p[2], :]
                    o += p[2] - p[1]
        x, y, c, chips = _place()
        q = 2 * x + y
        to_sib = [pltpu.make_async_remote_copy(src_ref=v_refs[i], dst_ref=sibs[i], send_sem=s_sem.at[i, 3], recv_sem=r_sem.at[i, 3],
                                               device_id=(x, y, 1 - c), device_id_type=MESH) for i in range(n)]
        for cp in to_sib:
            cp.start()
        cps = []
        for i in range(n):
            to_sib[i].wait()
            mine, other = v_refs[i][...], sibs[i][...]
            css[i][...] = jnp.where(c == 0, mine, other) + jnp.where(c == 0, other, mine)
            for j, chip in enumerate(chips):
                cp = pltpu.make_async_remote_copy(src_ref=css[i], dst_ref=gots[i].at[j], send_sem=s_sem.at[i, j],
                                                  recv_sem=r_sem.at[i, j], device_id=(*chip, c), device_id_type=MESH)
                cp.start()
                cps.append(cp)
        for cp in cps:
            cp.wait()
        flips = [2, 1, 3]
        for i in range(n):
            total = None
            for k in range(N_CHIPS):
                d = q ^ k
                term = jnp.where(d == 0, css[i][...], 0.0)
                for j in range(3):
                    term = jnp.where(d == flips[j], gots[i][j], term)
                total = term if total is None else total + term
            o_refs[i][...] = total

    vm = pl.BlockSpec(memory_space=pltpu.VMEM)
    return pl.pallas_call(
        body, name="small_allreduce", in_specs=[vm] * na, out_specs=[vm] * n, out_shape=[S_(s, F32) for s in shapes],
        scratch_shapes=[pltpu.VMEM(s, F32) for s in shapes] * 3 + [pltpu.VMEM((3,) + s, F32) for s in shapes]
        + [pltpu.SemaphoreType.DMA((n, 4)), pltpu.SemaphoreType.DMA((n, 4))],
    )(*arrays)


def _row_block(r, c, target_bytes=1 << 20):
    br = r
    while br % 32 == 0 and br * c * 4 > target_bytes:
        br //= 2
    return br


def _chip_sum(part, sib, ids, name):
    _, _, r, c = part.shape
    br = _row_block(r, c)

    def body(ids_ref, p_ref, s_ref, o_ref):
        o_ref[...] = (p_ref[...].astype(F32) + s_ref[...].astype(F32)).astype(BF16)

    return _pcall(body, name=name, grid=(N_CHIPS, r // br), prefetch=1,
                  in_specs=[pl.BlockSpec((None, None, br, c), lambda s, i, ids: (s, ids[1], i, 0)),
                            pl.BlockSpec((None, br, c), lambda s, i, ids: (s, i, 0))],
                  out_specs=pl.BlockSpec((None, br, c), lambda s, i, ids: (s, i, 0)),
                  out_shape=S_((N_CHIPS, r, c), BF16), args=(ids, part, sib))


def _final_sum(part, sib, got, ids, name, layer=0, n_layers=1, buf=None):
    _, r, c = got.shape
    br = _row_block(r, c)

    def body(ids_ref, p_ref, s_ref, got_ref, *rest):
        v = p_ref[...].astype(F32) + s_ref[...].astype(F32)
        for j in range(3):
            v = v + got_ref[j].astype(F32)
        rest[-1][...] = v

    in_specs = [pl.BlockSpec((None, None, br, c), lambda i, ids: (ids[0], ids[1], i, 0)),
                pl.BlockSpec((None, br, c), lambda i, ids: (ids[0], i, 0)),
                pl.BlockSpec((3, br, c), lambda i, ids: (0, i, 0))]
    args = [ids, part, sib, got]
    if buf is not None:
        in_specs.append(pl.BlockSpec(memory_space=pl.ANY))
        args.append(buf)
    return _pcall(body, name=name, grid=(r // br,), prefetch=1, in_specs=in_specs,
                  out_specs=pl.BlockSpec((None, None, br, c), lambda i, ids: (layer, ids[1], i, 0)),
                  out_shape=S_((n_layers, 2, r, c), F32), args=args, aliases={4: 0} if buf is not None else None)


def _adam_math(w, g, m, v):
    c1 = 1.0 / (1.0 - ADAM_B1 ** ADAM_STEP)
    c2 = 1.0 / (1.0 - ADAM_B2 ** ADAM_STEP)
    mn = ADAM_B1 * m + (1.0 - ADAM_B1) * g
    vn = ADAM_B2 * v + (1.0 - ADAM_B2) * (g * g)
    return -ADAM_LR * ((mn * c1) / (jnp.sqrt(vn * c2) + ADAM_EPS) + ADAM_WD * w), mn, vn


def _adamw(w, g, m, v, name):
    nl, r, c = w.shape
    br = _row_block(r, c, 1 << 19)

    def body(w_ref, g_ref, m_ref, v_ref, d_ref, mo_ref, vo_ref):
        d_ref[...], mo_ref[...], vo_ref[...] = _adam_math(w_ref[...], g_ref[...], m_ref[...], v_ref[...])

    blk = pl.BlockSpec((None, br, c), lambda l, i: (l, i, 0))
    return _pcall(body, name=name, grid=(nl, r // br), in_specs=[blk] * 4, out_specs=[blk] * 3,
                  out_shape=[S_((nl, r, c), F32)] * 3, args=(w, g, m, v))


def _adamw_small(rd, rf, sharded_g, ws, ms, vs, rows_d, rows_f, loss_row):
    n = len(ws)
    ns = len(sharded_g)
    rep = [k for k in range(n) if rows_d[k] is not None or rows_f[k] is not None]

    def body(*refs):
        rd_ref, rf_ref = refs[0], refs[1]
        sg = refs[2:2 + ns]
        w_refs, m_refs, v_refs = (refs[2 + ns + t * n:2 + ns + (t + 1) * n] for t in range(3))
        loss_ref, outs = refs[2 + ns + 3 * n], refs[3 + ns + 3 * n:]
        g_out, d_out, m_out, v_out = outs[:len(rep)], outs[len(rep):len(rep) + n], outs[len(rep) + n:len(rep) + 2 * n], \
            outs[len(rep) + 2 * n:]
        loss_ref[...] = jnp.sum(rd_ref[loss_row:loss_row + 1, :], axis=1, keepdims=True)
        si = 0
        for k in range(n):
            shape = w_refs[k].shape
            if rows_d[k] is not None or rows_f[k] is not None:
                src, rws = (rd_ref, rows_d[k]) if rows_d[k] is not None else (rf_ref, rows_f[k])
                axis = 0 if shape[0] == len(rws) else 1
                g = src[rws[0]:rws[0] + 1, :]
                if len(rws) > 1:
                    g = jnp.concatenate([src[r:r + 1, :] for r in rws], axis=axis)
                g_out[rep.index(k)][...] = g
            else:
                g = sg[si][...]
                si += 1
            d_out[k][...], m_out[k][...], v_out[k][...] = _adam_math(w_refs[k][...], g, m_refs[k][...], v_refs[k][...])

    vm = pl.BlockSpec(memory_space=pltpu.VMEM)
    shapes = [S_(w.shape, F32) for w in ws]
    outs = pl.pallas_call(
        body, name="adamw_small", in_specs=[vm] * (2 + ns + 3 * n), out_specs=[vm] * (1 + len(rep) + 3 * n),
        out_shape=[S_((1, 1), F32)] + [shapes[k] for k in rep] + shapes * 3,
    )(rd, rf, *sharded_g, *ws, *ms, *vs)
    g_rep = dict(zip(rep, outs[1:1 + len(rep)]))
    o = outs[1 + len(rep):]
    return outs[0], g_rep, o[:n], o[n:2 * n], o[2 * n:]


def _pack(pieces, rows):
    flat = jnp.concatenate([p.reshape(-1).astype(F32) for p in pieces])
    return jnp.pad(flat, (0, rows * 128 - flat.shape[0])).reshape(rows, 128)


def _unpack(packed, shapes):
    flat = packed.reshape(-1)
    out, o = [], 0
    for s in shapes:
        n = 1
        for d in s:
            n *= d
        out.append(flat[o:o + n].reshape(s))
        o += n
    return out


def _rows_for(pieces_or_shapes):
    n = 0
    for p in pieces_or_shapes:
        k = 1
        for d in (p if isinstance(p, tuple) else p.shape):
            k *= d
        n += k
    return -(-n // 1024) * 8


def kernel(x, norm_mix, norm_ffn, conv_w_pw1, conv_b_pw1, conv_w_dw, conv_b_dw, conv_ln_g, conv_ln_b, conv_w_pw2, conv_b_pw2, pool_w, pool_b, pool_scale, ffn_w_up, ffn_w_dw, ffn_b_dw, ffn_w_down, final_norm, loss_target, m_norm_mix, m_norm_ffn, m_conv_w_pw1, m_conv_b_pw1, m_conv_w_dw, m_conv_b_dw, m_conv_ln_g, m_conv_ln_b, m_conv_w_pw2, m_conv_b_pw2, m_pool_w, m_pool_b, m_pool_scale, m_ffn_w_up, m_ffn_w_dw, m_ffn_b_dw, m_ffn_w_down, m_final_norm, v_norm_mix, v_norm_ffn, v_conv_w_pw1, v_conv_b_pw1, v_conv_w_dw, v_conv_b_dw, v_conv_ln_g, v_conv_ln_b, v_conv_w_pw2, v_conv_b_pw2, v_pool_w, v_pool_b, v_pool_scale, v_ffn_w_up, v_ffn_w_dw, v_ffn_b_dw, v_ffn_w_down, v_final_norm):
    nb, seq, D = x.shape
    T = nb * seq
    F = ffn_w_down.shape[1] * N_CHIPS
    taps = conv_w_dw.shape[1]
    ffn_taps = ffn_w_dw.shape[1]
    ng = pool_w.shape[1]
    cg = pool_w.shape[3]
    dsh = D // N_CHIPS
    fsh = F // N_CHIPS
    assert taps - 1 <= HALO_CONV and ffn_taps - 1 <= HALO_FFN and max(POOL_WINDOWS) <= HALO_POOL
    tm = min(TILE_D, seq)
    tf = min(TILE_F, seq)
    assert seq % tm == 0 and seq % tf == 0 and tm % HALO_CONV == 0 and tf % CHUNK == 0
    tps, tpf = seq // tm, seq // tf
    tw = min(TILE_MM, T)
    assert T % tw == 0

    xi, yi, ci = lax.axis_index("x"), lax.axis_index("y"), lax.axis_index("c")
    qi = 2 * xi + yi
    ids = jnp.stack([qi, ci]).astype(jnp.int32)
    x2d = x.reshape(T, D)
    tg2d = loss_target.reshape(T, D)

    small_sharded = [conv_w_dw[0], ffn_w_dw, pool_b, pool_scale]
    srows = 2 * _rows_for(small_sharded)
    s_pw1, s_pw2 = _own_slot(conv_w_pw1, 0, ids, BF16, "slot_pw1"), _own_slot(conv_w_pw2, 0, ids, BF16, "slot_pw2")
    s_small = _own_slot(_pack(small_sharded, srows)[None], 0, ids, F32, "slot_small")
    s_pool = _own_slot(pool_w.reshape(1, ng * (cg // N_CHIPS), cg), 0, ids, BF16, "slot_pool")
    s_up = [_own_slot(ffn_w_up, l, ids, BF16, "slot_up%d" % l) for l in range(2)]
    s_down = [_own_slot(ffn_w_down, l, ids, BF16, "slot_down%d" % l) for l in range(2)]
    (w1g,) = _comm_only(_GatherComm([s_pw1]), "gather_pw1")
    hb0t, a, w2g, smallg = _conv_in(x2d, norm_mix[0:1], w1g, conv_b_pw1, tm, comm=_GatherComm([s_pw2, s_small]))
    w2f = w2g.reshape(D, D)
    sm_parts = [_unpack(smallg[s], [(taps, dsh), (2, ffn_taps, fsh), (1, dsh), (1, dsh)]) for s in range(N_CHIPS)]
    wdw_f = jnp.concatenate([p[0] for p in sm_parts], axis=1)
    wdw_f = jnp.pad(wdw_f, ((0, HALO_CONV - taps), (0, 0)))
    fdw_f = jnp.concatenate([p[1] for p in sm_parts], axis=2)
    fdw_f = jnp.pad(fdw_f, ((0, 0), (0, 8 - ffn_taps), (0, 0)))
    pb_f = jnp.concatenate([p[2] for p in sm_parts], axis=1)
    ps_f = jnp.concatenate([p[3] for p in sm_parts], axis=1)

    c, sbt, x1, hb1, hb1t, wup0, wd0, pwg = _conv_mid(a, x2d, wdw_f, conv_b_dw, conv_ln_g, conv_ln_b, w2f, conv_b_pw2,
                                                      norm_ffn[0:1], tm, tps, taps,
                                                      comm=_GatherComm([s_up[0], s_down[0], s_pool]))
    up0, wup1 = _mm_up(hb1, wup0, tw, "ffn_up0", comm=_GatherComm([s_up[1]]))
    gb0t, x2, h2, wd1 = _ffn_down(up0, x1, fdw_f[0], ffn_b_dw[0:1], wd0.reshape(F, D), norm_mix[1:2], tf, tpf, ffn_taps,
                                  "ffn_down0", comm=_GatherComm([s_down[1]]))
    wup = [wup0, wup1]
    wdn = [wd0.reshape(F, D), wd1.reshape(F, D)]
    pwf = pwg.reshape(N_CHIPS, ng, cg // N_CHIPS, cg).transpose(1, 0, 2, 3).reshape(ng, cg, cg)
    plb, x3, hb3, hb3t = _pool_fwd(h2, x2, pwf, pb_f, ps_f, norm_ffn[1:2], tm, tps)
    up1 = _mm_up(hb3, wup[1], tw, "ffn_up1")
    gb1t, dx4, loss_part, d_final = _ffn_down(up1, x3, fdw_f[1], ffn_b_dw[1:2], wdn[1], final_norm.reshape(1, D), tf, tpf,
                                              ffn_taps, "ffn_down1", target=tg2d)

    fs = {}

    def chip_sums(group, tag):
        nms = list(group)
        from_sib = _pair_exchange([group[nm] for nm in nms], "grad_pair_exchange_" + tag)
        sums = [_chip_sum(group[nm], s, ids, "chip_sum_" + nm) for nm, s in zip(nms, from_sib)]
        for nm, s in zip(nms, from_sib):
            fs[nm] = (group[nm], s)
        return nms, _ExchangeComm(sums)

    def arrived(nms, got):
        for nm, g in zip(nms, got):
            fs[nm] = fs[nm] + (g,)

    dac1, dup1, fw1 = _bwd_down(dx4, up1, fdw_f[1], ffn_b_dw[1:2], wdn[1], tf, tpf, ffn_taps, "bwd_down1")
    p_down1 = _mm_tn(gb1t, dx4, F // 2, D, tw, False, "dw_down1")
    nms, xchg = chip_sums({"down1": p_down1.reshape(N_CHIPS, 2, fsh // 2, D)}, "down1")
    dup1, dx3, dnf1, *got = _bwd_up(dac1, dup1, wup[1], fdw_f[1], x3, norm_ffn[1:2], dx4, tf, tpf, ffn_taps,
                                         "bwd_up1", comm=xchg)
    arrived(nms, got)
    p_up1 = _mm_tn(hb3t, dup1, D, 2 * fsh, tw, True, "dw_up1")
    nms, xchg = chip_sums({"up1": p_up1.reshape(N_CHIPS, 2, D // 2, 2 * fsh)}, "up1")
    dx2, dpw, pool_sm = _pool_bwd(dx3, plb, x2, pwf, pb_f, ps_f, norm_mix[1:2], tm, tps)
    dac0, dup0, fw0, *got = _bwd_down(dx2, up0, fdw_f[0], ffn_b_dw[0:1], wdn[0], tf, tpf, ffn_taps, "bwd_down0", comm=xchg)
    arrived(nms, got)
    p_down0 = _mm_tn(gb0t, dx2, F // 2, D, tw, False, "dw_down0")
    nms, xchg = chip_sums({"down0": p_down0.reshape(N_CHIPS, 2, fsh // 2, D)}, "down0")
    dup0, dx1, dnf0, *got = _bwd_up(dac0, dup0, wup[0], fdw_f[0], x1, norm_ffn[0:1], dx2, tf, tpf, ffn_taps,
                                         "bwd_up0", comm=xchg)
    arrived(nms, got)
    p_up0 = _mm_tn(hb1t, dup0, D, 2 * fsh, tw, True, "dw_up0")
    dc, conv_sm = _conv_bwd1(dx1, c, w2f, conv_ln_g, conv_ln_b, tm)
    p_pw2 = _mm_tn(sbt, dx1, D, D, tw, False, "dw_pw2")
    csh = cg // N_CHIPS
    p_pool = dpw.reshape(2, ng // 2, N_CHIPS, csh, cg).transpose(2, 0, 1, 3, 4).reshape(N_CHIPS, 2, (ng // 2) * csh, cg)
    nms, xchg = chip_sums({"up0": p_up0.reshape(N_CHIPS, 2, D // 2, 2 * fsh), "pool": p_pool.astype(BF16),
                           "pw2": p_pw2.reshape(N_CHIPS, 2, dsh // 2, D)}, "mix")
    grad_x, dab, dwdw, db1, dnm0, *got = _conv_bwd2(dc, a, x2d, w1g, wdw_f, norm_mix[0:1], dx1, tm, tps, taps, comm=xchg)
    arrived(nms, got)
    p_pw1 = _mm_tn(hb0t, dab, D, 2 * dsh, tw, True, "dw_pw1")
    nms, xchg = chip_sums({"pw1": p_pw1.reshape(N_CHIPS, 2, D // 2, 2 * dsh)}, "pw1")
    bufs = [_final_sum(*fs[nm], ids, "final_sum_" + nm) for nm in ("pw2", "pool")]
    for nm in ("up", "down"):
        b = _final_sum(*fs[nm + "0"], ids, "final_sum_" + nm + "0", layer=0, n_layers=2)
        bufs.append(_final_sum(*fs[nm + "1"], ids, "final_sum_" + nm + "1", layer=1, n_layers=2, buf=b))
    *shared, got_pw1 = _comm_only(_Both(_ShareComm(bufs), xchg), "grad_share_halves")
    arrived(nms, [got_pw1])
    shared += _comm_only(_ShareComm([_final_sum(*fs["pw1"], ids, "final_sum_pw1")]), "grad_share_halves_pw1")
    g_pw2, g_pool, g_up, g_down, g_pw1 = [b.reshape(b.shape[0], 2 * b.shape[2], b.shape[3]) for b in shared]

    rd, rf = _small_allreduce([
        [(dnm0, 0, 1), (pool_sm, 2, 3), (dnf0, 0, 1), (dnf1, 0, 1), (db1, 0, 2), (conv_sm, 0, 4), (d_final, 0, 1),
         (pool_sm, 0, 2), (loss_part, 0, 1), 2, (dwdw, 0, dwdw.shape[0])],
        [(fw0, 0, 4), (fw1, 0, 4)]])
    g_wdw = lax.dynamic_slice_in_dim(rd[16:16 + taps], qi * dsh, dsh, axis=1)
    g_ps = lax.dynamic_slice_in_dim(rd[11:12], qi * dsh, dsh, axis=1)
    g_pb = lax.dynamic_slice_in_dim(rd[12:13], qi * dsh, dsh, axis=1)
    g_fdw = lax.dynamic_slice_in_dim(rf.reshape(2, 4, F)[:, :ffn_taps], qi * fsh, fsh, axis=2).reshape(2 * ffn_taps, fsh)

    def big(w, g, m, v, nm):
        d, mn, vn = _adamw(w.reshape(g.shape), g, m.reshape(g.shape), v.reshape(g.shape), "adamw_" + nm)
        return g.reshape(w.shape), d.reshape(w.shape), mn.reshape(w.shape), vn.reshape(w.shape)

    o_pw1 = big(conv_w_pw1, g_pw1, m_conv_w_pw1, v_conv_w_pw1, "pw1")
    o_pw2 = big(conv_w_pw2, g_pw2, m_conv_w_pw2, v_conv_w_pw2, "pw2")
    o_pool = big(pool_w, g_pool, m_pool_w, v_pool_w, "pool")
    o_up = big(ffn_w_up, g_up, m_ffn_w_up, v_ffn_w_up, "up")
    o_down = big(ffn_w_down, g_down, m_ffn_w_down, v_ffn_w_down, "down")

    snames = ["norm_mix", "norm_ffn", "b_pw1", "w_dw", "b_dw", "ln_g", "ln_b", "b_pw2", "pool_b", "pool_scale", "ffn_w_dw",
              "ffn_b_dw", "final_norm"]
    sw = [norm_mix, norm_ffn, conv_b_pw1, conv_w_dw, conv_b_dw, conv_ln_g, conv_ln_b, conv_b_pw2, pool_b, pool_scale,
          ffn_w_dw, ffn_b_dw, final_norm]
    smm = [m_norm_mix, m_norm_ffn, m_conv_b_pw1, m_conv_w_dw, m_conv_b_dw, m_conv_ln_g, m_conv_ln_b, m_conv_b_pw2, m_pool_b,
           m_pool_scale, m_ffn_w_dw, m_ffn_b_dw, m_final_norm]
    svv = [v_norm_mix, v_norm_ffn, v_conv_b_pw1, v_conv_w_dw, v_conv_b_dw, v_conv_ln_g, v_conv_ln_b, v_conv_b_pw2, v_pool_b,
           v_pool_scale, v_ffn_w_dw, v_ffn_b_dw, v_final_norm]
    sshapes = [tuple(w.shape) for w in sw]
    two_d = [(2, D), (2, D), (1, 2 * D), (taps, dsh), (1, D), (1, D), (1, D), (1, D), (1, dsh), (1, dsh), (2 * ffn_taps, fsh),
             (2, F), (1, D)]
    rows_d = [[0, 1], [2, 3], [4, 5], None, [9], [6], [7], [8], None, None, None, None, [10]]
    rows_f = [None] * 11 + [[3, 7], None]
    as2d = lambda ts: [t.reshape(s) for t, s in zip(ts, two_d)]
    loss, g_rep, sd, sm_new, sv_new = _adamw_small(rd, rf, [g_wdw, g_pb, g_ps, g_fdw], as2d(sw), as2d(smm), as2d(svv), rows_d,
                                                   rows_f, loss_row=13)
    sg = dict(g_rep)
    sg.update({3: g_wdw, 8: g_pb, 9: g_ps, 10: g_fdw})
    small_out = {n: tuple(t.reshape(sshapes[k]) for t in (sg[k], sd[k], sm_new[k], sv_new[k])) for k, n in enumerate(snames)}

    order = [small_out["norm_mix"], small_out["norm_ffn"], o_pw1, small_out["b_pw1"], small_out["w_dw"], small_out["b_dw"],
             small_out["ln_g"], small_out["ln_b"], o_pw2, small_out["b_pw2"], o_pool, small_out["pool_b"],
             small_out["pool_scale"], o_up, small_out["ffn_w_dw"], small_out["ffn_b_dw"], o_down, small_out["final_norm"]]
    return (loss[0, 0], grad_x.reshape(nb, seq, D), *[o[0] for o in order], *[o[1] for o in order], *[o[2] for o in order],
            *[o[3] for o in order])
```

```python
import functools

import jax
import jax.numpy as jnp
from jax import lax
from jax.experimental import pallas as pl
from jax.experimental.pallas import tpu as pltpu

F32, BF16 = jnp.float32, jnp.bfloat16
S_ = jax.ShapeDtypeStruct
MESH = pl.DeviceIdType.MESH

RMS_EPS, LN_EPS = 1e-6, 1e-5
POOL_WINDOWS = (2, 4, 8, 16)
ADAM_LR, ADAM_B1, ADAM_B2, ADAM_EPS, ADAM_WD, ADAM_STEP = 0.001, 0.9, 0.999, 1e-08, 0.01, 10

VMEM_LIMIT_BYTES = 60 * 1024 * 1024
N_CHIPS = 4
CHUNK = 16
HALO_CONV = 32
HALO_POOL = 32
HALO_FFN = 8
TILE_D = 512
TILE_F = 256
TILE_MM = 1024


def _pcall(body, *, name, grid, in_specs, out_specs, out_shape, args, scratch=(), aliases=None, prefetch=0, comm=None):
    params = pltpu.CompilerParams(dimension_semantics=("arbitrary",) * len(grid), vmem_limit_bytes=VMEM_LIMIT_BYTES)
    if comm is not None:
        assert not prefetch
        single = not isinstance(out_shape, (list, tuple))
        in_specs, args, scratch = list(in_specs), list(args), list(scratch)
        out_specs, out_shape = ([out_specs], [out_shape]) if single else (list(out_specs), list(out_shape))
        n_in, n_out, n_scr, n_cin, n_cout = len(in_specs), len(out_specs), len(scratch), len(comm.inputs), len(comm.out_shape)
        any_spec = pl.BlockSpec(memory_space=pl.ANY)
        aliases = dict(aliases or {})
        aliases.update({n_in + a: n_out + b for a, b in comm.aliases.items()})
        inner = body

        def body(*refs):
            ins, cin = refs[:n_in], refs[n_in:n_in + n_cin]
            outs = refs[n_in + n_cin:n_in + n_cin + n_out]
            cout = refs[n_in + n_cin + n_out:n_in + n_cin + n_out + n_cout]
            scr = refs[n_in + n_cin + n_out + n_cout:n_in + n_cin + n_out + n_cout + n_scr]
            sems = refs[n_in + n_cin + n_out + n_cout + n_scr:]
            step, total = 0, 1
            for d, g in enumerate(grid):
                step = step * g + pl.program_id(d)
                total *= g
            comm.before(cin, cout, sems, step, total)
            inner(*ins, *outs, *scr)
            comm.after(cin, cout, sems, step, total)

        in_specs += [any_spec] * n_cin
        args += list(comm.inputs)
        out_specs += [any_spec] * n_cout
        out_shape += list(comm.out_shape)
        scratch += list(comm.sems)
    if prefetch:
        spec = pltpu.PrefetchScalarGridSpec(num_scalar_prefetch=prefetch, grid=grid, in_specs=in_specs,
                                            out_specs=out_specs, scratch_shapes=list(scratch))
        return pl.pallas_call(body, name=name, grid_spec=spec, out_shape=out_shape,
                              input_output_aliases=aliases or {}, compiler_params=params)(*args)
    return pl.pallas_call(body, name=name, grid=grid, in_specs=in_specs, out_specs=out_specs, out_shape=out_shape,
                          scratch_shapes=list(scratch), input_output_aliases=aliases or {},
                          compiler_params=params)(*args)


def _full(shape):
    n = len(shape)
    return pl.BlockSpec(tuple(shape), lambda *_: (0,) * n, pipeline_mode=pl.Buffered(1))


def _rows(tm, c, col=0):
    return pl.BlockSpec((tm, c), lambda i, *_: (i, col))


def _cols(r, tm):
    return pl.BlockSpec((r, tm), lambda i, *_: (0, i))


def _prev_halo(tm, hb, c):
    return pl.BlockSpec((hb, c), lambda i, *_: (jnp.maximum(i * (tm // hb) - 1, 0), 0))


def _next_halo(tm, hb, c, total_rows):
    last = total_rows // hb - 1
    return pl.BlockSpec((hb, c), lambda i, *_: (jnp.minimum((i + 1) * (tm // hb), last), 0))


def _acc_spec(r, c):
    return pl.BlockSpec((r, c), lambda *_: (0, 0))


def _rms(x, g):
    r = lax.rsqrt(jnp.mean(x * x, axis=-1, keepdims=True) + RMS_EPS)
    xh = x * r
    return xh * g, xh, r


def _rms_bwd(dy, xh, r, g):
    dxh = dy * g
    dx = r * (dxh - xh * jnp.mean(dxh * xh, axis=-1, keepdims=True))
    return dx, jnp.sum(dy * xh, axis=0, keepdims=True)


def _colsum(v):
    return jnp.sum(v, axis=0, keepdims=True)


def _fold8(v):
    out = v[0:8]
    for j in range(1, v.shape[0] // 8):
        out = out + v[8 * j:8 * j + 8]
    return out


def _chunks(n_rows, fn):
    def step(j, carry):
        fn(pl.multiple_of(j * CHUNK, CHUNK))
        return carry
    lax.fori_loop(0, n_rows // CHUNK, step, 0)


def _dot(a, b):
    return jnp.dot(a, b, preferred_element_type=F32)


def _dot_nt(a, b):
    return lax.dot_general(a, b, (((1,), (1,)), ((), ())), preferred_element_type=F32)


def _dot_tn(a, b):
    return lax.dot_general(a, b, (((0,), (0,)), ((), ())), preferred_element_type=F32)


def _sig(v):
    return jax.nn.sigmoid(v)


def _accumulate(ref, val, first):
    @pl.when(first)
    def _():
        ref[...] = val

    @pl.when(jnp.logical_not(first))
    def _():
        ref[...] = ref[...] + val


def _conv_in(x, g, w1g, b1, tm, comm=None):
    T, D = x.shape
    ns, _, nc = w1g.shape

    def body(x_ref, g_ref, w_ref, b_ref, hbt_ref, a_ref):
        hb = _rms(x_ref[...], g_ref[...])[0].astype(BF16)
        hbt_ref[...] = hb.T
        for s in range(ns):
            a_ref[:, s * nc:(s + 1) * nc] = _dot(hb, w_ref[s]) + b_ref[:, s * nc:(s + 1) * nc]

    return _pcall(body, name="conv_in", grid=(T // tm,),
                  in_specs=[_rows(tm, D), _full((1, D)), _full(w1g.shape), _full((1, ns * nc))],
                  out_specs=[_cols(D, tm), _rows(tm, ns * nc)],
                  out_shape=[S_((D, T), BF16), S_((T, ns * nc), F32)], args=(x, g, w1g, b1), comm=comm)


def _conv_mid(a, x, wdw, bdw, lng, lnb, w2, b2, gn, tm, tps, taps, comm=None):
    T, D = x.shape
    hc = HALO_CONV

    def body(a_ref, ah_ref, x_ref, wdw_ref, bdw_ref, lng_ref, lnb_ref, w2_ref, b2_ref, gn_ref,
             c_ref, sbt_ref, x1_ref, hb_ref, hbt_ref, uext, shifted):
        i = pl.program_id(0)
        ah = ah_ref[...]
        uext[0:hc, :] = jnp.where(i % tps == 0, 0.0, ah[:, :D] * _sig(ah[:, D:]))
        av = a_ref[...]
        uext[hc:hc + tm, :] = av[:, :D] * _sig(av[:, D:])

        def conv(r0, src, ks, first):
            acc = jnp.broadcast_to(bdw_ref[...], (CHUNK, D)) if first else c_ref[pl.ds(r0, CHUNK), :]
            for k, o in ks:
                acc = acc + wdw_ref[k:k + 1, :] * src[pl.ds(pl.multiple_of(r0 + o, 8), CHUNK), :]
            c_ref[pl.ds(r0, CHUNK), :] = acc
        _by_sublane_shift(uext, shifted, tm, [hc - (taps - 1) + k for k in range(taps)], conv)

        cv = c_ref[...]
        xc = cv - jnp.mean(cv, axis=-1, keepdims=True)
        nh = xc * lax.rsqrt(jnp.mean(xc * xc, axis=-1, keepdims=True) + LN_EPS)
        n = nh * lng_ref[...] + lnb_ref[...]
        sb = (n * _sig(n)).astype(BF16)
        sbt_ref[...] = sb.T
        x1 = x_ref[...] + _dot(sb, w2_ref[...]) + b2_ref[...]
        x1_ref[...] = x1
        hb = _rms(x1, gn_ref[...])[0].astype(BF16)
        hb_ref[...] = hb
        hbt_ref[...] = hb.T

    return _pcall(body, name="conv_mid", grid=(T // tm,),
                  in_specs=[_rows(tm, 2 * D), _prev_halo(tm, hc, 2 * D), _rows(tm, D), _full(wdw.shape), _full((1, D)),
                            _full((1, D)), _full((1, D)), _full((D, D)), _full((1, D)), _full((1, D))],
                  out_specs=[_rows(tm, D), _cols(D, tm), _rows(tm, D), _rows(tm, D), _cols(D, tm)],
                  out_shape=[S_((T, D), F32), S_((D, T), BF16), S_((T, D), F32), S_((T, D), BF16), S_((D, T), BF16)],
                  scratch=[pltpu.VMEM((tm + hc, D), F32), pltpu.VMEM((tm + hc - 8, D), F32)],
                  args=(a, a, x, wdw, bdw, lng, lnb, w2, b2, gn), comm=comm)


def _mm_up(hb, wg, tm, name, comm=None):
    T, D = hb.shape
    ns, _, nc = wg.shape

    def body(h_ref, w_ref, o_ref):
        o_ref[...] = _dot(h_ref[...], w_ref[...])

    return _pcall(body, name=name, grid=(ns, T // tm),
                  in_specs=[pl.BlockSpec((tm, D), lambda s, i: (i, 0)), pl.BlockSpec((None, D, nc), lambda s, i: (s, 0, 0))],
                  out_specs=pl.BlockSpec((tm, nc), lambda s, i: (i, s)),
                  out_shape=S_((T, ns * nc), F32), args=(hb, wg), comm=comm)


def _ffn_taps(aext, realigned, tm, ffn_taps):
    for k in range(ffn_taps - 1):
        realigned[k] = aext[pl.ds(HALO_FFN - (ffn_taps - 1) + k, tm), :]
    return [realigned.at[k] for k in range(ffn_taps - 1)] + [aext.at[pl.ds(HALO_FFN, tm)]]


def _ffn_conv(taps, wdw_ref, bdw_ref):
    acc = bdw_ref[...]
    for k, tap in enumerate(taps):
        acc = acc + wdw_ref[k:k + 1, :] * tap[...]
    return acc


def _by_sublane_shift(ext, shifted, tm, offsets, per_chunk):
    first = True
    for b in range(8):
        ks = [(k, o - b) for k, o in enumerate(offsets) if o % 8 == b]
        if not ks:
            continue
        if b:
            shifted[...] = ext[pl.ds(b, shifted.shape[0]), :]
        _chunks(tm, functools.partial(per_chunk, src=shifted if b else ext, ks=ks, first=first))
        first = False


def _ffn_down(up, xin, wdw, bdw, wd, gn, tm, tps, ffn_taps, name, target=None, comm=None):
    T, D = xin.shape
    F = up.shape[1] // 2
    hf = HALO_FFN
    final = target is not None

    def body(*refs):
        if final:
            (up_ref, uph_ref, x_ref, wdw_ref, bdw_ref, wd_ref, gn_ref, tg_ref,
             gb_ref, dx_ref, loss_ref, dgn_ref, aext, realigned) = refs
        else:
            (up_ref, uph_ref, x_ref, wdw_ref, bdw_ref, wd_ref, gn_ref,
             gb_ref, xo_ref, h_ref, aext, realigned) = refs
        i = pl.program_id(0)
        aext[0:hf, :] = jnp.where(i % tps == 0, 0.0, uph_ref[...])
        aext[hf:hf + tm, :] = up_ref[:, 0:F]
        acv = _ffn_conv(_ffn_taps(aext, realigned, tm, ffn_taps), wdw_ref, bdw_ref)
        gb = (acv * _sig(acv) * up_ref[:, F:2 * F]).astype(BF16)
        gb_ref[...] = gb.T
        xo = x_ref[...] + _dot(gb, wd_ref[...])
        if not final:
            xo_ref[...] = xo
            h_ref[...] = _rms(xo, gn_ref[...])[0]
        else:
            out, xh, r = _rms(xo, gn_ref[...])
            diff = out - tg_ref[...]
            part = 0.5 / D * _colsum(diff * diff)
            dx, dg = _rms_bwd(diff * (1.0 / D), xh, r, gn_ref[...])
            dx_ref[...] = dx
            _accumulate(loss_ref, part, i == 0)
            _accumulate(dgn_ref, dg, i == 0)

    in_specs = [_rows(tm, 2 * F), _prev_halo(tm, hf, F), _rows(tm, D), _full(wdw.shape), _full((1, F)),
                _full((F, D)), _full((1, D))]
    args = [up, up, xin, wdw, bdw, wd, gn]
    if final:
        in_specs.append(_rows(tm, D))
        args.append(target)
        out_specs = [_cols(F, tm), _rows(tm, D), _acc_spec(1, D), _acc_spec(1, D)]
        out_shape = [S_((F, T), BF16), S_((T, D), F32), S_((1, D), F32), S_((1, D), F32)]
    else:
        out_specs = [_cols(F, tm), _rows(tm, D), _rows(tm, D)]
        out_shape = [S_((F, T), BF16), S_((T, D), F32), S_((T, D), F32)]
    return _pcall(body, name=name, grid=(T // tm,), in_specs=in_specs, out_specs=out_specs, out_shape=out_shape,
                  scratch=[pltpu.VMEM((tm + hf, F), F32), pltpu.VMEM((ffn_taps - 1, tm, F), F32)], args=args, comm=comm)


def _ffn_fwd(hb, xin, wg, wdw, bdw, wd, gn, tm, tps, ffn_taps, name, target=None, comm=None):
    T, D = xin.shape
    ns, _, nc = wg.shape
    F = ns * nc // 2
    hf = HALO_FFN
    hh = 2 * HALO_FFN
    final = target is not None

    def body(*refs):
        if final:
            (h_ref, hh_ref, x_ref, wg_ref, wdw_ref, bdw_ref, wd_ref, gn_ref, tg_ref,
             up_ref, gb_ref, dx_ref, loss_ref, dgn_ref, aext, realigned) = refs
        else:
            (h_ref, hh_ref, x_ref, wg_ref, wdw_ref, bdw_ref, wd_ref, gn_ref,
             up_ref, gb_ref, xo_ref, ho_ref, aext, realigned) = refs
        i = pl.program_id(0)
        hv = h_ref[...]
        for s in range(ns):
            up_ref[:, s * nc:(s + 1) * nc] = _dot(hv, wg_ref[s])
        halo = jnp.concatenate([_dot(hh_ref[...], wg_ref[s]) for s in range(ns // 2)], axis=1)[hh - hf:hh, :]
        aext[0:hf, :] = jnp.where(i % tps == 0, 0.0, halo)
        aext[hf:hf + tm, :] = up_ref[:, 0:F]
        acv = _ffn_conv(_ffn_taps(aext, realigned, tm, ffn_taps), wdw_ref, bdw_ref)
        gb = (acv * _sig(acv) * up_ref[:, F:2 * F]).astype(BF16)
        gb_ref[...] = gb.T
        xo = x_ref[...] + _dot(gb, wd_ref[...])
        if not final:
            xo_ref[...] = xo
            ho_ref[...] = _rms(xo, gn_ref[...])[0]
        else:
            out, xh, r = _rms(xo, gn_ref[...])
            diff = out - tg_ref[...]
            part = 0.5 / D * _colsum(diff * diff)
            dx, dg = _rms_bwd(diff * (1.0 / D), xh, r, gn_ref[...])
            dx_ref[...] = dx
            _accumulate(loss_ref, part, i == 0)
            _accumulate(dgn_ref, dg, i == 0)

    in_specs = [_rows(tm, D), _prev_halo(tm, hh, D), _rows(tm, D), _full(wg.shape), _full(wdw.shape), _full((1, F)),
                _full((F, D)), _full((1, D))]
    args = [hb, hb, xin, wg, wdw, bdw, wd, gn]
    if final:
        in_specs.append(_rows(tm, D))
        args.append(target)
        out_specs = [_rows(tm, 2 * F), _cols(F, tm), _rows(tm, D), _acc_spec(1, D), _acc_spec(1, D)]
        out_shape = [S_((T, 2 * F), F32), S_((F, T), BF16), S_((T, D), F32), S_((1, D), F32), S_((1, D), F32)]
    else:
        out_specs = [_rows(tm, 2 * F), _cols(F, tm), _rows(tm, D), _rows(tm, D)]
        out_shape = [S_((T, 2 * F), F32), S_((F, T), BF16), S_((T, D), F32), S_((T, D), F32)]
    return _pcall(body, name=name, grid=(T // tm,), in_specs=in_specs, out_specs=out_specs, out_shape=out_shape,
                  scratch=[pltpu.VMEM((tm + hf, F), F32), pltpu.VMEM((ffn_taps - 1, tm, F), F32)], args=args, comm=comm)


def _pool_counts(i, tps, tm, w):
    pos = (i % tps) * tm + lax.broadcasted_iota(jnp.int32, (tm, 1), 0)
    return jnp.minimum(pos + 1, w).astype(F32)


def _window_sums(src, buf_a, buf_b, cg, causal):
    assert POOL_WINDOWS == (2, 4, 8, 16)
    n = src.shape[0]
    levels = len(POOL_WINDOWS)
    cur, outs = src, []
    for l in range(levels):
        dst = buf_b if l % 2 else buf_a
        cols = slice(l * cg, levels * cg)
        lo, m = 8 * (l + 1), n - 8 * (l + 1)
        if causal:
            dst[lo:n, cols] = cur[lo:n, cols] + cur[pl.ds(lo - (1 << l), m), cols]
        else:
            dst[0:m, cols] = cur[0:m, cols] + cur[pl.ds(1 << l, m), cols]
        outs.append(dst)
        cur = dst
    return outs


def _pool_fwd(h, x2, pw, pb, ps, gn, tm, tps):
    T, D = h.shape
    ng, cg, _ = pw.shape
    hp = HALO_POOL

    def body(h_ref, hh_ref, x_ref, pw_ref, pb_ref, ps_ref, gn_ref, pl_ref, xo_ref, hb_ref, hbt_ref, hext, buf_a, buf_b):
        i = pl.program_id(0)
        hext[0:hp, :] = jnp.where(i % tps == 0, 0.0, hh_ref[...])
        hext[hp:hp + tm, :] = h_ref[...]
        sums = _window_sums(hext, buf_a, buf_b, cg, True)
        for g, w in enumerate(POOL_WINDOWS):
            cs = slice(g * cg, (g + 1) * cg)
            pl_ref[:, cs] = (sums[g][hp:hp + tm, cs] / _pool_counts(i, tps, tm, w) - h_ref[:, cs]).astype(BF16)

        pv = pl_ref[...]
        mixed = jnp.concatenate([_dot(pv[:, g * cg:(g + 1) * cg], pw_ref[g]) for g in range(ng)], axis=1)
        xo = x_ref[...] + ps_ref[...] * (mixed + pb_ref[...])
        xo_ref[...] = xo
        hb = _rms(xo, gn_ref[...])[0].astype(BF16)
        hb_ref[...] = hb
        hbt_ref[...] = hb.T

    return _pcall(body, name="pool_fwd", grid=(T // tm,),
                  in_specs=[_rows(tm, D), _prev_halo(tm, hp, D), _rows(tm, D), _full(pw.shape), _full((1, D)), _full((1, D)),
                            _full((1, D))],
                  out_specs=[_rows(tm, D), _rows(tm, D), _rows(tm, D), _cols(D, tm)],
                  out_shape=[S_((T, D), BF16), S_((T, D), F32), S_((T, D), BF16), S_((D, T), BF16)],
                  scratch=[pltpu.VMEM((tm + hp, D), F32)] * 3, args=(h, h, x2, pw, pb, ps, gn))


def _bwd_down(dx, up, wdw, bdw, wd, tm, tps, ffn_taps, name, comm=None):
    T, D = dx.shape
    F = up.shape[1] // 2
    hf = HALO_FFN
    nt = T // tm

    def body(dx_ref, up_ref, uph_ref, wdw_ref, bdw_ref, wd_ref, dac_ref, dgate_ref, wsum_ref, aext, realigned):
        i = pl.program_id(0)
        aext[0:hf, :] = jnp.where(i % tps == 0, 0.0, uph_ref[...])
        aext[hf:hf + tm, :] = up_ref[:, 0:F]
        taps = _ffn_taps(aext, realigned, tm, ffn_taps)
        acv = _ffn_conv(taps, wdw_ref, bdw_ref)
        dg = _dot_nt(dx_ref[...].astype(BF16), wd_ref[...])
        sg = _sig(acv)
        dgate_ref[...] = (dg * acv * sg).astype(BF16)
        dac = dg * up_ref[:, F:2 * F] * (sg * (1.0 + acv * (1.0 - sg)))
        dac_ref[...] = dac
        rows = [_colsum(dac * tap[...]) for tap in taps]
        rows += [_colsum(dac), jnp.zeros((8 - ffn_taps - 1, F), F32)]
        _accumulate(wsum_ref, jnp.concatenate(rows, axis=0), i == 0)

    return _pcall(body, name=name, grid=(nt,),
                  in_specs=[_rows(tm, D), _rows(tm, 2 * F), _prev_halo(tm, hf, F), _full(wdw.shape), _full((1, F)),
                            _full((F, D))],
                  out_specs=[_rows(tm, F), _rows(tm, F, col=1), _acc_spec(8, F)],
                  out_shape=[S_((T, F), F32), S_((T, 2 * F), BF16), S_((8, F), F32)],
                  scratch=[pltpu.VMEM((tm + hf, F), F32), pltpu.VMEM((ffn_taps - 1, tm, F), F32)],
                  args=(dx, up, up, wdw, bdw, wd), comm=comm)


def _bwd_up(dac, dup, wg, wdw, xin, gn, dx, tm, tps, ffn_taps, name, comm=None):
    T, D = xin.shape
    F = dac.shape[1]
    ns, _, nc = wg.shape
    hf = HALO_FFN
    nt = T // tm

    def body(dac_ref, dach_ref, dgate_ref, wg_ref, wdw_ref, x_ref, gn_ref, dx_ref, dact_ref, dxo_ref, dgn_ref, dext):
        i = pl.program_id(0)
        dext[0:tm, :] = dac_ref[...]
        dext[tm:tm + hf, :] = jnp.where(i % tps == tps - 1, 0.0, dach_ref[...])
        dact = wdw_ref[ffn_taps - 1:ffn_taps, :] * dac_ref[...]
        for k in range(ffn_taps - 1):
            dact = dact + wdw_ref[k:k + 1, :] * dext[pl.ds(ffn_taps - 1 - k, tm), :]
        dact_ref[...] = dact.astype(BF16)

        dh = jnp.zeros((tm, D), F32)
        for s in range(ns):
            src = dact_ref if s < ns // 2 else dgate_ref
            o = (s % (ns // 2)) * nc
            dh = dh + _dot_nt(src[:, o:o + nc], wg_ref[s])
        _, xh, r = _rms(x_ref[...], gn_ref[...])
        dxn, dg = _rms_bwd(dh, xh, r, gn_ref[...])
        dxo_ref[...] = dx_ref[...] + dxn
        _accumulate(dgn_ref, dg, i == 0)

    return _pcall(body, name=name, grid=(nt,),
                  in_specs=[_rows(tm, F), _next_halo(tm, hf, F, T), _rows(tm, F, col=1), _full(wg.shape), _full(wdw.shape),
                            _rows(tm, D), _full((1, D)), _rows(tm, D)],
                  out_specs=[_rows(tm, F), _rows(tm, D), _acc_spec(1, D)],
                  out_shape=[S_((T, 2 * F), BF16), S_((T, D), F32), S_((1, D), F32)],
                  scratch=[pltpu.VMEM((tm + hf, F), F32)],
                  aliases={2: 0}, args=(dac, dac, dup, wg, wdw, xin, gn, dx), comm=comm)


def _mm_tn(xt, dy, bk, bn, tt, shard_major, name):
    K, T = xt.shape
    N = dy.shape[1]
    nt = T // tt

    def body(x_ref, dy_ref, o_ref, acc):
        t = pl.program_id(2)
        part = _dot(x_ref[...], dy_ref[...].astype(BF16))
        _accumulate(acc, part, t == 0)

        @pl.when(t == nt - 1)
        def _():
            o_ref[...] = acc[...].astype(o_ref.dtype)

    if shard_major:
        out_spec = pl.BlockSpec((None, bk, bn), lambda i, j, t: (j, i, 0))
        out_shape = S_((N // bn, K, bn), BF16)
    else:
        out_spec = pl.BlockSpec((bk, bn), lambda i, j, t: (i, j))
        out_shape = S_((K, N), BF16)
    return _pcall(body, name=name, grid=(K // bk, N // bn, nt),
                  in_specs=[pl.BlockSpec((bk, tt), lambda i, j, t: (i, t)), pl.BlockSpec((tt, bn), lambda i, j, t: (t, j))],
                  out_specs=out_spec, out_shape=out_shape, scratch=[pltpu.VMEM((bk, bn), F32)], args=(xt, dy))


def _pool_bwd(dx3, plb, x2, pw, pb, ps, gn, tm, tps):
    T, D = x2.shape
    ng, cg, _ = pw.shape
    hp = HALO_POOL
    nt = T // tm

    def body(do_ref, doh_ref, pl_ref, x_ref, pw_ref, pb_ref, ps_ref, gn_ref, dxo_ref, dpw_ref, sm_ref, qext, buf_a, buf_b,
             dh_s):
        i = pl.program_id(0)
        do = do_ref[...]
        dm = do * ps_ref[...]
        dmh = jnp.where(i % tps == tps - 1, 0.0, doh_ref[...]) * ps_ref[...]
        pv = pl_ref[...]
        mixed = jnp.concatenate([_dot(pv[:, g * cg:(g + 1) * cg], pw_ref[g]) for g in range(ng)], axis=1)
        dscale = _colsum(do * (mixed + pb_ref[...]))
        dbias = _colsum(dm)
        dmb = dm.astype(BF16)
        dmhb = dmh.astype(BF16)
        for g, w in enumerate(POOL_WINDOWS):
            cs = slice(g * cg, (g + 1) * cg)
            _accumulate(dpw_ref.at[g], _dot_tn(pv[:, cs], dmb[:, cs]), i == 0)
            dpo = _dot_nt(dmb[:, cs], pw_ref[g])
            dh_s[:, cs] = dpo
            qext[0:tm, cs] = dpo / _pool_counts(i, tps, tm, w)
            qext[tm:tm + hp, cs] = _dot_nt(dmhb[:, cs], pw_ref[g]) * (1.0 / w)
        sums = _window_sums(qext, buf_a, buf_b, cg, False)
        for g in range(ng):
            cs = slice(g * cg, (g + 1) * cg)
            dh_s[:, cs] = sums[g][0:tm, cs] - dh_s[:, cs]

        _, xh, r = _rms(x_ref[...], gn_ref[...])
        dxn, dg = _rms_bwd(dh_s[...], xh, r, gn_ref[...])
        dxo_ref[...] = do + dxn
        sm = jnp.concatenate([dscale, dbias, dg, jnp.zeros((5, D), F32)], axis=0)
        _accumulate(sm_ref, sm, i == 0)

    return _pcall(body, name="pool_bwd", grid=(nt,),
                  in_specs=[_rows(tm, D), _next_halo(tm, hp, D, T), _rows(tm, D), _rows(tm, D), _full(pw.shape), _full((1, D)),
                            _full((1, D)), _full((1, D))],
                  out_specs=[_rows(tm, D), pl.BlockSpec((ng, cg, cg), lambda i: (0, 0, 0)), _acc_spec(8, D)],
                  out_shape=[S_((T, D), F32), S_((ng, cg, cg), F32), S_((8, D), F32)],
                  scratch=[pltpu.VMEM((tm + hp, D), F32)] * 3 + [pltpu.VMEM((tm, D), F32)],
                  args=(dx3, dx3, plb, x2, pw, pb, ps, gn))


def _conv_bwd1(dx1, c, w2, lng, lnb, tm):
    T, D = dx1.shape

    def body(do_ref, c_ref, w2_ref, lng_ref, lnb_ref, dc_ref, sm_ref):
        i = pl.program_id(0)
        do = do_ref[...]
        ds = _dot_nt(do.astype(BF16), w2_ref[...])
        cv = c_ref[...]
        xc = cv - jnp.mean(cv, axis=-1, keepdims=True)
        rstd = lax.rsqrt(jnp.mean(xc * xc, axis=-1, keepdims=True) + LN_EPS)
        nh = xc * rstd
        n = nh * lng_ref[...] + lnb_ref[...]
        sg = _sig(n)
        dn = ds * (sg * (1.0 + n * (1.0 - sg)))
        dnh = dn * lng_ref[...]
        dc = rstd * (dnh - jnp.mean(dnh, axis=-1, keepdims=True) - nh * jnp.mean(dnh * nh, axis=-1, keepdims=True))
        dc_ref[...] = dc
        sm = jnp.concatenate([_colsum(dn * nh), _colsum(dn), _colsum(do), _colsum(dc), jnp.zeros((4, D), F32)], axis=0)
        _accumulate(sm_ref, sm, i == 0)

    return _pcall(body, name="conv_bwd1", grid=(T // tm,),
                  in_specs=[_rows(tm, D), _rows(tm, D), _full((D, D)), _full((1, D)), _full((1, D))],
                  out_specs=[_rows(tm, D), _acc_spec(8, D)],
                  out_shape=[S_((T, D), F32), S_((8, D), F32)], args=(dx1, c, w2, lng, lnb))


def _conv_bwd2(dc, a, x, w1g, wdw, gn, dx1, tm, tps, taps, comm=None):
    T, D = x.shape
    ns, _, nc = w1g.shape
    hc = HALO_CONV
    nt = T // tm
    tp = wdw.shape[0]

    def body(dc_ref, dch_ref, a_ref, ah_ref, x_ref, w_ref, wdw_ref, gn_ref, dx1_ref,
             gx_ref, dab_ref, dw_ref, db1_ref, dgn_ref, uext, dext, shifted, du_s, wacc):
        i = pl.program_id(0)
        ah = ah_ref[...]
        uext[0:hc, :] = jnp.where(i % tps == 0, 0.0, ah[:, :D] * _sig(ah[:, D:]))
        av = a_ref[...]
        sg2 = _sig(av[:, D:])
        uext[hc:hc + tm, :] = av[:, :D] * sg2
        dext[0:tm, :] = dc_ref[...]
        dext[tm:tm + hc, :] = jnp.where(i % tps == tps - 1, 0.0, dch_ref[...])

        @pl.when(i == 0)
        def _():
            wacc[...] = jnp.zeros_like(wacc)

        def wgrad(r0, src, ks, first):
            d = dc_ref[pl.ds(r0, CHUNK), :]
            for k, o in ks:
                wacc[8 * k:8 * k + 8, :] += _fold8(d * src[pl.ds(pl.multiple_of(r0 + o, 8), CHUNK), :])
        _by_sublane_shift(uext, shifted, tm, [hc - (taps - 1) + k for k in range(taps)], wgrad)

        def convt(r0, src, ks, first):
            acc = jnp.zeros((CHUNK, D), F32) if first else du_s[pl.ds(r0, CHUNK), :]
            for k, o in ks:
                acc = acc + wdw_ref[k:k + 1, :] * src[pl.ds(pl.multiple_of(r0 + o, 8), CHUNK), :]
            du_s[pl.ds(r0, CHUNK), :] = acc
        _by_sublane_shift(dext, shifted, tm, [taps - 1 - k for k in range(taps)], convt)

        du = du_s[...]
        da1 = du * sg2
        da2 = du * av[:, :D] * (sg2 * (1.0 - sg2))
        dab_ref[:, 0:D] = da1.astype(BF16)
        dab_ref[:, D:2 * D] = da2.astype(BF16)
        _accumulate(db1_ref, jnp.concatenate([_colsum(da1), _colsum(da2)], axis=0), i == 0)
        dh = jnp.zeros((tm, D), F32)
        for s in range(ns):
            dh = dh + _dot_nt(dab_ref[:, s * nc:(s + 1) * nc], w_ref[s])
        _, xh, r = _rms(x_ref[...], gn_ref[...])
        dxn, dg = _rms_bwd(dh, xh, r, gn_ref[...])
        gx_ref[...] = dx1_ref[...] + dxn
        _accumulate(dgn_ref, dg, i == 0)

        @pl.when(i == nt - 1)
        def _():
            dw_ref[...] = jnp.zeros_like(dw_ref)
            for k in range(taps):
                dw_ref[k:k + 1, :] = _colsum(wacc[8 * k:8 * k + 8, :])

    return _pcall(body, name="conv_bwd2", grid=(nt,),
                  in_specs=[_rows(tm, D), _next_halo(tm, hc, D, T), _rows(tm, 2 * D), _prev_halo(tm, hc, 2 * D), _rows(tm, D),
                            _full(w1g.shape), _full(wdw.shape), _full((1, D)), _rows(tm, D)],
                  out_specs=[_rows(tm, D), _rows(tm, 2 * D), _acc_spec(tp, D), _acc_spec(2, D), _acc_spec(1, D)],
                  out_shape=[S_((T, D), F32), S_((T, 2 * D), BF16), S_((tp, D), F32), S_((2, D), F32), S_((1, D), F32)],
                  scratch=[pltpu.VMEM((tm + hc, D), F32), pltpu.VMEM((tm + hc, D), F32), pltpu.VMEM((tm + hc - 8, D), F32),
                           pltpu.VMEM((tm, D), F32), pltpu.VMEM((8 * taps, D), F32)],
                  args=(dc, dc, a, a, x, w1g, wdw, gn, dx1), comm=comm)


def _place():
    x, y, c = lax.axis_index("x"), lax.axis_index("y"), lax.axis_index("c")
    chips = [(1 - x, y), (x, 1 - y), (1 - x, 1 - y)]
    return x, y, c, chips


def _own_slot(w, layer, ids, dtype, name):
    _, r, c = w.shape
    br = _row_block(r, c)

    def body(ids_ref, w_ref, o_ref):
        o_ref[...] = w_ref[...].astype(dtype)

    return _pcall(body, name=name, grid=(r // br,), prefetch=1,
                  in_specs=[pl.BlockSpec((None, br, c), lambda i, ids: (layer, i, 0))],
                  out_specs=pl.BlockSpec((None, br, c), lambda i, ids: (ids[0], i, 0)),
                  out_shape=S_((N_CHIPS, r, c), dtype), args=(ids, w))


class _GatherComm:
    def __init__(self, slots):
        n = len(slots)
        self.inputs = list(slots)
        self.out_shape = [S_(s.shape, s.dtype) for s in slots]
        self.aliases = {i: i for i in range(n)}
        self.sems = [pltpu.SemaphoreType.DMA((n, 3))] * 4
        self.halves = [s.shape[1] // 2 for s in slots]

    def _copies(self, outs, sems):
        s_ici, r_ici, s_d2d, r_d2d = sems
        x, y, c, chips = _place()
        peers = [(j, chip, 2 * chip[0] + chip[1]) for j, chip in enumerate(chips)]

        def half(i, h):
            return pl.ds(pl.multiple_of(h * self.halves[i], 8), self.halves[i])

        def ici(i, j, chip, to):
            blk = outs[i].at[chip, half(i, c)]
            return pltpu.make_async_remote_copy(src_ref=blk, dst_ref=blk, send_sem=s_ici.at[i, j], recv_sem=r_ici.at[i, j],
                                                device_id=to, device_id_type=MESH)

        def d2d(i, j, chip, h):
            blk = outs[i].at[chip, half(i, h)]
            return pltpu.make_async_remote_copy(src_ref=blk, dst_ref=blk, send_sem=s_d2d.at[i, j], recv_sem=r_d2d.at[i, j],
                                                device_id=(x, y, 1 - c), device_id_type=MESH)
        return (x, y, c, 2 * x + y), peers, ici, d2d

    def before(self, cin, outs, sems, step, total):
        (x, y, c, q), peers, ici, d2d = self._copies(outs, sems)

        @pl.when(step == 0)
        def _():
            for i in range(len(outs)):
                for j, chip, _ in peers:
                    ici(i, j, q, (*chip, c)).start()

    def after(self, cin, outs, sems, step, total):
        (x, y, c, q), peers, ici, d2d = self._copies(outs, sems)

        @pl.when(step == max(total - 2, 0))
        def _():
            for i in range(len(outs)):
                for j, chip, qj in peers:
                    ici(i, j, qj, (x, y, c)).wait_recv()
                    d2d(i, j, qj, c).start()

        @pl.when(step == total - 1)
        def _():
            for i in range(len(outs)):
                for j, chip, qj in peers:
                    d2d(i, j, qj, 1 - c).wait_recv()
            for i in range(len(outs)):
                for j, chip, qj in peers:
                    ici(i, j, q, (*chip, c)).wait_send()
                    d2d(i, j, qj, c).wait_send()


class _ExchangeComm:
    def __init__(self, sums):
        n = len(sums)
        self.inputs = list(sums)
        self.out_shape = [S_((3,) + s.shape[1:], s.dtype) for s in sums]
        self.aliases = {}
        self.sems = [pltpu.SemaphoreType.DMA((n, 3))] * 2

    def _copies(self, srcs, outs, sems):
        s_sem, r_sem = sems
        x, y, c, chips = _place()
        return [pltpu.make_async_remote_copy(src_ref=srcs[i].at[2 * chip[0] + chip[1]], dst_ref=outs[i].at[j],
                                             send_sem=s_sem.at[i, j], recv_sem=r_sem.at[i, j],
                                             device_id=(*chip, c), device_id_type=MESH)
                for i in range(len(srcs)) for j, chip in enumerate(chips)]

    def before(self, srcs, outs, sems, step, total):
        @pl.when(step == 0)
        def _():
            for cp in self._copies(srcs, outs, sems):
                cp.start()

    def after(self, srcs, outs, sems, step, total):
        @pl.when(step == total - 1)
        def _():
            for cp in self._copies(srcs, outs, sems):
                cp.wait()


def _comm_only(comm, name):
    def body():
        pass
    outs = _pcall(body, name=name, grid=(1,), in_specs=[], out_specs=[], out_shape=[], args=[], comm=comm)
    return list(outs)


def _pair_exchange(parts, name):
    n = len(parts)

    def body(*refs):
        srcs, outs = refs[:n], refs[n:2 * n]
        s_sem, r_sem = refs[2 * n:]
        x, y, c, _ = _place()
        cps = []
        for i in range(n):
            for s in range(N_CHIPS):
                cp = pltpu.make_async_remote_copy(src_ref=srcs[i].at[s, 1 - c], dst_ref=outs[i].at[s], send_sem=s_sem.at[i, s],
                                                  recv_sem=r_sem.at[i, s], device_id=(x, y, 1 - c), device_id_type=MESH)
                cp.start()
                cps.append(cp)
        for cp in cps:
            cp.wait()

    any_spec = pl.BlockSpec(memory_space=pl.ANY)
    outs = pl.pallas_call(
        body, name=name, in_specs=[any_spec] * n, out_specs=[any_spec] * n,
        out_shape=[S_((N_CHIPS,) + p.shape[2:], p.dtype) for p in parts],
        scratch_shapes=[pltpu.SemaphoreType.DMA((n, N_CHIPS))] * 2,
    )(*parts)
    return list(outs)


class _ShareComm:
    def __init__(self, bufs):
        n = len(bufs)
        self.inputs = list(bufs)
        self.out_shape = [S_(b.shape, b.dtype) for b in bufs]
        self.aliases = {i: i for i in range(n)}
        self.sems = [pltpu.SemaphoreType.DMA((n, max(b.shape[0] for b in bufs)))] * 2
        self.layers = [b.shape[0] for b in bufs]

    def _copies(self, outs, sems):
        s_sem, r_sem = sems
        x, y, c, _ = _place()
        cps = []
        for i, nl in enumerate(self.layers):
            for l in range(nl):
                blk = outs[i].at[l, c]
                cps.append(pltpu.make_async_remote_copy(src_ref=blk, dst_ref=blk, send_sem=s_sem.at[i, l],
                                                        recv_sem=r_sem.at[i, l], device_id=(x, y, 1 - c),
                                                        device_id_type=MESH))
        return cps

    def before(self, cin, outs, sems, step, total):
        @pl.when(step == 0)
        def _():
            for cp in self._copies(outs, sems):
                cp.start()

    def after(self, cin, outs, sems, step, total):
        @pl.when(step == total - 1)
        def _():
            for cp in self._copies(outs, sems):
                cp.wait()


class _Both:
    def __init__(self, a, b):
        self.parts = (a, b)
        self.inputs = a.inputs + b.inputs
        self.out_shape = a.out_shape + b.out_shape
        self.aliases = dict(a.aliases)
        self.aliases.update({len(a.inputs) + i: len(a.out_shape) + o for i, o in b.aliases.items()})
        self.sems = a.sems + b.sems

    def _each(self, phase, cin, cout, sems, step, total):
        i = o = s = 0
        for p in self.parts:
            ni, no, ns = len(p.inputs), len(p.out_shape), len(p.sems)
            getattr(p, phase)(cin[i:i + ni], cout[o:o + no], sems[s:s + ns], step, total)
            i, o, s = i + ni, o + no, s + ns

    def before(self, cin, cout, sems, step, total):
        self._each("before", cin, cout, sems, step, total)

    def after(self, cin, cout, sems, step, total):
        self._each("after", cin, cout, sems, step, total)


def _small_allreduce(groups):
    n = len(groups)
    arrays = []
    for grp in groups:
        for p in grp:
            if not isinstance(p, int) and not any(p[0] is a for a in arrays):
                arrays.append(p[0])
    shapes = []
    for grp in groups:
        rows = sum(p if isinstance(p, int) else p[2] - p[1] for p in grp)
        width = next(p[0].shape[1] for p in grp if not isinstance(p, int))
        assert rows % 8 == 0
        shapes.append((rows, width))
    na = len(arrays)

    def body(*refs):
        a_refs, o_refs = refs[:na], refs[na:na + n]
        rest = refs[na + n:]
        v_refs, sibs, css, gots = rest[:n], rest[n:2 * n], rest[2 * n:3 * n], rest[3 * n:4 * n]
        s_sem, r_sem = rest[4 * n:]
        for i, grp in enumerate(groups):
            o = 0
            for p in grp:
                if isinstance(p, int):
                    v_refs[i][o:o + p, :] = jnp.zeros((p, shapes[i][1]), F32)
                    o += p
                else:
                    k = next(t for t, a in enumerate(arrays) if a is p[0])
                    v_refs[i][o:o + p[2] - p[1], :] = a_refs[k][p[1]:p[2], :]
                    o += p[2] - p[1]
        x, y, c, chips = _place()
        q = 2 * x + y
        to_sib = [pltpu.make_async_remote_copy(src_ref=v_refs[i], dst_ref=sibs[i], send_sem=s_sem.at[i, 3], recv_sem=r_sem.at[i, 3],
                                               device_id=(x, y, 1 - c), device_id_type=MESH) for i in range(n)]
        for cp in to_sib:
            cp.start()
        cps = []
        for i in range(n):
            to_sib[i].wait()
            mine, other = v_refs[i][...], sibs[i][...]
            css[i][...] = jnp.where(c == 0, mine, other) + jnp.where(c == 0, other, mine)
            for j, chip in enumerate(chips):
                cp = pltpu.make_async_remote_copy(src_ref=css[i], dst_ref=gots[i].at[j], send_sem=s_sem.at[i, j],
                                                  recv_sem=r_sem.at[i, j], device_id=(*chip, c), device_id_type=MESH)
                cp.start()
                cps.append(cp)
        for cp in cps:
            cp.wait()
        flips = [2, 1, 3]
        for i in range(n):
            total = None
            for k in range(N_CHIPS):
                d = q ^ k
                term = jnp.where(d == 0, css[i][...], 0.0)
                for j in range(3):
                    term = jnp.where(d == flips[j], gots[i][j], term)
                total = term if total is None else total + term
            o_refs[i][...] = total

    vm = pl.BlockSpec(memory_space=pltpu.VMEM)
    return pl.pallas_call(
        body, name="small_allreduce", in_specs=[vm] * na, out_specs=[vm] * n, out_shape=[S_(s, F32) for s in shapes],
        scratch_shapes=[pltpu.VMEM(s, F32) for s in shapes] * 3 + [pltpu.VMEM((3,) + s, F32) for s in shapes]
        + [pltpu.SemaphoreType.DMA((n, 4)), pltpu.SemaphoreType.DMA((n, 4))],
    )(*arrays)


def _row_block(r, c, target_bytes=1 << 20):
    br = r
    while br % 32 == 0 and br * c * 4 > target_bytes:
        br //= 2
    return br


def _chip_sum(part, sib, ids, name):
    _, _, r, c = part.shape
    br = _row_block(r, c, 1 << 22)

    def body(ids_ref, p_ref, s_ref, o_ref):
        o_ref[...] = (p_ref[...].astype(F32) + s_ref[...].astype(F32)).astype(BF16)

    return _pcall(body, name=name, grid=(N_CHIPS, r // br), prefetch=1,
                  in_specs=[pl.BlockSpec((None, None, br, c), lambda s, i, ids: (s, ids[1], i, 0)),
                            pl.BlockSpec((None, br, c), lambda s, i, ids: (s, i, 0))],
                  out_specs=pl.BlockSpec((None, br, c), lambda s, i, ids: (s, i, 0)),
                  out_shape=S_((N_CHIPS, r, c), BF16), args=(ids, part, sib))


def _final_sum(part, sib, got, ids, name, layer=0, n_layers=1, buf=None):
    _, r, c = got.shape
    br = _row_block(r, c, 1 << 21)

    def body(ids_ref, p_ref, s_ref, got_ref, *rest):
        v = p_ref[...].astype(F32) + s_ref[...].astype(F32)
        for j in range(3):
            v = v + got_ref[j].astype(F32)
        rest[-1][...] = v

    in_specs = [pl.BlockSpec((None, None, br, c), lambda i, ids: (ids[0], ids[1], i, 0)),
                pl.BlockSpec((None, br, c), lambda i, ids: (ids[0], i, 0)),
                pl.BlockSpec((3, br, c), lambda i, ids: (0, i, 0))]
    args = [ids, part, sib, got]
    if buf is not None:
        in_specs.append(pl.BlockSpec(memory_space=pl.ANY))
        args.append(buf)
    return _pcall(body, name=name, grid=(r // br,), prefetch=1, in_specs=in_specs,
                  out_specs=pl.BlockSpec((None, None, br, c), lambda i, ids: (layer, ids[1], i, 0)),
                  out_shape=S_((n_layers, 2, r, c), F32), args=args, aliases={4: 0} if buf is not None else None)


def _adam_math(w, g, m, v):
    c1 = 1.0 / (1.0 - ADAM_B1 ** ADAM_STEP)
    c2 = 1.0 / (1.0 - ADAM_B2 ** ADAM_STEP)
    mn = ADAM_B1 * m + (1.0 - ADAM_B1) * g
    vn = ADAM_B2 * v + (1.0 - ADAM_B2) * (g * g)
    return -ADAM_LR * ((mn * c1) / (jnp.sqrt(vn * c2) + ADAM_EPS) + ADAM_WD * w), mn, vn


def _adamw(w, g, m, v, name):
    nl, r, c = w.shape
    br = _row_block(r, c, 1 << 20)

    def body(w_ref, g_ref, m_ref, v_ref, d_ref, mo_ref, vo_ref):
        d_ref[...], mo_ref[...], vo_ref[...] = _adam_math(w_ref[...], g_ref[...], m_ref[...], v_ref[...])

    blk = pl.BlockSpec((None, br, c), lambda l, i: (l, i, 0))
    return _pcall(body, name=name, grid=(nl, r // br), in_specs=[blk] * 4, out_specs=[blk] * 3,
                  out_shape=[S_((nl, r, c), F32)] * 3, args=(w, g, m, v))


def _adamw_small(rd, rf, sharded_g, ws, ms, vs, rows_d, rows_f, loss_row):
    n = len(ws)
    ns = len(sharded_g)
    rep = [k for k in range(n) if rows_d[k] is not None or rows_f[k] is not None]

    def body(*refs):
        rd_ref, rf_ref = refs[0], refs[1]
        sg = refs[2:2 + ns]
        w_refs, m_refs, v_refs = (refs[2 + ns + t * n:2 + ns + (t + 1) * n] for t in range(3))
        loss_ref, outs = refs[2 + ns + 3 * n], refs[3 + ns + 3 * n:]
        g_out, d_out, m_out, v_out = outs[:len(rep)], outs[len(rep):len(rep) + n], outs[len(rep) + n:len(rep) + 2 * n], \
            outs[len(rep) + 2 * n:]
        loss_ref[...] = jnp.sum(rd_ref[loss_row:loss_row + 1, :], axis=1, keepdims=True)
        si = 0
        for k in range(n):
            shape = w_refs[k].shape
            if rows_d[k] is not None or rows_f[k] is not None:
                src, rws = (rd_ref, rows_d[k]) if rows_d[k] is not None else (rf_ref, rows_f[k])
                axis = 0 if shape[0] == len(rws) else 1
                g = src[rws[0]:rws[0] + 1, :]
                if len(rws) > 1:
                    g = jnp.concatenate([src[r:r + 1, :] for r in rws], axis=axis)
                g_out[rep.index(k)][...] = g
            else:
                g = sg[si][...]
                si += 1
            d_out[k][...], m_out[k][...], v_out[k][...] = _adam_math(w_refs[k][...], g, m_refs[k][...], v_refs[k][...])

    vm = pl.BlockSpec(memory_space=pltpu.VMEM)
    shapes = [S_(w.shape, F32) for w in ws]
    outs = pl.pallas_call(
        body, name="adamw_small", in_specs=[vm] * (2 + ns + 3 * n), out_specs=[vm] * (1 + len(rep) + 3 * n),
        out_shape=[S_((1, 1), F32)] + [shapes[k] for k in rep] + shapes * 3,
    )(rd, rf, *sharded_g, *ws, *ms, *vs)
    g_rep = dict(zip(rep, outs[1:1 + len(rep)]))
    o = outs[1 + len(rep):]
    return outs[0], g_rep, o[:n], o[n:2 * n], o[2 * n:]


def _pack(pieces, rows):
    flat = jnp.concatenate([p.reshape(-1).astype(F32) for p in pieces])
    return jnp.pad(flat, (0, rows * 128 - flat.shape[0])).reshape(rows, 128)


def _unpack(packed, shapes):
    flat = packed.reshape(-1)
    out, o = [], 0
    for s in shapes:
        n = 1
        for d in s:
            n *= d
        out.append(flat[o:o + n].reshape(s))
        o += n
    return out


def _rows_for(pieces_or_shapes):
    n = 0
    for p in pieces_or_shapes:
        k = 1
        for d in (p if isinstance(p, tuple) else p.shape):
            k *= d
        n += k
    return -(-n // 1024) * 8


def kernel(x, norm_mix, norm_ffn, conv_w_pw1, conv_b_pw1, conv_w_dw, conv_b_dw, conv_ln_g, conv_ln_b, conv_w_pw2, conv_b_pw2, pool_w, pool_b, pool_scale, ffn_w_up, ffn_w_dw, ffn_b_dw, ffn_w_down, final_norm, loss_target, m_norm_mix, m_norm_ffn, m_conv_w_pw1, m_conv_b_pw1, m_conv_w_dw, m_conv_b_dw, m_conv_ln_g, m_conv_ln_b, m_conv_w_pw2, m_conv_b_pw2, m_pool_w, m_pool_b, m_pool_scale, m_ffn_w_up, m_ffn_w_dw, m_ffn_b_dw, m_ffn_w_down, m_final_norm, v_norm_mix, v_norm_ffn, v_conv_w_pw1, v_conv_b_pw1, v_conv_w_dw, v_conv_b_dw, v_conv_ln_g, v_conv_ln_b, v_conv_w_pw2, v_conv_b_pw2, v_pool_w, v_pool_b, v_pool_scale, v_ffn_w_up, v_ffn_w_dw, v_ffn_b_dw, v_ffn_w_down, v_final_norm):
    nb, seq, D = x.shape
    T = nb * seq
    F = ffn_w_down.shape[1] * N_CHIPS
    taps = conv_w_dw.shape[1]
    ffn_taps = ffn_w_dw.shape[1]
    ng = pool_w.shape[1]
    cg = pool_w.shape[3]
    dsh = D // N_CHIPS
    fsh = F // N_CHIPS
    assert taps - 1 <= HALO_CONV and ffn_taps - 1 <= HALO_FFN and max(POOL_WINDOWS) <= HALO_POOL
    tm = min(TILE_D, seq)
    tf = min(TILE_F, seq)
    assert seq % tm == 0 and seq % tf == 0 and tm % HALO_CONV == 0 and tf % CHUNK == 0
    tps, tpf = seq // tm, seq // tf
    tw = min(TILE_MM, T)
    assert T % tw == 0

    xi, yi, ci = lax.axis_index("x"), lax.axis_index("y"), lax.axis_index("c")
    qi = 2 * xi + yi
    ids = jnp.stack([qi, ci]).astype(jnp.int32)
    x2d = x.reshape(T, D)
    tg2d = loss_target.reshape(T, D)

    small_sharded = [conv_w_dw[0], ffn_w_dw, pool_b, pool_scale]
    srows = 2 * _rows_for(small_sharded)
    s_pw1, s_pw2 = _own_slot(conv_w_pw1, 0, ids, BF16, "slot_pw1"), _own_slot(conv_w_pw2, 0, ids, BF16, "slot_pw2")
    s_small = _own_slot(_pack(small_sharded, srows)[None], 0, ids, F32, "slot_small")
    s_pool = _own_slot(pool_w.reshape(1, ng * (cg // N_CHIPS), cg), 0, ids, BF16, "slot_pool")
    s_up = [_own_slot(ffn_w_up, l, ids, BF16, "slot_up%d" % l) for l in range(2)]
    s_down = [_own_slot(ffn_w_down, l, ids, BF16, "slot_down%d" % l) for l in range(2)]
    (w1g,) = _comm_only(_GatherComm([s_pw1]), "gather_pw1")
    hb0t, a, w2g, smallg = _conv_in(x2d, norm_mix[0:1], w1g, conv_b_pw1, tm, comm=_GatherComm([s_pw2, s_small]))
    w2f = w2g.reshape(D, D)
    sm_parts = [_unpack(smallg[s], [(taps, dsh), (2, ffn_taps, fsh), (1, dsh), (1, dsh)]) for s in range(N_CHIPS)]
    wdw_f = jnp.concatenate([p[0] for p in sm_parts], axis=1)
    wdw_f = jnp.pad(wdw_f, ((0, HALO_CONV - taps), (0, 0)))
    fdw_f = jnp.concatenate([p[1] for p in sm_parts], axis=2)
    fdw_f = jnp.pad(fdw_f, ((0, 0), (0, 8 - ffn_taps), (0, 0)))
    pb_f = jnp.concatenate([p[2] for p in sm_parts], axis=1)
    ps_f = jnp.concatenate([p[3] for p in sm_parts], axis=1)

    c, sbt, x1, hb1, hb1t, wup0, wd0, pwg = _conv_mid(a, x2d, wdw_f, conv_b_dw, conv_ln_g, conv_ln_b, w2f, conv_b_pw2,
                                                      norm_ffn[0:1], tm, tps, taps,
                                                      comm=_GatherComm([s_up[0], s_down[0], s_pool]))
    up0, gb0t, x2, h2, wup1, wd1 = _ffn_fwd(hb1, x1, wup0, fdw_f[0], ffn_b_dw[0:1], wd0.reshape(F, D), norm_mix[1:2], tf, tpf,
                                            ffn_taps, "ffn_fwd0", comm=_GatherComm([s_up[1], s_down[1]]))
    wup = [wup0, wup1]
    wdn = [wd0.reshape(F, D), wd1.reshape(F, D)]
    pwf = pwg.reshape(N_CHIPS, ng, cg // N_CHIPS, cg).transpose(1, 0, 2, 3).reshape(ng, cg, cg)
    plb, x3, hb3, hb3t = _pool_fwd(h2, x2, pwf, pb_f, ps_f, norm_ffn[1:2], tm, tps)
    up1, gb1t, dx4, loss_part, d_final = _ffn_fwd(hb3, x3, wup[1], fdw_f[1], ffn_b_dw[1:2], wdn[1], final_norm.reshape(1, D),
                                                  tf, tpf, ffn_taps, "ffn_fwd1", target=tg2d)

    fs = {}

    def chip_sums(group, tag):
        nms = list(group)
        from_sib = _pair_exchange([group[nm] for nm in nms], "grad_pair_exchange_" + tag)
        sums = [_chip_sum(group[nm], s, ids, "chip_sum_" + nm) for nm, s in zip(nms, from_sib)]
        for nm, s in zip(nms, from_sib):
            fs[nm] = (group[nm], s)
        return nms, _ExchangeComm(sums)

    def arrived(nms, got):
        for nm, g in zip(nms, got):
            fs[nm] = fs[nm] + (g,)

    dac1, dup1, fw1 = _bwd_down(dx4, up1, fdw_f[1], ffn_b_dw[1:2], wdn[1], tf, tpf, ffn_taps, "bwd_down1")
    p_down1 = _mm_tn(gb1t, dx4, F // 2, D, tw, False, "dw_down1")
    nms, xchg = chip_sums({"down1": p_down1.reshape(N_CHIPS, 2, fsh // 2, D)}, "down1")
    dup1, dx3, dnf1, *got = _bwd_up(dac1, dup1, wup[1], fdw_f[1], x3, norm_ffn[1:2], dx4, tf, tpf, ffn_taps,
                                         "bwd_up1", comm=xchg)
    arrived(nms, got)
    p_up1 = _mm_tn(hb3t, dup1, D, 2 * fsh, tw, True, "dw_up1")
    nms, xchg = chip_sums({"up1": p_up1.reshape(N_CHIPS, 2, D // 2, 2 * fsh)}, "up1")
    dx2, dpw, pool_sm = _pool_bwd(dx3, plb, x2, pwf, pb_f, ps_f, norm_mix[1:2], tm, tps)
    dac0, dup0, fw0, *got = _bwd_down(dx2, up0, fdw_f[0], ffn_b_dw[0:1], wdn[0], tf, tpf, ffn_taps, "bwd_down0", comm=xchg)
    arrived(nms, got)
    p_down0 = _mm_tn(gb0t, dx2, F // 2, D, tw, False, "dw_down0")
    nms, xchg = chip_sums({"down0": p_down0.reshape(N_CHIPS, 2, fsh // 2, D)}, "down0")
    dup0, dx1, dnf0, *got = _bwd_up(dac0, dup0, wup[0], fdw_f[0], x1, norm_ffn[0:1], dx2, tf, tpf, ffn_taps,
                                         "bwd_up0", comm=xchg)
    arrived(nms, got)
    p_up0 = _mm_tn(hb1t, dup0, D, 2 * fsh, tw, True, "dw_up0")
    dc, conv_sm = _conv_bwd1(dx1, c, w2f, conv_ln_g, conv_ln_b, tm)
    p_pw2 = _mm_tn(sbt, dx1, D, D, tw, False, "dw_pw2")
    csh = cg // N_CHIPS
    p_pool = dpw.reshape(2, ng // 2, N_CHIPS, csh, cg).transpose(2, 0, 1, 3, 4).reshape(N_CHIPS, 2, (ng // 2) * csh, cg)
    nms, xchg = chip_sums({"up0": p_up0.reshape(N_CHIPS, 2, D // 2, 2 * fsh), "pool": p_pool.astype(BF16),
                           "pw2": p_pw2.reshape(N_CHIPS, 2, dsh // 2, D)}, "mix")
    grad_x, dab, dwdw, db1, dnm0, *got = _conv_bwd2(dc, a, x2d, w1g, wdw_f, norm_mix[0:1], dx1, tm, tps, taps, comm=xchg)
    arrived(nms, got)
    p_pw1 = _mm_tn(hb0t, dab, D, 2 * dsh, tw, True, "dw_pw1")
    nms, xchg = chip_sums({"pw1": p_pw1.reshape(N_CHIPS, 2, D // 2, 2 * dsh)}, "pw1")
    bufs = [_final_sum(*fs[nm], ids, "final_sum_" + nm) for nm in ("pw2", "pool")]
    for nm in ("up", "down"):
        b = _final_sum(*fs[nm + "0"], ids, "final_sum_" + nm + "0", layer=0, n_layers=2)
        bufs.append(_final_sum(*fs[nm + "1"], ids, "final_sum_" + nm + "1", layer=1, n_layers=2, buf=b))
    *shared, got_pw1 = _comm_only(_Both(_ShareComm(bufs), xchg), "grad_share_halves")
    arrived(nms, [got_pw1])
    shared += _comm_only(_ShareComm([_final_sum(*fs["pw1"], ids, "final_sum_pw1")]), "grad_share_halves_pw1")
    g_pw2, g_pool, g_up, g_down, g_pw1 = [b.reshape(b.shape[0], 2 * b.shape[2], b.shape[3]) for b in shared]

    rd, rf = _small_allreduce([
        [(dnm0, 0, 1), (pool_sm, 2, 3), (dnf0, 0, 1), (dnf1, 0, 1), (db1, 0, 2), (conv_sm, 0, 4), (d_final, 0, 1),
         (pool_sm, 0, 2), (loss_part, 0, 1), 2, (dwdw, 0, dwdw.shape[0])],
        [(fw0, 0, 4), (fw1, 0, 4)]])
    g_wdw = lax.dynamic_slice_in_dim(rd[16:16 + taps], qi * dsh, dsh, axis=1)
    g_ps = lax.dynamic_slice_in_dim(rd[11:12], qi * dsh, dsh, axis=1)
    g_pb = lax.dynamic_slice_in_dim(rd[12:13], qi * dsh, dsh, axis=1)
    g_fdw = lax.dynamic_slice_in_dim(rf.reshape(2, 4, F)[:, :ffn_taps], qi * fsh, fsh, axis=2).reshape(2 * ffn_taps, fsh)

    def big(w, g, m, v, nm):
        d, mn, vn = _adamw(w.reshape(g.shape), g, m.reshape(g.shape), v.reshape(g.shape), "adamw_" + nm)
        return g.reshape(w.shape), d.reshape(w.shape), mn.reshape(w.shape), vn.reshape(w.shape)

    o_pw1 = big(conv_w_pw1, g_pw1, m_conv_w_pw1, v_conv_w_pw1, "pw1")
    o_pw2 = big(conv_w_pw2, g_pw2, m_conv_w_pw2, v_conv_w_pw2, "pw2")
    o_pool = big(pool_w, g_pool, m_pool_w, v_pool_w, "pool")
    o_up = big(ffn_w_up, g_up, m_ffn_w_up, v_ffn_w_up, "up")
    o_down = big(ffn_w_down, g_down, m_ffn_w_down, v_ffn_w_down, "down")

    snames = ["norm_mix", "norm_ffn", "b_pw1", "w_dw", "b_dw", "ln_g", "ln_b", "b_pw2", "pool_b", "pool_scale", "ffn_w_dw",
              "ffn_b_dw", "final_norm"]
    sw = [norm_mix, norm_ffn, conv_b_pw1, conv_w_dw, conv_b_dw, conv_ln_g, conv_ln_b, conv_b_pw2, pool_b, pool_scale,
          ffn_w_dw, ffn_b_dw, final_norm]
    smm = [m_norm_mix, m_norm_ffn, m_conv_b_pw1, m_conv_w_dw, m_conv_b_dw, m_conv_ln_g, m_conv_ln_b, m_conv_b_pw2, m_pool_b,
           m_pool_scale, m_ffn_w_dw, m_ffn_b_dw, m_final_norm]
    svv = [v_norm_mix, v_norm_ffn, v_conv_b_pw1, v_conv_w_dw, v_conv_b_dw, v_conv_ln_g, v_conv_ln_b, v_conv_b_pw2, v_pool_b,
           v_pool_scale, v_ffn_w_dw, v_ffn_b_dw, v_final_norm]
    sshapes = [tuple(w.shape) for w in sw]
    two_d = [(2, D), (2, D), (1, 2 * D), (taps, dsh), (1, D), (1, D), (1, D), (1, D), (1, dsh), (1, dsh), (2 * ffn_taps, fsh),
             (2, F), (1, D)]
    rows_d = [[0, 1], [2, 3], [4, 5], None, [9], [6], [7], [8], None, None, None, None, [10]]
    rows_f = [None] * 11 + [[3, 7], None]
    as2d = lambda ts: [t.reshape(s) for t, s in zip(ts, two_d)]
    loss, g_rep, sd, sm_new, sv_new = _adamw_small(rd, rf, [g_wdw, g_pb, g_ps, g_fdw], as2d(sw), as2d(smm), as2d(svv), rows_d,
                                                   rows_f, loss_row=13)
    sg = dict(g_rep)
    sg.update({3: g_wdw, 8: g_pb, 9: g_ps, 10: g_fdw})
    small_out = {n: tuple(t.reshape(sshapes[k]) for t in (sg[k], sd[k], sm_new[k], sv_new[k])) for k, n in enumerate(snames)}

    order = [small_out["norm_mix"], small_out["norm_ffn"], o_pw1, small_out["b_pw1"], small_out["w_dw"], small_out["b_dw"],
             small_out["ln_g"], small_out["ln_b"], o_pw2, small_out["b_pw2"], o_pool, small_out["pool_b"],
             small_out["pool_scale"], o_up, small_out["ffn_w_dw"], small_out["ffn_b_dw"], o_down, small_out["final_norm"]]
    return (loss[0, 0], grad_x.reshape(nb, seq, D), *[o[0] for o in order], *[o[1] for o in order], *[o[2] for o in order],
            *[o[3] for o in order])
```

```python
import functools

import jax
import jax.numpy as jnp
from jax import lax
from jax.experimental import pallas as pl
from jax.experimental.pallas import tpu as pltpu

F32, BF16 = jnp.float32, jnp.bfloat16
S_ = jax.ShapeDtypeStruct
MESH = pl.DeviceIdType.MESH

RMS_EPS, LN_EPS = 1e-6, 1e-5
POOL_WINDOWS = (2, 4, 8, 16)
ADAM_LR, ADAM_B1, ADAM_B2, ADAM_EPS, ADAM_WD, ADAM_STEP = 0.001, 0.9, 0.999, 1e-08, 0.01, 10

VMEM_LIMIT_BYTES = 60 * 1024 * 1024
N_CHIPS = 4
CHUNK = 32
HALO_CONV = 32
HALO_POOL = 32
HALO_FFN = 8
TILE_D = 512
TILE_F = 256
TILE_MM = 1024


def _pcall(body, *, name, grid, in_specs, out_specs, out_shape, args, scratch=(), aliases=None, prefetch=0, comm=None):
    params = pltpu.CompilerParams(dimension_semantics=("arbitrary",) * len(grid), vmem_limit_bytes=VMEM_LIMIT_BYTES)
    if comm is not None:
        assert not prefetch
        single = not isinstance(out_shape, (list, tuple))
        in_specs, args, scratch = list(in_specs), list(args), list(scratch)
        out_specs, out_shape = ([out_specs], [out_shape]) if single else (list(out_specs), list(out_shape))
        n_in, n_out, n_scr, n_cin, n_cout = len(in_specs), len(out_specs), len(scratch), len(comm.inputs), len(comm.out_shape)
        any_spec = pl.BlockSpec(memory_space=pl.ANY)
        aliases = dict(aliases or {})
        aliases.update({n_in + a: n_out + b for a, b in comm.aliases.items()})
        inner = body

        def body(*refs):
            ins, cin = refs[:n_in], refs[n_in:n_in + n_cin]
            outs = refs[n_in + n_cin:n_in + n_cin + n_out]
            cout = refs[n_in + n_cin + n_out:n_in + n_cin + n_out + n_cout]
            scr = refs[n_in + n_cin + n_out + n_cout:n_in + n_cin + n_out + n_cout + n_scr]
            sems = refs[n_in + n_cin + n_out + n_cout + n_scr:]
            step, total = 0, 1
            for d, g in enumerate(grid):
                step = step * g + pl.program_id(d)
                total *= g
            comm.before(cin, cout, sems, step, total)
            inner(*ins, *outs, *scr)
            comm.after(cin, cout, sems, step, total)

        in_specs += [any_spec] * n_cin
        args += list(comm.inputs)
        out_specs += [any_spec] * n_cout
        out_shape += list(comm.out_shape)
        scratch += list(comm.sems)
    if prefetch:
        spec = pltpu.PrefetchScalarGridSpec(num_scalar_prefetch=prefetch, grid=grid, in_specs=in_specs,
                                            out_specs=out_specs, scratch_shapes=list(scratch))
        return pl.pallas_call(body, name=name, grid_spec=spec, out_shape=out_shape,
                              input_output_aliases=aliases or {}, compiler_params=params)(*args)
    return pl.pallas_call(body, name=name, grid=grid, in_specs=in_specs, out_specs=out_specs, out_shape=out_shape,
                          scratch_shapes=list(scratch), input_output_aliases=aliases or {},
                          compiler_params=params)(*args)


def _full(shape):
    n = len(shape)
    return pl.BlockSpec(tuple(shape), lambda *_: (0,) * n, pipeline_mode=pl.Buffered(1))


def _rows(tm, c, col=0):
    return pl.BlockSpec((tm, c), lambda i, *_: (i, col))


def _cols(r, tm):
    return pl.BlockSpec((r, tm), lambda i, *_: (0, i))


def _prev_halo(tm, hb, c):
    return pl.BlockSpec((hb, c), lambda i, *_: (jnp.maximum(i * (tm // hb) - 1, 0), 0))


def _next_halo(tm, hb, c, total_rows):
    last = total_rows // hb - 1
    return pl.BlockSpec((hb, c), lambda i, *_: (jnp.minimum((i + 1) * (tm // hb), last), 0))


def _acc_spec(r, c):
    return pl.BlockSpec((r, c), lambda *_: (0, 0))


def _rms(x, g):
    r = lax.rsqrt(jnp.mean(x * x, axis=-1, keepdims=True) + RMS_EPS)
    xh = x * r
    return xh * g, xh, r


def _rms_bwd(dy, xh, r, g):
    dxh = dy * g
    dx = r * (dxh - xh * jnp.mean(dxh * xh, axis=-1, keepdims=True))
    return dx, jnp.sum(dy * xh, axis=0, keepdims=True)


def _colsum(v):
    return jnp.sum(v, axis=0, keepdims=True)


def _fold8(v):
    out = v[0:8]
    for j in range(1, v.shape[0] // 8):
        out = out + v[8 * j:8 * j + 8]
    return out


def _chunks(n_rows, fn):
    def step(j, carry):
        fn(pl.multiple_of(j * CHUNK, CHUNK))
        return carry
    lax.fori_loop(0, n_rows // CHUNK, step, 0)


def _dot(a, b):
    return jnp.dot(a, b, preferred_element_type=F32)


def _dot_nt(a, b):
    return lax.dot_general(a, b, (((1,), (1,)), ((), ())), preferred_element_type=F32)


def _dot_tn(a, b):
    return lax.dot_general(a, b, (((0,), (0,)), ((), ())), preferred_element_type=F32)


def _sig(v):
    return jax.nn.sigmoid(v)


def _accumulate(ref, val, first):
    @pl.when(first)
    def _():
        ref[...] = val

    @pl.when(jnp.logical_not(first))
    def _():
        ref[...] = ref[...] + val


def _conv_in(x, g, w1g, b1, tm, comm=None):
    T, D = x.shape
    ns, _, nc = w1g.shape

    def body(x_ref, g_ref, w_ref, b_ref, hbt_ref, a_ref):
        hb = _rms(x_ref[...], g_ref[...])[0].astype(BF16)
        hbt_ref[...] = hb.T
        for s in range(ns):
            a_ref[:, s * nc:(s + 1) * nc] = _dot(hb, w_ref[s]) + b_ref[:, s * nc:(s + 1) * nc]

    return _pcall(body, name="conv_in", grid=(T // tm,),
                  in_specs=[_rows(tm, D), _full((1, D)), _full(w1g.shape), _full((1, ns * nc))],
                  out_specs=[_cols(D, tm), _rows(tm, ns * nc)],
                  out_shape=[S_((D, T), BF16), S_((T, ns * nc), F32)], args=(x, g, w1g, b1), comm=comm)


def _conv_mid(a, x, wdw, bdw, lng, lnb, w2, b2, gn, tm, tps, taps, comm=None):
    T, D = x.shape
    hc = HALO_CONV

    def body(a_ref, ah_ref, x_ref, wdw_ref, bdw_ref, lng_ref, lnb_ref, w2_ref, b2_ref, gn_ref,
             c_ref, sbt_ref, x1_ref, hb_ref, hbt_ref, uext, shifted):
        i = pl.program_id(0)
        ah = ah_ref[...]
        uext[0:hc, :] = jnp.where(i % tps == 0, 0.0, ah[:, :D] * _sig(ah[:, D:]))
        av = a_ref[...]
        uext[hc:hc + tm, :] = av[:, :D] * _sig(av[:, D:])

        def conv(r0, src, ks, first):
            acc = jnp.broadcast_to(bdw_ref[...], (CHUNK, D)) if first else c_ref[pl.ds(r0, CHUNK), :]
            for k, o in ks:
                acc = acc + wdw_ref[k:k + 1, :] * src[pl.ds(pl.multiple_of(r0 + o, 8), CHUNK), :]
            c_ref[pl.ds(r0, CHUNK), :] = acc
        _by_sublane_shift(uext, shifted, tm, [hc - (taps - 1) + k for k in range(taps)], conv)

        cv = c_ref[...]
        xc = cv - jnp.mean(cv, axis=-1, keepdims=True)
        nh = xc * lax.rsqrt(jnp.mean(xc * xc, axis=-1, keepdims=True) + LN_EPS)
        n = nh * lng_ref[...] + lnb_ref[...]
        sb = (n * _sig(n)).astype(BF16)
        sbt_ref[...] = sb.T
        x1 = x_ref[...] + _dot(sb, w2_ref[...]) + b2_ref[...]
        x1_ref[...] = x1
        hb = _rms(x1, gn_ref[...])[0].astype(BF16)
        hb_ref[...] = hb
        hbt_ref[...] = hb.T

    return _pcall(body, name="conv_mid", grid=(T // tm,),
                  in_specs=[_rows(tm, 2 * D), _prev_halo(tm, hc, 2 * D), _rows(tm, D), _full(wdw.shape), _full((1, D)),
                            _full((1, D)), _full((1, D)), _full((D, D)), _full((1, D)), _full((1, D))],
                  out_specs=[_rows(tm, D), _cols(D, tm), _rows(tm, D), _rows(tm, D), _cols(D, tm)],
                  out_shape=[S_((T, D), F32), S_((D, T), BF16), S_((T, D), F32), S_((T, D), BF16), S_((D, T), BF16)],
                  scratch=[pltpu.VMEM((tm + hc, D), F32), pltpu.VMEM((tm + hc - 8, D), F32)],
                  args=(a, a, x, wdw, bdw, lng, lnb, w2, b2, gn), comm=comm)


def _ffn_taps(aext, realigned, tm, ffn_taps):
    for k in range(ffn_taps - 1):
        realigned[k] = aext[pl.ds(HALO_FFN - (ffn_taps - 1) + k, tm), :]
    return [realigned.at[k] for k in range(ffn_taps - 1)] + [aext.at[pl.ds(HALO_FFN, tm)]]


def _ffn_conv(taps, wdw_ref, bdw_ref):
    acc = bdw_ref[...]
    for k, tap in enumerate(taps):
        acc = acc + wdw_ref[k:k + 1, :] * tap[...]
    return acc


def _by_sublane_shift(ext, shifted, tm, offsets, per_chunk):
    first = True
    for b in range(8):
        ks = [(k, o - b) for k, o in enumerate(offsets) if o % 8 == b]
        if not ks:
            continue
        if b:
            shifted[...] = ext[pl.ds(b, shifted.shape[0]), :]
        _chunks(tm, functools.partial(per_chunk, src=shifted if b else ext, ks=ks, first=first))
        first = False


def _ffn_fwd(hb, xin, wg, wdw, bdw, wd, gn, tm, tps, ffn_taps, name, target=None, comm=None):
    T, D = xin.shape
    ns, _, nc = wg.shape
    F = ns * nc // 2
    hf = HALO_FFN
    hh = 2 * HALO_FFN
    final = target is not None

    def body(*refs):
        if final:
            (h_ref, hh_ref, x_ref, wg_ref, wdw_ref, bdw_ref, wd_ref, gn_ref, tg_ref,
             up_ref, gb_ref, dx_ref, loss_ref, dgn_ref, aext, realigned) = refs
        else:
            (h_ref, hh_ref, x_ref, wg_ref, wdw_ref, bdw_ref, wd_ref, gn_ref,
             up_ref, gb_ref, xo_ref, ho_ref, aext, realigned) = refs
        i = pl.program_id(0)
        hv = h_ref[...]
        for s in range(ns):
            up_ref[:, s * nc:(s + 1) * nc] = _dot(hv, wg_ref[s])
        halo = jnp.concatenate([_dot(hh_ref[...], wg_ref[s]) for s in range(ns // 2)], axis=1)[hh - hf:hh, :]
        aext[0:hf, :] = jnp.where(i % tps == 0, 0.0, halo)
        aext[hf:hf + tm, :] = up_ref[:, 0:F]
        acv = _ffn_conv(_ffn_taps(aext, realigned, tm, ffn_taps), wdw_ref, bdw_ref)
        gb = (acv * _sig(acv) * up_ref[:, F:2 * F]).astype(BF16)
        gb_ref[...] = gb.T
        xo = x_ref[...] + _dot(gb, wd_ref[...])
        if not final:
            xo_ref[...] = xo
            ho_ref[...] = _rms(xo, gn_ref[...])[0]
        else:
            out, xh, r = _rms(xo, gn_ref[...])
            diff = out - tg_ref[...]
            part = 0.5 / D * _colsum(diff * diff)
            dx, dg = _rms_bwd(diff * (1.0 / D), xh, r, gn_ref[...])
            dx_ref[...] = dx
            _accumulate(loss_ref, part, i == 0)
            _accumulate(dgn_ref, dg, i == 0)

    in_specs = [_rows(tm, D), _prev_halo(tm, hh, D), _rows(tm, D), _full(wg.shape), _full(wdw.shape), _full((1, F)),
                _full((F, D)), _full((1, D))]
    args = [hb, hb, xin, wg, wdw, bdw, wd, gn]
    if final:
        in_specs.append(_rows(tm, D))
        args.append(target)
        out_specs = [_rows(tm, 2 * F), _cols(F, tm), _rows(tm, D), _acc_spec(1, D), _acc_spec(1, D)]
        out_shape = [S_((T, 2 * F), F32), S_((F, T), BF16), S_((T, D), F32), S_((1, D), F32), S_((1, D), F32)]
    else:
        out_specs = [_rows(tm, 2 * F), _cols(F, tm), _rows(tm, D), _rows(tm, D)]
        out_shape = [S_((T, 2 * F), F32), S_((F, T), BF16), S_((T, D), F32), S_((T, D), F32)]
    return _pcall(body, name=name, grid=(T // tm,), in_specs=in_specs, out_specs=out_specs, out_shape=out_shape,
                  scratch=[pltpu.VMEM((tm + hf, F), F32), pltpu.VMEM((ffn_taps - 1, tm, F), F32)], args=args, comm=comm)


def _pool_counts(i, tps, tm, w):
    pos = (i % tps) * tm + lax.broadcasted_iota(jnp.int32, (tm, 1), 0)
    return jnp.minimum(pos + 1, w).astype(F32)


def _window_sums(src, buf_a, buf_b, cg, causal):
    assert POOL_WINDOWS == (2, 4, 8, 16)
    n = src.shape[0]
    levels = len(POOL_WINDOWS)
    cur, outs = src, []
    for l in range(levels):
        dst = buf_b if l % 2 else buf_a
        cols = slice(l * cg, levels * cg)
        lo, m = 8 * (l + 1), n - 8 * (l + 1)
        if causal:
            dst[lo:n, cols] = cur[lo:n, cols] + cur[pl.ds(lo - (1 << l), m), cols]
        else:
            dst[0:m, cols] = cur[0:m, cols] + cur[pl.ds(1 << l, m), cols]
        outs.append(dst)
        cur = dst
    return outs


def _pool_fwd(h, x2, pw, pb, ps, gn, tm, tps):
    T, D = h.shape
    ng, cg, _ = pw.shape
    hp = HALO_POOL

    def body(h_ref, hh_ref, x_ref, pw_ref, pb_ref, ps_ref, gn_ref, pl_ref, xo_ref, hb_ref, hbt_ref, hext, buf_a, buf_b):
        i = pl.program_id(0)
        hext[0:hp, :] = jnp.where(i % tps == 0, 0.0, hh_ref[...])
        hext[hp:hp + tm, :] = h_ref[...]
        sums = _window_sums(hext, buf_a, buf_b, cg, True)
        for g, w in enumerate(POOL_WINDOWS):
            cs = slice(g * cg, (g + 1) * cg)
            pl_ref[:, cs] = (sums[g][hp:hp + tm, cs] / _pool_counts(i, tps, tm, w) - h_ref[:, cs]).astype(BF16)

        pv = pl_ref[...]
        mixed = jnp.concatenate([_dot(pv[:, g * cg:(g + 1) * cg], pw_ref[g]) for g in range(ng)], axis=1)
        xo = x_ref[...] + ps_ref[...] * (mixed + pb_ref[...])
        xo_ref[...] = xo
        hb = _rms(xo, gn_ref[...])[0].astype(BF16)
        hb_ref[...] = hb
        hbt_ref[...] = hb.T

    return _pcall(body, name="pool_fwd", grid=(T // tm,),
                  in_specs=[_rows(tm, D), _prev_halo(tm, hp, D), _rows(tm, D), _full(pw.shape), _full((1, D)), _full((1, D)),
                            _full((1, D))],
                  out_specs=[_rows(tm, D), _rows(tm, D), _rows(tm, D), _cols(D, tm)],
                  out_shape=[S_((T, D), BF16), S_((T, D), F32), S_((T, D), BF16), S_((D, T), BF16)],
                  scratch=[pltpu.VMEM((tm + hp, D), F32)] * 3, args=(h, h, x2, pw, pb, ps, gn))


def _bwd_down(dx, up, wdw, bdw, wd, tm, tps, ffn_taps, name, comm=None):
    T, D = dx.shape
    F = up.shape[1] // 2
    hf = HALO_FFN
    nt = T // tm

    def body(dx_ref, up_ref, uph_ref, wdw_ref, bdw_ref, wd_ref, dac_ref, dgate_ref, wsum_ref, aext, realigned):
        i = pl.program_id(0)
        aext[0:hf, :] = jnp.where(i % tps == 0, 0.0, uph_ref[...])
        aext[hf:hf + tm, :] = up_ref[:, 0:F]
        taps = _ffn_taps(aext, realigned, tm, ffn_taps)
        acv = _ffn_conv(taps, wdw_ref, bdw_ref)
        dg = _dot_nt(dx_ref[...].astype(BF16), wd_ref[...])
        sg = _sig(acv)
        dgate_ref[...] = (dg * acv * sg).astype(BF16)
        dac = dg * up_ref[:, F:2 * F] * (sg * (1.0 + acv * (1.0 - sg)))
        dac_ref[...] = dac
        rows = [_colsum(dac * tap[...]) for tap in taps]
        rows += [_colsum(dac), jnp.zeros((8 - ffn_taps - 1, F), F32)]
        _accumulate(wsum_ref, jnp.concatenate(rows, axis=0), i == 0)

    return _pcall(body, name=name, grid=(nt,),
                  in_specs=[_rows(tm, D), _rows(tm, 2 * F), _prev_halo(tm, hf, F), _full(wdw.shape), _full((1, F)),
                            _full((F, D))],
                  out_specs=[_rows(tm, F), _rows(tm, F, col=1), _acc_spec(8, F)],
                  out_shape=[S_((T, F), F32), S_((T, 2 * F), BF16), S_((8, F), F32)],
                  scratch=[pltpu.VMEM((tm + hf, F), F32), pltpu.VMEM((ffn_taps - 1, tm, F), F32)],
                  args=(dx, up, up, wdw, bdw, wd), comm=comm)


def _bwd_up(dac, dup, wg, wdw, xin, gn, dx, tm, tps, ffn_taps, name, comm=None):
    T, D = xin.shape
    F = dac.shape[1]
    ns, _, nc = wg.shape
    hf = HALO_FFN
    nt = T // tm

    def body(dac_ref, dach_ref, dgate_ref, wg_ref, wdw_ref, x_ref, gn_ref, dx_ref, dact_ref, dxo_ref, dgn_ref, dext):
        i = pl.program_id(0)
        dext[0:tm, :] = dac_ref[...]
        dext[tm:tm + hf, :] = jnp.where(i % tps == tps - 1, 0.0, dach_ref[...])
        dact = wdw_ref[ffn_taps - 1:ffn_taps, :] * dac_ref[...]
        for k in range(ffn_taps - 1):
            dact = dact + wdw_ref[k:k + 1, :] * dext[pl.ds(ffn_taps - 1 - k, tm), :]
        dact_ref[...] = dact.astype(BF16)

        dh = jnp.zeros((tm, D), F32)
        for s in range(ns):
            src = dact_ref if s < ns // 2 else dgate_ref
            o = (s % (ns // 2)) * nc
            dh = dh + _dot_nt(src[:, o:o + nc], wg_ref[s])
        _, xh, r = _rms(x_ref[...], gn_ref[...])
        dxn, dg = _rms_bwd(dh, xh, r, gn_ref[...])
        dxo_ref[...] = dx_ref[...] + dxn
        _accumulate(dgn_ref, dg, i == 0)

    return _pcall(body, name=name, grid=(nt,),
                  in_specs=[_rows(tm, F), _next_halo(tm, hf, F, T), _rows(tm, F, col=1), _full(wg.shape), _full(wdw.shape),
                            _rows(tm, D), _full((1, D)), _rows(tm, D)],
                  out_specs=[_rows(tm, F), _rows(tm, D), _acc_spec(1, D)],
                  out_shape=[S_((T, 2 * F), BF16), S_((T, D), F32), S_((1, D), F32)],
                  scratch=[pltpu.VMEM((tm + hf, F), F32)],
                  aliases={2: 0}, args=(dac, dac, dup, wg, wdw, xin, gn, dx), comm=comm)


def _mm_tn(xt, dy, bk, bn, tt, shard_major, name, split=1):
    K, T = xt.shape
    N = dy.shape[1]
    nt = T // tt
    w = bn // split

    def body(x_ref, dy_ref, o_ref, acc):
        t = pl.program_id(2)
        part = _dot(x_ref[...], dy_ref[...].astype(BF16))
        _accumulate(acc, part, t == 0)

        @pl.when(t == nt - 1)
        def _():
            if shard_major:
                for p in range(split):
                    o_ref[p] = acc[:, p * w:(p + 1) * w].astype(o_ref.dtype)
            else:
                o_ref[...] = acc[...].astype(o_ref.dtype)

    if shard_major:
        out_spec = pl.BlockSpec((split, bk, w), lambda i, j, t: (j, i, 0))
        out_shape = S_((N // w, K, w), BF16)
    else:
        out_spec = pl.BlockSpec((bk, bn), lambda i, j, t: (i, j))
        out_shape = S_((K, N), BF16)
    return _pcall(body, name=name, grid=(K // bk, N // bn, nt),
                  in_specs=[pl.BlockSpec((bk, tt), lambda i, j, t: (i, t)), pl.BlockSpec((tt, bn), lambda i, j, t: (t, j))],
                  out_specs=out_spec, out_shape=out_shape, scratch=[pltpu.VMEM((bk, bn), F32)], args=(xt, dy))


def _pool_bwd(dx3, plb, x2, pw, pb, ps, gn, tm, tps):
    T, D = x2.shape
    ng, cg, _ = pw.shape
    hp = HALO_POOL
    nt = T // tm

    def body(do_ref, doh_ref, pl_ref, x_ref, pw_ref, pb_ref, ps_ref, gn_ref, dxo_ref, dpw_ref, sm_ref, qext, buf_a, buf_b,
             dh_s):
        i = pl.program_id(0)
        do = do_ref[...]
        dm = do * ps_ref[...]
        dmh = jnp.where(i % tps == tps - 1, 0.0, doh_ref[...]) * ps_ref[...]
        pv = pl_ref[...]
        mixed = jnp.concatenate([_dot(pv[:, g * cg:(g + 1) * cg], pw_ref[g]) for g in range(ng)], axis=1)
        dscale = _colsum(do * (mixed + pb_ref[...]))
        dbias = _colsum(dm)
        dmb = dm.astype(BF16)
        dmhb = dmh.astype(BF16)
        for g, w in enumerate(POOL_WINDOWS):
            cs = slice(g * cg, (g + 1) * cg)
            _accumulate(dpw_ref.at[g], _dot_tn(pv[:, cs], dmb[:, cs]), i == 0)
            dpo = _dot_nt(dmb[:, cs], pw_ref[g])
            dh_s[:, cs] = dpo
            qext[0:tm, cs] = dpo / _pool_counts(i, tps, tm, w)
            qext[tm:tm + hp, cs] = _dot_nt(dmhb[:, cs], pw_ref[g]) * (1.0 / w)
        sums = _window_sums(qext, buf_a, buf_b, cg, False)
        for g in range(ng):
            cs = slice(g * cg, (g + 1) * cg)
            dh_s[:, cs] = sums[g][0:tm, cs] - dh_s[:, cs]

        _, xh, r = _rms(x_ref[...], gn_ref[...])
        dxn, dg = _rms_bwd(dh_s[...], xh, r, gn_ref[...])
        dxo_ref[...] = do + dxn
        sm = jnp.concatenate([dscale, dbias, dg, jnp.zeros((5, D), F32)], axis=0)
        _accumulate(sm_ref, sm, i == 0)

    return _pcall(body, name="pool_bwd", grid=(nt,),
                  in_specs=[_rows(tm, D), _next_halo(tm, hp, D, T), _rows(tm, D), _rows(tm, D), _full(pw.shape), _full((1, D)),
                            _full((1, D)), _full((1, D))],
                  out_specs=[_rows(tm, D), pl.BlockSpec((ng, cg, cg), lambda i: (0, 0, 0)), _acc_spec(8, D)],
                  out_shape=[S_((T, D), F32), S_((ng, cg, cg), F32), S_((8, D), F32)],
                  scratch=[pltpu.VMEM((tm + hp, D), F32)] * 3 + [pltpu.VMEM((tm, D), F32)],
                  args=(dx3, dx3, plb, x2, pw, pb, ps, gn))


def _conv_bwd1(dx1, c, w2, lng, lnb, tm):
    T, D = dx1.shape

    def body(do_ref, c_ref, w2_ref, lng_ref, lnb_ref, dc_ref, sm_ref):
        i = pl.program_id(0)
        do = do_ref[...]
        ds = _dot_nt(do.astype(BF16), w2_ref[...])
        cv = c_ref[...]
        xc = cv - jnp.mean(cv, axis=-1, keepdims=True)
        rstd = lax.rsqrt(jnp.mean(xc * xc, axis=-1, keepdims=True) + LN_EPS)
        nh = xc * rstd
        n = nh * lng_ref[...] + lnb_ref[...]
        sg = _sig(n)
        dn = ds * (sg * (1.0 + n * (1.0 - sg)))
        dnh = dn * lng_ref[...]
        dc = rstd * (dnh - jnp.mean(dnh, axis=-1, keepdims=True) - nh * jnp.mean(dnh * nh, axis=-1, keepdims=True))
        dc_ref[...] = dc
        sm = jnp.concatenate([_colsum(dn * nh), _colsum(dn), _colsum(do), _colsum(dc), jnp.zeros((4, D), F32)], axis=0)
        _accumulate(sm_ref, sm, i == 0)

    return _pcall(body, name="conv_bwd1", grid=(T // tm,),
                  in_specs=[_rows(tm, D), _rows(tm, D), _full((D, D)), _full((1, D)), _full((1, D))],
                  out_specs=[_rows(tm, D), _acc_spec(8, D)],
                  out_shape=[S_((T, D), F32), S_((8, D), F32)], args=(dx1, c, w2, lng, lnb))


def _conv_bwd2(dc, a, x, w1g, wdw, gn, dx1, tm, tps, taps, comm=None):
    T, D = x.shape
    ns, _, nc = w1g.shape
    hc = HALO_CONV
    nt = T // tm
    tp = wdw.shape[0]

    def body(dc_ref, dch_ref, a_ref, ah_ref, x_ref, w_ref, wdw_ref, gn_ref, dx1_ref,
             gx_ref, dab_ref, dw_ref, db1_ref, dgn_ref, uext, dext, shifted, du_s, wacc):
        i = pl.program_id(0)
        ah = ah_ref[...]
        uext[0:hc, :] = jnp.where(i % tps == 0, 0.0, ah[:, :D] * _sig(ah[:, D:]))
        av = a_ref[...]
        sg2 = _sig(av[:, D:])
        uext[hc:hc + tm, :] = av[:, :D] * sg2
        dext[0:tm, :] = dc_ref[...]
        dext[tm:tm + hc, :] = jnp.where(i % tps == tps - 1, 0.0, dch_ref[...])

        @pl.when(i == 0)
        def _():
            wacc[...] = jnp.zeros_like(wacc)

        def wgrad(r0, src, ks, first):
            d = dc_ref[pl.ds(r0, CHUNK), :]
            for k, o in ks:
                wacc[8 * k:8 * k + 8, :] += _fold8(d * src[pl.ds(pl.multiple_of(r0 + o, 8), CHUNK), :])
        _by_sublane_shift(uext, shifted, tm, [hc - (taps - 1) + k for k in range(taps)], wgrad)

        def convt(r0, src, ks, first):
            acc = jnp.zeros((CHUNK, D), F32) if first else du_s[pl.ds(r0, CHUNK), :]
            for k, o in ks:
                acc = acc + wdw_ref[k:k + 1, :] * src[pl.ds(pl.multiple_of(r0 + o, 8), CHUNK), :]
            du_s[pl.ds(r0, CHUNK), :] = acc
        _by_sublane_shift(dext, shifted, tm, [taps - 1 - k for k in range(taps)], convt)

        du = du_s[...]
        da1 = du * sg2
        da2 = du * av[:, :D] * (sg2 * (1.0 - sg2))
        dab_ref[:, 0:D] = da1.astype(BF16)
        dab_ref[:, D:2 * D] = da2.astype(BF16)
        _accumulate(db1_ref, jnp.concatenate([_colsum(da1), _colsum(da2)], axis=0), i == 0)
        dh = jnp.zeros((tm, D), F32)
        for s in range(ns):
            dh = dh + _dot_nt(dab_ref[:, s * nc:(s + 1) * nc], w_ref[s])
        _, xh, r = _rms(x_ref[...], gn_ref[...])
        dxn, dg = _rms_bwd(dh, xh, r, gn_ref[...])
        gx_ref[...] = dx1_ref[...] + dxn
        _accumulate(dgn_ref, dg, i == 0)

        @pl.when(i == nt - 1)
        def _():
            dw_ref[...] = jnp.zeros_like(dw_ref)
            for k in range(taps):
                dw_ref[k:k + 1, :] = _colsum(wacc[8 * k:8 * k + 8, :])

    return _pcall(body, name="conv_bwd2", grid=(nt,),
                  in_specs=[_rows(tm, D), _next_halo(tm, hc, D, T), _rows(tm, 2 * D), _prev_halo(tm, hc, 2 * D), _rows(tm, D),
                            _full(w1g.shape), _full(wdw.shape), _full((1, D)), _rows(tm, D)],
                  out_specs=[_rows(tm, D), _rows(tm, 2 * D), _acc_spec(tp, D), _acc_spec(2, D), _acc_spec(1, D)],
                  out_shape=[S_((T, D), F32), S_((T, 2 * D), BF16), S_((tp, D), F32), S_((2, D), F32), S_((1, D), F32)],
                  scratch=[pltpu.VMEM((tm + hc, D), F32), pltpu.VMEM((tm + hc, D), F32), pltpu.VMEM((tm + hc - 8, D), F32),
                           pltpu.VMEM((tm, D), F32), pltpu.VMEM((8 * taps, D), F32)],
                  args=(dc, dc, a, a, x, w1g, wdw, gn, dx1), comm=comm)


def _place():
    x, y, c = lax.axis_index("x"), lax.axis_index("y"), lax.axis_index("c")
    chips = [(1 - x, y), (x, 1 - y), (1 - x, 1 - y)]
    return x, y, c, chips


def _own_slot(w, layer, ids, dtype, name, pairs=False):
    _, r, c = w.shape
    br = _row_block(r, c)

    def body(ids_ref, w_ref, o_ref):
        o_ref[...] = w_ref[...].astype(dtype)

    if pairs:
        out_spec = pl.BlockSpec((None, br, c), lambda i, ids: (ids[0] // 2, i, ids[0] % 2))
        out_shape = S_((N_CHIPS // 2, r, 2 * c), dtype)
    else:
        out_spec = pl.BlockSpec((None, br, c), lambda i, ids: (ids[0], i, 0))
        out_shape = S_((N_CHIPS, r, c), dtype)
    return _pcall(body, name=name, grid=(r // br,), prefetch=1,
                  in_specs=[pl.BlockSpec((None, br, c), lambda i, ids: (layer, i, 0))],
                  out_specs=out_spec, out_shape=out_shape, args=(ids, w))


class _GatherComm:
    def __init__(self, slots):
        n = len(slots)
        self.inputs = list(slots)
        self.out_shape = [S_(s.shape, s.dtype) for s in slots]
        self.aliases = {i: i for i in range(n)}
        self.sems = [pltpu.SemaphoreType.DMA((n, 3))] * 4
        self.halves = [s.shape[1] // 2 for s in slots]
        self.paired = [s.shape[0] == N_CHIPS // 2 for s in slots]
        self.widths = [s.shape[2] // 2 if p else s.shape[2] for s, p in zip(slots, self.paired)]

    def _copies(self, outs, sems):
        s_ici, r_ici, s_d2d, r_d2d = sems
        x, y, c, chips = _place()
        peers = [(j, chip, 2 * chip[0] + chip[1]) for j, chip in enumerate(chips)]

        def block(i, chip, h):
            rows = pl.ds(pl.multiple_of(h * self.halves[i], 8), self.halves[i])
            if self.paired[i]:
                return outs[i].at[chip // 2, rows, pl.ds(pl.multiple_of((chip % 2) * self.widths[i], 128), self.widths[i])]
            return outs[i].at[chip, rows]

        def ici(i, j, chip, to):
            blk = block(i, chip, c)
            return pltpu.make_async_remote_copy(src_ref=blk, dst_ref=blk, send_sem=s_ici.at[i, j], recv_sem=r_ici.at[i, j],
                                                device_id=to, device_id_type=MESH)

        def d2d(i, j, chip, h):
            blk = block(i, chip, h)
            return pltpu.make_async_remote_copy(src_ref=blk, dst_ref=blk, send_sem=s_d2d.at[i, j], recv_sem=r_d2d.at[i, j],
                                                device_id=(x, y, 1 - c), device_id_type=MESH)
        return (x, y, c, 2 * x + y), peers, ici, d2d

    def before(self, cin, outs, sems, step, total):
        (x, y, c, q), peers, ici, d2d = self._copies(outs, sems)

        @pl.when(step == 0)
        def _():
            for i in range(len(outs)):
                for j, chip, _ in peers:
                    ici(i, j, q, (*chip, c)).start()

    def after(self, cin, outs, sems, step, total):
        (x, y, c, q), peers, ici, d2d = self._copies(outs, sems)

        @pl.when(step == max(total - 2, 0))
        def _():
            for i in range(len(outs)):
                for j, chip, qj in peers:
                    ici(i, j, qj, (x, y, c)).wait_recv()
                    d2d(i, j, qj, c).start()

        @pl.when(step == total - 1)
        def _():
            for i in range(len(outs)):
                for j, chip, qj in peers:
                    d2d(i, j, qj, 1 - c).wait_recv()
            for i in range(len(outs)):
                for j, chip, qj in peers:
                    ici(i, j, q, (*chip, c)).wait_send()
                    d2d(i, j, qj, c).wait_send()


class _ExchangeComm:
    def __init__(self, sums):
        n = len(sums)
        self.inputs = list(sums)
        self.out_shape = [S_((3,) + s.shape[1:], s.dtype) for s in sums]
        self.aliases = {}
        self.sems = [pltpu.SemaphoreType.DMA((n, 3))] * 2

    def _copies(self, srcs, outs, sems):
        s_sem, r_sem = sems
        x, y, c, chips = _place()
        return [pltpu.make_async_remote_copy(src_ref=srcs[i].at[2 * chip[0] + chip[1]], dst_ref=outs[i].at[j],
                                             send_sem=s_sem.at[i, j], recv_sem=r_sem.at[i, j],
                                             device_id=(*chip, c), device_id_type=MESH)
                for i in range(len(srcs)) for j, chip in enumerate(chips)]

    def before(self, srcs, outs, sems, step, total):
        @pl.when(step == 0)
        def _():
            for cp in self._copies(srcs, outs, sems):
                cp.start()

    def after(self, srcs, outs, sems, step, total):
        @pl.when(step == total - 1)
        def _():
            for cp in self._copies(srcs, outs, sems):
                cp.wait()


def _comm_only(comm, name):
    def body():
        pass
    outs = _pcall(body, name=name, grid=(1,), in_specs=[], out_specs=[], out_shape=[], args=[], comm=comm)
    return list(outs)


def _pair_exchange(parts, name):
    n = len(parts)

    def body(*refs):
        srcs, outs = refs[:n], refs[n:2 * n]
        s_sem, r_sem = refs[2 * n:]
        x, y, c, _ = _place()
        cps = []
        for i in range(n):
            for s in range(N_CHIPS):
                cp = pltpu.make_async_remote_copy(src_ref=srcs[i].at[s, 1 - c], dst_ref=outs[i].at[s], send_sem=s_sem.at[i, s],
                                                  recv_sem=r_sem.at[i, s], device_id=(x, y, 1 - c), device_id_type=MESH)
                cp.start()
                cps.append(cp)
        for cp in cps:
            cp.wait()

    any_spec = pl.BlockSpec(memory_space=pl.ANY)
    outs = pl.pallas_call(
        body, name=name, in_specs=[any_spec] * n, out_specs=[any_spec] * n,
        out_shape=[S_((N_CHIPS,) + p.shape[2:], p.dtype) for p in parts],
        scratch_shapes=[pltpu.SemaphoreType.DMA((n, N_CHIPS))] * 2,
    )(*parts)
    return list(outs)


class _ShareComm:
    def __init__(self, bufs):
        n = len(bufs)
        self.inputs = list(bufs)
        self.out_shape = [S_(b.shape, b.dtype) for b in bufs]
        self.aliases = {i: i for i in range(n)}
        self.sems = [pltpu.SemaphoreType.DMA((n, max(b.shape[0] for b in bufs)))] * 2
        self.layers = [b.shape[0] for b in bufs]

    def _copies(self, outs, sems):
        s_sem, r_sem = sems
        x, y, c, _ = _place()
        cps = []
        for i, nl in enumerate(self.layers):
            for l in range(nl):
                blk = outs[i].at[l, c]
                cps.append(pltpu.make_async_remote_copy(src_ref=blk, dst_ref=blk, send_sem=s_sem.at[i, l],
                                                        recv_sem=r_sem.at[i, l], device_id=(x, y, 1 - c),
                                                        device_id_type=MESH))
        return cps

    def before(self, cin, outs, sems, step, total):
        @pl.when(step == 0)
        def _():
            for cp in self._copies(outs, sems):
                cp.start()

    def after(self, cin, outs, sems, step, total):
        @pl.when(step == total - 1)
        def _():
            for cp in self._copies(outs, sems):
                cp.wait()


class _Both:
    def __init__(self, a, b):
        self.parts = (a, b)
        self.inputs = a.inputs + b.inputs
        self.out_shape = a.out_shape + b.out_shape
        self.aliases = dict(a.aliases)
        self.aliases.update({len(a.inputs) + i: len(a.out_shape) + o for i, o in b.aliases.items()})
        self.sems = a.sems + b.sems

    def _each(self, phase, cin, cout, sems, step, total):
        i = o = s = 0
        for p in self.parts:
            ni, no, ns = len(p.inputs), len(p.out_shape), len(p.sems)
            getattr(p, phase)(cin[i:i + ni], cout[o:o + no], sems[s:s + ns], step, total)
            i, o, s = i + ni, o + no, s + ns

    def before(self, cin, cout, sems, step, total):
        self._each("before", cin, cout, sems, step, total)

    def after(self, cin, cout, sems, step, total):
        self._each("after", cin, cout, sems, step, total)


def _small_allreduce(groups):
    n = len(groups)
    arrays = []
    for grp in groups:
        for p in grp:
            if not isinstance(p, int) and not any(p[0] is a for a in arrays):
                arrays.append(p[0])
    shapes = []
    for grp in groups:
        rows = sum(p if isinstance(p, int) else p[2] - p[1] for p in grp)
        width = next(p[0].shape[1] for p in grp if not isinstance(p, int))
        assert rows % 8 == 0
        shapes.append((rows, width))
    na = len(arrays)

    def body(*refs):
        a_refs, o_refs = refs[:na], refs[na:na + n]
        rest = refs[na + n:]
        v_refs, sibs, css, gots = rest[:n], rest[n:2 * n], rest[2 * n:3 * n], rest[3 * n:4 * n]
        s_sem, r_sem = rest[4 * n:]
        for i, grp in enumerate(groups):
            o = 0
            for p in grp:
                if isinstance(p, int):
                    v_refs[i][o:o + p, :] = jnp.zeros((p, shapes[i][1]), F32)
                    o += p
                else:
                    k = next(t for t, a in enumerate(arrays) if a is p[0])
                    v_refs[i][o:o + p[2] - p[1], :] = a_refs[k][p[1]:p[2], :]
                    o += p[2] - p[1]
        x, y, c, chips = _place()
        q = 2 * x + y
        to_sib = [pltpu.make_async_remote_copy(src_ref=v_refs[i], dst_ref=sibs[i], send_sem=s_sem.at[i, 3], recv_sem=r_sem.at[i, 3],
                                               device_id=(x, y, 1 - c), device_id_type=MESH) for i in range(n)]
        for cp in to_sib:
            cp.start()
        cps = []
        for i in range(n):
            to_sib[i].wait()
            mine, other = v_refs[i][...], sibs[i][...]
            css[i][...] = jnp.where(c == 0, mine, other) + jnp.where(c == 0, other, mine)
            for j, chip in enumerate(chips):
                cp = pltpu.make_async_remote_copy(src_ref=css[i], dst_ref=gots[i].at[j], send_sem=s_sem.at[i, j],
                                                  recv_sem=r_sem.at[i, j], device_id=(*chip, c), device_id_type=MESH)
                cp.start()
                cps.append(cp)
        for cp in cps:
            cp.wait()
        flips = [2, 1, 3]
        for i in range(n):
            total = None
            for k in range(N_CHIPS):
                d = q ^ k
                term = jnp.where(d == 0, css[i][...], 0.0)
                for j in range(3):
                    term = jnp.where(d == flips[j], gots[i][j], term)
                total = term if total is None else total + term
            o_refs[i][...] = total

    vm = pl.BlockSpec(memory_space=pltpu.VMEM)
    return pl.pallas_call(
        body, name="small_allreduce", in_specs=[vm] * na, out_specs=[vm] * n, out_shape=[S_(s, F32) for s in shapes],
        scratch_shapes=[pltpu.VMEM(s, F32) for s in shapes] * 3 + [pltpu.VMEM((3,) + s, F32) for s in shapes]
        + [pltpu.SemaphoreType.DMA((n, 4)), pltpu.SemaphoreType.DMA((n, 4))],
    )(*arrays)


def _row_block(r, c, target_bytes=1 << 20):
    br = r
    while br % 32 == 0 and br * c * 4 > target_bytes:
        br //= 2
    return br


def _chip_sum(part, sib, ids, name):
    _, _, r, c = part.shape
    br = _row_block(r, c, 1 << 22)

    def body(ids_ref, p_ref, s_ref, o_ref):
        o_ref[...] = (p_ref[...].astype(F32) + s_ref[...].astype(F32)).astype(BF16)

    return _pcall(body, name=name, grid=(N_CHIPS, r // br), prefetch=1,
                  in_specs=[pl.BlockSpec((None, None, br, c), lambda s, i, ids: (s, ids[1], i, 0)),
                            pl.BlockSpec((None, br, c), lambda s, i, ids: (s, i, 0))],
                  out_specs=pl.BlockSpec((None, br, c), lambda s, i, ids: (s, i, 0)),
                  out_shape=S_((N_CHIPS, r, c), BF16), args=(ids, part, sib))


def _final_sum(part, sib, got, ids, name, layer=0, n_layers=1, buf=None):
    _, r, c = got.shape
    br = _row_block(r, c, 1 << 21)

    def body(ids_ref, p_ref, s_ref, got_ref, *rest):
        v = p_ref[...].astype(F32) + s_ref[...].astype(F32)
        for j in range(3):
            v = v + got_ref[j].astype(F32)
        rest[-1][...] = v

    in_specs = [pl.BlockSpec((None, None, br, c), lambda i, ids: (ids[0], ids[1], i, 0)),
                pl.BlockSpec((None, br, c), lambda i, ids: (ids[0], i, 0)),
                pl.BlockSpec((3, br, c), lambda i, ids: (0, i, 0))]
    args = [ids, part, sib, got]
    if buf is not None:
        in_specs.append(pl.BlockSpec(memory_space=pl.ANY))
        args.append(buf)
    return _pcall(body, name=name, grid=(r // br,), prefetch=1, in_specs=in_specs,
                  out_specs=pl.BlockSpec((None, None, br, c), lambda i, ids: (layer, ids[1], i, 0)),
                  out_shape=S_((n_layers, 2, r, c), F32), args=args, aliases={4: 0} if buf is not None else None)


def _adam_math(w, g, m, v):
    c1 = 1.0 / (1.0 - ADAM_B1 ** ADAM_STEP)
    c2 = 1.0 / (1.0 - ADAM_B2 ** ADAM_STEP)
    mn = ADAM_B1 * m + (1.0 - ADAM_B1) * g
    vn = ADAM_B2 * v + (1.0 - ADAM_B2) * (g * g)
    return -ADAM_LR * ((mn * c1) / (jnp.sqrt(vn * c2) + ADAM_EPS) + ADAM_WD * w), mn, vn


def _adamw(w, g, m, v, name):
    nl, r, c = w.shape
    br = _row_block(r, c, 1 << 20)

    def body(w_ref, g_ref, m_ref, v_ref, d_ref, mo_ref, vo_ref):
        d_ref[...], mo_ref[...], vo_ref[...] = _adam_math(w_ref[...], g_ref[...], m_ref[...], v_ref[...])

    blk = pl.BlockSpec((None, br, c), lambda l, i: (l, i, 0))
    return _pcall(body, name=name, grid=(nl, r // br), in_specs=[blk] * 4, out_specs=[blk] * 3,
                  out_shape=[S_((nl, r, c), F32)] * 3, args=(w, g, m, v))


def _adamw_small(rd, rf, sharded_g, ws, ms, vs, rows_d, rows_f, loss_row):
    n = len(ws)
    ns = len(sharded_g)
    rep = [k for k in range(n) if rows_d[k] is not None or rows_f[k] is not None]

    def body(*refs):
        rd_ref, rf_ref = refs[0], refs[1]
        sg = refs[2:2 + ns]
        w_refs, m_refs, v_refs = (refs[2 + ns + t * n:2 + ns + (t + 1) * n] for t in range(3))
        loss_ref, outs = refs[2 + ns + 3 * n], refs[3 + ns + 3 * n:]
        g_out, d_out, m_out, v_out = outs[:len(rep)], outs[len(rep):len(rep) + n], outs[len(rep) + n:len(rep) + 2 * n], \
            outs[len(rep) + 2 * n:]
        loss_ref[...] = jnp.sum(rd_ref[loss_row:loss_row + 1, :], axis=1, keepdims=True)
        si = 0
        for k in range(n):
            shape = w_refs[k].shape
            if rows_d[k] is not None or rows_f[k] is not None:
                src, rws = (rd_ref, rows_d[k]) if rows_d[k] is not None else (rf_ref, rows_f[k])
                axis = 0 if shape[0] == len(rws) else 1
                g = src[rws[0]:rws[0] + 1, :]
                if len(rws) > 1:
                    g = jnp.concatenate([src[r:r + 1, :] for r in rws], axis=axis)
                g_out[rep.index(k)][...] = g
            else:
                g = sg[si][...]
                si += 1
            d_out[k][...], m_out[k][...], v_out[k][...] = _adam_math(w_refs[k][...], g, m_refs[k][...], v_refs[k][...])

    vm = pl.BlockSpec(memory_space=pltpu.VMEM)
    shapes = [S_(w.shape, F32) for w in ws]
    outs = pl.pallas_call(
        body, name="adamw_small", in_specs=[vm] * (2 + ns + 3 * n), out_specs=[vm] * (1 + len(rep) + 3 * n),
        out_shape=[S_((1, 1), F32)] + [shapes[k] for k in rep] + shapes * 3,
    )(rd, rf, *sharded_g, *ws, *ms, *vs)
    g_rep = dict(zip(rep, outs[1:1 + len(rep)]))
    o = outs[1 + len(rep):]
    return outs[0], g_rep, o[:n], o[n:2 * n], o[2 * n:]


def _pack(pieces, rows):
    flat = jnp.concatenate([p.reshape(-1).astype(F32) for p in pieces])
    return jnp.pad(flat, (0, rows * 128 - flat.shape[0])).reshape(rows, 128)


def _unpack(packed, shapes):
    flat = packed.reshape(-1)
    out, o = [], 0
    for s in shapes:
        n = 1
        for d in s:
            n *= d
        out.append(flat[o:o + n].reshape(s))
        o += n
    return out


def _rows_for(pieces_or_shapes):
    n = 0
    for p in pieces_or_shapes:
        k = 1
        for d in (p if isinstance(p, tuple) else p.shape):
            k *= d
        n += k
    return -(-n // 1024) * 8


def kernel(x, norm_mix, norm_ffn, conv_w_pw1, conv_b_pw1, conv_w_dw, conv_b_dw, conv_ln_g, conv_ln_b, conv_w_pw2, conv_b_pw2, pool_w, pool_b, pool_scale, ffn_w_up, ffn_w_dw, ffn_b_dw, ffn_w_down, final_norm, loss_target, m_norm_mix, m_norm_ffn, m_conv_w_pw1, m_conv_b_pw1, m_conv_w_dw, m_conv_b_dw, m_conv_ln_g, m_conv_ln_b, m_conv_w_pw2, m_conv_b_pw2, m_pool_w, m_pool_b, m_pool_scale, m_ffn_w_up, m_ffn_w_dw, m_ffn_b_dw, m_ffn_w_down, m_final_norm, v_norm_mix, v_norm_ffn, v_conv_w_pw1, v_conv_b_pw1, v_conv_w_dw, v_conv_b_dw, v_conv_ln_g, v_conv_ln_b, v_conv_w_pw2, v_conv_b_pw2, v_pool_w, v_pool_b, v_pool_scale, v_ffn_w_up, v_ffn_w_dw, v_ffn_b_dw, v_ffn_w_down, v_final_norm):
    nb, seq, D = x.shape
    T = nb * seq
    F = ffn_w_down.shape[1] * N_CHIPS
    taps = conv_w_dw.shape[1]
    ffn_taps = ffn_w_dw.shape[1]
    ng = pool_w.shape[1]
    cg = pool_w.shape[3]
    dsh = D // N_CHIPS
    fsh = F // N_CHIPS
    assert taps - 1 <= HALO_CONV and ffn_taps - 1 <= HALO_FFN and max(POOL_WINDOWS) <= HALO_POOL
    tm = min(TILE_D, seq)
    tf = min(TILE_F, seq)
    assert seq % tm == 0 and seq % tf == 0 and tm % HALO_CONV == 0 and tf % CHUNK == 0
    tps, tpf = seq // tm, seq // tf
    tw = min(TILE_MM, T)
    assert T % tw == 0

    xi, yi, ci = lax.axis_index("x"), lax.axis_index("y"), lax.axis_index("c")
    qi = 2 * xi + yi
    ids = jnp.stack([qi, ci]).astype(jnp.int32)
    x2d = x.reshape(T, D)
    tg2d = loss_target.reshape(T, D)

    small_sharded = [conv_w_dw[0], ffn_w_dw, pool_b, pool_scale]
    srows = 2 * _rows_for(small_sharded)
    s_pw1, s_pw2 = _own_slot(conv_w_pw1, 0, ids, BF16, "slot_pw1"), _own_slot(conv_w_pw2, 0, ids, BF16, "slot_pw2")
    s_small = _own_slot(_pack(small_sharded, srows)[None], 0, ids, F32, "slot_small")
    s_pool = _own_slot(pool_w.reshape(1, ng * (cg // N_CHIPS), cg), 0, ids, BF16, "slot_pool")
    s_up = [_own_slot(ffn_w_up, l, ids, BF16, "slot_up%d" % l, pairs=True) for l in range(2)]
    s_down = [_own_slot(ffn_w_down, l, ids, BF16, "slot_down%d" % l) for l in range(2)]
    (w1g,) = _comm_only(_GatherComm([s_pw1]), "gather_pw1")
    hb0t, a, w2g, smallg = _conv_in(x2d, norm_mix[0:1], w1g, conv_b_pw1, tm, comm=_GatherComm([s_pw2, s_small]))
    w2f = w2g.reshape(D, D)
    sm_parts = [_unpack(smallg[s], [(taps, dsh), (2, ffn_taps, fsh), (1, dsh), (1, dsh)]) for s in range(N_CHIPS)]
    wdw_f = jnp.concatenate([p[0] for p in sm_parts], axis=1)
    wdw_f = jnp.pad(wdw_f, ((0, HALO_CONV - taps), (0, 0)))
    fdw_f = jnp.concatenate([p[1] for p in sm_parts], axis=2)
    fdw_f = jnp.pad(fdw_f, ((0, 0), (0, 8 - ffn_taps), (0, 0)))
    pb_f = jnp.concatenate([p[2] for p in sm_parts], axis=1)
    ps_f = jnp.concatenate([p[3] for p in sm_parts], axis=1)

    c, sbt, x1, hb1, hb1t, wup0, wd0, pwg = _conv_mid(a, x2d, wdw_f, conv_b_dw, conv_ln_g, conv_ln_b, w2f, conv_b_pw2,
                                                      norm_ffn[0:1], tm, tps, taps,
                                                      comm=_GatherComm([s_up[0], s_down[0], s_pool]))
    up0, gb0t, x2, h2, wup1, wd1 = _ffn_fwd(hb1, x1, wup0, fdw_f[0], ffn_b_dw[0:1], wd0.reshape(F, D), norm_mix[1:2], tf, tpf,
                                            ffn_taps, "ffn_fwd0", comm=_GatherComm([s_up[1], s_down[1]]))
    wup = [wup0, wup1]
    wdn = [wd0.reshape(F, D), wd1.reshape(F, D)]
    pwf = pwg.reshape(N_CHIPS, ng, cg // N_CHIPS, cg).transpose(1, 0, 2, 3).reshape(ng, cg, cg)
    plb, x3, hb3, hb3t = _pool_fwd(h2, x2, pwf, pb_f, ps_f, norm_ffn[1:2], tm, tps)
    up1, gb1t, dx4, loss_part, d_final = _ffn_fwd(hb3, x3, wup[1], fdw_f[1], ffn_b_dw[1:2], wdn[1], final_norm.reshape(1, D),
                                                  tf, tpf, ffn_taps, "ffn_fwd1", target=tg2d)

    fs = {}

    def chip_sums(group, tag):
        nms = list(group)
        from_sib = _pair_exchange([group[nm] for nm in nms], "grad_pair_exchange_" + tag)
        sums = [_chip_sum(group[nm], s, ids, "chip_sum_" + nm) for nm, s in zip(nms, from_sib)]
        for nm, s in zip(nms, from_sib):
            fs[nm] = (group[nm], s)
        return nms, _ExchangeComm(sums)

    def arrived(nms, got):
        for nm, g in zip(nms, got):
            fs[nm] = fs[nm] + (g,)

    dac1, dup1, fw1 = _bwd_down(dx4, up1, fdw_f[1], ffn_b_dw[1:2], wdn[1], tf, tpf, ffn_taps, "bwd_down1")
    p_down1 = _mm_tn(gb1t, dx4, F // 2, D, tw, False, "dw_down1")
    nms, xchg = chip_sums({"down1": p_down1.reshape(N_CHIPS, 2, fsh // 2, D)}, "down1")
    dup1, dx3, dnf1, *got = _bwd_up(dac1, dup1, wup[1], fdw_f[1], x3, norm_ffn[1:2], dx4, tf, tpf, ffn_taps,
                                         "bwd_up1", comm=xchg)
    arrived(nms, got)
    p_up1 = _mm_tn(hb3t, dup1, D // 2, 4 * fsh, tw, True, "dw_up1", split=2)
    nms, xchg = chip_sums({"up1": p_up1.reshape(N_CHIPS, 2, D // 2, 2 * fsh)}, "up1")
    dx2, dpw, pool_sm = _pool_bwd(dx3, plb, x2, pwf, pb_f, ps_f, norm_mix[1:2], tm, tps)
    dac0, dup0, fw0, *got = _bwd_down(dx2, up0, fdw_f[0], ffn_b_dw[0:1], wdn[0], tf, tpf, ffn_taps, "bwd_down0", comm=xchg)
    arrived(nms, got)
    p_down0 = _mm_tn(gb0t, dx2, F // 2, D, tw, False, "dw_down0")
    nms, xchg = chip_sums({"down0": p_down0.reshape(N_CHIPS, 2, fsh // 2, D)}, "down0")
    dup0, dx1, dnf0, *got = _bwd_up(dac0, dup0, wup[0], fdw_f[0], x1, norm_ffn[0:1], dx2, tf, tpf, ffn_taps,
                                         "bwd_up0", comm=xchg)
    arrived(nms, got)
    p_up0 = _mm_tn(hb1t, dup0, D // 2, 4 * fsh, tw, True, "dw_up0", split=2)
    dc, conv_sm = _conv_bwd1(dx1, c, w2f, conv_ln_g, conv_ln_b, tm)
    p_pw2 = _mm_tn(sbt, dx1, D, D, tw, False, "dw_pw2")
    csh = cg // N_CHIPS
    p_pool = dpw.reshape(2, ng // 2, N_CHIPS, csh, cg).transpose(2, 0, 1, 3, 4).reshape(N_CHIPS, 2, (ng // 2) * csh, cg)
    nms, xchg = chip_sums({"up0": p_up0.reshape(N_CHIPS, 2, D // 2, 2 * fsh), "pool": p_pool.astype(BF16),
                           "pw2": p_pw2.reshape(N_CHIPS, 2, dsh // 2, D)}, "mix")
    grad_x, dab, dwdw, db1, dnm0, *got = _conv_bwd2(dc, a, x2d, w1g, wdw_f, norm_mix[0:1], dx1, tm, tps, taps, comm=xchg)
    arrived(nms, got)
    p_pw1 = _mm_tn(hb0t, dab, D, 2 * dsh, tw, True, "dw_pw1")
    nms, xchg = chip_sums({"pw1": p_pw1.reshape(N_CHIPS, 2, D // 2, 2 * dsh)}, "pw1")
    bufs = [_final_sum(*fs[nm], ids, "final_sum_" + nm) for nm in ("pw2", "pool")]
    for nm in ("up", "down"):
        b = _final_sum(*fs[nm + "0"], ids, "final_sum_" + nm + "0", layer=0, n_layers=2)
        bufs.append(_final_sum(*fs[nm + "1"], ids, "final_sum_" + nm + "1", layer=1, n_layers=2, buf=b))
    *shared, got_pw1 = _comm_only(_Both(_ShareComm(bufs), xchg), "grad_share_halves")
    arrived(nms, [got_pw1])
    shared += _comm_only(_ShareComm([_final_sum(*fs["pw1"], ids, "final_sum_pw1")]), "grad_share_halves_pw1")
    g_pw2, g_pool, g_up, g_down, g_pw1 = [b.reshape(b.shape[0], 2 * b.shape[2], b.shape[3]) for b in shared]

    rd, rf = _small_allreduce([
        [(dnm0, 0, 1), (pool_sm, 2, 3), (dnf0, 0, 1), (dnf1, 0, 1), (db1, 0, 2), (conv_sm, 0, 4), (d_final, 0, 1),
         (pool_sm, 0, 2), (loss_part, 0, 1), 2, (dwdw, 0, dwdw.shape[0])],
        [(fw0, 0, 4), (fw1, 0, 4)]])
    g_wdw = lax.dynamic_slice_in_dim(rd[16:16 + taps], qi * dsh, dsh, axis=1)
    g_ps = lax.dynamic_slice_in_dim(rd[11:12], qi * dsh, dsh, axis=1)
    g_pb = lax.dynamic_slice_in_dim(rd[12:13], qi * dsh, dsh, axis=1)
    g_fdw = lax.dynamic_slice_in_dim(rf.reshape(2, 4, F)[:, :ffn_taps], qi * fsh, fsh, axis=2).reshape(2 * ffn_taps, fsh)

    def big(w, g, m, v, nm):
        d, mn, vn = _adamw(w.reshape(g.shape), g, m.reshape(g.shape), v.reshape(g.shape), "adamw_" + nm)
        return g.reshape(w.shape), d.reshape(w.shape), mn.reshape(w.shape), vn.reshape(w.shape)

    o_pw1 = big(conv_w_pw1, g_pw1, m_conv_w_pw1, v_conv_w_pw1, "pw1")
    o_pw2 = big(conv_w_pw2, g_pw2, m_conv_w_pw2, v_conv_w_pw2, "pw2")
    o_pool = big(pool_w, g_pool, m_pool_w, v_pool_w, "pool")
    o_up = big(ffn_w_up, g_up, m_ffn_w_up, v_ffn_w_up, "up")
    o_down = big(ffn_w_down, g_down, m_ffn_w_down, v_ffn_w_down, "down")

    snames = ["norm_mix", "norm_ffn", "b_pw1", "w_dw", "b_dw", "ln_g", "ln_b", "b_pw2", "pool_b", "pool_scale", "ffn_w_dw",
              "ffn_b_dw", "final_norm"]
    sw = [norm_mix, norm_ffn, conv_b_pw1, conv_w_dw, conv_b_dw, conv_ln_g, conv_ln_b, conv_b_pw2, pool_b, pool_scale,
          ffn_w_dw, ffn_b_dw, final_norm]
    smm = [m_norm_mix, m_norm_ffn, m_conv_b_pw1, m_conv_w_dw, m_conv_b_dw, m_conv_ln_g, m_conv_ln_b, m_conv_b_pw2, m_pool_b,
           m_pool_scale, m_ffn_w_dw, m_ffn_b_dw, m_final_norm]
    svv = [v_norm_mix, v_norm_ffn, v_conv_b_pw1, v_conv_w_dw, v_conv_b_dw, v_conv_ln_g, v_conv_ln_b, v_conv_b_pw2, v_pool_b,
           v_pool_scale, v_ffn_w_dw, v_ffn_b_dw, v_final_norm]
    sshapes = [tuple(w.shape) for w in sw]
    two_d = [(2, D), (2, D), (1, 2 * D), (taps, dsh), (1, D), (1, D), (1, D), (1, D), (1, dsh), (1, dsh), (2 * ffn_taps, fsh),
             (2, F), (1, D)]
    rows_d = [[0, 1], [2, 3], [4, 5], None, [9], [6], [7], [8], None, None, None, None, [10]]
    rows_f = [None] * 11 + [[3, 7], None]
    as2d = lambda ts: [t.reshape(s) for t, s in zip(ts, two_d)]
    loss, g_rep, sd, sm_new, sv_new = _adamw_small(rd, rf, [g_wdw, g_pb, g_ps, g_fdw], as2d(sw), as2d(smm), as2d(svv), rows_d,
                                                   rows_f, loss_row=13)
    sg = dict(g_rep)
    sg.update({3: g_wdw, 8: g_pb, 9: g_ps, 10: g_fdw})
    small_out = {n: tuple(t.reshape(sshapes[k]) for t in (sg[k], sd[k], sm_new[k], sv_new[k])) for k, n in enumerate(snames)}

    order = [small_out["norm_mix"], small_out["norm_ffn"], o_pw1, small_out["b_pw1"], small_out["w_dw"], small_out["b_dw"],
             small_out["ln_g"], small_out["ln_b"], o_pw2, small_out["b_pw2"], o_pool, small_out["pool_b"],
             small_out["pool_scale"], o_up, small_out["ffn_w_dw"], small_out["ffn_b_dw"], o_down, small_out["final_norm"]]
    return (loss[0, 0], grad_x.reshape(nb, seq, D), *[o[0] for o in order], *[o[1] for o in order], *[o[2] for o in order],
            *[o[3] for o in order])
```

```python
import functools

import jax
import jax.numpy as jnp
from jax import lax
from jax.experimental import pallas as pl
from jax.experimental.pallas import tpu as pltpu

F32, BF16 = jnp.float32, jnp.bfloat16
S_ = jax.ShapeDtypeStruct
MESH = pl.DeviceIdType.MESH

RMS_EPS, LN_EPS = 1e-6, 1e-5
POOL_WINDOWS = (2, 4, 8, 16)
ADAM_LR, ADAM_B1, ADAM_B2, ADAM_EPS, ADAM_WD, ADAM_STEP = 0.001, 0.9, 0.999, 1e-08, 0.01, 10

VMEM_LIMIT_BYTES = 60 * 1024 * 1024
N_CHIPS = 4
CHUNK = 32
HALO_CONV = 32
HALO_POOL = 32
HALO_FFN = 8
TILE_D = 512
TILE_F = 256
TILE_MM = 1024


def _pcall(body, *, name, grid, in_specs, out_specs, out_shape, args, scratch=(), aliases=None, prefetch=0, comm=None):
    params = pltpu.CompilerParams(dimension_semantics=("arbitrary",) * len(grid), vmem_limit_bytes=VMEM_LIMIT_BYTES)
    if comm is not None:
        assert not prefetch
        single = not isinstance(out_shape, (list, tuple))
        in_specs, args, scratch = list(in_specs), list(args), list(scratch)
        out_specs, out_shape = ([out_specs], [out_shape]) if single else (list(out_specs), list(out_shape))
        n_in, n_out, n_scr, n_cin, n_cout = len(in_specs), len(out_specs), len(scratch), len(comm.inputs), len(comm.out_shape)
        any_spec = pl.BlockSpec(memory_space=pl.ANY)
        aliases = dict(aliases or {})
        aliases.update({n_in + a: n_out + b for a, b in comm.aliases.items()})
        inner = body

        def body(*refs):
            ins, cin = refs[:n_in], refs[n_in:n_in + n_cin]
            outs = refs[n_in + n_cin:n_in + n_cin + n_out]
            cout = refs[n_in + n_cin + n_out:n_in + n_cin + n_out + n_cout]
            scr = refs[n_in + n_cin + n_out + n_cout:n_in + n_cin + n_out + n_cout + n_scr]
            sems = refs[n_in + n_cin + n_out + n_cout + n_scr:]
            step, total = 0, 1
            for d, g in enumerate(grid):
                step = step * g + pl.program_id(d)
                total *= g
            comm.before(cin, cout, sems, step, total)
            inner(*ins, *outs, *scr)
            comm.after(cin, cout, sems, step, total)

        in_specs += [any_spec] * n_cin
        args += list(comm.inputs)
        out_specs += [any_spec] * n_cout
        out_shape += list(comm.out_shape)
        scratch += list(comm.sems)
    if prefetch:
        spec = pltpu.PrefetchScalarGridSpec(num_scalar_prefetch=prefetch, grid=grid, in_specs=in_specs,
                                            out_specs=out_specs, scratch_shapes=list(scratch))
        return pl.pallas_call(body, name=name, grid_spec=spec, out_shape=out_shape,
                              input_output_aliases=aliases or {}, compiler_params=params)(*args)
    return pl.pallas_call(body, name=name, grid=grid, in_specs=in_specs, out_specs=out_specs, out_shape=out_shape,
                          scratch_shapes=list(scratch), input_output_aliases=aliases or {},
                          compiler_params=params)(*args)


def _full(shape):
    n = len(shape)
    return pl.BlockSpec(tuple(shape), lambda *_: (0,) * n, pipeline_mode=pl.Buffered(1))


def _rows(tm, c, col=0):
    return pl.BlockSpec((tm, c), lambda i, *_: (i, col))


def _cols(r, tm):
    return pl.BlockSpec((r, tm), lambda i, *_: (0, i))


def _prev_halo(tm, hb, c):
    return pl.BlockSpec((hb, c), lambda i, *_: (jnp.maximum(i * (tm // hb) - 1, 0), 0))


def _next_halo(tm, hb, c, total_rows):
    last = total_rows // hb - 1
    return pl.BlockSpec((hb, c), lambda i, *_: (jnp.minimum((i + 1) * (tm // hb), last), 0))


def _acc_spec(r, c):
    return pl.BlockSpec((r, c), lambda *_: (0, 0))


def _rms(x, g):
    r = lax.rsqrt(jnp.mean(x * x, axis=-1, keepdims=True) + RMS_EPS)
    xh = x * r
    return xh * g, xh, r


def _rms_bwd(dy, xh, r, g):
    dxh = dy * g
    dx = r * (dxh - xh * jnp.mean(dxh * xh, axis=-1, keepdims=True))
    return dx, jnp.sum(dy * xh, axis=0, keepdims=True)


def _colsum(v):
    return jnp.sum(v, axis=0, keepdims=True)


def _fold8(v):
    out = v[0:8]
    for j in range(1, v.shape[0] // 8):
        out = out + v[8 * j:8 * j + 8]
    return out


def _chunks(n_rows, fn):
    def step(j, carry):
        fn(pl.multiple_of(j * CHUNK, CHUNK))
        return carry
    lax.fori_loop(0, n_rows // CHUNK, step, 0)


def _dot(a, b):
    return jnp.dot(a, b, preferred_element_type=F32)


def _dot_nt(a, b):
    return lax.dot_general(a, b, (((1,), (1,)), ((), ())), preferred_element_type=F32)


def _dot_tn(a, b):
    return lax.dot_general(a, b, (((0,), (0,)), ((), ())), preferred_element_type=F32)


def _sig(v):
    return jax.nn.sigmoid(v)


def _accumulate(ref, val, first):
    @pl.when(first)
    def _():
        ref[...] = val

    @pl.when(jnp.logical_not(first))
    def _():
        ref[...] = ref[...] + val


def _conv_in(x, g, w1g, b1, tm, comm=None):
    T, D = x.shape
    ns, _, nc = w1g.shape

    def body(x_ref, g_ref, w_ref, b_ref, hbt_ref, a_ref):
        hb = _rms(x_ref[...], g_ref[...])[0].astype(BF16)
        hbt_ref[...] = hb.T
        for s in range(ns):
            a_ref[:, s * nc:(s + 1) * nc] = _dot(hb, w_ref[s]) + b_ref[:, s * nc:(s + 1) * nc]

    return _pcall(body, name="conv_in", grid=(T // tm,),
                  in_specs=[_rows(tm, D), _full((1, D)), _full(w1g.shape), _full((1, ns * nc))],
                  out_specs=[_cols(D, tm), _rows(tm, ns * nc)],
                  out_shape=[S_((D, T), BF16), S_((T, ns * nc), F32)], args=(x, g, w1g, b1), comm=comm)


def _conv_mid(a, x, wdw, bdw, lng, lnb, w2, b2, gn, tm, tps, taps, comm=None):
    T, D = x.shape
    hc = HALO_CONV

    def body(a_ref, ah_ref, x_ref, wdw_ref, bdw_ref, lng_ref, lnb_ref, w2_ref, b2_ref, gn_ref,
             c_ref, sbt_ref, x1_ref, hb_ref, hbt_ref, uext, shifted):
        i = pl.program_id(0)
        ah = ah_ref[...]
        uext[0:hc, :] = jnp.where(i % tps == 0, 0.0, ah[:, :D] * _sig(ah[:, D:]))
        av = a_ref[...]
        uext[hc:hc + tm, :] = av[:, :D] * _sig(av[:, D:])

        def conv(r0, src, ks, first):
            acc = jnp.broadcast_to(bdw_ref[...], (CHUNK, D)) if first else c_ref[pl.ds(r0, CHUNK), :]
            for k, o in ks:
                acc = acc + wdw_ref[k:k + 1, :] * src[pl.ds(pl.multiple_of(r0 + o, 8), CHUNK), :]
            c_ref[pl.ds(r0, CHUNK), :] = acc
        _by_sublane_shift(uext, shifted, tm, [hc - (taps - 1) + k for k in range(taps)], conv)

        cv = c_ref[...]
        xc = cv - jnp.mean(cv, axis=-1, keepdims=True)
        nh = xc * lax.rsqrt(jnp.mean(xc * xc, axis=-1, keepdims=True) + LN_EPS)
        n = nh * lng_ref[...] + lnb_ref[...]
        sb = (n * _sig(n)).astype(BF16)
        sbt_ref[...] = sb.T
        x1 = x_ref[...] + _dot(sb, w2_ref[...]) + b2_ref[...]
        x1_ref[...] = x1
        hb = _rms(x1, gn_ref[...])[0].astype(BF16)
        hb_ref[...] = hb
        hbt_ref[...] = hb.T

    return _pcall(body, name="conv_mid", grid=(T // tm,),
                  in_specs=[_rows(tm, 2 * D), _prev_halo(tm, hc, 2 * D), _rows(tm, D), _full(wdw.shape), _full((1, D)),
                            _full((1, D)), _full((1, D)), _full((D, D)), _full((1, D)), _full((1, D))],
                  out_specs=[_rows(tm, D), _cols(D, tm), _rows(tm, D), _rows(tm, D), _cols(D, tm)],
                  out_shape=[S_((T, D), F32), S_((D, T), BF16), S_((T, D), F32), S_((T, D), BF16), S_((D, T), BF16)],
                  scratch=[pltpu.VMEM((tm + hc, D), F32), pltpu.VMEM((tm + hc - 8, D), F32)],
                  args=(a, a, x, wdw, bdw, lng, lnb, w2, b2, gn), comm=comm)


def _ffn_taps(aext, realigned, tm, ffn_taps):
    for k in range(ffn_taps - 1):
        realigned[k] = aext[pl.ds(HALO_FFN - (ffn_taps - 1) + k, tm), :]
    return [realigned.at[k] for k in range(ffn_taps - 1)] + [aext.at[pl.ds(HALO_FFN, tm)]]


def _ffn_conv(taps, wdw_ref, bdw_ref):
    acc = bdw_ref[...]
    for k, tap in enumerate(taps):
        acc = acc + wdw_ref[k:k + 1, :] * tap[...]
    return acc


def _by_sublane_shift(ext, shifted, tm, offsets, per_chunk):
    first = True
    for b in range(8):
        ks = [(k, o - b) for k, o in enumerate(offsets) if o % 8 == b]
        if not ks:
            continue
        if b:
            shifted[...] = ext[pl.ds(b, shifted.shape[0]), :]
        _chunks(tm, functools.partial(per_chunk, src=shifted if b else ext, ks=ks, first=first))
        first = False


def _ffn_fwd(hb, xin, wg, wdw, bdw, wd, gn, tm, tps, ffn_taps, name, target=None, comm=None):
    T, D = xin.shape
    ns, _, nc = wg.shape
    F = ns * nc // 2
    hf = HALO_FFN
    hh = 2 * HALO_FFN
    final = target is not None

    def body(*refs):
        if final:
            (h_ref, hh_ref, x_ref, wg_ref, wdw_ref, bdw_ref, wd_ref, gn_ref, tg_ref,
             up_ref, gb_ref, dx_ref, loss_ref, dgn_ref, aext, realigned) = refs
        else:
            (h_ref, hh_ref, x_ref, wg_ref, wdw_ref, bdw_ref, wd_ref, gn_ref,
             up_ref, gb_ref, xo_ref, ho_ref, aext, realigned) = refs
        i = pl.program_id(0)
        hv = h_ref[...]
        for s in range(ns):
            up_ref[:, s * nc:(s + 1) * nc] = _dot(hv, wg_ref[s])
        halo = jnp.concatenate([_dot(hh_ref[...], wg_ref[s]) for s in range(ns // 2)], axis=1)[hh - hf:hh, :]
        aext[0:hf, :] = jnp.where(i % tps == 0, 0.0, halo)
        aext[hf:hf + tm, :] = up_ref[:, 0:F]
        acv = _ffn_conv(_ffn_taps(aext, realigned, tm, ffn_taps), wdw_ref, bdw_ref)
        gb = (acv * _sig(acv) * up_ref[:, F:2 * F]).astype(BF16)
        gb_ref[...] = gb.T
        xo = x_ref[...] + _dot(gb, wd_ref[...])
        if not final:
            xo_ref[...] = xo
            ho_ref[...] = _rms(xo, gn_ref[...])[0]
        else:
            out, xh, r = _rms(xo, gn_ref[...])
            diff = out - tg_ref[...]
            part = 0.5 / D * _colsum(diff * diff)
            dx, dg = _rms_bwd(diff * (1.0 / D), xh, r, gn_ref[...])
            dx_ref[...] = dx
            _accumulate(loss_ref, part, i == 0)
            _accumulate(dgn_ref, dg, i == 0)

    in_specs = [_rows(tm, D), _prev_halo(tm, hh, D), _rows(tm, D), _full(wg.shape), _full(wdw.shape), _full((1, F)),
                _full((F, D)), _full((1, D))]
    args = [hb, hb, xin, wg, wdw, bdw, wd, gn]
    if final:
        in_specs.append(_rows(tm, D))
        args.append(target)
        out_specs = [_rows(tm, 2 * F), _cols(F, tm), _rows(tm, D), _acc_spec(1, D), _acc_spec(1, D)]
        out_shape = [S_((T, 2 * F), F32), S_((F, T), BF16), S_((T, D), F32), S_((1, D), F32), S_((1, D), F32)]
    else:
        out_specs = [_rows(tm, 2 * F), _cols(F, tm), _rows(tm, D), _rows(tm, D)]
        out_shape = [S_((T, 2 * F), F32), S_((F, T), BF16), S_((T, D), F32), S_((T, D), F32)]
    return _pcall(body, name=name, grid=(T // tm,), in_specs=in_specs, out_specs=out_specs, out_shape=out_shape,
                  scratch=[pltpu.VMEM((tm + hf, F), F32), pltpu.VMEM((ffn_taps - 1, tm, F), F32)], args=args, comm=comm)


def _pool_counts(i, tps, tm, w):
    pos = (i % tps) * tm + lax.broadcasted_iota(jnp.int32, (tm, 1), 0)
    return jnp.minimum(pos + 1, w).astype(F32)


def _window_sums(src, buf_a, buf_b, cg, causal):
    assert POOL_WINDOWS == (2, 4, 8, 16)
    n = src.shape[0]
    levels = len(POOL_WINDOWS)
    cur, outs = src, []
    for l in range(levels):
        dst = buf_b if l % 2 else buf_a
        cols = slice(l * cg, levels * cg)
        lo, m = 8 * (l + 1), n - 8 * (l + 1)
        if causal:
            dst[lo:n, cols] = cur[lo:n, cols] + cur[pl.ds(lo - (1 << l), m), cols]
        else:
            dst[0:m, cols] = cur[0:m, cols] + cur[pl.ds(1 << l, m), cols]
        outs.append(dst)
        cur = dst
    return outs


def _pool_fwd(h, x2, pw, pb, ps, gn, tm, tps, comm=None):
    T, D = h.shape
    ng, cg, _ = pw.shape
    hp = HALO_POOL

    def body(h_ref, hh_ref, x_ref, pw_ref, pb_ref, ps_ref, gn_ref, pl_ref, xo_ref, hb_ref, hbt_ref, hext, buf_a, buf_b):
        i = pl.program_id(0)
        hext[0:hp, :] = jnp.where(i % tps == 0, 0.0, hh_ref[...])
        hext[hp:hp + tm, :] = h_ref[...]
        sums = _window_sums(hext, buf_a, buf_b, cg, True)
        for g, w in enumerate(POOL_WINDOWS):
            cs = slice(g * cg, (g + 1) * cg)
            pl_ref[:, cs] = (sums[g][hp:hp + tm, cs] / _pool_counts(i, tps, tm, w) - h_ref[:, cs]).astype(BF16)

        pv = pl_ref[...]
        mixed = jnp.concatenate([_dot(pv[:, g * cg:(g + 1) * cg], pw_ref[g]) for g in range(ng)], axis=1)
        xo = x_ref[...] + ps_ref[...] * (mixed + pb_ref[...])
        xo_ref[...] = xo
        hb = _rms(xo, gn_ref[...])[0].astype(BF16)
        hb_ref[...] = hb
        hbt_ref[...] = hb.T

    return _pcall(body, name="pool_fwd", grid=(T // tm,),
                  in_specs=[_rows(tm, D), _prev_halo(tm, hp, D), _rows(tm, D), _full(pw.shape), _full((1, D)), _full((1, D)),
                            _full((1, D))],
                  out_specs=[_rows(tm, D), _rows(tm, D), _rows(tm, D), _cols(D, tm)],
                  out_shape=[S_((T, D), BF16), S_((T, D), F32), S_((T, D), BF16), S_((D, T), BF16)],
                  scratch=[pltpu.VMEM((tm + hp, D), F32)] * 3, args=(h, h, x2, pw, pb, ps, gn), comm=comm)


def _bwd_down(dx, up, wdw, bdw, wd, tm, tps, ffn_taps, name, comm=None):
    T, D = dx.shape
    F = up.shape[1] // 2
    hf = HALO_FFN
    nt = T // tm

    def body(dx_ref, up_ref, uph_ref, wdw_ref, bdw_ref, wd_ref, dac_ref, dgate_ref, wsum_ref, aext, realigned):
        i = pl.program_id(0)
        aext[0:hf, :] = jnp.where(i % tps == 0, 0.0, uph_ref[...])
        aext[hf:hf + tm, :] = up_ref[:, 0:F]
        taps = _ffn_taps(aext, realigned, tm, ffn_taps)
        acv = _ffn_conv(taps, wdw_ref, bdw_ref)
        dg = _dot_nt(dx_ref[...].astype(BF16), wd_ref[...])
        sg = _sig(acv)
        dgate_ref[...] = (dg * acv * sg).astype(BF16)
        dac = dg * up_ref[:, F:2 * F] * (sg * (1.0 + acv * (1.0 - sg)))
        dac_ref[...] = dac
        rows = [_colsum(dac * tap[...]) for tap in taps]
        rows += [_colsum(dac), jnp.zeros((8 - ffn_taps - 1, F), F32)]
        _accumulate(wsum_ref, jnp.concatenate(rows, axis=0), i == 0)

    return _pcall(body, name=name, grid=(nt,),
                  in_specs=[_rows(tm, D), _rows(tm, 2 * F), _prev_halo(tm, hf, F), _full(wdw.shape), _full((1, F)),
                            _full((F, D))],
                  out_specs=[_rows(tm, F), _rows(tm, F, col=1), _acc_spec(8, F)],
                  out_shape=[S_((T, F), F32), S_((T, 2 * F), BF16), S_((8, F), F32)],
                  scratch=[pltpu.VMEM((tm + hf, F), F32), pltpu.VMEM((ffn_taps - 1, tm, F), F32)],
                  args=(dx, up, up, wdw, bdw, wd), comm=comm)


def _bwd_up(dac, dup, wg, wdw, xin, gn, dx, tm, tps, ffn_taps, name, comm=None):
    T, D = xin.shape
    F = dac.shape[1]
    ns, _, nc = wg.shape
    hf = HALO_FFN
    nt = T // tm

    def body(dac_ref, dach_ref, dgate_ref, wg_ref, wdw_ref, x_ref, gn_ref, dx_ref, dact_ref, dxo_ref, dgn_ref, dext):
        i = pl.program_id(0)
        dext[0:tm, :] = dac_ref[...]
        dext[tm:tm + hf, :] = jnp.where(i % tps == tps - 1, 0.0, dach_ref[...])
        dact = wdw_ref[ffn_taps - 1:ffn_taps, :] * dac_ref[...]
        for k in range(ffn_taps - 1):
            dact = dact + wdw_ref[k:k + 1, :] * dext[pl.ds(ffn_taps - 1 - k, tm), :]
        dact_ref[...] = dact.astype(BF16)

        dh = jnp.zeros((tm, D), F32)
        for s in range(ns):
            src = dact_ref if s < ns // 2 else dgate_ref
            o = (s % (ns // 2)) * nc
            dh = dh + _dot_nt(src[:, o:o + nc], wg_ref[s])
        _, xh, r = _rms(x_ref[...], gn_ref[...])
        dxn, dg = _rms_bwd(dh, xh, r, gn_ref[...])
        dxo_ref[...] = dx_ref[...] + dxn
        _accumulate(dgn_ref, dg, i == 0)

    return _pcall(body, name=name, grid=(nt,),
                  in_specs=[_rows(tm, F), _next_halo(tm, hf, F, T), _rows(tm, F, col=1), _full(wg.shape), _full(wdw.shape),
                            _rows(tm, D), _full((1, D)), _rows(tm, D)],
                  out_specs=[_rows(tm, F), _rows(tm, D), _acc_spec(1, D)],
                  out_shape=[S_((T, 2 * F), BF16), S_((T, D), F32), S_((1, D), F32)],
                  scratch=[pltpu.VMEM((tm + hf, F), F32)],
                  aliases={2: 0}, args=(dac, dac, dup, wg, wdw, xin, gn, dx), comm=comm)


def _mm_tn(xt, dy, bk, bn, tt, shard_major, name, split=1, comm=None):
    K, T = xt.shape
    N = dy.shape[1]
    nt = T // tt
    w = bn // split

    def body(x_ref, dy_ref, o_ref, acc):
        t = pl.program_id(2)
        part = _dot(x_ref[...], dy_ref[...].astype(BF16))
        _accumulate(acc, part, t == 0)

        @pl.when(t == nt - 1)
        def _():
            if shard_major:
                for p in range(split):
                    o_ref[p] = acc[:, p * w:(p + 1) * w].astype(o_ref.dtype)
            else:
                o_ref[...] = acc[...].astype(o_ref.dtype)

    if shard_major:
        out_spec = pl.BlockSpec((split, bk, w), lambda i, j, t: (j, i, 0))
        out_shape = S_((N // w, K, w), BF16)
    else:
        out_spec = pl.BlockSpec((bk, bn), lambda i, j, t: (i, j))
        out_shape = S_((K, N), BF16)
    return _pcall(body, name=name, grid=(K // bk, N // bn, nt),
                  in_specs=[pl.BlockSpec((bk, tt), lambda i, j, t: (i, t)), pl.BlockSpec((tt, bn), lambda i, j, t: (t, j))],
                  out_specs=out_spec, out_shape=out_shape, scratch=[pltpu.VMEM((bk, bn), F32)], args=(xt, dy), comm=comm)


def _pool_bwd(dx3, plb, x2, pw, pb, ps, gn, tm, tps, comm=None):
    T, D = x2.shape
    ng, cg, _ = pw.shape
    hp = HALO_POOL
    nt = T // tm

    def body(do_ref, doh_ref, pl_ref, x_ref, pw_ref, pb_ref, ps_ref, gn_ref, dxo_ref, dpw_ref, sm_ref, qext, buf_a, buf_b,
             dh_s):
        i = pl.program_id(0)
        do = do_ref[...]
        dm = do * ps_ref[...]
        dmh = jnp.where(i % tps == tps - 1, 0.0, doh_ref[...]) * ps_ref[...]
        pv = pl_ref[...]
        mixed = jnp.concatenate([_dot(pv[:, g * cg:(g + 1) * cg], pw_ref[g]) for g in range(ng)], axis=1)
        dscale = _colsum(do * (mixed + pb_ref[...]))
        dbias = _colsum(dm)
        dmb = dm.astype(BF16)
        dmhb = dmh.astype(BF16)
        for g, w in enumerate(POOL_WINDOWS):
            cs = slice(g * cg, (g + 1) * cg)
            _accumulate(dpw_ref.at[g], _dot_tn(pv[:, cs], dmb[:, cs]), i == 0)
            dpo = _dot_nt(dmb[:, cs], pw_ref[g])
            dh_s[:, cs] = dpo
            qext[0:tm, cs] = dpo / _pool_counts(i, tps, tm, w)
            qext[tm:tm + hp, cs] = _dot_nt(dmhb[:, cs], pw_ref[g]) * (1.0 / w)
        sums = _window_sums(qext, buf_a, buf_b, cg, False)
        for g in range(ng):
            cs = slice(g * cg, (g + 1) * cg)
            dh_s[:, cs] = sums[g][0:tm, cs] - dh_s[:, cs]

        _, xh, r = _rms(x_ref[...], gn_ref[...])
        dxn, dg = _rms_bwd(dh_s[...], xh, r, gn_ref[...])
        dxo_ref[...] = do + dxn
        sm = jnp.concatenate([dscale, dbias, dg, jnp.zeros((5, D), F32)], axis=0)
        _accumulate(sm_ref, sm, i == 0)

    return _pcall(body, name="pool_bwd", grid=(nt,),
                  in_specs=[_rows(tm, D), _next_halo(tm, hp, D, T), _rows(tm, D), _rows(tm, D), _full(pw.shape), _full((1, D)),
                            _full((1, D)), _full((1, D))],
                  out_specs=[_rows(tm, D), pl.BlockSpec((ng, cg, cg), lambda i: (0, 0, 0)), _acc_spec(8, D)],
                  out_shape=[S_((T, D), F32), S_((ng, cg, cg), F32), S_((8, D), F32)],
                  scratch=[pltpu.VMEM((tm + hp, D), F32)] * 3 + [pltpu.VMEM((tm, D), F32)],
                  args=(dx3, dx3, plb, x2, pw, pb, ps, gn), comm=comm)


def _conv_bwd1(dx1, c, w2, lng, lnb, tm, comm=None):
    T, D = dx1.shape

    def body(do_ref, c_ref, w2_ref, lng_ref, lnb_ref, dc_ref, sm_ref):
        i = pl.program_id(0)
        do = do_ref[...]
        ds = _dot_nt(do.astype(BF16), w2_ref[...])
        cv = c_ref[...]
        xc = cv - jnp.mean(cv, axis=-1, keepdims=True)
        rstd = lax.rsqrt(jnp.mean(xc * xc, axis=-1, keepdims=True) + LN_EPS)
        nh = xc * rstd
        n = nh * lng_ref[...] + lnb_ref[...]
        sg = _sig(n)
        dn = ds * (sg * (1.0 + n * (1.0 - sg)))
        dnh = dn * lng_ref[...]
        dc = rstd * (dnh - jnp.mean(dnh, axis=-1, keepdims=True) - nh * jnp.mean(dnh * nh, axis=-1, keepdims=True))
        dc_ref[...] = dc
        sm = jnp.concatenate([_colsum(dn * nh), _colsum(dn), _colsum(do), _colsum(dc), jnp.zeros((4, D), F32)], axis=0)
        _accumulate(sm_ref, sm, i == 0)

    return _pcall(body, name="conv_bwd1", grid=(T // tm,),
                  in_specs=[_rows(tm, D), _rows(tm, D), _full((D, D)), _full((1, D)), _full((1, D))],
                  out_specs=[_rows(tm, D), _acc_spec(8, D)],
                  out_shape=[S_((T, D), F32), S_((8, D), F32)], args=(dx1, c, w2, lng, lnb), comm=comm)


def _conv_bwd2(dc, a, x, w1g, wdw, gn, dx1, tm, tps, taps, comm=None):
    T, D = x.shape
    ns, _, nc = w1g.shape
    hc = HALO_CONV
    nt = T // tm
    tp = wdw.shape[0]

    def body(dc_ref, dch_ref, a_ref, ah_ref, x_ref, w_ref, wdw_ref, gn_ref, dx1_ref,
             gx_ref, dab_ref, dw_ref, db1_ref, dgn_ref, uext, dext, shifted, du_s, wacc):
        i = pl.program_id(0)
        ah = ah_ref[...]
        uext[0:hc, :] = jnp.where(i % tps == 0, 0.0, ah[:, :D] * _sig(ah[:, D:]))
        av = a_ref[...]
        sg2 = _sig(av[:, D:])
        uext[hc:hc + tm, :] = av[:, :D] * sg2
        dext[0:tm, :] = dc_ref[...]
        dext[tm:tm + hc, :] = jnp.where(i % tps == tps - 1, 0.0, dch_ref[...])

        @pl.when(i == 0)
        def _():
            wacc[...] = jnp.zeros_like(wacc)

        def wgrad(r0, src, ks, first):
            d = dc_ref[pl.ds(r0, CHUNK), :]
            for k, o in ks:
                wacc[8 * k:8 * k + 8, :] += _fold8(d * src[pl.ds(pl.multiple_of(r0 + o, 8), CHUNK), :])
        _by_sublane_shift(uext, shifted, tm, [hc - (taps - 1) + k for k in range(taps)], wgrad)

        def convt(r0, src, ks, first):
            acc = jnp.zeros((CHUNK, D), F32) if first else du_s[pl.ds(r0, CHUNK), :]
            for k, o in ks:
                acc = acc + wdw_ref[k:k + 1, :] * src[pl.ds(pl.multiple_of(r0 + o, 8), CHUNK), :]
            du_s[pl.ds(r0, CHUNK), :] = acc
        _by_sublane_shift(dext, shifted, tm, [taps - 1 - k for k in range(taps)], convt)

        du = du_s[...]
        da1 = du * sg2
        da2 = du * av[:, :D] * (sg2 * (1.0 - sg2))
        dab_ref[:, 0:D] = da1.astype(BF16)
        dab_ref[:, D:2 * D] = da2.astype(BF16)
        _accumulate(db1_ref, jnp.concatenate([_colsum(da1), _colsum(da2)], axis=0), i == 0)
        dh = jnp.zeros((tm, D), F32)
        for s in range(ns):
            dh = dh + _dot_nt(dab_ref[:, s * nc:(s + 1) * nc], w_ref[s])
        _, xh, r = _rms(x_ref[...], gn_ref[...])
        dxn, dg = _rms_bwd(dh, xh, r, gn_ref[...])
        gx_ref[...] = dx1_ref[...] + dxn
        _accumulate(dgn_ref, dg, i == 0)

        @pl.when(i == nt - 1)
        def _():
            dw_ref[...] = jnp.zeros_like(dw_ref)
            for k in range(taps):
                dw_ref[k:k + 1, :] = _colsum(wacc[8 * k:8 * k + 8, :])

    return _pcall(body, name="conv_bwd2", grid=(nt,),
                  in_specs=[_rows(tm, D), _next_halo(tm, hc, D, T), _rows(tm, 2 * D), _prev_halo(tm, hc, 2 * D), _rows(tm, D),
                            _full(w1g.shape), _full(wdw.shape), _full((1, D)), _rows(tm, D)],
                  out_specs=[_rows(tm, D), _rows(tm, 2 * D), _acc_spec(tp, D), _acc_spec(2, D), _acc_spec(1, D)],
                  out_shape=[S_((T, D), F32), S_((T, 2 * D), BF16), S_((tp, D), F32), S_((2, D), F32), S_((1, D), F32)],
                  scratch=[pltpu.VMEM((tm + hc, D), F32), pltpu.VMEM((tm + hc, D), F32), pltpu.VMEM((tm + hc - 8, D), F32),
                           pltpu.VMEM((tm, D), F32), pltpu.VMEM((8 * taps, D), F32)],
                  args=(dc, dc, a, a, x, w1g, wdw, gn, dx1), comm=comm)


def _place():
    x, y, c = lax.axis_index("x"), lax.axis_index("y"), lax.axis_index("c")
    chips = [(1 - x, y), (x, 1 - y), (1 - x, 1 - y)]
    return x, y, c, chips


class _GatherComm:
    def __init__(self, slots):
        n = len(slots)
        self.inputs = list(slots)
        self.out_shape = [S_(s.shape, s.dtype) for s in slots]
        self.aliases = {i: i for i in range(n)}
        self.sems = [pltpu.SemaphoreType.DMA((n, 3))] * 4
        self.halves = [s.shape[1] // 2 for s in slots]
        self.paired = [s.shape[0] == N_CHIPS // 2 for s in slots]
        self.widths = [s.shape[2] // 2 if p else s.shape[2] for s, p in zip(slots, self.paired)]

    def _copies(self, outs, sems):
        s_ici, r_ici, s_d2d, r_d2d = sems
        x, y, c, chips = _place()
        peers = [(j, chip, 2 * chip[0] + chip[1]) for j, chip in enumerate(chips)]

        def block(i, chip, h):
            rows = pl.ds(pl.multiple_of(h * self.halves[i], 8), self.halves[i])
            if self.paired[i]:
                return outs[i].at[chip // 2, rows, pl.ds(pl.multiple_of((chip % 2) * self.widths[i], 128), self.widths[i])]
            return outs[i].at[chip, rows]

        def ici(i, j, chip, to):
            blk = block(i, chip, c)
            return pltpu.make_async_remote_copy(src_ref=blk, dst_ref=blk, send_sem=s_ici.at[i, j], recv_sem=r_ici.at[i, j],
                                                device_id=to, device_id_type=MESH)

        def d2d(i, j, chip, h):
            blk = block(i, chip, h)
            return pltpu.make_async_remote_copy(src_ref=blk, dst_ref=blk, send_sem=s_d2d.at[i, j], recv_sem=r_d2d.at[i, j],
                                                device_id=(x, y, 1 - c), device_id_type=MESH)
        return (x, y, c, 2 * x + y), peers, ici, d2d

    def before(self, cin, outs, sems, step, total):
        (x, y, c, q), peers, ici, d2d = self._copies(outs, sems)

        @pl.when(step == 0)
        def _():
            for i in range(len(outs)):
                for j, chip, _ in peers:
                    ici(i, j, q, (*chip, c)).start()

    def after(self, cin, outs, sems, step, total):
        (x, y, c, q), peers, ici, d2d = self._copies(outs, sems)

        @pl.when(step == max(total - 2, 0))
        def _():
            for i in range(len(outs)):
                for j, chip, qj in peers:
                    ici(i, j, qj, (x, y, c)).wait_recv()
                    d2d(i, j, qj, c).start()

        @pl.when(step == total - 1)
        def _():
            for i in range(len(outs)):
                for j, chip, qj in peers:
                    d2d(i, j, qj, 1 - c).wait_recv()
            for i in range(len(outs)):
                for j, chip, qj in peers:
                    ici(i, j, q, (*chip, c)).wait_send()
                    d2d(i, j, qj, c).wait_send()


class _ExchangeComm:
    def __init__(self, sums):
        n = len(sums)
        self.inputs = list(sums)
        self.out_shape = [S_((3,) + s.shape[1:], s.dtype) for s in sums]
        self.aliases = {}
        self.sems = [pltpu.SemaphoreType.DMA((n, 3))] * 2

    def _copies(self, srcs, outs, sems):
        s_sem, r_sem = sems
        x, y, c, chips = _place()
        return [pltpu.make_async_remote_copy(src_ref=srcs[i].at[2 * chip[0] + chip[1]], dst_ref=outs[i].at[j],
                                             send_sem=s_sem.at[i, j], recv_sem=r_sem.at[i, j],
                                             device_id=(*chip, c), device_id_type=MESH)
                for i in range(len(srcs)) for j, chip in enumerate(chips)]

    def before(self, srcs, outs, sems, step, total):
        @pl.when(step == 0)
        def _():
            for cp in self._copies(srcs, outs, sems):
                cp.start()

    def after(self, srcs, outs, sems, step, total):
        @pl.when(step == total - 1)
        def _():
            for cp in self._copies(srcs, outs, sems):
                cp.wait()


def _own_slots(specs):
    n = len(specs)
    shapes, blocks, count = [], [], []
    for a, (w, layer, dtype, pairs) in enumerate(specs):
        _, r, c = w.shape
        br = _row_block(r, c)
        shapes.append(S_((N_CHIPS // 2, r, 2 * c) if pairs else (N_CHIPS, r, c), dtype))
        blocks += [(a, k, k * br, br) for k in range(r // br)]
        count.append(r // br)
    gather = _GatherComm(shapes[:1])
    n_sems = len(gather.sems)

    def body(*refs):
        srcs, outs = refs[:n], refs[n:2 * n]
        inb, outb = refs[2 * n:3 * n], refs[3 * n:4 * n]
        lsem, ssem = refs[4 * n], refs[4 * n + 1]
        gsems = refs[4 * n + 2:4 * n + 2 + n_sems]
        x, y, c, _ = _place()
        q = 2 * x + y

        def load(a, k, r0, br):
            return pltpu.make_async_copy(srcs[a].at[specs[a][1], pl.ds(r0, br)], inb[a].at[k % 2], lsem.at[a, k % 2])

        def store(a, k, r0, br):
            if specs[a][3]:
                wc = specs[a][0].shape[2]
                dst = outs[a].at[q // 2, pl.ds(r0, br), pl.ds(pl.multiple_of((q % 2) * wc, 128), wc)]
            else:
                dst = outs[a].at[q, pl.ds(r0, br)]
            return pltpu.make_async_copy(outb[a].at[k % 2], dst, ssem.at[a, k % 2])

        load(*blocks[0]).start()
        for b, (a, k, r0, br) in enumerate(blocks):
            if b + 1 < len(blocks):
                load(*blocks[b + 1]).start()
            load(a, k, r0, br).wait()
            if k >= 2:
                store(a, k - 2, r0 - 2 * br, br).wait()
            outb[a][k % 2] = inb[a][k % 2].astype(specs[a][2])
            store(a, k, r0, br).start()
            if a == 0 and k == count[0] - 1:
                for kk in range(max(k - 1, 0), k + 1):
                    store(0, kk, kk * br, br).wait()
                gather.before(None, outs[:1], gsems, 0, 1)
        for a in range(1, n):
            br = blocks[sum(count[:a])][3]
            for kk in range(max(count[a] - 2, 0), count[a]):
                store(a, kk, kk * br, br).wait()
        gather.after(None, outs[:1], gsems, 0, 1)

    any_spec = pl.BlockSpec(memory_space=pl.ANY)
    scratch = ([pltpu.VMEM((2, blocks[sum(count[:a])][3], specs[a][0].shape[2]), F32) for a in range(n)]
               + [pltpu.VMEM((2, blocks[sum(count[:a])][3], specs[a][0].shape[2]), specs[a][2]) for a in range(n)]
               + [pltpu.SemaphoreType.DMA((n, 2)), pltpu.SemaphoreType.DMA((n, 2))] + list(gather.sems))
    outs = pl.pallas_call(body, name="own_slots", in_specs=[any_spec] * n, out_specs=[any_spec] * n, out_shape=shapes,
                          scratch_shapes=scratch,
                          compiler_params=pltpu.CompilerParams(vmem_limit_bytes=VMEM_LIMIT_BYTES))(*[s[0] for s in specs])
    return list(outs)


def _comm_only(comm, name):
    def body():
        pass
    outs = _pcall(body, name=name, grid=(1,), in_specs=[], out_specs=[], out_shape=[], args=[], comm=comm)
    return list(outs)


def _pair_exchange(parts, name):
    n = len(parts)

    def body(*refs):
        srcs, outs = refs[:n], refs[n:2 * n]
        s_sem, r_sem = refs[2 * n:]
        x, y, c, _ = _place()
        cps = []
        for i in range(n):
            for s in range(N_CHIPS):
                cp = pltpu.make_async_remote_copy(src_ref=srcs[i].at[s, 1 - c], dst_ref=outs[i].at[s], send_sem=s_sem.at[i, s],
                                                  recv_sem=r_sem.at[i, s], device_id=(x, y, 1 - c), device_id_type=MESH)
                cp.start()
                cps.append(cp)
        for cp in cps:
            cp.wait()

    any_spec = pl.BlockSpec(memory_space=pl.ANY)
    outs = pl.pallas_call(
        body, name=name, in_specs=[any_spec] * n, out_specs=[any_spec] * n,
        out_shape=[S_((N_CHIPS,) + p.shape[2:], p.dtype) for p in parts],
        scratch_shapes=[pltpu.SemaphoreType.DMA((n, N_CHIPS))] * 2,
    )(*parts)
    return list(outs)


class _PairComm:
    def __init__(self, parts):
        n = len(parts)
        self.inputs = list(parts)
        self.out_shape = [S_((N_CHIPS,) + p.shape[2:], p.dtype) for p in parts]
        self.aliases = {}
        self.sems = [pltpu.SemaphoreType.DMA((n, N_CHIPS))] * 2

    def _copies(self, srcs, outs, sems):
        s_sem, r_sem = sems
        x, y, c, _ = _place()
        return [pltpu.make_async_remote_copy(src_ref=srcs[i].at[s, 1 - c], dst_ref=outs[i].at[s], send_sem=s_sem.at[i, s],
                                             recv_sem=r_sem.at[i, s], device_id=(x, y, 1 - c), device_id_type=MESH)
                for i in range(len(srcs)) for s in range(N_CHIPS)]

    def before(self, srcs, outs, sems, step, total):
        @pl.when(step == 0)
        def _():
            for cp in self._copies(srcs, outs, sems):
                cp.start()

    def after(self, srcs, outs, sems, step, total):
        @pl.when(step == total - 1)
        def _():
            for cp in self._copies(srcs, outs, sems):
                cp.wait()


class _ShareComm:
    def __init__(self, bufs):
        n = len(bufs)
        self.inputs = list(bufs)
        self.out_shape = [S_(b.shape, b.dtype) for b in bufs]
        self.aliases = {i: i for i in range(n)}
        self.sems = [pltpu.SemaphoreType.DMA((n, max(b.shape[0] for b in bufs)))] * 2
        self.layers = [b.shape[0] for b in bufs]

    def _copies(self, outs, sems):
        s_sem, r_sem = sems
        x, y, c, _ = _place()
        cps = []
        for i, nl in enumerate(self.layers):
            for l in range(nl):
                blk = outs[i].at[l, c]
                cps.append(pltpu.make_async_remote_copy(src_ref=blk, dst_ref=blk, send_sem=s_sem.at[i, l],
                                                        recv_sem=r_sem.at[i, l], device_id=(x, y, 1 - c),
                                                        device_id_type=MESH))
        return cps

    def before(self, cin, outs, sems, step, total):
        @pl.when(step == 0)
        def _():
            for cp in self._copies(outs, sems):
                cp.start()

    def after(self, cin, outs, sems, step, total):
        @pl.when(step == total - 1)
        def _():
            for cp in self._copies(outs, sems):
                cp.wait()


class _Both:
    def __init__(self, a, b):
        self.parts = (a, b)
        self.inputs = a.inputs + b.inputs
        self.out_shape = a.out_shape + b.out_shape
        self.aliases = dict(a.aliases)
        self.aliases.update({len(a.inputs) + i: len(a.out_shape) + o for i, o in b.aliases.items()})
        self.sems = a.sems + b.sems

    def _each(self, phase, cin, cout, sems, step, total):
        i = o = s = 0
        for p in self.parts:
            ni, no, ns = len(p.inputs), len(p.out_shape), len(p.sems)
            getattr(p, phase)(cin[i:i + ni], cout[o:o + no], sems[s:s + ns], step, total)
            i, o, s = i + ni, o + no, s + ns

    def before(self, cin, cout, sems, step, total):
        self._each("before", cin, cout, sems, step, total)

    def after(self, cin, cout, sems, step, total):
        self._each("after", cin, cout, sems, step, total)


def _small_allreduce(groups):
    n = len(groups)
    arrays = []
    for grp in groups:
        for p in grp:
            if not isinstance(p, int) and not any(p[0] is a for a in arrays):
                arrays.append(p[0])
    shapes = []
    for grp in groups:
        rows = sum(p if isinstance(p, int) else p[2] - p[1] for p in grp)
        width = next(p[0].shape[1] for p in grp if not isinstance(p, int))
        assert rows % 8 == 0
        shapes.append((rows, width))
    na = len(arrays)

    def body(*refs):
        a_refs, o_refs = refs[:na], refs[na:na + n]
        rest = refs[na + n:]
        v_refs, sibs, css, gots = rest[:n], rest[n:2 * n], rest[2 * n:3 * n], rest[3 * n:4 * n]
        s_sem, r_sem = rest[4 * n:]
        for i, grp in enumerate(groups):
            o = 0
            for p in grp:
                if isinstance(p, int):
                    v_refs[i][o:o + p, :] = jnp.zeros((p, shapes[i][1]), F32)
                    o += p
                else:
                    k = next(t for t, a in enumerate(arrays) if a is p[0])
                    v_refs[i][o:o + p[2] - p[1], :] = a_refs[k][p[1]:p[2], :]
                    o += p[2] - p[1]
        x, y, c, chips = _place()
        q = 2 * x + y
        to_sib = [pltpu.make_async_remote_copy(src_ref=v_refs[i], dst_ref=sibs[i], send_sem=s_sem.at[i, 3], recv_sem=r_sem.at[i, 3],
                                               device_id=(x, y, 1 - c), device_id_type=MESH) for i in range(n)]
        for cp in to_sib:
            cp.start()
        cps = []
        for i in range(n):
            to_sib[i].wait()
            mine, other = v_refs[i][...], sibs[i][...]
            css[i][...] = jnp.where(c == 0, mine, other) + jnp.where(c == 0, other, mine)
            for j, chip in enumerate(chips):
                cp = pltpu.make_async_remote_copy(src_ref=css[i], dst_ref=gots[i].at[j], send_sem=s_sem.at[i, j],
                                                  recv_sem=r_sem.at[i, j], device_id=(*chip, c), device_id_type=MESH)
                cp.start()
                cps.append(cp)
        for cp in cps:
            cp.wait()
        flips = [2, 1, 3]
        for i in range(n):
            total = None
            for k in range(N_CHIPS):
                d = q ^ k
                term = jnp.where(d == 0, css[i][...], 0.0)
                for j in range(3):
                    term = jnp.where(d == flips[j], gots[i][j], term)
                total = term if total is None else total + term
            o_refs[i][...] = total

    vm = pl.BlockSpec(memory_space=pltpu.VMEM)
    return pl.pallas_call(
        body, name="small_allreduce", in_specs=[vm] * na, out_specs=[vm] * n, out_shape=[S_(s, F32) for s in shapes],
        scratch_shapes=[pltpu.VMEM(s, F32) for s in shapes] * 3 + [pltpu.VMEM((3,) + s, F32) for s in shapes]
        + [pltpu.SemaphoreType.DMA((n, 4)), pltpu.SemaphoreType.DMA((n, 4))],
    )(*arrays)


def _row_block(r, c, target_bytes=1 << 20):
    br = r
    while br % 32 == 0 and br * c * 4 > target_bytes:
        br //= 2
    return br


def _chip_sum(part, sib, ids, name):
    _, _, r, c = part.shape
    br = _row_block(r, c, 1 << 22)

    def body(ids_ref, p_ref, s_ref, o_ref):
        o_ref[...] = (p_ref[...].astype(F32) + s_ref[...].astype(F32)).astype(BF16)

    return _pcall(body, name=name, grid=(N_CHIPS, r // br), prefetch=1,
                  in_specs=[pl.BlockSpec((None, None, br, c), lambda s, i, ids: (s, ids[1], i, 0)),
                            pl.BlockSpec((None, br, c), lambda s, i, ids: (s, i, 0))],
                  out_specs=pl.BlockSpec((None, br, c), lambda s, i, ids: (s, i, 0)),
                  out_shape=S_((N_CHIPS, r, c), BF16), args=(ids, part, sib))


def _final_sum(part, sib, got, ids, name, layer=0, n_layers=1, buf=None):
    _, r, c = got.shape
    br = _row_block(r, c, 1 << 21)

    def body(ids_ref, p_ref, s_ref, got_ref, *rest):
        v = p_ref[...].astype(F32) + s_ref[...].astype(F32)
        for j in range(3):
            v = v + got_ref[j].astype(F32)
        rest[-1][...] = v

    in_specs = [pl.BlockSpec((None, None, br, c), lambda i, ids: (ids[0], ids[1], i, 0)),
                pl.BlockSpec((None, br, c), lambda i, ids: (ids[0], i, 0)),
                pl.BlockSpec((3, br, c), lambda i, ids: (0, i, 0))]
    args = [ids, part, sib, got]
    if buf is not None:
        in_specs.append(pl.BlockSpec(memory_space=pl.ANY))
        args.append(buf)
    return _pcall(body, name=name, grid=(r // br,), prefetch=1, in_specs=in_specs,
                  out_specs=pl.BlockSpec((None, None, br, c), lambda i, ids: (layer, ids[1], i, 0)),
                  out_shape=S_((n_layers, 2, r, c), F32), args=args, aliases={4: 0} if buf is not None else None)


def _adam_math(w, g, m, v):
    c1 = 1.0 / (1.0 - ADAM_B1 ** ADAM_STEP)
    c2 = 1.0 / (1.0 - ADAM_B2 ** ADAM_STEP)
    mn = ADAM_B1 * m + (1.0 - ADAM_B1) * g
    vn = ADAM_B2 * v + (1.0 - ADAM_B2) * (g * g)
    return -ADAM_LR * ((mn * c1) / (jnp.sqrt(vn * c2) + ADAM_EPS) + ADAM_WD * w), mn, vn


def _adamw(w, g, m, v, name):
    nl, r, c = w.shape
    br = _row_block(r, c, 1 << 20)

    def body(w_ref, g_ref, m_ref, v_ref, go_ref, d_ref, mo_ref, vo_ref):
        gv = g_ref[...]
        go_ref[...] = gv
        d_ref[...], mo_ref[...], vo_ref[...] = _adam_math(w_ref[...], gv, m_ref[...], v_ref[...])

    blk = pl.BlockSpec((None, br, c), lambda l, i: (l, i, 0))
    return _pcall(body, name=name, grid=(nl, r // br), in_specs=[blk] * 4, out_specs=[blk] * 4,
                  out_shape=[S_((nl, r, c), F32)] * 4, args=(w, g, m, v))


def _adamw_small(rd, rf, sharded_g, ws, ms, vs, rows_d, rows_f, loss_row):
    n = len(ws)
    ns = len(sharded_g)
    rep = [k for k in range(n) if rows_d[k] is not None or rows_f[k] is not None]

    def body(*refs):
        rd_ref, rf_ref = refs[0], refs[1]
        sg = refs[2:2 + ns]
        w_refs, m_refs, v_refs = (refs[2 + ns + t * n:2 + ns + (t + 1) * n] for t in range(3))
        loss_ref, outs = refs[2 + ns + 3 * n], refs[3 + ns + 3 * n:]
        g_out, d_out, m_out, v_out = outs[:len(rep)], outs[len(rep):len(rep) + n], outs[len(rep) + n:len(rep) + 2 * n], \
            outs[len(rep) + 2 * n:]
        loss_ref[...] = jnp.sum(rd_ref[loss_row:loss_row + 1, :], axis=1, keepdims=True)
        si = 0
        for k in range(n):
            shape = w_refs[k].shape
            if rows_d[k] is not None or rows_f[k] is not None:
                src, rws = (rd_ref, rows_d[k]) if rows_d[k] is not None else (rf_ref, rows_f[k])
                axis = 0 if shape[0] == len(rws) else 1
                g = src[rws[0]:rws[0] + 1, :]
                if len(rws) > 1:
                    g = jnp.concatenate([src[r:r + 1, :] for r in rws], axis=axis)
                g_out[rep.index(k)][...] = g
            else:
                g = sg[si][...]
                si += 1
            d_out[k][...], m_out[k][...], v_out[k][...] = _adam_math(w_refs[k][...], g, m_refs[k][...], v_refs[k][...])

    vm = pl.BlockSpec(memory_space=pltpu.VMEM)
    shapes = [S_(w.shape, F32) for w in ws]
    outs = pl.pallas_call(
        body, name="adamw_small", in_specs=[vm] * (2 + ns + 3 * n), out_specs=[vm] * (1 + len(rep) + 3 * n),
        out_shape=[S_((1, 1), F32)] + [shapes[k] for k in rep] + shapes * 3,
    )(rd, rf, *sharded_g, *ws, *ms, *vs)
    g_rep = dict(zip(rep, outs[1:1 + len(rep)]))
    o = outs[1 + len(rep):]
    return outs[0], g_rep, o[:n], o[n:2 * n], o[2 * n:]


def _pack(pieces, rows):
    flat = jnp.concatenate([p.reshape(-1).astype(F32) for p in pieces])
    return jnp.pad(flat, (0, rows * 128 - flat.shape[0])).reshape(rows, 128)


def _unpack(packed, shapes):
    flat = packed.reshape(-1)
    out, o = [], 0
    for s in shapes:
        n = 1
        for d in s:
            n *= d
        out.append(flat[o:o + n].reshape(s))
        o += n
    return out


def _rows_for(pieces_or_shapes):
    n = 0
    for p in pieces_or_shapes:
        k = 1
        for d in (p if isinstance(p, tuple) else p.shape):
            k *= d
        n += k
    return -(-n // 1024) * 8


def kernel(x, norm_mix, norm_ffn, conv_w_pw1, conv_b_pw1, conv_w_dw, conv_b_dw, conv_ln_g, conv_ln_b, conv_w_pw2, conv_b_pw2, pool_w, pool_b, pool_scale, ffn_w_up, ffn_w_dw, ffn_b_dw, ffn_w_down, final_norm, loss_target, m_norm_mix, m_norm_ffn, m_conv_w_pw1, m_conv_b_pw1, m_conv_w_dw, m_conv_b_dw, m_conv_ln_g, m_conv_ln_b, m_conv_w_pw2, m_conv_b_pw2, m_pool_w, m_pool_b, m_pool_scale, m_ffn_w_up, m_ffn_w_dw, m_ffn_b_dw, m_ffn_w_down, m_final_norm, v_norm_mix, v_norm_ffn, v_conv_w_pw1, v_conv_b_pw1, v_conv_w_dw, v_conv_b_dw, v_conv_ln_g, v_conv_ln_b, v_conv_w_pw2, v_conv_b_pw2, v_pool_w, v_pool_b, v_pool_scale, v_ffn_w_up, v_ffn_w_dw, v_ffn_b_dw, v_ffn_w_down, v_final_norm):
    nb, seq, D = x.shape
    T = nb * seq
    F = ffn_w_down.shape[1] * N_CHIPS
    taps = conv_w_dw.shape[1]
    ffn_taps = ffn_w_dw.shape[1]
    ng = pool_w.shape[1]
    cg = pool_w.shape[3]
    dsh = D // N_CHIPS
    fsh = F // N_CHIPS
    assert taps - 1 <= HALO_CONV and ffn_taps - 1 <= HALO_FFN and max(POOL_WINDOWS) <= HALO_POOL
    tm = min(TILE_D, seq)
    tf = min(TILE_F, seq)
    assert seq % tm == 0 and seq % tf == 0 and tm % HALO_CONV == 0 and tf % CHUNK == 0
    tps, tpf = seq // tm, seq // tf
    tw = min(TILE_MM, T)
    assert T % tw == 0

    xi, yi, ci = lax.axis_index("x"), lax.axis_index("y"), lax.axis_index("c")
    qi = 2 * xi + yi
    ids = jnp.stack([qi, ci]).astype(jnp.int32)
    x2d = x.reshape(T, D)
    tg2d = loss_target.reshape(T, D)

    small_sharded = [conv_w_dw[0], ffn_w_dw, pool_b, pool_scale]
    srows = 2 * _rows_for(small_sharded)
    w1g, s_pw2, s_small, s_pool, s_up0, s_up1, s_down0, s_down1 = _own_slots([
        (conv_w_pw1, 0, BF16, False), (conv_w_pw2, 0, BF16, False), (_pack(small_sharded, srows)[None], 0, F32, False),
        (pool_w.reshape(1, ng * (cg // N_CHIPS), cg), 0, BF16, False), (ffn_w_up, 0, BF16, True), (ffn_w_up, 1, BF16, True),
        (ffn_w_down, 0, BF16, False), (ffn_w_down, 1, BF16, False)])
    s_up, s_down = [s_up0, s_up1], [s_down0, s_down1]
    hb0t, a, w2g, smallg = _conv_in(x2d, norm_mix[0:1], w1g, conv_b_pw1, tm, comm=_GatherComm([s_pw2, s_small]))
    w2f = w2g.reshape(D, D)
    sm_parts = [_unpack(smallg[s], [(taps, dsh), (2, ffn_taps, fsh), (1, dsh), (1, dsh)]) for s in range(N_CHIPS)]
    wdw_f = jnp.concatenate([p[0] for p in sm_parts], axis=1)
    wdw_f = jnp.pad(wdw_f, ((0, HALO_CONV - taps), (0, 0)))
    fdw_f = jnp.concatenate([p[1] for p in sm_parts], axis=2)
    fdw_f = jnp.pad(fdw_f, ((0, 0), (0, 8 - ffn_taps), (0, 0)))
    pb_f = jnp.concatenate([p[2] for p in sm_parts], axis=1)
    ps_f = jnp.concatenate([p[3] for p in sm_parts], axis=1)

    c, sbt, x1, hb1, hb1t, wup0, wd0 = _conv_mid(a, x2d, wdw_f, conv_b_dw, conv_ln_g, conv_ln_b, w2f, conv_b_pw2,
                                                 norm_ffn[0:1], tm, tps, taps, comm=_GatherComm([s_up[0], s_down[0]]))
    up0, gb0t, x2, h2, wup1, pwg = _ffn_fwd(hb1, x1, wup0, fdw_f[0], ffn_b_dw[0:1], wd0.reshape(F, D), norm_mix[1:2], tf, tpf,
                                            ffn_taps, "ffn_fwd0", comm=_GatherComm([s_up[1], s_pool]))
    pwf = pwg.reshape(N_CHIPS, ng, cg // N_CHIPS, cg).transpose(1, 0, 2, 3).reshape(ng, cg, cg)
    plb, x3, hb3, hb3t, wd1 = _pool_fwd(h2, x2, pwf, pb_f, ps_f, norm_ffn[1:2], tm, tps, comm=_GatherComm([s_down[1]]))
    wup = [wup0, wup1]
    wdn = [wd0.reshape(F, D), wd1.reshape(F, D)]
    up1, gb1t, dx4, loss_part, d_final = _ffn_fwd(hb3, x3, wup[1], fdw_f[1], ffn_b_dw[1:2], wdn[1], final_norm.reshape(1, D),
                                                  tf, tpf, ffn_taps, "ffn_fwd1", target=tg2d)

    fs = {}

    def pair(group):
        return list(group), list(group.values()), _PairComm(list(group.values()))

    def chip_sums(nms, parts, from_sib):
        for nm, p, s in zip(nms, parts, from_sib):
            fs[nm] = (p, s)
        return nms, _ExchangeComm([_chip_sum(p, s, ids, "chip_sum_" + nm) for nm, p, s in zip(nms, parts, from_sib)])

    def arrived(nms, got):
        for nm, g in zip(nms, got):
            fs[nm] = fs[nm] + (g,)

    dac1, dup1, fw1 = _bwd_down(dx4, up1, fdw_f[1], ffn_b_dw[1:2], wdn[1], tf, tpf, ffn_taps, "bwd_down1")
    p_down1 = _mm_tn(gb1t, dx4, F // 2, D, tw, False, "dw_down1")
    nms, parts, px = pair({"down1": p_down1.reshape(N_CHIPS, 2, fsh // 2, D)})
    dup1, dx3, dnf1, *sib = _bwd_up(dac1, dup1, wup[1], fdw_f[1], x3, norm_ffn[1:2], dx4, tf, tpf, ffn_taps, "bwd_up1", comm=px)
    nms, xchg = chip_sums(nms, parts, sib)
    p_up1, *got = _mm_tn(hb3t, dup1, D // 2, 4 * fsh, tw, True, "dw_up1", split=2, comm=xchg)
    arrived(nms, got)
    nms, parts, px = pair({"up1": p_up1.reshape(N_CHIPS, 2, D // 2, 2 * fsh)})
    dx2, dpw, pool_sm, *sib = _pool_bwd(dx3, plb, x2, pwf, pb_f, ps_f, norm_mix[1:2], tm, tps, comm=px)
    nms, xchg = chip_sums(nms, parts, sib)
    dac0, dup0, fw0, *got = _bwd_down(dx2, up0, fdw_f[0], ffn_b_dw[0:1], wdn[0], tf, tpf, ffn_taps, "bwd_down0", comm=xchg)
    arrived(nms, got)
    p_down0 = _mm_tn(gb0t, dx2, F // 2, D, tw, False, "dw_down0")
    nms, parts, px = pair({"down0": p_down0.reshape(N_CHIPS, 2, fsh // 2, D)})
    dup0, dx1, dnf0, *sib = _bwd_up(dac0, dup0, wup[0], fdw_f[0], x1, norm_ffn[0:1], dx2, tf, tpf, ffn_taps, "bwd_up0", comm=px)
    nms, xchg = chip_sums(nms, parts, sib)
    p_up0, *got = _mm_tn(hb1t, dup0, D // 2, 4 * fsh, tw, True, "dw_up0", split=2, comm=xchg)
    arrived(nms, got)
    csh = cg // N_CHIPS
    p_pool = dpw.reshape(2, ng // 2, N_CHIPS, csh, cg).transpose(2, 0, 1, 3, 4).reshape(N_CHIPS, 2, (ng // 2) * csh, cg)
    nms, parts, px = pair({"up0": p_up0.reshape(N_CHIPS, 2, D // 2, 2 * fsh), "pool": p_pool.astype(BF16)})
    dc, conv_sm, *sib = _conv_bwd1(dx1, c, w2f, conv_ln_g, conv_ln_b, tm, comm=px)
    nms, xchg = chip_sums(nms, parts, sib)
    p_pw2 = _mm_tn(sbt, dx1, D, D, tw, False, "dw_pw2")
    nms2, parts2, px = pair({"pw2": p_pw2.reshape(N_CHIPS, 2, dsh // 2, D)})
    grad_x, dab, dwdw, db1, dnm0, *both = _conv_bwd2(dc, a, x2d, w1g, wdw_f, norm_mix[0:1], dx1, tm, tps, taps,
                                                     comm=_Both(xchg, px))
    arrived(nms, both[:len(nms)])
    nms, xchg = chip_sums(nms2, parts2, both[len(nms):])
    p_pw1, *got = _mm_tn(hb0t, dab, D, 2 * dsh, tw, True, "dw_pw1", comm=xchg)
    arrived(nms, got)
    nms, parts, _ = pair({"pw1": p_pw1.reshape(N_CHIPS, 2, D // 2, 2 * dsh)})
    nms, xchg = chip_sums(nms, parts, _pair_exchange(parts, "grad_pair_exchange_pw1"))
    bufs = [_final_sum(*fs[nm], ids, "final_sum_" + nm) for nm in ("pw2", "pool")]
    for nm in ("up", "down"):
        b = _final_sum(*fs[nm + "0"], ids, "final_sum_" + nm + "0", layer=0, n_layers=2)
        bufs.append(_final_sum(*fs[nm + "1"], ids, "final_sum_" + nm + "1", layer=1, n_layers=2, buf=b))
    *shared, got_pw1 = _comm_only(_Both(_ShareComm(bufs), xchg), "grad_share_halves")
    arrived(nms, [got_pw1])
    shared += _comm_only(_ShareComm([_final_sum(*fs["pw1"], ids, "final_sum_pw1")]), "grad_share_halves_pw1")
    g_pw2, g_pool, g_up, g_down, g_pw1 = [b.reshape(b.shape[0], 2 * b.shape[2], b.shape[3]) for b in shared]

    rd, rf = _small_allreduce([
        [(dnm0, 0, 1), (pool_sm, 2, 3), (dnf0, 0, 1), (dnf1, 0, 1), (db1, 0, 2), (conv_sm, 0, 4), (d_final, 0, 1),
         (pool_sm, 0, 2), (loss_part, 0, 1), 2, (dwdw, 0, dwdw.shape[0])],
        [(fw0, 0, 4), (fw1, 0, 4)]])
    g_wdw = lax.dynamic_slice_in_dim(rd[16:16 + taps], qi * dsh, dsh, axis=1)
    g_ps = lax.dynamic_slice_in_dim(rd[11:12], qi * dsh, dsh, axis=1)
    g_pb = lax.dynamic_slice_in_dim(rd[12:13], qi * dsh, dsh, axis=1)
    g_fdw = lax.dynamic_slice_in_dim(rf.reshape(2, 4, F)[:, :ffn_taps], qi * fsh, fsh, axis=2).reshape(2 * ffn_taps, fsh)

    def big(w, g, m, v, nm):
        return tuple(t.reshape(w.shape) for t in _adamw(w.reshape(g.shape), g, m.reshape(g.shape), v.reshape(g.shape),
                                                        "adamw_" + nm))

    o_pw1 = big(conv_w_pw1, g_pw1, m_conv_w_pw1, v_conv_w_pw1, "pw1")
    o_pw2 = big(conv_w_pw2, g_pw2, m_conv_w_pw2, v_conv_w_pw2, "pw2")
    o_pool = big(pool_w, g_pool, m_pool_w, v_pool_w, "pool")
    o_up = big(ffn_w_up, g_up, m_ffn_w_up, v_ffn_w_up, "up")
    o_down = big(ffn_w_down, g_down, m_ffn_w_down, v_ffn_w_down, "down")

    snames = ["norm_mix", "norm_ffn", "b_pw1", "w_dw", "b_dw", "ln_g", "ln_b", "b_pw2", "pool_b", "pool_scale", "ffn_w_dw",
              "ffn_b_dw", "final_norm"]
    sw = [norm_mix, norm_ffn, conv_b_pw1, conv_w_dw, conv_b_dw, conv_ln_g, conv_ln_b, conv_b_pw2, pool_b, pool_scale,
          ffn_w_dw, ffn_b_dw, final_norm]
    smm = [m_norm_mix, m_norm_ffn, m_conv_b_pw1, m_conv_w_dw, m_conv_b_dw, m_conv_ln_g, m_conv_ln_b, m_conv_b_pw2, m_pool_b,
           m_pool_scale, m_ffn_w_dw, m_ffn_b_dw, m_final_norm]
    svv = [v_norm_mix, v_norm_ffn, v_conv_b_pw1, v_conv_w_dw, v_conv_b_dw, v_conv_ln_g, v_conv_ln_b, v_conv_b_pw2, v_pool_b,
           v_pool_scale, v_ffn_w_dw, v_ffn_b_dw, v_final_norm]
    sshapes = [tuple(w.shape) for w in sw]
    two_d = [(2, D), (2, D), (1, 2 * D), (taps, dsh), (1, D), (1, D), (1, D), (1, D), (1, dsh), (1, dsh), (2 * ffn_taps, fsh),
             (2, F), (1, D)]
    rows_d = [[0, 1], [2, 3], [4, 5], None, [9], [6], [7], [8], None, None, None, None, [10]]
    rows_f = [None] * 11 + [[3, 7], None]
    as2d = lambda ts: [t.reshape(s) for t, s in zip(ts, two_d)]
    loss, g_rep, sd, sm_new, sv_new = _adamw_small(rd, rf, [g_wdw, g_pb, g_ps, g_fdw], as2d(sw), as2d(smm), as2d(svv), rows_d,
                                                   rows_f, loss_row=13)
    sg = dict(g_rep)
    sg.update({3: g_wdw, 8: g_pb, 9: g_ps, 10: g_fdw})
    small_out = {n: tuple(t.reshape(sshapes[k]) for t in (sg[k], sd[k], sm_new[k], sv_new[k])) for k, n in enumerate(snames)}

    order = [small_out["norm_mix"], small_out["norm_ffn"], o_pw1, small_out["b_pw1"], small_out["w_dw"], small_out["b_dw"],
             small_out["ln_g"], small_out["ln_b"], o_pw2, small_out["b_pw2"], o_pool, small_out["pool_b"],
             small_out["pool_scale"], o_up, small_out["ffn_w_dw"], small_out["ffn_b_dw"], o_down, small_out["final_norm"]]
    return (loss[0, 0], grad_x.reshape(nb, seq, D), *[o[0] for o in order], *[o[1] for o in order], *[o[2] for o in order],
            *[o[3] for o in order])
```

```python
import functools

import jax
import jax.numpy as jnp
from jax import lax
from jax.experimental import pallas as pl
from jax.experimental.pallas import tpu as pltpu

F32, BF16 = jnp.float32, jnp.bfloat16
S_ = jax.ShapeDtypeStruct
MESH = pl.DeviceIdType.MESH

RMS_EPS, LN_EPS = 1e-6, 1e-5
POOL_WINDOWS = (2, 4, 8, 16)
ADAM_LR, ADAM_B1, ADAM_B2, ADAM_EPS, ADAM_WD, ADAM_STEP = 0.001, 0.9, 0.999, 1e-08, 0.01, 10

VMEM_LIMIT_BYTES = 60 * 1024 * 1024
N_CHIPS = 4
CHUNK = 32
HALO_CONV = 32
HALO_POOL = 32
HALO_FFN = 8
TILE_D = 512
TILE_F = 256
TILE_MM = 1024


def _pcall(body, *, name, grid, in_specs, out_specs, out_shape, args, scratch=(), aliases=None, prefetch=0, comm=None):
    params = pltpu.CompilerParams(dimension_semantics=("arbitrary",) * len(grid), vmem_limit_bytes=VMEM_LIMIT_BYTES)
    if comm is not None:
        assert not prefetch
        single = not isinstance(out_shape, (list, tuple))
        in_specs, args, scratch = list(in_specs), list(args), list(scratch)
        out_specs, out_shape = ([out_specs], [out_shape]) if single else (list(out_specs), list(out_shape))
        n_in, n_out, n_scr, n_cin, n_cout = len(in_specs), len(out_specs), len(scratch), len(comm.inputs), len(comm.out_shape)
        any_spec = pl.BlockSpec(memory_space=pl.ANY)
        aliases = dict(aliases or {})
        aliases.update({n_in + a: n_out + b for a, b in comm.aliases.items()})
        inner = body

        def body(*refs):
            ins, cin = refs[:n_in], refs[n_in:n_in + n_cin]
            outs = refs[n_in + n_cin:n_in + n_cin + n_out]
            cout = refs[n_in + n_cin + n_out:n_in + n_cin + n_out + n_cout]
            scr = refs[n_in + n_cin + n_out + n_cout:n_in + n_cin + n_out + n_cout + n_scr]
            sems = refs[n_in + n_cin + n_out + n_cout + n_scr:]
            step, total = 0, 1
            for d, g in enumerate(grid):
                step = step * g + pl.program_id(d)
                total *= g
            comm.before(cin, cout, sems, step, total)
            inner(*ins, *outs, *scr)
            comm.after(cin, cout, sems, step, total)

        in_specs += [any_spec] * n_cin
        args += list(comm.inputs)
        out_specs += [any_spec] * n_cout
        out_shape += list(comm.out_shape)
        scratch += list(comm.sems)
    if prefetch:
        spec = pltpu.PrefetchScalarGridSpec(num_scalar_prefetch=prefetch, grid=grid, in_specs=in_specs,
                                            out_specs=out_specs, scratch_shapes=list(scratch))
        return pl.pallas_call(body, name=name, grid_spec=spec, out_shape=out_shape,
                              input_output_aliases=aliases or {}, compiler_params=params)(*args)
    return pl.pallas_call(body, name=name, grid=grid, in_specs=in_specs, out_specs=out_specs, out_shape=out_shape,
                          scratch_shapes=list(scratch), input_output_aliases=aliases or {},
                          compiler_params=params)(*args)


def _full(shape):
    n = len(shape)
    return pl.BlockSpec(tuple(shape), lambda *_: (0,) * n, pipeline_mode=pl.Buffered(1))


def _rows(tm, c, col=0):
    return pl.BlockSpec((tm, c), lambda i, *_: (i, col))


def _cols(r, tm):
    return pl.BlockSpec((r, tm), lambda i, *_: (0, i))


def _prev_halo(tm, hb, c):
    return pl.BlockSpec((hb, c), lambda i, *_: (jnp.maximum(i * (tm // hb) - 1, 0), 0))


def _next_halo(tm, hb, c, total_rows):
    last = total_rows // hb - 1
    return pl.BlockSpec((hb, c), lambda i, *_: (jnp.minimum((i + 1) * (tm // hb), last), 0))


def _acc_spec(r, c):
    return pl.BlockSpec((r, c), lambda *_: (0, 0))


def _rms(x, g):
    r = lax.rsqrt(jnp.mean(x * x, axis=-1, keepdims=True) + RMS_EPS)
    xh = x * r
    return xh * g, xh, r


def _rms_bwd(dy, xh, r, g):
    dxh = dy * g
    dx = r * (dxh - xh * jnp.mean(dxh * xh, axis=-1, keepdims=True))
    return dx, jnp.sum(dy * xh, axis=0, keepdims=True)


def _colsum(v):
    return jnp.sum(v, axis=0, keepdims=True)


def _fold8(v):
    out = v[0:8]
    for j in range(1, v.shape[0] // 8):
        out = out + v[8 * j:8 * j + 8]
    return out


def _chunks(n_rows, fn):
    def step(j, carry):
        fn(pl.multiple_of(j * CHUNK, CHUNK))
        return carry
    lax.fori_loop(0, n_rows // CHUNK, step, 0)


def _dot(a, b):
    return jnp.dot(a, b, preferred_element_type=F32)


def _dot_nt(a, b):
    return lax.dot_general(a, b, (((1,), (1,)), ((), ())), preferred_element_type=F32)


def _dot_tn(a, b):
    return lax.dot_general(a, b, (((0,), (0,)), ((), ())), preferred_element_type=F32)


def _sig(v):
    return jax.nn.sigmoid(v)


def _accumulate(ref, val, first):
    @pl.when(first)
    def _():
        ref[...] = val

    @pl.when(jnp.logical_not(first))
    def _():
        ref[...] = ref[...] + val


def _conv_in(x, g, w1g, b1, tm, comm=None):
    T, D = x.shape
    ns, _, nc = w1g.shape

    def body(x_ref, g_ref, w_ref, b_ref, hbt_ref, a_ref):
        hb = _rms(x_ref[...], g_ref[...])[0].astype(BF16)
        hbt_ref[...] = hb.T
        for s in range(ns):
            a_ref[:, s * nc:(s + 1) * nc] = _dot(hb, w_ref[s]) + b_ref[:, s * nc:(s + 1) * nc]

    return _pcall(body, name="conv_in", grid=(T // tm,),
                  in_specs=[_rows(tm, D), _full((1, D)), _full(w1g.shape), _full((1, ns * nc))],
                  out_specs=[_cols(D, tm), _rows(tm, ns * nc)],
                  out_shape=[S_((D, T), BF16), S_((T, ns * nc), F32)], args=(x, g, w1g, b1), comm=comm)


def _conv_mid(a, x, wdw, bdw, lng, lnb, w2, b2, gn, tm, tps, taps, comm=None):
    T, D = x.shape
    hc = HALO_CONV

    def body(a_ref, ah_ref, x_ref, wdw_ref, bdw_ref, lng_ref, lnb_ref, w2_ref, b2_ref, gn_ref,
             c_ref, sbt_ref, x1_ref, hb_ref, hbt_ref, uext, shifted):
        i = pl.program_id(0)
        ah = ah_ref[...]
        uext[0:hc, :] = jnp.where(i % tps == 0, 0.0, ah[:, :D] * _sig(ah[:, D:]))
        av = a_ref[...]
        uext[hc:hc + tm, :] = av[:, :D] * _sig(av[:, D:])

        def conv(r0, src, ks, first):
            acc = jnp.broadcast_to(bdw_ref[...], (CHUNK, D)) if first else c_ref[pl.ds(r0, CHUNK), :]
            for k, o in ks:
                acc = acc + wdw_ref[k:k + 1, :] * src[pl.ds(pl.multiple_of(r0 + o, 8), CHUNK), :]
            c_ref[pl.ds(r0, CHUNK), :] = acc
        _by_sublane_shift(uext, shifted, tm, [hc - (taps - 1) + k for k in range(taps)], conv)

        cv = c_ref[...]
        xc = cv - jnp.mean(cv, axis=-1, keepdims=True)
        nh = xc * lax.rsqrt(jnp.mean(xc * xc, axis=-1, keepdims=True) + LN_EPS)
        n = nh * lng_ref[...] + lnb_ref[...]
        sb = (n * _sig(n)).astype(BF16)
        sbt_ref[...] = sb.T
        x1 = x_ref[...] + _dot(sb, w2_ref[...]) + b2_ref[...]
        x1_ref[...] = x1
        hb = _rms(x1, gn_ref[...])[0].astype(BF16)
        hb_ref[...] = hb
        hbt_ref[...] = hb.T

    return _pcall(body, name="conv_mid", grid=(T // tm,),
                  in_specs=[_rows(tm, 2 * D), _prev_halo(tm, hc, 2 * D), _rows(tm, D), _full(wdw.shape), _full((1, D)),
                            _full((1, D)), _full((1, D)), _full((D, D)), _full((1, D)), _full((1, D))],
                  out_specs=[_rows(tm, D), _cols(D, tm), _rows(tm, D), _rows(tm, D), _cols(D, tm)],
                  out_shape=[S_((T, D), F32), S_((D, T), BF16), S_((T, D), F32), S_((T, D), BF16), S_((D, T), BF16)],
                  scratch=[pltpu.VMEM((tm + hc, D), F32), pltpu.VMEM((tm + hc - 8, D), F32)],
                  args=(a, a, x, wdw, bdw, lng, lnb, w2, b2, gn), comm=comm)


def _ffn_taps(aext, realigned, tm, ffn_taps):
    for k in range(ffn_taps - 1):
        realigned[k] = aext[pl.ds(HALO_FFN - (ffn_taps - 1) + k, tm), :]
    return [realigned.at[k] for k in range(ffn_taps - 1)] + [aext.at[pl.ds(HALO_FFN, tm)]]


def _ffn_conv(taps, wdw_ref, bdw_ref):
    acc = bdw_ref[...]
    for k, tap in enumerate(taps):
        acc = acc + wdw_ref[k:k + 1, :] * tap[...]
    return acc


def _by_sublane_shift(ext, shifted, tm, offsets, per_chunk):
    first = True
    for b in range(8):
        ks = [(k, o - b) for k, o in enumerate(offsets) if o % 8 == b]
        if not ks:
            continue
        if b:
            shifted[...] = ext[pl.ds(b, shifted.shape[0]), :]
        _chunks(tm, functools.partial(per_chunk, src=shifted if b else ext, ks=ks, first=first))
        first = False


def _ffn_fwd(hb, xin, wg, wdw, bdw, wd, gn, tm, tps, ffn_taps, name, target=None, comm=None):
    T, D = xin.shape
    ns, _, nc = wg.shape
    F = ns * nc // 2
    hf = HALO_FFN
    hh = 2 * HALO_FFN
    final = target is not None

    def body(*refs):
        if final:
            (h_ref, hh_ref, x_ref, wg_ref, wdw_ref, bdw_ref, wd_ref, gn_ref, tg_ref,
             up_ref, gb_ref, dx_ref, loss_ref, dgn_ref, aext, realigned) = refs
        else:
            (h_ref, hh_ref, x_ref, wg_ref, wdw_ref, bdw_ref, wd_ref, gn_ref,
             up_ref, gb_ref, xo_ref, ho_ref, aext, realigned) = refs
        i = pl.program_id(0)
        hv = h_ref[...]
        for s in range(ns):
            up_ref[:, s * nc:(s + 1) * nc] = _dot(hv, wg_ref[s])
        halo = jnp.concatenate([_dot(hh_ref[...], wg_ref[s]) for s in range(ns // 2)], axis=1)[hh - hf:hh, :]
        aext[0:hf, :] = jnp.where(i % tps == 0, 0.0, halo)
        aext[hf:hf + tm, :] = up_ref[:, 0:F]
        acv = _ffn_conv(_ffn_taps(aext, realigned, tm, ffn_taps), wdw_ref, bdw_ref)
        gb = (acv * _sig(acv) * up_ref[:, F:2 * F]).astype(BF16)
        gb_ref[...] = gb.T
        xo = x_ref[...] + _dot(gb, wd_ref[...])
        if not final:
            xo_ref[...] = xo
            ho_ref[...] = _rms(xo, gn_ref[...])[0]
        else:
            out, xh, r = _rms(xo, gn_ref[...])
            diff = out - tg_ref[...]
            part = 0.5 / D * _colsum(diff * diff)
            dx, dg = _rms_bwd(diff * (1.0 / D), xh, r, gn_ref[...])
            dx_ref[...] = dx
            _accumulate(loss_ref, part, i == 0)
            _accumulate(dgn_ref, dg, i == 0)

    in_specs = [_rows(tm, D), _prev_halo(tm, hh, D), _rows(tm, D), _full(wg.shape), _full(wdw.shape), _full((1, F)),
                _full((F, D)), _full((1, D))]
    args = [hb, hb, xin, wg, wdw, bdw, wd, gn]
    if final:
        in_specs.append(_rows(tm, D))
        args.append(target)
        out_specs = [_rows(tm, 2 * F), _cols(F, tm), _rows(tm, D), _acc_spec(1, D), _acc_spec(1, D)]
        out_shape = [S_((T, 2 * F), F32), S_((F, T), BF16), S_((T, D), F32), S_((1, D), F32), S_((1, D), F32)]
    else:
        out_specs = [_rows(tm, 2 * F), _cols(F, tm), _rows(tm, D), _rows(tm, D)]
        out_shape = [S_((T, 2 * F), F32), S_((F, T), BF16), S_((T, D), F32), S_((T, D), F32)]
    return _pcall(body, name=name, grid=(T // tm,), in_specs=in_specs, out_specs=out_specs, out_shape=out_shape,
                  scratch=[pltpu.VMEM((tm + hf, F), F32), pltpu.VMEM((ffn_taps - 1, tm, F), F32)], args=args, comm=comm)


def _pool_counts(i, tps, tm, w):
    pos = (i % tps) * tm + lax.broadcasted_iota(jnp.int32, (tm, 1), 0)
    return jnp.minimum(pos + 1, w).astype(F32)


def _window_sums(src, buf_a, buf_b, cg, causal):
    assert POOL_WINDOWS == (2, 4, 8, 16)
    n = src.shape[0]
    levels = len(POOL_WINDOWS)
    cur, outs = src, []
    for l in range(levels):
        dst = buf_b if l % 2 else buf_a
        cols = slice(l * cg, levels * cg)
        lo, m = 8 * (l + 1), n - 8 * (l + 1)
        if causal:
            dst[lo:n, cols] = cur[lo:n, cols] + cur[pl.ds(lo - (1 << l), m), cols]
        else:
            dst[0:m, cols] = cur[0:m, cols] + cur[pl.ds(1 << l, m), cols]
        outs.append(dst)
        cur = dst
    return outs


def _pool_fwd(h, x2, pw, pb, ps, gn, tm, tps, comm=None):
    T, D = h.shape
    ng, cg, _ = pw.shape
    hp = HALO_POOL

    def body(h_ref, hh_ref, x_ref, pw_ref, pb_ref, ps_ref, gn_ref, pl_ref, xo_ref, hb_ref, hbt_ref, hext, buf_a, buf_b):
        i = pl.program_id(0)
        hext[0:hp, :] = jnp.where(i % tps == 0, 0.0, hh_ref[...])
        hext[hp:hp + tm, :] = h_ref[...]
        sums = _window_sums(hext, buf_a, buf_b, cg, True)
        for g, w in enumerate(POOL_WINDOWS):
            cs = slice(g * cg, (g + 1) * cg)
            pl_ref[:, cs] = (sums[g][hp:hp + tm, cs] / _pool_counts(i, tps, tm, w) - h_ref[:, cs]).astype(BF16)

        pv = pl_ref[...]
        mixed = jnp.concatenate([_dot(pv[:, g * cg:(g + 1) * cg], pw_ref[g]) for g in range(ng)], axis=1)
        xo = x_ref[...] + ps_ref[...] * (mixed + pb_ref[...])
        xo_ref[...] = xo
        hb = _rms(xo, gn_ref[...])[0].astype(BF16)
        hb_ref[...] = hb
        hbt_ref[...] = hb.T

    return _pcall(body, name="pool_fwd", grid=(T // tm,),
                  in_specs=[_rows(tm, D), _prev_halo(tm, hp, D), _rows(tm, D), _full(pw.shape), _full((1, D)), _full((1, D)),
                            _full((1, D))],
                  out_specs=[_rows(tm, D), _rows(tm, D), _rows(tm, D), _cols(D, tm)],
                  out_shape=[S_((T, D), BF16), S_((T, D), F32), S_((T, D), BF16), S_((D, T), BF16)],
                  scratch=[pltpu.VMEM((tm + hp, D), F32)] * 3, args=(h, h, x2, pw, pb, ps, gn), comm=comm)


def _bwd_down(dx, up, wdw, bdw, wd, tm, tps, ffn_taps, name, comm=None):
    T, D = dx.shape
    F = up.shape[1] // 2
    hf = HALO_FFN
    nt = T // tm

    def body(dx_ref, up_ref, uph_ref, wdw_ref, bdw_ref, wd_ref, dac_ref, dgate_ref, wsum_ref, aext, realigned):
        i = pl.program_id(0)
        aext[0:hf, :] = jnp.where(i % tps == 0, 0.0, uph_ref[...])
        aext[hf:hf + tm, :] = up_ref[:, 0:F]
        taps = _ffn_taps(aext, realigned, tm, ffn_taps)
        acv = _ffn_conv(taps, wdw_ref, bdw_ref)
        dg = _dot_nt(dx_ref[...].astype(BF16), wd_ref[...])
        sg = _sig(acv)
        dgate_ref[...] = (dg * acv * sg).astype(BF16)
        dac = dg * up_ref[:, F:2 * F] * (sg * (1.0 + acv * (1.0 - sg)))
        dac_ref[...] = dac
        rows = [_colsum(dac * tap[...]) for tap in taps]
        rows += [_colsum(dac), jnp.zeros((8 - ffn_taps - 1, F), F32)]
        _accumulate(wsum_ref, jnp.concatenate(rows, axis=0), i == 0)

    return _pcall(body, name=name, grid=(nt,),
                  in_specs=[_rows(tm, D), _rows(tm, 2 * F), _prev_halo(tm, hf, F), _full(wdw.shape), _full((1, F)),
                            _full((F, D))],
                  out_specs=[_rows(tm, F), _rows(tm, F, col=1), _acc_spec(8, F)],
                  out_shape=[S_((T, F), F32), S_((T, 2 * F), BF16), S_((8, F), F32)],
                  scratch=[pltpu.VMEM((tm + hf, F), F32), pltpu.VMEM((ffn_taps - 1, tm, F), F32)],
                  args=(dx, up, up, wdw, bdw, wd), comm=comm)


def _bwd_up(dac, dup, wg, wdw, xin, gn, dx, tm, tps, ffn_taps, name, comm=None):
    T, D = xin.shape
    F = dac.shape[1]
    ns, _, nc = wg.shape
    hf = HALO_FFN
    nt = T // tm

    def body(dac_ref, dach_ref, dgate_ref, wg_ref, wdw_ref, x_ref, gn_ref, dx_ref, dact_ref, dxo_ref, dgn_ref, dext):
        i = pl.program_id(0)
        dext[0:tm, :] = dac_ref[...]
        dext[tm:tm + hf, :] = jnp.where(i % tps == tps - 1, 0.0, dach_ref[...])
        dact = wdw_ref[ffn_taps - 1:ffn_taps, :] * dac_ref[...]
        for k in range(ffn_taps - 1):
            dact = dact + wdw_ref[k:k + 1, :] * dext[pl.ds(ffn_taps - 1 - k, tm), :]
        dact_ref[...] = dact.astype(BF16)

        dh = jnp.zeros((tm, D), F32)
        for s in range(ns):
            src = dact_ref if s < ns // 2 else dgate_ref
            o = (s % (ns // 2)) * nc
            dh = dh + _dot_nt(src[:, o:o + nc], wg_ref[s])
        _, xh, r = _rms(x_ref[...], gn_ref[...])
        dxn, dg = _rms_bwd(dh, xh, r, gn_ref[...])
        dxo_ref[...] = dx_ref[...] + dxn
        _accumulate(dgn_ref, dg, i == 0)

    return _pcall(body, name=name, grid=(nt,),
                  in_specs=[_rows(tm, F), _next_halo(tm, hf, F, T), _rows(tm, F, col=1), _full(wg.shape), _full(wdw.shape),
                            _rows(tm, D), _full((1, D)), _rows(tm, D)],
                  out_specs=[_rows(tm, F), _rows(tm, D), _acc_spec(1, D)],
                  out_shape=[S_((T, 2 * F), BF16), S_((T, D), F32), S_((1, D), F32)],
                  scratch=[pltpu.VMEM((tm + hf, F), F32)],
                  aliases={2: 0}, args=(dac, dac, dup, wg, wdw, xin, gn, dx), comm=comm)


def _mm_tn(xt, dy, bk, bn, tt, shard_major, name, split=1, comm=None):
    K, T = xt.shape
    N = dy.shape[1]
    nt = T // tt
    w = bn // split

    def body(x_ref, dy_ref, o_ref, acc):
        t = pl.program_id(2)
        part = _dot(x_ref[...], dy_ref[...].astype(BF16))
        _accumulate(acc, part, t == 0)

        @pl.when(t == nt - 1)
        def _():
            if shard_major:
                for p in range(split):
                    o_ref[p] = acc[:, p * w:(p + 1) * w].astype(o_ref.dtype)
            else:
                o_ref[...] = acc[...].astype(o_ref.dtype)

    if shard_major:
        out_spec = pl.BlockSpec((split, bk, w), lambda i, j, t: (j, i, 0))
        out_shape = S_((N // w, K, w), BF16)
    else:
        out_spec = pl.BlockSpec((bk, bn), lambda i, j, t: (i, j))
        out_shape = S_((K, N), BF16)
    return _pcall(body, name=name, grid=(K // bk, N // bn, nt),
                  in_specs=[pl.BlockSpec((bk, tt), lambda i, j, t: (i, t)), pl.BlockSpec((tt, bn), lambda i, j, t: (t, j))],
                  out_specs=out_spec, out_shape=out_shape, scratch=[pltpu.VMEM((bk, bn), F32)], args=(xt, dy), comm=comm)


def _pool_bwd(dx3, plb, x2, pw, pb, ps, gn, tm, tps, comm=None):
    T, D = x2.shape
    ng, cg, _ = pw.shape
    hp = HALO_POOL
    nt = T // tm

    def body(do_ref, doh_ref, pl_ref, x_ref, pw_ref, pb_ref, ps_ref, gn_ref, dxo_ref, dpw_ref, sm_ref, qext, buf_a, buf_b,
             dh_s):
        i = pl.program_id(0)
        do = do_ref[...]
        dm = do * ps_ref[...]
        dmh = jnp.where(i % tps == tps - 1, 0.0, doh_ref[...]) * ps_ref[...]
        pv = pl_ref[...]
        mixed = jnp.concatenate([_dot(pv[:, g * cg:(g + 1) * cg], pw_ref[g]) for g in range(ng)], axis=1)
        dscale = _colsum(do * (mixed + pb_ref[...]))
        dbias = _colsum(dm)
        dmb = dm.astype(BF16)
        dmhb = dmh.astype(BF16)
        for g, w in enumerate(POOL_WINDOWS):
            cs = slice(g * cg, (g + 1) * cg)
            _accumulate(dpw_ref.at[g], _dot_tn(pv[:, cs], dmb[:, cs]), i == 0)
            dpo = _dot_nt(dmb[:, cs], pw_ref[g])
            dh_s[:, cs] = dpo
            qext[0:tm, cs] = dpo / _pool_counts(i, tps, tm, w)
            qext[tm:tm + hp, cs] = _dot_nt(dmhb[:, cs], pw_ref[g]) * (1.0 / w)
        sums = _window_sums(qext, buf_a, buf_b, cg, False)
        for g in range(ng):
            cs = slice(g * cg, (g + 1) * cg)
            dh_s[:, cs] = sums[g][0:tm, cs] - dh_s[:, cs]

        _, xh, r = _rms(x_ref[...], gn_ref[...])
        dxn, dg = _rms_bwd(dh_s[...], xh, r, gn_ref[...])
        dxo_ref[...] = do + dxn
        sm = jnp.concatenate([dscale, dbias, dg, jnp.zeros((5, D), F32)], axis=0)
        _accumulate(sm_ref, sm, i == 0)

    return _pcall(body, name="pool_bwd", grid=(nt,),
                  in_specs=[_rows(tm, D), _next_halo(tm, hp, D, T), _rows(tm, D), _rows(tm, D), _full(pw.shape), _full((1, D)),
                            _full((1, D)), _full((1, D))],
                  out_specs=[_rows(tm, D), pl.BlockSpec((ng, cg, cg), lambda i: (0, 0, 0)), _acc_spec(8, D)],
                  out_shape=[S_((T, D), F32), S_((ng, cg, cg), F32), S_((8, D), F32)],
                  scratch=[pltpu.VMEM((tm + hp, D), F32)] * 3 + [pltpu.VMEM((tm, D), F32)],
                  args=(dx3, dx3, plb, x2, pw, pb, ps, gn), comm=comm)


def _conv_bwd1(dx1, c, w2, lng, lnb, tm, comm=None):
    T, D = dx1.shape

    def body(do_ref, c_ref, w2_ref, lng_ref, lnb_ref, dc_ref, sm_ref):
        i = pl.program_id(0)
        do = do_ref[...]
        ds = _dot_nt(do.astype(BF16), w2_ref[...])
        cv = c_ref[...]
        xc = cv - jnp.mean(cv, axis=-1, keepdims=True)
        rstd = lax.rsqrt(jnp.mean(xc * xc, axis=-1, keepdims=True) + LN_EPS)
        nh = xc * rstd
        n = nh * lng_ref[...] + lnb_ref[...]
        sg = _sig(n)
        dn = ds * (sg * (1.0 + n * (1.0 - sg)))
        dnh = dn * lng_ref[...]
        dc = rstd * (dnh - jnp.mean(dnh, axis=-1, keepdims=True) - nh * jnp.mean(dnh * nh, axis=-1, keepdims=True))
        dc_ref[...] = dc
        sm = jnp.concatenate([_colsum(dn * nh), _colsum(dn), _colsum(do), _colsum(dc), jnp.zeros((4, D), F32)], axis=0)
        _accumulate(sm_ref, sm, i == 0)

    return _pcall(body, name="conv_bwd1", grid=(T // tm,),
                  in_specs=[_rows(tm, D), _rows(tm, D), _full((D, D)), _full((1, D)), _full((1, D))],
                  out_specs=[_rows(tm, D), _acc_spec(8, D)],
                  out_shape=[S_((T, D), F32), S_((8, D), F32)], args=(dx1, c, w2, lng, lnb), comm=comm)


def _conv_bwd2(dc, a, x, w1g, wdw, gn, dx1, tm, tps, taps, comm=None):
    T, D = x.shape
    ns, _, nc = w1g.shape
    hc = HALO_CONV
    nt = T // tm
    tp = wdw.shape[0]

    def body(dc_ref, dch_ref, a_ref, ah_ref, x_ref, w_ref, wdw_ref, gn_ref, dx1_ref,
             gx_ref, dab_ref, dw_ref, db1_ref, dgn_ref, uext, dext, shifted, du_s, wacc):
        i = pl.program_id(0)
        ah = ah_ref[...]
        uext[0:hc, :] = jnp.where(i % tps == 0, 0.0, ah[:, :D] * _sig(ah[:, D:]))
        av = a_ref[...]
        sg2 = _sig(av[:, D:])
        uext[hc:hc + tm, :] = av[:, :D] * sg2
        dext[0:tm, :] = dc_ref[...]
        dext[tm:tm + hc, :] = jnp.where(i % tps == tps - 1, 0.0, dch_ref[...])

        @pl.when(i == 0)
        def _():
            wacc[...] = jnp.zeros_like(wacc)

        def wgrad(r0, src, ks, first):
            d = dc_ref[pl.ds(r0, CHUNK), :]
            for k, o in ks:
                wacc[8 * k:8 * k + 8, :] += _fold8(d * src[pl.ds(pl.multiple_of(r0 + o, 8), CHUNK), :])
        _by_sublane_shift(uext, shifted, tm, [hc - (taps - 1) + k for k in range(taps)], wgrad)

        def convt(r0, src, ks, first):
            acc = jnp.zeros((CHUNK, D), F32) if first else du_s[pl.ds(r0, CHUNK), :]
            for k, o in ks:
                acc = acc + wdw_ref[k:k + 1, :] * src[pl.ds(pl.multiple_of(r0 + o, 8), CHUNK), :]
            du_s[pl.ds(r0, CHUNK), :] = acc
        _by_sublane_shift(dext, shifted, tm, [taps - 1 - k for k in range(taps)], convt)

        du = du_s[...]
        da1 = du * sg2
        da2 = du * av[:, :D] * (sg2 * (1.0 - sg2))
        dab_ref[:, 0:D] = da1.astype(BF16)
        dab_ref[:, D:2 * D] = da2.astype(BF16)
        _accumulate(db1_ref, jnp.concatenate([_colsum(da1), _colsum(da2)], axis=0), i == 0)
        dh = jnp.zeros((tm, D), F32)
        for s in range(ns):
            dh = dh + _dot_nt(dab_ref[:, s * nc:(s + 1) * nc], w_ref[s])
        _, xh, r = _rms(x_ref[...], gn_ref[...])
        dxn, dg = _rms_bwd(dh, xh, r, gn_ref[...])
        gx_ref[...] = dx1_ref[...] + dxn
        _accumulate(dgn_ref, dg, i == 0)

        @pl.when(i == nt - 1)
        def _():
            dw_ref[...] = jnp.zeros_like(dw_ref)
            for k in range(taps):
                dw_ref[k:k + 1, :] = _colsum(wacc[8 * k:8 * k + 8, :])

    return _pcall(body, name="conv_bwd2", grid=(nt,),
                  in_specs=[_rows(tm, D), _next_halo(tm, hc, D, T), _rows(tm, 2 * D), _prev_halo(tm, hc, 2 * D), _rows(tm, D),
                            _full(w1g.shape), _full(wdw.shape), _full((1, D)), _rows(tm, D)],
                  out_specs=[_rows(tm, D), _rows(tm, 2 * D), _acc_spec(tp, D), _acc_spec(2, D), _acc_spec(1, D)],
                  out_shape=[S_((T, D), F32), S_((T, 2 * D), BF16), S_((tp, D), F32), S_((2, D), F32), S_((1, D), F32)],
                  scratch=[pltpu.VMEM((tm + hc, D), F32), pltpu.VMEM((tm + hc, D), F32), pltpu.VMEM((tm + hc - 8, D), F32),
                           pltpu.VMEM((tm, D), F32), pltpu.VMEM((8 * taps, D), F32)],
                  args=(dc, dc, a, a, x, w1g, wdw, gn, dx1), comm=comm)


def _place():
    x, y, c = lax.axis_index("x"), lax.axis_index("y"), lax.axis_index("c")
    chips = [(1 - x, y), (x, 1 - y), (1 - x, 1 - y)]
    return x, y, c, chips


class _GatherComm:
    def __init__(self, slots):
        n = len(slots)
        self.inputs = list(slots)
        self.out_shape = [S_(s.shape, s.dtype) for s in slots]
        self.aliases = {i: i for i in range(n)}
        self.sems = [pltpu.SemaphoreType.DMA((n, 3))] * 4
        self.halves = [s.shape[1] // 2 for s in slots]
        self.paired = [s.shape[0] == N_CHIPS // 2 for s in slots]
        self.widths = [s.shape[2] // 2 if p else s.shape[2] for s, p in zip(slots, self.paired)]

    def _copies(self, outs, sems):
        s_ici, r_ici, s_d2d, r_d2d = sems
        x, y, c, chips = _place()
        peers = [(j, chip, 2 * chip[0] + chip[1]) for j, chip in enumerate(chips)]

        def block(i, chip, h):
            rows = pl.ds(pl.multiple_of(h * self.halves[i], 8), self.halves[i])
            if self.paired[i]:
                return outs[i].at[chip // 2, rows, pl.ds(pl.multiple_of((chip % 2) * self.widths[i], 128), self.widths[i])]
            return outs[i].at[chip, rows]

        def ici(i, j, chip, to):
            blk = block(i, chip, c)
            return pltpu.make_async_remote_copy(src_ref=blk, dst_ref=blk, send_sem=s_ici.at[i, j], recv_sem=r_ici.at[i, j],
                                                device_id=to, device_id_type=MESH)

        def d2d(i, j, chip, h):
            blk = block(i, chip, h)
            return pltpu.make_async_remote_copy(src_ref=blk, dst_ref=blk, send_sem=s_d2d.at[i, j], recv_sem=r_d2d.at[i, j],
                                                device_id=(x, y, 1 - c), device_id_type=MESH)
        return (x, y, c, 2 * x + y), peers, ici, d2d

    def before(self, cin, outs, sems, step, total):
        (x, y, c, q), peers, ici, d2d = self._copies(outs, sems)

        @pl.when(step == 0)
        def _():
            for i in range(len(outs)):
                for j, chip, _ in peers:
                    ici(i, j, q, (*chip, c)).start()

    def after(self, cin, outs, sems, step, total):
        (x, y, c, q), peers, ici, d2d = self._copies(outs, sems)

        @pl.when(step == max(total - 2, 0))
        def _():
            for i in range(len(outs)):
                for j, chip, qj in peers:
                    ici(i, j, qj, (x, y, c)).wait_recv()
                    d2d(i, j, qj, c).start()

        @pl.when(step == total - 1)
        def _():
            for i in range(len(outs)):
                for j, chip, qj in peers:
                    d2d(i, j, qj, 1 - c).wait_recv()
            for i in range(len(outs)):
                for j, chip, qj in peers:
                    ici(i, j, q, (*chip, c)).wait_send()
                    d2d(i, j, qj, c).wait_send()


class _ExchangeComm:
    def __init__(self, sums):
        n = len(sums)
        self.inputs = list(sums)
        self.out_shape = [S_((3,) + s.shape[1:], s.dtype) for s in sums]
        self.aliases = {}
        self.sems = [pltpu.SemaphoreType.DMA((n, 3))] * 2

    def _copies(self, srcs, outs, sems):
        s_sem, r_sem = sems
        x, y, c, chips = _place()
        return [pltpu.make_async_remote_copy(src_ref=srcs[i].at[2 * chip[0] + chip[1]], dst_ref=outs[i].at[j],
                                             send_sem=s_sem.at[i, j], recv_sem=r_sem.at[i, j],
                                             device_id=(*chip, c), device_id_type=MESH)
                for i in range(len(srcs)) for j, chip in enumerate(chips)]

    def before(self, srcs, outs, sems, step, total):
        @pl.when(step == 0)
        def _():
            for cp in self._copies(srcs, outs, sems):
                cp.start()

    def after(self, srcs, outs, sems, step, total):
        @pl.when(step == total - 1)
        def _():
            for cp in self._copies(srcs, outs, sems):
                cp.wait()


def _own_slots(specs):
    n = len(specs)
    shapes, blocks, count = [], [], []
    for a, (w, layer, dtype, pairs) in enumerate(specs):
        _, r, c = w.shape
        br = _row_block(r, c, 1 << 21)
        shapes.append(S_((N_CHIPS // 2, r, 2 * c) if pairs else (N_CHIPS, r, c), dtype))
        blocks += [(a, k, k * br, br) for k in range(r // br)]
        count.append(r // br)
    gather = _GatherComm(shapes[:1])
    n_sems = len(gather.sems)

    def body(*refs):
        srcs, outs = refs[:n], refs[n:2 * n]
        inb, outb = refs[2 * n:3 * n], refs[3 * n:4 * n]
        lsem, ssem = refs[4 * n], refs[4 * n + 1]
        gsems = refs[4 * n + 2:4 * n + 2 + n_sems]
        x, y, c, _ = _place()
        q = 2 * x + y

        def load(a, k, r0, br):
            return pltpu.make_async_copy(srcs[a].at[specs[a][1], pl.ds(r0, br)], inb[a].at[k % 2], lsem.at[a, k % 2])

        def store(a, k, r0, br):
            if specs[a][3]:
                wc = specs[a][0].shape[2]
                dst = outs[a].at[q // 2, pl.ds(r0, br), pl.ds(pl.multiple_of((q % 2) * wc, 128), wc)]
            else:
                dst = outs[a].at[q, pl.ds(r0, br)]
            return pltpu.make_async_copy(outb[a].at[k % 2], dst, ssem.at[a, k % 2])

        load(*blocks[0]).start()
        for b, (a, k, r0, br) in enumerate(blocks):
            if b + 1 < len(blocks):
                load(*blocks[b + 1]).start()
            load(a, k, r0, br).wait()
            if k >= 2:
                store(a, k - 2, r0 - 2 * br, br).wait()
            outb[a][k % 2] = inb[a][k % 2].astype(specs[a][2])
            store(a, k, r0, br).start()
            if a == 0 and k == count[0] - 1:
                for kk in range(max(k - 1, 0), k + 1):
                    store(0, kk, kk * br, br).wait()
                gather.before(None, outs[:1], gsems, 0, 1)
        for a in range(1, n):
            br = blocks[sum(count[:a])][3]
            for kk in range(max(count[a] - 2, 0), count[a]):
                store(a, kk, kk * br, br).wait()
        gather.after(None, outs[:1], gsems, 0, 1)

    any_spec = pl.BlockSpec(memory_space=pl.ANY)
    scratch = ([pltpu.VMEM((2, blocks[sum(count[:a])][3], specs[a][0].shape[2]), F32) for a in range(n)]
               + [pltpu.VMEM((2, blocks[sum(count[:a])][3], specs[a][0].shape[2]), specs[a][2]) for a in range(n)]
               + [pltpu.SemaphoreType.DMA((n, 2)), pltpu.SemaphoreType.DMA((n, 2))] + list(gather.sems))
    outs = pl.pallas_call(body, name="own_slots", in_specs=[any_spec] * n, out_specs=[any_spec] * n, out_shape=shapes,
                          scratch_shapes=scratch,
                          compiler_params=pltpu.CompilerParams(vmem_limit_bytes=VMEM_LIMIT_BYTES))(*[s[0] for s in specs])
    return list(outs)


def _comm_only(comm, name):
    def body():
        pass
    outs = _pcall(body, name=name, grid=(1,), in_specs=[], out_specs=[], out_shape=[], args=[], comm=comm)
    return list(outs)


def _pair_exchange(parts, name):
    n = len(parts)

    def body(*refs):
        srcs, outs = refs[:n], refs[n:2 * n]
        s_sem, r_sem = refs[2 * n:]
        x, y, c, _ = _place()
        cps = []
        for i in range(n):
            for s in range(N_CHIPS):
                cp = pltpu.make_async_remote_copy(src_ref=srcs[i].at[s, 1 - c], dst_ref=outs[i].at[s], send_sem=s_sem.at[i, s],
                                                  recv_sem=r_sem.at[i, s], device_id=(x, y, 1 - c), device_id_type=MESH)
                cp.start()
                cps.append(cp)
        for cp in cps:
            cp.wait()

    any_spec = pl.BlockSpec(memory_space=pl.ANY)
    outs = pl.pallas_call(
        body, name=name, in_specs=[any_spec] * n, out_specs=[any_spec] * n,
        out_shape=[S_((N_CHIPS,) + p.shape[2:], p.dtype) for p in parts],
        scratch_shapes=[pltpu.SemaphoreType.DMA((n, N_CHIPS))] * 2,
    )(*parts)
    return list(outs)


class _PairComm:
    def __init__(self, parts):
        n = len(parts)
        self.inputs = list(parts)
        self.out_shape = [S_((N_CHIPS,) + p.shape[2:], p.dtype) for p in parts]
        self.aliases = {}
        self.sems = [pltpu.SemaphoreType.DMA((n, N_CHIPS))] * 2

    def _copies(self, srcs, outs, sems):
        s_sem, r_sem = sems
        x, y, c, _ = _place()
        return [pltpu.make_async_remote_copy(src_ref=srcs[i].at[s, 1 - c], dst_ref=outs[i].at[s], send_sem=s_sem.at[i, s],
                                             recv_sem=r_sem.at[i, s], device_id=(x, y, 1 - c), device_id_type=MESH)
                for i in range(len(srcs)) for s in range(N_CHIPS)]

    def before(self, srcs, outs, sems, step, total):
        @pl.when(step == 0)
        def _():
            for cp in self._copies(srcs, outs, sems):
                cp.start()

    def after(self, srcs, outs, sems, step, total):
        @pl.when(step == total - 1)
        def _():
            for cp in self._copies(srcs, outs, sems):
                cp.wait()


class _ShareComm:
    def __init__(self, bufs):
        n = len(bufs)
        self.inputs = list(bufs)
        self.out_shape = [S_(b.shape, b.dtype) for b in bufs]
        self.aliases = {i: i for i in range(n)}
        self.sems = [pltpu.SemaphoreType.DMA((n, max(b.shape[0] for b in bufs)))] * 2
        self.layers = [b.shape[0] for b in bufs]

    def _copies(self, outs, sems):
        s_sem, r_sem = sems
        x, y, c, _ = _place()
        cps = []
        for i, nl in enumerate(self.layers):
            for l in range(nl):
                blk = outs[i].at[l, c]
                cps.append(pltpu.make_async_remote_copy(src_ref=blk, dst_ref=blk, send_sem=s_sem.at[i, l],
                                                        recv_sem=r_sem.at[i, l], device_id=(x, y, 1 - c),
                                                        device_id_type=MESH))
        return cps

    def before(self, cin, outs, sems, step, total):
        @pl.when(step == 0)
        def _():
            for cp in self._copies(outs, sems):
                cp.start()

    def after(self, cin, outs, sems, step, total):
        @pl.when(step == total - 1)
        def _():
            for cp in self._copies(outs, sems):
                cp.wait()


class _Both:
    def __init__(self, a, b):
        self.parts = (a, b)
        self.inputs = a.inputs + b.inputs
        self.out_shape = a.out_shape + b.out_shape
        self.aliases = dict(a.aliases)
        self.aliases.update({len(a.inputs) + i: len(a.out_shape) + o for i, o in b.aliases.items()})
        self.sems = a.sems + b.sems

    def _each(self, phase, cin, cout, sems, step, total):
        i = o = s = 0
        for p in self.parts:
            ni, no, ns = len(p.inputs), len(p.out_shape), len(p.sems)
            getattr(p, phase)(cin[i:i + ni], cout[o:o + no], sems[s:s + ns], step, total)
            i, o, s = i + ni, o + no, s + ns

    def before(self, cin, cout, sems, step, total):
        self._each("before", cin, cout, sems, step, total)

    def after(self, cin, cout, sems, step, total):
        self._each("after", cin, cout, sems, step, total)


def _small_allreduce(groups, comm):
    n = len(groups)
    nci, nco, ncs = len(comm.inputs), len(comm.out_shape), len(comm.sems)
    arrays = []
    for grp in groups:
        for p in grp:
            if not isinstance(p, int) and not any(p[0] is a for a in arrays):
                arrays.append(p[0])
    shapes = []
    for grp in groups:
        rows = sum(p if isinstance(p, int) else p[2] - p[1] for p in grp)
        width = next(p[0].shape[1] for p in grp if not isinstance(p, int))
        assert rows % 8 == 0
        shapes.append((rows, width))
    na = len(arrays)

    def body(*refs):
        a_refs, cin = refs[:na], refs[na:na + nci]
        o_refs, cout = refs[na + nci:na + nci + n], refs[na + nci + n:na + nci + n + nco]
        rest = refs[na + nci + n + nco:]
        v_refs, sibs, css, gots = rest[:n], rest[n:2 * n], rest[2 * n:3 * n], rest[3 * n:4 * n]
        s_sem, r_sem = rest[4 * n:4 * n + 2]
        csems = rest[4 * n + 2:]
        comm.before(cin, cout, csems, 0, 1)
        for i, grp in enumerate(groups):
            o = 0
            for p in grp:
                if isinstance(p, int):
                    v_refs[i][o:o + p, :] = jnp.zeros((p, shapes[i][1]), F32)
                    o += p
                else:
                    k = next(t for t, a in enumerate(arrays) if a is p[0])
                    v_refs[i][o:o + p[2] - p[1], :] = a_refs[k][p[1]:p[2], :]
                    o += p[2] - p[1]
        x, y, c, chips = _place()
        q = 2 * x + y
        to_sib = [pltpu.make_async_remote_copy(src_ref=v_refs[i], dst_ref=sibs[i], send_sem=s_sem.at[i, 3], recv_sem=r_sem.at[i, 3],
                                               device_id=(x, y, 1 - c), device_id_type=MESH) for i in range(n)]
        for cp in to_sib:
            cp.start()
        cps = []
        for i in range(n):
            to_sib[i].wait()
            mine, other = v_refs[i][...], sibs[i][...]
            css[i][...] = jnp.where(c == 0, mine, other) + jnp.where(c == 0, other, mine)
            for j, chip in enumerate(chips):
                cp = pltpu.make_async_remote_copy(src_ref=css[i], dst_ref=gots[i].at[j], send_sem=s_sem.at[i, j],
                                                  recv_sem=r_sem.at[i, j], device_id=(*chip, c), device_id_type=MESH)
                cp.start()
                cps.append(cp)
        for cp in cps:
            cp.wait()
        flips = [2, 1, 3]
        for i in range(n):
            total = None
            for k in range(N_CHIPS):
                d = q ^ k
                term = jnp.where(d == 0, css[i][...], 0.0)
                for j in range(3):
                    term = jnp.where(d == flips[j], gots[i][j], term)
                total = term if total is None else total + term
            o_refs[i][...] = total
        comm.after(cin, cout, csems, 0, 1)

    vm, any_spec = pl.BlockSpec(memory_space=pltpu.VMEM), pl.BlockSpec(memory_space=pl.ANY)
    outs = pl.pallas_call(
        body, name="small_allreduce", in_specs=[vm] * na + [any_spec] * nci, out_specs=[vm] * n + [any_spec] * nco,
        out_shape=[S_(s, F32) for s in shapes] + list(comm.out_shape),
        scratch_shapes=[pltpu.VMEM(s, F32) for s in shapes] * 3 + [pltpu.VMEM((3,) + s, F32) for s in shapes]
        + [pltpu.SemaphoreType.DMA((n, 4)), pltpu.SemaphoreType.DMA((n, 4))] + list(comm.sems),
    )(*arrays, *comm.inputs)
    return list(outs[:n]), list(outs[n:])


def _row_block(r, c, target_bytes=1 << 20):
    br = r
    while br % 32 == 0 and br * c * 4 > target_bytes:
        br //= 2
    return br


def _chip_sum(part, sib, ids, name):
    _, _, r, c = part.shape
    br = _row_block(r, c, 1 << 22)

    def body(ids_ref, p_ref, s_ref, o_ref):
        o_ref[...] = (p_ref[...].astype(F32) + s_ref[...].astype(F32)).astype(BF16)

    return _pcall(body, name=name, grid=(N_CHIPS, r // br), prefetch=1,
                  in_specs=[pl.BlockSpec((None, None, br, c), lambda s, i, ids: (s, ids[1], i, 0)),
                            pl.BlockSpec((None, br, c), lambda s, i, ids: (s, i, 0))],
                  out_specs=pl.BlockSpec((None, br, c), lambda s, i, ids: (s, i, 0)),
                  out_shape=S_((N_CHIPS, r, c), BF16), args=(ids, part, sib))


def _final_sum(part, sib, got, ids, name, layer=0, n_layers=1, buf=None):
    _, r, c = got.shape
    br = _row_block(r, c, 1 << 21)

    def body(ids_ref, p_ref, s_ref, got_ref, *rest):
        v = p_ref[...].astype(F32) + s_ref[...].astype(F32)
        for j in range(3):
            v = v + got_ref[j].astype(F32)
        rest[-1][...] = v

    in_specs = [pl.BlockSpec((None, None, br, c), lambda i, ids: (ids[0], ids[1], i, 0)),
                pl.BlockSpec((None, br, c), lambda i, ids: (ids[0], i, 0)),
                pl.BlockSpec((3, br, c), lambda i, ids: (0, i, 0))]
    args = [ids, part, sib, got]
    if buf is not None:
        in_specs.append(pl.BlockSpec(memory_space=pl.ANY))
        args.append(buf)
    return _pcall(body, name=name, grid=(r // br,), prefetch=1, in_specs=in_specs,
                  out_specs=pl.BlockSpec((None, None, br, c), lambda i, ids: (layer, ids[1], i, 0)),
                  out_shape=S_((n_layers, 2, r, c), F32), args=args, aliases={4: 0} if buf is not None else None)


def _adam_math(w, g, m, v):
    c1 = 1.0 / (1.0 - ADAM_B1 ** ADAM_STEP)
    c2 = 1.0 / (1.0 - ADAM_B2 ** ADAM_STEP)
    mn = ADAM_B1 * m + (1.0 - ADAM_B1) * g
    vn = ADAM_B2 * v + (1.0 - ADAM_B2) * (g * g)
    return -ADAM_LR * ((mn * c1) / (jnp.sqrt(vn * c2) + ADAM_EPS) + ADAM_WD * w), mn, vn


def _adamw(w, g, m, v, name):
    nl, r, c = w.shape
    br = _row_block(r, c, 1 << 20)

    def body(w_ref, g_ref, m_ref, v_ref, go_ref, d_ref, mo_ref, vo_ref):
        gv = g_ref[...]
        go_ref[...] = gv
        d_ref[...], mo_ref[...], vo_ref[...] = _adam_math(w_ref[...], gv, m_ref[...], v_ref[...])

    blk = pl.BlockSpec((None, br, c), lambda l, i: (l, i, 0))
    return _pcall(body, name=name, grid=(nl, r // br), in_specs=[blk] * 4, out_specs=[blk] * 4,
                  out_shape=[S_((nl, r, c), F32)] * 4, args=(w, g, m, v))


def _adamw_small(rd, rf, sharded_g, ws, ms, vs, rows_d, rows_f, loss_row):
    n = len(ws)
    ns = len(sharded_g)
    rep = [k for k in range(n) if rows_d[k] is not None or rows_f[k] is not None]

    def body(*refs):
        rd_ref, rf_ref = refs[0], refs[1]
        sg = refs[2:2 + ns]
        w_refs, m_refs, v_refs = (refs[2 + ns + t * n:2 + ns + (t + 1) * n] for t in range(3))
        loss_ref, outs = refs[2 + ns + 3 * n], refs[3 + ns + 3 * n:]
        g_out, d_out, m_out, v_out = outs[:len(rep)], outs[len(rep):len(rep) + n], outs[len(rep) + n:len(rep) + 2 * n], \
            outs[len(rep) + 2 * n:]
        loss_ref[...] = jnp.sum(rd_ref[loss_row:loss_row + 1, :], axis=1, keepdims=True)
        si = 0
        for k in range(n):
            shape = w_refs[k].shape
            if rows_d[k] is not None or rows_f[k] is not None:
                src, rws = (rd_ref, rows_d[k]) if rows_d[k] is not None else (rf_ref, rows_f[k])
                axis = 0 if shape[0] == len(rws) else 1
                g = src[rws[0]:rws[0] + 1, :]
                if len(rws) > 1:
                    g = jnp.concatenate([src[r:r + 1, :] for r in rws], axis=axis)
                g_out[rep.index(k)][...] = g
            else:
                g = sg[si][...]
                si += 1
            d_out[k][...], m_out[k][...], v_out[k][...] = _adam_math(w_refs[k][...], g, m_refs[k][...], v_refs[k][...])

    vm = pl.BlockSpec(memory_space=pltpu.VMEM)
    shapes = [S_(w.shape, F32) for w in ws]
    outs = pl.pallas_call(
        body, name="adamw_small", in_specs=[vm] * (2 + ns + 3 * n), out_specs=[vm] * (1 + len(rep) + 3 * n),
        out_shape=[S_((1, 1), F32)] + [shapes[k] for k in rep] + shapes * 3,
    )(rd, rf, *sharded_g, *ws, *ms, *vs)
    g_rep = dict(zip(rep, outs[1:1 + len(rep)]))
    o = outs[1 + len(rep):]
    return outs[0], g_rep, o[:n], o[n:2 * n], o[2 * n:]


def _pack(pieces, rows):
    flat = jnp.concatenate([p.reshape(-1).astype(F32) for p in pieces])
    return jnp.pad(flat, (0, rows * 128 - flat.shape[0])).reshape(rows, 128)


def _unpack(packed, shapes):
    flat = packed.reshape(-1)
    out, o = [], 0
    for s in shapes:
        n = 1
        for d in s:
            n *= d
        out.append(flat[o:o + n].reshape(s))
        o += n
    return out


def _rows_for(pieces_or_shapes):
    n = 0
    for p in pieces_or_shapes:
        k = 1
        for d in (p if isinstance(p, tuple) else p.shape):
            k *= d
        n += k
    return -(-n // 1024) * 8


def kernel(x, norm_mix, norm_ffn, conv_w_pw1, conv_b_pw1, conv_w_dw, conv_b_dw, conv_ln_g, conv_ln_b, conv_w_pw2, conv_b_pw2, pool_w, pool_b, pool_scale, ffn_w_up, ffn_w_dw, ffn_b_dw, ffn_w_down, final_norm, loss_target, m_norm_mix, m_norm_ffn, m_conv_w_pw1, m_conv_b_pw1, m_conv_w_dw, m_conv_b_dw, m_conv_ln_g, m_conv_ln_b, m_conv_w_pw2, m_conv_b_pw2, m_pool_w, m_pool_b, m_pool_scale, m_ffn_w_up, m_ffn_w_dw, m_ffn_b_dw, m_ffn_w_down, m_final_norm, v_norm_mix, v_norm_ffn, v_conv_w_pw1, v_conv_b_pw1, v_conv_w_dw, v_conv_b_dw, v_conv_ln_g, v_conv_ln_b, v_conv_w_pw2, v_conv_b_pw2, v_pool_w, v_pool_b, v_pool_scale, v_ffn_w_up, v_ffn_w_dw, v_ffn_b_dw, v_ffn_w_down, v_final_norm):
    nb, seq, D = x.shape
    T = nb * seq
    F = ffn_w_down.shape[1] * N_CHIPS
    taps = conv_w_dw.shape[1]
    ffn_taps = ffn_w_dw.shape[1]
    ng = pool_w.shape[1]
    cg = pool_w.shape[3]
    dsh = D // N_CHIPS
    fsh = F // N_CHIPS
    assert taps - 1 <= HALO_CONV and ffn_taps - 1 <= HALO_FFN and max(POOL_WINDOWS) <= HALO_POOL
    tm = min(TILE_D, seq)
    tf = min(TILE_F, seq)
    assert seq % tm == 0 and seq % tf == 0 and tm % HALO_CONV == 0 and tf % CHUNK == 0
    tps, tpf = seq // tm, seq // tf
    tw = min(TILE_MM, T)
    assert T % tw == 0

    xi, yi, ci = lax.axis_index("x"), lax.axis_index("y"), lax.axis_index("c")
    qi = 2 * xi + yi
    ids = jnp.stack([qi, ci]).astype(jnp.int32)
    x2d = x.reshape(T, D)
    tg2d = loss_target.reshape(T, D)

    small_sharded = [conv_w_dw[0], ffn_w_dw, pool_b, pool_scale]
    srows = 2 * _rows_for(small_sharded)
    w1g, s_pw2, s_small, s_pool, s_up0, s_up1, s_down0, s_down1 = _own_slots([
        (conv_w_pw1, 0, BF16, False), (conv_w_pw2, 0, BF16, False), (_pack(small_sharded, srows)[None], 0, F32, False),
        (pool_w.reshape(1, ng * (cg // N_CHIPS), cg), 0, BF16, False), (ffn_w_up, 0, BF16, True), (ffn_w_up, 1, BF16, True),
        (ffn_w_down, 0, BF16, False), (ffn_w_down, 1, BF16, False)])
    s_up, s_down = [s_up0, s_up1], [s_down0, s_down1]
    hb0t, a, w2g, smallg, pwg = _conv_in(x2d, norm_mix[0:1], w1g, conv_b_pw1, tm, comm=_GatherComm([s_pw2, s_small, s_pool]))
    w2f = w2g.reshape(D, D)
    sm_parts = [_unpack(smallg[s], [(taps, dsh), (2, ffn_taps, fsh), (1, dsh), (1, dsh)]) for s in range(N_CHIPS)]
    wdw_f = jnp.concatenate([p[0] for p in sm_parts], axis=1)
    wdw_f = jnp.pad(wdw_f, ((0, HALO_CONV - taps), (0, 0)))
    fdw_f = jnp.concatenate([p[1] for p in sm_parts], axis=2)
    fdw_f = jnp.pad(fdw_f, ((0, 0), (0, 8 - ffn_taps), (0, 0)))
    pb_f = jnp.concatenate([p[2] for p in sm_parts], axis=1)
    ps_f = jnp.concatenate([p[3] for p in sm_parts], axis=1)

    c, sbt, x1, hb1, hb1t, wup0, wd0 = _conv_mid(a, x2d, wdw_f, conv_b_dw, conv_ln_g, conv_ln_b, w2f, conv_b_pw2,
                                                 norm_ffn[0:1], tm, tps, taps, comm=_GatherComm([s_up[0], s_down[0]]))
    up0, gb0t, x2, h2, wup1, wd1 = _ffn_fwd(hb1, x1, wup0, fdw_f[0], ffn_b_dw[0:1], wd0.reshape(F, D), norm_mix[1:2], tf, tpf,
                                            ffn_taps, "ffn_fwd0", comm=_GatherComm([s_up[1], s_down[1]]))
    pwf = pwg.reshape(N_CHIPS, ng, cg // N_CHIPS, cg).transpose(1, 0, 2, 3).reshape(ng, cg, cg)
    plb, x3, hb3, hb3t = _pool_fwd(h2, x2, pwf, pb_f, ps_f, norm_ffn[1:2], tm, tps)
    wup = [wup0, wup1]
    wdn = [wd0.reshape(F, D), wd1.reshape(F, D)]
    up1, gb1t, dx4, loss_part, d_final = _ffn_fwd(hb3, x3, wup[1], fdw_f[1], ffn_b_dw[1:2], wdn[1], final_norm.reshape(1, D),
                                                  tf, tpf, ffn_taps, "ffn_fwd1", target=tg2d)

    fs = {}

    def pair(group):
        return list(group), list(group.values()), _PairComm(list(group.values()))

    def chip_sums(nms, parts, from_sib):
        for nm, p, s in zip(nms, parts, from_sib):
            fs[nm] = (p, s)
        return nms, _ExchangeComm([_chip_sum(p, s, ids, "chip_sum_" + nm) for nm, p, s in zip(nms, parts, from_sib)])

    def arrived(nms, got):
        for nm, g in zip(nms, got):
            fs[nm] = fs[nm] + (g,)

    dac1, dup1, fw1 = _bwd_down(dx4, up1, fdw_f[1], ffn_b_dw[1:2], wdn[1], tf, tpf, ffn_taps, "bwd_down1")
    p_down1 = _mm_tn(gb1t, dx4, F // 2, D, tw, False, "dw_down1")
    nms, parts, px = pair({"down1": p_down1.reshape(N_CHIPS, 2, fsh // 2, D)})
    dup1, dx3, dnf1, *sib = _bwd_up(dac1, dup1, wup[1], fdw_f[1], x3, norm_ffn[1:2], dx4, tf, tpf, ffn_taps, "bwd_up1", comm=px)
    nms, xchg = chip_sums(nms, parts, sib)
    p_up1, *got = _mm_tn(hb3t, dup1, D // 2, 4 * fsh, tw, True, "dw_up1", split=2, comm=xchg)
    arrived(nms, got)
    nms, parts, px = pair({"up1": p_up1.reshape(N_CHIPS, 2, D // 2, 2 * fsh)})
    dx2, dpw, pool_sm, *sib = _pool_bwd(dx3, plb, x2, pwf, pb_f, ps_f, norm_mix[1:2], tm, tps, comm=px)
    nms, xchg = chip_sums(nms, parts, sib)
    dac0, dup0, fw0, *got = _bwd_down(dx2, up0, fdw_f[0], ffn_b_dw[0:1], wdn[0], tf, tpf, ffn_taps, "bwd_down0", comm=xchg)
    arrived(nms, got)
    p_down0 = _mm_tn(gb0t, dx2, F // 2, D, tw, False, "dw_down0")
    nms, parts, px = pair({"down0": p_down0.reshape(N_CHIPS, 2, fsh // 2, D)})
    dup0, dx1, dnf0, *sib = _bwd_up(dac0, dup0, wup[0], fdw_f[0], x1, norm_ffn[0:1], dx2, tf, tpf, ffn_taps, "bwd_up0", comm=px)
    nms, xchg = chip_sums(nms, parts, sib)
    p_up0, *got = _mm_tn(hb1t, dup0, D // 2, 4 * fsh, tw, True, "dw_up0", split=2, comm=xchg)
    arrived(nms, got)
    csh = cg // N_CHIPS
    p_pool = dpw.reshape(2, ng // 2, N_CHIPS, csh, cg).transpose(2, 0, 1, 3, 4).reshape(N_CHIPS, 2, (ng // 2) * csh, cg)
    nms, parts, px = pair({"up0": p_up0.reshape(N_CHIPS, 2, D // 2, 2 * fsh), "pool": p_pool.astype(BF16)})
    dc, conv_sm, *sib = _conv_bwd1(dx1, c, w2f, conv_ln_g, conv_ln_b, tm, comm=px)
    nms, xchg = chip_sums(nms, parts, sib)
    p_pw2 = _mm_tn(sbt, dx1, D, D, tw, False, "dw_pw2")
    nms2, parts2, px = pair({"pw2": p_pw2.reshape(N_CHIPS, 2, dsh // 2, D)})
    grad_x, dab, dwdw, db1, dnm0, *both = _conv_bwd2(dc, a, x2d, w1g, wdw_f, norm_mix[0:1], dx1, tm, tps, taps,
                                                     comm=_Both(xchg, px))
    arrived(nms, both[:len(nms)])
    nms, xchg = chip_sums(nms2, parts2, both[len(nms):])
    p_pw1, *got = _mm_tn(hb0t, dab, D, 2 * dsh, tw, True, "dw_pw1", comm=xchg)
    arrived(nms, got)
    nms, parts, _ = pair({"pw1": p_pw1.reshape(N_CHIPS, 2, D // 2, 2 * dsh)})
    nms, xchg = chip_sums(nms, parts, _pair_exchange(parts, "grad_pair_exchange_pw1"))
    (rd, rf), got = _small_allreduce([
        [(dnm0, 0, 1), (pool_sm, 2, 3), (dnf0, 0, 1), (dnf1, 0, 1), (db1, 0, 2), (conv_sm, 0, 4), (d_final, 0, 1),
         (pool_sm, 0, 2), (loss_part, 0, 1), 2, (dwdw, 0, dwdw.shape[0])],
        [(fw0, 0, 4), (fw1, 0, 4)]], xchg)
    arrived(nms, got)
    bufs = [_final_sum(*fs[nm], ids, "final_sum_" + nm) for nm in ("pw1", "pw2", "pool")]
    for nm in ("up", "down"):
        b = _final_sum(*fs[nm + "0"], ids, "final_sum_" + nm + "0", layer=0, n_layers=2)
        bufs.append(_final_sum(*fs[nm + "1"], ids, "final_sum_" + nm + "1", layer=1, n_layers=2, buf=b))
    shared = _comm_only(_ShareComm(bufs), "grad_share_halves")
    g_pw1, g_pw2, g_pool, g_up, g_down = [b.reshape(b.shape[0], 2 * b.shape[2], b.shape[3]) for b in shared]
    g_wdw = lax.dynamic_slice_in_dim(rd[16:16 + taps], qi * dsh, dsh, axis=1)
    g_ps = lax.dynamic_slice_in_dim(rd[11:12], qi * dsh, dsh, axis=1)
    g_pb = lax.dynamic_slice_in_dim(rd[12:13], qi * dsh, dsh, axis=1)
    g_fdw = lax.dynamic_slice_in_dim(rf.reshape(2, 4, F)[:, :ffn_taps], qi * fsh, fsh, axis=2).reshape(2 * ffn_taps, fsh)

    def big(w, g, m, v, nm):
        return tuple(t.reshape(w.shape) for t in _adamw(w.reshape(g.shape), g, m.reshape(g.shape), v.reshape(g.shape),
                                                        "adamw_" + nm))

    o_pw1 = big(conv_w_pw1, g_pw1, m_conv_w_pw1, v_conv_w_pw1, "pw1")
    o_pw2 = big(conv_w_pw2, g_pw2, m_conv_w_pw2, v_conv_w_pw2, "pw2")
    o_pool = big(pool_w, g_pool, m_pool_w, v_pool_w, "pool")
    o_up = big(ffn_w_up, g_up, m_ffn_w_up, v_ffn_w_up, "up")
    o_down = big(ffn_w_down, g_down, m_ffn_w_down, v_ffn_w_down, "down")

    snames = ["norm_mix", "norm_ffn", "b_pw1", "w_dw", "b_dw", "ln_g", "ln_b", "b_pw2", "pool_b", "pool_scale", "ffn_w_dw",
              "ffn_b_dw", "final_norm"]
    sw = [norm_mix, norm_ffn, conv_b_pw1, conv_w_dw, conv_b_dw, conv_ln_g, conv_ln_b, conv_b_pw2, pool_b, pool_scale,
          ffn_w_dw, ffn_b_dw, final_norm]
    smm = [m_norm_mix, m_norm_ffn, m_conv_b_pw1, m_conv_w_dw, m_conv_b_dw, m_conv_ln_g, m_conv_ln_b, m_conv_b_pw2, m_pool_b,
           m_pool_scale, m_ffn_w_dw, m_ffn_b_dw, m_final_norm]
    svv = [v_norm_mix, v_norm_ffn, v_conv_b_pw1, v_conv_w_dw, v_conv_b_dw, v_conv_ln_g, v_conv_ln_b, v_conv_b_pw2, v_pool_b,
           v_pool_scale, v_ffn_w_dw, v_ffn_b_dw, v_final_norm]
    sshapes = [tuple(w.shape) for w in sw]
    two_d = [(2, D), (2, D), (1, 2 * D), (taps, dsh), (1, D), (1, D), (1, D), (1, D), (1, dsh), (1, dsh), (2 * ffn_taps, fsh),
             (2, F), (1, D)]
    rows_d = [[0, 1], [2, 3], [4, 5], None, [9], [6], [7], [8], None, None, None, None, [10]]
    rows_f = [None] * 11 + [[3, 7], None]
    as2d = lambda ts: [t.reshape(s) for t, s in zip(ts, two_d)]
    loss, g_rep, sd, sm_new, sv_new = _adamw_small(rd, rf, [g_wdw, g_pb, g_ps, g_fdw], as2d(sw), as2d(smm), as2d(svv), rows_d,
                                                   rows_f, loss_row=13)
    sg = dict(g_rep)
    sg.update({3: g_wdw, 8: g_pb, 9: g_ps, 10: g_fdw})
    small_out = {n: tuple(t.reshape(sshapes[k]) for t in (sg[k], sd[k], sm_new[k], sv_new[k])) for k, n in enumerate(snames)}

    order = [small_out["norm_mix"], small_out["norm_ffn"], o_pw1, small_out["b_pw1"], small_out["w_dw"], small_out["b_dw"],
             small_out["ln_g"], small_out["ln_b"], o_pw2, small_out["b_pw2"], o_pool, small_out["pool_b"],
             small_out["pool_scale"], o_up, small_out["ffn_w_dw"], small_out["ffn_b_dw"], o_down, small_out["final_norm"]]
    return (loss[0, 0], grad_x.reshape(nb, seq, D), *[o[0] for o in order], *[o[1] for o in order], *[o[2] for o in order],
            *[o[3] for o in order])
```

```python
import functools

import jax
import jax.numpy as jnp
from jax import lax
from jax.experimental import pallas as pl
from jax.experimental.pallas import tpu as pltpu

F32, BF16 = jnp.float32, jnp.bfloat16
S_ = jax.ShapeDtypeStruct
MESH = pl.DeviceIdType.MESH

RMS_EPS, LN_EPS = 1e-6, 1e-5
POOL_WINDOWS = (2, 4, 8, 16)
ADAM_LR, ADAM_B1, ADAM_B2, ADAM_EPS, ADAM_WD, ADAM_STEP = 0.001, 0.9, 0.999, 1e-08, 0.01, 10

VMEM_LIMIT_BYTES = 60 * 1024 * 1024
N_CHIPS = 4
CHUNK = 32
HALO_CONV = 32
HALO_POOL = 32
HALO_FFN = 8
TILE_D = 512
TILE_F = 256
TILE_MM = 1024


def _pcall(body, *, name, grid, in_specs, out_specs, out_shape, args, scratch=(), aliases=None, prefetch=0, comm=None):
    params = pltpu.CompilerParams(dimension_semantics=("arbitrary",) * len(grid), vmem_limit_bytes=VMEM_LIMIT_BYTES)
    if comm is not None:
        assert not prefetch
        single = not isinstance(out_shape, (list, tuple))
        in_specs, args, scratch = list(in_specs), list(args), list(scratch)
        out_specs, out_shape = ([out_specs], [out_shape]) if single else (list(out_specs), list(out_shape))
        n_in, n_out, n_scr, n_cin, n_cout = len(in_specs), len(out_specs), len(scratch), len(comm.inputs), len(comm.out_shape)
        any_spec = pl.BlockSpec(memory_space=pl.ANY)
        aliases = dict(aliases or {})
        aliases.update({n_in + a: n_out + b for a, b in comm.aliases.items()})
        inner = body

        def body(*refs):
            ins, cin = refs[:n_in], refs[n_in:n_in + n_cin]
            outs = refs[n_in + n_cin:n_in + n_cin + n_out]
            cout = refs[n_in + n_cin + n_out:n_in + n_cin + n_out + n_cout]
            scr = refs[n_in + n_cin + n_out + n_cout:n_in + n_cin + n_out + n_cout + n_scr]
            sems = refs[n_in + n_cin + n_out + n_cout + n_scr:]
            step, total = 0, 1
            for d, g in enumerate(grid):
                step = step * g + pl.program_id(d)
                total *= g
            comm.before(cin, cout, sems, step, total)
            inner(*ins, *outs, *scr)
            comm.after(cin, cout, sems, step, total)

        in_specs += [any_spec] * n_cin
        args += list(comm.inputs)
        out_specs += [any_spec] * n_cout
        out_shape += list(comm.out_shape)
        scratch += list(comm.sems)
    if prefetch:
        spec = pltpu.PrefetchScalarGridSpec(num_scalar_prefetch=prefetch, grid=grid, in_specs=in_specs,
                                            out_specs=out_specs, scratch_shapes=list(scratch))
        return pl.pallas_call(body, name=name, grid_spec=spec, out_shape=out_shape,
                              input_output_aliases=aliases or {}, compiler_params=params)(*args)
    return pl.pallas_call(body, name=name, grid=grid, in_specs=in_specs, out_specs=out_specs, out_shape=out_shape,
                          scratch_shapes=list(scratch), input_output_aliases=aliases or {},
                          compiler_params=params)(*args)


def _full(shape):
    n = len(shape)
    return pl.BlockSpec(tuple(shape), lambda *_: (0,) * n, pipeline_mode=pl.Buffered(1))


def _rows(tm, c, col=0):
    return pl.BlockSpec((tm, c), lambda i, *_: (i, col))


def _cols(r, tm):
    return pl.BlockSpec((r, tm), lambda i, *_: (0, i))


def _prev_halo(tm, hb, c):
    return pl.BlockSpec((hb, c), lambda i, *_: (jnp.maximum(i * (tm // hb) - 1, 0), 0))


def _next_halo(tm, hb, c, total_rows):
    last = total_rows // hb - 1
    return pl.BlockSpec((hb, c), lambda i, *_: (jnp.minimum((i + 1) * (tm // hb), last), 0))


def _acc_spec(r, c):
    return pl.BlockSpec((r, c), lambda *_: (0, 0))


def _rms(x, g):
    r = lax.rsqrt(jnp.mean(x * x, axis=-1, keepdims=True) + RMS_EPS)
    xh = x * r
    return xh * g, xh, r


def _rms_bwd(dy, xh, r, g):
    dxh = dy * g
    dx = r * (dxh - xh * jnp.mean(dxh * xh, axis=-1, keepdims=True))
    return dx, jnp.sum(dy * xh, axis=0, keepdims=True)


def _colsum(v):
    return jnp.sum(v, axis=0, keepdims=True)


def _fold8(v):
    out = v[0:8]
    for j in range(1, v.shape[0] // 8):
        out = out + v[8 * j:8 * j + 8]
    return out


def _chunks(n_rows, fn):
    def step(j, carry):
        fn(pl.multiple_of(j * CHUNK, CHUNK))
        return carry
    lax.fori_loop(0, n_rows // CHUNK, step, 0)


def _dot(a, b):
    return jnp.dot(a, b, preferred_element_type=F32)


def _dot_nt(a, b):
    return lax.dot_general(a, b, (((1,), (1,)), ((), ())), preferred_element_type=F32)


def _dot_tn(a, b):
    return lax.dot_general(a, b, (((0,), (0,)), ((), ())), preferred_element_type=F32)


def _sig(v):
    return jax.nn.sigmoid(v)


def _accumulate(ref, val, first):
    @pl.when(first)
    def _():
        ref[...] = val

    @pl.when(jnp.logical_not(first))
    def _():
        ref[...] = ref[...] + val


def _conv_in(x, g, w1g, b1, tm, comm=None):
    T, D = x.shape
    ns, _, nc = w1g.shape

    def body(x_ref, g_ref, w_ref, b_ref, hbt_ref, a_ref):
        hb = _rms(x_ref[...], g_ref[...])[0].astype(BF16)
        hbt_ref[...] = hb.T
        for s in range(ns):
            a_ref[:, s * nc:(s + 1) * nc] = _dot(hb, w_ref[s]) + b_ref[:, s * nc:(s + 1) * nc]

    return _pcall(body, name="conv_in", grid=(T // tm,),
                  in_specs=[_rows(tm, D), _full((1, D)), _full(w1g.shape), _full((1, ns * nc))],
                  out_specs=[_cols(D, tm), _rows(tm, ns * nc)],
                  out_shape=[S_((D, T), BF16), S_((T, ns * nc), F32)], args=(x, g, w1g, b1), comm=comm)


def _conv_mid(a, x, wdw, bdw, lng, lnb, w2, b2, gn, tm, tps, taps, comm=None):
    T, D = x.shape
    hc = HALO_CONV

    def body(a_ref, ah_ref, x_ref, wdw_ref, bdw_ref, lng_ref, lnb_ref, w2_ref, b2_ref, gn_ref,
             c_ref, sbt_ref, x1_ref, hb_ref, hbt_ref, uext, shifted):
        i = pl.program_id(0)
        ah = ah_ref[...]
        uext[0:hc, :] = jnp.where(i % tps == 0, 0.0, ah[:, :D] * _sig(ah[:, D:]))
        av = a_ref[...]
        uext[hc:hc + tm, :] = av[:, :D] * _sig(av[:, D:])

        def conv(r0, src, ks, first):
            acc = jnp.broadcast_to(bdw_ref[...], (CHUNK, D)) if first else c_ref[pl.ds(r0, CHUNK), :]
            for k, o in ks:
                acc = acc + wdw_ref[k:k + 1, :] * src[pl.ds(pl.multiple_of(r0 + o, 8), CHUNK), :]
            c_ref[pl.ds(r0, CHUNK), :] = acc
        _by_sublane_shift(uext, shifted, tm, [hc - (taps - 1) + k for k in range(taps)], conv)

        cv = c_ref[...]
        xc = cv - jnp.mean(cv, axis=-1, keepdims=True)
        nh = xc * lax.rsqrt(jnp.mean(xc * xc, axis=-1, keepdims=True) + LN_EPS)
        n = nh * lng_ref[...] + lnb_ref[...]
        sb = (n * _sig(n)).astype(BF16)
        sbt_ref[...] = sb.T
        x1 = x_ref[...] + _dot(sb, w2_ref[...]) + b2_ref[...]
        x1_ref[...] = x1
        hb = _rms(x1, gn_ref[...])[0].astype(BF16)
        hb_ref[...] = hb
        hbt_ref[...] = hb.T

    return _pcall(body, name="conv_mid", grid=(T // tm,),
                  in_specs=[_rows(tm, 2 * D), _prev_halo(tm, hc, 2 * D), _rows(tm, D), _full(wdw.shape), _full((1, D)),
                            _full((1, D)), _full((1, D)), _full((D, D)), _full((1, D)), _full((1, D))],
                  out_specs=[_rows(tm, D), _cols(D, tm), _rows(tm, D), _rows(tm, D), _cols(D, tm)],
                  out_shape=[S_((T, D), F32), S_((D, T), BF16), S_((T, D), F32), S_((T, D), BF16), S_((D, T), BF16)],
                  scratch=[pltpu.VMEM((tm + hc, D), F32), pltpu.VMEM((tm + hc - 8, D), F32)],
                  args=(a, a, x, wdw, bdw, lng, lnb, w2, b2, gn), comm=comm)


def _ffn_taps(aext, realigned, tm, ffn_taps):
    for k in range(ffn_taps - 1):
        realigned[k] = aext[pl.ds(HALO_FFN - (ffn_taps - 1) + k, tm), :]
    return [realigned.at[k] for k in range(ffn_taps - 1)] + [aext.at[pl.ds(HALO_FFN, tm)]]


def _ffn_conv(taps, wdw_ref, bdw_ref):
    acc = bdw_ref[...]
    for k, tap in enumerate(taps):
        acc = acc + wdw_ref[k:k + 1, :] * tap[...]
    return acc


def _by_sublane_shift(ext, shifted, tm, offsets, per_chunk):
    first = True
    for b in range(8):
        ks = [(k, o - b) for k, o in enumerate(offsets) if o % 8 == b]
        if not ks:
            continue
        if b:
            shifted[...] = ext[pl.ds(b, shifted.shape[0]), :]
        _chunks(tm, functools.partial(per_chunk, src=shifted if b else ext, ks=ks, first=first))
        first = False


def _ffn_fwd(hb, xin, wg, wdw, bdw, wd, gn, tm, tps, ffn_taps, name, target=None, comm=None):
    T, D = xin.shape
    ns, _, nc = wg.shape
    F = ns * nc // 2
    hf = HALO_FFN
    hh = 2 * HALO_FFN
    final = target is not None

    def body(*refs):
        if final:
            (h_ref, hh_ref, x_ref, wg_ref, wdw_ref, bdw_ref, wd_ref, gn_ref, tg_ref,
             up_ref, gb_ref, dx_ref, loss_ref, dgn_ref, aext, realigned) = refs
        else:
            (h_ref, hh_ref, x_ref, wg_ref, wdw_ref, bdw_ref, wd_ref, gn_ref,
             up_ref, gb_ref, xo_ref, ho_ref, aext, realigned) = refs
        i = pl.program_id(0)
        hv = h_ref[...]
        for s in range(ns):
            up_ref[:, s * nc:(s + 1) * nc] = _dot(hv, wg_ref[s])
        halo = jnp.concatenate([_dot(hh_ref[...], wg_ref[s]) for s in range(ns // 2)], axis=1)[hh - hf:hh, :]
        aext[0:hf, :] = jnp.where(i % tps == 0, 0.0, halo)
        aext[hf:hf + tm, :] = up_ref[:, 0:F]
        acv = _ffn_conv(_ffn_taps(aext, realigned, tm, ffn_taps), wdw_ref, bdw_ref)
        gb = (acv * _sig(acv) * up_ref[:, F:2 * F]).astype(BF16)
        gb_ref[...] = gb.T
        xo = x_ref[...] + _dot(gb, wd_ref[...])
        if not final:
            xo_ref[...] = xo
            ho_ref[...] = _rms(xo, gn_ref[...])[0]
        else:
            out, xh, r = _rms(xo, gn_ref[...])
            diff = out - tg_ref[...]
            part = 0.5 / D * _colsum(diff * diff)
            dx, dg = _rms_bwd(diff * (1.0 / D), xh, r, gn_ref[...])
            dx_ref[...] = dx
            _accumulate(loss_ref, part, i == 0)
            _accumulate(dgn_ref, dg, i == 0)

    in_specs = [_rows(tm, D), _prev_halo(tm, hh, D), _rows(tm, D), _full(wg.shape), _full(wdw.shape), _full((1, F)),
                _full((F, D)), _full((1, D))]
    args = [hb, hb, xin, wg, wdw, bdw, wd, gn]
    if final:
        in_specs.append(_rows(tm, D))
        args.append(target)
        out_specs = [_rows(tm, 2 * F), _cols(F, tm), _rows(tm, D), _acc_spec(1, D), _acc_spec(1, D)]
        out_shape = [S_((T, 2 * F), F32), S_((F, T), BF16), S_((T, D), F32), S_((1, D), F32), S_((1, D), F32)]
    else:
        out_specs = [_rows(tm, 2 * F), _cols(F, tm), _rows(tm, D), _rows(tm, D)]
        out_shape = [S_((T, 2 * F), F32), S_((F, T), BF16), S_((T, D), F32), S_((T, D), F32)]
    return _pcall(body, name=name, grid=(T // tm,), in_specs=in_specs, out_specs=out_specs, out_shape=out_shape,
                  scratch=[pltpu.VMEM((tm + hf, F), F32), pltpu.VMEM((ffn_taps - 1, tm, F), F32)], args=args, comm=comm)


def _pool_counts(i, tps, tm, w):
    pos = (i % tps) * tm + lax.broadcasted_iota(jnp.int32, (tm, 1), 0)
    return jnp.minimum(pos + 1, w).astype(F32)


def _window_sums(src, buf_a, buf_b, cg, causal):
    assert POOL_WINDOWS == (2, 4, 8, 16)
    n = src.shape[0]
    levels = len(POOL_WINDOWS)
    cur, outs = src, []
    for l in range(levels):
        dst = buf_b if l % 2 else buf_a
        cols = slice(l * cg, levels * cg)
        lo, m = 8 * (l + 1), n - 8 * (l + 1)
        if causal:
            dst[lo:n, cols] = cur[lo:n, cols] + cur[pl.ds(lo - (1 << l), m), cols]
        else:
            dst[0:m, cols] = cur[0:m, cols] + cur[pl.ds(1 << l, m), cols]
        outs.append(dst)
        cur = dst
    return outs


def _pool_fwd(h, x2, pw, pb, ps, gn, tm, tps, comm=None):
    T, D = h.shape
    ng, cg, _ = pw.shape
    hp = HALO_POOL

    def body(h_ref, hh_ref, x_ref, pw_ref, pb_ref, ps_ref, gn_ref, pl_ref, xo_ref, hb_ref, hbt_ref, hext, buf_a, buf_b):
        i = pl.program_id(0)
        hext[0:hp, :] = jnp.where(i % tps == 0, 0.0, hh_ref[...])
        hext[hp:hp + tm, :] = h_ref[...]
        sums = _window_sums(hext, buf_a, buf_b, cg, True)
        for g, w in enumerate(POOL_WINDOWS):
            cs = slice(g * cg, (g + 1) * cg)
            pl_ref[:, cs] = (sums[g][hp:hp + tm, cs] / _pool_counts(i, tps, tm, w) - h_ref[:, cs]).astype(BF16)

        pv = pl_ref[...]
        mixed = jnp.concatenate([_dot(pv[:, g * cg:(g + 1) * cg], pw_ref[g]) for g in range(ng)], axis=1)
        xo = x_ref[...] + ps_ref[...] * (mixed + pb_ref[...])
        xo_ref[...] = xo
        hb = _rms(xo, gn_ref[...])[0].astype(BF16)
        hb_ref[...] = hb
        hbt_ref[...] = hb.T

    return _pcall(body, name="pool_fwd", grid=(T // tm,),
                  in_specs=[_rows(tm, D), _prev_halo(tm, hp, D), _rows(tm, D), _full(pw.shape), _full((1, D)), _full((1, D)),
                            _full((1, D))],
                  out_specs=[_rows(tm, D), _rows(tm, D), _rows(tm, D), _cols(D, tm)],
                  out_shape=[S_((T, D), BF16), S_((T, D), F32), S_((T, D), BF16), S_((D, T), BF16)],
                  scratch=[pltpu.VMEM((tm + hp, D), F32)] * 3, args=(h, h, x2, pw, pb, ps, gn), comm=comm)


def _bwd_down(dx, up, wdw, bdw, wd, tm, tps, ffn_taps, name, comm=None):
    T, D = dx.shape
    F = up.shape[1] // 2
    hf = HALO_FFN
    nt = T // tm

    def body(dx_ref, up_ref, uph_ref, wdw_ref, bdw_ref, wd_ref, dac_ref, dgate_ref, wsum_ref, aext, realigned):
        i = pl.program_id(0)
        aext[0:hf, :] = jnp.where(i % tps == 0, 0.0, uph_ref[...])
        aext[hf:hf + tm, :] = up_ref[:, 0:F]
        taps = _ffn_taps(aext, realigned, tm, ffn_taps)
        acv = _ffn_conv(taps, wdw_ref, bdw_ref)
        dg = _dot_nt(dx_ref[...].astype(BF16), wd_ref[...])
        sg = _sig(acv)
        dgate_ref[...] = (dg * acv * sg).astype(BF16)
        dac = dg * up_ref[:, F:2 * F] * (sg * (1.0 + acv * (1.0 - sg)))
        dac_ref[...] = dac
        rows = [_colsum(dac * tap[...]) for tap in taps]
        rows += [_colsum(dac), jnp.zeros((8 - ffn_taps - 1, F), F32)]
        _accumulate(wsum_ref, jnp.concatenate(rows, axis=0), i == 0)

    return _pcall(body, name=name, grid=(nt,),
                  in_specs=[_rows(tm, D), _rows(tm, 2 * F), _prev_halo(tm, hf, F), _full(wdw.shape), _full((1, F)),
                            _full((F, D))],
                  out_specs=[_rows(tm, F), _rows(tm, F, col=1), _acc_spec(8, F)],
                  out_shape=[S_((T, F), F32), S_((T, 2 * F), BF16), S_((8, F), F32)],
                  scratch=[pltpu.VMEM((tm + hf, F), F32), pltpu.VMEM((ffn_taps - 1, tm, F), F32)],
                  args=(dx, up, up, wdw, bdw, wd), comm=comm)


def _bwd_up(dac, dup, wg, wdw, xin, gn, dx, tm, tps, ffn_taps, name, comm=None):
    T, D = xin.shape
    F = dac.shape[1]
    ns, _, nc = wg.shape
    hf = HALO_FFN
    nt = T // tm

    def body(dac_ref, dach_ref, dgate_ref, wg_ref, wdw_ref, x_ref, gn_ref, dx_ref, dact_ref, dxo_ref, dgn_ref, dext):
        i = pl.program_id(0)
        dext[0:tm, :] = dac_ref[...]
        dext[tm:tm + hf, :] = jnp.where(i % tps == tps - 1, 0.0, dach_ref[...])
        dact = wdw_ref[ffn_taps - 1:ffn_taps, :] * dac_ref[...]
        for k in range(ffn_taps - 1):
            dact = dact + wdw_ref[k:k + 1, :] * dext[pl.ds(ffn_taps - 1 - k, tm), :]
        dact_ref[...] = dact.astype(BF16)

        dh = jnp.zeros((tm, D), F32)
        for s in range(ns):
            src = dact_ref if s < ns // 2 else dgate_ref
            o = (s % (ns // 2)) * nc
            dh = dh + _dot_nt(src[:, o:o + nc], wg_ref[s])
        _, xh, r = _rms(x_ref[...], gn_ref[...])
        dxn, dg = _rms_bwd(dh, xh, r, gn_ref[...])
        dxo_ref[...] = dx_ref[...] + dxn
        _accumulate(dgn_ref, dg, i == 0)

    return _pcall(body, name=name, grid=(nt,),
                  in_specs=[_rows(tm, F), _next_halo(tm, hf, F, T), _rows(tm, F, col=1), _full(wg.shape), _full(wdw.shape),
                            _rows(tm, D), _full((1, D)), _rows(tm, D)],
                  out_specs=[_rows(tm, F), _rows(tm, D), _acc_spec(1, D)],
                  out_shape=[S_((T, 2 * F), BF16), S_((T, D), F32), S_((1, D), F32)],
                  scratch=[pltpu.VMEM((tm + hf, F), F32)],
                  aliases={2: 0}, args=(dac, dac, dup, wg, wdw, xin, gn, dx), comm=comm)


def _mm_tn(xt, dy, bk, bn, tt, shard_major, name, split=1, comm=None):
    K, T = xt.shape
    N = dy.shape[1]
    nt = T // tt
    w = bn // split

    def body(x_ref, dy_ref, o_ref, acc):
        t = pl.program_id(2)
        part = _dot(x_ref[...], dy_ref[...].astype(BF16))
        _accumulate(acc, part, t == 0)

        @pl.when(t == nt - 1)
        def _():
            if shard_major:
                for p in range(split):
                    o_ref[p] = acc[:, p * w:(p + 1) * w].astype(o_ref.dtype)
            else:
                o_ref[...] = acc[...].astype(o_ref.dtype)

    if shard_major:
        out_spec = pl.BlockSpec((split, bk, w), lambda i, j, t: (j, i, 0))
        out_shape = S_((N // w, K, w), BF16)
    else:
        out_spec = pl.BlockSpec((bk, bn), lambda i, j, t: (i, j))
        out_shape = S_((K, N), BF16)
    return _pcall(body, name=name, grid=(K // bk, N // bn, nt),
                  in_specs=[pl.BlockSpec((bk, tt), lambda i, j, t: (i, t)), pl.BlockSpec((tt, bn), lambda i, j, t: (t, j))],
                  out_specs=out_spec, out_shape=out_shape, scratch=[pltpu.VMEM((bk, bn), F32)], args=(xt, dy), comm=comm)


def _pool_bwd(dx3, plb, x2, pw, pb, ps, gn, tm, tps, comm=None):
    T, D = x2.shape
    ng, cg, _ = pw.shape
    hp = HALO_POOL
    nt = T // tm

    def body(do_ref, doh_ref, pl_ref, x_ref, pw_ref, pb_ref, ps_ref, gn_ref, dxo_ref, dpw_ref, sm_ref, qext, buf_a, buf_b,
             dh_s):
        i = pl.program_id(0)
        do = do_ref[...]
        dm = do * ps_ref[...]
        dmh = jnp.where(i % tps == tps - 1, 0.0, doh_ref[...]) * ps_ref[...]
        pv = pl_ref[...]
        mixed = jnp.concatenate([_dot(pv[:, g * cg:(g + 1) * cg], pw_ref[g]) for g in range(ng)], axis=1)
        dscale = _colsum(do * (mixed + pb_ref[...]))
        dbias = _colsum(dm)
        dmb = dm.astype(BF16)
        dmhb = dmh.astype(BF16)
        for g, w in enumerate(POOL_WINDOWS):
            cs = slice(g * cg, (g + 1) * cg)
            _accumulate(dpw_ref.at[g], _dot_tn(pv[:, cs], dmb[:, cs]), i == 0)
            dpo = _dot_nt(dmb[:, cs], pw_ref[g])
            dh_s[:, cs] = dpo
            qext[0:tm, cs] = dpo / _pool_counts(i, tps, tm, w)
            qext[tm:tm + hp, cs] = _dot_nt(dmhb[:, cs], pw_ref[g]) * (1.0 / w)
        sums = _window_sums(qext, buf_a, buf_b, cg, False)
        for g in range(ng):
            cs = slice(g * cg, (g + 1) * cg)
            dh_s[:, cs] = sums[g][0:tm, cs] - dh_s[:, cs]

        _, xh, r = _rms(x_ref[...], gn_ref[...])
        dxn, dg = _rms_bwd(dh_s[...], xh, r, gn_ref[...])
        dxo_ref[...] = do + dxn
        sm = jnp.concatenate([dscale, dbias, dg, jnp.zeros((5, D), F32)], axis=0)
        _accumulate(sm_ref, sm, i == 0)

    return _pcall(body, name="pool_bwd", grid=(nt,),
                  in_specs=[_rows(tm, D), _next_halo(tm, hp, D, T), _rows(tm, D), _rows(tm, D), _full(pw.shape), _full((1, D)),
                            _full((1, D)), _full((1, D))],
                  out_specs=[_rows(tm, D), pl.BlockSpec((ng, cg, cg), lambda i: (0, 0, 0)), _acc_spec(8, D)],
                  out_shape=[S_((T, D), F32), S_((ng, cg, cg), F32), S_((8, D), F32)],
                  scratch=[pltpu.VMEM((tm + hp, D), F32)] * 3 + [pltpu.VMEM((tm, D), F32)],
                  args=(dx3, dx3, plb, x2, pw, pb, ps, gn), comm=comm)


def _conv_bwd1(dx1, c, w2, lng, lnb, tm, comm=None):
    T, D = dx1.shape

    def body(do_ref, c_ref, w2_ref, lng_ref, lnb_ref, dc_ref, sm_ref):
        i = pl.program_id(0)
        do = do_ref[...]
        ds = _dot_nt(do.astype(BF16), w2_ref[...])
        cv = c_ref[...]
        xc = cv - jnp.mean(cv, axis=-1, keepdims=True)
        rstd = lax.rsqrt(jnp.mean(xc * xc, axis=-1, keepdims=True) + LN_EPS)
        nh = xc * rstd
        n = nh * lng_ref[...] + lnb_ref[...]
        sg = _sig(n)
        dn = ds * (sg * (1.0 + n * (1.0 - sg)))
        dnh = dn * lng_ref[...]
        dc = rstd * (dnh - jnp.mean(dnh, axis=-1, keepdims=True) - nh * jnp.mean(dnh * nh, axis=-1, keepdims=True))
        dc_ref[...] = dc
        sm = jnp.concatenate([_colsum(dn * nh), _colsum(dn), _colsum(do), _colsum(dc), jnp.zeros((4, D), F32)], axis=0)
        _accumulate(sm_ref, sm, i == 0)

    return _pcall(body, name="conv_bwd1", grid=(T // tm,),
                  in_specs=[_rows(tm, D), _rows(tm, D), _full((D, D)), _full((1, D)), _full((1, D))],
                  out_specs=[_rows(tm, D), _acc_spec(8, D)],
                  out_shape=[S_((T, D), F32), S_((8, D), F32)], args=(dx1, c, w2, lng, lnb), comm=comm)


def _conv_bwd2(dc, a, x, w1g, wdw, gn, dx1, tm, tps, taps, comm=None):
    T, D = x.shape
    ns, _, nc = w1g.shape
    hc = HALO_CONV
    nt = T // tm
    tp = wdw.shape[0]

    def body(dc_ref, dch_ref, a_ref, ah_ref, x_ref, w_ref, wdw_ref, gn_ref, dx1_ref,
             gx_ref, dab_ref, dw_ref, db1_ref, dgn_ref, uext, dext, shifted, du_s, wacc):
        i = pl.program_id(0)
        ah = ah_ref[...]
        uext[0:hc, :] = jnp.where(i % tps == 0, 0.0, ah[:, :D] * _sig(ah[:, D:]))
        av = a_ref[...]
        sg2 = _sig(av[:, D:])
        uext[hc:hc + tm, :] = av[:, :D] * sg2
        dext[0:tm, :] = dc_ref[...]
        dext[tm:tm + hc, :] = jnp.where(i % tps == tps - 1, 0.0, dch_ref[...])

        @pl.when(i == 0)
        def _():
            wacc[...] = jnp.zeros_like(wacc)

        def wgrad(r0, src, ks, first):
            d = dc_ref[pl.ds(r0, CHUNK), :]
            for k, o in ks:
                wacc[8 * k:8 * k + 8, :] += _fold8(d * src[pl.ds(pl.multiple_of(r0 + o, 8), CHUNK), :])
        _by_sublane_shift(uext, shifted, tm, [hc - (taps - 1) + k for k in range(taps)], wgrad)

        def convt(r0, src, ks, first):
            acc = jnp.zeros((CHUNK, D), F32) if first else du_s[pl.ds(r0, CHUNK), :]
            for k, o in ks:
                acc = acc + wdw_ref[k:k + 1, :] * src[pl.ds(pl.multiple_of(r0 + o, 8), CHUNK), :]
            du_s[pl.ds(r0, CHUNK), :] = acc
        _by_sublane_shift(dext, shifted, tm, [taps - 1 - k for k in range(taps)], convt)

        du = du_s[...]
        da1 = du * sg2
        da2 = du * av[:, :D] * (sg2 * (1.0 - sg2))
        dab_ref[:, 0:D] = da1.astype(BF16)
        dab_ref[:, D:2 * D] = da2.astype(BF16)
        _accumulate(db1_ref, jnp.concatenate([_colsum(da1), _colsum(da2)], axis=0), i == 0)
        dh = jnp.zeros((tm, D), F32)
        for s in range(ns):
            dh = dh + _dot_nt(dab_ref[:, s * nc:(s + 1) * nc], w_ref[s])
        _, xh, r = _rms(x_ref[...], gn_ref[...])
        dxn, dg = _rms_bwd(dh, xh, r, gn_ref[...])
        gx_ref[...] = dx1_ref[...] + dxn
        _accumulate(dgn_ref, dg, i == 0)

        @pl.when(i == nt - 1)
        def _():
            dw_ref[...] = jnp.zeros_like(dw_ref)
            for k in range(taps):
                dw_ref[k:k + 1, :] = _colsum(wacc[8 * k:8 * k + 8, :])

    return _pcall(body, name="conv_bwd2", grid=(nt,),
                  in_specs=[_rows(tm, D), _next_halo(tm, hc, D, T), _rows(tm, 2 * D), _prev_halo(tm, hc, 2 * D), _rows(tm, D),
                            _full(w1g.shape), _full(wdw.shape), _full((1, D)), _rows(tm, D)],
                  out_specs=[_rows(tm, D), _rows(tm, 2 * D), _acc_spec(tp, D), _acc_spec(2, D), _acc_spec(1, D)],
                  out_shape=[S_((T, D), F32), S_((T, 2 * D), BF16), S_((tp, D), F32), S_((2, D), F32), S_((1, D), F32)],
                  scratch=[pltpu.VMEM((tm + hc, D), F32), pltpu.VMEM((tm + hc, D), F32), pltpu.VMEM((tm + hc - 8, D), F32),
                           pltpu.VMEM((tm, D), F32), pltpu.VMEM((8 * taps, D), F32)],
                  args=(dc, dc, a, a, x, w1g, wdw, gn, dx1), comm=comm)


def _place():
    x, y, c = lax.axis_index("x"), lax.axis_index("y"), lax.axis_index("c")
    chips = [(1 - x, y), (x, 1 - y), (1 - x, 1 - y)]
    return x, y, c, chips


class _GatherComm:
    def __init__(self, slots):
        n = len(slots)
        self.inputs = list(slots)
        self.out_shape = [S_(s.shape, s.dtype) for s in slots]
        self.aliases = {i: i for i in range(n)}
        self.sems = [pltpu.SemaphoreType.DMA((n, 3))] * 4
        self.halves = [s.shape[1] // 2 for s in slots]
        self.paired = [s.shape[0] == N_CHIPS // 2 for s in slots]
        self.widths = [s.shape[2] // 2 if p else s.shape[2] for s, p in zip(slots, self.paired)]

    def _copies(self, outs, sems):
        s_ici, r_ici, s_d2d, r_d2d = sems
        x, y, c, chips = _place()
        peers = [(j, chip, 2 * chip[0] + chip[1]) for j, chip in enumerate(chips)]

        def block(i, chip, h):
            rows = pl.ds(pl.multiple_of(h * self.halves[i], 8), self.halves[i])
            if self.paired[i]:
                return outs[i].at[chip // 2, rows, pl.ds(pl.multiple_of((chip % 2) * self.widths[i], 128), self.widths[i])]
            return outs[i].at[chip, rows]

        def ici(i, j, chip, to):
            blk = block(i, chip, c)
            return pltpu.make_async_remote_copy(src_ref=blk, dst_ref=blk, send_sem=s_ici.at[i, j], recv_sem=r_ici.at[i, j],
                                                device_id=to, device_id_type=MESH)

        def d2d(i, j, chip, h):
            blk = block(i, chip, h)
            return pltpu.make_async_remote_copy(src_ref=blk, dst_ref=blk, send_sem=s_d2d.at[i, j], recv_sem=r_d2d.at[i, j],
                                                device_id=(x, y, 1 - c), device_id_type=MESH)
        return (x, y, c, 2 * x + y), peers, ici, d2d

    def before(self, cin, outs, sems, step, total):
        (x, y, c, q), peers, ici, d2d = self._copies(outs, sems)

        @pl.when(step == 0)
        def _():
            for i in range(len(outs)):
                for j, chip, _ in peers:
                    ici(i, j, q, (*chip, c)).start()

    def after(self, cin, outs, sems, step, total):
        (x, y, c, q), peers, ici, d2d = self._copies(outs, sems)

        @pl.when(step == max(total - 2, 0))
        def _():
            for i in range(len(outs)):
                for j, chip, qj in peers:
                    ici(i, j, qj, (x, y, c)).wait_recv()
                    d2d(i, j, qj, c).start()

        @pl.when(step == total - 1)
        def _():
            for i in range(len(outs)):
                for j, chip, qj in peers:
                    d2d(i, j, qj, 1 - c).wait_recv()
            for i in range(len(outs)):
                for j, chip, qj in peers:
                    ici(i, j, q, (*chip, c)).wait_send()
                    d2d(i, j, qj, c).wait_send()


class _ExchangeComm:
    def __init__(self, sums):
        n = len(sums)
        self.inputs = list(sums)
        self.out_shape = [S_((3,) + s.shape[1:], s.dtype) for s in sums]
        self.aliases = {}
        self.sems = [pltpu.SemaphoreType.DMA((n, 3))] * 2

    def _copies(self, srcs, outs, sems):
        s_sem, r_sem = sems
        x, y, c, chips = _place()
        return [pltpu.make_async_remote_copy(src_ref=srcs[i].at[2 * chip[0] + chip[1]], dst_ref=outs[i].at[j],
                                             send_sem=s_sem.at[i, j], recv_sem=r_sem.at[i, j],
                                             device_id=(*chip, c), device_id_type=MESH)
                for i in range(len(srcs)) for j, chip in enumerate(chips)]

    def before(self, srcs, outs, sems, step, total):
        @pl.when(step == 0)
        def _():
            for cp in self._copies(srcs, outs, sems):
                cp.start()

    def after(self, srcs, outs, sems, step, total):
        @pl.when(step == total - 1)
        def _():
            for cp in self._copies(srcs, outs, sems):
                cp.wait()


def _own_slots(specs):
    n = len(specs)
    shapes, blocks, count = [], [], []
    for a, (w, layer, dtype, pairs) in enumerate(specs):
        _, r, c = w.shape
        br = _row_block(r, c, 1 << 21)
        shapes.append(S_((N_CHIPS // 2, r, 2 * c) if pairs else (N_CHIPS, r, c), dtype))
        blocks += [(a, k, k * br, br) for k in range(r // br)]
        count.append(r // br)
    gather = _GatherComm(shapes[:1])
    n_sems = len(gather.sems)

    def body(*refs):
        srcs, outs = refs[:n], refs[n:2 * n]
        inb, outb = refs[2 * n:3 * n], refs[3 * n:4 * n]
        lsem, ssem = refs[4 * n], refs[4 * n + 1]
        gsems = refs[4 * n + 2:4 * n + 2 + n_sems]
        x, y, c, _ = _place()
        q = 2 * x + y

        def load(a, k, r0, br):
            return pltpu.make_async_copy(srcs[a].at[specs[a][1], pl.ds(r0, br)], inb[a].at[k % 2], lsem.at[a, k % 2])

        def store(a, k, r0, br):
            if specs[a][3]:
                wc = specs[a][0].shape[2]
                dst = outs[a].at[q // 2, pl.ds(r0, br), pl.ds(pl.multiple_of((q % 2) * wc, 128), wc)]
            else:
                dst = outs[a].at[q, pl.ds(r0, br)]
            return pltpu.make_async_copy(outb[a].at[k % 2], dst, ssem.at[a, k % 2])

        load(*blocks[0]).start()
        for b, (a, k, r0, br) in enumerate(blocks):
            if b + 1 < len(blocks):
                load(*blocks[b + 1]).start()
            load(a, k, r0, br).wait()
            if k >= 2:
                store(a, k - 2, r0 - 2 * br, br).wait()
            outb[a][k % 2] = inb[a][k % 2].astype(specs[a][2])
            store(a, k, r0, br).start()
            if a == 0 and k == count[0] - 1:
                for kk in range(max(k - 1, 0), k + 1):
                    store(0, kk, kk * br, br).wait()
                gather.before(None, outs[:1], gsems, 0, 1)
        for a in range(1, n):
            br = blocks[sum(count[:a])][3]
            for kk in range(max(count[a] - 2, 0), count[a]):
                store(a, kk, kk * br, br).wait()
        gather.after(None, outs[:1], gsems, 0, 1)

    any_spec = pl.BlockSpec(memory_space=pl.ANY)
    scratch = ([pltpu.VMEM((2, blocks[sum(count[:a])][3], specs[a][0].shape[2]), F32) for a in range(n)]
               + [pltpu.VMEM((2, blocks[sum(count[:a])][3], specs[a][0].shape[2]), specs[a][2]) for a in range(n)]
               + [pltpu.SemaphoreType.DMA((n, 2)), pltpu.SemaphoreType.DMA((n, 2))] + list(gather.sems))
    outs = pl.pallas_call(body, name="own_slots", in_specs=[any_spec] * n, out_specs=[any_spec] * n, out_shape=shapes,
                          scratch_shapes=scratch,
                          compiler_params=pltpu.CompilerParams(vmem_limit_bytes=VMEM_LIMIT_BYTES))(*[s[0] for s in specs])
    return list(outs)


def _comm_only(comm, name):
    def body():
        pass
    outs = _pcall(body, name=name, grid=(1,), in_specs=[], out_specs=[], out_shape=[], args=[], comm=comm)
    return list(outs)


def _pair_exchange(parts, name):
    n = len(parts)

    def body(*refs):
        srcs, outs = refs[:n], refs[n:2 * n]
        s_sem, r_sem = refs[2 * n:]
        x, y, c, _ = _place()
        cps = []
        for i in range(n):
            for s in range(N_CHIPS):
                cp = pltpu.make_async_remote_copy(src_ref=srcs[i].at[s, 1 - c], dst_ref=outs[i].at[s], send_sem=s_sem.at[i, s],
                                                  recv_sem=r_sem.at[i, s], device_id=(x, y, 1 - c), device_id_type=MESH)
                cp.start()
                cps.append(cp)
        for cp in cps:
            cp.wait()

    any_spec = pl.BlockSpec(memory_space=pl.ANY)
    outs = pl.pallas_call(
        body, name=name, in_specs=[any_spec] * n, out_specs=[any_spec] * n,
        out_shape=[S_((N_CHIPS,) + p.shape[2:], p.dtype) for p in parts],
        scratch_shapes=[pltpu.SemaphoreType.DMA((n, N_CHIPS))] * 2,
    )(*parts)
    return list(outs)


class _PairComm:
    def __init__(self, parts):
        n = len(parts)
        self.inputs = list(parts)
        self.out_shape = [S_((N_CHIPS,) + p.shape[2:], p.dtype) for p in parts]
        self.aliases = {}
        self.sems = [pltpu.SemaphoreType.DMA((n, N_CHIPS))] * 2

    def _copies(self, srcs, outs, sems):
        s_sem, r_sem = sems
        x, y, c, _ = _place()
        return [pltpu.make_async_remote_copy(src_ref=srcs[i].at[s, 1 - c], dst_ref=outs[i].at[s], send_sem=s_sem.at[i, s],
                                             recv_sem=r_sem.at[i, s], device_id=(x, y, 1 - c), device_id_type=MESH)
                for i in range(len(srcs)) for s in range(N_CHIPS)]

    def before(self, srcs, outs, sems, step, total):
        @pl.when(step == 0)
        def _():
            for cp in self._copies(srcs, outs, sems):
                cp.start()

    def after(self, srcs, outs, sems, step, total):
        @pl.when(step == total - 1)
        def _():
            for cp in self._copies(srcs, outs, sems):
                cp.wait()


class _ShareComm:
    def __init__(self, bufs):
        n = len(bufs)
        self.inputs = list(bufs)
        self.out_shape = [S_(b.shape, b.dtype) for b in bufs]
        self.aliases = {i: i for i in range(n)}
        self.sems = [pltpu.SemaphoreType.DMA((n, max(b.shape[0] for b in bufs)))] * 2
        self.layers = [b.shape[0] for b in bufs]

    def _copies(self, outs, sems):
        s_sem, r_sem = sems
        x, y, c, _ = _place()
        cps = []
        for i, nl in enumerate(self.layers):
            for l in range(nl):
                blk = outs[i].at[l, c]
                cps.append(pltpu.make_async_remote_copy(src_ref=blk, dst_ref=blk, send_sem=s_sem.at[i, l],
                                                        recv_sem=r_sem.at[i, l], device_id=(x, y, 1 - c),
                                                        device_id_type=MESH))
        return cps

    def before(self, cin, outs, sems, step, total):
        @pl.when(step == 0)
        def _():
            for cp in self._copies(outs, sems):
                cp.start()

    def after(self, cin, outs, sems, step, total):
        @pl.when(step == total - 1)
        def _():
            for cp in self._copies(outs, sems):
                cp.wait()


class _Both:
    def __init__(self, a, b):
        self.parts = (a, b)
        self.inputs = a.inputs + b.inputs
        self.out_shape = a.out_shape + b.out_shape
        self.aliases = dict(a.aliases)
        self.aliases.update({len(a.inputs) + i: len(a.out_shape) + o for i, o in b.aliases.items()})
        self.sems = a.sems + b.sems

    def _each(self, phase, cin, cout, sems, step, total):
        i = o = s = 0
        for p in self.parts:
            ni, no, ns = len(p.inputs), len(p.out_shape), len(p.sems)
            getattr(p, phase)(cin[i:i + ni], cout[o:o + no], sems[s:s + ns], step, total)
            i, o, s = i + ni, o + no, s + ns

    def before(self, cin, cout, sems, step, total):
        self._each("before", cin, cout, sems, step, total)

    def after(self, cin, cout, sems, step, total):
        self._each("after", cin, cout, sems, step, total)


def _small_allreduce(groups, comm):
    n = len(groups)
    nci, nco, ncs = len(comm.inputs), len(comm.out_shape), len(comm.sems)
    arrays = []
    for grp in groups:
        for p in grp:
            if not isinstance(p, int) and not any(p[0] is a for a in arrays):
                arrays.append(p[0])
    shapes = []
    for grp in groups:
        rows = sum(p if isinstance(p, int) else p[2] - p[1] for p in grp)
        width = next(p[0].shape[1] for p in grp if not isinstance(p, int))
        assert rows % 8 == 0
        shapes.append((rows, width))
    na = len(arrays)

    def body(*refs):
        a_refs, cin = refs[:na], refs[na:na + nci]
        o_refs, cout = refs[na + nci:na + nci + n], refs[na + nci + n:na + nci + n + nco]
        rest = refs[na + nci + n + nco:]
        v_refs, sibs, css, gots = rest[:n], rest[n:2 * n], rest[2 * n:3 * n], rest[3 * n:4 * n]
        s_sem, r_sem = rest[4 * n:4 * n + 2]
        csems = rest[4 * n + 2:]
        for i, grp in enumerate(groups):
            o = 0
            for p in grp:
                if isinstance(p, int):
                    v_refs[i][o:o + p, :] = jnp.zeros((p, shapes[i][1]), F32)
                    o += p
                else:
                    k = next(t for t, a in enumerate(arrays) if a is p[0])
                    v_refs[i][o:o + p[2] - p[1], :] = a_refs[k][p[1]:p[2], :]
                    o += p[2] - p[1]
        x, y, c, chips = _place()
        q = 2 * x + y
        to_sib = [pltpu.make_async_remote_copy(src_ref=v_refs[i], dst_ref=sibs[i], send_sem=s_sem.at[i, 3], recv_sem=r_sem.at[i, 3],
                                               device_id=(x, y, 1 - c), device_id_type=MESH) for i in range(n)]
        for cp in to_sib:
            cp.start()
        cps = []
        for i in range(n):
            to_sib[i].wait()
            mine, other = v_refs[i][...], sibs[i][...]
            css[i][...] = jnp.where(c == 0, mine, other) + jnp.where(c == 0, other, mine)
            for j, chip in enumerate(chips):
                cp = pltpu.make_async_remote_copy(src_ref=css[i], dst_ref=gots[i].at[j], send_sem=s_sem.at[i, j],
                                                  recv_sem=r_sem.at[i, j], device_id=(*chip, c), device_id_type=MESH)
                cp.start()
                cps.append(cp)
        comm.before(cin, cout, csems, 0, 1)
        for cp in cps:
            cp.wait()
        flips = [2, 1, 3]
        for i in range(n):
            total = None
            for k in range(N_CHIPS):
                d = q ^ k
                term = jnp.where(d == 0, css[i][...], 0.0)
                for j in range(3):
                    term = jnp.where(d == flips[j], gots[i][j], term)
                total = term if total is None else total + term
            o_refs[i][...] = total
        comm.after(cin, cout, csems, 0, 1)

    vm, any_spec = pl.BlockSpec(memory_space=pltpu.VMEM), pl.BlockSpec(memory_space=pl.ANY)
    outs = pl.pallas_call(
        body, name="small_allreduce", in_specs=[vm] * na + [any_spec] * nci, out_specs=[vm] * n + [any_spec] * nco,
        out_shape=[S_(s, F32) for s in shapes] + list(comm.out_shape),
        scratch_shapes=[pltpu.VMEM(s, F32) for s in shapes] * 3 + [pltpu.VMEM((3,) + s, F32) for s in shapes]
        + [pltpu.SemaphoreType.DMA((n, 4)), pltpu.SemaphoreType.DMA((n, 4))] + list(comm.sems),
    )(*arrays, *comm.inputs)
    return list(outs[:n]), list(outs[n:])


def _row_block(r, c, target_bytes=1 << 20):
    br = r
    while br % 32 == 0 and br * c * 4 > target_bytes:
        br //= 2
    return br


def _chip_sum(part, sib, ids, name):
    _, _, r, c = part.shape
    br = _row_block(r, c, 1 << 22)

    def body(ids_ref, p_ref, s_ref, o_ref):
        o_ref[...] = (p_ref[...].astype(F32) + s_ref[...].astype(F32)).astype(BF16)

    return _pcall(body, name=name, grid=(N_CHIPS, r // br), prefetch=1,
                  in_specs=[pl.BlockSpec((None, None, br, c), lambda s, i, ids: (s, ids[1], i, 0)),
                            pl.BlockSpec((None, br, c), lambda s, i, ids: (s, i, 0))],
                  out_specs=pl.BlockSpec((None, br, c), lambda s, i, ids: (s, i, 0)),
                  out_shape=S_((N_CHIPS, r, c), BF16), args=(ids, part, sib))


def _final_sum(part, sib, got, ids, name, layer=0, n_layers=1, buf=None):
    _, r, c = got.shape
    br = _row_block(r, c, 1 << 21)

    def body(ids_ref, p_ref, s_ref, got_ref, *rest):
        v = p_ref[...].astype(F32) + s_ref[...].astype(F32)
        for j in range(3):
            v = v + got_ref[j].astype(F32)
        rest[-1][...] = v

    in_specs = [pl.BlockSpec((None, None, br, c), lambda i, ids: (ids[0], ids[1], i, 0)),
                pl.BlockSpec((None, br, c), lambda i, ids: (ids[0], i, 0)),
                pl.BlockSpec((3, br, c), lambda i, ids: (0, i, 0))]
    args = [ids, part, sib, got]
    if buf is not None:
        in_specs.append(pl.BlockSpec(memory_space=pl.ANY))
        args.append(buf)
    return _pcall(body, name=name, grid=(r // br,), prefetch=1, in_specs=in_specs,
                  out_specs=pl.BlockSpec((None, None, br, c), lambda i, ids: (layer, ids[1], i, 0)),
                  out_shape=S_((n_layers, 2, r, c), F32), args=args, aliases={4: 0} if buf is not None else None)


def _adam_math(w, g, m, v):
    c1 = 1.0 / (1.0 - ADAM_B1 ** ADAM_STEP)
    c2 = 1.0 / (1.0 - ADAM_B2 ** ADAM_STEP)
    mn = ADAM_B1 * m + (1.0 - ADAM_B1) * g
    vn = ADAM_B2 * v + (1.0 - ADAM_B2) * (g * g)
    return -ADAM_LR * ((mn * c1) / (jnp.sqrt(vn * c2) + ADAM_EPS) + ADAM_WD * w), mn, vn


def _adamw(w, g, m, v, name):
    nl, r, c = w.shape
    br = _row_block(r, c, 1 << 20)

    def body(w_ref, g_ref, m_ref, v_ref, go_ref, d_ref, mo_ref, vo_ref):
        gv = g_ref[...]
        go_ref[...] = gv
        d_ref[...], mo_ref[...], vo_ref[...] = _adam_math(w_ref[...], gv, m_ref[...], v_ref[...])

    blk = pl.BlockSpec((None, br, c), lambda l, i: (l, i, 0))
    return _pcall(body, name=name, grid=(nl, r // br), in_specs=[blk] * 4, out_specs=[blk] * 4,
                  out_shape=[S_((nl, r, c), F32)] * 4, args=(w, g, m, v))


def _adamw_small(rd, rf, sharded_g, ws, ms, vs, rows_d, rows_f, loss_row):
    n = len(ws)
    ns = len(sharded_g)
    rep = [k for k in range(n) if rows_d[k] is not None or rows_f[k] is not None]

    def body(*refs):
        rd_ref, rf_ref = refs[0], refs[1]
        sg = refs[2:2 + ns]
        w_refs, m_refs, v_refs = (refs[2 + ns + t * n:2 + ns + (t + 1) * n] for t in range(3))
        loss_ref, outs = refs[2 + ns + 3 * n], refs[3 + ns + 3 * n:]
        g_out, d_out, m_out, v_out = outs[:len(rep)], outs[len(rep):len(rep) + n], outs[len(rep) + n:len(rep) + 2 * n], \
            outs[len(rep) + 2 * n:]
        loss_ref[...] = jnp.sum(rd_ref[loss_row:loss_row + 1, :], axis=1, keepdims=True)
        si = 0
        for k in range(n):
            shape = w_refs[k].shape
            if rows_d[k] is not None or rows_f[k] is not None:
                src, rws = (rd_ref, rows_d[k]) if rows_d[k] is not None else (rf_ref, rows_f[k])
                axis = 0 if shape[0] == len(rws) else 1
                g = src[rws[0]:rws[0] + 1, :]
                if len(rws) > 1:
                    g = jnp.concatenate([src[r:r + 1, :] for r in rws], axis=axis)
                g_out[rep.index(k)][...] = g
            else:
                g = sg[si][...]
                si += 1
            d_out[k][...], m_out[k][...], v_out[k][...] = _adam_math(w_refs[k][...], g, m_refs[k][...], v_refs[k][...])

    vm = pl.BlockSpec(memory_space=pltpu.VMEM)
    shapes = [S_(w.shape, F32) for w in ws]
    outs = pl.pallas_call(
        body, name="adamw_small", in_specs=[vm] * (2 + ns + 3 * n), out_specs=[vm] * (1 + len(rep) + 3 * n),
        out_shape=[S_((1, 1), F32)] + [shapes[k] for k in rep] + shapes * 3,
    )(rd, rf, *sharded_g, *ws, *ms, *vs)
    g_rep = dict(zip(rep, outs[1:1 + len(rep)]))
    o = outs[1 + len(rep):]
    return outs[0], g_rep, o[:n], o[n:2 * n], o[2 * n:]


def _pack(pieces, rows):
    flat = jnp.concatenate([p.reshape(-1).astype(F32) for p in pieces])
    return jnp.pad(flat, (0, rows * 128 - flat.shape[0])).reshape(rows, 128)


def _unpack(packed, shapes):
    flat = packed.reshape(-1)
    out, o = [], 0
    for s in shapes:
        n = 1
        for d in s:
            n *= d
        out.append(flat[o:o + n].reshape(s))
        o += n
    return out


def _rows_for(pieces_or_shapes):
    n = 0
    for p in pieces_or_shapes:
        k = 1
        for d in (p if isinstance(p, tuple) else p.shape):
            k *= d
        n += k
    return -(-n // 1024) * 8


def kernel(x, norm_mix, norm_ffn, conv_w_pw1, conv_b_pw1, conv_w_dw, conv_b_dw, conv_ln_g, conv_ln_b, conv_w_pw2, conv_b_pw2, pool_w, pool_b, pool_scale, ffn_w_up, ffn_w_dw, ffn_b_dw, ffn_w_down, final_norm, loss_target, m_norm_mix, m_norm_ffn, m_conv_w_pw1, m_conv_b_pw1, m_conv_w_dw, m_conv_b_dw, m_conv_ln_g, m_conv_ln_b, m_conv_w_pw2, m_conv_b_pw2, m_pool_w, m_pool_b, m_pool_scale, m_ffn_w_up, m_ffn_w_dw, m_ffn_b_dw, m_ffn_w_down, m_final_norm, v_norm_mix, v_norm_ffn, v_conv_w_pw1, v_conv_b_pw1, v_conv_w_dw, v_conv_b_dw, v_conv_ln_g, v_conv_ln_b, v_conv_w_pw2, v_conv_b_pw2, v_pool_w, v_pool_b, v_pool_scale, v_ffn_w_up, v_ffn_w_dw, v_ffn_b_dw, v_ffn_w_down, v_final_norm):
    nb, seq, D = x.shape
    T = nb * seq
    F = ffn_w_down.shape[1] * N_CHIPS
    taps = conv_w_dw.shape[1]
    ffn_taps = ffn_w_dw.shape[1]
    ng = pool_w.shape[1]
    cg = pool_w.shape[3]
    dsh = D // N_CHIPS
    fsh = F // N_CHIPS
    assert taps - 1 <= HALO_CONV and ffn_taps - 1 <= HALO_FFN and max(POOL_WINDOWS) <= HALO_POOL
    tm = min(TILE_D, seq)
    tf = min(TILE_F, seq)
    assert seq % tm == 0 and seq % tf == 0 and tm % HALO_CONV == 0 and tf % CHUNK == 0
    tps, tpf = seq // tm, seq // tf
    tw = min(TILE_MM, T)
    assert T % tw == 0

    xi, yi, ci = lax.axis_index("x"), lax.axis_index("y"), lax.axis_index("c")
    qi = 2 * xi + yi
    ids = jnp.stack([qi, ci]).astype(jnp.int32)
    x2d = x.reshape(T, D)
    tg2d = loss_target.reshape(T, D)

    small_sharded = [conv_w_dw[0], ffn_w_dw, pool_b, pool_scale]
    srows = 2 * _rows_for(small_sharded)
    w1g, s_pw2, s_small, s_pool, s_up0, s_up1, s_down0, s_down1 = _own_slots([
        (conv_w_pw1, 0, BF16, False), (conv_w_pw2, 0, BF16, False), (_pack(small_sharded, srows)[None], 0, F32, False),
        (pool_w.reshape(1, ng * (cg // N_CHIPS), cg), 0, BF16, False), (ffn_w_up, 0, BF16, True), (ffn_w_up, 1, BF16, True),
        (ffn_w_down, 0, BF16, False), (ffn_w_down, 1, BF16, False)])
    s_up, s_down = [s_up0, s_up1], [s_down0, s_down1]
    hb0t, a, w2g, smallg, pwg = _conv_in(x2d, norm_mix[0:1], w1g, conv_b_pw1, tm, comm=_GatherComm([s_pw2, s_small, s_pool]))
    w2f = w2g.reshape(D, D)
    sm_parts = [_unpack(smallg[s], [(taps, dsh), (2, ffn_taps, fsh), (1, dsh), (1, dsh)]) for s in range(N_CHIPS)]
    wdw_f = jnp.concatenate([p[0] for p in sm_parts], axis=1)
    wdw_f = jnp.pad(wdw_f, ((0, HALO_CONV - taps), (0, 0)))
    fdw_f = jnp.concatenate([p[1] for p in sm_parts], axis=2)
    fdw_f = jnp.pad(fdw_f, ((0, 0), (0, 8 - ffn_taps), (0, 0)))
    pb_f = jnp.concatenate([p[2] for p in sm_parts], axis=1)
    ps_f = jnp.concatenate([p[3] for p in sm_parts], axis=1)

    c, sbt, x1, hb1, hb1t, wup0, wd0 = _conv_mid(a, x2d, wdw_f, conv_b_dw, conv_ln_g, conv_ln_b, w2f, conv_b_pw2,
                                                 norm_ffn[0:1], tm, tps, taps, comm=_GatherComm([s_up[0], s_down[0]]))
    up0, gb0t, x2, h2, wup1, wd1 = _ffn_fwd(hb1, x1, wup0, fdw_f[0], ffn_b_dw[0:1], wd0.reshape(F, D), norm_mix[1:2], tf, tpf,
                                            ffn_taps, "ffn_fwd0", comm=_GatherComm([s_up[1], s_down[1]]))
    pwf = pwg.reshape(N_CHIPS, ng, cg // N_CHIPS, cg).transpose(1, 0, 2, 3).reshape(ng, cg, cg)
    plb, x3, hb3, hb3t = _pool_fwd(h2, x2, pwf, pb_f, ps_f, norm_ffn[1:2], tm, tps)
    wup = [wup0, wup1]
    wdn = [wd0.reshape(F, D), wd1.reshape(F, D)]
    up1, gb1t, dx4, loss_part, d_final = _ffn_fwd(hb3, x3, wup[1], fdw_f[1], ffn_b_dw[1:2], wdn[1], final_norm.reshape(1, D),
                                                  tf, tpf, ffn_taps, "ffn_fwd1", target=tg2d)

    fs = {}

    def pair(group):
        return list(group), list(group.values()), _PairComm(list(group.values()))

    def chip_sums(nms, parts, from_sib):
        for nm, p, s in zip(nms, parts, from_sib):
            fs[nm] = (p, s)
        return nms, _ExchangeComm([_chip_sum(p, s, ids, "chip_sum_" + nm) for nm, p, s in zip(nms, parts, from_sib)])

    def arrived(nms, got):
        for nm, g in zip(nms, got):
            fs[nm] = fs[nm] + (g,)

    dac1, dup1, fw1 = _bwd_down(dx4, up1, fdw_f[1], ffn_b_dw[1:2], wdn[1], tf, tpf, ffn_taps, "bwd_down1")
    p_down1 = _mm_tn(gb1t, dx4, F // 2, D, tw, False, "dw_down1")
    nms, parts, px = pair({"down1": p_down1.reshape(N_CHIPS, 2, fsh // 2, D)})
    dup1, dx3, dnf1, *sib = _bwd_up(dac1, dup1, wup[1], fdw_f[1], x3, norm_ffn[1:2], dx4, tf, tpf, ffn_taps, "bwd_up1", comm=px)
    nms, xchg = chip_sums(nms, parts, sib)
    p_up1, *got = _mm_tn(hb3t, dup1, D // 2, 4 * fsh, tw, True, "dw_up1", split=2, comm=xchg)
    arrived(nms, got)
    nms, parts, px = pair({"up1": p_up1.reshape(N_CHIPS, 2, D // 2, 2 * fsh)})
    dx2, dpw, pool_sm, *sib = _pool_bwd(dx3, plb, x2, pwf, pb_f, ps_f, norm_mix[1:2], tm, tps, comm=px)
    nms, xchg = chip_sums(nms, parts, sib)
    dac0, dup0, fw0, *got = _bwd_down(dx2, up0, fdw_f[0], ffn_b_dw[0:1], wdn[0], tf, tpf, ffn_taps, "bwd_down0", comm=xchg)
    arrived(nms, got)
    p_down0 = _mm_tn(gb0t, dx2, F // 2, D, tw, False, "dw_down0")
    nms, parts, px = pair({"down0": p_down0.reshape(N_CHIPS, 2, fsh // 2, D)})
    dup0, dx1, dnf0, *sib = _bwd_up(dac0, dup0, wup[0], fdw_f[0], x1, norm_ffn[0:1], dx2, tf, tpf, ffn_taps, "bwd_up0", comm=px)
    nms, xchg = chip_sums(nms, parts, sib)
    p_up0, *got = _mm_tn(hb1t, dup0, D // 2, 4 * fsh, tw, True, "dw_up0", split=2, comm=xchg)
    arrived(nms, got)
    csh = cg // N_CHIPS
    p_pool = dpw.reshape(2, ng // 2, N_CHIPS, csh, cg).transpose(2, 0, 1, 3, 4).reshape(N_CHIPS, 2, (ng // 2) * csh, cg)
    nms, parts, px = pair({"up0": p_up0.reshape(N_CHIPS, 2, D // 2, 2 * fsh), "pool": p_pool.astype(BF16)})
    dc, conv_sm, *sib = _conv_bwd1(dx1, c, w2f, conv_ln_g, conv_ln_b, tm, comm=px)
    nms, xchg = chip_sums(nms, parts, sib)
    p_pw2 = _mm_tn(sbt, dx1, D, D, tw, False, "dw_pw2")
    nms2, parts2, px = pair({"pw2": p_pw2.reshape(N_CHIPS, 2, dsh // 2, D)})
    grad_x, dab, dwdw, db1, dnm0, *both = _conv_bwd2(dc, a, x2d, w1g, wdw_f, norm_mix[0:1], dx1, tm, tps, taps,
                                                     comm=_Both(xchg, px))
    arrived(nms, both[:len(nms)])
    nms, xchg = chip_sums(nms2, parts2, both[len(nms):])
    p_pw1, *got = _mm_tn(hb0t, dab, D, 2 * dsh, tw, True, "dw_pw1", comm=xchg)
    arrived(nms, got)
    nms, parts, _ = pair({"pw1": p_pw1.reshape(N_CHIPS, 2, D // 2, 2 * dsh)})
    nms, xchg = chip_sums(nms, parts, _pair_exchange(parts, "grad_pair_exchange_pw1"))
    (rd, rf), got = _small_allreduce([
        [(dnm0, 0, 1), (pool_sm, 2, 3), (dnf0, 0, 1), (dnf1, 0, 1), (db1, 0, 2), (conv_sm, 0, 4), (d_final, 0, 1),
         (pool_sm, 0, 2), (loss_part, 0, 1), 2, (dwdw, 0, dwdw.shape[0])],
        [(fw0, 0, 4), (fw1, 0, 4)]], xchg)
    arrived(nms, got)
    bufs = [_final_sum(*fs[nm], ids, "final_sum_" + nm) for nm in ("pw1", "pw2", "pool")]
    for nm in ("up", "down"):
        b = _final_sum(*fs[nm + "0"], ids, "final_sum_" + nm + "0", layer=0, n_layers=2)
        bufs.append(_final_sum(*fs[nm + "1"], ids, "final_sum_" + nm + "1", layer=1, n_layers=2, buf=b))
    shared = _comm_only(_ShareComm(bufs), "grad_share_halves")
    g_pw1, g_pw2, g_pool, g_up, g_down = [b.reshape(b.shape[0], 2 * b.shape[2], b.shape[3]) for b in shared]
    g_wdw = lax.dynamic_slice_in_dim(rd[16:16 + taps], qi * dsh, dsh, axis=1)
    g_ps = lax.dynamic_slice_in_dim(rd[11:12], qi * dsh, dsh, axis=1)
    g_pb = lax.dynamic_slice_in_dim(rd[12:13], qi * dsh, dsh, axis=1)
    g_fdw = lax.dynamic_slice_in_dim(rf.reshape(2, 4, F)[:, :ffn_taps], qi * fsh, fsh, axis=2).reshape(2 * ffn_taps, fsh)

    def big(w, g, m, v, nm):
        return tuple(t.reshape(w.shape) for t in _adamw(w.reshape(g.shape), g, m.reshape(g.shape), v.reshape(g.shape),
                                                        "adamw_" + nm))

    o_pw1 = big(conv_w_pw1, g_pw1, m_conv_w_pw1, v_conv_w_pw1, "pw1")
    o_pw2 = big(conv_w_pw2, g_pw2, m_conv_w_pw2, v_conv_w_pw2, "pw2")
    o_pool = big(pool_w, g_pool, m_pool_w, v_pool_w, "pool")
    o_up = big(ffn_w_up, g_up, m_ffn_w_up, v_ffn_w_up, "up")
    o_down = big(ffn_w_down, g_down, m_ffn_w_down, v_ffn_w_down, "down")

    snames = ["norm_mix", "norm_ffn", "b_pw1", "w_dw", "b_dw", "ln_g", "ln_b", "b_pw2", "pool_b", "pool_scale", "ffn_w_dw",
              "ffn_b_dw", "final_norm"]
    sw = [norm_mix, norm_ffn, conv_b_pw1, conv_w_dw, conv_b_dw, conv_ln_g, conv_ln_b, conv_b_pw2, pool_b, pool_scale,
          ffn_w_dw, ffn_b_dw, final_norm]
    smm = [m_norm_mix, m_norm_ffn, m_conv_b_pw1, m_conv_w_dw, m_conv_b_dw, m_conv_ln_g, m_conv_ln_b, m_conv_b_pw2, m_pool_b,
           m_pool_scale, m_ffn_w_dw, m_ffn_b_dw, m_final_norm]
    svv = [v_norm_mix, v_norm_ffn, v_conv_b_pw1, v_conv_w_dw, v_conv_b_dw, v_conv_ln_g, v_conv_ln_b, v_conv_b_pw2, v_pool_b,
           v_pool_scale, v_ffn_w_dw, v_ffn_b_dw, v_final_norm]
    sshapes = [tuple(w.shape) for w in sw]
    two_d = [(2, D), (2, D), (1, 2 * D), (taps, dsh), (1, D), (1, D), (1, D), (1, D), (1, dsh), (1, dsh), (2 * ffn_taps, fsh),
             (2, F), (1, D)]
    rows_d = [[0, 1], [2, 3], [4, 5], None, [9], [6], [7], [8], None, None, None, None, [10]]
    rows_f = [None] * 11 + [[3, 7], None]
    as2d = lambda ts: [t.reshape(s) for t, s in zip(ts, two_d)]
    loss, g_rep, sd, sm_new, sv_new = _adamw_small(rd, rf, [g_wdw, g_pb, g_ps, g_fdw], as2d(sw), as2d(smm), as2d(svv), rows_d,
                                                   rows_f, loss_row=13)
    sg = dict(g_rep)
    sg.update({3: g_wdw, 8: g_pb, 9: g_ps, 10: g_fdw})
    small_out = {n: tuple(t.reshape(sshapes[k]) for t in (sg[k], sd[k], sm_new[k], sv_new[k])) for k, n in enumerate(snames)}

    order = [small_out["norm_mix"], small_out["norm_ffn"], o_pw1, small_out["b_pw1"], small_out["w_dw"], small_out["b_dw"],
             small_out["ln_g"], small_out["ln_b"], o_pw2, small_out["b_pw2"], o_pool, small_out["pool_b"],
             small_out["pool_scale"], o_up, small_out["ffn_w_dw"], small_out["ffn_b_dw"], o_down, small_out["final_norm"]]
    return (loss[0, 0], grad_x.reshape(nb, seq, D), *[o[0] for o in order], *[o[1] for o in order], *[o[2] for o in order],
            *[o[3] for o in order])
```

```python
import functools

import jax
import jax.numpy as jnp
from jax import lax
from jax.experimental import pallas as pl
from jax.experimental.pallas import tpu as pltpu

F32, BF16 = jnp.float32, jnp.bfloat16
S_ = jax.ShapeDtypeStruct
MESH = pl.DeviceIdType.MESH

RMS_EPS, LN_EPS = 1e-6, 1e-5
POOL_WINDOWS = (2, 4, 8, 16)
ADAM_LR, ADAM_B1, ADAM_B2, ADAM_EPS, ADAM_WD, ADAM_STEP = 0.001, 0.9, 0.999, 1e-08, 0.01, 10

VMEM_LIMIT_BYTES = 60 * 1024 * 1024
N_CHIPS = 4
CHUNK = 32
HALO_CONV = 32
HALO_POOL = 32
HALO_FFN = 8
TILE_D = 512
TILE_F = 256
TILE_MM = 1024


def _pcall(body, *, name, grid, in_specs, out_specs, out_shape, args, scratch=(), aliases=None, prefetch=0, comm=None):
    params = pltpu.CompilerParams(dimension_semantics=("arbitrary",) * len(grid), vmem_limit_bytes=VMEM_LIMIT_BYTES)
    if comm is not None:
        assert not prefetch
        single = not isinstance(out_shape, (list, tuple))
        in_specs, args, scratch = list(in_specs), list(args), list(scratch)
        out_specs, out_shape = ([out_specs], [out_shape]) if single else (list(out_specs), list(out_shape))
        n_in, n_out, n_scr, n_cin, n_cout = len(in_specs), len(out_specs), len(scratch), len(comm.inputs), len(comm.out_shape)
        any_spec = pl.BlockSpec(memory_space=pl.ANY)
        aliases = dict(aliases or {})
        aliases.update({n_in + a: n_out + b for a, b in comm.aliases.items()})
        inner = body

        def body(*refs):
            ins, cin = refs[:n_in], refs[n_in:n_in + n_cin]
            outs = refs[n_in + n_cin:n_in + n_cin + n_out]
            cout = refs[n_in + n_cin + n_out:n_in + n_cin + n_out + n_cout]
            scr = refs[n_in + n_cin + n_out + n_cout:n_in + n_cin + n_out + n_cout + n_scr]
            sems = refs[n_in + n_cin + n_out + n_cout + n_scr:]
            step, total = 0, 1
            for d, g in enumerate(grid):
                step = step * g + pl.program_id(d)
                total *= g
            comm.before(cin, cout, sems, step, total)
            inner(*ins, *outs, *scr)
            comm.after(cin, cout, sems, step, total)

        in_specs += [any_spec] * n_cin
        args += list(comm.inputs)
        out_specs += [any_spec] * n_cout
        out_shape += list(comm.out_shape)
        scratch += list(comm.sems)
    if prefetch:
        spec = pltpu.PrefetchScalarGridSpec(num_scalar_prefetch=prefetch, grid=grid, in_specs=in_specs,
                                            out_specs=out_specs, scratch_shapes=list(scratch))
        return pl.pallas_call(body, name=name, grid_spec=spec, out_shape=out_shape,
                              input_output_aliases=aliases or {}, compiler_params=params)(*args)
    return pl.pallas_call(body, name=name, grid=grid, in_specs=in_specs, out_specs=out_specs, out_shape=out_shape,
                          scratch_shapes=list(scratch), input_output_aliases=aliases or {},
                          compiler_params=params)(*args)


def _full(shape):
    n = len(shape)
    return pl.BlockSpec(tuple(shape), lambda *_: (0,) * n, pipeline_mode=pl.Buffered(1))


def _rows(tm, c, col=0):
    return pl.BlockSpec((tm, c), lambda i, *_: (i, col))


def _cols(r, tm):
    return pl.BlockSpec((r, tm), lambda i, *_: (0, i))


def _prev_halo(tm, hb, c):
    return pl.BlockSpec((hb, c), lambda i, *_: (jnp.maximum(i * (tm // hb) - 1, 0), 0))


def _next_halo(tm, hb, c, total_rows):
    last = total_rows // hb - 1
    return pl.BlockSpec((hb, c), lambda i, *_: (jnp.minimum((i + 1) * (tm // hb), last), 0))


def _acc_spec(r, c):
    return pl.BlockSpec((r, c), lambda *_: (0, 0))


def _rms(x, g):
    r = lax.rsqrt(jnp.mean(x * x, axis=-1, keepdims=True) + RMS_EPS)
    xh = x * r
    return xh * g, xh, r


def _rms_bwd(dy, xh, r, g):
    dxh = dy * g
    dx = r * (dxh - xh * jnp.mean(dxh * xh, axis=-1, keepdims=True))
    return dx, jnp.sum(dy * xh, axis=0, keepdims=True)


def _colsum(v):
    return jnp.sum(v, axis=0, keepdims=True)


def _fold8(v):
    out = v[0:8]
    for j in range(1, v.shape[0] // 8):
        out = out + v[8 * j:8 * j + 8]
    return out


def _chunks(n_rows, fn):
    def step(j, carry):
        fn(pl.multiple_of(j * CHUNK, CHUNK))
        return carry
    lax.fori_loop(0, n_rows // CHUNK, step, 0)


def _dot(a, b):
    return jnp.dot(a, b, preferred_element_type=F32)


def _dot_nt(a, b):
    return lax.dot_general(a, b, (((1,), (1,)), ((), ())), preferred_element_type=F32)


def _dot_tn(a, b):
    return lax.dot_general(a, b, (((0,), (0,)), ((), ())), preferred_element_type=F32)


def _sig(v):
    return jax.nn.sigmoid(v)


def _accumulate(ref, val, first):
    @pl.when(first)
    def _():
        ref[...] = val

    @pl.when(jnp.logical_not(first))
    def _():
        ref[...] = ref[...] + val


def _conv_in(x, g, w1g, b1, tm, comm=None):
    T, D = x.shape
    ns, _, nc = w1g.shape

    def body(x_ref, g_ref, w_ref, b_ref, hbt_ref, a_ref):
        hb = _rms(x_ref[...], g_ref[...])[0].astype(BF16)
        hbt_ref[...] = hb.T
        for s in range(ns):
            a_ref[:, s * nc:(s + 1) * nc] = _dot(hb, w_ref[s]) + b_ref[:, s * nc:(s + 1) * nc]

    return _pcall(body, name="conv_in", grid=(T // tm,),
                  in_specs=[_rows(tm, D), _full((1, D)), _full(w1g.shape), _full((1, ns * nc))],
                  out_specs=[_cols(D, tm), _rows(tm, ns * nc)],
                  out_shape=[S_((D, T), BF16), S_((T, ns * nc), F32)], args=(x, g, w1g, b1), comm=comm)


def _conv_mid(a, x, wdw, bdw, lng, lnb, w2, b2, gn, tm, tps, taps, comm=None):
    T, D = x.shape
    hc = HALO_CONV

    def body(a_ref, ah_ref, x_ref, wdw_ref, bdw_ref, lng_ref, lnb_ref, w2_ref, b2_ref, gn_ref,
             c_ref, sbt_ref, x1_ref, hb_ref, hbt_ref, uext, shifted):
        i = pl.program_id(0)
        ah = ah_ref[...]
        uext[0:hc, :] = jnp.where(i % tps == 0, 0.0, ah[:, :D] * _sig(ah[:, D:]))
        av = a_ref[...]
        uext[hc:hc + tm, :] = av[:, :D] * _sig(av[:, D:])

        def conv(r0, src, ks, first):
            acc = jnp.broadcast_to(bdw_ref[...], (CHUNK, D)) if first else c_ref[pl.ds(r0, CHUNK), :]
            for k, o in ks:
                acc = acc + wdw_ref[k:k + 1, :] * src[pl.ds(pl.multiple_of(r0 + o, 8), CHUNK), :]
            c_ref[pl.ds(r0, CHUNK), :] = acc
        _by_sublane_shift(uext, shifted, tm, [hc - (taps - 1) + k for k in range(taps)], conv)

        cv = c_ref[...]
        xc = cv - jnp.mean(cv, axis=-1, keepdims=True)
        nh = xc * lax.rsqrt(jnp.mean(xc * xc, axis=-1, keepdims=True) + LN_EPS)
        n = nh * lng_ref[...] + lnb_ref[...]
        sb = (n * _sig(n)).astype(BF16)
        sbt_ref[...] = sb.T
        x1 = x_ref[...] + _dot(sb, w2_ref[...]) + b2_ref[...]
        x1_ref[...] = x1
        hb = _rms(x1, gn_ref[...])[0].astype(BF16)
        hb_ref[...] = hb
        hbt_ref[...] = hb.T

    return _pcall(body, name="conv_mid", grid=(T // tm,),
                  in_specs=[_rows(tm, 2 * D), _prev_halo(tm, hc, 2 * D), _rows(tm, D), _full(wdw.shape), _full((1, D)),
                            _full((1, D)), _full((1, D)), _full((D, D)), _full((1, D)), _full((1, D))],
                  out_specs=[_rows(tm, D), _cols(D, tm), _rows(tm, D), _rows(tm, D), _cols(D, tm)],
                  out_shape=[S_((T, D), F32), S_((D, T), BF16), S_((T, D), F32), S_((T, D), BF16), S_((D, T), BF16)],
                  scratch=[pltpu.VMEM((tm + hc, D), F32), pltpu.VMEM((tm + hc - 8, D), F32)],
                  args=(a, a, x, wdw, bdw, lng, lnb, w2, b2, gn), comm=comm)


def _ffn_taps(aext, realigned, tm, ffn_taps):
    for k in range(ffn_taps - 1):
        realigned[k] = aext[pl.ds(HALO_FFN - (ffn_taps - 1) + k, tm), :]
    return [realigned.at[k] for k in range(ffn_taps - 1)] + [aext.at[pl.ds(HALO_FFN, tm)]]


def _ffn_conv(taps, wdw_ref, bdw_ref):
    acc = bdw_ref[...]
    for k, tap in enumerate(taps):
        acc = acc + wdw_ref[k:k + 1, :] * tap[...]
    return acc


def _by_sublane_shift(ext, shifted, tm, offsets, per_chunk):
    first = True
    for b in range(8):
        ks = [(k, o - b) for k, o in enumerate(offsets) if o % 8 == b]
        if not ks:
            continue
        if b:
            shifted[...] = ext[pl.ds(b, shifted.shape[0]), :]
        _chunks(tm, functools.partial(per_chunk, src=shifted if b else ext, ks=ks, first=first))
        first = False


def _ffn_fwd(hb, xin, wg, wdw, bdw, wd, gn, tm, tps, ffn_taps, name, target=None, comm=None):
    T, D = xin.shape
    ns, _, nc = wg.shape
    F = ns * nc // 2
    hf = HALO_FFN
    hh = 2 * HALO_FFN
    final = target is not None

    def body(*refs):
        if final:
            (h_ref, hh_ref, x_ref, wg_ref, wdw_ref, bdw_ref, wd_ref, gn_ref, tg_ref,
             up_ref, gb_ref, dx_ref, loss_ref, dgn_ref, aext, realigned) = refs
        else:
            (h_ref, hh_ref, x_ref, wg_ref, wdw_ref, bdw_ref, wd_ref, gn_ref,
             up_ref, gb_ref, xo_ref, ho_ref, aext, realigned) = refs
        i = pl.program_id(0)
        hv = h_ref[...]
        for s in range(ns):
            up_ref[:, s * nc:(s + 1) * nc] = _dot(hv, wg_ref[s])
        halo = jnp.concatenate([_dot(hh_ref[...], wg_ref[s]) for s in range(ns // 2)], axis=1)[hh - hf:hh, :]
        aext[0:hf, :] = jnp.where(i % tps == 0, 0.0, halo)
        aext[hf:hf + tm, :] = up_ref[:, 0:F]
        acv = _ffn_conv(_ffn_taps(aext, realigned, tm, ffn_taps), wdw_ref, bdw_ref)
        gb = (acv * _sig(acv) * up_ref[:, F:2 * F]).astype(BF16)
        gb_ref[...] = gb.T
        xo = x_ref[...] + _dot(gb, wd_ref[...])
        if not final:
            xo_ref[...] = xo
            ho_ref[...] = _rms(xo, gn_ref[...])[0]
        else:
            out, xh, r = _rms(xo, gn_ref[...])
            diff = out - tg_ref[...]
            part = 0.5 / D * _colsum(diff * diff)
            dx, dg = _rms_bwd(diff * (1.0 / D), xh, r, gn_ref[...])
            dx_ref[...] = dx
            _accumulate(loss_ref, part, i == 0)
            _accumulate(dgn_ref, dg, i == 0)

    in_specs = [_rows(tm, D), _prev_halo(tm, hh, D), _rows(tm, D), _full(wg.shape), _full(wdw.shape), _full((1, F)),
                _full((F, D)), _full((1, D))]
    args = [hb, hb, xin, wg, wdw, bdw, wd, gn]
    if final:
        in_specs.append(_rows(tm, D))
        args.append(target)
        out_specs = [_rows(tm, 2 * F), _cols(F, tm), _rows(tm, D), _acc_spec(1, D), _acc_spec(1, D)]
        out_shape = [S_((T, 2 * F), F32), S_((F, T), BF16), S_((T, D), F32), S_((1, D), F32), S_((1, D), F32)]
    else:
        out_specs = [_rows(tm, 2 * F), _cols(F, tm), _rows(tm, D), _rows(tm, D)]
        out_shape = [S_((T, 2 * F), F32), S_((F, T), BF16), S_((T, D), F32), S_((T, D), F32)]
    return _pcall(body, name=name, grid=(T // tm,), in_specs=in_specs, out_specs=out_specs, out_shape=out_shape,
                  scratch=[pltpu.VMEM((tm + hf, F), F32), pltpu.VMEM((ffn_taps - 1, tm, F), F32)], args=args, comm=comm)


def _pool_counts(i, tps, tm, w):
    pos = (i % tps) * tm + lax.broadcasted_iota(jnp.int32, (tm, 1), 0)
    return jnp.minimum(pos + 1, w).astype(F32)


def _window_sums(src, buf_a, buf_b, cg, causal):
    assert POOL_WINDOWS == (2, 4, 8, 16)
    n = src.shape[0]
    levels = len(POOL_WINDOWS)
    cur, outs = src, []
    for l in range(levels):
        dst = buf_b if l % 2 else buf_a
        cols = slice(l * cg, levels * cg)
        lo, m = 8 * (l + 1), n - 8 * (l + 1)
        if causal:
            dst[lo:n, cols] = cur[lo:n, cols] + cur[pl.ds(lo - (1 << l), m), cols]
        else:
            dst[0:m, cols] = cur[0:m, cols] + cur[pl.ds(1 << l, m), cols]
        outs.append(dst)
        cur = dst
    return outs


def _pool_fwd(h, x2, pw, pb, ps, gn, tm, tps, comm=None):
    T, D = h.shape
    ng, cg, _ = pw.shape
    hp = HALO_POOL

    def body(h_ref, hh_ref, x_ref, pw_ref, pb_ref, ps_ref, gn_ref, pl_ref, xo_ref, hb_ref, hbt_ref, hext, buf_a, buf_b):
        i = pl.program_id(0)
        hext[0:hp, :] = jnp.where(i % tps == 0, 0.0, hh_ref[...])
        hext[hp:hp + tm, :] = h_ref[...]
        sums = _window_sums(hext, buf_a, buf_b, cg, True)
        for g, w in enumerate(POOL_WINDOWS):
            cs = slice(g * cg, (g + 1) * cg)
            pl_ref[:, cs] = (sums[g][hp:hp + tm, cs] / _pool_counts(i, tps, tm, w) - h_ref[:, cs]).astype(BF16)

        pv = pl_ref[...]
        mixed = jnp.concatenate([_dot(pv[:, g * cg:(g + 1) * cg], pw_ref[g]) for g in range(ng)], axis=1)
        xo = x_ref[...] + ps_ref[...] * (mixed + pb_ref[...])
        xo_ref[...] = xo
        hb = _rms(xo, gn_ref[...])[0].astype(BF16)
        hb_ref[...] = hb
        hbt_ref[...] = hb.T

    return _pcall(body, name="pool_fwd", grid=(T // tm,),
                  in_specs=[_rows(tm, D), _prev_halo(tm, hp, D), _rows(tm, D), _full(pw.shape), _full((1, D)), _full((1, D)),
                            _full((1, D))],
                  out_specs=[_rows(tm, D), _rows(tm, D), _rows(tm, D), _cols(D, tm)],
                  out_shape=[S_((T, D), BF16), S_((T, D), F32), S_((T, D), BF16), S_((D, T), BF16)],
                  scratch=[pltpu.VMEM((tm + hp, D), F32)] * 3, args=(h, h, x2, pw, pb, ps, gn), comm=comm)


def _ffn_bwd(dx, up, wd, wg, wdw, bdw, xin, gn, tm, tps, ffn_taps, name, comm=None):
    T, D = xin.shape
    F = up.shape[1] // 2
    fh = F // 2
    hf, hn = HALO_FFN, 2 * HALO_FFN
    n = tm + hn
    nt = T // tm
    k_last = ffn_taps - 1

    def body(dx_ref, dxn_ref, up_ref, uph_ref, upn_ref, wd_ref, wg_ref, wdw_ref, bdw_ref, x_ref, gn_ref,
             dup_ref, dxo_ref, dgn_ref, wsum_ref, aext, realigned, dext, dxe):
        i = pl.program_id(0)
        first, last = i % tps == 0, i % tps == tps - 1
        dxe[0:tm, :] = dx_ref[...].astype(BF16)
        dxe[tm:n, :] = dxn_ref[...].astype(BF16)
        dh = jnp.zeros((tm, D), F32)
        for half in range(2):
            lo = half * fh
            aext[0:hf, :] = jnp.where(first, 0.0, uph_ref[:, lo:lo + fh])
            aext[hf:hf + tm, :] = up_ref[:, lo:lo + fh]
            aext[hf + tm:hf + n, :] = upn_ref[:, lo:lo + fh]
            for k in range(k_last):
                realigned[k] = aext[pl.ds(hf - k_last + k, n), :]
            taps = [realigned.at[k] for k in range(k_last)] + [aext.at[pl.ds(hf, n)]]
            acv = bdw_ref[:, lo:lo + fh]
            for k, tap in enumerate(taps):
                acv = acv + wdw_ref[k:k + 1, lo:lo + fh] * tap[...]
            dg = _dot_nt(dxe[...], wd_ref[lo:lo + fh, :])
            sg = _sig(acv)
            dup_ref[:, F + lo:F + lo + fh] = (dg[0:tm] * acv[0:tm] * sg[0:tm]).astype(BF16)
            dac = dg * (sg * (1.0 + acv * (1.0 - sg)))
            dext[0:tm, :] = dac[0:tm] * up_ref[:, F + lo:F + lo + fh]
            dext[tm:n, :] = jnp.where(last, 0.0, dac[tm:n] * upn_ref[:, F + lo:F + lo + fh])
            dmain = dext[0:tm, :]
            dact = wdw_ref[k_last:k_last + 1, lo:lo + fh] * dmain
            for k in range(k_last):
                dact = dact + wdw_ref[k:k + 1, lo:lo + fh] * dext[pl.ds(k_last - k, tm), :]
            dup_ref[:, lo:lo + fh] = dact.astype(BF16)
            rows = [_colsum(dmain * tap[0:tm, :]) for tap in taps]
            rows += [_colsum(dmain), jnp.zeros((8 - ffn_taps - 1, fh), F32)]
            _accumulate(wsum_ref.at[:, lo:lo + fh], jnp.concatenate(rows, axis=0), i == 0)
            dh = dh + _dot_nt(dup_ref[:, lo:lo + fh], wg_ref[0, :, lo:lo + fh])
            dh = dh + _dot_nt(dup_ref[:, F + lo:F + lo + fh], wg_ref[1, :, lo:lo + fh])
        _, xh, r = _rms(x_ref[...], gn_ref[...])
        dxn, dgn = _rms_bwd(dh, xh, r, gn_ref[...])
        dxo_ref[...] = dx_ref[...] + dxn
        _accumulate(dgn_ref, dgn, i == 0)

    return _pcall(body, name=name, grid=(nt,),
                  in_specs=[_rows(tm, D), _next_halo(tm, hn, D, T), _rows(tm, 2 * F), _prev_halo(tm, hf, F),
                            _next_halo(tm, hn, 2 * F, T), _full((F, D)), _full(wg.shape), _full(wdw.shape), _full((1, F)),
                            _rows(tm, D), _full((1, D))],
                  out_specs=[_rows(tm, 2 * F), _rows(tm, D), _acc_spec(1, D), _acc_spec(8, F)],
                  out_shape=[S_((T, 2 * F), BF16), S_((T, D), F32), S_((1, D), F32), S_((8, F), F32)],
                  scratch=[pltpu.VMEM((hf + n, fh), F32), pltpu.VMEM((k_last, n, fh), F32), pltpu.VMEM((n, fh), F32),
                           pltpu.VMEM((n, D), BF16)],
                  args=(dx, dx, up, up, up, wd, wg, wdw, bdw, xin, gn), comm=comm)


def _bwd_down(dx, up, wdw, bdw, wd, tm, tps, ffn_taps, name, comm=None):
    T, D = dx.shape
    F = up.shape[1] // 2
    hf = HALO_FFN
    nt = T // tm

    def body(dx_ref, up_ref, uph_ref, wdw_ref, bdw_ref, wd_ref, dac_ref, dgate_ref, wsum_ref, aext, realigned):
        i = pl.program_id(0)
        aext[0:hf, :] = jnp.where(i % tps == 0, 0.0, uph_ref[...])
        aext[hf:hf + tm, :] = up_ref[:, 0:F]
        taps = _ffn_taps(aext, realigned, tm, ffn_taps)
        acv = _ffn_conv(taps, wdw_ref, bdw_ref)
        dg = _dot_nt(dx_ref[...].astype(BF16), wd_ref[...])
        sg = _sig(acv)
        dgate_ref[...] = (dg * acv * sg).astype(BF16)
        dac = dg * up_ref[:, F:2 * F] * (sg * (1.0 + acv * (1.0 - sg)))
        dac_ref[...] = dac
        rows = [_colsum(dac * tap[...]) for tap in taps]
        rows += [_colsum(dac), jnp.zeros((8 - ffn_taps - 1, F), F32)]
        _accumulate(wsum_ref, jnp.concatenate(rows, axis=0), i == 0)

    return _pcall(body, name=name, grid=(nt,),
                  in_specs=[_rows(tm, D), _rows(tm, 2 * F), _prev_halo(tm, hf, F), _full(wdw.shape), _full((1, F)),
                            _full((F, D))],
                  out_specs=[_rows(tm, F), _rows(tm, F, col=1), _acc_spec(8, F)],
                  out_shape=[S_((T, F), F32), S_((T, 2 * F), BF16), S_((8, F), F32)],
                  scratch=[pltpu.VMEM((tm + hf, F), F32), pltpu.VMEM((ffn_taps - 1, tm, F), F32)],
                  args=(dx, up, up, wdw, bdw, wd), comm=comm)


def _bwd_up(dac, dup, wg, wdw, xin, gn, dx, tm, tps, ffn_taps, name, comm=None):
    T, D = xin.shape
    F = dac.shape[1]
    ns, _, nc = wg.shape
    hf = HALO_FFN
    nt = T // tm

    def body(dac_ref, dach_ref, dgate_ref, wg_ref, wdw_ref, x_ref, gn_ref, dx_ref, dact_ref, dxo_ref, dgn_ref, dext):
        i = pl.program_id(0)
        dext[0:tm, :] = dac_ref[...]
        dext[tm:tm + hf, :] = jnp.where(i % tps == tps - 1, 0.0, dach_ref[...])
        dact = wdw_ref[ffn_taps - 1:ffn_taps, :] * dac_ref[...]
        for k in range(ffn_taps - 1):
            dact = dact + wdw_ref[k:k + 1, :] * dext[pl.ds(ffn_taps - 1 - k, tm), :]
        dact_ref[...] = dact.astype(BF16)

        dh = jnp.zeros((tm, D), F32)
        for s in range(ns):
            src = dact_ref if s < ns // 2 else dgate_ref
            o = (s % (ns // 2)) * nc
            dh = dh + _dot_nt(src[:, o:o + nc], wg_ref[s])
        _, xh, r = _rms(x_ref[...], gn_ref[...])
        dxn, dg = _rms_bwd(dh, xh, r, gn_ref[...])
        dxo_ref[...] = dx_ref[...] + dxn
        _accumulate(dgn_ref, dg, i == 0)

    return _pcall(body, name=name, grid=(nt,),
                  in_specs=[_rows(tm, F), _next_halo(tm, hf, F, T), _rows(tm, F, col=1), _full(wg.shape), _full(wdw.shape),
                            _rows(tm, D), _full((1, D)), _rows(tm, D)],
                  out_specs=[_rows(tm, F), _rows(tm, D), _acc_spec(1, D)],
                  out_shape=[S_((T, 2 * F), BF16), S_((T, D), F32), S_((1, D), F32)],
                  scratch=[pltpu.VMEM((tm + hf, F), F32)],
                  aliases={2: 0}, args=(dac, dac, dup, wg, wdw, xin, gn, dx), comm=comm)


def _mm_tn(xt, dy, bk, bn, tt, shard_major, name, split=1, comm=None):
    K, T = xt.shape
    N = dy.shape[1]
    nt = T // tt
    w = bn // split

    def body(x_ref, dy_ref, o_ref, acc):
        t = pl.program_id(2)
        part = _dot(x_ref[...], dy_ref[...].astype(BF16))
        _accumulate(acc, part, t == 0)

        @pl.when(t == nt - 1)
        def _():
            if shard_major:
                for p in range(split):
                    o_ref[p] = acc[:, p * w:(p + 1) * w].astype(o_ref.dtype)
            else:
                o_ref[...] = acc[...].astype(o_ref.dtype)

    if shard_major:
        out_spec = pl.BlockSpec((split, bk, w), lambda i, j, t: (j, i, 0))
        out_shape = S_((N // w, K, w), BF16)
    else:
        out_spec = pl.BlockSpec((bk, bn), lambda i, j, t: (i, j))
        out_shape = S_((K, N), BF16)
    return _pcall(body, name=name, grid=(K // bk, N // bn, nt),
                  in_specs=[pl.BlockSpec((bk, tt), lambda i, j, t: (i, t)), pl.BlockSpec((tt, bn), lambda i, j, t: (t, j))],
                  out_specs=out_spec, out_shape=out_shape, scratch=[pltpu.VMEM((bk, bn), F32)], args=(xt, dy), comm=comm)


def _pool_bwd(dx3, plb, x2, pw, pb, ps, gn, tm, tps, comm=None):
    T, D = x2.shape
    ng, cg, _ = pw.shape
    hp = HALO_POOL
    nt = T // tm

    def body(do_ref, doh_ref, pl_ref, x_ref, pw_ref, pb_ref, ps_ref, gn_ref, dxo_ref, dpw_ref, sm_ref, qext, buf_a, buf_b,
             dh_s):
        i = pl.program_id(0)
        do = do_ref[...]
        dm = do * ps_ref[...]
        dmh = jnp.where(i % tps == tps - 1, 0.0, doh_ref[...]) * ps_ref[...]
        pv = pl_ref[...]
        mixed = jnp.concatenate([_dot(pv[:, g * cg:(g + 1) * cg], pw_ref[g]) for g in range(ng)], axis=1)
        dscale = _colsum(do * (mixed + pb_ref[...]))
        dbias = _colsum(dm)
        dmb = dm.astype(BF16)
        dmhb = dmh.astype(BF16)
        for g, w in enumerate(POOL_WINDOWS):
            cs = slice(g * cg, (g + 1) * cg)
            _accumulate(dpw_ref.at[g], _dot_tn(pv[:, cs], dmb[:, cs]), i == 0)
            dpo = _dot_nt(dmb[:, cs], pw_ref[g])
            dh_s[:, cs] = dpo
            qext[0:tm, cs] = dpo / _pool_counts(i, tps, tm, w)
            qext[tm:tm + hp, cs] = _dot_nt(dmhb[:, cs], pw_ref[g]) * (1.0 / w)
        sums = _window_sums(qext, buf_a, buf_b, cg, False)
        for g in range(ng):
            cs = slice(g * cg, (g + 1) * cg)
            dh_s[:, cs] = sums[g][0:tm, cs] - dh_s[:, cs]

        _, xh, r = _rms(x_ref[...], gn_ref[...])
        dxn, dg = _rms_bwd(dh_s[...], xh, r, gn_ref[...])
        dxo_ref[...] = do + dxn
        sm = jnp.concatenate([dscale, dbias, dg, jnp.zeros((5, D), F32)], axis=0)
        _accumulate(sm_ref, sm, i == 0)

    return _pcall(body, name="pool_bwd", grid=(nt,),
                  in_specs=[_rows(tm, D), _next_halo(tm, hp, D, T), _rows(tm, D), _rows(tm, D), _full(pw.shape), _full((1, D)),
                            _full((1, D)), _full((1, D))],
                  out_specs=[_rows(tm, D), pl.BlockSpec((ng, cg, cg), lambda i: (0, 0, 0)), _acc_spec(8, D)],
                  out_shape=[S_((T, D), F32), S_((ng, cg, cg), F32), S_((8, D), F32)],
                  scratch=[pltpu.VMEM((tm + hp, D), F32)] * 3 + [pltpu.VMEM((tm, D), F32)],
                  args=(dx3, dx3, plb, x2, pw, pb, ps, gn), comm=comm)


def _conv_bwd1(dx1, c, w2, lng, lnb, tm, comm=None):
    T, D = dx1.shape

    def body(do_ref, c_ref, w2_ref, lng_ref, lnb_ref, dc_ref, sm_ref):
        i = pl.program_id(0)
        do = do_ref[...]
        ds = _dot_nt(do.astype(BF16), w2_ref[...])
        cv = c_ref[...]
        xc = cv - jnp.mean(cv, axis=-1, keepdims=True)
        rstd = lax.rsqrt(jnp.mean(xc * xc, axis=-1, keepdims=True) + LN_EPS)
        nh = xc * rstd
        n = nh * lng_ref[...] + lnb_ref[...]
        sg = _sig(n)
        dn = ds * (sg * (1.0 + n * (1.0 - sg)))
        dnh = dn * lng_ref[...]
        dc = rstd * (dnh - jnp.mean(dnh, axis=-1, keepdims=True) - nh * jnp.mean(dnh * nh, axis=-1, keepdims=True))
        dc_ref[...] = dc
        sm = jnp.concatenate([_colsum(dn * nh), _colsum(dn), _colsum(do), _colsum(dc), jnp.zeros((4, D), F32)], axis=0)
        _accumulate(sm_ref, sm, i == 0)

    return _pcall(body, name="conv_bwd1", grid=(T // tm,),
                  in_specs=[_rows(tm, D), _rows(tm, D), _full((D, D)), _full((1, D)), _full((1, D))],
                  out_specs=[_rows(tm, D), _acc_spec(8, D)],
                  out_shape=[S_((T, D), F32), S_((8, D), F32)], args=(dx1, c, w2, lng, lnb), comm=comm)


def _conv_bwd2(dc, a, x, w1g, wdw, gn, dx1, tm, tps, taps, comm=None):
    T, D = x.shape
    ns, _, nc = w1g.shape
    hc = HALO_CONV
    nt = T // tm
    tp = wdw.shape[0]

    def body(dc_ref, dch_ref, a_ref, ah_ref, x_ref, w_ref, wdw_ref, gn_ref, dx1_ref,
             gx_ref, dab_ref, dw_ref, db1_ref, dgn_ref, uext, dext, shifted, du_s, wacc):
        i = pl.program_id(0)
        ah = ah_ref[...]
        uext[0:hc, :] = jnp.where(i % tps == 0, 0.0, ah[:, :D] * _sig(ah[:, D:]))
        av = a_ref[...]
        sg2 = _sig(av[:, D:])
        uext[hc:hc + tm, :] = av[:, :D] * sg2
        dext[0:tm, :] = dc_ref[...]
        dext[tm:tm + hc, :] = jnp.where(i % tps == tps - 1, 0.0, dch_ref[...])

        @pl.when(i == 0)
        def _():
            wacc[...] = jnp.zeros_like(wacc)

        def wgrad(r0, src, ks, first):
            d = dc_ref[pl.ds(r0, CHUNK), :]
            for k, o in ks:
                wacc[8 * k:8 * k + 8, :] += _fold8(d * src[pl.ds(pl.multiple_of(r0 + o, 8), CHUNK), :])
        _by_sublane_shift(uext, shifted, tm, [hc - (taps - 1) + k for k in range(taps)], wgrad)

        def convt(r0, src, ks, first):
            acc = jnp.zeros((CHUNK, D), F32) if first else du_s[pl.ds(r0, CHUNK), :]
            for k, o in ks:
                acc = acc + wdw_ref[k:k + 1, :] * src[pl.ds(pl.multiple_of(r0 + o, 8), CHUNK), :]
            du_s[pl.ds(r0, CHUNK), :] = acc
        _by_sublane_shift(dext, shifted, tm, [taps - 1 - k for k in range(taps)], convt)

        du = du_s[...]
        da1 = du * sg2
        da2 = du * av[:, :D] * (sg2 * (1.0 - sg2))
        dab_ref[:, 0:D] = da1.astype(BF16)
        dab_ref[:, D:2 * D] = da2.astype(BF16)
        _accumulate(db1_ref, jnp.concatenate([_colsum(da1), _colsum(da2)], axis=0), i == 0)
        dh = jnp.zeros((tm, D), F32)
        for s in range(ns):
            dh = dh + _dot_nt(dab_ref[:, s * nc:(s + 1) * nc], w_ref[s])
        _, xh, r = _rms(x_ref[...], gn_ref[...])
        dxn, dg = _rms_bwd(dh, xh, r, gn_ref[...])
        gx_ref[...] = dx1_ref[...] + dxn
        _accumulate(dgn_ref, dg, i == 0)

        @pl.when(i == nt - 1)
        def _():
            dw_ref[...] = jnp.zeros_like(dw_ref)
            for k in range(taps):
                dw_ref[k:k + 1, :] = _colsum(wacc[8 * k:8 * k + 8, :])

    return _pcall(body, name="conv_bwd2", grid=(nt,),
                  in_specs=[_rows(tm, D), _next_halo(tm, hc, D, T), _rows(tm, 2 * D), _prev_halo(tm, hc, 2 * D), _rows(tm, D),
                            _full(w1g.shape), _full(wdw.shape), _full((1, D)), _rows(tm, D)],
                  out_specs=[_rows(tm, D), _rows(tm, 2 * D), _acc_spec(tp, D), _acc_spec(2, D), _acc_spec(1, D)],
                  out_shape=[S_((T, D), F32), S_((T, 2 * D), BF16), S_((tp, D), F32), S_((2, D), F32), S_((1, D), F32)],
                  scratch=[pltpu.VMEM((tm + hc, D), F32), pltpu.VMEM((tm + hc, D), F32), pltpu.VMEM((tm + hc - 8, D), F32),
                           pltpu.VMEM((tm, D), F32), pltpu.VMEM((8 * taps, D), F32)],
                  args=(dc, dc, a, a, x, w1g, wdw, gn, dx1), comm=comm)


def _place():
    x, y, c = lax.axis_index("x"), lax.axis_index("y"), lax.axis_index("c")
    chips = [(1 - x, y), (x, 1 - y), (1 - x, 1 - y)]
    return x, y, c, chips


class _GatherComm:
    def __init__(self, slots):
        n = len(slots)
        self.inputs = list(slots)
        self.out_shape = [S_(s.shape, s.dtype) for s in slots]
        self.aliases = {i: i for i in range(n)}
        self.sems = [pltpu.SemaphoreType.DMA((n, 3))] * 4
        self.halves = [s.shape[1] // 2 for s in slots]
        self.paired = [s.shape[0] == N_CHIPS // 2 for s in slots]
        self.widths = [s.shape[2] // 2 if p else s.shape[2] for s, p in zip(slots, self.paired)]

    def _copies(self, outs, sems):
        s_ici, r_ici, s_d2d, r_d2d = sems
        x, y, c, chips = _place()
        peers = [(j, chip, 2 * chip[0] + chip[1]) for j, chip in enumerate(chips)]

        def block(i, chip, h):
            rows = pl.ds(pl.multiple_of(h * self.halves[i], 8), self.halves[i])
            if self.paired[i]:
                return outs[i].at[chip // 2, rows, pl.ds(pl.multiple_of((chip % 2) * self.widths[i], 128), self.widths[i])]
            return outs[i].at[chip, rows]

        def ici(i, j, chip, to):
            blk = block(i, chip, c)
            return pltpu.make_async_remote_copy(src_ref=blk, dst_ref=blk, send_sem=s_ici.at[i, j], recv_sem=r_ici.at[i, j],
                                                device_id=to, device_id_type=MESH)

        def d2d(i, j, chip, h):
            blk = block(i, chip, h)
            return pltpu.make_async_remote_copy(src_ref=blk, dst_ref=blk, send_sem=s_d2d.at[i, j], recv_sem=r_d2d.at[i, j],
                                                device_id=(x, y, 1 - c), device_id_type=MESH)
        return (x, y, c, 2 * x + y), peers, ici, d2d

    def before(self, cin, outs, sems, step, total):
        (x, y, c, q), peers, ici, d2d = self._copies(outs, sems)

        @pl.when(step == 0)
        def _():
            for i in range(len(outs)):
                for j, chip, _ in peers:
                    ici(i, j, q, (*chip, c)).start()

    def after(self, cin, outs, sems, step, total):
        (x, y, c, q), peers, ici, d2d = self._copies(outs, sems)

        @pl.when(step == max(total - 2, 0))
        def _():
            for i in range(len(outs)):
                for j, chip, qj in peers:
                    ici(i, j, qj, (x, y, c)).wait_recv()
                    d2d(i, j, qj, c).start()

        @pl.when(step == total - 1)
        def _():
            for i in range(len(outs)):
                for j, chip, qj in peers:
                    d2d(i, j, qj, 1 - c).wait_recv()
            for i in range(len(outs)):
                for j, chip, qj in peers:
                    ici(i, j, q, (*chip, c)).wait_send()
                    d2d(i, j, qj, c).wait_send()


class _ExchangeComm:
    def __init__(self, sums):
        n = len(sums)
        self.inputs = list(sums)
        self.out_shape = [S_((3,) + s.shape[1:], s.dtype) for s in sums]
        self.aliases = {}
        self.sems = [pltpu.SemaphoreType.DMA((n, 3))] * 2

    def _copies(self, srcs, outs, sems):
        s_sem, r_sem = sems
        x, y, c, chips = _place()
        return [pltpu.make_async_remote_copy(src_ref=srcs[i].at[2 * chip[0] + chip[1]], dst_ref=outs[i].at[j],
                                             send_sem=s_sem.at[i, j], recv_sem=r_sem.at[i, j],
                                             device_id=(*chip, c), device_id_type=MESH)
                for i in range(len(srcs)) for j, chip in enumerate(chips)]

    def before(self, srcs, outs, sems, step, total):
        @pl.when(step == 0)
        def _():
            for cp in self._copies(srcs, outs, sems):
                cp.start()

    def after(self, srcs, outs, sems, step, total):
        @pl.when(step == total - 1)
        def _():
            for cp in self._copies(srcs, outs, sems):
                cp.wait()


def _own_slots(specs):
    n = len(specs)
    shapes, blocks, count = [], [], []
    for a, (w, layer, dtype, pairs) in enumerate(specs):
        _, r, c = w.shape
        br = _row_block(r, c, 1 << 21)
        shapes.append(S_((N_CHIPS // 2, r, 2 * c) if pairs else (N_CHIPS, r, c), dtype))
        blocks += [(a, k, k * br, br) for k in range(r // br)]
        count.append(r // br)
    gather = _GatherComm(shapes[:1])
    n_sems = len(gather.sems)

    def body(*refs):
        srcs, outs = refs[:n], refs[n:2 * n]
        inb, outb = refs[2 * n:3 * n], refs[3 * n:4 * n]
        lsem, ssem = refs[4 * n], refs[4 * n + 1]
        gsems = refs[4 * n + 2:4 * n + 2 + n_sems]
        x, y, c, _ = _place()
        q = 2 * x + y

        def load(a, k, r0, br):
            return pltpu.make_async_copy(srcs[a].at[specs[a][1], pl.ds(r0, br)], inb[a].at[k % 2], lsem.at[a, k % 2])

        def store(a, k, r0, br):
            if specs[a][3]:
                wc = specs[a][0].shape[2]
                dst = outs[a].at[q // 2, pl.ds(r0, br), pl.ds(pl.multiple_of((q % 2) * wc, 128), wc)]
            else:
                dst = outs[a].at[q, pl.ds(r0, br)]
            return pltpu.make_async_copy(outb[a].at[k % 2], dst, ssem.at[a, k % 2])

        load(*blocks[0]).start()
        for b, (a, k, r0, br) in enumerate(blocks):
            if b + 1 < len(blocks):
                load(*blocks[b + 1]).start()
            load(a, k, r0, br).wait()
            if k >= 2:
                store(a, k - 2, r0 - 2 * br, br).wait()
            outb[a][k % 2] = inb[a][k % 2].astype(specs[a][2])
            store(a, k, r0, br).start()
            if a == 0 and k == count[0] - 1:
                for kk in range(max(k - 1, 0), k + 1):
                    store(0, kk, kk * br, br).wait()
                gather.before(None, outs[:1], gsems, 0, 1)
        for a in range(1, n):
            br = blocks[sum(count[:a])][3]
            for kk in range(max(count[a] - 2, 0), count[a]):
                store(a, kk, kk * br, br).wait()
        gather.after(None, outs[:1], gsems, 0, 1)

    any_spec = pl.BlockSpec(memory_space=pl.ANY)
    scratch = ([pltpu.VMEM((2, blocks[sum(count[:a])][3], specs[a][0].shape[2]), F32) for a in range(n)]
               + [pltpu.VMEM((2, blocks[sum(count[:a])][3], specs[a][0].shape[2]), specs[a][2]) for a in range(n)]
               + [pltpu.SemaphoreType.DMA((n, 2)), pltpu.SemaphoreType.DMA((n, 2))] + list(gather.sems))
    outs = pl.pallas_call(body, name="own_slots", in_specs=[any_spec] * n, out_specs=[any_spec] * n, out_shape=shapes,
                          scratch_shapes=scratch,
                          compiler_params=pltpu.CompilerParams(vmem_limit_bytes=VMEM_LIMIT_BYTES))(*[s[0] for s in specs])
    return list(outs)


def _comm_only(comm, name):
    def body():
        pass
    outs = _pcall(body, name=name, grid=(1,), in_specs=[], out_specs=[], out_shape=[], args=[], comm=comm)
    return list(outs)


def _pair_exchange(parts, name):
    n = len(parts)

    def body(*refs):
        srcs, outs = refs[:n], refs[n:2 * n]
        s_sem, r_sem = refs[2 * n:]
        x, y, c, _ = _place()
        cps = []
        for i in range(n):
            for s in range(N_CHIPS):
                cp = pltpu.make_async_remote_copy(src_ref=srcs[i].at[s, 1 - c], dst_ref=outs[i].at[s], send_sem=s_sem.at[i, s],
                                                  recv_sem=r_sem.at[i, s], device_id=(x, y, 1 - c), device_id_type=MESH)
                cp.start()
                cps.append(cp)
        for cp in cps:
            cp.wait()

    any_spec = pl.BlockSpec(memory_space=pl.ANY)
    outs = pl.pallas_call(
        body, name=name, in_specs=[any_spec] * n, out_specs=[any_spec] * n,
        out_shape=[S_((N_CHIPS,) + p.shape[2:], p.dtype) for p in parts],
        scratch_shapes=[pltpu.SemaphoreType.DMA((n, N_CHIPS))] * 2,
    )(*parts)
    return list(outs)


class _PairComm:
    def __init__(self, parts):
        n = len(parts)
        self.inputs = list(parts)
        self.out_shape = [S_((N_CHIPS,) + p.shape[2:], p.dtype) for p in parts]
        self.aliases = {}
        self.sems = [pltpu.SemaphoreType.DMA((n, N_CHIPS))] * 2

    def _copies(self, srcs, outs, sems):
        s_sem, r_sem = sems
        x, y, c, _ = _place()
        return [pltpu.make_async_remote_copy(src_ref=srcs[i].at[s, 1 - c], dst_ref=outs[i].at[s], send_sem=s_sem.at[i, s],
                                             recv_sem=r_sem.at[i, s], device_id=(x, y, 1 - c), device_id_type=MESH)
                for i in range(len(srcs)) for s in range(N_CHIPS)]

    def before(self, srcs, outs, sems, step, total):
        @pl.when(step == 0)
        def _():
            for cp in self._copies(srcs, outs, sems):
                cp.start()

    def after(self, srcs, outs, sems, step, total):
        @pl.when(step == total - 1)
        def _():
            for cp in self._copies(srcs, outs, sems):
                cp.wait()


class _ShareComm:
    def __init__(self, bufs):
        n = len(bufs)
        self.inputs = list(bufs)
        self.out_shape = [S_(b.shape, b.dtype) for b in bufs]
        self.aliases = {i: i for i in range(n)}
        self.sems = [pltpu.SemaphoreType.DMA((n, max(b.shape[0] for b in bufs)))] * 2
        self.layers = [b.shape[0] for b in bufs]

    def _copies(self, outs, sems):
        s_sem, r_sem = sems
        x, y, c, _ = _place()
        cps = []
        for i, nl in enumerate(self.layers):
            for l in range(nl):
                blk = outs[i].at[l, c]
                cps.append(pltpu.make_async_remote_copy(src_ref=blk, dst_ref=blk, send_sem=s_sem.at[i, l],
                                                        recv_sem=r_sem.at[i, l], device_id=(x, y, 1 - c),
                                                        device_id_type=MESH))
        return cps

    def before(self, cin, outs, sems, step, total):
        @pl.when(step == 0)
        def _():
            for cp in self._copies(outs, sems):
                cp.start()

    def after(self, cin, outs, sems, step, total):
        @pl.when(step == total - 1)
        def _():
            for cp in self._copies(outs, sems):
                cp.wait()


class _Both:
    def __init__(self, a, b):
        self.parts = (a, b)
        self.inputs = a.inputs + b.inputs
        self.out_shape = a.out_shape + b.out_shape
        self.aliases = dict(a.aliases)
        self.aliases.update({len(a.inputs) + i: len(a.out_shape) + o for i, o in b.aliases.items()})
        self.sems = a.sems + b.sems

    def _each(self, phase, cin, cout, sems, step, total):
        i = o = s = 0
        for p in self.parts:
            ni, no, ns = len(p.inputs), len(p.out_shape), len(p.sems)
            getattr(p, phase)(cin[i:i + ni], cout[o:o + no], sems[s:s + ns], step, total)
            i, o, s = i + ni, o + no, s + ns

    def before(self, cin, cout, sems, step, total):
        self._each("before", cin, cout, sems, step, total)

    def after(self, cin, cout, sems, step, total):
        self._each("after", cin, cout, sems, step, total)


def _small_allreduce(groups, comm):
    n = len(groups)
    nci, nco, ncs = len(comm.inputs), len(comm.out_shape), len(comm.sems)
    arrays = []
    for grp in groups:
        for p in grp:
            if not isinstance(p, int) and not any(p[0] is a for a in arrays):
                arrays.append(p[0])
    shapes = []
    for grp in groups:
        rows = sum(p if isinstance(p, int) else p[2] - p[1] for p in grp)
        width = next(p[0].shape[1] for p in grp if not isinstance(p, int))
        assert rows % 8 == 0
        shapes.append((rows, width))
    na = len(arrays)

    def body(*refs):
        a_refs, cin = refs[:na], refs[na:na + nci]
        o_refs, cout = refs[na + nci:na + nci + n], refs[na + nci + n:na + nci + n + nco]
        rest = refs[na + nci + n + nco:]
        v_refs, sibs, css, gots = rest[:n], rest[n:2 * n], rest[2 * n:3 * n], rest[3 * n:4 * n]
        s_sem, r_sem = rest[4 * n:4 * n + 2]
        csems = rest[4 * n + 2:]
        for i, grp in enumerate(groups):
            o = 0
            for p in grp:
                if isinstance(p, int):
                    v_refs[i][o:o + p, :] = jnp.zeros((p, shapes[i][1]), F32)
                    o += p
                else:
                    k = next(t for t, a in enumerate(arrays) if a is p[0])
                    v_refs[i][o:o + p[2] - p[1], :] = a_refs[k][p[1]:p[2], :]
                    o += p[2] - p[1]
        x, y, c, chips = _place()
        q = 2 * x + y
        to_sib = [pltpu.make_async_remote_copy(src_ref=v_refs[i], dst_ref=sibs[i], send_sem=s_sem.at[i, 3], recv_sem=r_sem.at[i, 3],
                                               device_id=(x, y, 1 - c), device_id_type=MESH) for i in range(n)]
        for cp in to_sib:
            cp.start()
        cps = []
        for i in range(n):
            to_sib[i].wait()
            mine, other = v_refs[i][...], sibs[i][...]
            css[i][...] = jnp.where(c == 0, mine, other) + jnp.where(c == 0, other, mine)
            for j, chip in enumerate(chips):
                cp = pltpu.make_async_remote_copy(src_ref=css[i], dst_ref=gots[i].at[j], send_sem=s_sem.at[i, j],
                                                  recv_sem=r_sem.at[i, j], device_id=(*chip, c), device_id_type=MESH)
                cp.start()
                cps.append(cp)
        comm.before(cin, cout, csems, 0, 1)
        for cp in cps:
            cp.wait()
        flips = [2, 1, 3]
        for i in range(n):
            total = None
            for k in range(N_CHIPS):
                d = q ^ k
                term = jnp.where(d == 0, css[i][...], 0.0)
                for j in range(3):
                    term = jnp.where(d == flips[j], gots[i][j], term)
                total = term if total is None else total + term
            o_refs[i][...] = total
        comm.after(cin, cout, csems, 0, 1)

    vm, any_spec = pl.BlockSpec(memory_space=pltpu.VMEM), pl.BlockSpec(memory_space=pl.ANY)
    outs = pl.pallas_call(
        body, name="small_allreduce", in_specs=[vm] * na + [any_spec] * nci, out_specs=[vm] * n + [any_spec] * nco,
        out_shape=[S_(s, F32) for s in shapes] + list(comm.out_shape),
        scratch_shapes=[pltpu.VMEM(s, F32) for s in shapes] * 3 + [pltpu.VMEM((3,) + s, F32) for s in shapes]
        + [pltpu.SemaphoreType.DMA((n, 4)), pltpu.SemaphoreType.DMA((n, 4))] + list(comm.sems),
    )(*arrays, *comm.inputs)
    return list(outs[:n]), list(outs[n:])


def _row_block(r, c, target_bytes=1 << 20):
    br = r
    while br % 32 == 0 and br * c * 4 > target_bytes:
        br //= 2
    return br


def _chip_sum(part, sib, ids, name):
    _, _, r, c = part.shape
    br = _row_block(r, c, 1 << 22)

    def body(ids_ref, p_ref, s_ref, o_ref):
        o_ref[...] = (p_ref[...].astype(F32) + s_ref[...].astype(F32)).astype(BF16)

    return _pcall(body, name=name, grid=(N_CHIPS, r // br), prefetch=1,
                  in_specs=[pl.BlockSpec((None, None, br, c), lambda s, i, ids: (s, ids[1], i, 0)),
                            pl.BlockSpec((None, br, c), lambda s, i, ids: (s, i, 0))],
                  out_specs=pl.BlockSpec((None, br, c), lambda s, i, ids: (s, i, 0)),
                  out_shape=S_((N_CHIPS, r, c), BF16), args=(ids, part, sib))


def _final_sum(part, sib, got, ids, name, layer=0, n_layers=1, buf=None):
    _, r, c = got.shape
    br = _row_block(r, c, 1 << 21)

    def body(ids_ref, p_ref, s_ref, got_ref, *rest):
        v = p_ref[...].astype(F32) + s_ref[...].astype(F32)
        for j in range(3):
            v = v + got_ref[j].astype(F32)
        rest[-1][...] = v

    in_specs = [pl.BlockSpec((None, None, br, c), lambda i, ids: (ids[0], ids[1], i, 0)),
                pl.BlockSpec((None, br, c), lambda i, ids: (ids[0], i, 0)),
                pl.BlockSpec((3, br, c), lambda i, ids: (0, i, 0))]
    args = [ids, part, sib, got]
    if buf is not None:
        in_specs.append(pl.BlockSpec(memory_space=pl.ANY))
        args.append(buf)
    return _pcall(body, name=name, grid=(r // br,), prefetch=1, in_specs=in_specs,
                  out_specs=pl.BlockSpec((None, None, br, c), lambda i, ids: (layer, ids[1], i, 0)),
                  out_shape=S_((n_layers, 2, r, c), F32), args=args, aliases={4: 0} if buf is not None else None)


def _adam_math(w, g, m, v):
    c1 = 1.0 / (1.0 - ADAM_B1 ** ADAM_STEP)
    c2 = 1.0 / (1.0 - ADAM_B2 ** ADAM_STEP)
    mn = ADAM_B1 * m + (1.0 - ADAM_B1) * g
    vn = ADAM_B2 * v + (1.0 - ADAM_B2) * (g * g)
    return -ADAM_LR * ((mn * c1) / (jnp.sqrt(vn * c2) + ADAM_EPS) + ADAM_WD * w), mn, vn


def _adamw(w, g, m, v, name):
    nl, r, c = w.shape
    br = _row_block(r, c, 1 << 20)

    def body(w_ref, g_ref, m_ref, v_ref, go_ref, d_ref, mo_ref, vo_ref):
        gv = g_ref[...]
        go_ref[...] = gv
        d_ref[...], mo_ref[...], vo_ref[...] = _adam_math(w_ref[...], gv, m_ref[...], v_ref[...])

    blk = pl.BlockSpec((None, br, c), lambda l, i: (l, i, 0))
    return _pcall(body, name=name, grid=(nl, r // br), in_specs=[blk] * 4, out_specs=[blk] * 4,
                  out_shape=[S_((nl, r, c), F32)] * 4, args=(w, g, m, v))


def _adamw_small(rd, rf, sharded_g, ws, ms, vs, rows_d, rows_f, loss_row):
    n = len(ws)
    ns = len(sharded_g)
    rep = [k for k in range(n) if rows_d[k] is not None or rows_f[k] is not None]

    def body(*refs):
        rd_ref, rf_ref = refs[0], refs[1]
        sg = refs[2:2 + ns]
        w_refs, m_refs, v_refs = (refs[2 + ns + t * n:2 + ns + (t + 1) * n] for t in range(3))
        loss_ref, outs = refs[2 + ns + 3 * n], refs[3 + ns + 3 * n:]
        g_out, d_out, m_out, v_out = outs[:len(rep)], outs[len(rep):len(rep) + n], outs[len(rep) + n:len(rep) + 2 * n], \
            outs[len(rep) + 2 * n:]
        loss_ref[...] = jnp.sum(rd_ref[loss_row:loss_row + 1, :], axis=1, keepdims=True)
        si = 0
        for k in range(n):
            shape = w_refs[k].shape
            if rows_d[k] is not None or rows_f[k] is not None:
                src, rws = (rd_ref, rows_d[k]) if rows_d[k] is not None else (rf_ref, rows_f[k])
                axis = 0 if shape[0] == len(rws) else 1
                g = src[rws[0]:rws[0] + 1, :]
                if len(rws) > 1:
                    g = jnp.concatenate([src[r:r + 1, :] for r in rws], axis=axis)
                g_out[rep.index(k)][...] = g
            else:
                g = sg[si][...]
                si += 1
            d_out[k][...], m_out[k][...], v_out[k][...] = _adam_math(w_refs[k][...], g, m_refs[k][...], v_refs[k][...])

    vm = pl.BlockSpec(memory_space=pltpu.VMEM)
    shapes = [S_(w.shape, F32) for w in ws]
    outs = pl.pallas_call(
        body, name="adamw_small", in_specs=[vm] * (2 + ns + 3 * n), out_specs=[vm] * (1 + len(rep) + 3 * n),
        out_shape=[S_((1, 1), F32)] + [shapes[k] for k in rep] + shapes * 3,
    )(rd, rf, *sharded_g, *ws, *ms, *vs)
    g_rep = dict(zip(rep, outs[1:1 + len(rep)]))
    o = outs[1 + len(rep):]
    return outs[0], g_rep, o[:n], o[n:2 * n], o[2 * n:]


def _pack(pieces, rows):
    flat = jnp.concatenate([p.reshape(-1).astype(F32) for p in pieces])
    return jnp.pad(flat, (0, rows * 128 - flat.shape[0])).reshape(rows, 128)


def _unpack(packed, shapes):
    flat = packed.reshape(-1)
    out, o = [], 0
    for s in shapes:
        n = 1
        for d in s:
            n *= d
        out.append(flat[o:o + n].reshape(s))
        o += n
    return out


def _rows_for(pieces_or_shapes):
    n = 0
    for p in pieces_or_shapes:
        k = 1
        for d in (p if isinstance(p, tuple) else p.shape):
            k *= d
        n += k
    return -(-n // 1024) * 8


def kernel(x, norm_mix, norm_ffn, conv_w_pw1, conv_b_pw1, conv_w_dw, conv_b_dw, conv_ln_g, conv_ln_b, conv_w_pw2, conv_b_pw2, pool_w, pool_b, pool_scale, ffn_w_up, ffn_w_dw, ffn_b_dw, ffn_w_down, final_norm, loss_target, m_norm_mix, m_norm_ffn, m_conv_w_pw1, m_conv_b_pw1, m_conv_w_dw, m_conv_b_dw, m_conv_ln_g, m_conv_ln_b, m_conv_w_pw2, m_conv_b_pw2, m_pool_w, m_pool_b, m_pool_scale, m_ffn_w_up, m_ffn_w_dw, m_ffn_b_dw, m_ffn_w_down, m_final_norm, v_norm_mix, v_norm_ffn, v_conv_w_pw1, v_conv_b_pw1, v_conv_w_dw, v_conv_b_dw, v_conv_ln_g, v_conv_ln_b, v_conv_w_pw2, v_conv_b_pw2, v_pool_w, v_pool_b, v_pool_scale, v_ffn_w_up, v_ffn_w_dw, v_ffn_b_dw, v_ffn_w_down, v_final_norm):
    nb, seq, D = x.shape
    T = nb * seq
    F = ffn_w_down.shape[1] * N_CHIPS
    taps = conv_w_dw.shape[1]
    ffn_taps = ffn_w_dw.shape[1]
    ng = pool_w.shape[1]
    cg = pool_w.shape[3]
    dsh = D // N_CHIPS
    fsh = F // N_CHIPS
    assert taps - 1 <= HALO_CONV and ffn_taps - 1 <= HALO_FFN and max(POOL_WINDOWS) <= HALO_POOL
    tm = min(TILE_D, seq)
    tf = min(TILE_F, seq)
    assert seq % tm == 0 and seq % tf == 0 and tm % HALO_CONV == 0 and tf % CHUNK == 0
    tps, tpf = seq // tm, seq // tf
    tw = min(TILE_MM, T)
    assert T % tw == 0

    xi, yi, ci = lax.axis_index("x"), lax.axis_index("y"), lax.axis_index("c")
    qi = 2 * xi + yi
    ids = jnp.stack([qi, ci]).astype(jnp.int32)
    x2d = x.reshape(T, D)
    tg2d = loss_target.reshape(T, D)

    small_sharded = [conv_w_dw[0], ffn_w_dw, pool_b, pool_scale]
    srows = 2 * _rows_for(small_sharded)
    w1g, s_pw2, s_small, s_pool, s_up0, s_up1, s_down0, s_down1 = _own_slots([
        (conv_w_pw1, 0, BF16, False), (conv_w_pw2, 0, BF16, False), (_pack(small_sharded, srows)[None], 0, F32, False),
        (pool_w.reshape(1, ng * (cg // N_CHIPS), cg), 0, BF16, False), (ffn_w_up, 0, BF16, True), (ffn_w_up, 1, BF16, True),
        (ffn_w_down, 0, BF16, False), (ffn_w_down, 1, BF16, False)])
    s_up, s_down = [s_up0, s_up1], [s_down0, s_down1]
    hb0t, a, w2g, smallg, pwg = _conv_in(x2d, norm_mix[0:1], w1g, conv_b_pw1, tm, comm=_GatherComm([s_pw2, s_small, s_pool]))
    w2f = w2g.reshape(D, D)
    sm_parts = [_unpack(smallg[s], [(taps, dsh), (2, ffn_taps, fsh), (1, dsh), (1, dsh)]) for s in range(N_CHIPS)]
    wdw_f = jnp.concatenate([p[0] for p in sm_parts], axis=1)
    wdw_f = jnp.pad(wdw_f, ((0, HALO_CONV - taps), (0, 0)))
    fdw_f = jnp.concatenate([p[1] for p in sm_parts], axis=2)
    fdw_f = jnp.pad(fdw_f, ((0, 0), (0, 8 - ffn_taps), (0, 0)))
    pb_f = jnp.concatenate([p[2] for p in sm_parts], axis=1)
    ps_f = jnp.concatenate([p[3] for p in sm_parts], axis=1)

    c, sbt, x1, hb1, hb1t, wup0, wd0 = _conv_mid(a, x2d, wdw_f, conv_b_dw, conv_ln_g, conv_ln_b, w2f, conv_b_pw2,
                                                 norm_ffn[0:1], tm, tps, taps, comm=_GatherComm([s_up[0], s_down[0]]))
    up0, gb0t, x2, h2, wup1, wd1 = _ffn_fwd(hb1, x1, wup0, fdw_f[0], ffn_b_dw[0:1], wd0.reshape(F, D), norm_mix[1:2], tf, tpf,
                                            ffn_taps, "ffn_fwd0", comm=_GatherComm([s_up[1], s_down[1]]))
    pwf = pwg.reshape(N_CHIPS, ng, cg // N_CHIPS, cg).transpose(1, 0, 2, 3).reshape(ng, cg, cg)
    plb, x3, hb3, hb3t = _pool_fwd(h2, x2, pwf, pb_f, ps_f, norm_ffn[1:2], tm, tps)
    wup = [wup0, wup1]
    wdn = [wd0.reshape(F, D), wd1.reshape(F, D)]
    up1, gb1t, dx4, loss_part, d_final = _ffn_fwd(hb3, x3, wup[1], fdw_f[1], ffn_b_dw[1:2], wdn[1], final_norm.reshape(1, D),
                                                  tf, tpf, ffn_taps, "ffn_fwd1", target=tg2d)

    fs = {}

    def pair(group):
        return list(group), list(group.values()), _PairComm(list(group.values()))

    def chip_sums(nms, parts, from_sib):
        for nm, p, s in zip(nms, parts, from_sib):
            fs[nm] = (p, s)
        return nms, _ExchangeComm([_chip_sum(p, s, ids, "chip_sum_" + nm) for nm, p, s in zip(nms, parts, from_sib)])

    def arrived(nms, got):
        for nm, g in zip(nms, got):
            fs[nm] = fs[nm] + (g,)

    p_down1 = _mm_tn(gb1t, dx4, F // 2, D, tw, False, "dw_down1")
    nms, parts, px = pair({"down1": p_down1.reshape(N_CHIPS, 2, fsh // 2, D)})
    dup1, dx3, dnf1, fw1, *sib = _ffn_bwd(dx4, up1, wdn[1], wup[1], fdw_f[1], ffn_b_dw[1:2], x3, norm_ffn[1:2], tf, tpf,
                                          ffn_taps, "ffn_bwd1", comm=px)
    nms, xchg = chip_sums(nms, parts, sib)
    p_up1, *got = _mm_tn(hb3t, dup1, D // 2, 4 * fsh, tw, True, "dw_up1", split=2, comm=xchg)
    arrived(nms, got)
    nms, parts, px = pair({"up1": p_up1.reshape(N_CHIPS, 2, D // 2, 2 * fsh)})
    dx2, dpw, pool_sm, *sib = _pool_bwd(dx3, plb, x2, pwf, pb_f, ps_f, norm_mix[1:2], tm, tps, comm=px)
    nms_u, xchg = chip_sums(nms, parts, sib)
    p_down0 = _mm_tn(gb0t, dx2, F // 2, D, tw, False, "dw_down0")
    nms, parts, px = pair({"down0": p_down0.reshape(N_CHIPS, 2, fsh // 2, D)})
    dup0, dx1, dnf0, fw0, *both = _ffn_bwd(dx2, up0, wdn[0], wup[0], fdw_f[0], ffn_b_dw[0:1], x1, norm_ffn[0:1], tf, tpf,
                                           ffn_taps, "ffn_bwd0", comm=_Both(xchg, px))
    arrived(nms_u, both[:len(nms_u)])
    nms, xchg = chip_sums(nms, parts, both[len(nms_u):])
    p_up0, *got = _mm_tn(hb1t, dup0, D // 2, 4 * fsh, tw, True, "dw_up0", split=2, comm=xchg)
    arrived(nms, got)
    csh = cg // N_CHIPS
    p_pool = dpw.reshape(2, ng // 2, N_CHIPS, csh, cg).transpose(2, 0, 1, 3, 4).reshape(N_CHIPS, 2, (ng // 2) * csh, cg)
    nms, parts, px = pair({"up0": p_up0.reshape(N_CHIPS, 2, D // 2, 2 * fsh), "pool": p_pool.astype(BF16)})
    dc, conv_sm, *sib = _conv_bwd1(dx1, c, w2f, conv_ln_g, conv_ln_b, tm, comm=px)
    nms, xchg = chip_sums(nms, parts, sib)
    p_pw2 = _mm_tn(sbt, dx1, D, D, tw, False, "dw_pw2")
    nms2, parts2, px = pair({"pw2": p_pw2.reshape(N_CHIPS, 2, dsh // 2, D)})
    grad_x, dab, dwdw, db1, dnm0, *both = _conv_bwd2(dc, a, x2d, w1g, wdw_f, norm_mix[0:1], dx1, tm, tps, taps,
                                                     comm=_Both(xchg, px))
    arrived(nms, both[:len(nms)])
    nms, xchg = chip_sums(nms2, parts2, both[len(nms):])
    p_pw1, *got = _mm_tn(hb0t, dab, D, 2 * dsh, tw, True, "dw_pw1", comm=xchg)
    arrived(nms, got)
    nms, parts, _ = pair({"pw1": p_pw1.reshape(N_CHIPS, 2, D // 2, 2 * dsh)})
    nms, xchg = chip_sums(nms, parts, _pair_exchange(parts, "grad_pair_exchange_pw1"))
    (rd, rf), got = _small_allreduce([
        [(dnm0, 0, 1), (pool_sm, 2, 3), (dnf0, 0, 1), (dnf1, 0, 1), (db1, 0, 2), (conv_sm, 0, 4), (d_final, 0, 1),
         (pool_sm, 0, 2), (loss_part, 0, 1), 2, (dwdw, 0, dwdw.shape[0])],
        [(fw0, 0, 4), (fw1, 0, 4)]], xchg)
    arrived(nms, got)
    bufs = [_final_sum(*fs[nm], ids, "final_sum_" + nm) for nm in ("pw1", "pw2", "pool")]
    for nm in ("up", "down"):
        b = _final_sum(*fs[nm + "0"], ids, "final_sum_" + nm + "0", layer=0, n_layers=2)
        bufs.append(_final_sum(*fs[nm + "1"], ids, "final_sum_" + nm + "1", layer=1, n_layers=2, buf=b))
    shared = _comm_only(_ShareComm(bufs), "grad_share_halves")
    g_pw1, g_pw2, g_pool, g_up, g_down = [b.reshape(b.shape[0], 2 * b.shape[2], b.shape[3]) for b in shared]
    g_wdw = lax.dynamic_slice_in_dim(rd[16:16 + taps], qi * dsh, dsh, axis=1)
    g_ps = lax.dynamic_slice_in_dim(rd[11:12], qi * dsh, dsh, axis=1)
    g_pb = lax.dynamic_slice_in_dim(rd[12:13], qi * dsh, dsh, axis=1)
    g_fdw = lax.dynamic_slice_in_dim(rf.reshape(2, 4, F)[:, :ffn_taps], qi * fsh, fsh, axis=2).reshape(2 * ffn_taps, fsh)

    def big(w, g, m, v, nm):
        return tuple(t.reshape(w.shape) for t in _adamw(w.reshape(g.shape), g, m.reshape(g.shape), v.reshape(g.shape),
                                                        "adamw_" + nm))

    o_pw1 = big(conv_w_pw1, g_pw1, m_conv_w_pw1, v_conv_w_pw1, "pw1")
    o_pw2 = big(conv_w_pw2, g_pw2, m_conv_w_pw2, v_conv_w_pw2, "pw2")
    o_pool = big(pool_w, g_pool, m_pool_w, v_pool_w, "pool")
    o_up = big(ffn_w_up, g_up, m_ffn_w_up, v_ffn_w_up, "up")
    o_down = big(ffn_w_down, g_down, m_ffn_w_down, v_ffn_w_down, "down")

    snames = ["norm_mix", "norm_ffn", "b_pw1", "w_dw", "b_dw", "ln_g", "ln_b", "b_pw2", "pool_b", "pool_scale", "ffn_w_dw",
              "ffn_b_dw", "final_norm"]
    sw = [norm_mix, norm_ffn, conv_b_pw1, conv_w_dw, conv_b_dw, conv_ln_g, conv_ln_b, conv_b_pw2, pool_b, pool_scale,
          ffn_w_dw, ffn_b_dw, final_norm]
    smm = [m_norm_mix, m_norm_ffn, m_conv_b_pw1, m_conv_w_dw, m_conv_b_dw, m_conv_ln_g, m_conv_ln_b, m_conv_b_pw2, m_pool_b,
           m_pool_scale, m_ffn_w_dw, m_ffn_b_dw, m_final_norm]
    svv = [v_norm_mix, v_norm_ffn, v_conv_b_pw1, v_conv_w_dw, v_conv_b_dw, v_conv_ln_g, v_conv_ln_b, v_conv_b_pw2, v_pool_b,
           v_pool_scale, v_ffn_w_dw, v_ffn_b_dw, v_final_norm]
    sshapes = [tuple(w.shape) for w in sw]
    two_d = [(2, D), (2, D), (1, 2 * D), (taps, dsh), (1, D), (1, D), (1, D), (1, D), (1, dsh), (1, dsh), (2 * ffn_taps, fsh),
             (2, F), (1, D)]
    rows_d = [[0, 1], [2, 3], [4, 5], None, [9], [6], [7], [8], None, None, None, None, [10]]
    rows_f = [None] * 11 + [[3, 7], None]
    as2d = lambda ts: [t.reshape(s) for t, s in zip(ts, two_d)]
    loss, g_rep, sd, sm_new, sv_new = _adamw_small(rd, rf, [g_wdw, g_pb, g_ps, g_fdw], as2d(sw), as2d(smm), as2d(svv), rows_d,
                                                   rows_f, loss_row=13)
    sg = dict(g_rep)
    sg.update({3: g_wdw, 8: g_pb, 9: g_ps, 10: g_fdw})
    small_out = {n: tuple(t.reshape(sshapes[k]) for t in (sg[k], sd[k], sm_new[k], sv_new[k])) for k, n in enumerate(snames)}

    order = [small_out["norm_mix"], small_out["norm_ffn"], o_pw1, small_out["b_pw1"], small_out["w_dw"], small_out["b_dw"],
             small_out["ln_g"], small_out["ln_b"], o_pw2, small_out["b_pw2"], o_pool, small_out["pool_b"],
             small_out["pool_scale"], o_up, small_out["ffn_w_dw"], small_out["ffn_b_dw"], o_down, small_out["final_norm"]]
    return (loss[0, 0], grad_x.reshape(nb, seq, D), *[o[0] for o in order], *[o[1] for o in order], *[o[2] for o in order],
            *[o[3] for o in order])
```

```python
import functools

import jax
import jax.numpy as jnp
from jax import lax
from jax.experimental import pallas as pl
from jax.experimental.pallas import tpu as pltpu

F32, BF16 = jnp.float32, jnp.bfloat16
S_ = jax.ShapeDtypeStruct
MESH = pl.DeviceIdType.MESH

RMS_EPS, LN_EPS = 1e-6, 1e-5
POOL_WINDOWS = (2, 4, 8, 16)
ADAM_LR, ADAM_B1, ADAM_B2, ADAM_EPS, ADAM_WD, ADAM_STEP = 0.001, 0.9, 0.999, 1e-08, 0.01, 10

VMEM_LIMIT_BYTES = 60 * 1024 * 1024
SUBLANES, LANES = 8, 128
BF16_ROWS = 16
BLOCK_BYTES = 1 << 20
N_CHIPS = 4
CHUNK = 32
HALO_CONV = 32
HALO_POOL = 32
HALO_FFN = 8
TILE_D = 512
TILE_F = 256
TILE_MM = 2048


def _pcall(body, *, name, grid, in_specs, out_specs, out_shape, args, scratch=(), aliases=None, prefetch=0, comm=None):
    params = pltpu.CompilerParams(dimension_semantics=("arbitrary",) * len(grid), vmem_limit_bytes=VMEM_LIMIT_BYTES)
    if comm is not None:
        assert not prefetch
        single = not isinstance(out_shape, (list, tuple))
        in_specs, args, scratch = list(in_specs), list(args), list(scratch)
        out_specs, out_shape = ([out_specs], [out_shape]) if single else (list(out_specs), list(out_shape))
        n_in, n_out, n_scr, n_cin, n_cout = len(in_specs), len(out_specs), len(scratch), len(comm.inputs), len(comm.out_shape)
        any_spec = pl.BlockSpec(memory_space=pl.ANY)
        aliases = dict(aliases or {})
        aliases.update({n_in + a: n_out + b for a, b in comm.aliases.items()})
        inner = body

        def body(*refs):
            ins, cin = refs[:n_in], refs[n_in:n_in + n_cin]
            outs = refs[n_in + n_cin:n_in + n_cin + n_out]
            cout = refs[n_in + n_cin + n_out:n_in + n_cin + n_out + n_cout]
            scr = refs[n_in + n_cin + n_out + n_cout:n_in + n_cin + n_out + n_cout + n_scr]
            sems = refs[n_in + n_cin + n_out + n_cout + n_scr:]
            step, total = 0, 1
            for d, g in enumerate(grid):
                step = step * g + pl.program_id(d)
                total *= g
            comm.before(cin, cout, sems, step, total)
            inner(*ins, *outs, *scr)
            comm.after(cin, cout, sems, step, total)

        in_specs += [any_spec] * n_cin
        args += list(comm.inputs)
        out_specs += [any_spec] * n_cout
        out_shape += list(comm.out_shape)
        scratch += list(comm.sems)
    if prefetch:
        spec = pltpu.PrefetchScalarGridSpec(num_scalar_prefetch=prefetch, grid=grid, in_specs=in_specs,
                                            out_specs=out_specs, scratch_shapes=list(scratch))
        return pl.pallas_call(body, name=name, grid_spec=spec, out_shape=out_shape,
                              input_output_aliases=aliases or {}, compiler_params=params)(*args)
    return pl.pallas_call(body, name=name, grid=grid, in_specs=in_specs, out_specs=out_specs, out_shape=out_shape,
                          scratch_shapes=list(scratch), input_output_aliases=aliases or {},
                          compiler_params=params)(*args)


def _full(shape):
    n = len(shape)
    return pl.BlockSpec(tuple(shape), lambda *_: (0,) * n, pipeline_mode=pl.Buffered(1))


def _rows(tm, c, col=0):
    return pl.BlockSpec((tm, c), lambda i, *_: (i, col))


def _cols(r, tm):
    return pl.BlockSpec((r, tm), lambda i, *_: (0, i))


def _prev_halo(tm, hb, c):
    return pl.BlockSpec((hb, c), lambda i, *_: (jnp.maximum(i * (tm // hb) - 1, 0), 0))


def _next_halo(tm, hb, c, total_rows):
    last = total_rows // hb - 1
    return pl.BlockSpec((hb, c), lambda i, *_: (jnp.minimum((i + 1) * (tm // hb), last), 0))


def _acc_spec(r, c):
    return pl.BlockSpec((r, c), lambda *_: (0, 0))


def _rms(x, g):
    r = lax.rsqrt(jnp.mean(x * x, axis=-1, keepdims=True) + RMS_EPS)
    xh = x * r
    return xh * g, xh, r


def _rms_bwd(dy, xh, r, g):
    dxh = dy * g
    dx = r * (dxh - xh * jnp.mean(dxh * xh, axis=-1, keepdims=True))
    return dx, jnp.sum(dy * xh, axis=0, keepdims=True)


def _colsum(v):
    return jnp.sum(v, axis=0, keepdims=True)


def _fold8(v):
    out = v[0:8]
    for j in range(1, v.shape[0] // 8):
        out = out + v[8 * j:8 * j + 8]
    return out


def _chunks(n_rows, fn):
    def step(j, carry):
        fn(pl.multiple_of(j * CHUNK, CHUNK))
        return carry
    lax.fori_loop(0, n_rows // CHUNK, step, 0)


def _dot(a, b):
    return jnp.dot(a, b, preferred_element_type=F32)


def _dot_nt(a, b):
    return lax.dot_general(a, b, (((1,), (1,)), ((), ())), preferred_element_type=F32)


def _dot_tn(a, b):
    return lax.dot_general(a, b, (((0,), (0,)), ((), ())), preferred_element_type=F32)


def _sig(v):
    return jax.nn.sigmoid(v)


def _accumulate(ref, val, first):
    @pl.when(first)
    def _():
        ref[...] = val

    @pl.when(jnp.logical_not(first))
    def _():
        ref[...] = ref[...] + val


def _conv_in(x, g, w1g, b1, tm, comm=None):
    T, D = x.shape
    ns, _, nc = w1g.shape

    def body(x_ref, g_ref, w_ref, b_ref, hbt_ref, a_ref):
        hb = _rms(x_ref[...], g_ref[...])[0].astype(BF16)
        hbt_ref[...] = hb.T
        for s in range(ns):
            a_ref[:, s * nc:(s + 1) * nc] = _dot(hb, w_ref[s]) + b_ref[:, s * nc:(s + 1) * nc]

    return _pcall(body, name="conv_in", grid=(T // tm,),
                  in_specs=[_rows(tm, D), _full((1, D)), _full(w1g.shape), _full((1, ns * nc))],
                  out_specs=[_cols(D, tm), _rows(tm, ns * nc)],
                  out_shape=[S_((D, T), BF16), S_((T, ns * nc), F32)], args=(x, g, w1g, b1), comm=comm)


def _conv_mid(a, x, wdw, bdw, lng, lnb, w2, b2, gn, tm, tps, taps, comm=None):
    T, D = x.shape
    hc = HALO_CONV

    def body(a_ref, ah_ref, x_ref, wdw_ref, bdw_ref, lng_ref, lnb_ref, w2_ref, b2_ref, gn_ref,
             c_ref, sbt_ref, x1_ref, hb_ref, hbt_ref, uext, shifted):
        i = pl.program_id(0)
        ah = ah_ref[...]
        uext[0:hc, :] = jnp.where(i % tps == 0, 0.0, ah[:, :D] * _sig(ah[:, D:]))
        av = a_ref[...]
        uext[hc:hc + tm, :] = av[:, :D] * _sig(av[:, D:])

        def conv(r0, src, ks, first):
            acc = jnp.broadcast_to(bdw_ref[...], (CHUNK, D)) if first else c_ref[pl.ds(r0, CHUNK), :]
            for k, o in ks:
                acc = acc + wdw_ref[k:k + 1, :] * src[pl.ds(pl.multiple_of(r0 + o, 8), CHUNK), :]
            c_ref[pl.ds(r0, CHUNK), :] = acc
        _by_sublane_shift(uext, shifted, tm, [hc - (taps - 1) + k for k in range(taps)], conv)

        cv = c_ref[...]
        xc = cv - jnp.mean(cv, axis=-1, keepdims=True)
        nh = xc * lax.rsqrt(jnp.mean(xc * xc, axis=-1, keepdims=True) + LN_EPS)
        n = nh * lng_ref[...] + lnb_ref[...]
        sb = (n * _sig(n)).astype(BF16)
        sbt_ref[...] = sb.T
        x1 = x_ref[...] + _dot(sb, w2_ref[...]) + b2_ref[...]
        x1_ref[...] = x1
        hb = _rms(x1, gn_ref[...])[0].astype(BF16)
        hb_ref[...] = hb
        hbt_ref[...] = hb.T

    return _pcall(body, name="conv_mid", grid=(T // tm,),
                  in_specs=[_rows(tm, 2 * D), _prev_halo(tm, hc, 2 * D), _rows(tm, D), _full(wdw.shape), _full((1, D)),
                            _full((1, D)), _full((1, D)), _full((D, D)), _full((1, D)), _full((1, D))],
                  out_specs=[_rows(tm, D), _cols(D, tm), _rows(tm, D), _rows(tm, D), _cols(D, tm)],
                  out_shape=[S_((T, D), F32), S_((D, T), BF16), S_((T, D), F32), S_((T, D), BF16), S_((D, T), BF16)],
                  scratch=[pltpu.VMEM((tm + hc, D), F32), pltpu.VMEM((tm + hc - 8, D), F32)],
                  args=(a, a, x, wdw, bdw, lng, lnb, w2, b2, gn), comm=comm)


def _ffn_taps(aext, realigned, tm, ffn_taps):
    for k in range(ffn_taps - 1):
        realigned[k] = aext[pl.ds(HALO_FFN - (ffn_taps - 1) + k, tm), :]
    return [realigned.at[k] for k in range(ffn_taps - 1)] + [aext.at[pl.ds(HALO_FFN, tm)]]


def _ffn_conv(taps, wdw_ref, bdw_ref):
    acc = bdw_ref[...]
    for k, tap in enumerate(taps):
        acc = acc + wdw_ref[k:k + 1, :] * tap[...]
    return acc


def _by_sublane_shift(ext, shifted, tm, offsets, per_chunk):
    first = True
    for b in range(8):
        ks = [(k, o - b) for k, o in enumerate(offsets) if o % 8 == b]
        if not ks:
            continue
        if b:
            shifted[...] = ext[pl.ds(b, shifted.shape[0]), :]
        _chunks(tm, functools.partial(per_chunk, src=shifted if b else ext, ks=ks, first=first))
        first = False


def _ffn_fwd(hb, xin, wg, wdw, bdw, wd, gn, tm, tps, ffn_taps, name, target=None, comm=None):
    T, D = xin.shape
    ns, _, nc = wg.shape
    F = ns * nc // 2
    hf = HALO_FFN
    hh = 2 * HALO_FFN
    final = target is not None

    def body(*refs):
        if final:
            (h_ref, hh_ref, x_ref, wg_ref, wdw_ref, bdw_ref, wd_ref, gn_ref, tg_ref,
             up_ref, gb_ref, dx_ref, loss_ref, dgn_ref, aext, realigned) = refs
        else:
            (h_ref, hh_ref, x_ref, wg_ref, wdw_ref, bdw_ref, wd_ref, gn_ref,
             up_ref, gb_ref, xo_ref, ho_ref, aext, realigned) = refs
        i = pl.program_id(0)
        hv = h_ref[...]
        for s in range(ns):
            up_ref[:, s * nc:(s + 1) * nc] = _dot(hv, wg_ref[s])
        halo = jnp.concatenate([_dot(hh_ref[...], wg_ref[s]) for s in range(ns // 2)], axis=1)[hh - hf:hh, :]
        aext[0:hf, :] = jnp.where(i % tps == 0, 0.0, halo)
        aext[hf:hf + tm, :] = up_ref[:, 0:F]
        acv = _ffn_conv(_ffn_taps(aext, realigned, tm, ffn_taps), wdw_ref, bdw_ref)
        gb = (acv * _sig(acv) * up_ref[:, F:2 * F]).astype(BF16)
        gb_ref[...] = gb.T
        xo = x_ref[...] + _dot(gb, wd_ref[...])
        if not final:
            xo_ref[...] = xo
            ho_ref[...] = _rms(xo, gn_ref[...])[0]
        else:
            out, xh, r = _rms(xo, gn_ref[...])
            diff = out - tg_ref[...]
            part = 0.5 / D * _colsum(diff * diff)
            dx, dg = _rms_bwd(diff * (1.0 / D), xh, r, gn_ref[...])
            dx_ref[...] = dx
            _accumulate(loss_ref, part, i == 0)
            _accumulate(dgn_ref, dg, i == 0)

    in_specs = [_rows(tm, D), _prev_halo(tm, hh, D), _rows(tm, D), _full(wg.shape), _full(wdw.shape), _full((1, F)),
                _full((F, D)), _full((1, D))]
    args = [hb, hb, xin, wg, wdw, bdw, wd, gn]
    if final:
        in_specs.append(_rows(tm, D))
        args.append(target)
        out_specs = [_rows(tm, 2 * F), _cols(F, tm), _rows(tm, D), _acc_spec(1, D), _acc_spec(1, D)]
        out_shape = [S_((T, 2 * F), F32), S_((F, T), BF16), S_((T, D), F32), S_((1, D), F32), S_((1, D), F32)]
    else:
        out_specs = [_rows(tm, 2 * F), _cols(F, tm), _rows(tm, D), _rows(tm, D)]
        out_shape = [S_((T, 2 * F), F32), S_((F, T), BF16), S_((T, D), F32), S_((T, D), F32)]
    return _pcall(body, name=name, grid=(T // tm,), in_specs=in_specs, out_specs=out_specs, out_shape=out_shape,
                  scratch=[pltpu.VMEM((tm + hf, F), F32), pltpu.VMEM((ffn_taps - 1, tm, F), F32)], args=args, comm=comm)


def _pool_counts(i, tps, tm, w):
    pos = (i % tps) * tm + lax.broadcasted_iota(jnp.int32, (tm, 1), 0)
    return jnp.minimum(pos + 1, w).astype(F32)


def _window_sums(src, buf_a, buf_b, cg, causal):
    assert POOL_WINDOWS == (2, 4, 8, 16)
    n = src.shape[0]
    levels = len(POOL_WINDOWS)
    cur, outs = src, []
    for l in range(levels):
        dst = buf_b if l % 2 else buf_a
        cols = slice(l * cg, levels * cg)
        lo, m = 8 * (l + 1), n - 8 * (l + 1)
        if causal:
            dst[lo:n, cols] = cur[lo:n, cols] + cur[pl.ds(lo - (1 << l), m), cols]
        else:
            dst[0:m, cols] = cur[0:m, cols] + cur[pl.ds(1 << l, m), cols]
        outs.append(dst)
        cur = dst
    return outs


def _pool_fwd(h, x2, pw, pb, ps, gn, tm, tps, comm=None):
    T, D = h.shape
    ng, cg, _ = pw.shape
    hp = HALO_POOL

    def body(h_ref, hh_ref, x_ref, pw_ref, pb_ref, ps_ref, gn_ref, pl_ref, xo_ref, hb_ref, hbt_ref, hext, buf_a, buf_b):
        i = pl.program_id(0)
        hext[0:hp, :] = jnp.where(i % tps == 0, 0.0, hh_ref[...])
        hext[hp:hp + tm, :] = h_ref[...]
        sums = _window_sums(hext, buf_a, buf_b, cg, True)
        for g, w in enumerate(POOL_WINDOWS):
            cs = slice(g * cg, (g + 1) * cg)
            pl_ref[:, cs] = (sums[g][hp:hp + tm, cs] / _pool_counts(i, tps, tm, w) - h_ref[:, cs]).astype(BF16)

        pv = pl_ref[...]
        mixed = jnp.concatenate([_dot(pv[:, g * cg:(g + 1) * cg], pw_ref[g]) for g in range(ng)], axis=1)
        xo = x_ref[...] + ps_ref[...] * (mixed + pb_ref[...])
        xo_ref[...] = xo
        hb = _rms(xo, gn_ref[...])[0].astype(BF16)
        hb_ref[...] = hb
        hbt_ref[...] = hb.T

    return _pcall(body, name="pool_fwd", grid=(T // tm,),
                  in_specs=[_rows(tm, D), _prev_halo(tm, hp, D), _rows(tm, D), _full(pw.shape), _full((1, D)), _full((1, D)),
                            _full((1, D))],
                  out_specs=[_rows(tm, D), _rows(tm, D), _rows(tm, D), _cols(D, tm)],
                  out_shape=[S_((T, D), BF16), S_((T, D), F32), S_((T, D), BF16), S_((D, T), BF16)],
                  scratch=[pltpu.VMEM((tm + hp, D), F32)] * 3, args=(h, h, x2, pw, pb, ps, gn), comm=comm)


def _bwd_down(dx, up, wdw, bdw, wd, tm, tps, ffn_taps, name, comm=None):
    T, D = dx.shape
    F = up.shape[1] // 2
    hf = HALO_FFN
    nt = T // tm

    def body(dx_ref, up_ref, uph_ref, wdw_ref, bdw_ref, wd_ref, dac_ref, dgate_ref, wsum_ref, aext, realigned):
        i = pl.program_id(0)
        aext[0:hf, :] = jnp.where(i % tps == 0, 0.0, uph_ref[...])
        aext[hf:hf + tm, :] = up_ref[:, 0:F]
        taps = _ffn_taps(aext, realigned, tm, ffn_taps)
        acv = _ffn_conv(taps, wdw_ref, bdw_ref)
        dg = _dot_nt(dx_ref[...].astype(BF16), wd_ref[...])
        sg = _sig(acv)
        dgate_ref[...] = (dg * acv * sg).astype(BF16)
        dac = dg * up_ref[:, F:2 * F] * (sg * (1.0 + acv * (1.0 - sg)))
        dac_ref[...] = dac
        rows = [_colsum(dac * tap[...]) for tap in taps]
        rows += [_colsum(dac), jnp.zeros((8 - ffn_taps - 1, F), F32)]
        _accumulate(wsum_ref, jnp.concatenate(rows, axis=0), i == 0)

    return _pcall(body, name=name, grid=(nt,),
                  in_specs=[_rows(tm, D), _rows(tm, 2 * F), _prev_halo(tm, hf, F), _full(wdw.shape), _full((1, F)),
                            _full((F, D))],
                  out_specs=[_rows(tm, F), _rows(tm, F, col=1), _acc_spec(8, F)],
                  out_shape=[S_((T, F), F32), S_((T, 2 * F), BF16), S_((8, F), F32)],
                  scratch=[pltpu.VMEM((tm + hf, F), F32), pltpu.VMEM((ffn_taps - 1, tm, F), F32)],
                  args=(dx, up, up, wdw, bdw, wd), comm=comm)


def _bwd_up(dac, dup, wg, wdw, xin, gn, dx, tm, tps, ffn_taps, name, comm=None):
    T, D = xin.shape
    F = dac.shape[1]
    ns, _, nc = wg.shape
    hf = HALO_FFN
    nt = T // tm

    def body(dac_ref, dach_ref, dgate_ref, wg_ref, wdw_ref, x_ref, gn_ref, dx_ref, dact_ref, dxo_ref, dgn_ref, dext):
        i = pl.program_id(0)
        dext[0:tm, :] = dac_ref[...]
        dext[tm:tm + hf, :] = jnp.where(i % tps == tps - 1, 0.0, dach_ref[...])
        dact = wdw_ref[ffn_taps - 1:ffn_taps, :] * dac_ref[...]
        for k in range(ffn_taps - 1):
            dact = dact + wdw_ref[k:k + 1, :] * dext[pl.ds(ffn_taps - 1 - k, tm), :]
        dact_ref[...] = dact.astype(BF16)

        dh = jnp.zeros((tm, D), F32)
        for s in range(ns):
            src = dact_ref if s < ns // 2 else dgate_ref
            o = (s % (ns // 2)) * nc
            dh = dh + _dot_nt(src[:, o:o + nc], wg_ref[s])
        _, xh, r = _rms(x_ref[...], gn_ref[...])
        dxn, dg = _rms_bwd(dh, xh, r, gn_ref[...])
        dxo_ref[...] = dx_ref[...] + dxn
        _accumulate(dgn_ref, dg, i == 0)

    return _pcall(body, name=name, grid=(nt,),
                  in_specs=[_rows(tm, F), _next_halo(tm, hf, F, T), _rows(tm, F, col=1), _full(wg.shape), _full(wdw.shape),
                            _rows(tm, D), _full((1, D)), _rows(tm, D)],
                  out_specs=[_rows(tm, F), _rows(tm, D), _acc_spec(1, D)],
                  out_shape=[S_((T, 2 * F), BF16), S_((T, D), F32), S_((1, D), F32)],
                  scratch=[pltpu.VMEM((tm + hf, F), F32)],
                  aliases={2: 0}, args=(dac, dac, dup, wg, wdw, xin, gn, dx), comm=comm)


def _mm_tn(xt, dy, bk, bn, tt, shard_major, name, comm=None):
    K, T = xt.shape
    N = dy.shape[1]
    nt = T // tt

    def body(x_ref, dy_ref, o_ref, *acc):
        part = _dot(x_ref[...], dy_ref[...].astype(BF16))
        if nt == 1:
            o_ref[...] = part.astype(o_ref.dtype)
            return
        t = pl.program_id(2)
        _accumulate(acc[0], part, t == 0)

        @pl.when(t == nt - 1)
        def _():
            o_ref[...] = acc[0][...].astype(o_ref.dtype)

    if shard_major:
        out_spec = pl.BlockSpec((None, bk, bn), lambda i, j, t: (j, i, 0))
        out_shape = S_((N // bn, K, bn), BF16)
    else:
        out_spec = pl.BlockSpec((bk, bn), lambda i, j, t: (i, j))
        out_shape = S_((K, N), BF16)
    return _pcall(body, name=name, grid=(K // bk, N // bn, nt),
                  in_specs=[pl.BlockSpec((bk, tt), lambda i, j, t: (i, t)), pl.BlockSpec((tt, bn), lambda i, j, t: (t, j))],
                  out_specs=out_spec, out_shape=out_shape, scratch=[pltpu.VMEM((bk, bn), F32)] if nt > 1 else [],
                  args=(xt, dy), comm=comm)


def _pool_bwd(dx3, plb, x2, pw, pb, ps, gn, tm, tps, comm=None):
    T, D = x2.shape
    ng, cg, _ = pw.shape
    hp = HALO_POOL
    nt = T // tm

    def body(do_ref, doh_ref, pl_ref, x_ref, pw_ref, pb_ref, ps_ref, gn_ref, dxo_ref, dpw_ref, sm_ref, qext, buf_a, buf_b,
             dh_s):
        i = pl.program_id(0)
        do = do_ref[...]
        dm = do * ps_ref[...]
        dmh = jnp.where(i % tps == tps - 1, 0.0, doh_ref[...]) * ps_ref[...]
        pv = pl_ref[...]
        mixed = jnp.concatenate([_dot(pv[:, g * cg:(g + 1) * cg], pw_ref[g]) for g in range(ng)], axis=1)
        dscale = _colsum(do * (mixed + pb_ref[...]))
        dbias = _colsum(dm)
        dmb = dm.astype(BF16)
        dmhb = dmh.astype(BF16)
        for g, w in enumerate(POOL_WINDOWS):
            cs = slice(g * cg, (g + 1) * cg)
            _accumulate(dpw_ref.at[g], _dot_tn(pv[:, cs], dmb[:, cs]), i == 0)
            dpo = _dot_nt(dmb[:, cs], pw_ref[g])
            dh_s[:, cs] = dpo
            qext[0:tm, cs] = dpo / _pool_counts(i, tps, tm, w)
            qext[tm:tm + hp, cs] = _dot_nt(dmhb[:, cs], pw_ref[g]) * (1.0 / w)
        sums = _window_sums(qext, buf_a, buf_b, cg, False)
        for g in range(ng):
            cs = slice(g * cg, (g + 1) * cg)
            dh_s[:, cs] = sums[g][0:tm, cs] - dh_s[:, cs]

        _, xh, r = _rms(x_ref[...], gn_ref[...])
        dxn, dg = _rms_bwd(dh_s[...], xh, r, gn_ref[...])
        dxo_ref[...] = do + dxn
        sm = jnp.concatenate([dscale, dbias, dg, jnp.zeros((5, D), F32)], axis=0)
        _accumulate(sm_ref, sm, i == 0)

    return _pcall(body, name="pool_bwd", grid=(nt,),
                  in_specs=[_rows(tm, D), _next_halo(tm, hp, D, T), _rows(tm, D), _rows(tm, D), _full(pw.shape), _full((1, D)),
                            _full((1, D)), _full((1, D))],
                  out_specs=[_rows(tm, D), pl.BlockSpec((ng, cg, cg), lambda i: (0, 0, 0)), _acc_spec(8, D)],
                  out_shape=[S_((T, D), F32), S_((ng, cg, cg), F32), S_((8, D), F32)],
                  scratch=[pltpu.VMEM((tm + hp, D), F32)] * 3 + [pltpu.VMEM((tm, D), F32)],
                  args=(dx3, dx3, plb, x2, pw, pb, ps, gn), comm=comm)


def _conv_bwd1(dx1, c, w2, lng, lnb, tm, comm=None):
    T, D = dx1.shape

    def body(do_ref, c_ref, w2_ref, lng_ref, lnb_ref, dc_ref, sm_ref):
        i = pl.program_id(0)
        do = do_ref[...]
        ds = _dot_nt(do.astype(BF16), w2_ref[...])
        cv = c_ref[...]
        xc = cv - jnp.mean(cv, axis=-1, keepdims=True)
        rstd = lax.rsqrt(jnp.mean(xc * xc, axis=-1, keepdims=True) + LN_EPS)
        nh = xc * rstd
        n = nh * lng_ref[...] + lnb_ref[...]
        sg = _sig(n)
        dn = ds * (sg * (1.0 + n * (1.0 - sg)))
        dnh = dn * lng_ref[...]
        dc = rstd * (dnh - jnp.mean(dnh, axis=-1, keepdims=True) - nh * jnp.mean(dnh * nh, axis=-1, keepdims=True))
        dc_ref[...] = dc
        sm = jnp.concatenate([_colsum(dn * nh), _colsum(dn), _colsum(do), _colsum(dc), jnp.zeros((4, D), F32)], axis=0)
        _accumulate(sm_ref, sm, i == 0)

    return _pcall(body, name="conv_bwd1", grid=(T // tm,),
                  in_specs=[_rows(tm, D), _rows(tm, D), _full((D, D)), _full((1, D)), _full((1, D))],
                  out_specs=[_rows(tm, D), _acc_spec(8, D)],
                  out_shape=[S_((T, D), F32), S_((8, D), F32)], args=(dx1, c, w2, lng, lnb), comm=comm)


def _conv_bwd2(dc, a, x, w1g, wdw, gn, dx1, tm, tps, taps, comm=None):
    T, D = x.shape
    ns, _, nc = w1g.shape
    hc = HALO_CONV
    nt = T // tm
    tp = wdw.shape[0]

    def body(dc_ref, dch_ref, a_ref, ah_ref, x_ref, w_ref, wdw_ref, gn_ref, dx1_ref,
             gx_ref, dab_ref, dw_ref, db1_ref, dgn_ref, uext, dext, shifted, du_s, wacc):
        i = pl.program_id(0)
        ah = ah_ref[...]
        uext[0:hc, :] = jnp.where(i % tps == 0, 0.0, ah[:, :D] * _sig(ah[:, D:]))
        av = a_ref[...]
        sg2 = _sig(av[:, D:])
        uext[hc:hc + tm, :] = av[:, :D] * sg2
        dext[0:tm, :] = dc_ref[...]
        dext[tm:tm + hc, :] = jnp.where(i % tps == tps - 1, 0.0, dch_ref[...])

        @pl.when(i == 0)
        def _():
            wacc[...] = jnp.zeros_like(wacc)

        def wgrad(r0, src, ks, first):
            d = dc_ref[pl.ds(r0, CHUNK), :]
            for k, o in ks:
                wacc[8 * k:8 * k + 8, :] += _fold8(d * src[pl.ds(pl.multiple_of(r0 + o, 8), CHUNK), :])
        _by_sublane_shift(uext, shifted, tm, [hc - (taps - 1) + k for k in range(taps)], wgrad)

        def convt(r0, src, ks, first):
            acc = jnp.zeros((CHUNK, D), F32) if first else du_s[pl.ds(r0, CHUNK), :]
            for k, o in ks:
                acc = acc + wdw_ref[k:k + 1, :] * src[pl.ds(pl.multiple_of(r0 + o, 8), CHUNK), :]
            du_s[pl.ds(r0, CHUNK), :] = acc
        _by_sublane_shift(dext, shifted, tm, [taps - 1 - k for k in range(taps)], convt)

        du = du_s[...]
        da1 = du * sg2
        da2 = du * av[:, :D] * (sg2 * (1.0 - sg2))
        dab_ref[:, 0:D] = da1.astype(BF16)
        dab_ref[:, D:2 * D] = da2.astype(BF16)
        _accumulate(db1_ref, jnp.concatenate([_colsum(da1), _colsum(da2)], axis=0), i == 0)
        dh = jnp.zeros((tm, D), F32)
        for s in range(ns):
            dh = dh + _dot_nt(dab_ref[:, s * nc:(s + 1) * nc], w_ref[s])
        _, xh, r = _rms(x_ref[...], gn_ref[...])
        dxn, dg = _rms_bwd(dh, xh, r, gn_ref[...])
        gx_ref[...] = dx1_ref[...] + dxn
        _accumulate(dgn_ref, dg, i == 0)

        @pl.when(i == nt - 1)
        def _():
            dw_ref[...] = jnp.zeros_like(dw_ref)
            for k in range(taps):
                dw_ref[k:k + 1, :] = _colsum(wacc[8 * k:8 * k + 8, :])

    return _pcall(body, name="conv_bwd2", grid=(nt,),
                  in_specs=[_rows(tm, D), _next_halo(tm, hc, D, T), _rows(tm, 2 * D), _prev_halo(tm, hc, 2 * D), _rows(tm, D),
                            _full(w1g.shape), _full(wdw.shape), _full((1, D)), _rows(tm, D)],
                  out_specs=[_rows(tm, D), _rows(tm, 2 * D), _acc_spec(tp, D), _acc_spec(2, D), _acc_spec(1, D)],
                  out_shape=[S_((T, D), F32), S_((T, 2 * D), BF16), S_((tp, D), F32), S_((2, D), F32), S_((1, D), F32)],
                  scratch=[pltpu.VMEM((tm + hc, D), F32), pltpu.VMEM((tm + hc, D), F32), pltpu.VMEM((tm + hc - 8, D), F32),
                           pltpu.VMEM((tm, D), F32), pltpu.VMEM((8 * taps, D), F32)],
                  args=(dc, dc, a, a, x, w1g, wdw, gn, dx1), comm=comm)


def _place():
    x, y, c = lax.axis_index("x"), lax.axis_index("y"), lax.axis_index("c")
    chips = [(1 - x, y), (x, 1 - y), (1 - x, 1 - y)]
    return x, y, c, chips


class _GatherComm:
    def __init__(self, slots):
        n = len(slots)
        self.inputs = list(slots)
        self.out_shape = [S_(s.shape, s.dtype) for s in slots]
        self.aliases = {i: i for i in range(n)}
        self.sems = [pltpu.SemaphoreType.DMA((n, 3))] * 4
        self.halves = [s.shape[1] // 2 for s in slots]
        self.paired = [s.shape[0] == N_CHIPS // 2 for s in slots]
        self.widths = [s.shape[2] // 2 if p else s.shape[2] for s, p in zip(slots, self.paired)]

    def _copies(self, outs, sems):
        s_ici, r_ici, s_d2d, r_d2d = sems
        x, y, c, chips = _place()
        peers = [(j, chip, 2 * chip[0] + chip[1]) for j, chip in enumerate(chips)]

        def block(i, chip, h):
            rows = pl.ds(pl.multiple_of(h * self.halves[i], 8), self.halves[i])
            if self.paired[i]:
                return outs[i].at[chip // 2, rows, pl.ds(pl.multiple_of((chip % 2) * self.widths[i], LANES), self.widths[i])]
            return outs[i].at[chip, rows]

        def ici(i, j, chip, to):
            blk = block(i, chip, c)
            return pltpu.make_async_remote_copy(src_ref=blk, dst_ref=blk, send_sem=s_ici.at[i, j], recv_sem=r_ici.at[i, j],
                                                device_id=to, device_id_type=MESH)

        def d2d(i, j, chip, h):
            blk = block(i, chip, h)
            return pltpu.make_async_remote_copy(src_ref=blk, dst_ref=blk, send_sem=s_d2d.at[i, j], recv_sem=r_d2d.at[i, j],
                                                device_id=(x, y, 1 - c), device_id_type=MESH)
        return (x, y, c, 2 * x + y), peers, ici, d2d

    def before(self, cin, outs, sems, step, total):
        (x, y, c, q), peers, ici, d2d = self._copies(outs, sems)

        @pl.when(step == 0)
        def _():
            for i in range(len(outs)):
                for j, chip, _ in peers:
                    ici(i, j, q, (*chip, c)).start()

    def after(self, cin, outs, sems, step, total):
        (x, y, c, q), peers, ici, d2d = self._copies(outs, sems)

        @pl.when(step == max(total - 2, 0))
        def _():
            for i in range(len(outs)):
                for j, chip, qj in peers:
                    ici(i, j, qj, (x, y, c)).wait_recv()
                    d2d(i, j, qj, c).start()

        @pl.when(step == total - 1)
        def _():
            for i in range(len(outs)):
                for j, chip, qj in peers:
                    d2d(i, j, qj, 1 - c).wait_recv()
            for i in range(len(outs)):
                for j, chip, qj in peers:
                    ici(i, j, q, (*chip, c)).wait_send()
                    d2d(i, j, qj, c).wait_send()


class _ExchangeComm:
    def __init__(self, sums):
        n = len(sums)
        self.inputs = list(sums)
        self.out_shape = [S_((3,) + s.shape[1:], s.dtype) for s in sums]
        self.aliases = {}
        self.sems = [pltpu.SemaphoreType.DMA((n, 3))] * 2

    def _copies(self, srcs, outs, sems):
        s_sem, r_sem = sems
        x, y, c, chips = _place()
        return [pltpu.make_async_remote_copy(src_ref=srcs[i].at[2 * chip[0] + chip[1]], dst_ref=outs[i].at[j],
                                             send_sem=s_sem.at[i, j], recv_sem=r_sem.at[i, j],
                                             device_id=(*chip, c), device_id_type=MESH)
                for i in range(len(srcs)) for j, chip in enumerate(chips)]

    def before(self, srcs, outs, sems, step, total):
        @pl.when(step == 0)
        def _():
            for cp in self._copies(srcs, outs, sems):
                cp.start()

    def after(self, srcs, outs, sems, step, total):
        @pl.when(step == total - 1)
        def _():
            for cp in self._copies(srcs, outs, sems):
                cp.wait()


def _own_slots(specs):
    n = len(specs)
    shapes, blocks, count = [], [], []
    for a, (w, layer, dtype, pairs) in enumerate(specs):
        _, r, c = w.shape
        br = _row_block(r, c, 2 * BLOCK_BYTES)
        shapes.append(S_((N_CHIPS // 2, r, 2 * c) if pairs else (N_CHIPS, r, c), dtype))
        blocks += [(a, k, k * br, br) for k in range(r // br)]
        count.append(r // br)
    gather = _GatherComm(shapes[:1])
    n_sems = len(gather.sems)

    def body(*refs):
        srcs, outs = refs[:n], refs[n:2 * n]
        inb, outb = refs[2 * n:3 * n], refs[3 * n:4 * n]
        lsem, ssem = refs[4 * n], refs[4 * n + 1]
        gsems = refs[4 * n + 2:4 * n + 2 + n_sems]
        x, y, c, _ = _place()
        q = 2 * x + y

        def load(a, k, r0, br):
            return pltpu.make_async_copy(srcs[a].at[specs[a][1], pl.ds(r0, br)], inb[a].at[k % 2], lsem.at[a, k % 2])

        def store(a, k, r0, br):
            if specs[a][3]:
                wc = specs[a][0].shape[2]
                dst = outs[a].at[q // 2, pl.ds(r0, br), pl.ds(pl.multiple_of((q % 2) * wc, LANES), wc)]
            else:
                dst = outs[a].at[q, pl.ds(r0, br)]
            return pltpu.make_async_copy(outb[a].at[k % 2], dst, ssem.at[a, k % 2])

        load(*blocks[0]).start()
        for b, (a, k, r0, br) in enumerate(blocks):
            if b + 1 < len(blocks):
                load(*blocks[b + 1]).start()
            load(a, k, r0, br).wait()
            if k >= 2:
                store(a, k - 2, r0 - 2 * br, br).wait()
            outb[a][k % 2] = inb[a][k % 2].astype(specs[a][2])
            store(a, k, r0, br).start()
            if a == 0 and k == count[0] - 1:
                for kk in range(max(k - 1, 0), k + 1):
                    store(0, kk, kk * br, br).wait()
                gather.before(None, outs[:1], gsems, 0, 1)
        for a in range(1, n):
            br = blocks[sum(count[:a])][3]
            for kk in range(max(count[a] - 2, 0), count[a]):
                store(a, kk, kk * br, br).wait()
        gather.after(None, outs[:1], gsems, 0, 1)

    any_spec = pl.BlockSpec(memory_space=pl.ANY)
    scratch = ([pltpu.VMEM((2, blocks[sum(count[:a])][3], specs[a][0].shape[2]), F32) for a in range(n)]
               + [pltpu.VMEM((2, blocks[sum(count[:a])][3], specs[a][0].shape[2]), specs[a][2]) for a in range(n)]
               + [pltpu.SemaphoreType.DMA((n, 2)), pltpu.SemaphoreType.DMA((n, 2))] + list(gather.sems))
    outs = pl.pallas_call(body, name="own_slots", in_specs=[any_spec] * n, out_specs=[any_spec] * n, out_shape=shapes,
                          scratch_shapes=scratch,
                          compiler_params=pltpu.CompilerParams(vmem_limit_bytes=VMEM_LIMIT_BYTES))(*[s[0] for s in specs])
    return list(outs)


def _comm_only(comm, name):
    def body():
        pass
    outs = _pcall(body, name=name, grid=(1,), in_specs=[], out_specs=[], out_shape=[], args=[], comm=comm)
    return list(outs)


def _pair_exchange(parts, name):
    n = len(parts)

    def body(*refs):
        srcs, outs = refs[:n], refs[n:2 * n]
        s_sem, r_sem = refs[2 * n:]
        x, y, c, _ = _place()
        cps = []
        for i in range(n):
            for s in range(N_CHIPS):
                cp = pltpu.make_async_remote_copy(src_ref=srcs[i].at[s, 1 - c], dst_ref=outs[i].at[s], send_sem=s_sem.at[i, s],
                                                  recv_sem=r_sem.at[i, s], device_id=(x, y, 1 - c), device_id_type=MESH)
                cp.start()
                cps.append(cp)
        for cp in cps:
            cp.wait()

    any_spec = pl.BlockSpec(memory_space=pl.ANY)
    outs = pl.pallas_call(
        body, name=name, in_specs=[any_spec] * n, out_specs=[any_spec] * n,
        out_shape=[S_((N_CHIPS,) + p.shape[2:], p.dtype) for p in parts],
        scratch_shapes=[pltpu.SemaphoreType.DMA((n, N_CHIPS))] * 2,
    )(*parts)
    return list(outs)


class _PairComm:
    def __init__(self, parts):
        n = len(parts)
        self.inputs = list(parts)
        self.out_shape = [S_((N_CHIPS,) + p.shape[2:], p.dtype) for p in parts]
        self.aliases = {}
        self.sems = [pltpu.SemaphoreType.DMA((n, N_CHIPS))] * 2

    def _copies(self, srcs, outs, sems):
        s_sem, r_sem = sems
        x, y, c, _ = _place()
        return [pltpu.make_async_remote_copy(src_ref=srcs[i].at[s, 1 - c], dst_ref=outs[i].at[s], send_sem=s_sem.at[i, s],
                                             recv_sem=r_sem.at[i, s], device_id=(x, y, 1 - c), device_id_type=MESH)
                for i in range(len(srcs)) for s in range(N_CHIPS)]

    def before(self, srcs, outs, sems, step, total):
        @pl.when(step == 0)
        def _():
            for cp in self._copies(srcs, outs, sems):
                cp.start()

    def after(self, srcs, outs, sems, step, total):
        @pl.when(step == total - 1)
        def _():
            for cp in self._copies(srcs, outs, sems):
                cp.wait()


class _ShareComm:
    def __init__(self, bufs):
        n = len(bufs)
        self.inputs = list(bufs)
        self.out_shape = [S_(b.shape, b.dtype) for b in bufs]
        self.aliases = {i: i for i in range(n)}
        self.sems = [pltpu.SemaphoreType.DMA((n, max(b.shape[0] for b in bufs)))] * 2
        self.layers = [b.shape[0] for b in bufs]

    def _copies(self, outs, sems):
        s_sem, r_sem = sems
        x, y, c, _ = _place()
        cps = []
        for i, nl in enumerate(self.layers):
            for l in range(nl):
                blk = outs[i].at[l, c]
                cps.append(pltpu.make_async_remote_copy(src_ref=blk, dst_ref=blk, send_sem=s_sem.at[i, l],
                                                        recv_sem=r_sem.at[i, l], device_id=(x, y, 1 - c),
                                                        device_id_type=MESH))
        return cps

    def before(self, cin, outs, sems, step, total):
        @pl.when(step == 0)
        def _():
            for cp in self._copies(outs, sems):
                cp.start()

    def after(self, cin, outs, sems, step, total):
        @pl.when(step == total - 1)
        def _():
            for cp in self._copies(outs, sems):
                cp.wait()


class _Both:
    def __init__(self, a, b):
        self.parts = (a, b)
        self.inputs = a.inputs + b.inputs
        self.out_shape = a.out_shape + b.out_shape
        self.aliases = dict(a.aliases)
        self.aliases.update({len(a.inputs) + i: len(a.out_shape) + o for i, o in b.aliases.items()})
        self.sems = a.sems + b.sems

    def _each(self, phase, cin, cout, sems, step, total):
        i = o = s = 0
        for p in self.parts:
            ni, no, ns = len(p.inputs), len(p.out_shape), len(p.sems)
            getattr(p, phase)(cin[i:i + ni], cout[o:o + no], sems[s:s + ns], step, total)
            i, o, s = i + ni, o + no, s + ns

    def before(self, cin, cout, sems, step, total):
        self._each("before", cin, cout, sems, step, total)

    def after(self, cin, cout, sems, step, total):
        self._each("after", cin, cout, sems, step, total)


def _small_allreduce(groups, comm):
    n = len(groups)
    nci, nco, ncs = len(comm.inputs), len(comm.out_shape), len(comm.sems)
    arrays = []
    for grp in groups:
        for p in grp:
            if not isinstance(p, int) and not any(p[0] is a for a in arrays):
                arrays.append(p[0])
    shapes = []
    for grp in groups:
        rows = sum(p if isinstance(p, int) else p[2] - p[1] for p in grp)
        width = next(p[0].shape[1] for p in grp if not isinstance(p, int))
        assert rows % 8 == 0
        shapes.append((rows, width))
    na = len(arrays)

    def body(*refs):
        a_refs, cin = refs[:na], refs[na:na + nci]
        o_refs, cout = refs[na + nci:na + nci + n], refs[na + nci + n:na + nci + n + nco]
        rest = refs[na + nci + n + nco:]
        v_refs, sibs, css, gots = rest[:n], rest[n:2 * n], rest[2 * n:3 * n], rest[3 * n:4 * n]
        s_sem, r_sem = rest[4 * n:4 * n + 2]
        csems = rest[4 * n + 2:]
        for i, grp in enumerate(groups):
            o = 0
            for p in grp:
                if isinstance(p, int):
                    v_refs[i][o:o + p, :] = jnp.zeros((p, shapes[i][1]), F32)
                    o += p
                else:
                    k = next(t for t, a in enumerate(arrays) if a is p[0])
                    v_refs[i][o:o + p[2] - p[1], :] = a_refs[k][p[1]:p[2], :]
                    o += p[2] - p[1]
        x, y, c, chips = _place()
        q = 2 * x + y
        to_sib = [pltpu.make_async_remote_copy(src_ref=v_refs[i], dst_ref=sibs[i], send_sem=s_sem.at[i, 3], recv_sem=r_sem.at[i, 3],
                                               device_id=(x, y, 1 - c), device_id_type=MESH) for i in range(n)]
        for cp in to_sib:
            cp.start()
        cps = []
        for i in range(n):
            to_sib[i].wait()
            mine, other = v_refs[i][...], sibs[i][...]
            css[i][...] = jnp.where(c == 0, mine, other) + jnp.where(c == 0, other, mine)
            for j, chip in enumerate(chips):
                cp = pltpu.make_async_remote_copy(src_ref=css[i], dst_ref=gots[i].at[j], send_sem=s_sem.at[i, j],
                                                  recv_sem=r_sem.at[i, j], device_id=(*chip, c), device_id_type=MESH)
                cp.start()
                cps.append(cp)
        comm.before(cin, cout, csems, 0, 1)
        for cp in cps:
            cp.wait()
        flips = [2, 1, 3]
        for i in range(n):
            total = None
            for k in range(N_CHIPS):
                d = q ^ k
                term = jnp.where(d == 0, css[i][...], 0.0)
                for j in range(3):
                    term = jnp.where(d == flips[j], gots[i][j], term)
                total = term if total is None else total + term
            o_refs[i][...] = total
        comm.after(cin, cout, csems, 0, 1)

    vm, any_spec = pl.BlockSpec(memory_space=pltpu.VMEM), pl.BlockSpec(memory_space=pl.ANY)
    outs = pl.pallas_call(
        body, name="small_allreduce", in_specs=[vm] * na + [any_spec] * nci, out_specs=[vm] * n + [any_spec] * nco,
        out_shape=[S_(s, F32) for s in shapes] + list(comm.out_shape),
        scratch_shapes=[pltpu.VMEM(s, F32) for s in shapes] * 3 + [pltpu.VMEM((3,) + s, F32) for s in shapes]
        + [pltpu.SemaphoreType.DMA((n, 4)), pltpu.SemaphoreType.DMA((n, 4))] + list(comm.sems),
    )(*arrays, *comm.inputs)
    return list(outs[:n]), list(outs[n:])


def _row_block(r, c, target_bytes=BLOCK_BYTES):
    br = r
    while br % (2 * BF16_ROWS) == 0 and br * c * 4 > target_bytes:
        br //= 2
    return br


def _chip_sum(part, sib, ids, name):
    _, _, r, c = part.shape
    br = _row_block(r, c, 4 * BLOCK_BYTES)

    def body(ids_ref, p_ref, s_ref, o_ref):
        o_ref[...] = (p_ref[...].astype(F32) + s_ref[...].astype(F32)).astype(BF16)

    return _pcall(body, name=name, grid=(N_CHIPS, r // br), prefetch=1,
                  in_specs=[pl.BlockSpec((None, None, br, c), lambda s, i, ids: (s, ids[1], i, 0)),
                            pl.BlockSpec((None, br, c), lambda s, i, ids: (s, i, 0))],
                  out_specs=pl.BlockSpec((None, br, c), lambda s, i, ids: (s, i, 0)),
                  out_shape=S_((N_CHIPS, r, c), BF16), args=(ids, part, sib))


def _final_sum(part, sib, got, ids, name, layer=0, n_layers=1, buf=None):
    _, r, c = got.shape
    br = _row_block(r, c, 2 * BLOCK_BYTES)

    def body(ids_ref, p_ref, s_ref, got_ref, *rest):
        v = p_ref[...].astype(F32) + s_ref[...].astype(F32)
        for j in range(3):
            v = v + got_ref[j].astype(F32)
        rest[-1][...] = v

    in_specs = [pl.BlockSpec((None, None, br, c), lambda i, ids: (ids[0], ids[1], i, 0)),
                pl.BlockSpec((None, br, c), lambda i, ids: (ids[0], i, 0)),
                pl.BlockSpec((3, br, c), lambda i, ids: (0, i, 0))]
    args = [ids, part, sib, got]
    if buf is not None:
        in_specs.append(pl.BlockSpec(memory_space=pl.ANY))
        args.append(buf)
    return _pcall(body, name=name, grid=(r // br,), prefetch=1, in_specs=in_specs,
                  out_specs=pl.BlockSpec((None, None, br, c), lambda i, ids: (layer, ids[1], i, 0)),
                  out_shape=S_((n_layers, 2, r, c), F32), args=args, aliases={4: 0} if buf is not None else None)


def _adam_math(w, g, m, v):
    c1 = 1.0 / (1.0 - ADAM_B1 ** ADAM_STEP)
    c2 = 1.0 / (1.0 - ADAM_B2 ** ADAM_STEP)
    mn = ADAM_B1 * m + (1.0 - ADAM_B1) * g
    vn = ADAM_B2 * v + (1.0 - ADAM_B2) * (g * g)
    return -ADAM_LR * ((mn * c1) / (jnp.sqrt(vn * c2) + ADAM_EPS) + ADAM_WD * w), mn, vn


def _adamw(w, g, m, v, name):
    nl, r, c = w.shape
    br = _row_block(r, c)

    def body(w_ref, g_ref, m_ref, v_ref, go_ref, d_ref, mo_ref, vo_ref):
        gv = g_ref[...]
        go_ref[...] = gv
        d_ref[...], mo_ref[...], vo_ref[...] = _adam_math(w_ref[...], gv, m_ref[...], v_ref[...])

    blk = pl.BlockSpec((None, br, c), lambda l, i: (l, i, 0))
    return _pcall(body, name=name, grid=(nl, r // br), in_specs=[blk] * 4, out_specs=[blk] * 4,
                  out_shape=[S_((nl, r, c), F32)] * 4, args=(w, g, m, v))


def _adamw_small(rd, rf, sharded_g, ws, ms, vs, rows_d, rows_f, loss_row):
    n = len(ws)
    ns = len(sharded_g)
    rep = [k for k in range(n) if rows_d[k] is not None or rows_f[k] is not None]

    def body(*refs):
        rd_ref, rf_ref = refs[0], refs[1]
        sg = refs[2:2 + ns]
        w_refs, m_refs, v_refs = (refs[2 + ns + t * n:2 + ns + (t + 1) * n] for t in range(3))
        loss_ref, outs = refs[2 + ns + 3 * n], refs[3 + ns + 3 * n:]
        g_out, d_out, m_out, v_out = outs[:len(rep)], outs[len(rep):len(rep) + n], outs[len(rep) + n:len(rep) + 2 * n], \
            outs[len(rep) + 2 * n:]
        loss_ref[...] = jnp.sum(rd_ref[loss_row:loss_row + 1, :], axis=1, keepdims=True)
        si = 0
        for k in range(n):
            shape = w_refs[k].shape
            if rows_d[k] is not None or rows_f[k] is not None:
                src, rws = (rd_ref, rows_d[k]) if rows_d[k] is not None else (rf_ref, rows_f[k])
                axis = 0 if shape[0] == len(rws) else 1
                g = src[rws[0]:rws[0] + 1, :]
                if len(rws) > 1:
                    g = jnp.concatenate([src[r:r + 1, :] for r in rws], axis=axis)
                g_out[rep.index(k)][...] = g
            else:
                g = sg[si][...]
                si += 1
            d_out[k][...], m_out[k][...], v_out[k][...] = _adam_math(w_refs[k][...], g, m_refs[k][...], v_refs[k][...])

    vm = pl.BlockSpec(memory_space=pltpu.VMEM)
    shapes = [S_(w.shape, F32) for w in ws]
    outs = pl.pallas_call(
        body, name="adamw_small", in_specs=[vm] * (2 + ns + 3 * n), out_specs=[vm] * (1 + len(rep) + 3 * n),
        out_shape=[S_((1, 1), F32)] + [shapes[k] for k in rep] + shapes * 3,
    )(rd, rf, *sharded_g, *ws, *ms, *vs)
    g_rep = dict(zip(rep, outs[1:1 + len(rep)]))
    o = outs[1 + len(rep):]
    return outs[0], g_rep, o[:n], o[n:2 * n], o[2 * n:]


def _pack(pieces, rows):
    flat = jnp.concatenate([p.reshape(-1).astype(F32) for p in pieces])
    return jnp.pad(flat, (0, rows * LANES - flat.shape[0])).reshape(rows, LANES)


def _unpack(packed, shapes):
    flat = packed.reshape(-1)
    out, o = [], 0
    for s in shapes:
        n = 1
        for d in s:
            n *= d
        out.append(flat[o:o + n].reshape(s))
        o += n
    return out


def _rows_for(pieces_or_shapes):
    n = 0
    for p in pieces_or_shapes:
        k = 1
        for d in (p if isinstance(p, tuple) else p.shape):
            k *= d
        n += k
    return -(-n // (SUBLANES * LANES)) * SUBLANES


def kernel(x, norm_mix, norm_ffn, conv_w_pw1, conv_b_pw1, conv_w_dw, conv_b_dw, conv_ln_g, conv_ln_b, conv_w_pw2, conv_b_pw2, pool_w, pool_b, pool_scale, ffn_w_up, ffn_w_dw, ffn_b_dw, ffn_w_down, final_norm, loss_target, m_norm_mix, m_norm_ffn, m_conv_w_pw1, m_conv_b_pw1, m_conv_w_dw, m_conv_b_dw, m_conv_ln_g, m_conv_ln_b, m_conv_w_pw2, m_conv_b_pw2, m_pool_w, m_pool_b, m_pool_scale, m_ffn_w_up, m_ffn_w_dw, m_ffn_b_dw, m_ffn_w_down, m_final_norm, v_norm_mix, v_norm_ffn, v_conv_w_pw1, v_conv_b_pw1, v_conv_w_dw, v_conv_b_dw, v_conv_ln_g, v_conv_ln_b, v_conv_w_pw2, v_conv_b_pw2, v_pool_w, v_pool_b, v_pool_scale, v_ffn_w_up, v_ffn_w_dw, v_ffn_b_dw, v_ffn_w_down, v_final_norm):
    nb, seq, D = x.shape
    T = nb * seq
    F = ffn_w_down.shape[1] * N_CHIPS
    taps = conv_w_dw.shape[1]
    ffn_taps = ffn_w_dw.shape[1]
    ng = pool_w.shape[1]
    cg = pool_w.shape[3]
    dsh = D // N_CHIPS
    fsh = F // N_CHIPS
    assert taps - 1 <= HALO_CONV and ffn_taps - 1 <= HALO_FFN and max(POOL_WINDOWS) <= HALO_POOL
    tm = min(TILE_D, seq)
    tf = min(TILE_F, seq)
    assert seq % tm == 0 and seq % tf == 0 and tm % HALO_CONV == 0 and tf % CHUNK == 0
    tps, tpf = seq // tm, seq // tf
    tw = min(TILE_MM, T)
    assert T % tw == 0

    xi, yi, ci = lax.axis_index("x"), lax.axis_index("y"), lax.axis_index("c")
    qi = 2 * xi + yi
    ids = jnp.stack([qi, ci]).astype(jnp.int32)
    x2d = x.reshape(T, D)
    tg2d = loss_target.reshape(T, D)

    small_sharded = [conv_w_dw[0], ffn_w_dw, pool_b, pool_scale]
    srows = 2 * _rows_for(small_sharded)
    w1g, s_pw2, s_small, s_pool, s_up0, s_up1, s_down0, s_down1 = _own_slots([
        (conv_w_pw1, 0, BF16, False), (conv_w_pw2, 0, BF16, False), (_pack(small_sharded, srows)[None], 0, F32, False),
        (pool_w.reshape(1, ng * (cg // N_CHIPS), cg), 0, BF16, False), (ffn_w_up, 0, BF16, True), (ffn_w_up, 1, BF16, True),
        (ffn_w_down, 0, BF16, False), (ffn_w_down, 1, BF16, False)])
    s_up, s_down = [s_up0, s_up1], [s_down0, s_down1]
    hb0t, a, w2g, smallg, pwg = _conv_in(x2d, norm_mix[0:1], w1g, conv_b_pw1, tm, comm=_GatherComm([s_pw2, s_small, s_pool]))
    w2f = w2g.reshape(D, D)
    sm_parts = [_unpack(smallg[s], [(taps, dsh), (2, ffn_taps, fsh), (1, dsh), (1, dsh)]) for s in range(N_CHIPS)]
    wdw_f = jnp.concatenate([p[0] for p in sm_parts], axis=1)
    wdw_f = jnp.pad(wdw_f, ((0, HALO_CONV - taps), (0, 0)))
    fdw_f = jnp.concatenate([p[1] for p in sm_parts], axis=2)
    fdw_f = jnp.pad(fdw_f, ((0, 0), (0, 8 - ffn_taps), (0, 0)))
    pb_f = jnp.concatenate([p[2] for p in sm_parts], axis=1)
    ps_f = jnp.concatenate([p[3] for p in sm_parts], axis=1)

    c, sbt, x1, hb1, hb1t, wup0, wd0 = _conv_mid(a, x2d, wdw_f, conv_b_dw, conv_ln_g, conv_ln_b, w2f, conv_b_pw2,
                                                 norm_ffn[0:1], tm, tps, taps, comm=_GatherComm([s_up[0], s_down[0]]))
    up0, gb0t, x2, h2, wup1, wd1 = _ffn_fwd(hb1, x1, wup0, fdw_f[0], ffn_b_dw[0:1], wd0.reshape(F, D), norm_mix[1:2], tf, tpf,
                                            ffn_taps, "ffn_fwd0", comm=_GatherComm([s_up[1], s_down[1]]))
    pwf = pwg.reshape(N_CHIPS, ng, cg // N_CHIPS, cg).transpose(1, 0, 2, 3).reshape(ng, cg, cg)
    plb, x3, hb3, hb3t = _pool_fwd(h2, x2, pwf, pb_f, ps_f, norm_ffn[1:2], tm, tps)
    wup = [wup0, wup1]
    wdn = [wd0.reshape(F, D), wd1.reshape(F, D)]
    up1, gb1t, dx4, loss_part, d_final = _ffn_fwd(hb3, x3, wup[1], fdw_f[1], ffn_b_dw[1:2], wdn[1], final_norm.reshape(1, D),
                                                  tf, tpf, ffn_taps, "ffn_fwd1", target=tg2d)

    fs = {}

    def pair(group):
        return list(group), list(group.values()), _PairComm(list(group.values()))

    def chip_sums(nms, parts, from_sib):
        for nm, p, s in zip(nms, parts, from_sib):
            fs[nm] = (p, s)
        return nms, _ExchangeComm([_chip_sum(p, s, ids, "chip_sum_" + nm) for nm, p, s in zip(nms, parts, from_sib)])

    def arrived(nms, got):
        for nm, g in zip(nms, got):
            fs[nm] = fs[nm] + (g,)

    dac1, dup1, fw1 = _bwd_down(dx4, up1, fdw_f[1], ffn_b_dw[1:2], wdn[1], tf, tpf, ffn_taps, "bwd_down1")
    p_down1 = _mm_tn(gb1t, dx4, F // 2, D, tw, False, "dw_down1")
    nms, parts, px = pair({"down1": p_down1.reshape(N_CHIPS, 2, fsh // 2, D)})
    dup1, dx3, dnf1, *sib = _bwd_up(dac1, dup1, wup[1], fdw_f[1], x3, norm_ffn[1:2], dx4, tf, tpf, ffn_taps, "bwd_up1", comm=px)
    nms, xchg = chip_sums(nms, parts, sib)
    p_up1, *got = _mm_tn(hb3t, dup1, D // 2, 2 * fsh, T, True, "dw_up1", comm=xchg)
    arrived(nms, got)
    nms, parts, px = pair({"up1": p_up1.reshape(N_CHIPS, 2, D // 2, 2 * fsh)})
    dx2, dpw, pool_sm, *sib = _pool_bwd(dx3, plb, x2, pwf, pb_f, ps_f, norm_mix[1:2], tm, tps, comm=px)
    nms, xchg = chip_sums(nms, parts, sib)
    dac0, dup0, fw0, *got = _bwd_down(dx2, up0, fdw_f[0], ffn_b_dw[0:1], wdn[0], tf, tpf, ffn_taps, "bwd_down0", comm=xchg)
    arrived(nms, got)
    p_down0 = _mm_tn(gb0t, dx2, F // 2, D, tw, False, "dw_down0")
    nms, parts, px = pair({"down0": p_down0.reshape(N_CHIPS, 2, fsh // 2, D)})
    dup0, dx1, dnf0, *sib = _bwd_up(dac0, dup0, wup[0], fdw_f[0], x1, norm_ffn[0:1], dx2, tf, tpf, ffn_taps, "bwd_up0", comm=px)
    nms, xchg = chip_sums(nms, parts, sib)
    p_up0, *got = _mm_tn(hb1t, dup0, D // 2, 2 * fsh, T, True, "dw_up0", comm=xchg)
    arrived(nms, got)
    csh = cg // N_CHIPS
    p_pool = dpw.reshape(2, ng // 2, N_CHIPS, csh, cg).transpose(2, 0, 1, 3, 4).reshape(N_CHIPS, 2, (ng // 2) * csh, cg)
    nms, parts, px = pair({"up0": p_up0.reshape(N_CHIPS, 2, D // 2, 2 * fsh), "pool": p_pool.astype(BF16)})
    dc, conv_sm, *sib = _conv_bwd1(dx1, c, w2f, conv_ln_g, conv_ln_b, tm, comm=px)
    nms, xchg = chip_sums(nms, parts, sib)
    p_pw2 = _mm_tn(sbt, dx1, D, D, tw, False, "dw_pw2")
    nms2, parts2, px = pair({"pw2": p_pw2.reshape(N_CHIPS, 2, dsh // 2, D)})
    grad_x, dab, dwdw, db1, dnm0, *both = _conv_bwd2(dc, a, x2d, w1g, wdw_f, norm_mix[0:1], dx1, tm, tps, taps,
                                                     comm=_Both(xchg, px))
    arrived(nms, both[:len(nms)])
    nms, xchg = chip_sums(nms2, parts2, both[len(nms):])
    p_pw1, *got = _mm_tn(hb0t, dab, D, 2 * dsh, T, True, "dw_pw1", comm=xchg)
    arrived(nms, got)
    nms, parts, _ = pair({"pw1": p_pw1.reshape(N_CHIPS, 2, D // 2, 2 * dsh)})
    nms, xchg = chip_sums(nms, parts, _pair_exchange(parts, "grad_pair_exchange_pw1"))
    (rd, rf), got = _small_allreduce([
        [(dnm0, 0, 1), (pool_sm, 2, 3), (dnf0, 0, 1), (dnf1, 0, 1), (db1, 0, 2), (conv_sm, 0, 4), (d_final, 0, 1),
         (pool_sm, 0, 2), (loss_part, 0, 1), 2, (dwdw, 0, dwdw.shape[0])],
        [(fw0, 0, 4), (fw1, 0, 4)]], xchg)
    arrived(nms, got)
    bufs = [_final_sum(*fs[nm], ids, "final_sum_" + nm) for nm in ("pw1", "pw2", "pool")]
    for nm in ("up", "down"):
        b = _final_sum(*fs[nm + "0"], ids, "final_sum_" + nm + "0", layer=0, n_layers=2)
        bufs.append(_final_sum(*fs[nm + "1"], ids, "final_sum_" + nm + "1", layer=1, n_layers=2, buf=b))
    shared = _comm_only(_ShareComm(bufs), "grad_share_halves")
    g_pw1, g_pw2, g_pool, g_up, g_down = [b.reshape(b.shape[0], 2 * b.shape[2], b.shape[3]) for b in shared]
    g_wdw = lax.dynamic_slice_in_dim(rd[16:16 + taps], qi * dsh, dsh, axis=1)
    g_ps = lax.dynamic_slice_in_dim(rd[11:12], qi * dsh, dsh, axis=1)
    g_pb = lax.dynamic_slice_in_dim(rd[12:13], qi * dsh, dsh, axis=1)
    g_fdw = lax.dynamic_slice_in_dim(rf.reshape(2, 4, F)[:, :ffn_taps], qi * fsh, fsh, axis=2).reshape(2 * ffn_taps, fsh)

    def big(w, g, m, v, nm):
        return tuple(t.reshape(w.shape) for t in _adamw(w.reshape(g.shape), g, m.reshape(g.shape), v.reshape(g.shape),
                                                        "adamw_" + nm))

    o_pw1 = big(conv_w_pw1, g_pw1, m_conv_w_pw1, v_conv_w_pw1, "pw1")
    o_pw2 = big(conv_w_pw2, g_pw2, m_conv_w_pw2, v_conv_w_pw2, "pw2")
    o_pool = big(pool_w, g_pool, m_pool_w, v_pool_w, "pool")
    o_up = big(ffn_w_up, g_up, m_ffn_w_up, v_ffn_w_up, "up")
    o_down = big(ffn_w_down, g_down, m_ffn_w_down, v_ffn_w_down, "down")

    snames = ["norm_mix", "norm_ffn", "b_pw1", "w_dw", "b_dw", "ln_g", "ln_b", "b_pw2", "pool_b", "pool_scale", "ffn_w_dw",
              "ffn_b_dw", "final_norm"]
    sw = [norm_mix, norm_ffn, conv_b_pw1, conv_w_dw, conv_b_dw, conv_ln_g, conv_ln_b, conv_b_pw2, pool_b, pool_scale,
          ffn_w_dw, ffn_b_dw, final_norm]
    smm = [m_norm_mix, m_norm_ffn, m_conv_b_pw1, m_conv_w_dw, m_conv_b_dw, m_conv_ln_g, m_conv_ln_b, m_conv_b_pw2, m_pool_b,
           m_pool_scale, m_ffn_w_dw, m_ffn_b_dw, m_final_norm]
    svv = [v_norm_mix, v_norm_ffn, v_conv_b_pw1, v_conv_w_dw, v_conv_b_dw, v_conv_ln_g, v_conv_ln_b, v_conv_b_pw2, v_pool_b,
           v_pool_scale, v_ffn_w_dw, v_ffn_b_dw, v_final_norm]
    sshapes = [tuple(w.shape) for w in sw]
    two_d = [(2, D), (2, D), (1, 2 * D), (taps, dsh), (1, D), (1, D), (1, D), (1, D), (1, dsh), (1, dsh), (2 * ffn_taps, fsh),
             (2, F), (1, D)]
    rows_d = [[0, 1], [2, 3], [4, 5], None, [9], [6], [7], [8], None, None, None, None, [10]]
    rows_f = [None] * 11 + [[3, 7], None]
    as2d = lambda ts: [t.reshape(s) for t, s in zip(ts, two_d)]
    loss, g_rep, sd, sm_new, sv_new = _adamw_small(rd, rf, [g_wdw, g_pb, g_ps, g_fdw], as2d(sw), as2d(smm), as2d(svv), rows_d,
                                                   rows_f, loss_row=13)
    sg = dict(g_rep)
    sg.update({3: g_wdw, 8: g_pb, 9: g_ps, 10: g_fdw})
    small_out = {n: tuple(t.reshape(sshapes[k]) for t in (sg[k], sd[k], sm_new[k], sv_new[k])) for k, n in enumerate(snames)}

    order = [small_out["norm_mix"], small_out["norm_ffn"], o_pw1, small_out["b_pw1"], small_out["w_dw"], small_out["b_dw"],
             small_out["ln_g"], small_out["ln_b"], o_pw2, small_out["b_pw2"], o_pool, small_out["pool_b"],
             small_out["pool_scale"], o_up, small_out["ffn_w_dw"], small_out["ffn_b_dw"], o_down, small_out["final_norm"]]
    return (loss[0, 0], grad_x.reshape(nb, seq, D), *[o[0] for o in order], *[o[1] for o in order], *[o[2] for o in order],
            *[o[3] for o in order])
```

```python
import functools

import jax
import jax.numpy as jnp
from jax import lax
from jax.experimental import pallas as pl
from jax.experimental.pallas import tpu as pltpu

F32, BF16 = jnp.float32, jnp.bfloat16
S_ = jax.ShapeDtypeStruct
MESH = pl.DeviceIdType.MESH

RMS_EPS, LN_EPS = 1e-6, 1e-5
POOL_WINDOWS = (2, 4, 8, 16)
ADAM_LR, ADAM_B1, ADAM_B2, ADAM_EPS, ADAM_WD, ADAM_STEP = 0.001, 0.9, 0.999, 1e-08, 0.01, 10

VMEM_LIMIT_BYTES = 60 * 1024 * 1024
SUBLANES, LANES = 8, 128
BF16_ROWS = 16
BLOCK_BYTES = 1 << 20
N_CHIPS = 4
CHUNK = 32
HALO_CONV = 32
HALO_POOL = 32
HALO_FFN = 8
TILE_D = 512
TILE_F = 256


def _pcall(body, *, name, grid, in_specs, out_specs, out_shape, args, scratch=(), aliases=None, prefetch=0, comm=None):
    params = pltpu.CompilerParams(dimension_semantics=("arbitrary",) * len(grid), vmem_limit_bytes=VMEM_LIMIT_BYTES)
    if comm is not None:
        assert not prefetch
        single = not isinstance(out_shape, (list, tuple))
        in_specs, args, scratch = list(in_specs), list(args), list(scratch)
        out_specs, out_shape = ([out_specs], [out_shape]) if single else (list(out_specs), list(out_shape))
        n_in, n_out, n_scr, n_cin, n_cout = len(in_specs), len(out_specs), len(scratch), len(comm.inputs), len(comm.out_shape)
        any_spec = pl.BlockSpec(memory_space=pl.ANY)
        aliases = dict(aliases or {})
        aliases.update({n_in + a: n_out + b for a, b in comm.aliases.items()})
        inner = body

        def body(*refs):
            ins, cin = refs[:n_in], refs[n_in:n_in + n_cin]
            outs = refs[n_in + n_cin:n_in + n_cin + n_out]
            cout = refs[n_in + n_cin + n_out:n_in + n_cin + n_out + n_cout]
            scr = refs[n_in + n_cin + n_out + n_cout:n_in + n_cin + n_out + n_cout + n_scr]
            sems = refs[n_in + n_cin + n_out + n_cout + n_scr:]
            step, total = 0, 1
            for d, g in enumerate(grid):
                step = step * g + pl.program_id(d)
                total *= g
            comm.before(cin, cout, sems, step, total)
            inner(*ins, *outs, *scr)
            comm.after(cin, cout, sems, step, total)

        in_specs += [any_spec] * n_cin
        args += list(comm.inputs)
        out_specs += [any_spec] * n_cout
        out_shape += list(comm.out_shape)
        scratch += list(comm.sems)
    if prefetch:
        spec = pltpu.PrefetchScalarGridSpec(num_scalar_prefetch=prefetch, grid=grid, in_specs=in_specs,
                                            out_specs=out_specs, scratch_shapes=list(scratch))
        return pl.pallas_call(body, name=name, grid_spec=spec, out_shape=out_shape,
                              input_output_aliases=aliases or {}, compiler_params=params)(*args)
    return pl.pallas_call(body, name=name, grid=grid, in_specs=in_specs, out_specs=out_specs, out_shape=out_shape,
                          scratch_shapes=list(scratch), input_output_aliases=aliases or {},
                          compiler_params=params)(*args)


def _full(shape):
    n = len(shape)
    return pl.BlockSpec(tuple(shape), lambda *_: (0,) * n, pipeline_mode=pl.Buffered(1))


def _rows(tm, c, col=0):
    return pl.BlockSpec((tm, c), lambda i, *_: (i, col))


def _cols(r, tm):
    return pl.BlockSpec((r, tm), lambda i, *_: (0, i))


def _prev_halo(tm, hb, c):
    return pl.BlockSpec((hb, c), lambda i, *_: (jnp.maximum(i * (tm // hb) - 1, 0), 0))


def _next_halo(tm, hb, c, total_rows):
    last = total_rows // hb - 1
    return pl.BlockSpec((hb, c), lambda i, *_: (jnp.minimum((i + 1) * (tm // hb), last), 0))


def _acc_spec(r, c):
    return pl.BlockSpec((r, c), lambda *_: (0, 0))


def _rms(x, g):
    r = lax.rsqrt(jnp.mean(x * x, axis=-1, keepdims=True) + RMS_EPS)
    xh = x * r
    return xh * g, xh, r


def _rms_bwd(dy, xh, r, g):
    dxh = dy * g
    dx = r * (dxh - xh * jnp.mean(dxh * xh, axis=-1, keepdims=True))
    return dx, jnp.sum(dy * xh, axis=0, keepdims=True)


def _colsum(v):
    return jnp.sum(v, axis=0, keepdims=True)


def _fold8(v):
    out = v[0:8]
    for j in range(1, v.shape[0] // 8):
        out = out + v[8 * j:8 * j + 8]
    return out


def _chunks(n_rows, fn):
    def step(j, carry):
        fn(pl.multiple_of(j * CHUNK, CHUNK))
        return carry
    lax.fori_loop(0, n_rows // CHUNK, step, 0)


def _dot(a, b):
    return jnp.dot(a, b, preferred_element_type=F32)


def _dot_nt(a, b):
    return lax.dot_general(a, b, (((1,), (1,)), ((), ())), preferred_element_type=F32)


def _dot_tn(a, b):
    return lax.dot_general(a, b, (((0,), (0,)), ((), ())), preferred_element_type=F32)


def _sig(v):
    return jax.nn.sigmoid(v)


def _accumulate(ref, val, first):
    @pl.when(first)
    def _():
        ref[...] = val

    @pl.when(jnp.logical_not(first))
    def _():
        ref[...] = ref[...] + val


def _conv_in(x, g, w1g, b1, tm, comm=None):
    T, D = x.shape
    ns, _, nc = w1g.shape

    def body(x_ref, g_ref, w_ref, b_ref, hbt_ref, a_ref):
        hb = _rms(x_ref[...], g_ref[...])[0].astype(BF16)
        hbt_ref[...] = hb.T
        for s in range(ns):
            a_ref[:, s * nc:(s + 1) * nc] = _dot(hb, w_ref[s]) + b_ref[:, s * nc:(s + 1) * nc]

    return _pcall(body, name="conv_in", grid=(T // tm,),
                  in_specs=[_rows(tm, D), _full((1, D)), _full(w1g.shape), _full((1, ns * nc))],
                  out_specs=[_cols(D, tm), _rows(tm, ns * nc)],
                  out_shape=[S_((D, T), BF16), S_((T, ns * nc), F32)], args=(x, g, w1g, b1), comm=comm)


def _conv_mid(a, x, wdw, bdw, lng, lnb, w2, b2, gn, tm, tps, taps, comm=None):
    T, D = x.shape
    hc = HALO_CONV

    def body(a_ref, ah_ref, x_ref, wdw_ref, bdw_ref, lng_ref, lnb_ref, w2_ref, b2_ref, gn_ref,
             c_ref, sbt_ref, x1_ref, hb_ref, hbt_ref, uext, shifted):
        i = pl.program_id(0)
        ah = ah_ref[...]
        uext[0:hc, :] = jnp.where(i % tps == 0, 0.0, ah[:, :D] * _sig(ah[:, D:]))
        av = a_ref[...]
        uext[hc:hc + tm, :] = av[:, :D] * _sig(av[:, D:])

        def conv(r0, src, ks, first):
            acc = jnp.broadcast_to(bdw_ref[...], (CHUNK, D)) if first else c_ref[pl.ds(r0, CHUNK), :]
            for k, o in ks:
                acc = acc + wdw_ref[k:k + 1, :] * src[pl.ds(pl.multiple_of(r0 + o, 8), CHUNK), :]
            c_ref[pl.ds(r0, CHUNK), :] = acc
        _by_sublane_shift(uext, shifted, tm, [hc - (taps - 1) + k for k in range(taps)], conv)

        cv = c_ref[...]
        xc = cv - jnp.mean(cv, axis=-1, keepdims=True)
        nh = xc * lax.rsqrt(jnp.mean(xc * xc, axis=-1, keepdims=True) + LN_EPS)
        n = nh * lng_ref[...] + lnb_ref[...]
        sb = (n * _sig(n)).astype(BF16)
        sbt_ref[...] = sb.T
        x1 = x_ref[...] + _dot(sb, w2_ref[...]) + b2_ref[...]
        x1_ref[...] = x1
        hb = _rms(x1, gn_ref[...])[0].astype(BF16)
        hb_ref[...] = hb
        hbt_ref[...] = hb.T

    return _pcall(body, name="conv_mid", grid=(T // tm,),
                  in_specs=[_rows(tm, 2 * D), _prev_halo(tm, hc, 2 * D), _rows(tm, D), _full(wdw.shape), _full((1, D)),
                            _full((1, D)), _full((1, D)), _full((D, D)), _full((1, D)), _full((1, D))],
                  out_specs=[_rows(tm, D), _cols(D, tm), _rows(tm, D), _rows(tm, D), _cols(D, tm)],
                  out_shape=[S_((T, D), F32), S_((D, T), BF16), S_((T, D), F32), S_((T, D), BF16), S_((D, T), BF16)],
                  scratch=[pltpu.VMEM((tm + hc, D), F32), pltpu.VMEM((tm + hc - 8, D), F32)],
                  args=(a, a, x, wdw, bdw, lng, lnb, w2, b2, gn), comm=comm)


def _ffn_taps(aext, realigned, tm, ffn_taps):
    for k in range(ffn_taps - 1):
        realigned[k] = aext[pl.ds(HALO_FFN - (ffn_taps - 1) + k, tm), :]
    return [realigned.at[k] for k in range(ffn_taps - 1)] + [aext.at[pl.ds(HALO_FFN, tm)]]


def _ffn_conv(taps, wdw_ref, bdw_ref):
    acc = bdw_ref[...]
    for k, tap in enumerate(taps):
        acc = acc + wdw_ref[k:k + 1, :] * tap[...]
    return acc


def _by_sublane_shift(ext, shifted, tm, offsets, per_chunk):
    first = True
    for b in range(8):
        ks = [(k, o - b) for k, o in enumerate(offsets) if o % 8 == b]
        if not ks:
            continue
        if b:
            shifted[...] = ext[pl.ds(b, shifted.shape[0]), :]
        _chunks(tm, functools.partial(per_chunk, src=shifted if b else ext, ks=ks, first=first))
        first = False


def _ffn_fwd(hb, xin, wg, wdw, bdw, wd, gn, tm, tps, ffn_taps, name, target=None, comm=None):
    T, D = xin.shape
    ns, _, nc = wg.shape
    F = ns * nc // 2
    hf = HALO_FFN
    hh = 2 * HALO_FFN
    final = target is not None

    def body(*refs):
        if final:
            (h_ref, hh_ref, x_ref, wg_ref, wdw_ref, bdw_ref, wd_ref, gn_ref, tg_ref,
             up_ref, gb_ref, dx_ref, dxb_ref, loss_ref, dgn_ref, aext, realigned) = refs
        else:
            (h_ref, hh_ref, x_ref, wg_ref, wdw_ref, bdw_ref, wd_ref, gn_ref,
             up_ref, gb_ref, xo_ref, ho_ref, aext, realigned) = refs
        i = pl.program_id(0)
        hv = h_ref[...]
        for s in range(ns):
            up_ref[:, s * nc:(s + 1) * nc] = _dot(hv, wg_ref[s])
        halo = jnp.concatenate([_dot(hh_ref[...], wg_ref[s]) for s in range(ns // 2)], axis=1)[hh - hf:hh, :]
        aext[0:hf, :] = jnp.where(i % tps == 0, 0.0, halo)
        aext[hf:hf + tm, :] = up_ref[:, 0:F]
        acv = _ffn_conv(_ffn_taps(aext, realigned, tm, ffn_taps), wdw_ref, bdw_ref)
        gb = (acv * _sig(acv) * up_ref[:, F:2 * F]).astype(BF16)
        gb_ref[...] = gb.T
        xo = x_ref[...] + _dot(gb, wd_ref[...])
        if not final:
            xo_ref[...] = xo
            ho_ref[...] = _rms(xo, gn_ref[...])[0]
        else:
            out, xh, r = _rms(xo, gn_ref[...])
            diff = out - tg_ref[...]
            part = 0.5 / D * _colsum(diff * diff)
            dx, dg = _rms_bwd(diff * (1.0 / D), xh, r, gn_ref[...])
            dx_ref[...] = dx
            dxb_ref[...] = dx.astype(BF16)
            _accumulate(loss_ref, part, i == 0)
            _accumulate(dgn_ref, dg, i == 0)

    in_specs = [_rows(tm, D), _prev_halo(tm, hh, D), _rows(tm, D), _full(wg.shape), _full(wdw.shape), _full((1, F)),
                _full((F, D)), _full((1, D))]
    args = [hb, hb, xin, wg, wdw, bdw, wd, gn]
    if final:
        in_specs.append(_rows(tm, D))
        args.append(target)
        out_specs = [_rows(tm, 2 * F), _cols(F, tm), _rows(tm, D), _rows(tm, D), _acc_spec(1, D), _acc_spec(1, D)]
        out_shape = [S_((T, 2 * F), F32), S_((F, T), BF16), S_((T, D), F32), S_((T, D), BF16), S_((1, D), F32),
                     S_((1, D), F32)]
    else:
        out_specs = [_rows(tm, 2 * F), _cols(F, tm), _rows(tm, D), _rows(tm, D)]
        out_shape = [S_((T, 2 * F), F32), S_((F, T), BF16), S_((T, D), F32), S_((T, D), F32)]
    return _pcall(body, name=name, grid=(T // tm,), in_specs=in_specs, out_specs=out_specs, out_shape=out_shape,
                  scratch=[pltpu.VMEM((tm + hf, F), F32), pltpu.VMEM((ffn_taps - 1, tm, F), F32)], args=args, comm=comm)


def _pool_counts(i, tps, tm, w):
    pos = (i % tps) * tm + lax.broadcasted_iota(jnp.int32, (tm, 1), 0)
    return jnp.minimum(pos + 1, w).astype(F32)


def _window_sums(src, buf_a, buf_b, cg, causal):
    assert POOL_WINDOWS == (2, 4, 8, 16)
    n = src.shape[0]
    levels = len(POOL_WINDOWS)
    cur, outs = src, []
    for l in range(levels):
        dst = buf_b if l % 2 else buf_a
        cols = slice(l * cg, levels * cg)
        lo, m = 8 * (l + 1), n - 8 * (l + 1)
        if causal:
            dst[lo:n, cols] = cur[lo:n, cols] + cur[pl.ds(lo - (1 << l), m), cols]
        else:
            dst[0:m, cols] = cur[0:m, cols] + cur[pl.ds(1 << l, m), cols]
        outs.append(dst)
        cur = dst
    return outs


def _pool_fwd(h, x2, pw, pb, ps, gn, tm, tps, comm=None):
    T, D = h.shape
    ng, cg, _ = pw.shape
    hp = HALO_POOL

    def body(h_ref, hh_ref, x_ref, pw_ref, pb_ref, ps_ref, gn_ref, pl_ref, xo_ref, hb_ref, hbt_ref, hext, buf_a, buf_b):
        i = pl.program_id(0)
        hext[0:hp, :] = jnp.where(i % tps == 0, 0.0, hh_ref[...])
        hext[hp:hp + tm, :] = h_ref[...]
        sums = _window_sums(hext, buf_a, buf_b, cg, True)
        for g, w in enumerate(POOL_WINDOWS):
            cs = slice(g * cg, (g + 1) * cg)
            pl_ref[:, cs] = (sums[g][hp:hp + tm, cs] / _pool_counts(i, tps, tm, w) - h_ref[:, cs]).astype(BF16)

        pv = pl_ref[...]
        mixed = jnp.concatenate([_dot(pv[:, g * cg:(g + 1) * cg], pw_ref[g]) for g in range(ng)], axis=1)
        xo = x_ref[...] + ps_ref[...] * (mixed + pb_ref[...])
        xo_ref[...] = xo
        hb = _rms(xo, gn_ref[...])[0].astype(BF16)
        hb_ref[...] = hb
        hbt_ref[...] = hb.T

    return _pcall(body, name="pool_fwd", grid=(T // tm,),
                  in_specs=[_rows(tm, D), _prev_halo(tm, hp, D), _rows(tm, D), _full(pw.shape), _full((1, D)), _full((1, D)),
                            _full((1, D))],
                  out_specs=[_rows(tm, D), _rows(tm, D), _rows(tm, D), _cols(D, tm)],
                  out_shape=[S_((T, D), BF16), S_((T, D), F32), S_((T, D), BF16), S_((D, T), BF16)],
                  scratch=[pltpu.VMEM((tm + hp, D), F32)] * 3, args=(h, h, x2, pw, pb, ps, gn), comm=comm)


def _bwd_down(dx, up, wdw, bdw, wd, tm, tps, ffn_taps, name, comm=None):
    T, D = dx.shape
    F = up.shape[1] // 2
    hf = HALO_FFN
    nt = T // tm

    def body(dx_ref, up_ref, uph_ref, wdw_ref, bdw_ref, wd_ref, dac_ref, dgate_ref, wsum_ref, aext, realigned):
        i = pl.program_id(0)
        aext[0:hf, :] = jnp.where(i % tps == 0, 0.0, uph_ref[...])
        aext[hf:hf + tm, :] = up_ref[:, 0:F]
        taps = _ffn_taps(aext, realigned, tm, ffn_taps)
        acv = _ffn_conv(taps, wdw_ref, bdw_ref)
        dg = _dot_nt(dx_ref[...].astype(BF16), wd_ref[...])
        sg = _sig(acv)
        dgate_ref[...] = (dg * acv * sg).astype(BF16)
        dac = dg * up_ref[:, F:2 * F] * (sg * (1.0 + acv * (1.0 - sg)))
        dac_ref[...] = dac
        rows = [_colsum(dac * tap[...]) for tap in taps]
        rows += [_colsum(dac), jnp.zeros((8 - ffn_taps - 1, F), F32)]
        _accumulate(wsum_ref, jnp.concatenate(rows, axis=0), i == 0)

    return _pcall(body, name=name, grid=(nt,),
                  in_specs=[_rows(tm, D), _rows(tm, 2 * F), _prev_halo(tm, hf, F), _full(wdw.shape), _full((1, F)),
                            _full((F, D))],
                  out_specs=[_rows(tm, F), _rows(tm, F, col=1), _acc_spec(8, F)],
                  out_shape=[S_((T, F), F32), S_((T, 2 * F), BF16), S_((8, F), F32)],
                  scratch=[pltpu.VMEM((tm + hf, F), F32), pltpu.VMEM((ffn_taps - 1, tm, F), F32)],
                  args=(dx, up, up, wdw, bdw, wd), comm=comm)


def _bwd_up(dac, dup, wg, wdw, xin, gn, dx, tm, tps, ffn_taps, name, comm=None):
    T, D = xin.shape
    F = dac.shape[1]
    ns, _, nc = wg.shape
    hf = HALO_FFN
    nt = T // tm

    def body(dac_ref, dach_ref, dgate_ref, wg_ref, wdw_ref, x_ref, gn_ref, dx_ref, dact_ref, dxo_ref, dxb_ref, dgn_ref, dext):
        i = pl.program_id(0)
        dext[0:tm, :] = dac_ref[...]
        dext[tm:tm + hf, :] = jnp.where(i % tps == tps - 1, 0.0, dach_ref[...])
        dact = wdw_ref[ffn_taps - 1:ffn_taps, :] * dac_ref[...]
        for k in range(ffn_taps - 1):
            dact = dact + wdw_ref[k:k + 1, :] * dext[pl.ds(ffn_taps - 1 - k, tm), :]
        dact_ref[...] = dact.astype(BF16)

        dh = jnp.zeros((tm, D), F32)
        for s in range(ns):
            src = dact_ref if s < ns // 2 else dgate_ref
            o = (s % (ns // 2)) * nc
            dh = dh + _dot_nt(src[:, o:o + nc], wg_ref[s])
        _, xh, r = _rms(x_ref[...], gn_ref[...])
        dxn, dg = _rms_bwd(dh, xh, r, gn_ref[...])
        dxo = dx_ref[...] + dxn
        dxo_ref[...] = dxo
        dxb_ref[...] = dxo.astype(BF16)
        _accumulate(dgn_ref, dg, i == 0)

    return _pcall(body, name=name, grid=(nt,),
                  in_specs=[_rows(tm, F), _next_halo(tm, hf, F, T), _rows(tm, F, col=1), _full(wg.shape), _full(wdw.shape),
                            _rows(tm, D), _full((1, D)), _rows(tm, D)],
                  out_specs=[_rows(tm, F), _rows(tm, D), _rows(tm, D), _acc_spec(1, D)],
                  out_shape=[S_((T, 2 * F), BF16), S_((T, D), F32), S_((T, D), BF16), S_((1, D), F32)],
                  scratch=[pltpu.VMEM((tm + hf, F), F32)],
                  aliases={2: 0}, args=(dac, dac, dup, wg, wdw, xin, gn, dx), comm=comm)


def _mm_tn(xt, dy, bk, bn, shard_major, name, comm=None):
    K, T = xt.shape
    N = dy.shape[1]

    def body(x_ref, dy_ref, o_ref):
        o_ref[...] = _dot(x_ref[...], dy_ref[...]).astype(o_ref.dtype)

    if shard_major:
        out_spec = pl.BlockSpec((None, bk, bn), lambda i, j: (j, i, 0))
        out_shape = S_((N // bn, K, bn), BF16)
    else:
        out_spec = pl.BlockSpec((bk, bn), lambda i, j: (i, j))
        out_shape = S_((K, N), BF16)
    return _pcall(body, name=name, grid=(K // bk, N // bn),
                  in_specs=[pl.BlockSpec((bk, T), lambda i, j: (i, 0)), pl.BlockSpec((T, bn), lambda i, j: (0, j))],
                  out_specs=out_spec, out_shape=out_shape, args=(xt, dy), comm=comm)


def _pool_bwd(dx3, plb, x2, pw, pb, ps, gn, tm, tps, comm=None):
    T, D = x2.shape
    ng, cg, _ = pw.shape
    hp = HALO_POOL
    nt = T // tm

    def body(do_ref, doh_ref, pl_ref, x_ref, pw_ref, pb_ref, ps_ref, gn_ref, dxo_ref, dxb_ref, dpw_ref, sm_ref, qext, buf_a,
             buf_b, dh_s):
        i = pl.program_id(0)
        do = do_ref[...]
        dm = do * ps_ref[...]
        dmh = jnp.where(i % tps == tps - 1, 0.0, doh_ref[...]) * ps_ref[...]
        pv = pl_ref[...]
        mixed = jnp.concatenate([_dot(pv[:, g * cg:(g + 1) * cg], pw_ref[g]) for g in range(ng)], axis=1)
        dscale = _colsum(do * (mixed + pb_ref[...]))
        dbias = _colsum(dm)
        dmb = dm.astype(BF16)
        dmhb = dmh.astype(BF16)
        for g, w in enumerate(POOL_WINDOWS):
            cs = slice(g * cg, (g + 1) * cg)
            _accumulate(dpw_ref.at[g], _dot_tn(pv[:, cs], dmb[:, cs]), i == 0)
            dpo = _dot_nt(dmb[:, cs], pw_ref[g])
            dh_s[:, cs] = dpo
            qext[0:tm, cs] = dpo / _pool_counts(i, tps, tm, w)
            qext[tm:tm + hp, cs] = _dot_nt(dmhb[:, cs], pw_ref[g]) * (1.0 / w)
        sums = _window_sums(qext, buf_a, buf_b, cg, False)
        for g in range(ng):
            cs = slice(g * cg, (g + 1) * cg)
            dh_s[:, cs] = sums[g][0:tm, cs] - dh_s[:, cs]

        _, xh, r = _rms(x_ref[...], gn_ref[...])
        dxn, dg = _rms_bwd(dh_s[...], xh, r, gn_ref[...])
        dxo = do + dxn
        dxo_ref[...] = dxo
        dxb_ref[...] = dxo.astype(BF16)
        sm = jnp.concatenate([dscale, dbias, dg, jnp.zeros((5, D), F32)], axis=0)
        _accumulate(sm_ref, sm, i == 0)

    return _pcall(body, name="pool_bwd", grid=(nt,),
                  in_specs=[_rows(tm, D), _next_halo(tm, hp, D, T), _rows(tm, D), _rows(tm, D), _full(pw.shape), _full((1, D)),
                            _full((1, D)), _full((1, D))],
                  out_specs=[_rows(tm, D), _rows(tm, D), pl.BlockSpec((ng, cg, cg), lambda i: (0, 0, 0)), _acc_spec(8, D)],
                  out_shape=[S_((T, D), F32), S_((T, D), BF16), S_((ng, cg, cg), F32), S_((8, D), F32)],
                  scratch=[pltpu.VMEM((tm + hp, D), F32)] * 3 + [pltpu.VMEM((tm, D), F32)],
                  args=(dx3, dx3, plb, x2, pw, pb, ps, gn), comm=comm)


def _conv_bwd1(dx1, c, w2, lng, lnb, tm, comm=None):
    T, D = dx1.shape

    def body(do_ref, c_ref, w2_ref, lng_ref, lnb_ref, dc_ref, sm_ref):
        i = pl.program_id(0)
        do = do_ref[...]
        ds = _dot_nt(do.astype(BF16), w2_ref[...])
        cv = c_ref[...]
        xc = cv - jnp.mean(cv, axis=-1, keepdims=True)
        rstd = lax.rsqrt(jnp.mean(xc * xc, axis=-1, keepdims=True) + LN_EPS)
        nh = xc * rstd
        n = nh * lng_ref[...] + lnb_ref[...]
        sg = _sig(n)
        dn = ds * (sg * (1.0 + n * (1.0 - sg)))
        dnh = dn * lng_ref[...]
        dc = rstd * (dnh - jnp.mean(dnh, axis=-1, keepdims=True) - nh * jnp.mean(dnh * nh, axis=-1, keepdims=True))
        dc_ref[...] = dc
        sm = jnp.concatenate([_colsum(dn * nh), _colsum(dn), _colsum(do), _colsum(dc), jnp.zeros((4, D), F32)], axis=0)
        _accumulate(sm_ref, sm, i == 0)

    return _pcall(body, name="conv_bwd1", grid=(T // tm,),
                  in_specs=[_rows(tm, D), _rows(tm, D), _full((D, D)), _full((1, D)), _full((1, D))],
                  out_specs=[_rows(tm, D), _acc_spec(8, D)],
                  out_shape=[S_((T, D), F32), S_((8, D), F32)], args=(dx1, c, w2, lng, lnb), comm=comm)


def _conv_bwd2(dc, a, x, w1g, wdw, gn, dx1, tm, tps, taps, comm=None):
    T, D = x.shape
    ns, _, nc = w1g.shape
    hc = HALO_CONV
    nt = T // tm
    tp = wdw.shape[0]

    def body(dc_ref, dch_ref, a_ref, ah_ref, x_ref, w_ref, wdw_ref, gn_ref, dx1_ref,
             gx_ref, dab_ref, dw_ref, db1_ref, dgn_ref, uext, dext, shifted, du_s, wacc):
        i = pl.program_id(0)
        ah = ah_ref[...]
        uext[0:hc, :] = jnp.where(i % tps == 0, 0.0, ah[:, :D] * _sig(ah[:, D:]))
        av = a_ref[...]
        sg2 = _sig(av[:, D:])
        uext[hc:hc + tm, :] = av[:, :D] * sg2
        dext[0:tm, :] = dc_ref[...]
        dext[tm:tm + hc, :] = jnp.where(i % tps == tps - 1, 0.0, dch_ref[...])

        @pl.when(i == 0)
        def _():
            wacc[...] = jnp.zeros_like(wacc)

        def wgrad(r0, src, ks, first):
            d = dc_ref[pl.ds(r0, CHUNK), :]
            for k, o in ks:
                wacc[8 * k:8 * k + 8, :] += _fold8(d * src[pl.ds(pl.multiple_of(r0 + o, 8), CHUNK), :])
        _by_sublane_shift(uext, shifted, tm, [hc - (taps - 1) + k for k in range(taps)], wgrad)

        def convt(r0, src, ks, first):
            acc = jnp.zeros((CHUNK, D), F32) if first else du_s[pl.ds(r0, CHUNK), :]
            for k, o in ks:
                acc = acc + wdw_ref[k:k + 1, :] * src[pl.ds(pl.multiple_of(r0 + o, 8), CHUNK), :]
            du_s[pl.ds(r0, CHUNK), :] = acc
        _by_sublane_shift(dext, shifted, tm, [taps - 1 - k for k in range(taps)], convt)

        du = du_s[...]
        da1 = du * sg2
        da2 = du * av[:, :D] * (sg2 * (1.0 - sg2))
        dab_ref[:, 0:D] = da1.astype(BF16)
        dab_ref[:, D:2 * D] = da2.astype(BF16)
        _accumulate(db1_ref, jnp.concatenate([_colsum(da1), _colsum(da2)], axis=0), i == 0)
        dh = jnp.zeros((tm, D), F32)
        for s in range(ns):
            dh = dh + _dot_nt(dab_ref[:, s * nc:(s + 1) * nc], w_ref[s])
        _, xh, r = _rms(x_ref[...], gn_ref[...])
        dxn, dg = _rms_bwd(dh, xh, r, gn_ref[...])
        gx_ref[...] = dx1_ref[...] + dxn
        _accumulate(dgn_ref, dg, i == 0)

        @pl.when(i == nt - 1)
        def _():
            dw_ref[...] = jnp.zeros_like(dw_ref)
            for k in range(taps):
                dw_ref[k:k + 1, :] = _colsum(wacc[8 * k:8 * k + 8, :])

    return _pcall(body, name="conv_bwd2", grid=(nt,),
                  in_specs=[_rows(tm, D), _next_halo(tm, hc, D, T), _rows(tm, 2 * D), _prev_halo(tm, hc, 2 * D), _rows(tm, D),
                            _full(w1g.shape), _full(wdw.shape), _full((1, D)), _rows(tm, D)],
                  out_specs=[_rows(tm, D), _rows(tm, 2 * D), _acc_spec(tp, D), _acc_spec(2, D), _acc_spec(1, D)],
                  out_shape=[S_((T, D), F32), S_((T, 2 * D), BF16), S_((tp, D), F32), S_((2, D), F32), S_((1, D), F32)],
                  scratch=[pltpu.VMEM((tm + hc, D), F32), pltpu.VMEM((tm + hc, D), F32), pltpu.VMEM((tm + hc - 8, D), F32),
                           pltpu.VMEM((tm, D), F32), pltpu.VMEM((8 * taps, D), F32)],
                  args=(dc, dc, a, a, x, w1g, wdw, gn, dx1), comm=comm)


def _place():
    x, y, c = lax.axis_index("x"), lax.axis_index("y"), lax.axis_index("c")
    chips = [(1 - x, y), (x, 1 - y), (1 - x, 1 - y)]
    return x, y, c, chips


class _GatherComm:
    def __init__(self, slots):
        n = len(slots)
        self.inputs = list(slots)
        self.out_shape = [S_(s.shape, s.dtype) for s in slots]
        self.aliases = {i: i for i in range(n)}
        self.sems = [pltpu.SemaphoreType.DMA((n, 3))] * 4
        self.halves = [s.shape[1] // 2 for s in slots]
        self.paired = [s.shape[0] == N_CHIPS // 2 for s in slots]
        self.widths = [s.shape[2] // 2 if p else s.shape[2] for s, p in zip(slots, self.paired)]

    def _copies(self, outs, sems):
        s_ici, r_ici, s_d2d, r_d2d = sems
        x, y, c, chips = _place()
        peers = [(j, chip, 2 * chip[0] + chip[1]) for j, chip in enumerate(chips)]

        def block(i, chip, h):
            rows = pl.ds(pl.multiple_of(h * self.halves[i], 8), self.halves[i])
            if self.paired[i]:
                return outs[i].at[chip // 2, rows, pl.ds(pl.multiple_of((chip % 2) * self.widths[i], LANES), self.widths[i])]
            return outs[i].at[chip, rows]

        def ici(i, j, chip, to):
            blk = block(i, chip, c)
            return pltpu.make_async_remote_copy(src_ref=blk, dst_ref=blk, send_sem=s_ici.at[i, j], recv_sem=r_ici.at[i, j],
                                                device_id=to, device_id_type=MESH)

        def d2d(i, j, chip, h):
            blk = block(i, chip, h)
            return pltpu.make_async_remote_copy(src_ref=blk, dst_ref=blk, send_sem=s_d2d.at[i, j], recv_sem=r_d2d.at[i, j],
                                                device_id=(x, y, 1 - c), device_id_type=MESH)
        return (x, y, c, 2 * x + y), peers, ici, d2d

    def before(self, cin, outs, sems, step, total):
        (x, y, c, q), peers, ici, d2d = self._copies(outs, sems)

        @pl.when(step == 0)
        def _():
            for i in range(len(outs)):
                for j, chip, _ in peers:
                    ici(i, j, q, (*chip, c)).start()

    def after(self, cin, outs, sems, step, total):
        (x, y, c, q), peers, ici, d2d = self._copies(outs, sems)

        @pl.when(step == max(total - 2, 0))
        def _():
            for i in range(len(outs)):
                for j, chip, qj in peers:
                    ici(i, j, qj, (x, y, c)).wait_recv()
                    d2d(i, j, qj, c).start()

        @pl.when(step == total - 1)
        def _():
            for i in range(len(outs)):
                for j, chip, qj in peers:
                    d2d(i, j, qj, 1 - c).wait_recv()
            for i in range(len(outs)):
                for j, chip, qj in peers:
                    ici(i, j, q, (*chip, c)).wait_send()
                    d2d(i, j, qj, c).wait_send()


class _ExchangeComm:
    def __init__(self, sums):
        n = len(sums)
        self.inputs = list(sums)
        self.out_shape = [S_((3,) + s.shape[1:], s.dtype) for s in sums]
        self.aliases = {}
        self.sems = [pltpu.SemaphoreType.DMA((n, 3))] * 2

    def _copies(self, srcs, outs, sems):
        s_sem, r_sem = sems
        x, y, c, chips = _place()
        return [pltpu.make_async_remote_copy(src_ref=srcs[i].at[2 * chip[0] + chip[1]], dst_ref=outs[i].at[j],
                                             send_sem=s_sem.at[i, j], recv_sem=r_sem.at[i, j],
                                             device_id=(*chip, c), device_id_type=MESH)
                for i in range(len(srcs)) for j, chip in enumerate(chips)]

    def before(self, srcs, outs, sems, step, total):
        @pl.when(step == 0)
        def _():
            for cp in self._copies(srcs, outs, sems):
                cp.start()

    def after(self, srcs, outs, sems, step, total):
        @pl.when(step == total - 1)
        def _():
            for cp in self._copies(srcs, outs, sems):
                cp.wait()


def _own_slots(specs):
    n = len(specs)
    shapes, blocks, count = [], [], []
    for a, (w, layer, dtype, pairs) in enumerate(specs):
        _, r, c = w.shape
        br = _row_block(r, c, 2 * BLOCK_BYTES)
        shapes.append(S_((N_CHIPS // 2, r, 2 * c) if pairs else (N_CHIPS, r, c), dtype))
        blocks += [(a, k, k * br, br) for k in range(r // br)]
        count.append(r // br)
    gather = _GatherComm(shapes[:1])
    n_sems = len(gather.sems)

    def body(*refs):
        srcs, outs = refs[:n], refs[n:2 * n]
        inb, outb = refs[2 * n:3 * n], refs[3 * n:4 * n]
        lsem, ssem = refs[4 * n], refs[4 * n + 1]
        gsems = refs[4 * n + 2:4 * n + 2 + n_sems]
        x, y, c, _ = _place()
        q = 2 * x + y

        def load(a, k, r0, br):
            return pltpu.make_async_copy(srcs[a].at[specs[a][1], pl.ds(r0, br)], inb[a].at[k % 2], lsem.at[a, k % 2])

        def store(a, k, r0, br):
            if specs[a][3]:
                wc = specs[a][0].shape[2]
                dst = outs[a].at[q // 2, pl.ds(r0, br), pl.ds(pl.multiple_of((q % 2) * wc, LANES), wc)]
            else:
                dst = outs[a].at[q, pl.ds(r0, br)]
            return pltpu.make_async_copy(outb[a].at[k % 2], dst, ssem.at[a, k % 2])

        load(*blocks[0]).start()
        for b, (a, k, r0, br) in enumerate(blocks):
            if b + 1 < len(blocks):
                load(*blocks[b + 1]).start()
            load(a, k, r0, br).wait()
            if k >= 2:
                store(a, k - 2, r0 - 2 * br, br).wait()
            outb[a][k % 2] = inb[a][k % 2].astype(specs[a][2])
            store(a, k, r0, br).start()
            if a == 0 and k == count[0] - 1:
                for kk in range(max(k - 1, 0), k + 1):
                    store(0, kk, kk * br, br).wait()
                gather.before(None, outs[:1], gsems, 0, 1)
        for a in range(1, n):
            br = blocks[sum(count[:a])][3]
            for kk in range(max(count[a] - 2, 0), count[a]):
                store(a, kk, kk * br, br).wait()
        gather.after(None, outs[:1], gsems, 0, 1)

    any_spec = pl.BlockSpec(memory_space=pl.ANY)
    scratch = ([pltpu.VMEM((2, blocks[sum(count[:a])][3], specs[a][0].shape[2]), F32) for a in range(n)]
               + [pltpu.VMEM((2, blocks[sum(count[:a])][3], specs[a][0].shape[2]), specs[a][2]) for a in range(n)]
               + [pltpu.SemaphoreType.DMA((n, 2)), pltpu.SemaphoreType.DMA((n, 2))] + list(gather.sems))
    outs = pl.pallas_call(body, name="own_slots", in_specs=[any_spec] * n, out_specs=[any_spec] * n, out_shape=shapes,
                          scratch_shapes=scratch,
                          compiler_params=pltpu.CompilerParams(vmem_limit_bytes=VMEM_LIMIT_BYTES))(*[s[0] for s in specs])
    return list(outs)


def _comm_only(comm, name):
    def body():
        pass
    outs = _pcall(body, name=name, grid=(1,), in_specs=[], out_specs=[], out_shape=[], args=[], comm=comm)
    return list(outs)


def _pair_exchange(parts, name):
    n = len(parts)

    def body(*refs):
        srcs, outs = refs[:n], refs[n:2 * n]
        s_sem, r_sem = refs[2 * n:]
        x, y, c, _ = _place()
        cps = []
        for i in range(n):
            for s in range(N_CHIPS):
                cp = pltpu.make_async_remote_copy(src_ref=srcs[i].at[s, 1 - c], dst_ref=outs[i].at[s], send_sem=s_sem.at[i, s],
                                                  recv_sem=r_sem.at[i, s], device_id=(x, y, 1 - c), device_id_type=MESH)
                cp.start()
                cps.append(cp)
        for cp in cps:
            cp.wait()

    any_spec = pl.BlockSpec(memory_space=pl.ANY)
    outs = pl.pallas_call(
        body, name=name, in_specs=[any_spec] * n, out_specs=[any_spec] * n,
        out_shape=[S_((N_CHIPS,) + p.shape[2:], p.dtype) for p in parts],
        scratch_shapes=[pltpu.SemaphoreType.DMA((n, N_CHIPS))] * 2,
    )(*parts)
    return list(outs)


class _PairComm:
    def __init__(self, parts):
        n = len(parts)
        self.inputs = list(parts)
        self.out_shape = [S_((N_CHIPS,) + p.shape[2:], p.dtype) for p in parts]
        self.aliases = {}
        self.sems = [pltpu.SemaphoreType.DMA((n, N_CHIPS))] * 2

    def _copies(self, srcs, outs, sems):
        s_sem, r_sem = sems
        x, y, c, _ = _place()
        return [pltpu.make_async_remote_copy(src_ref=srcs[i].at[s, 1 - c], dst_ref=outs[i].at[s], send_sem=s_sem.at[i, s],
                                             recv_sem=r_sem.at[i, s], device_id=(x, y, 1 - c), device_id_type=MESH)
                for i in range(len(srcs)) for s in range(N_CHIPS)]

    def before(self, srcs, outs, sems, step, total):
        @pl.when(step == 0)
        def _():
            for cp in self._copies(srcs, outs, sems):
                cp.start()

    def after(self, srcs, outs, sems, step, total):
        @pl.when(step == total - 1)
        def _():
            for cp in self._copies(srcs, outs, sems):
                cp.wait()


class _ShareComm:
    def __init__(self, bufs):
        n = len(bufs)
        self.inputs = list(bufs)
        self.out_shape = [S_(b.shape, b.dtype) for b in bufs]
        self.aliases = {i: i for i in range(n)}
        self.sems = [pltpu.SemaphoreType.DMA((n, max(b.shape[0] for b in bufs)))] * 2
        self.layers = [b.shape[0] for b in bufs]

    def _copies(self, outs, sems):
        s_sem, r_sem = sems
        x, y, c, _ = _place()
        cps = []
        for i, nl in enumerate(self.layers):
            for l in range(nl):
                blk = outs[i].at[l, c]
                cps.append(pltpu.make_async_remote_copy(src_ref=blk, dst_ref=blk, send_sem=s_sem.at[i, l],
                                                        recv_sem=r_sem.at[i, l], device_id=(x, y, 1 - c),
                                                        device_id_type=MESH))
        return cps

    def before(self, cin, outs, sems, step, total):
        @pl.when(step == 0)
        def _():
            for cp in self._copies(outs, sems):
                cp.start()

    def after(self, cin, outs, sems, step, total):
        @pl.when(step == total - 1)
        def _():
            for cp in self._copies(outs, sems):
                cp.wait()


class _Both:
    def __init__(self, a, b):
        self.parts = (a, b)
        self.inputs = a.inputs + b.inputs
        self.out_shape = a.out_shape + b.out_shape
        self.aliases = dict(a.aliases)
        self.aliases.update({len(a.inputs) + i: len(a.out_shape) + o for i, o in b.aliases.items()})
        self.sems = a.sems + b.sems

    def _each(self, phase, cin, cout, sems, step, total):
        i = o = s = 0
        for p in self.parts:
            ni, no, ns = len(p.inputs), len(p.out_shape), len(p.sems)
            getattr(p, phase)(cin[i:i + ni], cout[o:o + no], sems[s:s + ns], step, total)
            i, o, s = i + ni, o + no, s + ns

    def before(self, cin, cout, sems, step, total):
        self._each("before", cin, cout, sems, step, total)

    def after(self, cin, cout, sems, step, total):
        self._each("after", cin, cout, sems, step, total)


def _small_allreduce(groups, comm):
    n = len(groups)
    nci, nco, ncs = len(comm.inputs), len(comm.out_shape), len(comm.sems)
    arrays = []
    for grp in groups:
        for p in grp:
            if not isinstance(p, int) and not any(p[0] is a for a in arrays):
                arrays.append(p[0])
    shapes = []
    for grp in groups:
        rows = sum(p if isinstance(p, int) else p[2] - p[1] for p in grp)
        width = next(p[0].shape[1] for p in grp if not isinstance(p, int))
        assert rows % 8 == 0
        shapes.append((rows, width))
    na = len(arrays)

    def body(*refs):
        a_refs, cin = refs[:na], refs[na:na + nci]
        o_refs, cout = refs[na + nci:na + nci + n], refs[na + nci + n:na + nci + n + nco]
        rest = refs[na + nci + n + nco:]
        v_refs, sibs, css, gots = rest[:n], rest[n:2 * n], rest[2 * n:3 * n], rest[3 * n:4 * n]
        s_sem, r_sem = rest[4 * n:4 * n + 2]
        csems = rest[4 * n + 2:]
        for i, grp in enumerate(groups):
            o = 0
            for p in grp:
                if isinstance(p, int):
                    v_refs[i][o:o + p, :] = jnp.zeros((p, shapes[i][1]), F32)
                    o += p
                else:
                    k = next(t for t, a in enumerate(arrays) if a is p[0])
                    v_refs[i][o:o + p[2] - p[1], :] = a_refs[k][p[1]:p[2], :]
                    o += p[2] - p[1]
        x, y, c, chips = _place()
        q = 2 * x + y
        to_sib = [pltpu.make_async_remote_copy(src_ref=v_refs[i], dst_ref=sibs[i], send_sem=s_sem.at[i, 3], recv_sem=r_sem.at[i, 3],
                                               device_id=(x, y, 1 - c), device_id_type=MESH) for i in range(n)]
        for cp in to_sib:
            cp.start()
        cps = []
        for i in range(n):
            to_sib[i].wait()
            mine, other = v_refs[i][...], sibs[i][...]
            css[i][...] = jnp.where(c == 0, mine, other) + jnp.where(c == 0, other, mine)
            for j, chip in enumerate(chips):
                cp = pltpu.make_async_remote_copy(src_ref=css[i], dst_ref=gots[i].at[j], send_sem=s_sem.at[i, j],
                                                  recv_sem=r_sem.at[i, j], device_id=(*chip, c), device_id_type=MESH)
                cp.start()
                cps.append(cp)
        comm.before(cin, cout, csems, 0, 1)
        for cp in cps:
            cp.wait()
        flips = [2, 1, 3]
        for i in range(n):
            total = None
            for k in range(N_CHIPS):
                d = q ^ k
                term = jnp.where(d == 0, css[i][...], 0.0)
                for j in range(3):
                    term = jnp.where(d == flips[j], gots[i][j], term)
                total = term if total is None else total + term
            o_refs[i][...] = total
        comm.after(cin, cout, csems, 0, 1)

    vm, any_spec = pl.BlockSpec(memory_space=pltpu.VMEM), pl.BlockSpec(memory_space=pl.ANY)
    outs = pl.pallas_call(
        body, name="small_allreduce", in_specs=[vm] * na + [any_spec] * nci, out_specs=[vm] * n + [any_spec] * nco,
        out_shape=[S_(s, F32) for s in shapes] + list(comm.out_shape),
        scratch_shapes=[pltpu.VMEM(s, F32) for s in shapes] * 3 + [pltpu.VMEM((3,) + s, F32) for s in shapes]
        + [pltpu.SemaphoreType.DMA((n, 4)), pltpu.SemaphoreType.DMA((n, 4))] + list(comm.sems),
    )(*arrays, *comm.inputs)
    return list(outs[:n]), list(outs[n:])


def _row_block(r, c, target_bytes=BLOCK_BYTES):
    br = r
    while br % (2 * BF16_ROWS) == 0 and br * c * 4 > target_bytes:
        br //= 2
    return br


def _chip_sum(part, sib, ids, name):
    _, _, r, c = part.shape
    br = _row_block(r, c, 4 * BLOCK_BYTES)

    def body(ids_ref, p_ref, s_ref, o_ref):
        o_ref[...] = (p_ref[...].astype(F32) + s_ref[...].astype(F32)).astype(BF16)

    return _pcall(body, name=name, grid=(N_CHIPS, r // br), prefetch=1,
                  in_specs=[pl.BlockSpec((None, None, br, c), lambda s, i, ids: (s, ids[1], i, 0)),
                            pl.BlockSpec((None, br, c), lambda s, i, ids: (s, i, 0))],
                  out_specs=pl.BlockSpec((None, br, c), lambda s, i, ids: (s, i, 0)),
                  out_shape=S_((N_CHIPS, r, c), BF16), args=(ids, part, sib))


def _final_sum(part, sib, got, ids, name, layer=0, n_layers=1, buf=None):
    _, r, c = got.shape
    br = _row_block(r, c, 2 * BLOCK_BYTES)

    def body(ids_ref, p_ref, s_ref, got_ref, *rest):
        v = p_ref[...].astype(F32) + s_ref[...].astype(F32)
        for j in range(3):
            v = v + got_ref[j].astype(F32)
        rest[-1][...] = v

    in_specs = [pl.BlockSpec((None, None, br, c), lambda i, ids: (ids[0], ids[1], i, 0)),
                pl.BlockSpec((None, br, c), lambda i, ids: (ids[0], i, 0)),
                pl.BlockSpec((3, br, c), lambda i, ids: (0, i, 0))]
    args = [ids, part, sib, got]
    if buf is not None:
        in_specs.append(pl.BlockSpec(memory_space=pl.ANY))
        args.append(buf)
    return _pcall(body, name=name, grid=(r // br,), prefetch=1, in_specs=in_specs,
                  out_specs=pl.BlockSpec((None, None, br, c), lambda i, ids: (layer, ids[1], i, 0)),
                  out_shape=S_((n_layers, 2, r, c), F32), args=args, aliases={4: 0} if buf is not None else None)


def _adam_math(w, g, m, v):
    c1 = 1.0 / (1.0 - ADAM_B1 ** ADAM_STEP)
    c2 = 1.0 / (1.0 - ADAM_B2 ** ADAM_STEP)
    mn = ADAM_B1 * m + (1.0 - ADAM_B1) * g
    vn = ADAM_B2 * v + (1.0 - ADAM_B2) * (g * g)
    return -ADAM_LR * ((mn * c1) / (jnp.sqrt(vn * c2) + ADAM_EPS) + ADAM_WD * w), mn, vn


def _adamw(w, g, m, v, name):
    nl, r, c = w.shape
    br = _row_block(r, c)

    def body(w_ref, g_ref, m_ref, v_ref, go_ref, d_ref, mo_ref, vo_ref):
        gv = g_ref[...]
        go_ref[...] = gv
        d_ref[...], mo_ref[...], vo_ref[...] = _adam_math(w_ref[...], gv, m_ref[...], v_ref[...])

    blk = pl.BlockSpec((None, br, c), lambda l, i: (l, i, 0))
    return _pcall(body, name=name, grid=(nl, r // br), in_specs=[blk] * 4, out_specs=[blk] * 4,
                  out_shape=[S_((nl, r, c), F32)] * 4, args=(w, g, m, v))


def _adamw_small(rd, rf, sharded_g, ws, ms, vs, rows_d, rows_f, loss_row):
    n = len(ws)
    ns = len(sharded_g)
    rep = [k for k in range(n) if rows_d[k] is not None or rows_f[k] is not None]

    def body(*refs):
        rd_ref, rf_ref = refs[0], refs[1]
        sg = refs[2:2 + ns]
        w_refs, m_refs, v_refs = (refs[2 + ns + t * n:2 + ns + (t + 1) * n] for t in range(3))
        loss_ref, outs = refs[2 + ns + 3 * n], refs[3 + ns + 3 * n:]
        g_out, d_out, m_out, v_out = outs[:len(rep)], outs[len(rep):len(rep) + n], outs[len(rep) + n:len(rep) + 2 * n], \
            outs[len(rep) + 2 * n:]
        loss_ref[...] = jnp.sum(rd_ref[loss_row:loss_row + 1, :], axis=1, keepdims=True)
        si = 0
        for k in range(n):
            shape = w_refs[k].shape
            if rows_d[k] is not None or rows_f[k] is not None:
                src, rws = (rd_ref, rows_d[k]) if rows_d[k] is not None else (rf_ref, rows_f[k])
                axis = 0 if shape[0] == len(rws) else 1
                g = src[rws[0]:rws[0] + 1, :]
                if len(rws) > 1:
                    g = jnp.concatenate([src[r:r + 1, :] for r in rws], axis=axis)
                g_out[rep.index(k)][...] = g
            else:
                g = sg[si][...]
                si += 1
            d_out[k][...], m_out[k][...], v_out[k][...] = _adam_math(w_refs[k][...], g, m_refs[k][...], v_refs[k][...])

    vm = pl.BlockSpec(memory_space=pltpu.VMEM)
    shapes = [S_(w.shape, F32) for w in ws]
    outs = pl.pallas_call(
        body, name="adamw_small", in_specs=[vm] * (2 + ns + 3 * n), out_specs=[vm] * (1 + len(rep) + 3 * n),
        out_shape=[S_((1, 1), F32)] + [shapes[k] for k in rep] + shapes * 3,
    )(rd, rf, *sharded_g, *ws, *ms, *vs)
    g_rep = dict(zip(rep, outs[1:1 + len(rep)]))
    o = outs[1 + len(rep):]
    return outs[0], g_rep, o[:n], o[n:2 * n], o[2 * n:]


def _pack(pieces, rows):
    flat = jnp.concatenate([p.reshape(-1).astype(F32) for p in pieces])
    return jnp.pad(flat, (0, rows * LANES - flat.shape[0])).reshape(rows, LANES)


def _unpack(packed, shapes):
    flat = packed.reshape(-1)
    out, o = [], 0
    for s in shapes:
        n = 1
        for d in s:
            n *= d
        out.append(flat[o:o + n].reshape(s))
        o += n
    return out


def _rows_for(pieces_or_shapes):
    n = 0
    for p in pieces_or_shapes:
        k = 1
        for d in (p if isinstance(p, tuple) else p.shape):
            k *= d
        n += k
    return -(-n // (SUBLANES * LANES)) * SUBLANES


def kernel(x, norm_mix, norm_ffn, conv_w_pw1, conv_b_pw1, conv_w_dw, conv_b_dw, conv_ln_g, conv_ln_b, conv_w_pw2, conv_b_pw2, pool_w, pool_b, pool_scale, ffn_w_up, ffn_w_dw, ffn_b_dw, ffn_w_down, final_norm, loss_target, m_norm_mix, m_norm_ffn, m_conv_w_pw1, m_conv_b_pw1, m_conv_w_dw, m_conv_b_dw, m_conv_ln_g, m_conv_ln_b, m_conv_w_pw2, m_conv_b_pw2, m_pool_w, m_pool_b, m_pool_scale, m_ffn_w_up, m_ffn_w_dw, m_ffn_b_dw, m_ffn_w_down, m_final_norm, v_norm_mix, v_norm_ffn, v_conv_w_pw1, v_conv_b_pw1, v_conv_w_dw, v_conv_b_dw, v_conv_ln_g, v_conv_ln_b, v_conv_w_pw2, v_conv_b_pw2, v_pool_w, v_pool_b, v_pool_scale, v_ffn_w_up, v_ffn_w_dw, v_ffn_b_dw, v_ffn_w_down, v_final_norm):
    nb, seq, D = x.shape
    T = nb * seq
    F = ffn_w_down.shape[1] * N_CHIPS
    taps = conv_w_dw.shape[1]
    ffn_taps = ffn_w_dw.shape[1]
    ng = pool_w.shape[1]
    cg = pool_w.shape[3]
    dsh = D // N_CHIPS
    fsh = F // N_CHIPS
    assert taps - 1 <= HALO_CONV and ffn_taps - 1 <= HALO_FFN and max(POOL_WINDOWS) <= HALO_POOL
    tm = min(TILE_D, seq)
    tf = min(TILE_F, seq)
    assert seq % tm == 0 and seq % tf == 0 and tm % HALO_CONV == 0 and tf % CHUNK == 0
    tps, tpf = seq // tm, seq // tf

    xi, yi, ci = lax.axis_index("x"), lax.axis_index("y"), lax.axis_index("c")
    qi = 2 * xi + yi
    ids = jnp.stack([qi, ci]).astype(jnp.int32)
    x2d = x.reshape(T, D)
    tg2d = loss_target.reshape(T, D)

    small_sharded = [conv_w_dw[0], ffn_w_dw, pool_b, pool_scale]
    srows = 2 * _rows_for(small_sharded)
    w1g, s_pw2, s_small, s_pool, s_up0, s_up1, s_down0, s_down1 = _own_slots([
        (conv_w_pw1, 0, BF16, False), (conv_w_pw2, 0, BF16, False), (_pack(small_sharded, srows)[None], 0, F32, False),
        (pool_w.reshape(1, ng * (cg // N_CHIPS), cg), 0, BF16, False), (ffn_w_up, 0, BF16, True), (ffn_w_up, 1, BF16, True),
        (ffn_w_down, 0, BF16, False), (ffn_w_down, 1, BF16, False)])
    s_up, s_down = [s_up0, s_up1], [s_down0, s_down1]
    hb0t, a, w2g, smallg, pwg = _conv_in(x2d, norm_mix[0:1], w1g, conv_b_pw1, tm, comm=_GatherComm([s_pw2, s_small, s_pool]))
    w2f = w2g.reshape(D, D)
    sm_parts = [_unpack(smallg[s], [(taps, dsh), (2, ffn_taps, fsh), (1, dsh), (1, dsh)]) for s in range(N_CHIPS)]
    wdw_f = jnp.concatenate([p[0] for p in sm_parts], axis=1)
    wdw_f = jnp.pad(wdw_f, ((0, HALO_CONV - taps), (0, 0)))
    fdw_f = jnp.concatenate([p[1] for p in sm_parts], axis=2)
    fdw_f = jnp.pad(fdw_f, ((0, 0), (0, 8 - ffn_taps), (0, 0)))
    pb_f = jnp.concatenate([p[2] for p in sm_parts], axis=1)
    ps_f = jnp.concatenate([p[3] for p in sm_parts], axis=1)

    c, sbt, x1, hb1, hb1t, wup0, wd0 = _conv_mid(a, x2d, wdw_f, conv_b_dw, conv_ln_g, conv_ln_b, w2f, conv_b_pw2,
                                                 norm_ffn[0:1], tm, tps, taps, comm=_GatherComm([s_up[0], s_down[0]]))
    up0, gb0t, x2, h2, wup1, wd1 = _ffn_fwd(hb1, x1, wup0, fdw_f[0], ffn_b_dw[0:1], wd0.reshape(F, D), norm_mix[1:2], tf, tpf,
                                            ffn_taps, "ffn_fwd0", comm=_GatherComm([s_up[1], s_down[1]]))
    pwf = pwg.reshape(N_CHIPS, ng, cg // N_CHIPS, cg).transpose(1, 0, 2, 3).reshape(ng, cg, cg)
    plb, x3, hb3, hb3t = _pool_fwd(h2, x2, pwf, pb_f, ps_f, norm_ffn[1:2], tm, tps)
    wup = [wup0, wup1]
    wdn = [wd0.reshape(F, D), wd1.reshape(F, D)]
    up1, gb1t, dx4, dx4b, loss_part, d_final = _ffn_fwd(hb3, x3, wup[1], fdw_f[1], ffn_b_dw[1:2], wdn[1], final_norm.reshape(1, D),
                                                  tf, tpf, ffn_taps, "ffn_fwd1", target=tg2d)

    fs = {}

    def pair(group):
        return list(group), list(group.values()), _PairComm(list(group.values()))

    def chip_sums(nms, parts, from_sib):
        for nm, p, s in zip(nms, parts, from_sib):
            fs[nm] = (p, s)
        return nms, _ExchangeComm([_chip_sum(p, s, ids, "chip_sum_" + nm) for nm, p, s in zip(nms, parts, from_sib)])

    def arrived(nms, got):
        for nm, g in zip(nms, got):
            fs[nm] = fs[nm] + (g,)

    dac1, dup1, fw1 = _bwd_down(dx4, up1, fdw_f[1], ffn_b_dw[1:2], wdn[1], tf, tpf, ffn_taps, "bwd_down1")
    p_down1 = _mm_tn(gb1t, dx4b, F // 4, D, False, "dw_down1")
    nms, parts, px = pair({"down1": p_down1.reshape(N_CHIPS, 2, fsh // 2, D)})
    dup1, dx3, _, dnf1, *sib = _bwd_up(dac1, dup1, wup[1], fdw_f[1], x3, norm_ffn[1:2], dx4, tf, tpf, ffn_taps, "bwd_up1", comm=px)
    nms, xchg = chip_sums(nms, parts, sib)
    p_up1, *got = _mm_tn(hb3t, dup1, D // 2, 2 * fsh, True, "dw_up1", comm=xchg)
    arrived(nms, got)
    nms, parts, px = pair({"up1": p_up1.reshape(N_CHIPS, 2, D // 2, 2 * fsh)})
    dx2, dx2b, dpw, pool_sm, *sib = _pool_bwd(dx3, plb, x2, pwf, pb_f, ps_f, norm_mix[1:2], tm, tps, comm=px)
    nms, xchg = chip_sums(nms, parts, sib)
    dac0, dup0, fw0, *got = _bwd_down(dx2, up0, fdw_f[0], ffn_b_dw[0:1], wdn[0], tf, tpf, ffn_taps, "bwd_down0", comm=xchg)
    arrived(nms, got)
    p_down0 = _mm_tn(gb0t, dx2b, F // 4, D, False, "dw_down0")
    nms, parts, px = pair({"down0": p_down0.reshape(N_CHIPS, 2, fsh // 2, D)})
    dup0, dx1, dx1b, dnf0, *sib = _bwd_up(dac0, dup0, wup[0], fdw_f[0], x1, norm_ffn[0:1], dx2, tf, tpf, ffn_taps, "bwd_up0", comm=px)
    nms, xchg = chip_sums(nms, parts, sib)
    p_up0, *got = _mm_tn(hb1t, dup0, D // 2, 2 * fsh, True, "dw_up0", comm=xchg)
    arrived(nms, got)
    csh = cg // N_CHIPS
    p_pool = dpw.reshape(2, ng // 2, N_CHIPS, csh, cg).transpose(2, 0, 1, 3, 4).reshape(N_CHIPS, 2, (ng // 2) * csh, cg)
    nms, parts, px = pair({"up0": p_up0.reshape(N_CHIPS, 2, D // 2, 2 * fsh), "pool": p_pool.astype(BF16)})
    dc, conv_sm, *sib = _conv_bwd1(dx1, c, w2f, conv_ln_g, conv_ln_b, tm, comm=px)
    nms, xchg = chip_sums(nms, parts, sib)
    p_pw2 = _mm_tn(sbt, dx1b, D, D, False, "dw_pw2")
    nms2, parts2, px = pair({"pw2": p_pw2.reshape(N_CHIPS, 2, dsh // 2, D)})
    grad_x, dab, dwdw, db1, dnm0, *both = _conv_bwd2(dc, a, x2d, w1g, wdw_f, norm_mix[0:1], dx1, tm, tps, taps,
                                                     comm=_Both(xchg, px))
    arrived(nms, both[:len(nms)])
    nms, xchg = chip_sums(nms2, parts2, both[len(nms):])
    p_pw1, *got = _mm_tn(hb0t, dab, D, 2 * dsh, True, "dw_pw1", comm=xchg)
    arrived(nms, got)
    nms, parts, _ = pair({"pw1": p_pw1.reshape(N_CHIPS, 2, D // 2, 2 * dsh)})
    nms, xchg = chip_sums(nms, parts, _pair_exchange(parts, "grad_pair_exchange_pw1"))
    (rd, rf), got = _small_allreduce([
        [(dnm0, 0, 1), (pool_sm, 2, 3), (dnf0, 0, 1), (dnf1, 0, 1), (db1, 0, 2), (conv_sm, 0, 4), (d_final, 0, 1),
         (pool_sm, 0, 2), (loss_part, 0, 1), 2, (dwdw, 0, dwdw.shape[0])],
        [(fw0, 0, 4), (fw1, 0, 4)]], xchg)
    arrived(nms, got)
    bufs = [_final_sum(*fs[nm], ids, "final_sum_" + nm) for nm in ("pw1", "pw2", "pool")]
    for nm in ("up", "down"):
        b = _final_sum(*fs[nm + "0"], ids, "final_sum_" + nm + "0", layer=0, n_layers=2)
        bufs.append(_final_sum(*fs[nm + "1"], ids, "final_sum_" + nm + "1", layer=1, n_layers=2, buf=b))
    shared = _comm_only(_ShareComm(bufs), "grad_share_halves")
    g_pw1, g_pw2, g_pool, g_up, g_down = [b.reshape(b.shape[0], 2 * b.shape[2], b.shape[3]) for b in shared]
    g_wdw = lax.dynamic_slice_in_dim(rd[16:16 + taps], qi * dsh, dsh, axis=1)
    g_ps = lax.dynamic_slice_in_dim(rd[11:12], qi * dsh, dsh, axis=1)
    g_pb = lax.dynamic_slice_in_dim(rd[12:13], qi * dsh, dsh, axis=1)
    g_fdw = lax.dynamic_slice_in_dim(rf.reshape(2, 4, F)[:, :ffn_taps], qi * fsh, fsh, axis=2).reshape(2 * ffn_taps, fsh)

    def big(w, g, m, v, nm):
        return tuple(t.reshape(w.shape) for t in _adamw(w.reshape(g.shape), g, m.reshape(g.shape), v.reshape(g.shape),
                                                        "adamw_" + nm))

    o_pw1 = big(conv_w_pw1, g_pw1, m_conv_w_pw1, v_conv_w_pw1, "pw1")
    o_pw2 = big(conv_w_pw2, g_pw2, m_conv_w_pw2, v_conv_w_pw2, "pw2")
    o_pool = big(pool_w, g_pool, m_pool_w, v_pool_w, "pool")
    o_up = big(ffn_w_up, g_up, m_ffn_w_up, v_ffn_w_up, "up")
    o_down = big(ffn_w_down, g_down, m_ffn_w_down, v_ffn_w_down, "down")

    snames = ["norm_mix", "norm_ffn", "b_pw1", "w_dw", "b_dw", "ln_g", "ln_b", "b_pw2", "pool_b", "pool_scale", "ffn_w_dw",
              "ffn_b_dw", "final_norm"]
    sw = [norm_mix, norm_ffn, conv_b_pw1, conv_w_dw, conv_b_dw, conv_ln_g, conv_ln_b, conv_b_pw2, pool_b, pool_scale,
          ffn_w_dw, ffn_b_dw, final_norm]
    smm = [m_norm_mix, m_norm_ffn, m_conv_b_pw1, m_conv_w_dw, m_conv_b_dw, m_conv_ln_g, m_conv_ln_b, m_conv_b_pw2, m_pool_b,
           m_pool_scale, m_ffn_w_dw, m_ffn_b_dw, m_final_norm]
    svv = [v_norm_mix, v_norm_ffn, v_conv_b_pw1, v_conv_w_dw, v_conv_b_dw, v_conv_ln_g, v_conv_ln_b, v_conv_b_pw2, v_pool_b,
           v_pool_scale, v_ffn_w_dw, v_ffn_b_dw, v_final_norm]
    sshapes = [tuple(w.shape) for w in sw]
    two_d = [(2, D), (2, D), (1, 2 * D), (taps, dsh), (1, D), (1, D), (1, D), (1, D), (1, dsh), (1, dsh), (2 * ffn_taps, fsh),
             (2, F), (1, D)]
    rows_d = [[0, 1], [2, 3], [4, 5], None, [9], [6], [7], [8], None, None, None, None, [10]]
    rows_f = [None] * 11 + [[3, 7], None]
    as2d = lambda ts: [t.reshape(s) for t, s in zip(ts, two_d)]
    loss, g_rep, sd, sm_new, sv_new = _adamw_small(rd, rf, [g_wdw, g_pb, g_ps, g_fdw], as2d(sw), as2d(smm), as2d(svv), rows_d,
                                                   rows_f, loss_row=13)
    sg = dict(g_rep)
    sg.update({3: g_wdw, 8: g_pb, 9: g_ps, 10: g_fdw})
    small_out = {n: tuple(t.reshape(sshapes[k]) for t in (sg[k], sd[k], sm_new[k], sv_new[k])) for k, n in enumerate(snames)}

    order = [small_out["norm_mix"], small_out["norm_ffn"], o_pw1, small_out["b_pw1"], small_out["w_dw"], small_out["b_dw"],
             small_out["ln_g"], small_out["ln_b"], o_pw2, small_out["b_pw2"], o_pool, small_out["pool_b"],
             small_out["pool_scale"], o_up, small_out["ffn_w_dw"], small_out["ffn_b_dw"], o_down, small_out["final_norm"]]
    return (loss[0, 0], grad_x.reshape(nb, seq, D), *[o[0] for o in order], *[o[1] for o in order], *[o[2] for o in order],
            *[o[3] for o in order])
```

```python
import functools

import jax
import jax.numpy as jnp
from jax import lax
from jax.experimental import pallas as pl
from jax.experimental.pallas import tpu as pltpu

F32, BF16 = jnp.float32, jnp.bfloat16
S_ = jax.ShapeDtypeStruct
MESH = pl.DeviceIdType.MESH

RMS_EPS, LN_EPS = 1e-6, 1e-5
POOL_WINDOWS = (2, 4, 8, 16)
ADAM_LR, ADAM_B1, ADAM_B2, ADAM_EPS, ADAM_WD, ADAM_STEP = 0.001, 0.9, 0.999, 1e-08, 0.01, 10

VMEM_LIMIT_BYTES = 60 * 1024 * 1024
SUBLANES, LANES = 8, 128
BF16_ROWS = 16
BLOCK_BYTES = 1 << 20
N_CHIPS = 4
CHUNK = 32
HALO_CONV = 32
HALO_POOL = 32
HALO_FFN = 8
TILE_D = 512
TILE_F = 256


def _pcall(body, *, name, grid, in_specs, out_specs, out_shape, args, scratch=(), aliases=None, prefetch=0, comm=None):
    params = pltpu.CompilerParams(dimension_semantics=("arbitrary",) * len(grid), vmem_limit_bytes=VMEM_LIMIT_BYTES)
    if comm is not None:
        assert not prefetch
        single = not isinstance(out_shape, (list, tuple))
        in_specs, args, scratch = list(in_specs), list(args), list(scratch)
        out_specs, out_shape = ([out_specs], [out_shape]) if single else (list(out_specs), list(out_shape))
        n_in, n_out, n_scr, n_cin, n_cout = len(in_specs), len(out_specs), len(scratch), len(comm.inputs), len(comm.out_shape)
        any_spec = pl.BlockSpec(memory_space=pl.ANY)
        aliases = dict(aliases or {})
        aliases.update({n_in + a: n_out + b for a, b in comm.aliases.items()})
        inner = body

        def body(*refs):
            ins, cin = refs[:n_in], refs[n_in:n_in + n_cin]
            outs = refs[n_in + n_cin:n_in + n_cin + n_out]
            cout = refs[n_in + n_cin + n_out:n_in + n_cin + n_out + n_cout]
            scr = refs[n_in + n_cin + n_out + n_cout:n_in + n_cin + n_out + n_cout + n_scr]
            sems = refs[n_in + n_cin + n_out + n_cout + n_scr:]
            step, total = 0, 1
            for d, g in enumerate(grid):
                step = step * g + pl.program_id(d)
                total *= g
            comm.before(cin, cout, sems, step, total)
            inner(*ins, *outs, *scr)
            comm.after(cin, cout, sems, step, total)

        in_specs += [any_spec] * n_cin
        args += list(comm.inputs)
        out_specs += [any_spec] * n_cout
        out_shape += list(comm.out_shape)
        scratch += list(comm.sems)
    if prefetch:
        spec = pltpu.PrefetchScalarGridSpec(num_scalar_prefetch=prefetch, grid=grid, in_specs=in_specs,
                                            out_specs=out_specs, scratch_shapes=list(scratch))
        return pl.pallas_call(body, name=name, grid_spec=spec, out_shape=out_shape,
                              input_output_aliases=aliases or {}, compiler_params=params)(*args)
    return pl.pallas_call(body, name=name, grid=grid, in_specs=in_specs, out_specs=out_specs, out_shape=out_shape,
                          scratch_shapes=list(scratch), input_output_aliases=aliases or {},
                          compiler_params=params)(*args)


def _full(shape):
    n = len(shape)
    return pl.BlockSpec(tuple(shape), lambda *_: (0,) * n, pipeline_mode=pl.Buffered(1))


def _rows(tm, c, col=0):
    return pl.BlockSpec((tm, c), lambda i, *_: (i, col))


def _cols(r, tm):
    return pl.BlockSpec((r, tm), lambda i, *_: (0, i))


def _prev_halo(tm, hb, c):
    return pl.BlockSpec((hb, c), lambda i, *_: (jnp.maximum(i * (tm // hb) - 1, 0), 0))


def _next_halo(tm, hb, c, total_rows):
    last = total_rows // hb - 1
    return pl.BlockSpec((hb, c), lambda i, *_: (jnp.minimum((i + 1) * (tm // hb), last), 0))


def _acc_spec(r, c):
    return pl.BlockSpec((r, c), lambda *_: (0, 0))


def _rms(x, g):
    r = lax.rsqrt(jnp.mean(x * x, axis=-1, keepdims=True) + RMS_EPS)
    xh = x * r
    return xh * g, xh, r


def _rms_bwd(dy, xh, r, g):
    dxh = dy * g
    dx = r * (dxh - xh * jnp.mean(dxh * xh, axis=-1, keepdims=True))
    return dx, jnp.sum(dy * xh, axis=0, keepdims=True)


def _colsum(v):
    return jnp.sum(v, axis=0, keepdims=True)


def _fold8(v):
    out = v[0:8]
    for j in range(1, v.shape[0] // 8):
        out = out + v[8 * j:8 * j + 8]
    return out


def _chunks(n_rows, fn):
    def step(j, carry):
        fn(pl.multiple_of(j * CHUNK, CHUNK))
        return carry
    lax.fori_loop(0, n_rows // CHUNK, step, 0)


def _dot(a, b):
    return jnp.dot(a, b, preferred_element_type=F32)


def _dot_nt(a, b):
    return lax.dot_general(a, b, (((1,), (1,)), ((), ())), preferred_element_type=F32)


def _dot_tn(a, b):
    return lax.dot_general(a, b, (((0,), (0,)), ((), ())), preferred_element_type=F32)


def _sig(v):
    return jax.nn.sigmoid(v)


def _accumulate(ref, val, first):
    @pl.when(first)
    def _():
        ref[...] = val

    @pl.when(jnp.logical_not(first))
    def _():
        ref[...] = ref[...] + val


def _conv_in(x, g, w1g, b1, tm, comm=None):
    T, D = x.shape
    ns, _, nc = w1g.shape

    def body(x_ref, g_ref, w_ref, b_ref, hbt_ref, a_ref):
        hb = _rms(x_ref[...], g_ref[...])[0].astype(BF16)
        hbt_ref[...] = hb.T
        for s in range(ns):
            a_ref[:, s * nc:(s + 1) * nc] = _dot(hb, w_ref[s]) + b_ref[:, s * nc:(s + 1) * nc]

    return _pcall(body, name="conv_in", grid=(T // tm,),
                  in_specs=[_rows(tm, D), _full((1, D)), _full(w1g.shape), _full((1, ns * nc))],
                  out_specs=[_cols(D, tm), _rows(tm, ns * nc)],
                  out_shape=[S_((D, T), BF16), S_((T, ns * nc), F32)], args=(x, g, w1g, b1), comm=comm)


def _conv_mid(a, x, wdw, bdw, lng, lnb, w2, b2, gn, tm, tps, taps, comm=None):
    T, D = x.shape
    hc = HALO_CONV

    def body(a_ref, ah_ref, x_ref, wdw_ref, bdw_ref, lng_ref, lnb_ref, w2_ref, b2_ref, gn_ref,
             c_ref, sbt_ref, x1_ref, hb_ref, hbt_ref, uext, shifted):
        i = pl.program_id(0)
        ah = ah_ref[...]
        uext[0:hc, :] = jnp.where(i % tps == 0, 0.0, ah[:, :D] * _sig(ah[:, D:]))
        av = a_ref[...]
        uext[hc:hc + tm, :] = av[:, :D] * _sig(av[:, D:])

        def conv(r0, src, ks, first):
            acc = jnp.broadcast_to(bdw_ref[...], (CHUNK, D)) if first else c_ref[pl.ds(r0, CHUNK), :]
            for k, o in ks:
                acc = acc + wdw_ref[k:k + 1, :] * src[pl.ds(pl.multiple_of(r0 + o, 8), CHUNK), :]
            c_ref[pl.ds(r0, CHUNK), :] = acc
        _by_sublane_shift(uext, shifted, tm, [hc - (taps - 1) + k for k in range(taps)], conv)

        cv = c_ref[...]
        xc = cv - jnp.mean(cv, axis=-1, keepdims=True)
        nh = xc * lax.rsqrt(jnp.mean(xc * xc, axis=-1, keepdims=True) + LN_EPS)
        n = nh * lng_ref[...] + lnb_ref[...]
        sb = (n * _sig(n)).astype(BF16)
        sbt_ref[...] = sb.T
        x1 = x_ref[...] + _dot(sb, w2_ref[...]) + b2_ref[...]
        x1_ref[...] = x1
        hb = _rms(x1, gn_ref[...])[0].astype(BF16)
        hb_ref[...] = hb
        hbt_ref[...] = hb.T

    return _pcall(body, name="conv_mid", grid=(T // tm,),
                  in_specs=[_rows(tm, 2 * D), _prev_halo(tm, hc, 2 * D), _rows(tm, D), _full(wdw.shape), _full((1, D)),
                            _full((1, D)), _full((1, D)), _full((D, D)), _full((1, D)), _full((1, D))],
                  out_specs=[_rows(tm, D), _cols(D, tm), _rows(tm, D), _rows(tm, D), _cols(D, tm)],
                  out_shape=[S_((T, D), F32), S_((D, T), BF16), S_((T, D), F32), S_((T, D), BF16), S_((D, T), BF16)],
                  scratch=[pltpu.VMEM((tm + hc, D), F32), pltpu.VMEM((tm + hc - 8, D), F32)],
                  args=(a, a, x, wdw, bdw, lng, lnb, w2, b2, gn), comm=comm)


def _ffn_taps(aext, realigned, tm, ffn_taps):
    for k in range(ffn_taps - 1):
        realigned[k] = aext[pl.ds(HALO_FFN - (ffn_taps - 1) + k, tm), :]
    return [realigned.at[k] for k in range(ffn_taps - 1)] + [aext.at[pl.ds(HALO_FFN, tm)]]


def _ffn_conv(taps, wdw_ref, bdw_ref):
    acc = bdw_ref[...]
    for k, tap in enumerate(taps):
        acc = acc + wdw_ref[k:k + 1, :] * tap[...]
    return acc


def _by_sublane_shift(ext, shifted, tm, offsets, per_chunk):
    first = True
    for b in range(8):
        ks = [(k, o - b) for k, o in enumerate(offsets) if o % 8 == b]
        if not ks:
            continue
        if b:
            shifted[...] = ext[pl.ds(b, shifted.shape[0]), :]
        _chunks(tm, functools.partial(per_chunk, src=shifted if b else ext, ks=ks, first=first))
        first = False


def _ffn_fwd(hb, xin, wg, wdw, bdw, wd, gn, tm, tps, ffn_taps, name, target=None, comm=None):
    T, D = xin.shape
    ns, _, nc = wg.shape
    F = ns * nc // 2
    hf = HALO_FFN
    hh = 2 * HALO_FFN
    final = target is not None

    def body(*refs):
        if final:
            (h_ref, hh_ref, x_ref, wg_ref, wdw_ref, bdw_ref, wd_ref, gn_ref, tg_ref,
             up_ref, gb_ref, dx_ref, dxb_ref, loss_ref, dgn_ref, aext, realigned) = refs
        else:
            (h_ref, hh_ref, x_ref, wg_ref, wdw_ref, bdw_ref, wd_ref, gn_ref,
             up_ref, gb_ref, xo_ref, ho_ref, aext, realigned) = refs
        i = pl.program_id(0)
        hv = h_ref[...]
        for s in range(ns):
            up_ref[:, s * nc:(s + 1) * nc] = _dot(hv, wg_ref[s])
        halo = jnp.concatenate([_dot(hh_ref[...], wg_ref[s]) for s in range(ns // 2)], axis=1)[hh - hf:hh, :]
        aext[0:hf, :] = jnp.where(i % tps == 0, 0.0, halo)
        aext[hf:hf + tm, :] = up_ref[:, 0:F]
        acv = _ffn_conv(_ffn_taps(aext, realigned, tm, ffn_taps), wdw_ref, bdw_ref)
        gb = (acv * _sig(acv) * up_ref[:, F:2 * F]).astype(BF16)
        gb_ref[...] = gb.T
        xo = x_ref[...] + _dot(gb, wd_ref[...])
        if not final:
            xo_ref[...] = xo
            ho_ref[...] = _rms(xo, gn_ref[...])[0]
        else:
            out, xh, r = _rms(xo, gn_ref[...])
            diff = out - tg_ref[...]
            part = 0.5 / D * _colsum(diff * diff)
            dx, dg = _rms_bwd(diff * (1.0 / D), xh, r, gn_ref[...])
            dx_ref[...] = dx
            dxb_ref[...] = dx.astype(BF16)
            _accumulate(loss_ref, part, i == 0)
            _accumulate(dgn_ref, dg, i == 0)

    in_specs = [_rows(tm, D), _prev_halo(tm, hh, D), _rows(tm, D), _full(wg.shape), _full(wdw.shape), _full((1, F)),
                _full((F, D)), _full((1, D))]
    args = [hb, hb, xin, wg, wdw, bdw, wd, gn]
    if final:
        in_specs.append(_rows(tm, D))
        args.append(target)
        out_specs = [_rows(tm, 2 * F), _cols(F, tm), _rows(tm, D), _rows(tm, D), _acc_spec(1, D), _acc_spec(1, D)]
        out_shape = [S_((T, 2 * F), F32), S_((F, T), BF16), S_((T, D), F32), S_((T, D), BF16), S_((1, D), F32),
                     S_((1, D), F32)]
    else:
        out_specs = [_rows(tm, 2 * F), _cols(F, tm), _rows(tm, D), _rows(tm, D)]
        out_shape = [S_((T, 2 * F), F32), S_((F, T), BF16), S_((T, D), F32), S_((T, D), F32)]
    return _pcall(body, name=name, grid=(T // tm,), in_specs=in_specs, out_specs=out_specs, out_shape=out_shape,
                  scratch=[pltpu.VMEM((tm + hf, F), F32), pltpu.VMEM((ffn_taps - 1, tm, F), F32)], args=args, comm=comm)


def _pool_counts(i, tps, tm, w):
    pos = (i % tps) * tm + lax.broadcasted_iota(jnp.int32, (tm, 1), 0)
    return jnp.minimum(pos + 1, w).astype(F32)


def _window_sums(src, buf_a, buf_b, cg, causal):
    assert POOL_WINDOWS == (2, 4, 8, 16)
    n = src.shape[0]
    levels = len(POOL_WINDOWS)
    cur, outs = src, []
    for l in range(levels):
        dst = buf_b if l % 2 else buf_a
        cols = slice(l * cg, levels * cg)
        lo, m = 8 * (l + 1), n - 8 * (l + 1)
        if causal:
            dst[lo:n, cols] = cur[lo:n, cols] + cur[pl.ds(lo - (1 << l), m), cols]
        else:
            dst[0:m, cols] = cur[0:m, cols] + cur[pl.ds(1 << l, m), cols]
        outs.append(dst)
        cur = dst
    return outs


def _pool_fwd(h, x2, pw, pb, ps, gn, tm, tps, comm=None):
    T, D = h.shape
    ng, cg, _ = pw.shape
    hp = HALO_POOL

    def body(h_ref, hh_ref, x_ref, pw_ref, pb_ref, ps_ref, gn_ref, pl_ref, xo_ref, hb_ref, hbt_ref, hext, buf_a, buf_b):
        i = pl.program_id(0)
        hext[0:hp, :] = jnp.where(i % tps == 0, 0.0, hh_ref[...])
        hext[hp:hp + tm, :] = h_ref[...]
        sums = _window_sums(hext, buf_a, buf_b, cg, True)
        for g, w in enumerate(POOL_WINDOWS):
            cs = slice(g * cg, (g + 1) * cg)
            pl_ref[:, cs] = (sums[g][hp:hp + tm, cs] / _pool_counts(i, tps, tm, w) - h_ref[:, cs]).astype(BF16)

        pv = pl_ref[...]
        mixed = jnp.concatenate([_dot(pv[:, g * cg:(g + 1) * cg], pw_ref[g]) for g in range(ng)], axis=1)
        xo = x_ref[...] + ps_ref[...] * (mixed + pb_ref[...])
        xo_ref[...] = xo
        hb = _rms(xo, gn_ref[...])[0].astype(BF16)
        hb_ref[...] = hb
        hbt_ref[...] = hb.T

    return _pcall(body, name="pool_fwd", grid=(T // tm,),
                  in_specs=[_rows(tm, D), _prev_halo(tm, hp, D), _rows(tm, D), _full(pw.shape), _full((1, D)), _full((1, D)),
                            _full((1, D))],
                  out_specs=[_rows(tm, D), _rows(tm, D), _rows(tm, D), _cols(D, tm)],
                  out_shape=[S_((T, D), BF16), S_((T, D), F32), S_((T, D), BF16), S_((D, T), BF16)],
                  scratch=[pltpu.VMEM((tm + hp, D), F32)] * 3, args=(h, h, x2, pw, pb, ps, gn), comm=comm)


def _bwd_down(dx, up, wdw, bdw, wd, tm, tps, ffn_taps, name, comm=None):
    T, D = dx.shape
    F = up.shape[1] // 2
    hf = HALO_FFN
    nt = T // tm

    def body(dx_ref, up_ref, uph_ref, wdw_ref, bdw_ref, wd_ref, dac_ref, dgate_ref, wsum_ref, aext, realigned):
        i = pl.program_id(0)
        aext[0:hf, :] = jnp.where(i % tps == 0, 0.0, uph_ref[...])
        aext[hf:hf + tm, :] = up_ref[:, 0:F]
        taps = _ffn_taps(aext, realigned, tm, ffn_taps)
        acv = _ffn_conv(taps, wdw_ref, bdw_ref)
        dg = _dot_nt(dx_ref[...].astype(BF16), wd_ref[...])
        sg = _sig(acv)
        dgate_ref[...] = (dg * acv * sg).astype(BF16)
        dac = dg * up_ref[:, F:2 * F] * (sg * (1.0 + acv * (1.0 - sg)))
        dac_ref[...] = dac
        rows = [_colsum(dac * tap[...]) for tap in taps]
        rows += [_colsum(dac), jnp.zeros((8 - ffn_taps - 1, F), F32)]
        _accumulate(wsum_ref, jnp.concatenate(rows, axis=0), i == 0)

    return _pcall(body, name=name, grid=(nt,),
                  in_specs=[_rows(tm, D), _rows(tm, 2 * F), _prev_halo(tm, hf, F), _full(wdw.shape), _full((1, F)),
                            _full((F, D))],
                  out_specs=[_rows(tm, F), _rows(tm, F, col=1), _acc_spec(8, F)],
                  out_shape=[S_((T, F), F32), S_((T, 2 * F), BF16), S_((8, F), F32)],
                  scratch=[pltpu.VMEM((tm + hf, F), F32), pltpu.VMEM((ffn_taps - 1, tm, F), F32)],
                  args=(dx, up, up, wdw, bdw, wd), comm=comm)


def _bwd_up(dac, dup, wg, wdw, xin, gn, dx, tm, tps, ffn_taps, name, comm=None):
    T, D = xin.shape
    F = dac.shape[1]
    ns, _, nc = wg.shape
    hf = HALO_FFN
    nt = T // tm

    def body(dac_ref, dach_ref, dgate_ref, wg_ref, wdw_ref, x_ref, gn_ref, dx_ref, dact_ref, dxo_ref, dxb_ref, dgn_ref, dext):
        i = pl.program_id(0)
        dext[0:tm, :] = dac_ref[...]
        dext[tm:tm + hf, :] = jnp.where(i % tps == tps - 1, 0.0, dach_ref[...])
        dact = wdw_ref[ffn_taps - 1:ffn_taps, :] * dac_ref[...]
        for k in range(ffn_taps - 1):
            dact = dact + wdw_ref[k:k + 1, :] * dext[pl.ds(ffn_taps - 1 - k, tm), :]
        dact_ref[...] = dact.astype(BF16)

        dh = jnp.zeros((tm, D), F32)
        for s in range(ns):
            src = dact_ref if s < ns // 2 else dgate_ref
            o = (s % (ns // 2)) * nc
            dh = dh + _dot_nt(src[:, o:o + nc], wg_ref[s])
        _, xh, r = _rms(x_ref[...], gn_ref[...])
        dxn, dg = _rms_bwd(dh, xh, r, gn_ref[...])
        dxo = dx_ref[...] + dxn
        dxo_ref[...] = dxo
        dxb_ref[...] = dxo.astype(BF16)
        _accumulate(dgn_ref, dg, i == 0)

    return _pcall(body, name=name, grid=(nt,),
                  in_specs=[_rows(tm, F), _next_halo(tm, hf, F, T), _rows(tm, F, col=1), _full(wg.shape), _full(wdw.shape),
                            _rows(tm, D), _full((1, D)), _rows(tm, D)],
                  out_specs=[_rows(tm, F), _rows(tm, D), _rows(tm, D), _acc_spec(1, D)],
                  out_shape=[S_((T, 2 * F), BF16), S_((T, D), F32), S_((T, D), BF16), S_((1, D), F32)],
                  scratch=[pltpu.VMEM((tm + hf, F), F32)],
                  aliases={2: 0}, args=(dac, dac, dup, wg, wdw, xin, gn, dx), comm=comm)


def _mm_tn(xt, dy, bk, bn, shard_major, name, comm=None):
    K, T = xt.shape
    N = dy.shape[1]

    def body(x_ref, dy_ref, o_ref):
        o_ref[...] = _dot(x_ref[...], dy_ref[...]).astype(o_ref.dtype)

    if shard_major:
        out_spec = pl.BlockSpec((None, bk, bn), lambda i, j: (j, i, 0))
        out_shape = S_((N // bn, K, bn), BF16)
    else:
        out_spec = pl.BlockSpec((bk, bn), lambda i, j: (i, j))
        out_shape = S_((K, N), BF16)
    return _pcall(body, name=name, grid=(K // bk, N // bn),
                  in_specs=[pl.BlockSpec((bk, T), lambda i, j: (i, 0)), pl.BlockSpec((T, bn), lambda i, j: (0, j))],
                  out_specs=out_spec, out_shape=out_shape, args=(xt, dy), comm=comm)


def _pool_bwd(dx3, plb, x2, pw, pb, ps, gn, tm, tps, comm=None):
    T, D = x2.shape
    ng, cg, _ = pw.shape
    hp = HALO_POOL
    nt = T // tm

    def body(do_ref, doh_ref, pl_ref, x_ref, pw_ref, pb_ref, ps_ref, gn_ref, dxo_ref, dxb_ref, dpw_ref, sm_ref, qext, buf_a,
             buf_b, dh_s):
        i = pl.program_id(0)
        do = do_ref[...]
        dm = do * ps_ref[...]
        dmh = jnp.where(i % tps == tps - 1, 0.0, doh_ref[...]) * ps_ref[...]
        pv = pl_ref[...]
        mixed = jnp.concatenate([_dot(pv[:, g * cg:(g + 1) * cg], pw_ref[g]) for g in range(ng)], axis=1)
        dscale = _colsum(do * (mixed + pb_ref[...]))
        dbias = _colsum(dm)
        dmb = dm.astype(BF16)
        dmhb = dmh.astype(BF16)
        for g, w in enumerate(POOL_WINDOWS):
            cs = slice(g * cg, (g + 1) * cg)
            _accumulate(dpw_ref.at[g], _dot_tn(pv[:, cs], dmb[:, cs]), i == 0)
            dpo = _dot_nt(dmb[:, cs], pw_ref[g])
            dh_s[:, cs] = dpo
            qext[0:tm, cs] = dpo / _pool_counts(i, tps, tm, w)
            qext[tm:tm + hp, cs] = _dot_nt(dmhb[:, cs], pw_ref[g]) * (1.0 / w)
        sums = _window_sums(qext, buf_a, buf_b, cg, False)
        for g in range(ng):
            cs = slice(g * cg, (g + 1) * cg)
            dh_s[:, cs] = sums[g][0:tm, cs] - dh_s[:, cs]

        _, xh, r = _rms(x_ref[...], gn_ref[...])
        dxn, dg = _rms_bwd(dh_s[...], xh, r, gn_ref[...])
        dxo = do + dxn
        dxo_ref[...] = dxo
        dxb_ref[...] = dxo.astype(BF16)
        sm = jnp.concatenate([dscale, dbias, dg, jnp.zeros((5, D), F32)], axis=0)
        _accumulate(sm_ref, sm, i == 0)

    return _pcall(body, name="pool_bwd", grid=(nt,),
                  in_specs=[_rows(tm, D), _next_halo(tm, hp, D, T), _rows(tm, D), _rows(tm, D), _full(pw.shape), _full((1, D)),
                            _full((1, D)), _full((1, D))],
                  out_specs=[_rows(tm, D), _rows(tm, D), pl.BlockSpec((ng, cg, cg), lambda i: (0, 0, 0)), _acc_spec(8, D)],
                  out_shape=[S_((T, D), F32), S_((T, D), BF16), S_((ng, cg, cg), F32), S_((8, D), F32)],
                  scratch=[pltpu.VMEM((tm + hp, D), F32)] * 3 + [pltpu.VMEM((tm, D), F32)],
                  args=(dx3, dx3, plb, x2, pw, pb, ps, gn), comm=comm)


def _conv_bwd1(dx1, c, w2, lng, lnb, tm, comm=None):
    T, D = dx1.shape

    def body(do_ref, c_ref, w2_ref, lng_ref, lnb_ref, dc_ref, sm_ref):
        i = pl.program_id(0)
        do = do_ref[...]
        ds = _dot_nt(do.astype(BF16), w2_ref[...])
        cv = c_ref[...]
        xc = cv - jnp.mean(cv, axis=-1, keepdims=True)
        rstd = lax.rsqrt(jnp.mean(xc * xc, axis=-1, keepdims=True) + LN_EPS)
        nh = xc * rstd
        n = nh * lng_ref[...] + lnb_ref[...]
        sg = _sig(n)
        dn = ds * (sg * (1.0 + n * (1.0 - sg)))
        dnh = dn * lng_ref[...]
        dc = rstd * (dnh - jnp.mean(dnh, axis=-1, keepdims=True) - nh * jnp.mean(dnh * nh, axis=-1, keepdims=True))
        dc_ref[...] = dc
        sm = jnp.concatenate([_colsum(dn * nh), _colsum(dn), _colsum(do), _colsum(dc), jnp.zeros((4, D), F32)], axis=0)
        _accumulate(sm_ref, sm, i == 0)

    return _pcall(body, name="conv_bwd1", grid=(T // tm,),
                  in_specs=[_rows(tm, D), _rows(tm, D), _full((D, D)), _full((1, D)), _full((1, D))],
                  out_specs=[_rows(tm, D), _acc_spec(8, D)],
                  out_shape=[S_((T, D), F32), S_((8, D), F32)], args=(dx1, c, w2, lng, lnb), comm=comm)


def _conv_bwd2(dc, a, x, w1g, wdw, gn, dx1, tm, tps, taps, comm=None):
    T, D = x.shape
    ns, _, nc = w1g.shape
    hc = HALO_CONV
    nt = T // tm
    tp = wdw.shape[0]

    def body(dc_ref, dch_ref, a_ref, ah_ref, x_ref, w_ref, wdw_ref, gn_ref, dx1_ref,
             gx_ref, dab_ref, dw_ref, db1_ref, dgn_ref, uext, dext, shifted, du_s, wacc):
        i = pl.program_id(0)
        ah = ah_ref[...]
        uext[0:hc, :] = jnp.where(i % tps == 0, 0.0, ah[:, :D] * _sig(ah[:, D:]))
        av = a_ref[...]
        sg2 = _sig(av[:, D:])
        uext[hc:hc + tm, :] = av[:, :D] * sg2
        dext[0:tm, :] = dc_ref[...]
        dext[tm:tm + hc, :] = jnp.where(i % tps == tps - 1, 0.0, dch_ref[...])

        @pl.when(i == 0)
        def _():
            wacc[...] = jnp.zeros_like(wacc)

        def wgrad(r0, src, ks, first):
            d = dc_ref[pl.ds(r0, CHUNK), :]
            for k, o in ks:
                wacc[8 * k:8 * k + 8, :] += _fold8(d * src[pl.ds(pl.multiple_of(r0 + o, 8), CHUNK), :])
        _by_sublane_shift(uext, shifted, tm, [hc - (taps - 1) + k for k in range(taps)], wgrad)

        def convt(r0, src, ks, first):
            acc = jnp.zeros((CHUNK, D), F32) if first else du_s[pl.ds(r0, CHUNK), :]
            for k, o in ks:
                acc = acc + wdw_ref[k:k + 1, :] * src[pl.ds(pl.multiple_of(r0 + o, 8), CHUNK), :]
            du_s[pl.ds(r0, CHUNK), :] = acc
        _by_sublane_shift(dext, shifted, tm, [taps - 1 - k for k in range(taps)], convt)

        du = du_s[...]
        da1 = du * sg2
        da2 = du * av[:, :D] * (sg2 * (1.0 - sg2))
        dab_ref[:, 0:D] = da1.astype(BF16)
        dab_ref[:, D:2 * D] = da2.astype(BF16)
        _accumulate(db1_ref, jnp.concatenate([_colsum(da1), _colsum(da2)], axis=0), i == 0)
        dh = jnp.zeros((tm, D), F32)
        for s in range(ns):
            dh = dh + _dot_nt(dab_ref[:, s * nc:(s + 1) * nc], w_ref[s])
        _, xh, r = _rms(x_ref[...], gn_ref[...])
        dxn, dg = _rms_bwd(dh, xh, r, gn_ref[...])
        gx_ref[...] = dx1_ref[...] + dxn
        _accumulate(dgn_ref, dg, i == 0)

        @pl.when(i == nt - 1)
        def _():
            dw_ref[...] = jnp.zeros_like(dw_ref)
            for k in range(taps):
                dw_ref[k:k + 1, :] = _colsum(wacc[8 * k:8 * k + 8, :])

    return _pcall(body, name="conv_bwd2", grid=(nt,),
                  in_specs=[_rows(tm, D), _next_halo(tm, hc, D, T), _rows(tm, 2 * D), _prev_halo(tm, hc, 2 * D), _rows(tm, D),
                            _full(w1g.shape), _full(wdw.shape), _full((1, D)), _rows(tm, D)],
                  out_specs=[_rows(tm, D), _rows(tm, 2 * D), _acc_spec(tp, D), _acc_spec(2, D), _acc_spec(1, D)],
                  out_shape=[S_((T, D), F32), S_((T, 2 * D), BF16), S_((tp, D), F32), S_((2, D), F32), S_((1, D), F32)],
                  scratch=[pltpu.VMEM((tm + hc, D), F32), pltpu.VMEM((tm + hc, D), F32), pltpu.VMEM((tm + hc - 8, D), F32),
                           pltpu.VMEM((tm, D), F32), pltpu.VMEM((8 * taps, D), F32)],
                  args=(dc, dc, a, a, x, w1g, wdw, gn, dx1), comm=comm)


def _place():
    x, y, c = lax.axis_index("x"), lax.axis_index("y"), lax.axis_index("c")
    chips = [(1 - x, y), (x, 1 - y), (1 - x, 1 - y)]
    return x, y, c, chips


class _GatherComm:
    def __init__(self, slots):
        n = len(slots)
        self.inputs = list(slots)
        self.out_shape = [S_(s.shape, s.dtype) for s in slots]
        self.aliases = {i: i for i in range(n)}
        self.sems = [pltpu.SemaphoreType.DMA((n, 3))] * 4
        self.halves = [s.shape[1] // 2 for s in slots]
        self.paired = [s.shape[0] == N_CHIPS // 2 for s in slots]
        self.widths = [s.shape[2] // 2 if p else s.shape[2] for s, p in zip(slots, self.paired)]

    def _copies(self, outs, sems):
        s_ici, r_ici, s_d2d, r_d2d = sems
        x, y, c, chips = _place()
        peers = [(j, chip, 2 * chip[0] + chip[1]) for j, chip in enumerate(chips)]

        def block(i, chip, h):
            rows = pl.ds(pl.multiple_of(h * self.halves[i], 8), self.halves[i])
            if self.paired[i]:
                return outs[i].at[chip // 2, rows, pl.ds(pl.multiple_of((chip % 2) * self.widths[i], LANES), self.widths[i])]
            return outs[i].at[chip, rows]

        def ici(i, j, chip, to):
            blk = block(i, chip, c)
            return pltpu.make_async_remote_copy(src_ref=blk, dst_ref=blk, send_sem=s_ici.at[i, j], recv_sem=r_ici.at[i, j],
                                                device_id=to, device_id_type=MESH)

        def d2d(i, j, chip, h):
            blk = block(i, chip, h)
            return pltpu.make_async_remote_copy(src_ref=blk, dst_ref=blk, send_sem=s_d2d.at[i, j], recv_sem=r_d2d.at[i, j],
                                                device_id=(x, y, 1 - c), device_id_type=MESH)
        return (x, y, c, 2 * x + y), peers, ici, d2d

    def before(self, cin, outs, sems, step, total):
        (x, y, c, q), peers, ici, d2d = self._copies(outs, sems)

        @pl.when(step == 0)
        def _():
            for i in range(len(outs)):
                for j, chip, _ in peers:
                    ici(i, j, q, (*chip, c)).start()

    def after(self, cin, outs, sems, step, total):
        (x, y, c, q), peers, ici, d2d = self._copies(outs, sems)

        @pl.when(step == max(total - 2, 0))
        def _():
            for i in range(len(outs)):
                for j, chip, qj in peers:
                    ici(i, j, qj, (x, y, c)).wait_recv()
                    d2d(i, j, qj, c).start()

        @pl.when(step == total - 1)
        def _():
            for i in range(len(outs)):
                for j, chip, qj in peers:
                    d2d(i, j, qj, 1 - c).wait_recv()
            for i in range(len(outs)):
                for j, chip, qj in peers:
                    ici(i, j, q, (*chip, c)).wait_send()
                    d2d(i, j, qj, c).wait_send()


class _ExchangeComm:
    def __init__(self, sums):
        n = len(sums)
        self.inputs = list(sums)
        self.out_shape = [S_((3,) + s.shape[1:], s.dtype) for s in sums]
        self.aliases = {}
        self.sems = [pltpu.SemaphoreType.DMA((n, 3))] * 2

    def _copies(self, srcs, outs, sems):
        s_sem, r_sem = sems
        x, y, c, chips = _place()
        return [pltpu.make_async_remote_copy(src_ref=srcs[i].at[2 * chip[0] + chip[1]], dst_ref=outs[i].at[j],
                                             send_sem=s_sem.at[i, j], recv_sem=r_sem.at[i, j],
                                             device_id=(*chip, c), device_id_type=MESH)
                for i in range(len(srcs)) for j, chip in enumerate(chips)]

    def before(self, srcs, outs, sems, step, total):
        @pl.when(step == 0)
        def _():
            for cp in self._copies(srcs, outs, sems):
                cp.start()

    def after(self, srcs, outs, sems, step, total):
        @pl.when(step == total - 1)
        def _():
            for cp in self._copies(srcs, outs, sems):
                cp.wait()


def _own_slots(specs):
    n = len(specs)
    shapes, blocks, count = [], [], []
    for a, (w, layer, dtype, pairs) in enumerate(specs):
        _, r, c = w.shape
        br = _row_block(r, c, 2 * BLOCK_BYTES)
        shapes.append(S_((N_CHIPS // 2, r, 2 * c) if pairs else (N_CHIPS, r, c), dtype))
        blocks += [(a, k, k * br, br) for k in range(r // br)]
        count.append(r // br)
    gather = _GatherComm(shapes[:1])
    n_sems = len(gather.sems)

    def body(*refs):
        srcs, outs = refs[:n], refs[n:2 * n]
        inb, outb = refs[2 * n:3 * n], refs[3 * n:4 * n]
        lsem, ssem = refs[4 * n], refs[4 * n + 1]
        gsems = refs[4 * n + 2:4 * n + 2 + n_sems]
        x, y, c, _ = _place()
        q = 2 * x + y

        def load(a, k, r0, br):
            return pltpu.make_async_copy(srcs[a].at[specs[a][1], pl.ds(r0, br)], inb[a].at[k % 2], lsem.at[a, k % 2])

        def store(a, k, r0, br):
            if specs[a][3]:
                wc = specs[a][0].shape[2]
                dst = outs[a].at[q // 2, pl.ds(r0, br), pl.ds(pl.multiple_of((q % 2) * wc, LANES), wc)]
            else:
                dst = outs[a].at[q, pl.ds(r0, br)]
            return pltpu.make_async_copy(outb[a].at[k % 2], dst, ssem.at[a, k % 2])

        load(*blocks[0]).start()
        for b, (a, k, r0, br) in enumerate(blocks):
            if b + 1 < len(blocks):
                load(*blocks[b + 1]).start()
            load(a, k, r0, br).wait()
            if k >= 2:
                store(a, k - 2, r0 - 2 * br, br).wait()
            outb[a][k % 2] = inb[a][k % 2].astype(specs[a][2])
            store(a, k, r0, br).start()
            if a == 0 and k == count[0] - 1:
                for kk in range(max(k - 1, 0), k + 1):
                    store(0, kk, kk * br, br).wait()
                gather.before(None, outs[:1], gsems, 0, 1)
        for a in range(1, n):
            br = blocks[sum(count[:a])][3]
            for kk in range(max(count[a] - 2, 0), count[a]):
                store(a, kk, kk * br, br).wait()
        gather.after(None, outs[:1], gsems, 0, 1)

    any_spec = pl.BlockSpec(memory_space=pl.ANY)
    scratch = ([pltpu.VMEM((2, blocks[sum(count[:a])][3], specs[a][0].shape[2]), F32) for a in range(n)]
               + [pltpu.VMEM((2, blocks[sum(count[:a])][3], specs[a][0].shape[2]), specs[a][2]) for a in range(n)]
               + [pltpu.SemaphoreType.DMA((n, 2)), pltpu.SemaphoreType.DMA((n, 2))] + list(gather.sems))
    outs = pl.pallas_call(body, name="own_slots", in_specs=[any_spec] * n, out_specs=[any_spec] * n, out_shape=shapes,
                          scratch_shapes=scratch,
                          compiler_params=pltpu.CompilerParams(vmem_limit_bytes=VMEM_LIMIT_BYTES))(*[s[0] for s in specs])
    return list(outs)


def _comm_only(comm, name):
    def body():
        pass
    outs = _pcall(body, name=name, grid=(1,), in_specs=[], out_specs=[], out_shape=[], args=[], comm=comm)
    return list(outs)


def _pair_exchange(parts, name):
    n = len(parts)

    def body(*refs):
        srcs, outs = refs[:n], refs[n:2 * n]
        s_sem, r_sem = refs[2 * n:]
        x, y, c, _ = _place()
        cps = []
        for i in range(n):
            for s in range(N_CHIPS):
                cp = pltpu.make_async_remote_copy(src_ref=srcs[i].at[s, 1 - c], dst_ref=outs[i].at[s], send_sem=s_sem.at[i, s],
                                                  recv_sem=r_sem.at[i, s], device_id=(x, y, 1 - c), device_id_type=MESH)
                cp.start()
                cps.append(cp)
        for cp in cps:
            cp.wait()

    any_spec = pl.BlockSpec(memory_space=pl.ANY)
    outs = pl.pallas_call(
        body, name=name, in_specs=[any_spec] * n, out_specs=[any_spec] * n,
        out_shape=[S_((N_CHIPS,) + p.shape[2:], p.dtype) for p in parts],
        scratch_shapes=[pltpu.SemaphoreType.DMA((n, N_CHIPS))] * 2,
    )(*parts)
    return list(outs)


class _PairComm:
    def __init__(self, parts):
        n = len(parts)
        self.inputs = list(parts)
        self.out_shape = [S_((N_CHIPS,) + p.shape[2:], p.dtype) for p in parts]
        self.aliases = {}
        self.sems = [pltpu.SemaphoreType.DMA((n, N_CHIPS))] * 2

    def _copies(self, srcs, outs, sems):
        s_sem, r_sem = sems
        x, y, c, _ = _place()
        return [pltpu.make_async_remote_copy(src_ref=srcs[i].at[s, 1 - c], dst_ref=outs[i].at[s], send_sem=s_sem.at[i, s],
                                             recv_sem=r_sem.at[i, s], device_id=(x, y, 1 - c), device_id_type=MESH)
                for i in range(len(srcs)) for s in range(N_CHIPS)]

    def before(self, srcs, outs, sems, step, total):
        @pl.when(step == 0)
        def _():
            for cp in self._copies(srcs, outs, sems):
                cp.start()

    def after(self, srcs, outs, sems, step, total):
        @pl.when(step == total - 1)
        def _():
            for cp in self._copies(srcs, outs, sems):
                cp.wait()


class _ShareComm:
    def __init__(self, bufs):
        n = len(bufs)
        self.inputs = list(bufs)
        self.out_shape = [S_(b.shape, b.dtype) for b in bufs]
        self.aliases = {i: i for i in range(n)}
        self.sems = [pltpu.SemaphoreType.DMA((n, max(b.shape[0] for b in bufs)))] * 2
        self.layers = [b.shape[0] for b in bufs]

    def _copies(self, outs, sems):
        s_sem, r_sem = sems
        x, y, c, _ = _place()
        cps = []
        for i, nl in enumerate(self.layers):
            for l in range(nl):
                blk = outs[i].at[l, c]
                cps.append(pltpu.make_async_remote_copy(src_ref=blk, dst_ref=blk, send_sem=s_sem.at[i, l],
                                                        recv_sem=r_sem.at[i, l], device_id=(x, y, 1 - c),
                                                        device_id_type=MESH))
        return cps

    def before(self, cin, outs, sems, step, total):
        @pl.when(step == 0)
        def _():
            for cp in self._copies(outs, sems):
                cp.start()

    def after(self, cin, outs, sems, step, total):
        @pl.when(step == total - 1)
        def _():
            for cp in self._copies(outs, sems):
                cp.wait()


class _Both:
    def __init__(self, a, b):
        self.parts = (a, b)
        self.inputs = a.inputs + b.inputs
        self.out_shape = a.out_shape + b.out_shape
        self.aliases = dict(a.aliases)
        self.aliases.update({len(a.inputs) + i: len(a.out_shape) + o for i, o in b.aliases.items()})
        self.sems = a.sems + b.sems

    def _each(self, phase, cin, cout, sems, step, total):
        i = o = s = 0
        for p in self.parts:
            ni, no, ns = len(p.inputs), len(p.out_shape), len(p.sems)
            getattr(p, phase)(cin[i:i + ni], cout[o:o + no], sems[s:s + ns], step, total)
            i, o, s = i + ni, o + no, s + ns

    def before(self, cin, cout, sems, step, total):
        self._each("before", cin, cout, sems, step, total)

    def after(self, cin, cout, sems, step, total):
        self._each("after", cin, cout, sems, step, total)


def _small_allreduce(groups):
    n = len(groups)
    arrays = []
    for grp in groups:
        for p in grp:
            if not isinstance(p, int) and not any(p[0] is a for a in arrays):
                arrays.append(p[0])
    shapes = []
    for grp in groups:
        rows = sum(p if isinstance(p, int) else p[2] - p[1] for p in grp)
        width = next(p[0].shape[1] for p in grp if not isinstance(p, int))
        assert rows % 8 == 0
        shapes.append((rows, width))
    na = len(arrays)

    def body(*refs):
        a_refs, o_refs = refs[:na], refs[na:na + n]
        rest = refs[na + n:]
        v_refs, sibs, css, gots = rest[:n], rest[n:2 * n], rest[2 * n:3 * n], rest[3 * n:4 * n]
        s_sem, r_sem = rest[4 * n:]
        for i, grp in enumerate(groups):
            o = 0
            for p in grp:
                if isinstance(p, int):
                    v_refs[i][o:o + p, :] = jnp.zeros((p, shapes[i][1]), F32)
                    o += p
                else:
                    k = next(t for t, a in enumerate(arrays) if a is p[0])
                    v_refs[i][o:o + p[2] - p[1], :] = a_refs[k][p[1]:p[2], :]
                    o += p[2] - p[1]
        x, y, c, chips = _place()
        q = 2 * x + y
        to_sib = [pltpu.make_async_remote_copy(src_ref=v_refs[i], dst_ref=sibs[i], send_sem=s_sem.at[i, 3], recv_sem=r_sem.at[i, 3],
                                               device_id=(x, y, 1 - c), device_id_type=MESH) for i in range(n)]
        for cp in to_sib:
            cp.start()
        cps = []
        for i in range(n):
            to_sib[i].wait()
            mine, other = v_refs[i][...], sibs[i][...]
            css[i][...] = jnp.where(c == 0, mine, other) + jnp.where(c == 0, other, mine)
            for j, chip in enumerate(chips):
                cp = pltpu.make_async_remote_copy(src_ref=css[i], dst_ref=gots[i].at[j], send_sem=s_sem.at[i, j],
                                                  recv_sem=r_sem.at[i, j], device_id=(*chip, c), device_id_type=MESH)
                cp.start()
                cps.append(cp)
        for cp in cps:
            cp.wait()
        flips = [2, 1, 3]
        for i in range(n):
            total = None
            for k in range(N_CHIPS):
                d = q ^ k
                term = jnp.where(d == 0, css[i][...], 0.0)
                for j in range(3):
                    term = jnp.where(d == flips[j], gots[i][j], term)
                total = term if total is None else total + term
            o_refs[i][...] = total

    vm = pl.BlockSpec(memory_space=pltpu.VMEM)
    return pl.pallas_call(
        body, name="small_allreduce", in_specs=[vm] * na, out_specs=[vm] * n, out_shape=[S_(s, F32) for s in shapes],
        scratch_shapes=[pltpu.VMEM(s, F32) for s in shapes] * 3 + [pltpu.VMEM((3,) + s, F32) for s in shapes]
        + [pltpu.SemaphoreType.DMA((n, 4)), pltpu.SemaphoreType.DMA((n, 4))],
    )(*arrays)


def _row_block(r, c, target_bytes=BLOCK_BYTES):
    br = r
    while br % (2 * BF16_ROWS) == 0 and br * c * 4 > target_bytes:
        br //= 2
    return br


def _chip_sum(part, sib, ids, name):
    _, _, r, c = part.shape
    br = _row_block(r, c, 4 * BLOCK_BYTES)

    def body(ids_ref, p_ref, s_ref, o_ref):
        o_ref[...] = (p_ref[...].astype(F32) + s_ref[...].astype(F32)).astype(BF16)

    return _pcall(body, name=name, grid=(N_CHIPS, r // br), prefetch=1,
                  in_specs=[pl.BlockSpec((None, None, br, c), lambda s, i, ids: (s, ids[1], i, 0)),
                            pl.BlockSpec((None, br, c), lambda s, i, ids: (s, i, 0))],
                  out_specs=pl.BlockSpec((None, br, c), lambda s, i, ids: (s, i, 0)),
                  out_shape=S_((N_CHIPS, r, c), BF16), args=(ids, part, sib))


def _final_sum(part, sib, got, ids, name):
    _, r, c = got.shape
    br = _row_block(r, c, 2 * BLOCK_BYTES)

    def body(ids_ref, p_ref, s_ref, got_ref, o_ref):
        v = p_ref[...].astype(F32) + s_ref[...].astype(F32)
        for j in range(3):
            v = v + got_ref[j].astype(F32)
        o_ref[...] = v

    return _pcall(body, name=name, grid=(r // br,), prefetch=1,
                  in_specs=[pl.BlockSpec((None, None, br, c), lambda i, ids: (ids[0], ids[1], i, 0)),
                            pl.BlockSpec((None, br, c), lambda i, ids: (ids[0], i, 0)),
                            pl.BlockSpec((3, br, c), lambda i, ids: (0, i, 0))],
                  out_specs=pl.BlockSpec((None, None, br, c), lambda i, ids: (0, ids[1], i, 0)),
                  out_shape=S_((1, 2, r, c), F32), args=(ids, part, sib, got))


def _final_sums(items, n_bufs, comm):
    n = len(items)
    nci, nco, ncs = len(comm.inputs), len(comm.out_shape), len(comm.sems)
    shapes = [None] * n_bufs
    blocks, geo = [], []
    for a, (part, sib, got, b, layer, nl) in enumerate(items):
        _, r, c = got.shape
        br = _row_block(r, c)
        shapes[b] = S_((nl, 2, r, c), F32)
        geo.append((br, c))
        blocks += [(a, k, k * br, br) for k in range(r // br)]
    n_src = 5

    def body(*refs):
        srcs = [refs[3 * a:3 * a + 3] for a in range(n)]
        cin = refs[3 * n:3 * n + nci]
        outs = refs[3 * n + nci:3 * n + nci + n_bufs]
        cout = refs[3 * n + nci + n_bufs:3 * n + nci + n_bufs + nco]
        rest = refs[3 * n + nci + n_bufs + nco:]
        inb, outb = rest[:n], rest[n:2 * n]
        lsem, ssem = rest[2 * n], rest[2 * n + 1]
        csems = rest[2 * n + 2:]
        x, y, c, _ = _place()
        q = 2 * x + y
        comm.before(cin, cout, csems, 0, 1)

        def loads(a, k, r0, br):
            part, sib, got = srcs[a]
            rows = pl.ds(r0, br)
            pieces = [part.at[q, c, rows], sib.at[q, rows]] + [got.at[j, rows] for j in range(3)]
            return [pltpu.make_async_copy(p, inb[a].at[k % 2, j], lsem.at[a, k % 2, j]) for j, p in enumerate(pieces)]

        def store(a, k, r0, br):
            _, _, _, b, layer, _ = items[a]
            return pltpu.make_async_copy(outb[a].at[k % 2], outs[b].at[layer, c, pl.ds(r0, br)], ssem.at[a, k % 2])

        for cp in loads(*blocks[0]):
            cp.start()
        for i, (a, k, r0, br) in enumerate(blocks):
            if i + 1 < len(blocks):
                for cp in loads(*blocks[i + 1]):
                    cp.start()
            for cp in loads(a, k, r0, br):
                cp.wait()
            if k >= 2:
                store(a, k - 2, r0 - 2 * br, br).wait()
            v = inb[a][k % 2, 0].astype(F32) + inb[a][k % 2, 1].astype(F32)
            for j in range(2, n_src):
                v = v + inb[a][k % 2, j].astype(F32)
            outb[a][k % 2] = v
            store(a, k, r0, br).start()
        for a in range(n):
            cnt = sum(1 for blk in blocks if blk[0] == a)
            for kk in range(max(cnt - 2, 0), cnt):
                store(a, kk, kk * geo[a][0], geo[a][0]).wait()
        comm.after(cin, cout, csems, 0, 1)

    any_spec = pl.BlockSpec(memory_space=pl.ANY)
    scratch = ([pltpu.VMEM((2, n_src) + g, BF16) for g in geo] + [pltpu.VMEM((2,) + g, F32) for g in geo]
               + [pltpu.SemaphoreType.DMA((n, 2, n_src)), pltpu.SemaphoreType.DMA((n, 2))] + list(comm.sems))
    args = [t for it in items for t in it[:3]] + list(comm.inputs)
    outs = pl.pallas_call(body, name="final_sums", in_specs=[any_spec] * len(args), out_specs=[any_spec] * (n_bufs + nco),
                          out_shape=shapes + list(comm.out_shape), scratch_shapes=scratch,
                          compiler_params=pltpu.CompilerParams(vmem_limit_bytes=VMEM_LIMIT_BYTES))(*args)
    return list(outs[:n_bufs]), list(outs[n_bufs:])


def _adam_math(w, g, m, v):
    c1 = 1.0 / (1.0 - ADAM_B1 ** ADAM_STEP)
    c2 = 1.0 / (1.0 - ADAM_B2 ** ADAM_STEP)
    mn = ADAM_B1 * m + (1.0 - ADAM_B1) * g
    vn = ADAM_B2 * v + (1.0 - ADAM_B2) * (g * g)
    return -ADAM_LR * ((mn * c1) / (jnp.sqrt(vn * c2) + ADAM_EPS) + ADAM_WD * w), mn, vn


def _adamw(w, g, m, v, name):
    nl, r, c = w.shape
    br = _row_block(r, c)

    def body(w_ref, g_ref, m_ref, v_ref, go_ref, d_ref, mo_ref, vo_ref):
        gv = g_ref[...]
        go_ref[...] = gv
        d_ref[...], mo_ref[...], vo_ref[...] = _adam_math(w_ref[...], gv, m_ref[...], v_ref[...])

    blk = pl.BlockSpec((None, br, c), lambda l, i: (l, i, 0))
    return _pcall(body, name=name, grid=(nl, r // br), in_specs=[blk] * 4, out_specs=[blk] * 4,
                  out_shape=[S_((nl, r, c), F32)] * 4, args=(w, g, m, v))


def _adamw_small(rd, rf, sharded_g, ws, ms, vs, rows_d, rows_f, loss_row):
    n = len(ws)
    ns = len(sharded_g)
    rep = [k for k in range(n) if rows_d[k] is not None or rows_f[k] is not None]

    def body(*refs):
        rd_ref, rf_ref = refs[0], refs[1]
        sg = refs[2:2 + ns]
        w_refs, m_refs, v_refs = (refs[2 + ns + t * n:2 + ns + (t + 1) * n] for t in range(3))
        loss_ref, outs = refs[2 + ns + 3 * n], refs[3 + ns + 3 * n:]
        g_out, d_out, m_out, v_out = outs[:len(rep)], outs[len(rep):len(rep) + n], outs[len(rep) + n:len(rep) + 2 * n], \
            outs[len(rep) + 2 * n:]
        loss_ref[...] = jnp.sum(rd_ref[loss_row:loss_row + 1, :], axis=1, keepdims=True)
        si = 0
        for k in range(n):
            shape = w_refs[k].shape
            if rows_d[k] is not None or rows_f[k] is not None:
                src, rws = (rd_ref, rows_d[k]) if rows_d[k] is not None else (rf_ref, rows_f[k])
                axis = 0 if shape[0] == len(rws) else 1
                g = src[rws[0]:rws[0] + 1, :]
                if len(rws) > 1:
                    g = jnp.concatenate([src[r:r + 1, :] for r in rws], axis=axis)
                g_out[rep.index(k)][...] = g
            else:
                g = sg[si][...]
                si += 1
            d_out[k][...], m_out[k][...], v_out[k][...] = _adam_math(w_refs[k][...], g, m_refs[k][...], v_refs[k][...])

    vm = pl.BlockSpec(memory_space=pltpu.VMEM)
    shapes = [S_(w.shape, F32) for w in ws]
    outs = pl.pallas_call(
        body, name="adamw_small", in_specs=[vm] * (2 + ns + 3 * n), out_specs=[vm] * (1 + len(rep) + 3 * n),
        out_shape=[S_((1, 1), F32)] + [shapes[k] for k in rep] + shapes * 3,
    )(rd, rf, *sharded_g, *ws, *ms, *vs)
    g_rep = dict(zip(rep, outs[1:1 + len(rep)]))
    o = outs[1 + len(rep):]
    return outs[0], g_rep, o[:n], o[n:2 * n], o[2 * n:]


def _pack(pieces, rows):
    flat = jnp.concatenate([p.reshape(-1).astype(F32) for p in pieces])
    return jnp.pad(flat, (0, rows * LANES - flat.shape[0])).reshape(rows, LANES)


def _unpack(packed, shapes):
    flat = packed.reshape(-1)
    out, o = [], 0
    for s in shapes:
        n = 1
        for d in s:
            n *= d
        out.append(flat[o:o + n].reshape(s))
        o += n
    return out


def _rows_for(pieces_or_shapes):
    n = 0
    for p in pieces_or_shapes:
        k = 1
        for d in (p if isinstance(p, tuple) else p.shape):
            k *= d
        n += k
    return -(-n // (SUBLANES * LANES)) * SUBLANES


def kernel(x, norm_mix, norm_ffn, conv_w_pw1, conv_b_pw1, conv_w_dw, conv_b_dw, conv_ln_g, conv_ln_b, conv_w_pw2, conv_b_pw2, pool_w, pool_b, pool_scale, ffn_w_up, ffn_w_dw, ffn_b_dw, ffn_w_down, final_norm, loss_target, m_norm_mix, m_norm_ffn, m_conv_w_pw1, m_conv_b_pw1, m_conv_w_dw, m_conv_b_dw, m_conv_ln_g, m_conv_ln_b, m_conv_w_pw2, m_conv_b_pw2, m_pool_w, m_pool_b, m_pool_scale, m_ffn_w_up, m_ffn_w_dw, m_ffn_b_dw, m_ffn_w_down, m_final_norm, v_norm_mix, v_norm_ffn, v_conv_w_pw1, v_conv_b_pw1, v_conv_w_dw, v_conv_b_dw, v_conv_ln_g, v_conv_ln_b, v_conv_w_pw2, v_conv_b_pw2, v_pool_w, v_pool_b, v_pool_scale, v_ffn_w_up, v_ffn_w_dw, v_ffn_b_dw, v_ffn_w_down, v_final_norm):
    nb, seq, D = x.shape
    T = nb * seq
    F = ffn_w_down.shape[1] * N_CHIPS
    taps = conv_w_dw.shape[1]
    ffn_taps = ffn_w_dw.shape[1]
    ng = pool_w.shape[1]
    cg = pool_w.shape[3]
    dsh = D // N_CHIPS
    fsh = F // N_CHIPS
    assert taps - 1 <= HALO_CONV and ffn_taps - 1 <= HALO_FFN and max(POOL_WINDOWS) <= HALO_POOL
    tm = min(TILE_D, seq)
    tf = min(TILE_F, seq)
    assert seq % tm == 0 and seq % tf == 0 and tm % HALO_CONV == 0 and tf % CHUNK == 0
    tps, tpf = seq // tm, seq // tf

    xi, yi, ci = lax.axis_index("x"), lax.axis_index("y"), lax.axis_index("c")
    qi = 2 * xi + yi
    ids = jnp.stack([qi, ci]).astype(jnp.int32)
    x2d = x.reshape(T, D)
    tg2d = loss_target.reshape(T, D)

    small_sharded = [conv_w_dw[0], ffn_w_dw, pool_b, pool_scale]
    srows = 2 * _rows_for(small_sharded)
    w1g, s_pw2, s_small, s_pool, s_up0, s_up1, s_down0, s_down1 = _own_slots([
        (conv_w_pw1, 0, BF16, False), (conv_w_pw2, 0, BF16, False), (_pack(small_sharded, srows)[None], 0, F32, False),
        (pool_w.reshape(1, ng * (cg // N_CHIPS), cg), 0, BF16, False), (ffn_w_up, 0, BF16, True), (ffn_w_up, 1, BF16, True),
        (ffn_w_down, 0, BF16, False), (ffn_w_down, 1, BF16, False)])
    s_up, s_down = [s_up0, s_up1], [s_down0, s_down1]
    hb0t, a, w2g, smallg, pwg = _conv_in(x2d, norm_mix[0:1], w1g, conv_b_pw1, tm, comm=_GatherComm([s_pw2, s_small, s_pool]))
    w2f = w2g.reshape(D, D)
    sm_parts = [_unpack(smallg[s], [(taps, dsh), (2, ffn_taps, fsh), (1, dsh), (1, dsh)]) for s in range(N_CHIPS)]
    wdw_f = jnp.concatenate([p[0] for p in sm_parts], axis=1)
    wdw_f = jnp.pad(wdw_f, ((0, HALO_CONV - taps), (0, 0)))
    fdw_f = jnp.concatenate([p[1] for p in sm_parts], axis=2)
    fdw_f = jnp.pad(fdw_f, ((0, 0), (0, 8 - ffn_taps), (0, 0)))
    pb_f = jnp.concatenate([p[2] for p in sm_parts], axis=1)
    ps_f = jnp.concatenate([p[3] for p in sm_parts], axis=1)

    c, sbt, x1, hb1, hb1t, wup0, wd0 = _conv_mid(a, x2d, wdw_f, conv_b_dw, conv_ln_g, conv_ln_b, w2f, conv_b_pw2,
                                                 norm_ffn[0:1], tm, tps, taps, comm=_GatherComm([s_up[0], s_down[0]]))
    up0, gb0t, x2, h2, wup1, wd1 = _ffn_fwd(hb1, x1, wup0, fdw_f[0], ffn_b_dw[0:1], wd0.reshape(F, D), norm_mix[1:2], tf, tpf,
                                            ffn_taps, "ffn_fwd0", comm=_GatherComm([s_up[1], s_down[1]]))
    pwf = pwg.reshape(N_CHIPS, ng, cg // N_CHIPS, cg).transpose(1, 0, 2, 3).reshape(ng, cg, cg)
    plb, x3, hb3, hb3t = _pool_fwd(h2, x2, pwf, pb_f, ps_f, norm_ffn[1:2], tm, tps)
    wup = [wup0, wup1]
    wdn = [wd0.reshape(F, D), wd1.reshape(F, D)]
    up1, gb1t, dx4, dx4b, loss_part, d_final = _ffn_fwd(hb3, x3, wup[1], fdw_f[1], ffn_b_dw[1:2], wdn[1], final_norm.reshape(1, D),
                                                  tf, tpf, ffn_taps, "ffn_fwd1", target=tg2d)

    fs = {}

    def pair(group):
        return list(group), list(group.values()), _PairComm(list(group.values()))

    def chip_sums(nms, parts, from_sib):
        for nm, p, s in zip(nms, parts, from_sib):
            fs[nm] = (p, s)
        return nms, _ExchangeComm([_chip_sum(p, s, ids, "chip_sum_" + nm) for nm, p, s in zip(nms, parts, from_sib)])

    def arrived(nms, got):
        for nm, g in zip(nms, got):
            fs[nm] = fs[nm] + (g,)

    dac1, dup1, fw1 = _bwd_down(dx4, up1, fdw_f[1], ffn_b_dw[1:2], wdn[1], tf, tpf, ffn_taps, "bwd_down1")
    p_down1 = _mm_tn(gb1t, dx4b, F // 4, D, False, "dw_down1")
    nms, parts, px = pair({"down1": p_down1.reshape(N_CHIPS, 2, fsh // 2, D)})
    dup1, dx3, _, dnf1, *sib = _bwd_up(dac1, dup1, wup[1], fdw_f[1], x3, norm_ffn[1:2], dx4, tf, tpf, ffn_taps, "bwd_up1", comm=px)
    nms, xchg = chip_sums(nms, parts, sib)
    p_up1, *got = _mm_tn(hb3t, dup1, D // 2, 2 * fsh, True, "dw_up1", comm=xchg)
    arrived(nms, got)
    nms, parts, px = pair({"up1": p_up1.reshape(N_CHIPS, 2, D // 2, 2 * fsh)})
    dx2, dx2b, dpw, pool_sm, *sib = _pool_bwd(dx3, plb, x2, pwf, pb_f, ps_f, norm_mix[1:2], tm, tps, comm=px)
    nms, xchg = chip_sums(nms, parts, sib)
    dac0, dup0, fw0, *got = _bwd_down(dx2, up0, fdw_f[0], ffn_b_dw[0:1], wdn[0], tf, tpf, ffn_taps, "bwd_down0", comm=xchg)
    arrived(nms, got)
    p_down0 = _mm_tn(gb0t, dx2b, F // 4, D, False, "dw_down0")
    nms, parts, px = pair({"down0": p_down0.reshape(N_CHIPS, 2, fsh // 2, D)})
    dup0, dx1, dx1b, dnf0, *sib = _bwd_up(dac0, dup0, wup[0], fdw_f[0], x1, norm_ffn[0:1], dx2, tf, tpf, ffn_taps, "bwd_up0", comm=px)
    nms, xchg = chip_sums(nms, parts, sib)
    p_up0, *got = _mm_tn(hb1t, dup0, D // 2, 2 * fsh, True, "dw_up0", comm=xchg)
    arrived(nms, got)
    csh = cg // N_CHIPS
    p_pool = dpw.reshape(2, ng // 2, N_CHIPS, csh, cg).transpose(2, 0, 1, 3, 4).reshape(N_CHIPS, 2, (ng // 2) * csh, cg)
    nms, parts, px = pair({"up0": p_up0.reshape(N_CHIPS, 2, D // 2, 2 * fsh), "pool": p_pool.astype(BF16)})
    dc, conv_sm, *sib = _conv_bwd1(dx1, c, w2f, conv_ln_g, conv_ln_b, tm, comm=px)
    nms, xchg = chip_sums(nms, parts, sib)
    p_pw2 = _mm_tn(sbt, dx1b, D, D, False, "dw_pw2")
    nms2, parts2, px = pair({"pw2": p_pw2.reshape(N_CHIPS, 2, dsh // 2, D)})
    grad_x, dab, dwdw, db1, dnm0, *both = _conv_bwd2(dc, a, x2d, w1g, wdw_f, norm_mix[0:1], dx1, tm, tps, taps,
                                                     comm=_Both(xchg, px))
    arrived(nms, both[:len(nms)])
    nms, xchg = chip_sums(nms2, parts2, both[len(nms):])
    p_pw1, *got = _mm_tn(hb0t, dab, D, 2 * dsh, True, "dw_pw1", comm=xchg)
    arrived(nms, got)
    nms, parts, _ = pair({"pw1": p_pw1.reshape(N_CHIPS, 2, D // 2, 2 * dsh)})
    nms, xchg = chip_sums(nms, parts, _pair_exchange(parts, "grad_pair_exchange_pw1"))
    bufs, got = _final_sums([(*fs["pw2"], 0, 0, 1), (*fs["pool"], 1, 0, 1), (*fs["up0"], 2, 0, 2), (*fs["up1"], 2, 1, 2),
                             (*fs["down0"], 3, 0, 2), (*fs["down1"], 3, 1, 2)], 4, xchg)
    arrived(nms, got)
    bufs.append(_final_sum(*fs["pw1"], ids, "final_sum_pw1"))
    shared = _comm_only(_ShareComm(bufs), "grad_share_halves")
    g_pw2, g_pool, g_up, g_down, g_pw1 = [b.reshape(b.shape[0], 2 * b.shape[2], b.shape[3]) for b in shared]
    rd, rf = _small_allreduce([
        [(dnm0, 0, 1), (pool_sm, 2, 3), (dnf0, 0, 1), (dnf1, 0, 1), (db1, 0, 2), (conv_sm, 0, 4), (d_final, 0, 1),
         (pool_sm, 0, 2), (loss_part, 0, 1), 2, (dwdw, 0, dwdw.shape[0])],
        [(fw0, 0, 4), (fw1, 0, 4)]])
    g_wdw = lax.dynamic_slice_in_dim(rd[16:16 + taps], qi * dsh, dsh, axis=1)
    g_ps = lax.dynamic_slice_in_dim(rd[11:12], qi * dsh, dsh, axis=1)
    g_pb = lax.dynamic_slice_in_dim(rd[12:13], qi * dsh, dsh, axis=1)
    g_fdw = lax.dynamic_slice_in_dim(rf.reshape(2, 4, F)[:, :ffn_taps], qi * fsh, fsh, axis=2).reshape(2 * ffn_taps, fsh)

    def big(w, g, m, v, nm):
        return tuple(t.reshape(w.shape) for t in _adamw(w.reshape(g.shape), g, m.reshape(g.shape), v.reshape(g.shape),
                                                        "adamw_" + nm))

    o_pw1 = big(conv_w_pw1, g_pw1, m_conv_w_pw1, v_conv_w_pw1, "pw1")
    o_pw2 = big(conv_w_pw2, g_pw2, m_conv_w_pw2, v_conv_w_pw2, "pw2")
    o_pool = big(pool_w, g_pool, m_pool_w, v_pool_w, "pool")
    o_up = big(ffn_w_up, g_up, m_ffn_w_up, v_ffn_w_up, "up")
    o_down = big(ffn_w_down, g_down, m_ffn_w_down, v_ffn_w_down, "down")

    snames = ["norm_mix", "norm_ffn", "b_pw1", "w_dw", "b_dw", "ln_g", "ln_b", "b_pw2", "pool_b", "pool_scale", "ffn_w_dw",
              "ffn_b_dw", "final_norm"]
    sw = [norm_mix, norm_ffn, conv_b_pw1, conv_w_dw, conv_b_dw, conv_ln_g, conv_ln_b, conv_b_pw2, pool_b, pool_scale,
          ffn_w_dw, ffn_b_dw, final_norm]
    smm = [m_norm_mix, m_norm_ffn, m_conv_b_pw1, m_conv_w_dw, m_conv_b_dw, m_conv_ln_g, m_conv_ln_b, m_conv_b_pw2, m_pool_b,
           m_pool_scale, m_ffn_w_dw, m_ffn_b_dw, m_final_norm]
    svv = [v_norm_mix, v_norm_ffn, v_conv_b_pw1, v_conv_w_dw, v_conv_b_dw, v_conv_ln_g, v_conv_ln_b, v_conv_b_pw2, v_pool_b,
           v_pool_scale, v_ffn_w_dw, v_ffn_b_dw, v_final_norm]
    sshapes = [tuple(w.shape) for w in sw]
    two_d = [(2, D), (2, D), (1, 2 * D), (taps, dsh), (1, D), (1, D), (1, D), (1, D), (1, dsh), (1, dsh), (2 * ffn_taps, fsh),
             (2, F), (1, D)]
    rows_d = [[0, 1], [2, 3], [4, 5], None, [9], [6], [7], [8], None, None, None, None, [10]]
    rows_f = [None] * 11 + [[3, 7], None]
    as2d = lambda ts: [t.reshape(s) for t, s in zip(ts, two_d)]
    loss, g_rep, sd, sm_new, sv_new = _adamw_small(rd, rf, [g_wdw, g_pb, g_ps, g_fdw], as2d(sw), as2d(smm), as2d(svv), rows_d,
                                                   rows_f, loss_row=13)
    sg = dict(g_rep)
    sg.update({3: g_wdw, 8: g_pb, 9: g_ps, 10: g_fdw})
    small_out = {n: tuple(t.reshape(sshapes[k]) for t in (sg[k], sd[k], sm_new[k], sv_new[k])) for k, n in enumerate(snames)}

    order = [small_out["norm_mix"], small_out["norm_ffn"], o_pw1, small_out["b_pw1"], small_out["w_dw"], small_out["b_dw"],
             small_out["ln_g"], small_out["ln_b"], o_pw2, small_out["b_pw2"], o_pool, small_out["pool_b"],
             small_out["pool_scale"], o_up, small_out["ffn_w_dw"], small_out["ffn_b_dw"], o_down, small_out["final_norm"]]
    return (loss[0, 0], grad_x.reshape(nb, seq, D), *[o[0] for o in order], *[o[1] for o in order], *[o[2] for o in order],
            *[o[3] for o in order])
```

```python
import functools

import jax
import jax.numpy as jnp
from jax import lax
from jax.experimental import pallas as pl
from jax.experimental.pallas import tpu as pltpu

F32, BF16 = jnp.float32, jnp.bfloat16
S_ = jax.ShapeDtypeStruct
MESH = pl.DeviceIdType.MESH

RMS_EPS, LN_EPS = 1e-6, 1e-5
POOL_WINDOWS = (2, 4, 8, 16)
ADAM_LR, ADAM_B1, ADAM_B2, ADAM_EPS, ADAM_WD, ADAM_STEP = 0.001, 0.9, 0.999, 1e-08, 0.01, 10

VMEM_LIMIT_BYTES = 60 * 1024 * 1024
SUBLANES, LANES = 8, 128
BF16_ROWS = 16
BLOCK_BYTES = 1 << 20
N_CHIPS = 4
CHUNK = 32
HALO_CONV = 32
HALO_POOL = 32
HALO_FFN = 8
TILE_D = 512
TILE_F = 256


def _pcall(body, *, name, grid, in_specs, out_specs, out_shape, args, scratch=(), aliases=None, prefetch=0, comm=None):
    params = pltpu.CompilerParams(dimension_semantics=("arbitrary",) * len(grid), vmem_limit_bytes=VMEM_LIMIT_BYTES)
    if comm is not None:
        assert not prefetch
        single = not isinstance(out_shape, (list, tuple))
        in_specs, args, scratch = list(in_specs), list(args), list(scratch)
        out_specs, out_shape = ([out_specs], [out_shape]) if single else (list(out_specs), list(out_shape))
        n_in, n_out, n_scr, n_cin, n_cout = len(in_specs), len(out_specs), len(scratch), len(comm.inputs), len(comm.out_shape)
        any_spec = pl.BlockSpec(memory_space=pl.ANY)
        aliases = dict(aliases or {})
        aliases.update({n_in + a: n_out + b for a, b in comm.aliases.items()})
        inner = body

        def body(*refs):
            ins, cin = refs[:n_in], refs[n_in:n_in + n_cin]
            outs = refs[n_in + n_cin:n_in + n_cin + n_out]
            cout = refs[n_in + n_cin + n_out:n_in + n_cin + n_out + n_cout]
            scr = refs[n_in + n_cin + n_out + n_cout:n_in + n_cin + n_out + n_cout + n_scr]
            sems = refs[n_in + n_cin + n_out + n_cout + n_scr:]
            step, total = 0, 1
            for d, g in enumerate(grid):
                step = step * g + pl.program_id(d)
                total *= g
            comm.before(cin, cout, sems, step, total)
            inner(*ins, *outs, *scr)
            comm.after(cin, cout, sems, step, total)

        in_specs += [any_spec] * n_cin
        args += list(comm.inputs)
        out_specs += [any_spec] * n_cout
        out_shape += list(comm.out_shape)
        scratch += list(comm.sems)
    if prefetch:
        spec = pltpu.PrefetchScalarGridSpec(num_scalar_prefetch=prefetch, grid=grid, in_specs=in_specs,
                                            out_specs=out_specs, scratch_shapes=list(scratch))
        return pl.pallas_call(body, name=name, grid_spec=spec, out_shape=out_shape,
                              input_output_aliases=aliases or {}, compiler_params=params)(*args)
    return pl.pallas_call(body, name=name, grid=grid, in_specs=in_specs, out_specs=out_specs, out_shape=out_shape,
                          scratch_shapes=list(scratch), input_output_aliases=aliases or {},
                          compiler_params=params)(*args)


def _full(shape):
    n = len(shape)
    return pl.BlockSpec(tuple(shape), lambda *_: (0,) * n, pipeline_mode=pl.Buffered(1))


def _rows(tm, c, col=0):
    return pl.BlockSpec((tm, c), lambda i, *_: (i, col))


def _cols(r, tm):
    return pl.BlockSpec((None, r, tm), lambda i, *_: (i, 0, 0))


def _prev_halo(tm, hb, c):
    return pl.BlockSpec((hb, c), lambda i, *_: (jnp.maximum(i * (tm // hb) - 1, 0), 0))


def _next_halo(tm, hb, c, total_rows):
    last = total_rows // hb - 1
    return pl.BlockSpec((hb, c), lambda i, *_: (jnp.minimum((i + 1) * (tm // hb), last), 0))


def _acc_spec(r, c):
    return pl.BlockSpec((r, c), lambda *_: (0, 0))


def _rms(x, g):
    r = lax.rsqrt(jnp.mean(x * x, axis=-1, keepdims=True) + RMS_EPS)
    xh = x * r
    return xh * g, xh, r


def _rms_bwd(dy, xh, r, g):
    dxh = dy * g
    dx = r * (dxh - xh * jnp.mean(dxh * xh, axis=-1, keepdims=True))
    return dx, jnp.sum(dy * xh, axis=0, keepdims=True)


def _colsum(v):
    return jnp.sum(v, axis=0, keepdims=True)


def _fold8(v):
    out = v[0:8]
    for j in range(1, v.shape[0] // 8):
        out = out + v[8 * j:8 * j + 8]
    return out


def _chunks(n_rows, fn):
    def step(j, carry):
        fn(pl.multiple_of(j * CHUNK, CHUNK))
        return carry
    lax.fori_loop(0, n_rows // CHUNK, step, 0)


def _dot(a, b):
    return jnp.dot(a, b, preferred_element_type=F32)


def _dot_nt(a, b):
    return lax.dot_general(a, b, (((1,), (1,)), ((), ())), preferred_element_type=F32)


def _dot_tn(a, b):
    return lax.dot_general(a, b, (((0,), (0,)), ((), ())), preferred_element_type=F32)


def _sig(v):
    return jax.nn.sigmoid(v)


def _accumulate(ref, val, first):
    @pl.when(first)
    def _():
        ref[...] = val

    @pl.when(jnp.logical_not(first))
    def _():
        ref[...] = ref[...] + val


def _conv_in(x, g, w1g, b1, tm, comm=None):
    T, D = x.shape
    ns, _, nc = w1g.shape

    def body(x_ref, g_ref, w_ref, b_ref, hbt_ref, a_ref):
        hb = _rms(x_ref[...], g_ref[...])[0].astype(BF16)
        hbt_ref[...] = hb.T
        for s in range(ns):
            a_ref[:, s * nc:(s + 1) * nc] = _dot(hb, w_ref[s]) + b_ref[:, s * nc:(s + 1) * nc]

    return _pcall(body, name="conv_in", grid=(T // tm,),
                  in_specs=[_rows(tm, D), _full((1, D)), _full(w1g.shape), _full((1, ns * nc))],
                  out_specs=[_cols(D, tm), _rows(tm, ns * nc)],
                  out_shape=[S_((T // tm, D, tm), BF16), S_((T, ns * nc), F32)], args=(x, g, w1g, b1), comm=comm)


def _conv_mid(a, x, wdw, bdw, lng, lnb, w2, b2, gn, tm, tps, taps, comm=None):
    T, D = x.shape
    hc = HALO_CONV

    def body(a_ref, ah_ref, x_ref, wdw_ref, bdw_ref, lng_ref, lnb_ref, w2_ref, b2_ref, gn_ref,
             c_ref, sbt_ref, x1_ref, hb_ref, hbt_ref, uext, shifted):
        i = pl.program_id(0)
        ah = ah_ref[...]
        uext[0:hc, :] = jnp.where(i % tps == 0, 0.0, ah[:, :D] * _sig(ah[:, D:]))
        av = a_ref[...]
        uext[hc:hc + tm, :] = av[:, :D] * _sig(av[:, D:])

        def conv(r0, src, ks, first):
            acc = jnp.broadcast_to(bdw_ref[...], (CHUNK, D)) if first else c_ref[pl.ds(r0, CHUNK), :]
            for k, o in ks:
                acc = acc + wdw_ref[k:k + 1, :] * src[pl.ds(pl.multiple_of(r0 + o, 8), CHUNK), :]
            c_ref[pl.ds(r0, CHUNK), :] = acc
        _by_sublane_shift(uext, shifted, tm, [hc - (taps - 1) + k for k in range(taps)], conv)

        cv = c_ref[...]
        xc = cv - jnp.mean(cv, axis=-1, keepdims=True)
        nh = xc * lax.rsqrt(jnp.mean(xc * xc, axis=-1, keepdims=True) + LN_EPS)
        n = nh * lng_ref[...] + lnb_ref[...]
        sb = (n * _sig(n)).astype(BF16)
        sbt_ref[...] = sb.T
        x1 = x_ref[...] + _dot(sb, w2_ref[...]) + b2_ref[...]
        x1_ref[...] = x1
        hb = _rms(x1, gn_ref[...])[0].astype(BF16)
        hb_ref[...] = hb
        hbt_ref[...] = hb.T

    return _pcall(body, name="conv_mid", grid=(T // tm,),
                  in_specs=[_rows(tm, 2 * D), _prev_halo(tm, hc, 2 * D), _rows(tm, D), _full(wdw.shape), _full((1, D)),
                            _full((1, D)), _full((1, D)), _full((D, D)), _full((1, D)), _full((1, D))],
                  out_specs=[_rows(tm, D), _cols(D, tm), _rows(tm, D), _rows(tm, D), _cols(D, tm)],
                  out_shape=[S_((T, D), F32), S_((T // tm, D, tm), BF16), S_((T, D), F32), S_((T, D), BF16), S_((T // tm, D, tm), BF16)],
                  scratch=[pltpu.VMEM((tm + hc, D), F32), pltpu.VMEM((tm + hc - 8, D), F32)],
                  args=(a, a, x, wdw, bdw, lng, lnb, w2, b2, gn), comm=comm)


def _ffn_taps(aext, realigned, tm, ffn_taps):
    for k in range(ffn_taps - 1):
        realigned[k] = aext[pl.ds(HALO_FFN - (ffn_taps - 1) + k, tm), :]
    return [realigned.at[k] for k in range(ffn_taps - 1)] + [aext.at[pl.ds(HALO_FFN, tm)]]


def _ffn_conv(taps, wdw_ref, bdw_ref):
    acc = bdw_ref[...]
    for k, tap in enumerate(taps):
        acc = acc + wdw_ref[k:k + 1, :] * tap[...]
    return acc


def _by_sublane_shift(ext, shifted, tm, offsets, per_chunk):
    first = True
    for b in range(8):
        ks = [(k, o - b) for k, o in enumerate(offsets) if o % 8 == b]
        if not ks:
            continue
        if b:
            shifted[...] = ext[pl.ds(b, shifted.shape[0]), :]
        _chunks(tm, functools.partial(per_chunk, src=shifted if b else ext, ks=ks, first=first))
        first = False


def _ffn_fwd(hb, xin, wg, wdw, bdw, wd, gn, tm, tps, ffn_taps, name, target=None, comm=None):
    T, D = xin.shape
    ns, _, nc = wg.shape
    F = ns * nc // 2
    hf = HALO_FFN
    hh = 2 * HALO_FFN
    final = target is not None

    def body(*refs):
        if final:
            (h_ref, hh_ref, x_ref, wg_ref, wdw_ref, bdw_ref, wd_ref, gn_ref, tg_ref,
             up_ref, gb_ref, dx_ref, dxb_ref, loss_ref, dgn_ref, aext, realigned) = refs
        else:
            (h_ref, hh_ref, x_ref, wg_ref, wdw_ref, bdw_ref, wd_ref, gn_ref,
             up_ref, gb_ref, xo_ref, ho_ref, aext, realigned) = refs
        i = pl.program_id(0)
        hv = h_ref[...]
        for s in range(ns):
            up_ref[:, s * nc:(s + 1) * nc] = _dot(hv, wg_ref[s])
        halo = jnp.concatenate([_dot(hh_ref[...], wg_ref[s]) for s in range(ns // 2)], axis=1)[hh - hf:hh, :]
        aext[0:hf, :] = jnp.where(i % tps == 0, 0.0, halo)
        aext[hf:hf + tm, :] = up_ref[:, 0:F]
        acv = _ffn_conv(_ffn_taps(aext, realigned, tm, ffn_taps), wdw_ref, bdw_ref)
        gb = (acv * _sig(acv) * up_ref[:, F:2 * F]).astype(BF16)
        gb_ref[...] = gb.T
        xo = x_ref[...] + _dot(gb, wd_ref[...])
        if not final:
            xo_ref[...] = xo
            ho_ref[...] = _rms(xo, gn_ref[...])[0]
        else:
            out, xh, r = _rms(xo, gn_ref[...])
            diff = out - tg_ref[...]
            part = 0.5 / D * _colsum(diff * diff)
            dx, dg = _rms_bwd(diff * (1.0 / D), xh, r, gn_ref[...])
            dx_ref[...] = dx
            dxb_ref[...] = dx.astype(BF16)
            _accumulate(loss_ref, part, i == 0)
            _accumulate(dgn_ref, dg, i == 0)

    in_specs = [_rows(tm, D), _prev_halo(tm, hh, D), _rows(tm, D), _full(wg.shape), _full(wdw.shape), _full((1, F)),
                _full((F, D)), _full((1, D))]
    args = [hb, hb, xin, wg, wdw, bdw, wd, gn]
    if final:
        in_specs.append(_rows(tm, D))
        args.append(target)
        out_specs = [_rows(tm, 2 * F), _cols(F, tm), _rows(tm, D), _rows(tm, D), _acc_spec(1, D), _acc_spec(1, D)]
        out_shape = [S_((T, 2 * F), F32), S_((T // tm, F, tm), BF16), S_((T, D), F32), S_((T, D), BF16), S_((1, D), F32),
                     S_((1, D), F32)]
    else:
        out_specs = [_rows(tm, 2 * F), _cols(F, tm), _rows(tm, D), _rows(tm, D)]
        out_shape = [S_((T, 2 * F), F32), S_((T // tm, F, tm), BF16), S_((T, D), F32), S_((T, D), F32)]
    return _pcall(body, name=name, grid=(T // tm,), in_specs=in_specs, out_specs=out_specs, out_shape=out_shape,
                  scratch=[pltpu.VMEM((tm + hf, F), F32), pltpu.VMEM((ffn_taps - 1, tm, F), F32)], args=args, comm=comm)


def _pool_counts(i, tps, tm, w):
    pos = (i % tps) * tm + lax.broadcasted_iota(jnp.int32, (tm, 1), 0)
    return jnp.minimum(pos + 1, w).astype(F32)


def _window_sums(src, buf_a, buf_b, cg, causal):
    assert POOL_WINDOWS == (2, 4, 8, 16)
    n = src.shape[0]
    levels = len(POOL_WINDOWS)
    cur, outs = src, []
    for l in range(levels):
        dst = buf_b if l % 2 else buf_a
        cols = slice(l * cg, levels * cg)
        lo, m = 8 * (l + 1), n - 8 * (l + 1)
        if causal:
            dst[lo:n, cols] = cur[lo:n, cols] + cur[pl.ds(lo - (1 << l), m), cols]
        else:
            dst[0:m, cols] = cur[0:m, cols] + cur[pl.ds(1 << l, m), cols]
        outs.append(dst)
        cur = dst
    return outs


def _pool_fwd(h, x2, pw, pb, ps, gn, tm, tps, comm=None):
    T, D = h.shape
    ng, cg, _ = pw.shape
    hp = HALO_POOL

    def body(h_ref, hh_ref, x_ref, pw_ref, pb_ref, ps_ref, gn_ref, pl_ref, xo_ref, hb_ref, hbt_ref, hext, buf_a, buf_b):
        i = pl.program_id(0)
        hext[0:hp, :] = jnp.where(i % tps == 0, 0.0, hh_ref[...])
        hext[hp:hp + tm, :] = h_ref[...]
        sums = _window_sums(hext, buf_a, buf_b, cg, True)
        for g, w in enumerate(POOL_WINDOWS):
            cs = slice(g * cg, (g + 1) * cg)
            pl_ref[:, cs] = (sums[g][hp:hp + tm, cs] / _pool_counts(i, tps, tm, w) - h_ref[:, cs]).astype(BF16)

        pv = pl_ref[...]
        mixed = jnp.concatenate([_dot(pv[:, g * cg:(g + 1) * cg], pw_ref[g]) for g in range(ng)], axis=1)
        xo = x_ref[...] + ps_ref[...] * (mixed + pb_ref[...])
        xo_ref[...] = xo
        hb = _rms(xo, gn_ref[...])[0].astype(BF16)
        hb_ref[...] = hb
        hbt_ref[...] = hb.T

    return _pcall(body, name="pool_fwd", grid=(T // tm,),
                  in_specs=[_rows(tm, D), _prev_halo(tm, hp, D), _rows(tm, D), _full(pw.shape), _full((1, D)), _full((1, D)),
                            _full((1, D))],
                  out_specs=[_rows(tm, D), _rows(tm, D), _rows(tm, D), _cols(D, tm)],
                  out_shape=[S_((T, D), BF16), S_((T, D), F32), S_((T, D), BF16), S_((T // tm, D, tm), BF16)],
                  scratch=[pltpu.VMEM((tm + hp, D), F32)] * 3, args=(h, h, x2, pw, pb, ps, gn), comm=comm)


def _bwd_down(dx, up, wdw, bdw, wd, tm, tps, ffn_taps, name, comm=None):
    T, D = dx.shape
    F = up.shape[1] // 2
    hf = HALO_FFN
    nt = T // tm

    def body(dx_ref, up_ref, uph_ref, wdw_ref, bdw_ref, wd_ref, dac_ref, dgate_ref, wsum_ref, aext, realigned):
        i = pl.program_id(0)
        aext[0:hf, :] = jnp.where(i % tps == 0, 0.0, uph_ref[...])
        aext[hf:hf + tm, :] = up_ref[:, 0:F]
        taps = _ffn_taps(aext, realigned, tm, ffn_taps)
        acv = _ffn_conv(taps, wdw_ref, bdw_ref)
        dg = _dot_nt(dx_ref[...].astype(BF16), wd_ref[...])
        sg = _sig(acv)
        dgate_ref[...] = (dg * acv * sg).astype(BF16)
        dac = dg * up_ref[:, F:2 * F] * (sg * (1.0 + acv * (1.0 - sg)))
        dac_ref[...] = dac
        rows = [_colsum(dac * tap[...]) for tap in taps]
        rows += [_colsum(dac), jnp.zeros((8 - ffn_taps - 1, F), F32)]
        _accumulate(wsum_ref, jnp.concatenate(rows, axis=0), i == 0)

    return _pcall(body, name=name, grid=(nt,),
                  in_specs=[_rows(tm, D), _rows(tm, 2 * F), _prev_halo(tm, hf, F), _full(wdw.shape), _full((1, F)),
                            _full((F, D))],
                  out_specs=[_rows(tm, F), _rows(tm, F, col=1), _acc_spec(8, F)],
                  out_shape=[S_((T, F), F32), S_((T, 2 * F), BF16), S_((8, F), F32)],
                  scratch=[pltpu.VMEM((tm + hf, F), F32), pltpu.VMEM((ffn_taps - 1, tm, F), F32)],
                  args=(dx, up, up, wdw, bdw, wd), comm=comm)


def _bwd_up(dac, dup, wg, wdw, xin, gn, dx, tm, tps, ffn_taps, name, comm=None):
    T, D = xin.shape
    F = dac.shape[1]
    ns, _, nc = wg.shape
    hf = HALO_FFN
    nt = T // tm

    def body(dac_ref, dach_ref, dgate_ref, wg_ref, wdw_ref, x_ref, gn_ref, dx_ref, dact_ref, dxo_ref, dxb_ref, dgn_ref, dext):
        i = pl.program_id(0)
        dext[0:tm, :] = dac_ref[...]
        dext[tm:tm + hf, :] = jnp.where(i % tps == tps - 1, 0.0, dach_ref[...])
        dact = wdw_ref[ffn_taps - 1:ffn_taps, :] * dac_ref[...]
        for k in range(ffn_taps - 1):
            dact = dact + wdw_ref[k:k + 1, :] * dext[pl.ds(ffn_taps - 1 - k, tm), :]
        dact_ref[...] = dact.astype(BF16)

        dh = jnp.zeros((tm, D), F32)
        for s in range(ns):
            src = dact_ref if s < ns // 2 else dgate_ref
            o = (s % (ns // 2)) * nc
            dh = dh + _dot_nt(src[:, o:o + nc], wg_ref[s])
        _, xh, r = _rms(x_ref[...], gn_ref[...])
        dxn, dg = _rms_bwd(dh, xh, r, gn_ref[...])
        dxo = dx_ref[...] + dxn
        dxo_ref[...] = dxo
        dxb_ref[...] = dxo.astype(BF16)
        _accumulate(dgn_ref, dg, i == 0)

    return _pcall(body, name=name, grid=(nt,),
                  in_specs=[_rows(tm, F), _next_halo(tm, hf, F, T), _rows(tm, F, col=1), _full(wg.shape), _full(wdw.shape),
                            _rows(tm, D), _full((1, D)), _rows(tm, D)],
                  out_specs=[_rows(tm, F), _rows(tm, D), _rows(tm, D), _acc_spec(1, D)],
                  out_shape=[S_((T, 2 * F), BF16), S_((T, D), F32), S_((T, D), BF16), S_((1, D), F32)],
                  scratch=[pltpu.VMEM((tm + hf, F), F32)],
                  aliases={2: 0}, args=(dac, dac, dup, wg, wdw, xin, gn, dx), comm=comm)


def _mm_tn(xt, dy, bk, bn, shard_major, name, comm=None):
    nt, K, tt = xt.shape
    T, N = dy.shape
    assert nt * tt == T

    def body(x_ref, dy_ref, o_ref):
        x = jnp.concatenate([x_ref[t] for t in range(nt)], axis=1)
        o_ref[...] = _dot(x, dy_ref[...]).astype(o_ref.dtype)

    if shard_major:
        out_spec = pl.BlockSpec((None, bk, bn), lambda i, j: (j, i, 0))
        out_shape = S_((N // bn, K, bn), BF16)
    else:
        out_spec = pl.BlockSpec((bk, bn), lambda i, j: (i, j))
        out_shape = S_((K, N), BF16)
    return _pcall(body, name=name, grid=(K // bk, N // bn),
                  in_specs=[pl.BlockSpec((nt, bk, tt), lambda i, j: (0, i, 0)), pl.BlockSpec((T, bn), lambda i, j: (0, j))],
                  out_specs=out_spec, out_shape=out_shape, args=(xt, dy), comm=comm)


def _pool_bwd(dx3, plb, x2, pw, pb, ps, gn, tm, tps, comm=None):
    T, D = x2.shape
    ng, cg, _ = pw.shape
    hp = HALO_POOL
    nt = T // tm

    def body(do_ref, doh_ref, pl_ref, x_ref, pw_ref, pb_ref, ps_ref, gn_ref, dxo_ref, dxb_ref, dpw_ref, sm_ref, qext, buf_a,
             buf_b, dh_s):
        i = pl.program_id(0)
        do = do_ref[...]
        dm = do * ps_ref[...]
        dmh = jnp.where(i % tps == tps - 1, 0.0, doh_ref[...]) * ps_ref[...]
        pv = pl_ref[...]
        mixed = jnp.concatenate([_dot(pv[:, g * cg:(g + 1) * cg], pw_ref[g]) for g in range(ng)], axis=1)
        dscale = _colsum(do * (mixed + pb_ref[...]))
        dbias = _colsum(dm)
        dmb = dm.astype(BF16)
        dmhb = dmh.astype(BF16)
        for g, w in enumerate(POOL_WINDOWS):
            cs = slice(g * cg, (g + 1) * cg)
            _accumulate(dpw_ref.at[g], _dot_tn(pv[:, cs], dmb[:, cs]), i == 0)
            dpo = _dot_nt(dmb[:, cs], pw_ref[g])
            dh_s[:, cs] = dpo
            qext[0:tm, cs] = dpo / _pool_counts(i, tps, tm, w)
            qext[tm:tm + hp, cs] = _dot_nt(dmhb[:, cs], pw_ref[g]) * (1.0 / w)
        sums = _window_sums(qext, buf_a, buf_b, cg, False)
        for g in range(ng):
            cs = slice(g * cg, (g + 1) * cg)
            dh_s[:, cs] = sums[g][0:tm, cs] - dh_s[:, cs]

        _, xh, r = _rms(x_ref[...], gn_ref[...])
        dxn, dg = _rms_bwd(dh_s[...], xh, r, gn_ref[...])
        dxo = do + dxn
        dxo_ref[...] = dxo
        dxb_ref[...] = dxo.astype(BF16)
        sm = jnp.concatenate([dscale, dbias, dg, jnp.zeros((5, D), F32)], axis=0)
        _accumulate(sm_ref, sm, i == 0)

    return _pcall(body, name="pool_bwd", grid=(nt,),
                  in_specs=[_rows(tm, D), _next_halo(tm, hp, D, T), _rows(tm, D), _rows(tm, D), _full(pw.shape), _full((1, D)),
                            _full((1, D)), _full((1, D))],
                  out_specs=[_rows(tm, D), _rows(tm, D), pl.BlockSpec((ng, cg, cg), lambda i: (0, 0, 0)), _acc_spec(8, D)],
                  out_shape=[S_((T, D), F32), S_((T, D), BF16), S_((ng, cg, cg), F32), S_((8, D), F32)],
                  scratch=[pltpu.VMEM((tm + hp, D), F32)] * 3 + [pltpu.VMEM((tm, D), F32)],
                  args=(dx3, dx3, plb, x2, pw, pb, ps, gn), comm=comm)


def _conv_bwd1(dx1, c, w2, lng, lnb, tm, comm=None):
    T, D = dx1.shape

    def body(do_ref, c_ref, w2_ref, lng_ref, lnb_ref, dc_ref, sm_ref):
        i = pl.program_id(0)
        do = do_ref[...]
        ds = _dot_nt(do.astype(BF16), w2_ref[...])
        cv = c_ref[...]
        xc = cv - jnp.mean(cv, axis=-1, keepdims=True)
        rstd = lax.rsqrt(jnp.mean(xc * xc, axis=-1, keepdims=True) + LN_EPS)
        nh = xc * rstd
        n = nh * lng_ref[...] + lnb_ref[...]
        sg = _sig(n)
        dn = ds * (sg * (1.0 + n * (1.0 - sg)))
        dnh = dn * lng_ref[...]
        dc = rstd * (dnh - jnp.mean(dnh, axis=-1, keepdims=True) - nh * jnp.mean(dnh * nh, axis=-1, keepdims=True))
        dc_ref[...] = dc
        sm = jnp.concatenate([_colsum(dn * nh), _colsum(dn), _colsum(do), _colsum(dc), jnp.zeros((4, D), F32)], axis=0)
        _accumulate(sm_ref, sm, i == 0)

    return _pcall(body, name="conv_bwd1", grid=(T // tm,),
                  in_specs=[_rows(tm, D), _rows(tm, D), _full((D, D)), _full((1, D)), _full((1, D))],
                  out_specs=[_rows(tm, D), _acc_spec(8, D)],
                  out_shape=[S_((T, D), F32), S_((8, D), F32)], args=(dx1, c, w2, lng, lnb), comm=comm)


def _conv_bwd2(dc, a, x, w1g, wdw, gn, dx1, tm, tps, taps, comm=None):
    T, D = x.shape
    ns, _, nc = w1g.shape
    hc = HALO_CONV
    nt = T // tm
    tp = wdw.shape[0]

    def body(dc_ref, dch_ref, a_ref, ah_ref, x_ref, w_ref, wdw_ref, gn_ref, dx1_ref,
             gx_ref, dab_ref, dw_ref, db1_ref, dgn_ref, uext, dext, shifted, du_s, wacc):
        i = pl.program_id(0)
        ah = ah_ref[...]
        uext[0:hc, :] = jnp.where(i % tps == 0, 0.0, ah[:, :D] * _sig(ah[:, D:]))
        av = a_ref[...]
        sg2 = _sig(av[:, D:])
        uext[hc:hc + tm, :] = av[:, :D] * sg2
        dext[0:tm, :] = dc_ref[...]
        dext[tm:tm + hc, :] = jnp.where(i % tps == tps - 1, 0.0, dch_ref[...])

        @pl.when(i == 0)
        def _():
            wacc[...] = jnp.zeros_like(wacc)

        def wgrad(r0, src, ks, first):
            d = dc_ref[pl.ds(r0, CHUNK), :]
            for k, o in ks:
                wacc[8 * k:8 * k + 8, :] += _fold8(d * src[pl.ds(pl.multiple_of(r0 + o, 8), CHUNK), :])
        _by_sublane_shift(uext, shifted, tm, [hc - (taps - 1) + k for k in range(taps)], wgrad)

        def convt(r0, src, ks, first):
            acc = jnp.zeros((CHUNK, D), F32) if first else du_s[pl.ds(r0, CHUNK), :]
            for k, o in ks:
                acc = acc + wdw_ref[k:k + 1, :] * src[pl.ds(pl.multiple_of(r0 + o, 8), CHUNK), :]
            du_s[pl.ds(r0, CHUNK), :] = acc
        _by_sublane_shift(dext, shifted, tm, [taps - 1 - k for k in range(taps)], convt)

        du = du_s[...]
        da1 = du * sg2
        da2 = du * av[:, :D] * (sg2 * (1.0 - sg2))
        dab_ref[:, 0:D] = da1.astype(BF16)
        dab_ref[:, D:2 * D] = da2.astype(BF16)
        _accumulate(db1_ref, jnp.concatenate([_colsum(da1), _colsum(da2)], axis=0), i == 0)
        dh = jnp.zeros((tm, D), F32)
        for s in range(ns):
            dh = dh + _dot_nt(dab_ref[:, s * nc:(s + 1) * nc], w_ref[s])
        _, xh, r = _rms(x_ref[...], gn_ref[...])
        dxn, dg = _rms_bwd(dh, xh, r, gn_ref[...])
        gx_ref[...] = dx1_ref[...] + dxn
        _accumulate(dgn_ref, dg, i == 0)

        @pl.when(i == nt - 1)
        def _():
            dw_ref[...] = jnp.zeros_like(dw_ref)
            for k in range(taps):
                dw_ref[k:k + 1, :] = _colsum(wacc[8 * k:8 * k + 8, :])

    return _pcall(body, name="conv_bwd2", grid=(nt,),
                  in_specs=[_rows(tm, D), _next_halo(tm, hc, D, T), _rows(tm, 2 * D), _prev_halo(tm, hc, 2 * D), _rows(tm, D),
                            _full(w1g.shape), _full(wdw.shape), _full((1, D)), _rows(tm, D)],
                  out_specs=[_rows(tm, D), _rows(tm, 2 * D), _acc_spec(tp, D), _acc_spec(2, D), _acc_spec(1, D)],
                  out_shape=[S_((T, D), F32), S_((T, 2 * D), BF16), S_((tp, D), F32), S_((2, D), F32), S_((1, D), F32)],
                  scratch=[pltpu.VMEM((tm + hc, D), F32), pltpu.VMEM((tm + hc, D), F32), pltpu.VMEM((tm + hc - 8, D), F32),
                           pltpu.VMEM((tm, D), F32), pltpu.VMEM((8 * taps, D), F32)],
                  args=(dc, dc, a, a, x, w1g, wdw, gn, dx1), comm=comm)


def _place():
    x, y, c = lax.axis_index("x"), lax.axis_index("y"), lax.axis_index("c")
    chips = [(1 - x, y), (x, 1 - y), (1 - x, 1 - y)]
    return x, y, c, chips


class _GatherComm:
    def __init__(self, slots):
        n = len(slots)
        self.inputs = list(slots)
        self.out_shape = [S_(s.shape, s.dtype) for s in slots]
        self.aliases = {i: i for i in range(n)}
        self.sems = [pltpu.SemaphoreType.DMA((n, 3))] * 4
        self.halves = [s.shape[1] // 2 for s in slots]
        self.paired = [s.shape[0] == N_CHIPS // 2 for s in slots]
        self.widths = [s.shape[2] // 2 if p else s.shape[2] for s, p in zip(slots, self.paired)]

    def _copies(self, outs, sems):
        s_ici, r_ici, s_d2d, r_d2d = sems
        x, y, c, chips = _place()
        peers = [(j, chip, 2 * chip[0] + chip[1]) for j, chip in enumerate(chips)]

        def block(i, chip, h):
            rows = pl.ds(pl.multiple_of(h * self.halves[i], 8), self.halves[i])
            if self.paired[i]:
                return outs[i].at[chip // 2, rows, pl.ds(pl.multiple_of((chip % 2) * self.widths[i], LANES), self.widths[i])]
            return outs[i].at[chip, rows]

        def ici(i, j, chip, to):
            blk = block(i, chip, c)
            return pltpu.make_async_remote_copy(src_ref=blk, dst_ref=blk, send_sem=s_ici.at[i, j], recv_sem=r_ici.at[i, j],
                                                device_id=to, device_id_type=MESH)

        def d2d(i, j, chip, h):
            blk = block(i, chip, h)
            return pltpu.make_async_remote_copy(src_ref=blk, dst_ref=blk, send_sem=s_d2d.at[i, j], recv_sem=r_d2d.at[i, j],
                                                device_id=(x, y, 1 - c), device_id_type=MESH)
        return (x, y, c, 2 * x + y), peers, ici, d2d

    def before(self, cin, outs, sems, step, total):
        (x, y, c, q), peers, ici, d2d = self._copies(outs, sems)

        @pl.when(step == 0)
        def _():
            for i in range(len(outs)):
                for j, chip, _ in peers:
                    ici(i, j, q, (*chip, c)).start()

    def after(self, cin, outs, sems, step, total):
        (x, y, c, q), peers, ici, d2d = self._copies(outs, sems)

        @pl.when(step == max(total - 2, 0))
        def _():
            for i in range(len(outs)):
                for j, chip, qj in peers:
                    ici(i, j, qj, (x, y, c)).wait_recv()
                    d2d(i, j, qj, c).start()

        @pl.when(step == total - 1)
        def _():
            for i in range(len(outs)):
                for j, chip, qj in peers:
                    d2d(i, j, qj, 1 - c).wait_recv()
            for i in range(len(outs)):
                for j, chip, qj in peers:
                    ici(i, j, q, (*chip, c)).wait_send()
                    d2d(i, j, qj, c).wait_send()


class _ExchangeComm:
    def __init__(self, sums):
        n = len(sums)
        self.inputs = list(sums)
        self.out_shape = [S_((3,) + s.shape[1:], s.dtype) for s in sums]
        self.aliases = {}
        self.sems = [pltpu.SemaphoreType.DMA((n, 3))] * 2

    def _copies(self, srcs, outs, sems):
        s_sem, r_sem = sems
        x, y, c, chips = _place()
        return [pltpu.make_async_remote_copy(src_ref=srcs[i].at[2 * chip[0] + chip[1]], dst_ref=outs[i].at[j],
                                             send_sem=s_sem.at[i, j], recv_sem=r_sem.at[i, j],
                                             device_id=(*chip, c), device_id_type=MESH)
                for i in range(len(srcs)) for j, chip in enumerate(chips)]

    def before(self, srcs, outs, sems, step, total):
        @pl.when(step == 0)
        def _():
            for cp in self._copies(srcs, outs, sems):
                cp.start()

    def after(self, srcs, outs, sems, step, total):
        @pl.when(step == total - 1)
        def _():
            for cp in self._copies(srcs, outs, sems):
                cp.wait()


def _own_slots(specs):
    n = len(specs)
    shapes, blocks, count = [], [], []
    for a, (w, layer, dtype, pairs) in enumerate(specs):
        _, r, c = w.shape
        br = _row_block(r, c, 2 * BLOCK_BYTES)
        shapes.append(S_((N_CHIPS // 2, r, 2 * c) if pairs else (N_CHIPS, r, c), dtype))
        blocks += [(a, k, k * br, br) for k in range(r // br)]
        count.append(r // br)
    gather = _GatherComm(shapes[:1])
    n_sems = len(gather.sems)

    def body(*refs):
        srcs, outs = refs[:n], refs[n:2 * n]
        inb, outb = refs[2 * n:3 * n], refs[3 * n:4 * n]
        lsem, ssem = refs[4 * n], refs[4 * n + 1]
        gsems = refs[4 * n + 2:4 * n + 2 + n_sems]
        x, y, c, _ = _place()
        q = 2 * x + y

        def load(a, k, r0, br):
            return pltpu.make_async_copy(srcs[a].at[specs[a][1], pl.ds(r0, br)], inb[a].at[k % 2], lsem.at[a, k % 2])

        def store(a, k, r0, br):
            if specs[a][3]:
                wc = specs[a][0].shape[2]
                dst = outs[a].at[q // 2, pl.ds(r0, br), pl.ds(pl.multiple_of((q % 2) * wc, LANES), wc)]
            else:
                dst = outs[a].at[q, pl.ds(r0, br)]
            return pltpu.make_async_copy(outb[a].at[k % 2], dst, ssem.at[a, k % 2])

        load(*blocks[0]).start()
        for b, (a, k, r0, br) in enumerate(blocks):
            if b + 1 < len(blocks):
                load(*blocks[b + 1]).start()
            load(a, k, r0, br).wait()
            if k >= 2:
                store(a, k - 2, r0 - 2 * br, br).wait()
            outb[a][k % 2] = inb[a][k % 2].astype(specs[a][2])
            store(a, k, r0, br).start()
            if a == 0 and k == count[0] - 1:
                for kk in range(max(k - 1, 0), k + 1):
                    store(0, kk, kk * br, br).wait()
                gather.before(None, outs[:1], gsems, 0, 1)
        for a in range(1, n):
            br = blocks[sum(count[:a])][3]
            for kk in range(max(count[a] - 2, 0), count[a]):
                store(a, kk, kk * br, br).wait()
        gather.after(None, outs[:1], gsems, 0, 1)

    any_spec = pl.BlockSpec(memory_space=pl.ANY)
    scratch = ([pltpu.VMEM((2, blocks[sum(count[:a])][3], specs[a][0].shape[2]), F32) for a in range(n)]
               + [pltpu.VMEM((2, blocks[sum(count[:a])][3], specs[a][0].shape[2]), specs[a][2]) for a in range(n)]
               + [pltpu.SemaphoreType.DMA((n, 2)), pltpu.SemaphoreType.DMA((n, 2))] + list(gather.sems))
    outs = pl.pallas_call(body, name="own_slots", in_specs=[any_spec] * n, out_specs=[any_spec] * n, out_shape=shapes,
                          scratch_shapes=scratch,
                          compiler_params=pltpu.CompilerParams(vmem_limit_bytes=VMEM_LIMIT_BYTES))(*[s[0] for s in specs])
    return list(outs)


def _comm_only(comm, name):
    def body():
        pass
    outs = _pcall(body, name=name, grid=(1,), in_specs=[], out_specs=[], out_shape=[], args=[], comm=comm)
    return list(outs)


def _pair_exchange(parts, name):
    n = len(parts)

    def body(*refs):
        srcs, outs = refs[:n], refs[n:2 * n]
        s_sem, r_sem = refs[2 * n:]
        x, y, c, _ = _place()
        cps = []
        for i in range(n):
            for s in range(N_CHIPS):
                cp = pltpu.make_async_remote_copy(src_ref=srcs[i].at[s, 1 - c], dst_ref=outs[i].at[s], send_sem=s_sem.at[i, s],
                                                  recv_sem=r_sem.at[i, s], device_id=(x, y, 1 - c), device_id_type=MESH)
                cp.start()
                cps.append(cp)
        for cp in cps:
            cp.wait()

    any_spec = pl.BlockSpec(memory_space=pl.ANY)
    outs = pl.pallas_call(
        body, name=name, in_specs=[any_spec] * n, out_specs=[any_spec] * n,
        out_shape=[S_((N_CHIPS,) + p.shape[2:], p.dtype) for p in parts],
        scratch_shapes=[pltpu.SemaphoreType.DMA((n, N_CHIPS))] * 2,
    )(*parts)
    return list(outs)


class _PairComm:
    def __init__(self, parts):
        n = len(parts)
        self.inputs = list(parts)
        self.out_shape = [S_((N_CHIPS,) + p.shape[2:], p.dtype) for p in parts]
        self.aliases = {}
        self.sems = [pltpu.SemaphoreType.DMA((n, N_CHIPS))] * 2

    def _copies(self, srcs, outs, sems):
        s_sem, r_sem = sems
        x, y, c, _ = _place()
        return [pltpu.make_async_remote_copy(src_ref=srcs[i].at[s, 1 - c], dst_ref=outs[i].at[s], send_sem=s_sem.at[i, s],
                                             recv_sem=r_sem.at[i, s], device_id=(x, y, 1 - c), device_id_type=MESH)
                for i in range(len(srcs)) for s in range(N_CHIPS)]

    def before(self, srcs, outs, sems, step, total):
        @pl.when(step == 0)
        def _():
            for cp in self._copies(srcs, outs, sems):
                cp.start()

    def after(self, srcs, outs, sems, step, total):
        @pl.when(step == total - 1)
        def _():
            for cp in self._copies(srcs, outs, sems):
                cp.wait()


class _ShareComm:
    def __init__(self, bufs):
        n = len(bufs)
        self.inputs = list(bufs)
        self.out_shape = [S_(b.shape, b.dtype) for b in bufs]
        self.aliases = {i: i for i in range(n)}
        self.sems = [pltpu.SemaphoreType.DMA((n, max(b.shape[0] for b in bufs)))] * 2
        self.layers = [b.shape[0] for b in bufs]

    def _copies(self, outs, sems):
        s_sem, r_sem = sems
        x, y, c, _ = _place()
        cps = []
        for i, nl in enumerate(self.layers):
            for l in range(nl):
                blk = outs[i].at[l, c]
                cps.append(pltpu.make_async_remote_copy(src_ref=blk, dst_ref=blk, send_sem=s_sem.at[i, l],
                                                        recv_sem=r_sem.at[i, l], device_id=(x, y, 1 - c),
                                                        device_id_type=MESH))
        return cps

    def before(self, cin, outs, sems, step, total):
        @pl.when(step == 0)
        def _():
            for cp in self._copies(outs, sems):
                cp.start()

    def after(self, cin, outs, sems, step, total):
        @pl.when(step == total - 1)
        def _():
            for cp in self._copies(outs, sems):
                cp.wait()


class _Both:
    def __init__(self, a, b):
        self.parts = (a, b)
        self.inputs = a.inputs + b.inputs
        self.out_shape = a.out_shape + b.out_shape
        self.aliases = dict(a.aliases)
        self.aliases.update({len(a.inputs) + i: len(a.out_shape) + o for i, o in b.aliases.items()})
        self.sems = a.sems + b.sems

    def _each(self, phase, cin, cout, sems, step, total):
        i = o = s = 0
        for p in self.parts:
            ni, no, ns = len(p.inputs), len(p.out_shape), len(p.sems)
            getattr(p, phase)(cin[i:i + ni], cout[o:o + no], sems[s:s + ns], step, total)
            i, o, s = i + ni, o + no, s + ns

    def before(self, cin, cout, sems, step, total):
        self._each("before", cin, cout, sems, step, total)

    def after(self, cin, cout, sems, step, total):
        self._each("after", cin, cout, sems, step, total)


def _small_allreduce(groups):
    n = len(groups)
    arrays = []
    for grp in groups:
        for p in grp:
            if not isinstance(p, int) and not any(p[0] is a for a in arrays):
                arrays.append(p[0])
    shapes = []
    for grp in groups:
        rows = sum(p if isinstance(p, int) else p[2] - p[1] for p in grp)
        width = next(p[0].shape[1] for p in grp if not isinstance(p, int))
        assert rows % 8 == 0
        shapes.append((rows, width))
    na = len(arrays)

    def body(*refs):
        a_refs, o_refs = refs[:na], refs[na:na + n]
        rest = refs[na + n:]
        v_refs, sibs, css, gots = rest[:n], rest[n:2 * n], rest[2 * n:3 * n], rest[3 * n:4 * n]
        s_sem, r_sem = rest[4 * n:]
        for i, grp in enumerate(groups):
            o = 0
            for p in grp:
                if isinstance(p, int):
                    v_refs[i][o:o + p, :] = jnp.zeros((p, shapes[i][1]), F32)
                    o += p
                else:
                    k = next(t for t, a in enumerate(arrays) if a is p[0])
                    v_refs[i][o:o + p[2] - p[1], :] = a_refs[k][p[1]:p[2], :]
                    o += p[2] - p[1]
        x, y, c, chips = _place()
        q = 2 * x + y
        to_sib = [pltpu.make_async_remote_copy(src_ref=v_refs[i], dst_ref=sibs[i], send_sem=s_sem.at[i, 3], recv_sem=r_sem.at[i, 3],
                                               device_id=(x, y, 1 - c), device_id_type=MESH) for i in range(n)]
        for cp in to_sib:
            cp.start()
        cps = []
        for i in range(n):
            to_sib[i].wait()
            mine, other = v_refs[i][...], sibs[i][...]
            css[i][...] = jnp.where(c == 0, mine, other) + jnp.where(c == 0, other, mine)
            for j, chip in enumerate(chips):
                cp = pltpu.make_async_remote_copy(src_ref=css[i], dst_ref=gots[i].at[j], send_sem=s_sem.at[i, j],
                                                  recv_sem=r_sem.at[i, j], device_id=(*chip, c), device_id_type=MESH)
                cp.start()
                cps.append(cp)
        for cp in cps:
            cp.wait()
        flips = [2, 1, 3]
        for i in range(n):
            total = None
            for k in range(N_CHIPS):
                d = q ^ k
                term = jnp.where(d == 0, css[i][...], 0.0)
                for j in range(3):
                    term = jnp.where(d == flips[j], gots[i][j], term)
                total = term if total is None else total + term
            o_refs[i][...] = total

    vm = pl.BlockSpec(memory_space=pltpu.VMEM)
    return pl.pallas_call(
        body, name="small_allreduce", in_specs=[vm] * na, out_specs=[vm] * n, out_shape=[S_(s, F32) for s in shapes],
        scratch_shapes=[pltpu.VMEM(s, F32) for s in shapes] * 3 + [pltpu.VMEM((3,) + s, F32) for s in shapes]
        + [pltpu.SemaphoreType.DMA((n, 4)), pltpu.SemaphoreType.DMA((n, 4))],
    )(*arrays)


def _row_block(r, c, target_bytes=BLOCK_BYTES):
    br = r
    while br % (2 * BF16_ROWS) == 0 and br * c * 4 > target_bytes:
        br //= 2
    return br


def _chip_sum(part, sib, ids, name):
    _, _, r, c = part.shape
    br = _row_block(r, c, 4 * BLOCK_BYTES)

    def body(ids_ref, p_ref, s_ref, o_ref):
        o_ref[...] = (p_ref[...].astype(F32) + s_ref[...].astype(F32)).astype(BF16)

    return _pcall(body, name=name, grid=(N_CHIPS, r // br), prefetch=1,
                  in_specs=[pl.BlockSpec((None, None, br, c), lambda s, i, ids: (s, ids[1], i, 0)),
                            pl.BlockSpec((None, br, c), lambda s, i, ids: (s, i, 0))],
                  out_specs=pl.BlockSpec((None, br, c), lambda s, i, ids: (s, i, 0)),
                  out_shape=S_((N_CHIPS, r, c), BF16), args=(ids, part, sib))


def _final_sum(part, sib, got, ids, name):
    _, r, c = got.shape
    br = _row_block(r, c, 2 * BLOCK_BYTES)

    def body(ids_ref, p_ref, s_ref, got_ref, o_ref):
        v = p_ref[...].astype(F32) + s_ref[...].astype(F32)
        for j in range(3):
            v = v + got_ref[j].astype(F32)
        o_ref[...] = v

    return _pcall(body, name=name, grid=(r // br,), prefetch=1,
                  in_specs=[pl.BlockSpec((None, None, br, c), lambda i, ids: (ids[0], ids[1], i, 0)),
                            pl.BlockSpec((None, br, c), lambda i, ids: (ids[0], i, 0)),
                            pl.BlockSpec((3, br, c), lambda i, ids: (0, i, 0))],
                  out_specs=pl.BlockSpec((None, None, br, c), lambda i, ids: (0, ids[1], i, 0)),
                  out_shape=S_((1, 2, r, c), F32), args=(ids, part, sib, got))


def _final_sums(items, n_bufs, comm):
    n = len(items)
    nci, nco, ncs = len(comm.inputs), len(comm.out_shape), len(comm.sems)
    shapes = [None] * n_bufs
    blocks, geo = [], []
    for a, (part, sib, got, b, layer, nl) in enumerate(items):
        _, r, c = got.shape
        br = _row_block(r, c)
        shapes[b] = S_((nl, 2, r, c), F32)
        geo.append((br, c))
        blocks += [(a, k, k * br, br) for k in range(r // br)]
    n_src = 5

    def body(*refs):
        srcs = [refs[3 * a:3 * a + 3] for a in range(n)]
        cin = refs[3 * n:3 * n + nci]
        outs = refs[3 * n + nci:3 * n + nci + n_bufs]
        cout = refs[3 * n + nci + n_bufs:3 * n + nci + n_bufs + nco]
        rest = refs[3 * n + nci + n_bufs + nco:]
        inb, outb = rest[:n], rest[n:2 * n]
        lsem, ssem = rest[2 * n], rest[2 * n + 1]
        csems = rest[2 * n + 2:]
        x, y, c, _ = _place()
        q = 2 * x + y
        comm.before(cin, cout, csems, 0, 1)

        def loads(a, k, r0, br):
            part, sib, got = srcs[a]
            rows = pl.ds(r0, br)
            pieces = [part.at[q, c, rows], sib.at[q, rows]] + [got.at[j, rows] for j in range(3)]
            return [pltpu.make_async_copy(p, inb[a].at[k % 2, j], lsem.at[a, k % 2, j]) for j, p in enumerate(pieces)]

        def store(a, k, r0, br):
            _, _, _, b, layer, _ = items[a]
            return pltpu.make_async_copy(outb[a].at[k % 2], outs[b].at[layer, c, pl.ds(r0, br)], ssem.at[a, k % 2])

        for cp in loads(*blocks[0]):
            cp.start()
        for i, (a, k, r0, br) in enumerate(blocks):
            if i + 1 < len(blocks):
                for cp in loads(*blocks[i + 1]):
                    cp.start()
            for cp in loads(a, k, r0, br):
                cp.wait()
            if k >= 2:
                store(a, k - 2, r0 - 2 * br, br).wait()
            v = inb[a][k % 2, 0].astype(F32) + inb[a][k % 2, 1].astype(F32)
            for j in range(2, n_src):
                v = v + inb[a][k % 2, j].astype(F32)
            outb[a][k % 2] = v
            store(a, k, r0, br).start()
        for a in range(n):
            cnt = sum(1 for blk in blocks if blk[0] == a)
            for kk in range(max(cnt - 2, 0), cnt):
                store(a, kk, kk * geo[a][0], geo[a][0]).wait()
        comm.after(cin, cout, csems, 0, 1)

    any_spec = pl.BlockSpec(memory_space=pl.ANY)
    scratch = ([pltpu.VMEM((2, n_src) + g, BF16) for g in geo] + [pltpu.VMEM((2,) + g, F32) for g in geo]
               + [pltpu.SemaphoreType.DMA((n, 2, n_src)), pltpu.SemaphoreType.DMA((n, 2))] + list(comm.sems))
    args = [t for it in items for t in it[:3]] + list(comm.inputs)
    outs = pl.pallas_call(body, name="final_sums", in_specs=[any_spec] * len(args), out_specs=[any_spec] * (n_bufs + nco),
                          out_shape=shapes + list(comm.out_shape), scratch_shapes=scratch,
                          compiler_params=pltpu.CompilerParams(vmem_limit_bytes=VMEM_LIMIT_BYTES))(*args)
    return list(outs[:n_bufs]), list(outs[n_bufs:])


def _adam_math(w, g, m, v):
    c1 = 1.0 / (1.0 - ADAM_B1 ** ADAM_STEP)
    c2 = 1.0 / (1.0 - ADAM_B2 ** ADAM_STEP)
    mn = ADAM_B1 * m + (1.0 - ADAM_B1) * g
    vn = ADAM_B2 * v + (1.0 - ADAM_B2) * (g * g)
    return -ADAM_LR * ((mn * c1) / (jnp.sqrt(vn * c2) + ADAM_EPS) + ADAM_WD * w), mn, vn


def _adamw(w, g, m, v, name):
    nl, r, c = w.shape
    br = _row_block(r, c)

    def body(w_ref, g_ref, m_ref, v_ref, go_ref, d_ref, mo_ref, vo_ref):
        gv = g_ref[...]
        go_ref[...] = gv
        d_ref[...], mo_ref[...], vo_ref[...] = _adam_math(w_ref[...], gv, m_ref[...], v_ref[...])

    blk = pl.BlockSpec((None, br, c), lambda l, i: (l, i, 0))
    return _pcall(body, name=name, grid=(nl, r // br), in_specs=[blk] * 4, out_specs=[blk] * 4,
                  out_shape=[S_((nl, r, c), F32)] * 4, args=(w, g, m, v))


def _adamw_small(rd, rf, sharded_g, ws, ms, vs, rows_d, rows_f, loss_row):
    n = len(ws)
    ns = len(sharded_g)
    rep = [k for k in range(n) if rows_d[k] is not None or rows_f[k] is not None]

    def body(*refs):
        rd_ref, rf_ref = refs[0], refs[1]
        sg = refs[2:2 + ns]
        w_refs, m_refs, v_refs = (refs[2 + ns + t * n:2 + ns + (t + 1) * n] for t in range(3))
        loss_ref, outs = refs[2 + ns + 3 * n], refs[3 + ns + 3 * n:]
        g_out, d_out, m_out, v_out = outs[:len(rep)], outs[len(rep):len(rep) + n], outs[len(rep) + n:len(rep) + 2 * n], \
            outs[len(rep) + 2 * n:]
        loss_ref[...] = jnp.sum(rd_ref[loss_row:loss_row + 1, :], axis=1, keepdims=True)
        si = 0
        for k in range(n):
            shape = w_refs[k].shape
            if rows_d[k] is not None or rows_f[k] is not None:
                src, rws = (rd_ref, rows_d[k]) if rows_d[k] is not None else (rf_ref, rows_f[k])
                axis = 0 if shape[0] == len(rws) else 1
                g = src[rws[0]:rws[0] + 1, :]
                if len(rws) > 1:
                    g = jnp.concatenate([src[r:r + 1, :] for r in rws], axis=axis)
                g_out[rep.index(k)][...] = g
            else:
                g = sg[si][...]
                si += 1
            d_out[k][...], m_out[k][...], v_out[k][...] = _adam_math(w_refs[k][...], g, m_refs[k][...], v_refs[k][...])

    vm = pl.BlockSpec(memory_space=pltpu.VMEM)
    shapes = [S_(w.shape, F32) for w in ws]
    outs = pl.pallas_call(
        body, name="adamw_small", in_specs=[vm] * (2 + ns + 3 * n), out_specs=[vm] * (1 + len(rep) + 3 * n),
        out_shape=[S_((1, 1), F32)] + [shapes[k] for k in rep] + shapes * 3,
    )(rd, rf, *sharded_g, *ws, *ms, *vs)
    g_rep = dict(zip(rep, outs[1:1 + len(rep)]))
    o = outs[1 + len(rep):]
    return outs[0], g_rep, o[:n], o[n:2 * n], o[2 * n:]


def _pack(pieces, rows):
    flat = jnp.concatenate([p.reshape(-1).astype(F32) for p in pieces])
    return jnp.pad(flat, (0, rows * LANES - flat.shape[0])).reshape(rows, LANES)


def _unpack(packed, shapes):
    flat = packed.reshape(-1)
    out, o = [], 0
    for s in shapes:
        n = 1
        for d in s:
            n *= d
        out.append(flat[o:o + n].reshape(s))
        o += n
    return out


def _rows_for(pieces_or_shapes):
    n = 0
    for p in pieces_or_shapes:
        k = 1
        for d in (p if isinstance(p, tuple) else p.shape):
            k *= d
        n += k
    return -(-n // (SUBLANES * LANES)) * SUBLANES


def kernel(x, norm_mix, norm_ffn, conv_w_pw1, conv_b_pw1, conv_w_dw, conv_b_dw, conv_ln_g, conv_ln_b, conv_w_pw2, conv_b_pw2, pool_w, pool_b, pool_scale, ffn_w_up, ffn_w_dw, ffn_b_dw, ffn_w_down, final_norm, loss_target, m_norm_mix, m_norm_ffn, m_conv_w_pw1, m_conv_b_pw1, m_conv_w_dw, m_conv_b_dw, m_conv_ln_g, m_conv_ln_b, m_conv_w_pw2, m_conv_b_pw2, m_pool_w, m_pool_b, m_pool_scale, m_ffn_w_up, m_ffn_w_dw, m_ffn_b_dw, m_ffn_w_down, m_final_norm, v_norm_mix, v_norm_ffn, v_conv_w_pw1, v_conv_b_pw1, v_conv_w_dw, v_conv_b_dw, v_conv_ln_g, v_conv_ln_b, v_conv_w_pw2, v_conv_b_pw2, v_pool_w, v_pool_b, v_pool_scale, v_ffn_w_up, v_ffn_w_dw, v_ffn_b_dw, v_ffn_w_down, v_final_norm):
    nb, seq, D = x.shape
    T = nb * seq
    F = ffn_w_down.shape[1] * N_CHIPS
    taps = conv_w_dw.shape[1]
    ffn_taps = ffn_w_dw.shape[1]
    ng = pool_w.shape[1]
    cg = pool_w.shape[3]
    dsh = D // N_CHIPS
    fsh = F // N_CHIPS
    assert taps - 1 <= HALO_CONV and ffn_taps - 1 <= HALO_FFN and max(POOL_WINDOWS) <= HALO_POOL
    tm = min(TILE_D, seq)
    tf = min(TILE_F, seq)
    assert seq % tm == 0 and seq % tf == 0 and tm % HALO_CONV == 0 and tf % CHUNK == 0
    tps, tpf = seq // tm, seq // tf

    xi, yi, ci = lax.axis_index("x"), lax.axis_index("y"), lax.axis_index("c")
    qi = 2 * xi + yi
    ids = jnp.stack([qi, ci]).astype(jnp.int32)
    x2d = x.reshape(T, D)
    tg2d = loss_target.reshape(T, D)

    small_sharded = [conv_w_dw[0], ffn_w_dw, pool_b, pool_scale]
    srows = 2 * _rows_for(small_sharded)
    w1g, s_pw2, s_small, s_pool, s_up0, s_up1, s_down0, s_down1 = _own_slots([
        (conv_w_pw1, 0, BF16, False), (conv_w_pw2, 0, BF16, False), (_pack(small_sharded, srows)[None], 0, F32, False),
        (pool_w.reshape(1, ng * (cg // N_CHIPS), cg), 0, BF16, False), (ffn_w_up, 0, BF16, True), (ffn_w_up, 1, BF16, True),
        (ffn_w_down, 0, BF16, False), (ffn_w_down, 1, BF16, False)])
    s_up, s_down = [s_up0, s_up1], [s_down0, s_down1]
    hb0t, a, w2g, smallg, pwg = _conv_in(x2d, norm_mix[0:1], w1g, conv_b_pw1, tm, comm=_GatherComm([s_pw2, s_small, s_pool]))
    w2f = w2g.reshape(D, D)
    sm_parts = [_unpack(smallg[s], [(taps, dsh), (2, ffn_taps, fsh), (1, dsh), (1, dsh)]) for s in range(N_CHIPS)]
    wdw_f = jnp.concatenate([p[0] for p in sm_parts], axis=1)
    wdw_f = jnp.pad(wdw_f, ((0, HALO_CONV - taps), (0, 0)))
    fdw_f = jnp.concatenate([p[1] for p in sm_parts], axis=2)
    fdw_f = jnp.pad(fdw_f, ((0, 0), (0, 8 - ffn_taps), (0, 0)))
    pb_f = jnp.concatenate([p[2] for p in sm_parts], axis=1)
    ps_f = jnp.concatenate([p[3] for p in sm_parts], axis=1)

    c, sbt, x1, hb1, hb1t, wup0, wd0 = _conv_mid(a, x2d, wdw_f, conv_b_dw, conv_ln_g, conv_ln_b, w2f, conv_b_pw2,
                                                 norm_ffn[0:1], tm, tps, taps, comm=_GatherComm([s_up[0], s_down[0]]))
    up0, gb0t, x2, h2, wup1, wd1 = _ffn_fwd(hb1, x1, wup0, fdw_f[0], ffn_b_dw[0:1], wd0.reshape(F, D), norm_mix[1:2], tf, tpf,
                                            ffn_taps, "ffn_fwd0", comm=_GatherComm([s_up[1], s_down[1]]))
    pwf = pwg.reshape(N_CHIPS, ng, cg // N_CHIPS, cg).transpose(1, 0, 2, 3).reshape(ng, cg, cg)
    plb, x3, hb3, hb3t = _pool_fwd(h2, x2, pwf, pb_f, ps_f, norm_ffn[1:2], tm, tps)
    wup = [wup0, wup1]
    wdn = [wd0.reshape(F, D), wd1.reshape(F, D)]
    up1, gb1t, dx4, dx4b, loss_part, d_final = _ffn_fwd(hb3, x3, wup[1], fdw_f[1], ffn_b_dw[1:2], wdn[1], final_norm.reshape(1, D),
                                                  tf, tpf, ffn_taps, "ffn_fwd1", target=tg2d)

    fs = {}

    def pair(group):
        return list(group), list(group.values()), _PairComm(list(group.values()))

    def chip_sums(nms, parts, from_sib):
        for nm, p, s in zip(nms, parts, from_sib):
            fs[nm] = (p, s)
        return nms, _ExchangeComm([_chip_sum(p, s, ids, "chip_sum_" + nm) for nm, p, s in zip(nms, parts, from_sib)])

    def arrived(nms, got):
        for nm, g in zip(nms, got):
            fs[nm] = fs[nm] + (g,)

    dac1, dup1, fw1 = _bwd_down(dx4, up1, fdw_f[1], ffn_b_dw[1:2], wdn[1], tf, tpf, ffn_taps, "bwd_down1")
    p_down1 = _mm_tn(gb1t, dx4b, F // 4, D, False, "dw_down1")
    nms, parts, px = pair({"down1": p_down1.reshape(N_CHIPS, 2, fsh // 2, D)})
    dup1, dx3, _, dnf1, *sib = _bwd_up(dac1, dup1, wup[1], fdw_f[1], x3, norm_ffn[1:2], dx4, tf, tpf, ffn_taps, "bwd_up1", comm=px)
    nms, xchg = chip_sums(nms, parts, sib)
    p_up1, *got = _mm_tn(hb3t, dup1, D // 2, 2 * fsh, True, "dw_up1", comm=xchg)
    arrived(nms, got)
    nms, parts, px = pair({"up1": p_up1.reshape(N_CHIPS, 2, D // 2, 2 * fsh)})
    dx2, dx2b, dpw, pool_sm, *sib = _pool_bwd(dx3, plb, x2, pwf, pb_f, ps_f, norm_mix[1:2], tm, tps, comm=px)
    nms, xchg = chip_sums(nms, parts, sib)
    dac0, dup0, fw0, *got = _bwd_down(dx2, up0, fdw_f[0], ffn_b_dw[0:1], wdn[0], tf, tpf, ffn_taps, "bwd_down0", comm=xchg)
    arrived(nms, got)
    p_down0 = _mm_tn(gb0t, dx2b, F // 4, D, False, "dw_down0")
    nms, parts, px = pair({"down0": p_down0.reshape(N_CHIPS, 2, fsh // 2, D)})
    dup0, dx1, dx1b, dnf0, *sib = _bwd_up(dac0, dup0, wup[0], fdw_f[0], x1, norm_ffn[0:1], dx2, tf, tpf, ffn_taps, "bwd_up0", comm=px)
    nms, xchg = chip_sums(nms, parts, sib)
    p_up0, *got = _mm_tn(hb1t, dup0, D // 2, 2 * fsh, True, "dw_up0", comm=xchg)
    arrived(nms, got)
    csh = cg // N_CHIPS
    p_pool = dpw.reshape(2, ng // 2, N_CHIPS, csh, cg).transpose(2, 0, 1, 3, 4).reshape(N_CHIPS, 2, (ng // 2) * csh, cg)
    nms, parts, px = pair({"up0": p_up0.reshape(N_CHIPS, 2, D // 2, 2 * fsh), "pool": p_pool.astype(BF16)})
    dc, conv_sm, *sib = _conv_bwd1(dx1, c, w2f, conv_ln_g, conv_ln_b, tm, comm=px)
    nms, xchg = chip_sums(nms, parts, sib)
    p_pw2 = _mm_tn(sbt, dx1b, D, D, False, "dw_pw2")
    nms2, parts2, px = pair({"pw2": p_pw2.reshape(N_CHIPS, 2, dsh // 2, D)})
    grad_x, dab, dwdw, db1, dnm0, *both = _conv_bwd2(dc, a, x2d, w1g, wdw_f, norm_mix[0:1], dx1, tm, tps, taps,
                                                     comm=_Both(xchg, px))
    arrived(nms, both[:len(nms)])
    nms, xchg = chip_sums(nms2, parts2, both[len(nms):])
    p_pw1, *got = _mm_tn(hb0t, dab, D, 2 * dsh, True, "dw_pw1", comm=xchg)
    arrived(nms, got)
    nms, parts, _ = pair({"pw1": p_pw1.reshape(N_CHIPS, 2, D // 2, 2 * dsh)})
    nms, xchg = chip_sums(nms, parts, _pair_exchange(parts, "grad_pair_exchange_pw1"))
    bufs, got = _final_sums([(*fs["pw2"], 0, 0, 1), (*fs["pool"], 1, 0, 1), (*fs["up0"], 2, 0, 2), (*fs["up1"], 2, 1, 2),
                             (*fs["down0"], 3, 0, 2), (*fs["down1"], 3, 1, 2)], 4, xchg)
    arrived(nms, got)
    bufs.append(_final_sum(*fs["pw1"], ids, "final_sum_pw1"))
    shared = _comm_only(_ShareComm(bufs), "grad_share_halves")
    g_pw2, g_pool, g_up, g_down, g_pw1 = [b.reshape(b.shape[0], 2 * b.shape[2], b.shape[3]) for b in shared]
    rd, rf = _small_allreduce([
        [(dnm0, 0, 1), (pool_sm, 2, 3), (dnf0, 0, 1), (dnf1, 0, 1), (db1, 0, 2), (conv_sm, 0, 4), (d_final, 0, 1),
         (pool_sm, 0, 2), (loss_part, 0, 1), 2, (dwdw, 0, dwdw.shape[0])],
        [(fw0, 0, 4), (fw1, 0, 4)]])
    g_wdw = lax.dynamic_slice_in_dim(rd[16:16 + taps], qi * dsh, dsh, axis=1)
    g_ps = lax.dynamic_slice_in_dim(rd[11:12], qi * dsh, dsh, axis=1)
    g_pb = lax.dynamic_slice_in_dim(rd[12:13], qi * dsh, dsh, axis=1)
    g_fdw = lax.dynamic_slice_in_dim(rf.reshape(2, 4, F)[:, :ffn_taps], qi * fsh, fsh, axis=2).reshape(2 * ffn_taps, fsh)

    def big(w, g, m, v, nm):
        return tuple(t.reshape(w.shape) for t in _adamw(w.reshape(g.shape), g, m.reshape(g.shape), v.reshape(g.shape),
                                                        "adamw_" + nm))

    o_pw1 = big(conv_w_pw1, g_pw1, m_conv_w_pw1, v_conv_w_pw1, "pw1")
    o_pw2 = big(conv_w_pw2, g_pw2, m_conv_w_pw2, v_conv_w_pw2, "pw2")
    o_pool = big(pool_w, g_pool, m_pool_w, v_pool_w, "pool")
    o_up = big(ffn_w_up, g_up, m_ffn_w_up, v_ffn_w_up, "up")
    o_down = big(ffn_w_down, g_down, m_ffn_w_down, v_ffn_w_down, "down")

    snames = ["norm_mix", "norm_ffn", "b_pw1", "w_dw", "b_dw", "ln_g", "ln_b", "b_pw2", "pool_b", "pool_scale", "ffn_w_dw",
              "ffn_b_dw", "final_norm"]
    sw = [norm_mix, norm_ffn, conv_b_pw1, conv_w_dw, conv_b_dw, conv_ln_g, conv_ln_b, conv_b_pw2, pool_b, pool_scale,
          ffn_w_dw, ffn_b_dw, final_norm]
    smm = [m_norm_mix, m_norm_ffn, m_conv_b_pw1, m_conv_w_dw, m_conv_b_dw, m_conv_ln_g, m_conv_ln_b, m_conv_b_pw2, m_pool_b,
           m_pool_scale, m_ffn_w_dw, m_ffn_b_dw, m_final_norm]
    svv = [v_norm_mix, v_norm_ffn, v_conv_b_pw1, v_conv_w_dw, v_conv_b_dw, v_conv_ln_g, v_conv_ln_b, v_conv_b_pw2, v_pool_b,
           v_pool_scale, v_ffn_w_dw, v_ffn_b_dw, v_final_norm]
    sshapes = [tuple(w.shape) for w in sw]
    two_d = [(2, D), (2, D), (1, 2 * D), (taps, dsh), (1, D), (1, D), (1, D), (1, D), (1, dsh), (1, dsh), (2 * ffn_taps, fsh),
             (2, F), (1, D)]
    rows_d = [[0, 1], [2, 3], [4, 5], None, [9], [6], [7], [8], None, None, None, None, [10]]
    rows_f = [None] * 11 + [[3, 7], None]
    as2d = lambda ts: [t.reshape(s) for t, s in zip(ts, two_d)]
    loss, g_rep, sd, sm_new, sv_new = _adamw_small(rd, rf, [g_wdw, g_pb, g_ps, g_fdw], as2d(sw), as2d(smm), as2d(svv), rows_d,
                                                   rows_f, loss_row=13)
    sg = dict(g_rep)
    sg.update({3: g_wdw, 8: g_pb, 9: g_ps, 10: g_fdw})
    small_out = {n: tuple(t.reshape(sshapes[k]) for t in (sg[k], sd[k], sm_new[k], sv_new[k])) for k, n in enumerate(snames)}

    order = [small_out["norm_mix"], small_out["norm_ffn"], o_pw1, small_out["b_pw1"], small_out["w_dw"], small_out["b_dw"],
             small_out["ln_g"], small_out["ln_b"], o_pw2, small_out["b_pw2"], o_pool, small_out["pool_b"],
             small_out["pool_scale"], o_up, small_out["ffn_w_dw"], small_out["ffn_b_dw"], o_down, small_out["final_norm"]]
    return (loss[0, 0], grad_x.reshape(nb, seq, D), *[o[0] for o in order], *[o[1] for o in order], *[o[2] for o in order],
            *[o[3] for o in order])
```

```python
import functools

import jax
import jax.numpy as jnp
from jax import lax
from jax.experimental import pallas as pl
from jax.experimental.pallas import tpu as pltpu

F32, BF16 = jnp.float32, jnp.bfloat16
S_ = jax.ShapeDtypeStruct
MESH = pl.DeviceIdType.MESH

RMS_EPS, LN_EPS = 1e-6, 1e-5
POOL_WINDOWS = (2, 4, 8, 16)
ADAM_LR, ADAM_B1, ADAM_B2, ADAM_EPS, ADAM_WD, ADAM_STEP = 0.001, 0.9, 0.999, 1e-08, 0.01, 10

VMEM_LIMIT_BYTES = 60 * 1024 * 1024
SUBLANES, LANES = 8, 128
BF16_ROWS = 16
BLOCK_BYTES = 1 << 20
N_CHIPS = 4
CHUNK = 32
HALO_CONV = 32
HALO_POOL = 32
HALO_FFN = 8
TILE_D = 512
TILE_F = 256


def _pcall(body, *, name, grid, in_specs, out_specs, out_shape, args, scratch=(), aliases=None, prefetch=0, comm=None):
    params = pltpu.CompilerParams(dimension_semantics=("arbitrary",) * len(grid), vmem_limit_bytes=VMEM_LIMIT_BYTES)
    if comm is not None:
        assert not prefetch
        single = not isinstance(out_shape, (list, tuple))
        in_specs, args, scratch = list(in_specs), list(args), list(scratch)
        out_specs, out_shape = ([out_specs], [out_shape]) if single else (list(out_specs), list(out_shape))
        n_in, n_out, n_scr, n_cin, n_cout = len(in_specs), len(out_specs), len(scratch), len(comm.inputs), len(comm.out_shape)
        any_spec = pl.BlockSpec(memory_space=pl.ANY)
        aliases = dict(aliases or {})
        aliases.update({n_in + a: n_out + b for a, b in comm.aliases.items()})
        inner = body

        def body(*refs):
            ins, cin = refs[:n_in], refs[n_in:n_in + n_cin]
            outs = refs[n_in + n_cin:n_in + n_cin + n_out]
            cout = refs[n_in + n_cin + n_out:n_in + n_cin + n_out + n_cout]
            scr = refs[n_in + n_cin + n_out + n_cout:n_in + n_cin + n_out + n_cout + n_scr]
            sems = refs[n_in + n_cin + n_out + n_cout + n_scr:]
            step, total = 0, 1
            for d, g in enumerate(grid):
                step = step * g + pl.program_id(d)
                total *= g
            comm.before(cin, cout, sems, step, total)
            inner(*ins, *outs, *scr)
            comm.after(cin, cout, sems, step, total)

        in_specs += [any_spec] * n_cin
        args += list(comm.inputs)
        out_specs += [any_spec] * n_cout
        out_shape += list(comm.out_shape)
        scratch += list(comm.sems)
    if prefetch:
        spec = pltpu.PrefetchScalarGridSpec(num_scalar_prefetch=prefetch, grid=grid, in_specs=in_specs,
                                            out_specs=out_specs, scratch_shapes=list(scratch))
        return pl.pallas_call(body, name=name, grid_spec=spec, out_shape=out_shape,
                              input_output_aliases=aliases or {}, compiler_params=params)(*args)
    return pl.pallas_call(body, name=name, grid=grid, in_specs=in_specs, out_specs=out_specs, out_shape=out_shape,
                          scratch_shapes=list(scratch), input_output_aliases=aliases or {},
                          compiler_params=params)(*args)


def _full(shape):
    n = len(shape)
    return pl.BlockSpec(tuple(shape), lambda *_: (0,) * n, pipeline_mode=pl.Buffered(1))


def _rows(tm, c, col=0):
    return pl.BlockSpec((tm, c), lambda i, *_: (i, col))


def _cols(r, tm):
    return pl.BlockSpec((None, r, tm), lambda i, *_: (i, 0, 0))


def _prev_halo(tm, hb, c):
    return pl.BlockSpec((hb, c), lambda i, *_: (jnp.maximum(i * (tm // hb) - 1, 0), 0))


def _next_halo(tm, hb, c, total_rows):
    last = total_rows // hb - 1
    return pl.BlockSpec((hb, c), lambda i, *_: (jnp.minimum((i + 1) * (tm // hb), last), 0))


def _acc_spec(r, c):
    return pl.BlockSpec((r, c), lambda *_: (0, 0))


def _rms(x, g):
    r = lax.rsqrt(jnp.mean(x * x, axis=-1, keepdims=True) + RMS_EPS)
    xh = x * r
    return xh * g, xh, r


def _rms_bwd(dy, xh, r, g):
    dxh = dy * g
    dx = r * (dxh - xh * jnp.mean(dxh * xh, axis=-1, keepdims=True))
    return dx, jnp.sum(dy * xh, axis=0, keepdims=True)


def _colsum(v):
    return jnp.sum(v, axis=0, keepdims=True)


def _fold8(v):
    out = v[0:8]
    for j in range(1, v.shape[0] // 8):
        out = out + v[8 * j:8 * j + 8]
    return out


def _chunks(n_rows, fn):
    def step(j, carry):
        fn(pl.multiple_of(j * CHUNK, CHUNK))
        return carry
    lax.fori_loop(0, n_rows // CHUNK, step, 0)


def _dot(a, b):
    return jnp.dot(a, b, preferred_element_type=F32)


def _dot_nt(a, b):
    return lax.dot_general(a, b, (((1,), (1,)), ((), ())), preferred_element_type=F32)


def _dot_tn(a, b):
    return lax.dot_general(a, b, (((0,), (0,)), ((), ())), preferred_element_type=F32)


def _sig(v):
    return jax.nn.sigmoid(v)


def _accumulate(ref, val, first):
    @pl.when(first)
    def _():
        ref[...] = val

    @pl.when(jnp.logical_not(first))
    def _():
        ref[...] = ref[...] + val


def _conv_in(x, g, w1g, b1, tm, comm=None):
    T, D = x.shape
    ns, _, nc = w1g.shape

    def body(x_ref, g_ref, w_ref, b_ref, hbt_ref, a_ref):
        hb = _rms(x_ref[...], g_ref[...])[0].astype(BF16)
        hbt_ref[...] = hb.T
        for s in range(ns):
            a_ref[:, s * nc:(s + 1) * nc] = _dot(hb, w_ref[s]) + b_ref[:, s * nc:(s + 1) * nc]

    return _pcall(body, name="conv_in", grid=(T // tm,),
                  in_specs=[_rows(tm, D), _full((1, D)), _full(w1g.shape), _full((1, ns * nc))],
                  out_specs=[_cols(D, tm), _rows(tm, ns * nc)],
                  out_shape=[S_((T // tm, D, tm), BF16), S_((T, ns * nc), F32)], args=(x, g, w1g, b1), comm=comm)


def _conv_mid(a, x, wdw, bdw, lng, lnb, w2, b2, gn, tm, tps, taps, comm=None):
    T, D = x.shape
    hc = HALO_CONV

    def body(a_ref, ah_ref, x_ref, wdw_ref, bdw_ref, lng_ref, lnb_ref, w2_ref, b2_ref, gn_ref,
             c_ref, sbt_ref, x1_ref, hb_ref, hbt_ref, uext, shifted):
        i = pl.program_id(0)
        ah = ah_ref[...]
        uext[0:hc, :] = jnp.where(i % tps == 0, 0.0, ah[:, :D] * _sig(ah[:, D:]))
        av = a_ref[...]
        uext[hc:hc + tm, :] = av[:, :D] * _sig(av[:, D:])

        def conv(r0, src, ks, first):
            acc = jnp.broadcast_to(bdw_ref[...], (CHUNK, D)) if first else c_ref[pl.ds(r0, CHUNK), :]
            for k, o in ks:
                acc = acc + wdw_ref[k:k + 1, :] * src[pl.ds(pl.multiple_of(r0 + o, 8), CHUNK), :]
            c_ref[pl.ds(r0, CHUNK), :] = acc
        _by_sublane_shift(uext, shifted, tm, [hc - (taps - 1) + k for k in range(taps)], conv)

        cv = c_ref[...]
        xc = cv - jnp.mean(cv, axis=-1, keepdims=True)
        nh = xc * lax.rsqrt(jnp.mean(xc * xc, axis=-1, keepdims=True) + LN_EPS)
        n = nh * lng_ref[...] + lnb_ref[...]
        sb = (n * _sig(n)).astype(BF16)
        sbt_ref[...] = sb.T
        x1 = x_ref[...] + _dot(sb, w2_ref[...]) + b2_ref[...]
        x1_ref[...] = x1
        hb = _rms(x1, gn_ref[...])[0].astype(BF16)
        hb_ref[...] = hb
        hbt_ref[...] = hb.T

    return _pcall(body, name="conv_mid", grid=(T // tm,),
                  in_specs=[_rows(tm, 2 * D), _prev_halo(tm, hc, 2 * D), _rows(tm, D), _full(wdw.shape), _full((1, D)),
                            _full((1, D)), _full((1, D)), _full((D, D)), _full((1, D)), _full((1, D))],
                  out_specs=[_rows(tm, D), _cols(D, tm), _rows(tm, D), _rows(tm, D), _cols(D, tm)],
                  out_shape=[S_((T, D), F32), S_((T // tm, D, tm), BF16), S_((T, D), F32), S_((T, D), BF16), S_((T // tm, D, tm), BF16)],
                  scratch=[pltpu.VMEM((tm + hc, D), F32), pltpu.VMEM((tm + hc - 8, D), F32)],
                  args=(a, a, x, wdw, bdw, lng, lnb, w2, b2, gn), comm=comm)


def _ffn_taps(aext, realigned, tm, ffn_taps):
    for k in range(ffn_taps - 1):
        realigned[k] = aext[pl.ds(HALO_FFN - (ffn_taps - 1) + k, tm), :]
    return [realigned.at[k] for k in range(ffn_taps - 1)] + [aext.at[pl.ds(HALO_FFN, tm)]]


def _ffn_conv(taps, wdw_ref, bdw_ref):
    acc = bdw_ref[...]
    for k, tap in enumerate(taps):
        acc = acc + wdw_ref[k:k + 1, :] * tap[...]
    return acc


def _by_sublane_shift(ext, shifted, tm, offsets, per_chunk):
    first = True
    for b in range(8):
        ks = [(k, o - b) for k, o in enumerate(offsets) if o % 8 == b]
        if not ks:
            continue
        if b:
            shifted[...] = ext[pl.ds(b, shifted.shape[0]), :]
        _chunks(tm, functools.partial(per_chunk, src=shifted if b else ext, ks=ks, first=first))
        first = False


def _ffn_fwd(hb, xin, wg, wdw, bdw, wd, gn, tm, tps, ffn_taps, name, target=None, comm=None):
    T, D = xin.shape
    ns, _, nc = wg.shape
    F = ns * nc // 2
    hf = HALO_FFN
    hh = 2 * HALO_FFN
    final = target is not None

    def body(*refs):
        if final:
            (h_ref, hh_ref, x_ref, wg_ref, wdw_ref, bdw_ref, wd_ref, gn_ref, tg_ref,
             up_ref, gb_ref, dx_ref, dxb_ref, loss_ref, dgn_ref, aext, realigned) = refs
        else:
            (h_ref, hh_ref, x_ref, wg_ref, wdw_ref, bdw_ref, wd_ref, gn_ref,
             up_ref, gb_ref, xo_ref, ho_ref, aext, realigned) = refs
        i = pl.program_id(0)
        hv = h_ref[...]
        gates = []
        for s in range(ns):
            part = _dot(hv, wg_ref[s])
            up_ref[:, s * nc:(s + 1) * nc] = part.astype(BF16)
            if s < ns // 2:
                aext[hf:hf + tm, s * nc:(s + 1) * nc] = part
            else:
                gates.append(part)
        gate = gates[0] if len(gates) == 1 else jnp.concatenate(gates, axis=1)
        halo = jnp.concatenate([_dot(hh_ref[...], wg_ref[s]) for s in range(ns // 2)], axis=1)[hh - hf:hh, :]
        aext[0:hf, :] = jnp.where(i % tps == 0, 0.0, halo)
        acv = _ffn_conv(_ffn_taps(aext, realigned, tm, ffn_taps), wdw_ref, bdw_ref)
        gb = (acv * _sig(acv) * gate).astype(BF16)
        gb_ref[...] = gb.T
        xo = x_ref[...] + _dot(gb, wd_ref[...])
        if not final:
            xo_ref[...] = xo
            ho_ref[...] = _rms(xo, gn_ref[...])[0]
        else:
            out, xh, r = _rms(xo, gn_ref[...])
            diff = out - tg_ref[...]
            part = 0.5 / D * _colsum(diff * diff)
            dx, dg = _rms_bwd(diff * (1.0 / D), xh, r, gn_ref[...])
            dx_ref[...] = dx
            dxb_ref[...] = dx.astype(BF16)
            _accumulate(loss_ref, part, i == 0)
            _accumulate(dgn_ref, dg, i == 0)

    in_specs = [_rows(tm, D), _prev_halo(tm, hh, D), _rows(tm, D), _full(wg.shape), _full(wdw.shape), _full((1, F)),
                _full((F, D)), _full((1, D))]
    args = [hb, hb, xin, wg, wdw, bdw, wd, gn]
    if final:
        in_specs.append(_rows(tm, D))
        args.append(target)
        out_specs = [_rows(tm, 2 * F), _cols(F, tm), _rows(tm, D), _rows(tm, D), _acc_spec(1, D), _acc_spec(1, D)]
        out_shape = [S_((T, 2 * F), BF16), S_((T // tm, F, tm), BF16), S_((T, D), F32), S_((T, D), BF16), S_((1, D), F32),
                     S_((1, D), F32)]
    else:
        out_specs = [_rows(tm, 2 * F), _cols(F, tm), _rows(tm, D), _rows(tm, D)]
        out_shape = [S_((T, 2 * F), BF16), S_((T // tm, F, tm), BF16), S_((T, D), F32), S_((T, D), F32)]
    return _pcall(body, name=name, grid=(T // tm,), in_specs=in_specs, out_specs=out_specs, out_shape=out_shape,
                  scratch=[pltpu.VMEM((tm + hf, F), F32), pltpu.VMEM((ffn_taps - 1, tm, F), F32)], args=args, comm=comm)


def _pool_counts(i, tps, tm, w):
    pos = (i % tps) * tm + lax.broadcasted_iota(jnp.int32, (tm, 1), 0)
    return jnp.minimum(pos + 1, w).astype(F32)


def _window_sums(src, buf_a, buf_b, cg, causal):
    assert POOL_WINDOWS == (2, 4, 8, 16)
    n = src.shape[0]
    levels = len(POOL_WINDOWS)
    cur, outs = src, []
    for l in range(levels):
        dst = buf_b if l % 2 else buf_a
        cols = slice(l * cg, levels * cg)
        lo, m = 8 * (l + 1), n - 8 * (l + 1)
        if causal:
            dst[lo:n, cols] = cur[lo:n, cols] + cur[pl.ds(lo - (1 << l), m), cols]
        else:
            dst[0:m, cols] = cur[0:m, cols] + cur[pl.ds(1 << l, m), cols]
        outs.append(dst)
        cur = dst
    return outs


def _pool_fwd(h, x2, pw, pb, ps, gn, tm, tps, comm=None):
    T, D = h.shape
    ng, cg, _ = pw.shape
    hp = HALO_POOL

    def body(h_ref, hh_ref, x_ref, pw_ref, pb_ref, ps_ref, gn_ref, pl_ref, xo_ref, hb_ref, hbt_ref, hext, buf_a, buf_b):
        i = pl.program_id(0)
        hext[0:hp, :] = jnp.where(i % tps == 0, 0.0, hh_ref[...])
        hext[hp:hp + tm, :] = h_ref[...]
        sums = _window_sums(hext, buf_a, buf_b, cg, True)
        for g, w in enumerate(POOL_WINDOWS):
            cs = slice(g * cg, (g + 1) * cg)
            pl_ref[:, cs] = (sums[g][hp:hp + tm, cs] / _pool_counts(i, tps, tm, w) - h_ref[:, cs]).astype(BF16)

        pv = pl_ref[...]
        mixed = jnp.concatenate([_dot(pv[:, g * cg:(g + 1) * cg], pw_ref[g]) for g in range(ng)], axis=1)
        xo = x_ref[...] + ps_ref[...] * (mixed + pb_ref[...])
        xo_ref[...] = xo
        hb = _rms(xo, gn_ref[...])[0].astype(BF16)
        hb_ref[...] = hb
        hbt_ref[...] = hb.T

    return _pcall(body, name="pool_fwd", grid=(T // tm,),
                  in_specs=[_rows(tm, D), _prev_halo(tm, hp, D), _rows(tm, D), _full(pw.shape), _full((1, D)), _full((1, D)),
                            _full((1, D))],
                  out_specs=[_rows(tm, D), _rows(tm, D), _rows(tm, D), _cols(D, tm)],
                  out_shape=[S_((T, D), BF16), S_((T, D), F32), S_((T, D), BF16), S_((T // tm, D, tm), BF16)],
                  scratch=[pltpu.VMEM((tm + hp, D), F32)] * 3, args=(h, h, x2, pw, pb, ps, gn), comm=comm)


def _bwd_down(dx, up, wdw, bdw, wd, tm, tps, ffn_taps, name, comm=None):
    T, D = dx.shape
    F = up.shape[1] // 2
    hf = HALO_FFN
    nt = T // tm

    def body(dx_ref, up_ref, uph_ref, wdw_ref, bdw_ref, wd_ref, dac_ref, dgate_ref, wsum_ref, aext, realigned):
        i = pl.program_id(0)
        aext[0:hf, :] = jnp.where(i % tps == 0, 0.0, uph_ref[BF16_ROWS - hf:BF16_ROWS, :].astype(F32))
        aext[hf:hf + tm, :] = up_ref[:, 0:F].astype(F32)
        taps = _ffn_taps(aext, realigned, tm, ffn_taps)
        acv = _ffn_conv(taps, wdw_ref, bdw_ref)
        dg = _dot_nt(dx_ref[...].astype(BF16), wd_ref[...])
        sg = _sig(acv)
        dgate_ref[...] = (dg * acv * sg).astype(BF16)
        dac = dg * up_ref[:, F:2 * F].astype(F32) * (sg * (1.0 + acv * (1.0 - sg)))
        dac_ref[...] = dac
        rows = [_colsum(dac * tap[...]) for tap in taps]
        rows += [_colsum(dac), jnp.zeros((8 - ffn_taps - 1, F), F32)]
        _accumulate(wsum_ref, jnp.concatenate(rows, axis=0), i == 0)

    return _pcall(body, name=name, grid=(nt,),
                  in_specs=[_rows(tm, D), _rows(tm, 2 * F), _prev_halo(tm, BF16_ROWS, F), _full(wdw.shape), _full((1, F)),
                            _full((F, D))],
                  out_specs=[_rows(tm, F), _rows(tm, F, col=1), _acc_spec(8, F)],
                  out_shape=[S_((T, F), F32), S_((T, 2 * F), BF16), S_((8, F), F32)],
                  scratch=[pltpu.VMEM((tm + hf, F), F32), pltpu.VMEM((ffn_taps - 1, tm, F), F32)],
                  args=(dx, up, up, wdw, bdw, wd), comm=comm)


def _bwd_up(dac, dup, wg, wdw, xin, gn, dx, tm, tps, ffn_taps, name, comm=None):
    T, D = xin.shape
    F = dac.shape[1]
    ns, _, nc = wg.shape
    hf = HALO_FFN
    nt = T // tm

    def body(dac_ref, dach_ref, dgate_ref, wg_ref, wdw_ref, x_ref, gn_ref, dx_ref, dact_ref, dxo_ref, dxb_ref, dgn_ref, dext):
        i = pl.program_id(0)
        dext[0:tm, :] = dac_ref[...]
        dext[tm:tm + hf, :] = jnp.where(i % tps == tps - 1, 0.0, dach_ref[...])
        dact = wdw_ref[ffn_taps - 1:ffn_taps, :] * dac_ref[...]
        for k in range(ffn_taps - 1):
            dact = dact + wdw_ref[k:k + 1, :] * dext[pl.ds(ffn_taps - 1 - k, tm), :]
        dact_ref[...] = dact.astype(BF16)

        dh = jnp.zeros((tm, D), F32)
        for s in range(ns):
            src = dact_ref if s < ns // 2 else dgate_ref
            o = (s % (ns // 2)) * nc
            dh = dh + _dot_nt(src[:, o:o + nc], wg_ref[s])
        _, xh, r = _rms(x_ref[...], gn_ref[...])
        dxn, dg = _rms_bwd(dh, xh, r, gn_ref[...])
        dxo = dx_ref[...] + dxn
        dxo_ref[...] = dxo
        dxb_ref[...] = dxo.astype(BF16)
        _accumulate(dgn_ref, dg, i == 0)

    return _pcall(body, name=name, grid=(nt,),
                  in_specs=[_rows(tm, F), _next_halo(tm, hf, F, T), _rows(tm, F, col=1), _full(wg.shape), _full(wdw.shape),
                            _rows(tm, D), _full((1, D)), _rows(tm, D)],
                  out_specs=[_rows(tm, F), _rows(tm, D), _rows(tm, D), _acc_spec(1, D)],
                  out_shape=[S_((T, 2 * F), BF16), S_((T, D), F32), S_((T, D), BF16), S_((1, D), F32)],
                  scratch=[pltpu.VMEM((tm + hf, F), F32)],
                  aliases={2: 0}, args=(dac, dac, dup, wg, wdw, xin, gn, dx), comm=comm)


def _mm_tn(xt, dy, bk, bn, shard_major, name, comm=None):
    nt, K, tt = xt.shape
    T, N = dy.shape
    assert nt * tt == T

    def body(x_ref, dy_ref, o_ref):
        x = jnp.concatenate([x_ref[t] for t in range(nt)], axis=1)
        o_ref[...] = _dot(x, dy_ref[...]).astype(o_ref.dtype)

    if shard_major:
        out_spec = pl.BlockSpec((None, bk, bn), lambda i, j: (j, i, 0))
        out_shape = S_((N // bn, K, bn), BF16)
    else:
        out_spec = pl.BlockSpec((bk, bn), lambda i, j: (i, j))
        out_shape = S_((K, N), BF16)
    return _pcall(body, name=name, grid=(K // bk, N // bn),
                  in_specs=[pl.BlockSpec((nt, bk, tt), lambda i, j: (0, i, 0)), pl.BlockSpec((T, bn), lambda i, j: (0, j))],
                  out_specs=out_spec, out_shape=out_shape, args=(xt, dy), comm=comm)


def _pool_bwd(dx3, plb, x2, pw, pb, ps, gn, tm, tps, comm=None):
    T, D = x2.shape
    ng, cg, _ = pw.shape
    hp = HALO_POOL
    nt = T // tm

    def body(do_ref, doh_ref, pl_ref, x_ref, pw_ref, pb_ref, ps_ref, gn_ref, dxo_ref, dxb_ref, dpw_ref, sm_ref, qext, buf_a,
             buf_b, dh_s):
        i = pl.program_id(0)
        do = do_ref[...]
        dm = do * ps_ref[...]
        dmh = jnp.where(i % tps == tps - 1, 0.0, doh_ref[...]) * ps_ref[...]
        pv = pl_ref[...]
        mixed = jnp.concatenate([_dot(pv[:, g * cg:(g + 1) * cg], pw_ref[g]) for g in range(ng)], axis=1)
        dscale = _colsum(do * (mixed + pb_ref[...]))
        dbias = _colsum(dm)
        dmb = dm.astype(BF16)
        dmhb = dmh.astype(BF16)
        for g, w in enumerate(POOL_WINDOWS):
            cs = slice(g * cg, (g + 1) * cg)
            _accumulate(dpw_ref.at[g], _dot_tn(pv[:, cs], dmb[:, cs]), i == 0)
            dpo = _dot_nt(dmb[:, cs], pw_ref[g])
            dh_s[:, cs] = dpo
            qext[0:tm, cs] = dpo / _pool_counts(i, tps, tm, w)
            qext[tm:tm + hp, cs] = _dot_nt(dmhb[:, cs], pw_ref[g]) * (1.0 / w)
        sums = _window_sums(qext, buf_a, buf_b, cg, False)
        for g in range(ng):
            cs = slice(g * cg, (g + 1) * cg)
            dh_s[:, cs] = sums[g][0:tm, cs] - dh_s[:, cs]

        _, xh, r = _rms(x_ref[...], gn_ref[...])
        dxn, dg = _rms_bwd(dh_s[...], xh, r, gn_ref[...])
        dxo = do + dxn
        dxo_ref[...] = dxo
        dxb_ref[...] = dxo.astype(BF16)
        sm = jnp.concatenate([dscale, dbias, dg, jnp.zeros((5, D), F32)], axis=0)
        _accumulate(sm_ref, sm, i == 0)

    return _pcall(body, name="pool_bwd", grid=(nt,),
                  in_specs=[_rows(tm, D), _next_halo(tm, hp, D, T), _rows(tm, D), _rows(tm, D), _full(pw.shape), _full((1, D)),
                            _full((1, D)), _full((1, D))],
                  out_specs=[_rows(tm, D), _rows(tm, D), pl.BlockSpec((ng, cg, cg), lambda i: (0, 0, 0)), _acc_spec(8, D)],
                  out_shape=[S_((T, D), F32), S_((T, D), BF16), S_((ng, cg, cg), F32), S_((8, D), F32)],
                  scratch=[pltpu.VMEM((tm + hp, D), F32)] * 3 + [pltpu.VMEM((tm, D), F32)],
                  args=(dx3, dx3, plb, x2, pw, pb, ps, gn), comm=comm)


def _conv_bwd1(dx1, c, w2, lng, lnb, tm, comm=None):
    T, D = dx1.shape

    def body(do_ref, c_ref, w2_ref, lng_ref, lnb_ref, dc_ref, sm_ref):
        i = pl.program_id(0)
        do = do_ref[...]
        ds = _dot_nt(do.astype(BF16), w2_ref[...])
        cv = c_ref[...]
        xc = cv - jnp.mean(cv, axis=-1, keepdims=True)
        rstd = lax.rsqrt(jnp.mean(xc * xc, axis=-1, keepdims=True) + LN_EPS)
        nh = xc * rstd
        n = nh * lng_ref[...] + lnb_ref[...]
        sg = _sig(n)
        dn = ds * (sg * (1.0 + n * (1.0 - sg)))
        dnh = dn * lng_ref[...]
        dc = rstd * (dnh - jnp.mean(dnh, axis=-1, keepdims=True) - nh * jnp.mean(dnh * nh, axis=-1, keepdims=True))
        dc_ref[...] = dc
        sm = jnp.concatenate([_colsum(dn * nh), _colsum(dn), _colsum(do), _colsum(dc), jnp.zeros((4, D), F32)], axis=0)
        _accumulate(sm_ref, sm, i == 0)

    return _pcall(body, name="conv_bwd1", grid=(T // tm,),
                  in_specs=[_rows(tm, D), _rows(tm, D), _full((D, D)), _full((1, D)), _full((1, D))],
                  out_specs=[_rows(tm, D), _acc_spec(8, D)],
                  out_shape=[S_((T, D), F32), S_((8, D), F32)], args=(dx1, c, w2, lng, lnb), comm=comm)


def _conv_bwd2(dc, a, x, w1g, wdw, gn, dx1, tm, tps, taps, comm=None):
    T, D = x.shape
    ns, _, nc = w1g.shape
    hc = HALO_CONV
    nt = T // tm
    tp = wdw.shape[0]

    def body(dc_ref, dch_ref, a_ref, ah_ref, x_ref, w_ref, wdw_ref, gn_ref, dx1_ref,
             gx_ref, dab_ref, dw_ref, db1_ref, dgn_ref, uext, dext, shifted, du_s, wacc):
        i = pl.program_id(0)
        ah = ah_ref[...]
        uext[0:hc, :] = jnp.where(i % tps == 0, 0.0, ah[:, :D] * _sig(ah[:, D:]))
        av = a_ref[...]
        sg2 = _sig(av[:, D:])
        uext[hc:hc + tm, :] = av[:, :D] * sg2
        dext[0:tm, :] = dc_ref[...]
        dext[tm:tm + hc, :] = jnp.where(i % tps == tps - 1, 0.0, dch_ref[...])

        @pl.when(i == 0)
        def _():
            wacc[...] = jnp.zeros_like(wacc)

        def wgrad(r0, src, ks, first):
            d = dc_ref[pl.ds(r0, CHUNK), :]
            for k, o in ks:
                wacc[8 * k:8 * k + 8, :] += _fold8(d * src[pl.ds(pl.multiple_of(r0 + o, 8), CHUNK), :])
        _by_sublane_shift(uext, shifted, tm, [hc - (taps - 1) + k for k in range(taps)], wgrad)

        def convt(r0, src, ks, first):
            acc = jnp.zeros((CHUNK, D), F32) if first else du_s[pl.ds(r0, CHUNK), :]
            for k, o in ks:
                acc = acc + wdw_ref[k:k + 1, :] * src[pl.ds(pl.multiple_of(r0 + o, 8), CHUNK), :]
            du_s[pl.ds(r0, CHUNK), :] = acc
        _by_sublane_shift(dext, shifted, tm, [taps - 1 - k for k in range(taps)], convt)

        du = du_s[...]
        da1 = du * sg2
        da2 = du * av[:, :D] * (sg2 * (1.0 - sg2))
        dab_ref[:, 0:D] = da1.astype(BF16)
        dab_ref[:, D:2 * D] = da2.astype(BF16)
        _accumulate(db1_ref, jnp.concatenate([_colsum(da1), _colsum(da2)], axis=0), i == 0)
        dh = jnp.zeros((tm, D), F32)
        for s in range(ns):
            dh = dh + _dot_nt(dab_ref[:, s * nc:(s + 1) * nc], w_ref[s])
        _, xh, r = _rms(x_ref[...], gn_ref[...])
        dxn, dg = _rms_bwd(dh, xh, r, gn_ref[...])
        gx_ref[...] = dx1_ref[...] + dxn
        _accumulate(dgn_ref, dg, i == 0)

        @pl.when(i == nt - 1)
        def _():
            dw_ref[...] = jnp.zeros_like(dw_ref)
            for k in range(taps):
                dw_ref[k:k + 1, :] = _colsum(wacc[8 * k:8 * k + 8, :])

    return _pcall(body, name="conv_bwd2", grid=(nt,),
                  in_specs=[_rows(tm, D), _next_halo(tm, hc, D, T), _rows(tm, 2 * D), _prev_halo(tm, hc, 2 * D), _rows(tm, D),
                            _full(w1g.shape), _full(wdw.shape), _full((1, D)), _rows(tm, D)],
                  out_specs=[_rows(tm, D), _rows(tm, 2 * D), _acc_spec(tp, D), _acc_spec(2, D), _acc_spec(1, D)],
                  out_shape=[S_((T, D), F32), S_((T, 2 * D), BF16), S_((tp, D), F32), S_((2, D), F32), S_((1, D), F32)],
                  scratch=[pltpu.VMEM((tm + hc, D), F32), pltpu.VMEM((tm + hc, D), F32), pltpu.VMEM((tm + hc - 8, D), F32),
                           pltpu.VMEM((tm, D), F32), pltpu.VMEM((8 * taps, D), F32)],
                  args=(dc, dc, a, a, x, w1g, wdw, gn, dx1), comm=comm)


def _place():
    x, y, c = lax.axis_index("x"), lax.axis_index("y"), lax.axis_index("c")
    chips = [(1 - x, y), (x, 1 - y), (1 - x, 1 - y)]
    return x, y, c, chips


class _GatherComm:
    def __init__(self, slots):
        n = len(slots)
        self.inputs = list(slots)
        self.out_shape = [S_(s.shape, s.dtype) for s in slots]
        self.aliases = {i: i for i in range(n)}
        self.sems = [pltpu.SemaphoreType.DMA((n, 3))] * 4
        self.halves = [s.shape[1] // 2 for s in slots]
        self.paired = [s.shape[0] == N_CHIPS // 2 for s in slots]
        self.widths = [s.shape[2] // 2 if p else s.shape[2] for s, p in zip(slots, self.paired)]

    def _copies(self, outs, sems):
        s_ici, r_ici, s_d2d, r_d2d = sems
        x, y, c, chips = _place()
        peers = [(j, chip, 2 * chip[0] + chip[1]) for j, chip in enumerate(chips)]

        def block(i, chip, h):
            rows = pl.ds(pl.multiple_of(h * self.halves[i], 8), self.halves[i])
            if self.paired[i]:
                return outs[i].at[chip // 2, rows, pl.ds(pl.multiple_of((chip % 2) * self.widths[i], LANES), self.widths[i])]
            return outs[i].at[chip, rows]

        def ici(i, j, chip, to):
            blk = block(i, chip, c)
            return pltpu.make_async_remote_copy(src_ref=blk, dst_ref=blk, send_sem=s_ici.at[i, j], recv_sem=r_ici.at[i, j],
                                                device_id=to, device_id_type=MESH)

        def d2d(i, j, chip, h):
            blk = block(i, chip, h)
            return pltpu.make_async_remote_copy(src_ref=blk, dst_ref=blk, send_sem=s_d2d.at[i, j], recv_sem=r_d2d.at[i, j],
                                                device_id=(x, y, 1 - c), device_id_type=MESH)
        return (x, y, c, 2 * x + y), peers, ici, d2d

    def before(self, cin, outs, sems, step, total):
        (x, y, c, q), peers, ici, d2d = self._copies(outs, sems)

        @pl.when(step == 0)
        def _():
            for i in range(len(outs)):
                for j, chip, _ in peers:
                    ici(i, j, q, (*chip, c)).start()

    def after(self, cin, outs, sems, step, total):
        (x, y, c, q), peers, ici, d2d = self._copies(outs, sems)

        @pl.when(step == max(total - 2, 0))
        def _():
            for i in range(len(outs)):
                for j, chip, qj in peers:
                    ici(i, j, qj, (x, y, c)).wait_recv()
                    d2d(i, j, qj, c).start()

        @pl.when(step == total - 1)
        def _():
            for i in range(len(outs)):
                for j, chip, qj in peers:
                    d2d(i, j, qj, 1 - c).wait_recv()
            for i in range(len(outs)):
                for j, chip, qj in peers:
                    ici(i, j, q, (*chip, c)).wait_send()
                    d2d(i, j, qj, c).wait_send()


class _ExchangeComm:
    def __init__(self, sums):
        n = len(sums)
        self.inputs = list(sums)
        self.out_shape = [S_((3,) + s.shape[1:], s.dtype) for s in sums]
        self.aliases = {}
        self.sems = [pltpu.SemaphoreType.DMA((n, 3))] * 2

    def _copies(self, srcs, outs, sems):
        s_sem, r_sem = sems
        x, y, c, chips = _place()
        return [pltpu.make_async_remote_copy(src_ref=srcs[i].at[2 * chip[0] + chip[1]], dst_ref=outs[i].at[j],
                                             send_sem=s_sem.at[i, j], recv_sem=r_sem.at[i, j],
                                             device_id=(*chip, c), device_id_type=MESH)
                for i in range(len(srcs)) for j, chip in enumerate(chips)]

    def before(self, srcs, outs, sems, step, total):
        @pl.when(step == 0)
        def _():
            for cp in self._copies(srcs, outs, sems):
                cp.start()

    def after(self, srcs, outs, sems, step, total):
        @pl.when(step == total - 1)
        def _():
            for cp in self._copies(srcs, outs, sems):
                cp.wait()


def _own_slots(specs):
    n = len(specs)
    shapes, blocks, count = [], [], []
    for a, (w, layer, dtype, pairs) in enumerate(specs):
        _, r, c = w.shape
        br = _row_block(r, c, 2 * BLOCK_BYTES)
        shapes.append(S_((N_CHIPS // 2, r, 2 * c) if pairs else (N_CHIPS, r, c), dtype))
        blocks += [(a, k, k * br, br) for k in range(r // br)]
        count.append(r // br)
    gather = _GatherComm(shapes[:1])
    n_sems = len(gather.sems)

    def body(*refs):
        srcs, outs = refs[:n], refs[n:2 * n]
        inb, outb = refs[2 * n:3 * n], refs[3 * n:4 * n]
        lsem, ssem = refs[4 * n], refs[4 * n + 1]
        gsems = refs[4 * n + 2:4 * n + 2 + n_sems]
        x, y, c, _ = _place()
        q = 2 * x + y

        def load(a, k, r0, br):
            return pltpu.make_async_copy(srcs[a].at[specs[a][1], pl.ds(r0, br)], inb[a].at[k % 2], lsem.at[a, k % 2])

        def store(a, k, r0, br):
            if specs[a][3]:
                wc = specs[a][0].shape[2]
                dst = outs[a].at[q // 2, pl.ds(r0, br), pl.ds(pl.multiple_of((q % 2) * wc, LANES), wc)]
            else:
                dst = outs[a].at[q, pl.ds(r0, br)]
            return pltpu.make_async_copy(outb[a].at[k % 2], dst, ssem.at[a, k % 2])

        load(*blocks[0]).start()
        for b, (a, k, r0, br) in enumerate(blocks):
            if b + 1 < len(blocks):
                load(*blocks[b + 1]).start()
            load(a, k, r0, br).wait()
            if k >= 2:
                store(a, k - 2, r0 - 2 * br, br).wait()
            outb[a][k % 2] = inb[a][k % 2].astype(specs[a][2])
            store(a, k, r0, br).start()
            if a == 0 and k == count[0] - 1:
                for kk in range(max(k - 1, 0), k + 1):
                    store(0, kk, kk * br, br).wait()
                gather.before(None, outs[:1], gsems, 0, 1)
        for a in range(1, n):
            br = blocks[sum(count[:a])][3]
            for kk in range(max(count[a] - 2, 0), count[a]):
                store(a, kk, kk * br, br).wait()
        gather.after(None, outs[:1], gsems, 0, 1)

    any_spec = pl.BlockSpec(memory_space=pl.ANY)
    scratch = ([pltpu.VMEM((2, blocks[sum(count[:a])][3], specs[a][0].shape[2]), F32) for a in range(n)]
               + [pltpu.VMEM((2, blocks[sum(count[:a])][3], specs[a][0].shape[2]), specs[a][2]) for a in range(n)]
               + [pltpu.SemaphoreType.DMA((n, 2)), pltpu.SemaphoreType.DMA((n, 2))] + list(gather.sems))
    outs = pl.pallas_call(body, name="own_slots", in_specs=[any_spec] * n, out_specs=[any_spec] * n, out_shape=shapes,
                          scratch_shapes=scratch,
                          compiler_params=pltpu.CompilerParams(vmem_limit_bytes=VMEM_LIMIT_BYTES))(*[s[0] for s in specs])
    return list(outs)


def _comm_only(comm, name):
    def body():
        pass
    outs = _pcall(body, name=name, grid=(1,), in_specs=[], out_specs=[], out_shape=[], args=[], comm=comm)
    return list(outs)


def _pair_exchange(parts, name):
    n = len(parts)

    def body(*refs):
        srcs, outs = refs[:n], refs[n:2 * n]
        s_sem, r_sem = refs[2 * n:]
        x, y, c, _ = _place()
        cps = []
        for i in range(n):
            for s in range(N_CHIPS):
                cp = pltpu.make_async_remote_copy(src_ref=srcs[i].at[s, 1 - c], dst_ref=outs[i].at[s], send_sem=s_sem.at[i, s],
                                                  recv_sem=r_sem.at[i, s], device_id=(x, y, 1 - c), device_id_type=MESH)
                cp.start()
                cps.append(cp)
        for cp in cps:
            cp.wait()

    any_spec = pl.BlockSpec(memory_space=pl.ANY)
    outs = pl.pallas_call(
        body, name=name, in_specs=[any_spec] * n, out_specs=[any_spec] * n,
        out_shape=[S_((N_CHIPS,) + p.shape[2:], p.dtype) for p in parts],
        scratch_shapes=[pltpu.SemaphoreType.DMA((n, N_CHIPS))] * 2,
    )(*parts)
    return list(outs)


class _PairComm:
    def __init__(self, parts):
        n = len(parts)
        self.inputs = list(parts)
        self.out_shape = [S_((N_CHIPS,) + p.shape[2:], p.dtype) for p in parts]
        self.aliases = {}
        self.sems = [pltpu.SemaphoreType.DMA((n, N_CHIPS))] * 2

    def _copies(self, srcs, outs, sems):
        s_sem, r_sem = sems
        x, y, c, _ = _place()
        return [pltpu.make_async_remote_copy(src_ref=srcs[i].at[s, 1 - c], dst_ref=outs[i].at[s], send_sem=s_sem.at[i, s],
                                             recv_sem=r_sem.at[i, s], device_id=(x, y, 1 - c), device_id_type=MESH)
                for i in range(len(srcs)) for s in range(N_CHIPS)]

    def before(self, srcs, outs, sems, step, total):
        @pl.when(step == 0)
        def _():
            for cp in self._copies(srcs, outs, sems):
                cp.start()

    def after(self, srcs, outs, sems, step, total):
        @pl.when(step == total - 1)
        def _():
            for cp in self._copies(srcs, outs, sems):
                cp.wait()


class _ShareComm:
    def __init__(self, bufs):
        n = len(bufs)
        self.inputs = list(bufs)
        self.out_shape = [S_(b.shape, b.dtype) for b in bufs]
        self.aliases = {i: i for i in range(n)}
        self.sems = [pltpu.SemaphoreType.DMA((n, max(b.shape[0] for b in bufs)))] * 2
        self.layers = [b.shape[0] for b in bufs]

    def _copies(self, outs, sems):
        s_sem, r_sem = sems
        x, y, c, _ = _place()
        cps = []
        for i, nl in enumerate(self.layers):
            for l in range(nl):
                blk = outs[i].at[l, c]
                cps.append(pltpu.make_async_remote_copy(src_ref=blk, dst_ref=blk, send_sem=s_sem.at[i, l],
                                                        recv_sem=r_sem.at[i, l], device_id=(x, y, 1 - c),
                                                        device_id_type=MESH))
        return cps

    def before(self, cin, outs, sems, step, total):
        @pl.when(step == 0)
        def _():
            for cp in self._copies(outs, sems):
                cp.start()

    def after(self, cin, outs, sems, step, total):
        @pl.when(step == total - 1)
        def _():
            for cp in self._copies(outs, sems):
                cp.wait()


class _Both:
    def __init__(self, a, b):
        self.parts = (a, b)
        self.inputs = a.inputs + b.inputs
        self.out_shape = a.out_shape + b.out_shape
        self.aliases = dict(a.aliases)
        self.aliases.update({len(a.inputs) + i: len(a.out_shape) + o for i, o in b.aliases.items()})
        self.sems = a.sems + b.sems

    def _each(self, phase, cin, cout, sems, step, total):
        i = o = s = 0
        for p in self.parts:
            ni, no, ns = len(p.inputs), len(p.out_shape), len(p.sems)
            getattr(p, phase)(cin[i:i + ni], cout[o:o + no], sems[s:s + ns], step, total)
            i, o, s = i + ni, o + no, s + ns

    def before(self, cin, cout, sems, step, total):
        self._each("before", cin, cout, sems, step, total)

    def after(self, cin, cout, sems, step, total):
        self._each("after", cin, cout, sems, step, total)


def _small_allreduce(groups):
    n = len(groups)
    arrays = []
    for grp in groups:
        for p in grp:
            if not isinstance(p, int) and not any(p[0] is a for a in arrays):
                arrays.append(p[0])
    shapes = []
    for grp in groups:
        rows = sum(p if isinstance(p, int) else p[2] - p[1] for p in grp)
        width = next(p[0].shape[1] for p in grp if not isinstance(p, int))
        assert rows % 8 == 0
        shapes.append((rows, width))
    na = len(arrays)

    def body(*refs):
        a_refs, o_refs = refs[:na], refs[na:na + n]
        rest = refs[na + n:]
        v_refs, sibs, css, gots = rest[:n], rest[n:2 * n], rest[2 * n:3 * n], rest[3 * n:4 * n]
        s_sem, r_sem = rest[4 * n:]
        for i, grp in enumerate(groups):
            o = 0
            for p in grp:
                if isinstance(p, int):
                    v_refs[i][o:o + p, :] = jnp.zeros((p, shapes[i][1]), F32)
                    o += p
                else:
                    k = next(t for t, a in enumerate(arrays) if a is p[0])
                    v_refs[i][o:o + p[2] - p[1], :] = a_refs[k][p[1]:p[2], :]
                    o += p[2] - p[1]
        x, y, c, chips = _place()
        q = 2 * x + y
        to_sib = [pltpu.make_async_remote_copy(src_ref=v_refs[i], dst_ref=sibs[i], send_sem=s_sem.at[i, 3], recv_sem=r_sem.at[i, 3],
                                               device_id=(x, y, 1 - c), device_id_type=MESH) for i in range(n)]
        for cp in to_sib:
            cp.start()
        cps = []
        for i in range(n):
            to_sib[i].wait()
            mine, other = v_refs[i][...], sibs[i][...]
            css[i][...] = jnp.where(c == 0, mine, other) + jnp.where(c == 0, other, mine)
            for j, chip in enumerate(chips):
                cp = pltpu.make_async_remote_copy(src_ref=css[i], dst_ref=gots[i].at[j], send_sem=s_sem.at[i, j],
                                                  recv_sem=r_sem.at[i, j], device_id=(*chip, c), device_id_type=MESH)
                cp.start()
                cps.append(cp)
        for cp in cps:
            cp.wait()
        flips = [2, 1, 3]
        for i in range(n):
            total = None
            for k in range(N_CHIPS):
                d = q ^ k
                term = jnp.where(d == 0, css[i][...], 0.0)
                for j in range(3):
                    term = jnp.where(d == flips[j], gots[i][j], term)
                total = term if total is None else total + term
            o_refs[i][...] = total

    vm = pl.BlockSpec(memory_space=pltpu.VMEM)
    return pl.pallas_call(
        body, name="small_allreduce", in_specs=[vm] * na, out_specs=[vm] * n, out_shape=[S_(s, F32) for s in shapes],
        scratch_shapes=[pltpu.VMEM(s, F32) for s in shapes] * 3 + [pltpu.VMEM((3,) + s, F32) for s in shapes]
        + [pltpu.SemaphoreType.DMA((n, 4)), pltpu.SemaphoreType.DMA((n, 4))],
    )(*arrays)


def _row_block(r, c, target_bytes=BLOCK_BYTES):
    br = r
    while br % (2 * BF16_ROWS) == 0 and br * c * 4 > target_bytes:
        br //= 2
    return br


def _chip_sum(part, sib, ids, name):
    _, _, r, c = part.shape
    br = _row_block(r, c, 4 * BLOCK_BYTES)

    def body(ids_ref, p_ref, s_ref, o_ref):
        o_ref[...] = (p_ref[...].astype(F32) + s_ref[...].astype(F32)).astype(BF16)

    return _pcall(body, name=name, grid=(N_CHIPS, r // br), prefetch=1,
                  in_specs=[pl.BlockSpec((None, None, br, c), lambda s, i, ids: (s, ids[1], i, 0)),
                            pl.BlockSpec((None, br, c), lambda s, i, ids: (s, i, 0))],
                  out_specs=pl.BlockSpec((None, br, c), lambda s, i, ids: (s, i, 0)),
                  out_shape=S_((N_CHIPS, r, c), BF16), args=(ids, part, sib))


def _final_sum(part, sib, got, ids, name):
    _, r, c = got.shape
    br = _row_block(r, c, 2 * BLOCK_BYTES)

    def body(ids_ref, p_ref, s_ref, got_ref, o_ref):
        v = p_ref[...].astype(F32) + s_ref[...].astype(F32)
        for j in range(3):
            v = v + got_ref[j].astype(F32)
        o_ref[...] = v

    return _pcall(body, name=name, grid=(r // br,), prefetch=1,
                  in_specs=[pl.BlockSpec((None, None, br, c), lambda i, ids: (ids[0], ids[1], i, 0)),
                            pl.BlockSpec((None, br, c), lambda i, ids: (ids[0], i, 0)),
                            pl.BlockSpec((3, br, c), lambda i, ids: (0, i, 0))],
                  out_specs=pl.BlockSpec((None, None, br, c), lambda i, ids: (0, ids[1], i, 0)),
                  out_shape=S_((1, 2, r, c), F32), args=(ids, part, sib, got))


def _final_sums(items, n_bufs, comm):
    n = len(items)
    nci, nco, ncs = len(comm.inputs), len(comm.out_shape), len(comm.sems)
    shapes = [None] * n_bufs
    blocks, geo = [], []
    for a, (part, sib, got, b, layer, nl) in enumerate(items):
        _, r, c = got.shape
        br = _row_block(r, c)
        shapes[b] = S_((nl, 2, r, c), F32)
        geo.append((br, c))
        blocks += [(a, k, k * br, br) for k in range(r // br)]
    n_src = 5

    def body(*refs):
        srcs = [refs[3 * a:3 * a + 3] for a in range(n)]
        cin = refs[3 * n:3 * n + nci]
        outs = refs[3 * n + nci:3 * n + nci + n_bufs]
        cout = refs[3 * n + nci + n_bufs:3 * n + nci + n_bufs + nco]
        rest = refs[3 * n + nci + n_bufs + nco:]
        inb, outb = rest[:n], rest[n:2 * n]
        lsem, ssem = rest[2 * n], rest[2 * n + 1]
        csems = rest[2 * n + 2:]
        x, y, c, _ = _place()
        q = 2 * x + y
        comm.before(cin, cout, csems, 0, 1)

        def loads(a, k, r0, br):
            part, sib, got = srcs[a]
            rows = pl.ds(r0, br)
            pieces = [part.at[q, c, rows], sib.at[q, rows]] + [got.at[j, rows] for j in range(3)]
            return [pltpu.make_async_copy(p, inb[a].at[k % 2, j], lsem.at[a, k % 2, j]) for j, p in enumerate(pieces)]

        def store(a, k, r0, br):
            _, _, _, b, layer, _ = items[a]
            return pltpu.make_async_copy(outb[a].at[k % 2], outs[b].at[layer, c, pl.ds(r0, br)], ssem.at[a, k % 2])

        for cp in loads(*blocks[0]):
            cp.start()
        for i, (a, k, r0, br) in enumerate(blocks):
            if i + 1 < len(blocks):
                for cp in loads(*blocks[i + 1]):
                    cp.start()
            for cp in loads(a, k, r0, br):
                cp.wait()
            if k >= 2:
                store(a, k - 2, r0 - 2 * br, br).wait()
            v = inb[a][k % 2, 0].astype(F32) + inb[a][k % 2, 1].astype(F32)
            for j in range(2, n_src):
                v = v + inb[a][k % 2, j].astype(F32)
            outb[a][k % 2] = v
            store(a, k, r0, br).start()
        for a in range(n):
            cnt = sum(1 for blk in blocks if blk[0] == a)
            for kk in range(max(cnt - 2, 0), cnt):
                store(a, kk, kk * geo[a][0], geo[a][0]).wait()
        comm.after(cin, cout, csems, 0, 1)

    any_spec = pl.BlockSpec(memory_space=pl.ANY)
    scratch = ([pltpu.VMEM((2, n_src) + g, BF16) for g in geo] + [pltpu.VMEM((2,) + g, F32) for g in geo]
               + [pltpu.SemaphoreType.DMA((n, 2, n_src)), pltpu.SemaphoreType.DMA((n, 2))] + list(comm.sems))
    args = [t for it in items for t in it[:3]] + list(comm.inputs)
    outs = pl.pallas_call(body, name="final_sums", in_specs=[any_spec] * len(args), out_specs=[any_spec] * (n_bufs + nco),
                          out_shape=shapes + list(comm.out_shape), scratch_shapes=scratch,
                          compiler_params=pltpu.CompilerParams(vmem_limit_bytes=VMEM_LIMIT_BYTES))(*args)
    return list(outs[:n_bufs]), list(outs[n_bufs:])


def _adam_math(w, g, m, v):
    c1 = 1.0 / (1.0 - ADAM_B1 ** ADAM_STEP)
    c2 = 1.0 / (1.0 - ADAM_B2 ** ADAM_STEP)
    mn = ADAM_B1 * m + (1.0 - ADAM_B1) * g
    vn = ADAM_B2 * v + (1.0 - ADAM_B2) * (g * g)
    return -ADAM_LR * ((mn * c1) / (jnp.sqrt(vn * c2) + ADAM_EPS) + ADAM_WD * w), mn, vn


def _adamw(w, g, m, v, name):
    nl, r, c = w.shape
    br = _row_block(r, c)

    def body(w_ref, g_ref, m_ref, v_ref, go_ref, d_ref, mo_ref, vo_ref):
        gv = g_ref[...]
        go_ref[...] = gv
        d_ref[...], mo_ref[...], vo_ref[...] = _adam_math(w_ref[...], gv, m_ref[...], v_ref[...])

    blk = pl.BlockSpec((None, br, c), lambda l, i: (l, i, 0))
    return _pcall(body, name=name, grid=(nl, r // br), in_specs=[blk] * 4, out_specs=[blk] * 4,
                  out_shape=[S_((nl, r, c), F32)] * 4, args=(w, g, m, v))


def _adamw_small(rd, rf, sharded_g, ws, ms, vs, rows_d, rows_f, loss_row):
    n = len(ws)
    ns = len(sharded_g)
    rep = [k for k in range(n) if rows_d[k] is not None or rows_f[k] is not None]

    def body(*refs):
        rd_ref, rf_ref = refs[0], refs[1]
        sg = refs[2:2 + ns]
        w_refs, m_refs, v_refs = (refs[2 + ns + t * n:2 + ns + (t + 1) * n] for t in range(3))
        loss_ref, outs = refs[2 + ns + 3 * n], refs[3 + ns + 3 * n:]
        g_out, d_out, m_out, v_out = outs[:len(rep)], outs[len(rep):len(rep) + n], outs[len(rep) + n:len(rep) + 2 * n], \
            outs[len(rep) + 2 * n:]
        loss_ref[...] = jnp.sum(rd_ref[loss_row:loss_row + 1, :], axis=1, keepdims=True)
        si = 0
        for k in range(n):
            shape = w_refs[k].shape
            if rows_d[k] is not None or rows_f[k] is not None:
                src, rws = (rd_ref, rows_d[k]) if rows_d[k] is not None else (rf_ref, rows_f[k])
                axis = 0 if shape[0] == len(rws) else 1
                g = src[rws[0]:rws[0] + 1, :]
                if len(rws) > 1:
                    g = jnp.concatenate([src[r:r + 1, :] for r in rws], axis=axis)
                g_out[rep.index(k)][...] = g
            else:
                g = sg[si][...]
                si += 1
            d_out[k][...], m_out[k][...], v_out[k][...] = _adam_math(w_refs[k][...], g, m_refs[k][...], v_refs[k][...])

    vm = pl.BlockSpec(memory_space=pltpu.VMEM)
    shapes = [S_(w.shape, F32) for w in ws]
    outs = pl.pallas_call(
        body, name="adamw_small", in_specs=[vm] * (2 + ns + 3 * n), out_specs=[vm] * (1 + len(rep) + 3 * n),
        out_shape=[S_((1, 1), F32)] + [shapes[k] for k in rep] + shapes * 3,
    )(rd, rf, *sharded_g, *ws, *ms, *vs)
    g_rep = dict(zip(rep, outs[1:1 + len(rep)]))
    o = outs[1 + len(rep):]
    return outs[0], g_rep, o[:n], o[n:2 * n], o[2 * n:]


def _pack(pieces, rows):
    flat = jnp.concatenate([p.reshape(-1).astype(F32) for p in pieces])
    return jnp.pad(flat, (0, rows * LANES - flat.shape[0])).reshape(rows, LANES)


def _unpack(packed, shapes):
    flat = packed.reshape(-1)
    out, o = [], 0
    for s in shapes:
        n = 1
        for d in s:
            n *= d
        out.append(flat[o:o + n].reshape(s))
        o += n
    return out


def _rows_for(pieces_or_shapes):
    n = 0
    for p in pieces_or_shapes:
        k = 1
        for d in (p if isinstance(p, tuple) else p.shape):
            k *= d
        n += k
    return -(-n // (SUBLANES * LANES)) * SUBLANES


def kernel(x, norm_mix, norm_ffn, conv_w_pw1, conv_b_pw1, conv_w_dw, conv_b_dw, conv_ln_g, conv_ln_b, conv_w_pw2, conv_b_pw2, pool_w, pool_b, pool_scale, ffn_w_up, ffn_w_dw, ffn_b_dw, ffn_w_down, final_norm, loss_target, m_norm_mix, m_norm_ffn, m_conv_w_pw1, m_conv_b_pw1, m_conv_w_dw, m_conv_b_dw, m_conv_ln_g, m_conv_ln_b, m_conv_w_pw2, m_conv_b_pw2, m_pool_w, m_pool_b, m_pool_scale, m_ffn_w_up, m_ffn_w_dw, m_ffn_b_dw, m_ffn_w_down, m_final_norm, v_norm_mix, v_norm_ffn, v_conv_w_pw1, v_conv_b_pw1, v_conv_w_dw, v_conv_b_dw, v_conv_ln_g, v_conv_ln_b, v_conv_w_pw2, v_conv_b_pw2, v_pool_w, v_pool_b, v_pool_scale, v_ffn_w_up, v_ffn_w_dw, v_ffn_b_dw, v_ffn_w_down, v_final_norm):
    nb, seq, D = x.shape
    T = nb * seq
    F = ffn_w_down.shape[1] * N_CHIPS
    taps = conv_w_dw.shape[1]
    ffn_taps = ffn_w_dw.shape[1]
    ng = pool_w.shape[1]
    cg = pool_w.shape[3]
    dsh = D // N_CHIPS
    fsh = F // N_CHIPS
    assert taps - 1 <= HALO_CONV and ffn_taps - 1 <= HALO_FFN and max(POOL_WINDOWS) <= HALO_POOL
    tm = min(TILE_D, seq)
    tf = min(TILE_F, seq)
    assert seq % tm == 0 and seq % tf == 0 and tm % HALO_CONV == 0 and tf % CHUNK == 0
    tps, tpf = seq // tm, seq // tf

    xi, yi, ci = lax.axis_index("x"), lax.axis_index("y"), lax.axis_index("c")
    qi = 2 * xi + yi
    ids = jnp.stack([qi, ci]).astype(jnp.int32)
    x2d = x.reshape(T, D)
    tg2d = loss_target.reshape(T, D)

    small_sharded = [conv_w_dw[0], ffn_w_dw, pool_b, pool_scale]
    srows = 2 * _rows_for(small_sharded)
    w1g, s_pw2, s_small, s_pool, s_up0, s_up1, s_down0, s_down1 = _own_slots([
        (conv_w_pw1, 0, BF16, False), (conv_w_pw2, 0, BF16, False), (_pack(small_sharded, srows)[None], 0, F32, False),
        (pool_w.reshape(1, ng * (cg // N_CHIPS), cg), 0, BF16, False), (ffn_w_up, 0, BF16, True), (ffn_w_up, 1, BF16, True),
        (ffn_w_down, 0, BF16, False), (ffn_w_down, 1, BF16, False)])
    s_up, s_down = [s_up0, s_up1], [s_down0, s_down1]
    hb0t, a, w2g, smallg, pwg = _conv_in(x2d, norm_mix[0:1], w1g, conv_b_pw1, tm, comm=_GatherComm([s_pw2, s_small, s_pool]))
    w2f = w2g.reshape(D, D)
    sm_parts = [_unpack(smallg[s], [(taps, dsh), (2, ffn_taps, fsh), (1, dsh), (1, dsh)]) for s in range(N_CHIPS)]
    wdw_f = jnp.concatenate([p[0] for p in sm_parts], axis=1)
    wdw_f = jnp.pad(wdw_f, ((0, HALO_CONV - taps), (0, 0)))
    fdw_f = jnp.concatenate([p[1] for p in sm_parts], axis=2)
    fdw_f = jnp.pad(fdw_f, ((0, 0), (0, 8 - ffn_taps), (0, 0)))
    pb_f = jnp.concatenate([p[2] for p in sm_parts], axis=1)
    ps_f = jnp.concatenate([p[3] for p in sm_parts], axis=1)

    c, sbt, x1, hb1, hb1t, wup0, wd0 = _conv_mid(a, x2d, wdw_f, conv_b_dw, conv_ln_g, conv_ln_b, w2f, conv_b_pw2,
                                                 norm_ffn[0:1], tm, tps, taps, comm=_GatherComm([s_up[0], s_down[0]]))
    up0, gb0t, x2, h2, wup1, wd1 = _ffn_fwd(hb1, x1, wup0, fdw_f[0], ffn_b_dw[0:1], wd0.reshape(F, D), norm_mix[1:2], tf, tpf,
                                            ffn_taps, "ffn_fwd0", comm=_GatherComm([s_up[1], s_down[1]]))
    pwf = pwg.reshape(N_CHIPS, ng, cg // N_CHIPS, cg).transpose(1, 0, 2, 3).reshape(ng, cg, cg)
    plb, x3, hb3, hb3t = _pool_fwd(h2, x2, pwf, pb_f, ps_f, norm_ffn[1:2], tm, tps)
    wup = [wup0, wup1]
    wdn = [wd0.reshape(F, D), wd1.reshape(F, D)]
    up1, gb1t, dx4, dx4b, loss_part, d_final = _ffn_fwd(hb3, x3, wup[1], fdw_f[1], ffn_b_dw[1:2], wdn[1], final_norm.reshape(1, D),
                                                  tf, tpf, ffn_taps, "ffn_fwd1", target=tg2d)

    fs = {}

    def pair(group):
        return list(group), list(group.values()), _PairComm(list(group.values()))

    def chip_sums(nms, parts, from_sib):
        for nm, p, s in zip(nms, parts, from_sib):
            fs[nm] = (p, s)
        return nms, _ExchangeComm([_chip_sum(p, s, ids, "chip_sum_" + nm) for nm, p, s in zip(nms, parts, from_sib)])

    def arrived(nms, got):
        for nm, g in zip(nms, got):
            fs[nm] = fs[nm] + (g,)

    dac1, dup1, fw1 = _bwd_down(dx4, up1, fdw_f[1], ffn_b_dw[1:2], wdn[1], tf, tpf, ffn_taps, "bwd_down1")
    p_down1 = _mm_tn(gb1t, dx4b, F // 4, D, False, "dw_down1")
    nms, parts, px = pair({"down1": p_down1.reshape(N_CHIPS, 2, fsh // 2, D)})
    dup1, dx3, _, dnf1, *sib = _bwd_up(dac1, dup1, wup[1], fdw_f[1], x3, norm_ffn[1:2], dx4, tf, tpf, ffn_taps, "bwd_up1", comm=px)
    nms, xchg = chip_sums(nms, parts, sib)
    p_up1, *got = _mm_tn(hb3t, dup1, D // 2, 2 * fsh, True, "dw_up1", comm=xchg)
    arrived(nms, got)
    nms, parts, px = pair({"up1": p_up1.reshape(N_CHIPS, 2, D // 2, 2 * fsh)})
    dx2, dx2b, dpw, pool_sm, *sib = _pool_bwd(dx3, plb, x2, pwf, pb_f, ps_f, norm_mix[1:2], tm, tps, comm=px)
    nms, xchg = chip_sums(nms, parts, sib)
    dac0, dup0, fw0, *got = _bwd_down(dx2, up0, fdw_f[0], ffn_b_dw[0:1], wdn[0], tf, tpf, ffn_taps, "bwd_down0", comm=xchg)
    arrived(nms, got)
    p_down0 = _mm_tn(gb0t, dx2b, F // 4, D, False, "dw_down0")
    nms, parts, px = pair({"down0": p_down0.reshape(N_CHIPS, 2, fsh // 2, D)})
    dup0, dx1, dx1b, dnf0, *sib = _bwd_up(dac0, dup0, wup[0], fdw_f[0], x1, norm_ffn[0:1], dx2, tf, tpf, ffn_taps, "bwd_up0", comm=px)
    nms, xchg = chip_sums(nms, parts, sib)
    p_up0, *got = _mm_tn(hb1t, dup0, D // 2, 2 * fsh, True, "dw_up0", comm=xchg)
    arrived(nms, got)
    csh = cg // N_CHIPS
    p_pool = dpw.reshape(2, ng // 2, N_CHIPS, csh, cg).transpose(2, 0, 1, 3, 4).reshape(N_CHIPS, 2, (ng // 2) * csh, cg)
    nms, parts, px = pair({"up0": p_up0.reshape(N_CHIPS, 2, D // 2, 2 * fsh), "pool": p_pool.astype(BF16)})
    dc, conv_sm, *sib = _conv_bwd1(dx1, c, w2f, conv_ln_g, conv_ln_b, tm, comm=px)
    nms, xchg = chip_sums(nms, parts, sib)
    p_pw2 = _mm_tn(sbt, dx1b, D, D, False, "dw_pw2")
    nms2, parts2, px = pair({"pw2": p_pw2.reshape(N_CHIPS, 2, dsh // 2, D)})
    grad_x, dab, dwdw, db1, dnm0, *both = _conv_bwd2(dc, a, x2d, w1g, wdw_f, norm_mix[0:1], dx1, tm, tps, taps,
                                                     comm=_Both(xchg, px))
    arrived(nms, both[:len(nms)])
    nms, xchg = chip_sums(nms2, parts2, both[len(nms):])
    p_pw1, *got = _mm_tn(hb0t, dab, D, 2 * dsh, True, "dw_pw1", comm=xchg)
    arrived(nms, got)
    nms, parts, _ = pair({"pw1": p_pw1.reshape(N_CHIPS, 2, D // 2, 2 * dsh)})
    nms, xchg = chip_sums(nms, parts, _pair_exchange(parts, "grad_pair_exchange_pw1"))
    bufs, got = _final_sums([(*fs["pw2"], 0, 0, 1), (*fs["pool"], 1, 0, 1), (*fs["up0"], 2, 0, 2), (*fs["up1"], 2, 1, 2),
                             (*fs["down0"], 3, 0, 2), (*fs["down1"], 3, 1, 2)], 4, xchg)
    arrived(nms, got)
    bufs.append(_final_sum(*fs["pw1"], ids, "final_sum_pw1"))
    shared = _comm_only(_ShareComm(bufs), "grad_share_halves")
    g_pw2, g_pool, g_up, g_down, g_pw1 = [b.reshape(b.shape[0], 2 * b.shape[2], b.shape[3]) for b in shared]
    rd, rf = _small_allreduce([
        [(dnm0, 0, 1), (pool_sm, 2, 3), (dnf0, 0, 1), (dnf1, 0, 1), (db1, 0, 2), (conv_sm, 0, 4), (d_final, 0, 1),
         (pool_sm, 0, 2), (loss_part, 0, 1), 2, (dwdw, 0, dwdw.shape[0])],
        [(fw0, 0, 4), (fw1, 0, 4)]])
    g_wdw = lax.dynamic_slice_in_dim(rd[16:16 + taps], qi * dsh, dsh, axis=1)
    g_ps = lax.dynamic_slice_in_dim(rd[11:12], qi * dsh, dsh, axis=1)
    g_pb = lax.dynamic_slice_in_dim(rd[12:13], qi * dsh, dsh, axis=1)
    g_fdw = lax.dynamic_slice_in_dim(rf.reshape(2, 4, F)[:, :ffn_taps], qi * fsh, fsh, axis=2).reshape(2 * ffn_taps, fsh)

    def big(w, g, m, v, nm):
        return tuple(t.reshape(w.shape) for t in _adamw(w.reshape(g.shape), g, m.reshape(g.shape), v.reshape(g.shape),
                                                        "adamw_" + nm))

    o_pw1 = big(conv_w_pw1, g_pw1, m_conv_w_pw1, v_conv_w_pw1, "pw1")
    o_pw2 = big(conv_w_pw2, g_pw2, m_conv_w_pw2, v_conv_w_pw2, "pw2")
    o_pool = big(pool_w, g_pool, m_pool_w, v_pool_w, "pool")
    o_up = big(ffn_w_up, g_up, m_ffn_w_up, v_ffn_w_up, "up")
    o_down = big(ffn_w_down, g_down, m_ffn_w_down, v_ffn_w_down, "down")

    snames = ["norm_mix", "norm_ffn", "b_pw1", "w_dw", "b_dw", "ln_g", "ln_b", "b_pw2", "pool_b", "pool_scale", "ffn_w_dw",
              "ffn_b_dw", "final_norm"]
    sw = [norm_mix, norm_ffn, conv_b_pw1, conv_w_dw, conv_b_dw, conv_ln_g, conv_ln_b, conv_b_pw2, pool_b, pool_scale,
          ffn_w_dw, ffn_b_dw, final_norm]
    smm = [m_norm_mix, m_norm_ffn, m_conv_b_pw1, m_conv_w_dw, m_conv_b_dw, m_conv_ln_g, m_conv_ln_b, m_conv_b_pw2, m_pool_b,
           m_pool_scale, m_ffn_w_dw, m_ffn_b_dw, m_final_norm]
    svv = [v_norm_mix, v_norm_ffn, v_conv_b_pw1, v_conv_w_dw, v_conv_b_dw, v_conv_ln_g, v_conv_ln_b, v_conv_b_pw2, v_pool_b,
           v_pool_scale, v_ffn_w_dw, v_ffn_b_dw, v_final_norm]
    sshapes = [tuple(w.shape) for w in sw]
    two_d = [(2, D), (2, D), (1, 2 * D), (taps, dsh), (1, D), (1, D), (1, D), (1, D), (1, dsh), (1, dsh), (2 * ffn_taps, fsh),
             (2, F), (1, D)]
    rows_d = [[0, 1], [2, 3], [4, 5], None, [9], [6], [7], [8], None, None, None, None, [10]]
    rows_f = [None] * 11 + [[3, 7], None]
    as2d = lambda ts: [t.reshape(s) for t, s in zip(ts, two_d)]
    loss, g_rep, sd, sm_new, sv_new = _adamw_small(rd, rf, [g_wdw, g_pb, g_ps, g_fdw], as2d(sw), as2d(smm), as2d(svv), rows_d,
                                                   rows_f, loss_row=13)
    sg = dict(g_rep)
    sg.update({3: g_wdw, 8: g_pb, 9: g_ps, 10: g_fdw})
    small_out = {n: tuple(t.reshape(sshapes[k]) for t in (sg[k], sd[k], sm_new[k], sv_new[k])) for k, n in enumerate(snames)}

    order = [small_out["norm_mix"], small_out["norm_ffn"], o_pw1, small_out["b_pw1"], small_out["w_dw"], small_out["b_dw"],
             small_out["ln_g"], small_out["ln_b"], o_pw2, small_out["b_pw2"], o_pool, small_out["pool_b"],
             small_out["pool_scale"], o_up, small_out["ffn_w_dw"], small_out["ffn_b_dw"], o_down, small_out["final_norm"]]
    return (loss[0, 0], grad_x.reshape(nb, seq, D), *[o[0] for o in order], *[o[1] for o in order], *[o[2] for o in order],
            *[o[3] for o in order])
```

```python
import functools

import jax
import jax.numpy as jnp
from jax import lax
from jax.experimental import pallas as pl
from jax.experimental.pallas import tpu as pltpu

F32, BF16 = jnp.float32, jnp.bfloat16
S_ = jax.ShapeDtypeStruct
MESH = pl.DeviceIdType.MESH

RMS_EPS, LN_EPS = 1e-6, 1e-5
POOL_WINDOWS = (2, 4, 8, 16)
ADAM_LR, ADAM_B1, ADAM_B2, ADAM_EPS, ADAM_WD, ADAM_STEP = 0.001, 0.9, 0.999, 1e-08, 0.01, 10

VMEM_LIMIT_BYTES = 60 * 1024 * 1024
SUBLANES, LANES = 8, 128
BF16_ROWS = 16
BLOCK_BYTES = 1 << 20
N_CHIPS = 4
CHUNK = 32
HALO_CONV = 32
HALO_POOL = 32
HALO_FFN = 8
TILE_D = 512
TILE_F = 256


def _pcall(body, *, name, grid, in_specs, out_specs, out_shape, args, scratch=(), aliases=None, prefetch=0, comm=None):
    params = pltpu.CompilerParams(dimension_semantics=("arbitrary",) * len(grid), vmem_limit_bytes=VMEM_LIMIT_BYTES)
    if comm is not None:
        assert not prefetch
        single = not isinstance(out_shape, (list, tuple))
        in_specs, args, scratch = list(in_specs), list(args), list(scratch)
        out_specs, out_shape = ([out_specs], [out_shape]) if single else (list(out_specs), list(out_shape))
        n_in, n_out, n_scr, n_cin, n_cout = len(in_specs), len(out_specs), len(scratch), len(comm.inputs), len(comm.out_shape)
        any_spec = pl.BlockSpec(memory_space=pl.ANY)
        aliases = dict(aliases or {})
        aliases.update({n_in + a: n_out + b for a, b in comm.aliases.items()})
        inner = body

        def body(*refs):
            ins, cin = refs[:n_in], refs[n_in:n_in + n_cin]
            outs = refs[n_in + n_cin:n_in + n_cin + n_out]
            cout = refs[n_in + n_cin + n_out:n_in + n_cin + n_out + n_cout]
            scr = refs[n_in + n_cin + n_out + n_cout:n_in + n_cin + n_out + n_cout + n_scr]
            sems = refs[n_in + n_cin + n_out + n_cout + n_scr:]
            step, total = 0, 1
            for d, g in enumerate(grid):
                step = step * g + pl.program_id(d)
                total *= g
            comm.before(cin, cout, sems, step, total)
            inner(*ins, *outs, *scr)
            comm.after(cin, cout, sems, step, total)

        in_specs += [any_spec] * n_cin
        args += list(comm.inputs)
        out_specs += [any_spec] * n_cout
        out_shape += list(comm.out_shape)
        scratch += list(comm.sems)
    if prefetch:
        spec = pltpu.PrefetchScalarGridSpec(num_scalar_prefetch=prefetch, grid=grid, in_specs=in_specs,
                                            out_specs=out_specs, scratch_shapes=list(scratch))
        return pl.pallas_call(body, name=name, grid_spec=spec, out_shape=out_shape,
                              input_output_aliases=aliases or {}, compiler_params=params)(*args)
    return pl.pallas_call(body, name=name, grid=grid, in_specs=in_specs, out_specs=out_specs, out_shape=out_shape,
                          scratch_shapes=list(scratch), input_output_aliases=aliases or {},
                          compiler_params=params)(*args)


def _full(shape):
    n = len(shape)
    return pl.BlockSpec(tuple(shape), lambda *_: (0,) * n, pipeline_mode=pl.Buffered(1))


def _rows(tm, c, col=0):
    return pl.BlockSpec((tm, c), lambda i, *_: (i, col))


def _cols(r, tm):
    return pl.BlockSpec((None, r, tm), lambda i, *_: (i, 0, 0))


def _prev_halo(tm, hb, c):
    return pl.BlockSpec((hb, c), lambda i, *_: (jnp.maximum(i * (tm // hb) - 1, 0), 0))


def _next_halo(tm, hb, c, total_rows):
    last = total_rows // hb - 1
    return pl.BlockSpec((hb, c), lambda i, *_: (jnp.minimum((i + 1) * (tm // hb), last), 0))


def _acc_spec(r, c):
    return pl.BlockSpec((r, c), lambda *_: (0, 0))


def _rms(x, g):
    r = lax.rsqrt(jnp.mean(x * x, axis=-1, keepdims=True) + RMS_EPS)
    xh = x * r
    return xh * g, xh, r


def _rms_bwd(dy, xh, r, g):
    dxh = dy * g
    dx = r * (dxh - xh * jnp.mean(dxh * xh, axis=-1, keepdims=True))
    return dx, jnp.sum(dy * xh, axis=0, keepdims=True)


def _colsum(v):
    return jnp.sum(v, axis=0, keepdims=True)


def _fold8(v):
    out = v[0:8]
    for j in range(1, v.shape[0] // 8):
        out = out + v[8 * j:8 * j + 8]
    return out


def _chunks(n_rows, fn):
    def step(j, carry):
        fn(pl.multiple_of(j * CHUNK, CHUNK))
        return carry
    lax.fori_loop(0, n_rows // CHUNK, step, 0)


def _dot(a, b):
    return jnp.dot(a, b, preferred_element_type=F32)


def _dot_nt(a, b):
    return lax.dot_general(a, b, (((1,), (1,)), ((), ())), preferred_element_type=F32)


def _dot_tn(a, b):
    return lax.dot_general(a, b, (((0,), (0,)), ((), ())), preferred_element_type=F32)


def _sig(v):
    return jax.nn.sigmoid(v)


def _accumulate(ref, val, first):
    @pl.when(first)
    def _():
        ref[...] = val

    @pl.when(jnp.logical_not(first))
    def _():
        ref[...] = ref[...] + val


def _conv_in(x, g, w1g, b1, tm, comm=None):
    T, D = x.shape
    ns, _, nc = w1g.shape

    def body(x_ref, g_ref, w_ref, b_ref, hbt_ref, a_ref):
        hb = _rms(x_ref[...], g_ref[...])[0].astype(BF16)
        hbt_ref[...] = hb.T
        for s in range(ns):
            a_ref[:, s * nc:(s + 1) * nc] = _dot(hb, w_ref[s]) + b_ref[:, s * nc:(s + 1) * nc]

    return _pcall(body, name="conv_in", grid=(T // tm,),
                  in_specs=[_rows(tm, D), _full((1, D)), _full(w1g.shape), _full((1, ns * nc))],
                  out_specs=[_cols(D, tm), _rows(tm, ns * nc)],
                  out_shape=[S_((T // tm, D, tm), BF16), S_((T, ns * nc), F32)], args=(x, g, w1g, b1), comm=comm)


def _conv_mid(a, x, wdw, bdw, lng, lnb, w2, b2, gn, tm, tps, taps, comm=None):
    T, D = x.shape
    hc = HALO_CONV

    def body(a_ref, ah_ref, x_ref, wdw_ref, bdw_ref, lng_ref, lnb_ref, w2_ref, b2_ref, gn_ref,
             c_ref, sbt_ref, x1_ref, hb_ref, hbt_ref, uext, shifted):
        i = pl.program_id(0)
        ah = ah_ref[...]
        uext[0:hc, :] = jnp.where(i % tps == 0, 0.0, ah[:, :D] * _sig(ah[:, D:]))
        av = a_ref[...]
        uext[hc:hc + tm, :] = av[:, :D] * _sig(av[:, D:])

        def conv(r0, src, ks, first):
            acc = jnp.broadcast_to(bdw_ref[...], (CHUNK, D)) if first else c_ref[pl.ds(r0, CHUNK), :]
            for k, o in ks:
                acc = acc + wdw_ref[k:k + 1, :] * src[pl.ds(pl.multiple_of(r0 + o, 8), CHUNK), :]
            c_ref[pl.ds(r0, CHUNK), :] = acc
        _by_sublane_shift(uext, shifted, tm, [hc - (taps - 1) + k for k in range(taps)], conv)

        cv = c_ref[...]
        xc = cv - jnp.mean(cv, axis=-1, keepdims=True)
        nh = xc * lax.rsqrt(jnp.mean(xc * xc, axis=-1, keepdims=True) + LN_EPS)
        n = nh * lng_ref[...] + lnb_ref[...]
        sb = (n * _sig(n)).astype(BF16)
        sbt_ref[...] = sb.T
        x1 = x_ref[...] + _dot(sb, w2_ref[...]) + b2_ref[...]
        x1_ref[...] = x1
        hb = _rms(x1, gn_ref[...])[0].astype(BF16)
        hb_ref[...] = hb
        hbt_ref[...] = hb.T

    return _pcall(body, name="conv_mid", grid=(T // tm,),
                  in_specs=[_rows(tm, 2 * D), _prev_halo(tm, hc, 2 * D), _rows(tm, D), _full(wdw.shape), _full((1, D)),
                            _full((1, D)), _full((1, D)), _full((D, D)), _full((1, D)), _full((1, D))],
                  out_specs=[_rows(tm, D), _cols(D, tm), _rows(tm, D), _rows(tm, D), _cols(D, tm)],
                  out_shape=[S_((T, D), F32), S_((T // tm, D, tm), BF16), S_((T, D), F32), S_((T, D), BF16), S_((T // tm, D, tm), BF16)],
                  scratch=[pltpu.VMEM((tm + hc, D), F32), pltpu.VMEM((tm + hc - 8, D), F32)],
                  args=(a, a, x, wdw, bdw, lng, lnb, w2, b2, gn), comm=comm)


def _ffn_taps(aext, realigned, tm, ffn_taps):
    for k in range(ffn_taps - 1):
        realigned[k] = aext[pl.ds(HALO_FFN - (ffn_taps - 1) + k, tm), :]
    return [realigned.at[k] for k in range(ffn_taps - 1)] + [aext.at[pl.ds(HALO_FFN, tm)]]


def _ffn_conv(taps, wdw_ref, bdw_ref):
    acc = bdw_ref[...]
    for k, tap in enumerate(taps):
        acc = acc + wdw_ref[k:k + 1, :] * tap[...]
    return acc


def _by_sublane_shift(ext, shifted, tm, offsets, per_chunk):
    first = True
    for b in range(8):
        ks = [(k, o - b) for k, o in enumerate(offsets) if o % 8 == b]
        if not ks:
            continue
        if b:
            shifted[...] = ext[pl.ds(b, shifted.shape[0]), :]
        _chunks(tm, functools.partial(per_chunk, src=shifted if b else ext, ks=ks, first=first))
        first = False


def _ffn_fwd(hb, xin, wg, wdw, bdw, wd, gn, tm, tps, ffn_taps, name, target=None, comm=None):
    T, D = xin.shape
    ns, _, nc = wg.shape
    F = ns * nc // 2
    hf = HALO_FFN
    hh = 2 * HALO_FFN
    final = target is not None

    def body(*refs):
        if final:
            (h_ref, hh_ref, x_ref, wg_ref, wdw_ref, bdw_ref, wd_ref, gn_ref, tg_ref,
             up_ref, gb_ref, dx_ref, dxb_ref, loss_ref, dgn_ref, aext, realigned) = refs
        else:
            (h_ref, hh_ref, x_ref, wg_ref, wdw_ref, bdw_ref, wd_ref, gn_ref,
             up_ref, gb_ref, xo_ref, ho_ref, aext, realigned) = refs
        i = pl.program_id(0)
        hv = h_ref[...]
        for s in range(ns):
            up_ref[:, s * nc:(s + 1) * nc] = _dot(hv, wg_ref[s])
        halo = jnp.concatenate([_dot(hh_ref[...], wg_ref[s]) for s in range(ns // 2)], axis=1)[hh - hf:hh, :]
        aext[0:hf, :] = jnp.where(i % tps == 0, 0.0, halo)
        aext[hf:hf + tm, :] = up_ref[:, 0:F]
        acv = _ffn_conv(_ffn_taps(aext, realigned, tm, ffn_taps), wdw_ref, bdw_ref)
        gb = (acv * _sig(acv) * up_ref[:, F:2 * F]).astype(BF16)
        gb_ref[...] = gb.T
        xo = x_ref[...] + _dot(gb, wd_ref[...])
        if not final:
            xo_ref[...] = xo
            ho_ref[...] = _rms(xo, gn_ref[...])[0]
        else:
            out, xh, r = _rms(xo, gn_ref[...])
            diff = out - tg_ref[...]
            part = 0.5 / D * _colsum(diff * diff)
            dx, dg = _rms_bwd(diff * (1.0 / D), xh, r, gn_ref[...])
            dx_ref[...] = dx
            dxb_ref[...] = dx.astype(BF16)
            _accumulate(loss_ref, part, i == 0)
            _accumulate(dgn_ref, dg, i == 0)

    in_specs = [_rows(tm, D), _prev_halo(tm, hh, D), _rows(tm, D), _full(wg.shape), _full(wdw.shape), _full((1, F)),
                _full((F, D)), _full((1, D))]
    args = [hb, hb, xin, wg, wdw, bdw, wd, gn]
    if final:
        in_specs.append(_rows(tm, D))
        args.append(target)
        out_specs = [_rows(tm, 2 * F), _cols(F, tm), _rows(tm, D), _rows(tm, D), _acc_spec(1, D), _acc_spec(1, D)]
        out_shape = [S_((T, 2 * F), F32), S_((T // tm, F, tm), BF16), S_((T, D), F32), S_((T, D), BF16), S_((1, D), F32),
                     S_((1, D), F32)]
    else:
        out_specs = [_rows(tm, 2 * F), _cols(F, tm), _rows(tm, D), _rows(tm, D)]
        out_shape = [S_((T, 2 * F), F32), S_((T // tm, F, tm), BF16), S_((T, D), F32), S_((T, D), F32)]
    return _pcall(body, name=name, grid=(T // tm,), in_specs=in_specs, out_specs=out_specs, out_shape=out_shape,
                  scratch=[pltpu.VMEM((tm + hf, F), F32), pltpu.VMEM((ffn_taps - 1, tm, F), F32)], args=args, comm=comm)


def _pool_counts(i, tps, tm, w):
    pos = (i % tps) * tm + lax.broadcasted_iota(jnp.int32, (tm, 1), 0)
    return jnp.minimum(pos + 1, w).astype(F32)


def _window_sums(src, buf_a, buf_b, cg, causal):
    assert POOL_WINDOWS == (2, 4, 8, 16)
    n = src.shape[0]
    levels = len(POOL_WINDOWS)
    cur, outs = src, []
    for l in range(levels):
        dst = buf_b if l % 2 else buf_a
        cols = slice(l * cg, levels * cg)
        lo, m = 8 * (l + 1), n - 8 * (l + 1)
        if causal:
            dst[lo:n, cols] = cur[lo:n, cols] + cur[pl.ds(lo - (1 << l), m), cols]
        else:
            dst[0:m, cols] = cur[0:m, cols] + cur[pl.ds(1 << l, m), cols]
        outs.append(dst)
        cur = dst
    return outs


def _pool_fwd(h, x2, pw, pb, ps, gn, tm, tps, comm=None):
    T, D = h.shape
    ng, cg, _ = pw.shape
    hp = HALO_POOL

    def body(h_ref, hh_ref, x_ref, pw_ref, pb_ref, ps_ref, gn_ref, pl_ref, xo_ref, hb_ref, hbt_ref, hext, buf_a, buf_b):
        i = pl.program_id(0)
        hext[0:hp, :] = jnp.where(i % tps == 0, 0.0, hh_ref[...])
        hext[hp:hp + tm, :] = h_ref[...]
        sums = _window_sums(hext, buf_a, buf_b, cg, True)
        for g, w in enumerate(POOL_WINDOWS):
            cs = slice(g * cg, (g + 1) * cg)
            pl_ref[:, cs] = (sums[g][hp:hp + tm, cs] / _pool_counts(i, tps, tm, w) - h_ref[:, cs]).astype(BF16)

        pv = pl_ref[...]
        mixed = jnp.concatenate([_dot(pv[:, g * cg:(g + 1) * cg], pw_ref[g]) for g in range(ng)], axis=1)
        xo = x_ref[...] + ps_ref[...] * (mixed + pb_ref[...])
        xo_ref[...] = xo
        hb = _rms(xo, gn_ref[...])[0].astype(BF16)
        hb_ref[...] = hb
        hbt_ref[...] = hb.T

    return _pcall(body, name="pool_fwd", grid=(T // tm,),
                  in_specs=[_rows(tm, D), _prev_halo(tm, hp, D), _rows(tm, D), _full(pw.shape), _full((1, D)), _full((1, D)),
                            _full((1, D))],
                  out_specs=[_rows(tm, D), _rows(tm, D), _rows(tm, D), _cols(D, tm)],
                  out_shape=[S_((T, D), BF16), S_((T, D), F32), S_((T, D), BF16), S_((T // tm, D, tm), BF16)],
                  scratch=[pltpu.VMEM((tm + hp, D), F32)] * 3, args=(h, h, x2, pw, pb, ps, gn), comm=comm)


def _bwd_down(dx, up, wdw, bdw, wd, tm, tps, ffn_taps, name, comm=None):
    T, D = dx.shape
    F = up.shape[1] // 2
    hf = HALO_FFN
    nt = T // tm

    def body(dx_ref, up_ref, uph_ref, wdw_ref, bdw_ref, wd_ref, dac_ref, dgate_ref, wsum_ref, aext, realigned):
        i = pl.program_id(0)
        aext[0:hf, :] = jnp.where(i % tps == 0, 0.0, uph_ref[...])
        aext[hf:hf + tm, :] = up_ref[:, 0:F]
        taps = _ffn_taps(aext, realigned, tm, ffn_taps)
        acv = _ffn_conv(taps, wdw_ref, bdw_ref)
        dg = _dot_nt(dx_ref[...].astype(BF16), wd_ref[...])
        sg = _sig(acv)
        dgate_ref[...] = (dg * acv * sg).astype(BF16)
        dac = dg * up_ref[:, F:2 * F] * (sg * (1.0 + acv * (1.0 - sg)))
        dac_ref[...] = dac
        rows = [_colsum(dac * tap[...]) for tap in taps]
        rows += [_colsum(dac), jnp.zeros((8 - ffn_taps - 1, F), F32)]
        _accumulate(wsum_ref, jnp.concatenate(rows, axis=0), i == 0)

    return _pcall(body, name=name, grid=(nt,),
                  in_specs=[_rows(tm, D), _rows(tm, 2 * F), _prev_halo(tm, hf, F), _full(wdw.shape), _full((1, F)),
                            _full((F, D))],
                  out_specs=[_rows(tm, F), _rows(tm, F, col=1), _acc_spec(8, F)],
                  out_shape=[S_((T, F), F32), S_((T, 2 * F), BF16), S_((8, F), F32)],
                  scratch=[pltpu.VMEM((tm + hf, F), F32), pltpu.VMEM((ffn_taps - 1, tm, F), F32)],
                  args=(dx, up, up, wdw, bdw, wd), comm=comm)


def _bwd_up(dac, dup, wg, wdw, xin, gn, dx, tm, tps, ffn_taps, name, comm=None):
    T, D = xin.shape
    F = dac.shape[1]
    ns, _, nc = wg.shape
    hf = HALO_FFN
    nt = T // tm

    def body(dac_ref, dach_ref, dgate_ref, wg_ref, wdw_ref, x_ref, gn_ref, dx_ref, dact_ref, dxo_ref, dxb_ref, dgn_ref, dext):
        i = pl.program_id(0)
        dext[0:tm, :] = dac_ref[...]
        dext[tm:tm + hf, :] = jnp.where(i % tps == tps - 1, 0.0, dach_ref[...])
        dact = wdw_ref[ffn_taps - 1:ffn_taps, :] * dac_ref[...]
        for k in range(ffn_taps - 1):
            dact = dact + wdw_ref[k:k + 1, :] * dext[pl.ds(ffn_taps - 1 - k, tm), :]
        dact_ref[...] = dact.astype(BF16)

        dh = jnp.zeros((tm, D), F32)
        for s in range(ns):
            src = dact_ref if s < ns // 2 else dgate_ref
            o = (s % (ns // 2)) * nc
            dh = dh + _dot_nt(src[:, o:o + nc], wg_ref[s])
        _, xh, r = _rms(x_ref[...], gn_ref[...])
        dxn, dg = _rms_bwd(dh, xh, r, gn_ref[...])
        dxo = dx_ref[...] + dxn
        dxo_ref[...] = dxo
        dxb_ref[...] = dxo.astype(BF16)
        _accumulate(dgn_ref, dg, i == 0)

    return _pcall(body, name=name, grid=(nt,),
                  in_specs=[_rows(tm, F), _next_halo(tm, hf, F, T), _rows(tm, F, col=1), _full(wg.shape), _full(wdw.shape),
                            _rows(tm, D), _full((1, D)), _rows(tm, D)],
                  out_specs=[_rows(tm, F), _rows(tm, D), _rows(tm, D), _acc_spec(1, D)],
                  out_shape=[S_((T, 2 * F), BF16), S_((T, D), F32), S_((T, D), BF16), S_((1, D), F32)],
                  scratch=[pltpu.VMEM((tm + hf, F), F32)],
                  aliases={2: 0}, args=(dac, dac, dup, wg, wdw, xin, gn, dx), comm=comm)


def _mm_tn(xt, dy, bk, bn, shard_major, name, comm=None):
    nt, K, tt = xt.shape
    T, N = dy.shape
    assert nt * tt == T

    def body(x_ref, dy_ref, o_ref):
        x = jnp.concatenate([x_ref[t] for t in range(nt)], axis=1)
        o_ref[...] = _dot(x, dy_ref[...]).astype(o_ref.dtype)

    if shard_major:
        out_spec = pl.BlockSpec((None, bk, bn), lambda i, j: (j, i, 0))
        out_shape = S_((N // bn, K, bn), BF16)
    else:
        out_spec = pl.BlockSpec((bk, bn), lambda i, j: (i, j))
        out_shape = S_((K, N), BF16)
    return _pcall(body, name=name, grid=(K // bk, N // bn),
                  in_specs=[pl.BlockSpec((nt, bk, tt), lambda i, j: (0, i, 0)), pl.BlockSpec((T, bn), lambda i, j: (0, j))],
                  out_specs=out_spec, out_shape=out_shape, args=(xt, dy), comm=comm)


def _pool_bwd(dx3, plb, x2, pw, pb, ps, gn, tm, tps, comm=None):
    T, D = x2.shape
    ng, cg, _ = pw.shape
    hp = HALO_POOL
    nt = T // tm

    def body(do_ref, doh_ref, pl_ref, x_ref, pw_ref, pb_ref, ps_ref, gn_ref, dxo_ref, dxb_ref, dpw_ref, sm_ref, qext, buf_a,
             buf_b, dh_s):
        i = pl.program_id(0)
        do = do_ref[...]
        dm = do * ps_ref[...]
        dmh = jnp.where(i % tps == tps - 1, 0.0, doh_ref[...]) * ps_ref[...]
        pv = pl_ref[...]
        mixed = jnp.concatenate([_dot(pv[:, g * cg:(g + 1) * cg], pw_ref[g]) for g in range(ng)], axis=1)
        dscale = _colsum(do * (mixed + pb_ref[...]))
        dbias = _colsum(dm)
        dmb = dm.astype(BF16)
        dmhb = dmh.astype(BF16)
        for g, w in enumerate(POOL_WINDOWS):
            cs = slice(g * cg, (g + 1) * cg)
            _accumulate(dpw_ref.at[g], _dot_tn(pv[:, cs], dmb[:, cs]), i == 0)
            dpo = _dot_nt(dmb[:, cs], pw_ref[g])
            dh_s[:, cs] = dpo
            qext[0:tm, cs] = dpo / _pool_counts(i, tps, tm, w)
            qext[tm:tm + hp, cs] = _dot_nt(dmhb[:, cs], pw_ref[g]) * (1.0 / w)
        sums = _window_sums(qext, buf_a, buf_b, cg, False)
        for g in range(ng):
            cs = slice(g * cg, (g + 1) * cg)
            dh_s[:, cs] = sums[g][0:tm, cs] - dh_s[:, cs]

        _, xh, r = _rms(x_ref[...], gn_ref[...])
        dxn, dg = _rms_bwd(dh_s[...], xh, r, gn_ref[...])
        dxo = do + dxn
        dxo_ref[...] = dxo
        dxb_ref[...] = dxo.astype(BF16)
        sm = jnp.concatenate([dscale, dbias, dg, jnp.zeros((5, D), F32)], axis=0)
        _accumulate(sm_ref, sm, i == 0)

    return _pcall(body, name="pool_bwd", grid=(nt,),
                  in_specs=[_rows(tm, D), _next_halo(tm, hp, D, T), _rows(tm, D), _rows(tm, D), _full(pw.shape), _full((1, D)),
                            _full((1, D)), _full((1, D))],
                  out_specs=[_rows(tm, D), _rows(tm, D), pl.BlockSpec((ng, cg, cg), lambda i: (0, 0, 0)), _acc_spec(8, D)],
                  out_shape=[S_((T, D), F32), S_((T, D), BF16), S_((ng, cg, cg), F32), S_((8, D), F32)],
                  scratch=[pltpu.VMEM((tm + hp, D), F32)] * 3 + [pltpu.VMEM((tm, D), F32)],
                  args=(dx3, dx3, plb, x2, pw, pb, ps, gn), comm=comm)


def _conv_bwd1(dx1, c, w2, lng, lnb, tm, comm=None):
    T, D = dx1.shape

    def body(do_ref, c_ref, w2_ref, lng_ref, lnb_ref, dc_ref, sm_ref):
        i = pl.program_id(0)
        do = do_ref[...]
        ds = _dot_nt(do.astype(BF16), w2_ref[...])
        cv = c_ref[...]
        xc = cv - jnp.mean(cv, axis=-1, keepdims=True)
        rstd = lax.rsqrt(jnp.mean(xc * xc, axis=-1, keepdims=True) + LN_EPS)
        nh = xc * rstd
        n = nh * lng_ref[...] + lnb_ref[...]
        sg = _sig(n)
        dn = ds * (sg * (1.0 + n * (1.0 - sg)))
        dnh = dn * lng_ref[...]
        dc = rstd * (dnh - jnp.mean(dnh, axis=-1, keepdims=True) - nh * jnp.mean(dnh * nh, axis=-1, keepdims=True))
        dc_ref[...] = dc
        sm = jnp.concatenate([_colsum(dn * nh), _colsum(dn), _colsum(do), _colsum(dc), jnp.zeros((4, D), F32)], axis=0)
        _accumulate(sm_ref, sm, i == 0)

    return _pcall(body, name="conv_bwd1", grid=(T // tm,),
                  in_specs=[_rows(tm, D), _rows(tm, D), _full((D, D)), _full((1, D)), _full((1, D))],
                  out_specs=[_rows(tm, D), _acc_spec(8, D)],
                  out_shape=[S_((T, D), F32), S_((8, D), F32)], args=(dx1, c, w2, lng, lnb), comm=comm)


def _conv_bwd2(dc, a, x, w1g, wdw, gn, dx1, tm, tps, taps, comm=None):
    T, D = x.shape
    ns, _, nc = w1g.shape
    hc = HALO_CONV
    nt = T // tm
    tp = wdw.shape[0]

    def body(dc_ref, dch_ref, a_ref, ah_ref, x_ref, w_ref, wdw_ref, gn_ref, dx1_ref,
             gx_ref, dab_ref, dw_ref, db1_ref, dgn_ref, uext, dext, shifted, du_s, wacc):
        i = pl.program_id(0)
        ah = ah_ref[...]
        uext[0:hc, :] = jnp.where(i % tps == 0, 0.0, ah[:, :D] * _sig(ah[:, D:]))
        av = a_ref[...]
        sg2 = _sig(av[:, D:])
        uext[hc:hc + tm, :] = av[:, :D] * sg2
        dext[0:tm, :] = dc_ref[...]
        dext[tm:tm + hc, :] = jnp.where(i % tps == tps - 1, 0.0, dch_ref[...])

        @pl.when(i == 0)
        def _():
            wacc[...] = jnp.zeros_like(wacc)

        def wgrad(r0, src, ks, first):
            d = dc_ref[pl.ds(r0, CHUNK), :]
            for k, o in ks:
                wacc[8 * k:8 * k + 8, :] += _fold8(d * src[pl.ds(pl.multiple_of(r0 + o, 8), CHUNK), :])
        _by_sublane_shift(uext, shifted, tm, [hc - (taps - 1) + k for k in range(taps)], wgrad)

        def convt(r0, src, ks, first):
            acc = jnp.zeros((CHUNK, D), F32) if first else du_s[pl.ds(r0, CHUNK), :]
            for k, o in ks:
                acc = acc + wdw_ref[k:k + 1, :] * src[pl.ds(pl.multiple_of(r0 + o, 8), CHUNK), :]
            du_s[pl.ds(r0, CHUNK), :] = acc
        _by_sublane_shift(dext, shifted, tm, [taps - 1 - k for k in range(taps)], convt)

        du = du_s[...]
        da1 = du * sg2
        da2 = du * av[:, :D] * (sg2 * (1.0 - sg2))
        dab_ref[:, 0:D] = da1.astype(BF16)
        dab_ref[:, D:2 * D] = da2.astype(BF16)
        _accumulate(db1_ref, jnp.concatenate([_colsum(da1), _colsum(da2)], axis=0), i == 0)
        dh = jnp.zeros((tm, D), F32)
        for s in range(ns):
            dh = dh + _dot_nt(dab_ref[:, s * nc:(s + 1) * nc], w_ref[s])
        _, xh, r = _rms(x_ref[...], gn_ref[...])
        dxn, dg = _rms_bwd(dh, xh, r, gn_ref[...])
        gx_ref[...] = dx1_ref[...] + dxn
        _accumulate(dgn_ref, dg, i == 0)

        @pl.when(i == nt - 1)
        def _():
            dw_ref[...] = jnp.zeros_like(dw_ref)
            for k in range(taps):
                dw_ref[k:k + 1, :] = _colsum(wacc[8 * k:8 * k + 8, :])

    return _pcall(body, name="conv_bwd2", grid=(nt,),
                  in_specs=[_rows(tm, D), _next_halo(tm, hc, D, T), _rows(tm, 2 * D), _prev_halo(tm, hc, 2 * D), _rows(tm, D),
                            _full(w1g.shape), _full(wdw.shape), _full((1, D)), _rows(tm, D)],
                  out_specs=[_rows(tm, D), _rows(tm, 2 * D), _acc_spec(tp, D), _acc_spec(2, D), _acc_spec(1, D)],
                  out_shape=[S_((T, D), F32), S_((T, 2 * D), BF16), S_((tp, D), F32), S_((2, D), F32), S_((1, D), F32)],
                  scratch=[pltpu.VMEM((tm + hc, D), F32), pltpu.VMEM((tm + hc, D), F32), pltpu.VMEM((tm + hc - 8, D), F32),
                           pltpu.VMEM((tm, D), F32), pltpu.VMEM((8 * taps, D), F32)],
                  args=(dc, dc, a, a, x, w1g, wdw, gn, dx1), comm=comm)


def _place():
    x, y, c = lax.axis_index("x"), lax.axis_index("y"), lax.axis_index("c")
    chips = [(1 - x, y), (x, 1 - y), (1 - x, 1 - y)]
    return x, y, c, chips


class _GatherComm:
    def __init__(self, slots):
        n = len(slots)
        self.inputs = list(slots)
        self.out_shape = [S_(s.shape, s.dtype) for s in slots]
        self.aliases = {i: i for i in range(n)}
        self.sems = [pltpu.SemaphoreType.DMA((n, 3))] * 4
        self.halves = [s.shape[1] // 2 for s in slots]
        self.paired = [s.shape[0] == N_CHIPS // 2 for s in slots]
        self.widths = [s.shape[2] // 2 if p else s.shape[2] for s, p in zip(slots, self.paired)]

    def _copies(self, outs, sems):
        s_ici, r_ici, s_d2d, r_d2d = sems
        x, y, c, chips = _place()
        peers = [(j, chip, 2 * chip[0] + chip[1]) for j, chip in enumerate(chips)]

        def block(i, chip, h):
            rows = pl.ds(pl.multiple_of(h * self.halves[i], 8), self.halves[i])
            if self.paired[i]:
                return outs[i].at[chip // 2, rows, pl.ds(pl.multiple_of((chip % 2) * self.widths[i], LANES), self.widths[i])]
            return outs[i].at[chip, rows]

        def ici(i, j, chip, to):
            blk = block(i, chip, c)
            return pltpu.make_async_remote_copy(src_ref=blk, dst_ref=blk, send_sem=s_ici.at[i, j], recv_sem=r_ici.at[i, j],
                                                device_id=to, device_id_type=MESH)

        def d2d(i, j, chip, h):
            blk = block(i, chip, h)
            return pltpu.make_async_remote_copy(src_ref=blk, dst_ref=blk, send_sem=s_d2d.at[i, j], recv_sem=r_d2d.at[i, j],
                                                device_id=(x, y, 1 - c), device_id_type=MESH)
        return (x, y, c, 2 * x + y), peers, ici, d2d

    def before(self, cin, outs, sems, step, total):
        (x, y, c, q), peers, ici, d2d = self._copies(outs, sems)

        @pl.when(step == 0)
        def _():
            for i in range(len(outs)):
                for j, chip, _ in peers:
                    ici(i, j, q, (*chip, c)).start()

    def after(self, cin, outs, sems, step, total):
        (x, y, c, q), peers, ici, d2d = self._copies(outs, sems)

        @pl.when(step == max(total - 2, 0))
        def _():
            for i in range(len(outs)):
                for j, chip, qj in peers:
                    ici(i, j, qj, (x, y, c)).wait_recv()
                    d2d(i, j, qj, c).start()

        @pl.when(step == total - 1)
        def _():
            for i in range(len(outs)):
                for j, chip, qj in peers:
                    d2d(i, j, qj, 1 - c).wait_recv()
            for i in range(len(outs)):
                for j, chip, qj in peers:
                    ici(i, j, q, (*chip, c)).wait_send()
                    d2d(i, j, qj, c).wait_send()


class _ExchangeComm:
    def __init__(self, sums):
        n = len(sums)
        self.inputs = list(sums)
        self.out_shape = [S_((3,) + s.shape[1:], s.dtype) for s in sums]
        self.aliases = {}
        self.sems = [pltpu.SemaphoreType.DMA((n, 3))] * 2

    def _copies(self, srcs, outs, sems):
        s_sem, r_sem = sems
        x, y, c, chips = _place()
        return [pltpu.make_async_remote_copy(src_ref=srcs[i].at[2 * chip[0] + chip[1]], dst_ref=outs[i].at[j],
                                             send_sem=s_sem.at[i, j], recv_sem=r_sem.at[i, j],
                                             device_id=(*chip, c), device_id_type=MESH)
                for i in range(len(srcs)) for j, chip in enumerate(chips)]

    def before(self, srcs, outs, sems, step, total):
        @pl.when(step == 0)
        def _():
            for cp in self._copies(srcs, outs, sems):
                cp.start()

    def after(self, srcs, outs, sems, step, total):
        @pl.when(step == total - 1)
        def _():
            for cp in self._copies(srcs, outs, sems):
                cp.wait()


def _own_slots(specs):
    n = len(specs)
    shapes, blocks, count = [], [], []
    for a, (w, layer, dtype, pairs) in enumerate(specs):
        _, r, c = w.shape
        br = _row_block(r, c, 2 * BLOCK_BYTES)
        shapes.append(S_((N_CHIPS // 2, r, 2 * c) if pairs else (N_CHIPS, r, c), dtype))
        blocks += [(a, k, k * br, br) for k in range(r // br)]
        count.append(r // br)
    gather = _GatherComm(shapes[:1])
    n_sems = len(gather.sems)

    def body(*refs):
        srcs, outs = refs[:n], refs[n:2 * n]
        inb, outb = refs[2 * n:3 * n], refs[3 * n:4 * n]
        lsem, ssem = refs[4 * n], refs[4 * n + 1]
        gsems = refs[4 * n + 2:4 * n + 2 + n_sems]
        x, y, c, _ = _place()
        q = 2 * x + y

        def load(a, k, r0, br):
            return pltpu.make_async_copy(srcs[a].at[specs[a][1], pl.ds(r0, br)], inb[a].at[k % 2], lsem.at[a, k % 2])

        def store(a, k, r0, br):
            if specs[a][3]:
                wc = specs[a][0].shape[2]
                dst = outs[a].at[q // 2, pl.ds(r0, br), pl.ds(pl.multiple_of((q % 2) * wc, LANES), wc)]
            else:
                dst = outs[a].at[q, pl.ds(r0, br)]
            return pltpu.make_async_copy(outb[a].at[k % 2], dst, ssem.at[a, k % 2])

        load(*blocks[0]).start()
        for b, (a, k, r0, br) in enumerate(blocks):
            if b + 1 < len(blocks):
                load(*blocks[b + 1]).start()
            load(a, k, r0, br).wait()
            if k >= 2:
                store(a, k - 2, r0 - 2 * br, br).wait()
            outb[a][k % 2] = inb[a][k % 2].astype(specs[a][2])
            store(a, k, r0, br).start()
            if a == 0 and k == count[0] - 1:
                for kk in range(max(k - 1, 0), k + 1):
                    store(0, kk, kk * br, br).wait()
                gather.before(None, outs[:1], gsems, 0, 1)
        for a in range(1, n):
            br = blocks[sum(count[:a])][3]
            for kk in range(max(count[a] - 2, 0), count[a]):
                store(a, kk, kk * br, br).wait()
        gather.after(None, outs[:1], gsems, 0, 1)

    any_spec = pl.BlockSpec(memory_space=pl.ANY)
    scratch = ([pltpu.VMEM((2, blocks[sum(count[:a])][3], specs[a][0].shape[2]), F32) for a in range(n)]
               + [pltpu.VMEM((2, blocks[sum(count[:a])][3], specs[a][0].shape[2]), specs[a][2]) for a in range(n)]
               + [pltpu.SemaphoreType.DMA((n, 2)), pltpu.SemaphoreType.DMA((n, 2))] + list(gather.sems))
    outs = pl.pallas_call(body, name="own_slots", in_specs=[any_spec] * n, out_specs=[any_spec] * n, out_shape=shapes,
                          scratch_shapes=scratch,
                          compiler_params=pltpu.CompilerParams(vmem_limit_bytes=VMEM_LIMIT_BYTES))(*[s[0] for s in specs])
    return list(outs)


def _comm_only(comm, name):
    def body():
        pass
    outs = _pcall(body, name=name, grid=(1,), in_specs=[], out_specs=[], out_shape=[], args=[], comm=comm)
    return list(outs)


def _pair_exchange(parts, name):
    n = len(parts)

    def body(*refs):
        srcs, outs = refs[:n], refs[n:2 * n]
        s_sem, r_sem = refs[2 * n:]
        x, y, c, _ = _place()
        cps = []
        for i in range(n):
            for s in range(N_CHIPS):
                cp = pltpu.make_async_remote_copy(src_ref=srcs[i].at[s, 1 - c], dst_ref=outs[i].at[s], send_sem=s_sem.at[i, s],
                                                  recv_sem=r_sem.at[i, s], device_id=(x, y, 1 - c), device_id_type=MESH)
                cp.start()
                cps.append(cp)
        for cp in cps:
            cp.wait()

    any_spec = pl.BlockSpec(memory_space=pl.ANY)
    outs = pl.pallas_call(
        body, name=name, in_specs=[any_spec] * n, out_specs=[any_spec] * n,
        out_shape=[S_((N_CHIPS,) + p.shape[2:], p.dtype) for p in parts],
        scratch_shapes=[pltpu.SemaphoreType.DMA((n, N_CHIPS))] * 2,
    )(*parts)
    return list(outs)


class _PairComm:
    def __init__(self, parts):
        n = len(parts)
        self.inputs = list(parts)
        self.out_shape = [S_((N_CHIPS,) + p.shape[2:], p.dtype) for p in parts]
        self.aliases = {}
        self.sems = [pltpu.SemaphoreType.DMA((n, N_CHIPS))] * 2

    def _copies(self, srcs, outs, sems):
        s_sem, r_sem = sems
        x, y, c, _ = _place()
        return [pltpu.make_async_remote_copy(src_ref=srcs[i].at[s, 1 - c], dst_ref=outs[i].at[s], send_sem=s_sem.at[i, s],
                                             recv_sem=r_sem.at[i, s], device_id=(x, y, 1 - c), device_id_type=MESH)
                for i in range(len(srcs)) for s in range(N_CHIPS)]

    def before(self, srcs, outs, sems, step, total):
        @pl.when(step == 0)
        def _():
            for cp in self._copies(srcs, outs, sems):
                cp.start()

    def after(self, srcs, outs, sems, step, total):
        @pl.when(step == total - 1)
        def _():
            for cp in self._copies(srcs, outs, sems):
                cp.wait()


class _ShareComm:
    def __init__(self, bufs):
        n = len(bufs)
        self.inputs = list(bufs)
        self.out_shape = [S_(b.shape, b.dtype) for b in bufs]
        self.aliases = {i: i for i in range(n)}
        self.sems = [pltpu.SemaphoreType.DMA((n, max(b.shape[0] for b in bufs)))] * 2
        self.layers = [b.shape[0] for b in bufs]

    def _copies(self, outs, sems):
        s_sem, r_sem = sems
        x, y, c, _ = _place()
        cps = []
        for i, nl in enumerate(self.layers):
            for l in range(nl):
                blk = outs[i].at[l, c]
                cps.append(pltpu.make_async_remote_copy(src_ref=blk, dst_ref=blk, send_sem=s_sem.at[i, l],
                                                        recv_sem=r_sem.at[i, l], device_id=(x, y, 1 - c),
                                                        device_id_type=MESH))
        return cps

    def before(self, cin, outs, sems, step, total):
        @pl.when(step == 0)
        def _():
            for cp in self._copies(outs, sems):
                cp.start()

    def after(self, cin, outs, sems, step, total):
        @pl.when(step == total - 1)
        def _():
            for cp in self._copies(outs, sems):
                cp.wait()


class _Both:
    def __init__(self, a, b):
        self.parts = (a, b)
        self.inputs = a.inputs + b.inputs
        self.out_shape = a.out_shape + b.out_shape
        self.aliases = dict(a.aliases)
        self.aliases.update({len(a.inputs) + i: len(a.out_shape) + o for i, o in b.aliases.items()})
        self.sems = a.sems + b.sems

    def _each(self, phase, cin, cout, sems, step, total):
        i = o = s = 0
        for p in self.parts:
            ni, no, ns = len(p.inputs), len(p.out_shape), len(p.sems)
            getattr(p, phase)(cin[i:i + ni], cout[o:o + no], sems[s:s + ns], step, total)
            i, o, s = i + ni, o + no, s + ns

    def before(self, cin, cout, sems, step, total):
        self._each("before", cin, cout, sems, step, total)

    def after(self, cin, cout, sems, step, total):
        self._each("after", cin, cout, sems, step, total)


def _small_allreduce(groups):
    n = len(groups)
    arrays = []
    for grp in groups:
        for p in grp:
            if not isinstance(p, int) and not any(p[0] is a for a in arrays):
                arrays.append(p[0])
    shapes = []
    for grp in groups:
        rows = sum(p if isinstance(p, int) else p[2] - p[1] for p in grp)
        width = next(p[0].shape[1] for p in grp if not isinstance(p, int))
        assert rows % 8 == 0
        shapes.append((rows, width))
    na = len(arrays)

    def body(*refs):
        a_refs, o_refs = refs[:na], refs[na:na + n]
        rest = refs[na + n:]
        v_refs, sibs, css, gots = rest[:n], rest[n:2 * n], rest[2 * n:3 * n], rest[3 * n:4 * n]
        s_sem, r_sem = rest[4 * n:]
        for i, grp in enumerate(groups):
            o = 0
            for p in grp:
                if isinstance(p, int):
                    v_refs[i][o:o + p, :] = jnp.zeros((p, shapes[i][1]), F32)
                    o += p
                else:
                    k = next(t for t, a in enumerate(arrays) if a is p[0])
                    v_refs[i][o:o + p[2] - p[1], :] = a_refs[k][p[1]:p[2], :]
                    o += p[2] - p[1]
        x, y, c, chips = _place()
        q = 2 * x + y
        to_sib = [pltpu.make_async_remote_copy(src_ref=v_refs[i], dst_ref=sibs[i], send_sem=s_sem.at[i, 3], recv_sem=r_sem.at[i, 3],
                                               device_id=(x, y, 1 - c), device_id_type=MESH) for i in range(n)]
        for cp in to_sib:
            cp.start()
        cps = []
        for i in range(n):
            to_sib[i].wait()
            mine, other = v_refs[i][...], sibs[i][...]
            css[i][...] = jnp.where(c == 0, mine, other) + jnp.where(c == 0, other, mine)
            for j, chip in enumerate(chips):
                cp = pltpu.make_async_remote_copy(src_ref=css[i], dst_ref=gots[i].at[j], send_sem=s_sem.at[i, j],
                                                  recv_sem=r_sem.at[i, j], device_id=(*chip, c), device_id_type=MESH)
                cp.start()
                cps.append(cp)
        for cp in cps:
            cp.wait()
        flips = [2, 1, 3]
        for i in range(n):
            total = None
            for k in range(N_CHIPS):
                d = q ^ k
                term = jnp.where(d == 0, css[i][...], 0.0)
                for j in range(3):
                    term = jnp.where(d == flips[j], gots[i][j], term)
                total = term if total is None else total + term
            o_refs[i][...] = total

    vm = pl.BlockSpec(memory_space=pltpu.VMEM)
    return pl.pallas_call(
        body, name="small_allreduce", in_specs=[vm] * na, out_specs=[vm] * n, out_shape=[S_(s, F32) for s in shapes],
        scratch_shapes=[pltpu.VMEM(s, F32) for s in shapes] * 3 + [pltpu.VMEM((3,) + s, F32) for s in shapes]
        + [pltpu.SemaphoreType.DMA((n, 4)), pltpu.SemaphoreType.DMA((n, 4))],
    )(*arrays)


def _row_block(r, c, target_bytes=BLOCK_BYTES):
    br = r
    while br % (2 * BF16_ROWS) == 0 and br * c * 4 > target_bytes:
        br //= 2
    return br


def _chip_sum(part, sib, ids, name):
    _, _, r, c = part.shape
    br = _row_block(r, c, 4 * BLOCK_BYTES)

    def body(ids_ref, p_ref, s_ref, o_ref):
        o_ref[...] = (p_ref[...].astype(F32) + s_ref[...].astype(F32)).astype(BF16)

    return _pcall(body, name=name, grid=(N_CHIPS, r // br), prefetch=1,
                  in_specs=[pl.BlockSpec((None, None, br, c), lambda s, i, ids: (s, ids[1], i, 0)),
                            pl.BlockSpec((None, br, c), lambda s, i, ids: (s, i, 0))],
                  out_specs=pl.BlockSpec((None, br, c), lambda s, i, ids: (s, i, 0)),
                  out_shape=S_((N_CHIPS, r, c), BF16), args=(ids, part, sib))


def _final_sum(part, sib, got, ids, name):
    _, r, c = got.shape
    br = _row_block(r, c, 2 * BLOCK_BYTES)

    def body(ids_ref, p_ref, s_ref, got_ref, o_ref):
        v = p_ref[...].astype(F32) + s_ref[...].astype(F32)
        for j in range(3):
            v = v + got_ref[j].astype(F32)
        o_ref[...] = v

    return _pcall(body, name=name, grid=(r // br,), prefetch=1,
                  in_specs=[pl.BlockSpec((None, None, br, c), lambda i, ids: (ids[0], ids[1], i, 0)),
                            pl.BlockSpec((None, br, c), lambda i, ids: (ids[0], i, 0)),
                            pl.BlockSpec((3, br, c), lambda i, ids: (0, i, 0))],
                  out_specs=pl.BlockSpec((None, None, br, c), lambda i, ids: (0, ids[1], i, 0)),
                  out_shape=S_((1, 2, r, c), F32), args=(ids, part, sib, got))


def _final_sums(items, n_bufs, comm):
    n = len(items)
    nci, nco, ncs = len(comm.inputs), len(comm.out_shape), len(comm.sems)
    shapes = [None] * n_bufs
    blocks, geo = [], []
    for a, (part, sib, got, b, layer, nl) in enumerate(items):
        _, r, c = got.shape
        br = _row_block(r, c)
        shapes[b] = S_((nl, 2, r, c), F32)
        geo.append((br, c))
        blocks += [(a, k, k * br, br) for k in range(r // br)]
    n_src = 5

    def body(*refs):
        srcs = [refs[3 * a:3 * a + 3] for a in range(n)]
        cin = refs[3 * n:3 * n + nci]
        outs = refs[3 * n + nci:3 * n + nci + n_bufs]
        cout = refs[3 * n + nci + n_bufs:3 * n + nci + n_bufs + nco]
        rest = refs[3 * n + nci + n_bufs + nco:]
        inb, outb = rest[:n], rest[n:2 * n]
        lsem, ssem = rest[2 * n], rest[2 * n + 1]
        csems = rest[2 * n + 2:]
        x, y, c, _ = _place()
        q = 2 * x + y
        comm.before(cin, cout, csems, 0, 1)

        def loads(a, k, r0, br):
            part, sib, got = srcs[a]
            rows = pl.ds(r0, br)
            pieces = [part.at[q, c, rows], sib.at[q, rows]] + [got.at[j, rows] for j in range(3)]
            return [pltpu.make_async_copy(p, inb[a].at[k % 2, j], lsem.at[a, k % 2, j]) for j, p in enumerate(pieces)]

        def store(a, k, r0, br):
            _, _, _, b, layer, _ = items[a]
            return pltpu.make_async_copy(outb[a].at[k % 2], outs[b].at[layer, c, pl.ds(r0, br)], ssem.at[a, k % 2])

        for cp in loads(*blocks[0]):
            cp.start()
        for i, (a, k, r0, br) in enumerate(blocks):
            if i + 1 < len(blocks):
                for cp in loads(*blocks[i + 1]):
                    cp.start()
            for cp in loads(a, k, r0, br):
                cp.wait()
            if k >= 2:
                store(a, k - 2, r0 - 2 * br, br).wait()
            v = inb[a][k % 2, 0].astype(F32) + inb[a][k % 2, 1].astype(F32)
            for j in range(2, n_src):
                v = v + inb[a][k % 2, j].astype(F32)
            outb[a][k % 2] = v
            store(a, k, r0, br).start()
        for a in range(n):
            cnt = sum(1 for blk in blocks if blk[0] == a)
            for kk in range(max(cnt - 2, 0), cnt):
                store(a, kk, kk * geo[a][0], geo[a][0]).wait()
        comm.after(cin, cout, csems, 0, 1)

    any_spec = pl.BlockSpec(memory_space=pl.ANY)
    scratch = ([pltpu.VMEM((2, n_src) + g, BF16) for g in geo] + [pltpu.VMEM((2,) + g, F32) for g in geo]
               + [pltpu.SemaphoreType.DMA((n, 2, n_src)), pltpu.SemaphoreType.DMA((n, 2))] + list(comm.sems))
    args = [t for it in items for t in it[:3]] + list(comm.inputs)
    outs = pl.pallas_call(body, name="final_sums", in_specs=[any_spec] * len(args), out_specs=[any_spec] * (n_bufs + nco),
                          out_shape=shapes + list(comm.out_shape), scratch_shapes=scratch,
                          compiler_params=pltpu.CompilerParams(vmem_limit_bytes=VMEM_LIMIT_BYTES))(*args)
    return list(outs[:n_bufs]), list(outs[n_bufs:])


def _adam_math(w, g, m, v):
    c1 = 1.0 / (1.0 - ADAM_B1 ** ADAM_STEP)
    c2 = 1.0 / (1.0 - ADAM_B2 ** ADAM_STEP)
    mn = ADAM_B1 * m + (1.0 - ADAM_B1) * g
    vn = ADAM_B2 * v + (1.0 - ADAM_B2) * (g * g)
    return -ADAM_LR * ((mn * c1) / (jnp.sqrt(vn * c2) + ADAM_EPS) + ADAM_WD * w), mn, vn


def _adamw(w, g, m, v, name):
    nl, r, c = w.shape
    br = _row_block(r, c)

    def body(w_ref, g_ref, m_ref, v_ref, go_ref, d_ref, mo_ref, vo_ref):
        gv = g_ref[...]
        go_ref[...] = gv
        d_ref[...], mo_ref[...], vo_ref[...] = _adam_math(w_ref[...], gv, m_ref[...], v_ref[...])

    blk = pl.BlockSpec((None, br, c), lambda l, i: (l, i, 0))
    return _pcall(body, name=name, grid=(nl, r // br), in_specs=[blk] * 4, out_specs=[blk] * 4,
                  out_shape=[S_((nl, r, c), F32)] * 4, args=(w, g, m, v))


def _adamw_small(rd, rf, sharded_g, ws, ms, vs, rows_d, rows_f, loss_row):
    n = len(ws)
    ns = len(sharded_g)
    rep = [k for k in range(n) if rows_d[k] is not None or rows_f[k] is not None]

    def body(*refs):
        rd_ref, rf_ref = refs[0], refs[1]
        sg = refs[2:2 + ns]
        w_refs, m_refs, v_refs = (refs[2 + ns + t * n:2 + ns + (t + 1) * n] for t in range(3))
        loss_ref, outs = refs[2 + ns + 3 * n], refs[3 + ns + 3 * n:]
        g_out, d_out, m_out, v_out = outs[:len(rep)], outs[len(rep):len(rep) + n], outs[len(rep) + n:len(rep) + 2 * n], \
            outs[len(rep) + 2 * n:]
        loss_ref[...] = jnp.sum(rd_ref[loss_row:loss_row + 1, :], axis=1, keepdims=True)
        si = 0
        for k in range(n):
            shape = w_refs[k].shape
            if rows_d[k] is not None or rows_f[k] is not None:
                src, rws = (rd_ref, rows_d[k]) if rows_d[k] is not None else (rf_ref, rows_f[k])
                axis = 0 if shape[0] == len(rws) else 1
                g = src[rws[0]:rws[0] + 1, :]
                if len(rws) > 1:
                    g = jnp.concatenate([src[r:r + 1, :] for r in rws], axis=axis)
                g_out[rep.index(k)][...] = g
            else:
                g = sg[si][...]
                si += 1
            d_out[k][...], m_out[k][...], v_out[k][...] = _adam_math(w_refs[k][...], g, m_refs[k][...], v_refs[k][...])

    vm = pl.BlockSpec(memory_space=pltpu.VMEM)
    shapes = [S_(w.shape, F32) for w in ws]
    outs = pl.pallas_call(
        body, name="adamw_small", in_specs=[vm] * (2 + ns + 3 * n), out_specs=[vm] * (1 + len(rep) + 3 * n),
        out_shape=[S_((1, 1), F32)] + [shapes[k] for k in rep] + shapes * 3,
    )(rd, rf, *sharded_g, *ws, *ms, *vs)
    g_rep = dict(zip(rep, outs[1:1 + len(rep)]))
    o = outs[1 + len(rep):]
    return outs[0], g_rep, o[:n], o[n:2 * n], o[2 * n:]


def _pack(pieces, rows):
    flat = jnp.concatenate([p.reshape(-1).astype(F32) for p in pieces])
    return jnp.pad(flat, (0, rows * LANES - flat.shape[0])).reshape(rows, LANES)


def _unpack(packed, shapes):
    flat = packed.reshape(-1)
    out, o = [], 0
    for s in shapes:
        n = 1
        for d in s:
            n *= d
        out.append(flat[o:o + n].reshape(s))
        o += n
    return out


def _rows_for(pieces_or_shapes):
    n = 0
    for p in pieces_or_shapes:
        k = 1
        for d in (p if isinstance(p, tuple) else p.shape):
            k *= d
        n += k
    return -(-n // (SUBLANES * LANES)) * SUBLANES


def kernel(x, norm_mix, norm_ffn, conv_w_pw1, conv_b_pw1, conv_w_dw, conv_b_dw, conv_ln_g, conv_ln_b, conv_w_pw2, conv_b_pw2, pool_w, pool_b, pool_scale, ffn_w_up, ffn_w_dw, ffn_b_dw, ffn_w_down, final_norm, loss_target, m_norm_mix, m_norm_ffn, m_conv_w_pw1, m_conv_b_pw1, m_conv_w_dw, m_conv_b_dw, m_conv_ln_g, m_conv_ln_b, m_conv_w_pw2, m_conv_b_pw2, m_pool_w, m_pool_b, m_pool_scale, m_ffn_w_up, m_ffn_w_dw, m_ffn_b_dw, m_ffn_w_down, m_final_norm, v_norm_mix, v_norm_ffn, v_conv_w_pw1, v_conv_b_pw1, v_conv_w_dw, v_conv_b_dw, v_conv_ln_g, v_conv_ln_b, v_conv_w_pw2, v_conv_b_pw2, v_pool_w, v_pool_b, v_pool_scale, v_ffn_w_up, v_ffn_w_dw, v_ffn_b_dw, v_ffn_w_down, v_final_norm):
    nb, seq, D = x.shape
    T = nb * seq
    F = ffn_w_down.shape[1] * N_CHIPS
    taps = conv_w_dw.shape[1]
    ffn_taps = ffn_w_dw.shape[1]
    ng = pool_w.shape[1]
    cg = pool_w.shape[3]
    dsh = D // N_CHIPS
    fsh = F // N_CHIPS
    assert taps - 1 <= HALO_CONV and ffn_taps - 1 <= HALO_FFN and max(POOL_WINDOWS) <= HALO_POOL
    tm = min(TILE_D, seq)
    tf = min(TILE_F, seq)
    assert seq % tm == 0 and seq % tf == 0 and tm % HALO_CONV == 0 and tf % CHUNK == 0
    tps, tpf = seq // tm, seq // tf
    tl = min(2 * tm, T)
    assert T % tl == 0

    xi, yi, ci = lax.axis_index("x"), lax.axis_index("y"), lax.axis_index("c")
    qi = 2 * xi + yi
    ids = jnp.stack([qi, ci]).astype(jnp.int32)
    x2d = x.reshape(T, D)
    tg2d = loss_target.reshape(T, D)

    small_sharded = [conv_w_dw[0], ffn_w_dw, pool_b, pool_scale]
    srows = 2 * _rows_for(small_sharded)
    w1g, s_pw2, s_small, s_pool, s_up0, s_up1, s_down0, s_down1 = _own_slots([
        (conv_w_pw1, 0, BF16, False), (conv_w_pw2, 0, BF16, False), (_pack(small_sharded, srows)[None], 0, F32, False),
        (pool_w.reshape(1, ng * (cg // N_CHIPS), cg), 0, BF16, False), (ffn_w_up, 0, BF16, True), (ffn_w_up, 1, BF16, True),
        (ffn_w_down, 0, BF16, False), (ffn_w_down, 1, BF16, False)])
    s_up, s_down = [s_up0, s_up1], [s_down0, s_down1]
    hb0t, a, w2g, smallg, pwg = _conv_in(x2d, norm_mix[0:1], w1g, conv_b_pw1, tl, comm=_GatherComm([s_pw2, s_small, s_pool]))
    w2f = w2g.reshape(D, D)
    sm_parts = [_unpack(smallg[s], [(taps, dsh), (2, ffn_taps, fsh), (1, dsh), (1, dsh)]) for s in range(N_CHIPS)]
    wdw_f = jnp.concatenate([p[0] for p in sm_parts], axis=1)
    wdw_f = jnp.pad(wdw_f, ((0, HALO_CONV - taps), (0, 0)))
    fdw_f = jnp.concatenate([p[1] for p in sm_parts], axis=2)
    fdw_f = jnp.pad(fdw_f, ((0, 0), (0, 8 - ffn_taps), (0, 0)))
    pb_f = jnp.concatenate([p[2] for p in sm_parts], axis=1)
    ps_f = jnp.concatenate([p[3] for p in sm_parts], axis=1)

    c, sbt, x1, hb1, hb1t, wup0, wd0 = _conv_mid(a, x2d, wdw_f, conv_b_dw, conv_ln_g, conv_ln_b, w2f, conv_b_pw2,
                                                 norm_ffn[0:1], tm, tps, taps, comm=_GatherComm([s_up[0], s_down[0]]))
    up0, gb0t, x2, h2, wup1, wd1 = _ffn_fwd(hb1, x1, wup0, fdw_f[0], ffn_b_dw[0:1], wd0.reshape(F, D), norm_mix[1:2], tf, tpf,
                                            ffn_taps, "ffn_fwd0", comm=_GatherComm([s_up[1], s_down[1]]))
    pwf = pwg.reshape(N_CHIPS, ng, cg // N_CHIPS, cg).transpose(1, 0, 2, 3).reshape(ng, cg, cg)
    plb, x3, hb3, hb3t = _pool_fwd(h2, x2, pwf, pb_f, ps_f, norm_ffn[1:2], tm, tps)
    wup = [wup0, wup1]
    wdn = [wd0.reshape(F, D), wd1.reshape(F, D)]
    up1, gb1t, dx4, dx4b, loss_part, d_final = _ffn_fwd(hb3, x3, wup[1], fdw_f[1], ffn_b_dw[1:2], wdn[1], final_norm.reshape(1, D),
                                                  tf, tpf, ffn_taps, "ffn_fwd1", target=tg2d)

    fs = {}

    def pair(group):
        return list(group), list(group.values()), _PairComm(list(group.values()))

    def chip_sums(nms, parts, from_sib):
        for nm, p, s in zip(nms, parts, from_sib):
            fs[nm] = (p, s)
        return nms, _ExchangeComm([_chip_sum(p, s, ids, "chip_sum_" + nm) for nm, p, s in zip(nms, parts, from_sib)])

    def arrived(nms, got):
        for nm, g in zip(nms, got):
            fs[nm] = fs[nm] + (g,)

    dac1, dup1, fw1 = _bwd_down(dx4, up1, fdw_f[1], ffn_b_dw[1:2], wdn[1], tf, tpf, ffn_taps, "bwd_down1")
    p_down1 = _mm_tn(gb1t, dx4b, F // 4, D, False, "dw_down1")
    nms, parts, px = pair({"down1": p_down1.reshape(N_CHIPS, 2, fsh // 2, D)})
    dup1, dx3, _, dnf1, *sib = _bwd_up(dac1, dup1, wup[1], fdw_f[1], x3, norm_ffn[1:2], dx4, tf, tpf, ffn_taps, "bwd_up1", comm=px)
    nms, xchg = chip_sums(nms, parts, sib)
    p_up1, *got = _mm_tn(hb3t, dup1, D // 2, 2 * fsh, True, "dw_up1", comm=xchg)
    arrived(nms, got)
    nms, parts, px = pair({"up1": p_up1.reshape(N_CHIPS, 2, D // 2, 2 * fsh)})
    dx2, dx2b, dpw, pool_sm, *sib = _pool_bwd(dx3, plb, x2, pwf, pb_f, ps_f, norm_mix[1:2], tm, tps, comm=px)
    nms, xchg = chip_sums(nms, parts, sib)
    dac0, dup0, fw0, *got = _bwd_down(dx2, up0, fdw_f[0], ffn_b_dw[0:1], wdn[0], tf, tpf, ffn_taps, "bwd_down0", comm=xchg)
    arrived(nms, got)
    p_down0 = _mm_tn(gb0t, dx2b, F // 4, D, False, "dw_down0")
    nms, parts, px = pair({"down0": p_down0.reshape(N_CHIPS, 2, fsh // 2, D)})
    dup0, dx1, dx1b, dnf0, *sib = _bwd_up(dac0, dup0, wup[0], fdw_f[0], x1, norm_ffn[0:1], dx2, tf, tpf, ffn_taps, "bwd_up0", comm=px)
    nms, xchg = chip_sums(nms, parts, sib)
    p_up0, *got = _mm_tn(hb1t, dup0, D // 2, 2 * fsh, True, "dw_up0", comm=xchg)
    arrived(nms, got)
    csh = cg // N_CHIPS
    p_pool = dpw.reshape(2, ng // 2, N_CHIPS, csh, cg).transpose(2, 0, 1, 3, 4).reshape(N_CHIPS, 2, (ng // 2) * csh, cg)
    nms, parts, px = pair({"up0": p_up0.reshape(N_CHIPS, 2, D // 2, 2 * fsh), "pool": p_pool.astype(BF16)})
    dc, conv_sm, *sib = _conv_bwd1(dx1, c, w2f, conv_ln_g, conv_ln_b, tl, comm=px)
    nms, xchg = chip_sums(nms, parts, sib)
    p_pw2 = _mm_tn(sbt, dx1b, D, D, False, "dw_pw2")
    nms2, parts2, px = pair({"pw2": p_pw2.reshape(N_CHIPS, 2, dsh // 2, D)})
    grad_x, dab, dwdw, db1, dnm0, *both = _conv_bwd2(dc, a, x2d, w1g, wdw_f, norm_mix[0:1], dx1, tm, tps, taps,
                                                     comm=_Both(xchg, px))
    arrived(nms, both[:len(nms)])
    nms, xchg = chip_sums(nms2, parts2, both[len(nms):])
    p_pw1, *got = _mm_tn(hb0t, dab, D, 2 * dsh, True, "dw_pw1", comm=xchg)
    arrived(nms, got)
    nms, parts, _ = pair({"pw1": p_pw1.reshape(N_CHIPS, 2, D // 2, 2 * dsh)})
    nms, xchg = chip_sums(nms, parts, _pair_exchange(parts, "grad_pair_exchange_pw1"))
    bufs, got = _final_sums([(*fs["pw2"], 0, 0, 1), (*fs["pool"], 1, 0, 1), (*fs["up0"], 2, 0, 2), (*fs["up1"], 2, 1, 2),
                             (*fs["down0"], 3, 0, 2), (*fs["down1"], 3, 1, 2)], 4, xchg)
    arrived(nms, got)
    bufs.append(_final_sum(*fs["pw1"], ids, "final_sum_pw1"))
    shared = _comm_only(_ShareComm(bufs), "grad_share_halves")
    g_pw2, g_pool, g_up, g_down, g_pw1 = [b.reshape(b.shape[0], 2 * b.shape[2], b.shape[3]) for b in shared]
    rd, rf = _small_allreduce([
        [(dnm0, 0, 1), (pool_sm, 2, 3), (dnf0, 0, 1), (dnf1, 0, 1), (db1, 0, 2), (conv_sm, 0, 4), (d_final, 0, 1),
         (pool_sm, 0, 2), (loss_part, 0, 1), 2, (dwdw, 0, dwdw.shape[0])],
        [(fw0, 0, 4), (fw1, 0, 4)]])
    g_wdw = lax.dynamic_slice_in_dim(rd[16:16 + taps], qi * dsh, dsh, axis=1)
    g_ps = lax.dynamic_slice_in_dim(rd[11:12], qi * dsh, dsh, axis=1)
    g_pb = lax.dynamic_slice_in_dim(rd[12:13], qi * dsh, dsh, axis=1)
    g_fdw = lax.dynamic_slice_in_dim(rf.reshape(2, 4, F)[:, :ffn_taps], qi * fsh, fsh, axis=2).reshape(2 * ffn_taps, fsh)

    def big(w, g, m, v, nm):
        return tuple(t.reshape(w.shape) for t in _adamw(w.reshape(g.shape), g, m.reshape(g.shape), v.reshape(g.shape),
                                                        "adamw_" + nm))

    o_pw1 = big(conv_w_pw1, g_pw1, m_conv_w_pw1, v_conv_w_pw1, "pw1")
    o_pw2 = big(conv_w_pw2, g_pw2, m_conv_w_pw2, v_conv_w_pw2, "pw2")
    o_pool = big(pool_w, g_pool, m_pool_w, v_pool_w, "pool")
    o_up = big(ffn_w_up, g_up, m_ffn_w_up, v_ffn_w_up, "up")
    o_down = big(ffn_w_down, g_down, m_ffn_w_down, v_ffn_w_down, "down")

    snames = ["norm_mix", "norm_ffn", "b_pw1", "w_dw", "b_dw", "ln_g", "ln_b", "b_pw2", "pool_b", "pool_scale", "ffn_w_dw",
              "ffn_b_dw", "final_norm"]
    sw = [norm_mix, norm_ffn, conv_b_pw1, conv_w_dw, conv_b_dw, conv_ln_g, conv_ln_b, conv_b_pw2, pool_b, pool_scale,
          ffn_w_dw, ffn_b_dw, final_norm]
    smm = [m_norm_mix, m_norm_ffn, m_conv_b_pw1, m_conv_w_dw, m_conv_b_dw, m_conv_ln_g, m_conv_ln_b, m_conv_b_pw2, m_pool_b,
           m_pool_scale, m_ffn_w_dw, m_ffn_b_dw, m_final_norm]
    svv = [v_norm_mix, v_norm_ffn, v_conv_b_pw1, v_conv_w_dw, v_conv_b_dw, v_conv_ln_g, v_conv_ln_b, v_conv_b_pw2, v_pool_b,
           v_pool_scale, v_ffn_w_dw, v_ffn_b_dw, v_final_norm]
    sshapes = [tuple(w.shape) for w in sw]
    two_d = [(2, D), (2, D), (1, 2 * D), (taps, dsh), (1, D), (1, D), (1, D), (1, D), (1, dsh), (1, dsh), (2 * ffn_taps, fsh),
             (2, F), (1, D)]
    rows_d = [[0, 1], [2, 3], [4, 5], None, [9], [6], [7], [8], None, None, None, None, [10]]
    rows_f = [None] * 11 + [[3, 7], None]
    as2d = lambda ts: [t.reshape(s) for t, s in zip(ts, two_d)]
    loss, g_rep, sd, sm_new, sv_new = _adamw_small(rd, rf, [g_wdw, g_pb, g_ps, g_fdw], as2d(sw), as2d(smm), as2d(svv), rows_d,
                                                   rows_f, loss_row=13)
    sg = dict(g_rep)
    sg.update({3: g_wdw, 8: g_pb, 9: g_ps, 10: g_fdw})
    small_out = {n: tuple(t.reshape(sshapes[k]) for t in (sg[k], sd[k], sm_new[k], sv_new[k])) for k, n in enumerate(snames)}

    order = [small_out["norm_mix"], small_out["norm_ffn"], o_pw1, small_out["b_pw1"], small_out["w_dw"], small_out["b_dw"],
             small_out["ln_g"], small_out["ln_b"], o_pw2, small_out["b_pw2"], o_pool, small_out["pool_b"],
             small_out["pool_scale"], o_up, small_out["ffn_w_dw"], small_out["ffn_b_dw"], o_down, small_out["final_norm"]]
    return (loss[0, 0], grad_x.reshape(nb, seq, D), *[o[0] for o in order], *[o[1] for o in order], *[o[2] for o in order],
            *[o[3] for o in order])
```

```python
import functools

import jax
import jax.numpy as jnp
from jax import lax
from jax.experimental import pallas as pl
from jax.experimental.pallas import tpu as pltpu

F32, BF16 = jnp.float32, jnp.bfloat16
S_ = jax.ShapeDtypeStruct
MESH = pl.DeviceIdType.MESH

RMS_EPS, LN_EPS = 1e-6, 1e-5
POOL_WINDOWS = (2, 4, 8, 16)
ADAM_LR, ADAM_B1, ADAM_B2, ADAM_EPS, ADAM_WD, ADAM_STEP = 0.001, 0.9, 0.999, 1e-08, 0.01, 10

VMEM_LIMIT_BYTES = 60 * 1024 * 1024
SUBLANES, LANES = 8, 128
BF16_ROWS = 16
BLOCK_BYTES = 1 << 20
N_CHIPS = 4
CHUNK = 32
HALO_CONV = 32
HALO_POOL = 32
HALO_FFN = 8
TILE_D = 512
TILE_F = 256


def _pcall(body, *, name, grid, in_specs, out_specs, out_shape, args, scratch=(), aliases=None, prefetch=0, comm=None):
    params = pltpu.CompilerParams(dimension_semantics=("arbitrary",) * len(grid), vmem_limit_bytes=VMEM_LIMIT_BYTES)
    if comm is not None:
        assert not prefetch
        single = not isinstance(out_shape, (list, tuple))
        in_specs, args, scratch = list(in_specs), list(args), list(scratch)
        out_specs, out_shape = ([out_specs], [out_shape]) if single else (list(out_specs), list(out_shape))
        n_in, n_out, n_scr, n_cin, n_cout = len(in_specs), len(out_specs), len(scratch), len(comm.inputs), len(comm.out_shape)
        any_spec = pl.BlockSpec(memory_space=pl.ANY)
        aliases = dict(aliases or {})
        aliases.update({n_in + a: n_out + b for a, b in comm.aliases.items()})
        inner = body

        def body(*refs):
            ins, cin = refs[:n_in], refs[n_in:n_in + n_cin]
            outs = refs[n_in + n_cin:n_in + n_cin + n_out]
            cout = refs[n_in + n_cin + n_out:n_in + n_cin + n_out + n_cout]
            scr = refs[n_in + n_cin + n_out + n_cout:n_in + n_cin + n_out + n_cout + n_scr]
            sems = refs[n_in + n_cin + n_out + n_cout + n_scr:]
            step, total = 0, 1
            for d, g in enumerate(grid):
                step = step * g + pl.program_id(d)
                total *= g
            comm.before(cin, cout, sems, step, total)
            inner(*ins, *outs, *scr)
            comm.after(cin, cout, sems, step, total)

        in_specs += [any_spec] * n_cin
        args += list(comm.inputs)
        out_specs += [any_spec] * n_cout
        out_shape += list(comm.out_shape)
        scratch += list(comm.sems)
    if prefetch:
        spec = pltpu.PrefetchScalarGridSpec(num_scalar_prefetch=prefetch, grid=grid, in_specs=in_specs,
                                            out_specs=out_specs, scratch_shapes=list(scratch))
        return pl.pallas_call(body, name=name, grid_spec=spec, out_shape=out_shape,
                              input_output_aliases=aliases or {}, compiler_params=params)(*args)
    return pl.pallas_call(body, name=name, grid=grid, in_specs=in_specs, out_specs=out_specs, out_shape=out_shape,
                          scratch_shapes=list(scratch), input_output_aliases=aliases or {},
                          compiler_params=params)(*args)


def _full(shape):
    n = len(shape)
    return pl.BlockSpec(tuple(shape), lambda *_: (0,) * n, pipeline_mode=pl.Buffered(1))


def _rows(tm, c, col=0):
    return pl.BlockSpec((tm, c), lambda i, *_: (i, col))


def _cols(r, tm):
    return pl.BlockSpec((None, r, tm), lambda i, *_: (i, 0, 0))


def _prev_halo(tm, hb, c):
    return pl.BlockSpec((hb, c), lambda i, *_: (jnp.maximum(i * (tm // hb) - 1, 0), 0))


def _next_halo(tm, hb, c, total_rows):
    last = total_rows // hb - 1
    return pl.BlockSpec((hb, c), lambda i, *_: (jnp.minimum((i + 1) * (tm // hb), last), 0))


def _acc_spec(r, c):
    return pl.BlockSpec((r, c), lambda *_: (0, 0))


def _rms(x, g):
    r = lax.rsqrt(jnp.mean(x * x, axis=-1, keepdims=True) + RMS_EPS)
    xh = x * r
    return xh * g, xh, r


def _rms_bwd(dy, xh, r, g):
    dxh = dy * g
    dx = r * (dxh - xh * jnp.mean(dxh * xh, axis=-1, keepdims=True))
    return dx, jnp.sum(dy * xh, axis=0, keepdims=True)


def _colsum(v):
    return jnp.sum(v, axis=0, keepdims=True)


def _fold8(v):
    out = v[0:8]
    for j in range(1, v.shape[0] // 8):
        out = out + v[8 * j:8 * j + 8]
    return out


def _chunks(n_rows, fn):
    def step(j, carry):
        fn(pl.multiple_of(j * CHUNK, CHUNK))
        return carry
    lax.fori_loop(0, n_rows // CHUNK, step, 0)


def _dot(a, b):
    return jnp.dot(a, b, preferred_element_type=F32)


def _dot_nt(a, b):
    return lax.dot_general(a, b, (((1,), (1,)), ((), ())), preferred_element_type=F32)


def _dot_tn(a, b):
    return lax.dot_general(a, b, (((0,), (0,)), ((), ())), preferred_element_type=F32)


def _sig(v):
    return jax.nn.sigmoid(v)


def _accumulate(ref, val, first):
    @pl.when(first)
    def _():
        ref[...] = val

    @pl.when(jnp.logical_not(first))
    def _():
        ref[...] = ref[...] + val


def _conv_in(x, g, w1g, b1, tm, comm=None):
    T, D = x.shape
    ns, _, nc = w1g.shape

    def body(x_ref, g_ref, w_ref, b_ref, hbt_ref, a_ref):
        hb = _rms(x_ref[...], g_ref[...])[0].astype(BF16)
        hbt_ref[...] = hb.T
        for s in range(ns):
            a_ref[:, s * nc:(s + 1) * nc] = _dot(hb, w_ref[s]) + b_ref[:, s * nc:(s + 1) * nc]

    return _pcall(body, name="conv_in", grid=(T // tm,),
                  in_specs=[_rows(tm, D), _full((1, D)), _full(w1g.shape), _full((1, ns * nc))],
                  out_specs=[_cols(D, tm), _rows(tm, ns * nc)],
                  out_shape=[S_((T // tm, D, tm), BF16), S_((T, ns * nc), F32)], args=(x, g, w1g, b1), comm=comm)


def _conv_mid(a, x, wdw, bdw, lng, lnb, w2, b2, gn, tm, tps, taps, comm=None):
    T, D = x.shape
    hc = HALO_CONV

    def body(a_ref, ah_ref, x_ref, wdw_ref, bdw_ref, lng_ref, lnb_ref, w2_ref, b2_ref, gn_ref,
             c_ref, sbt_ref, x1_ref, hb_ref, hbt_ref, uext, shifted):
        i = pl.program_id(0)
        ah = ah_ref[...]
        uext[0:hc, :] = jnp.where(i % tps == 0, 0.0, ah[:, :D] * _sig(ah[:, D:]))
        av = a_ref[...]
        uext[hc:hc + tm, :] = av[:, :D] * _sig(av[:, D:])

        def conv(r0, src, ks, first):
            acc = jnp.broadcast_to(bdw_ref[...], (CHUNK, D)) if first else c_ref[pl.ds(r0, CHUNK), :]
            for k, o in ks:
                acc = acc + wdw_ref[k:k + 1, :] * src[pl.ds(pl.multiple_of(r0 + o, 8), CHUNK), :]
            c_ref[pl.ds(r0, CHUNK), :] = acc
        _by_sublane_shift(uext, shifted, tm, [hc - (taps - 1) + k for k in range(taps)], conv)

        cv = c_ref[...]
        xc = cv - jnp.mean(cv, axis=-1, keepdims=True)
        nh = xc * lax.rsqrt(jnp.mean(xc * xc, axis=-1, keepdims=True) + LN_EPS)
        n = nh * lng_ref[...] + lnb_ref[...]
        sb = (n * _sig(n)).astype(BF16)
        sbt_ref[...] = sb.T
        x1 = x_ref[...] + _dot(sb, w2_ref[...]) + b2_ref[...]
        x1_ref[...] = x1
        hb = _rms(x1, gn_ref[...])[0].astype(BF16)
        hb_ref[...] = hb
        hbt_ref[...] = hb.T

    return _pcall(body, name="conv_mid", grid=(T // tm,),
                  in_specs=[_rows(tm, 2 * D), _prev_halo(tm, hc, 2 * D), _rows(tm, D), _full(wdw.shape), _full((1, D)),
                            _full((1, D)), _full((1, D)), _full((D, D)), _full((1, D)), _full((1, D))],
                  out_specs=[_rows(tm, D), _cols(D, tm), _rows(tm, D), _rows(tm, D), _cols(D, tm)],
                  out_shape=[S_((T, D), F32), S_((T // tm, D, tm), BF16), S_((T, D), F32), S_((T, D), BF16), S_((T // tm, D, tm), BF16)],
                  scratch=[pltpu.VMEM((tm + hc, D), F32), pltpu.VMEM((tm + hc - 8, D), F32)],
                  args=(a, a, x, wdw, bdw, lng, lnb, w2, b2, gn), comm=comm)


def _ffn_taps(aext, realigned, tm, ffn_taps):
    for k in range(ffn_taps - 1):
        realigned[k] = aext[pl.ds(HALO_FFN - (ffn_taps - 1) + k, tm), :]
    return [realigned.at[k] for k in range(ffn_taps - 1)] + [aext.at[pl.ds(HALO_FFN, tm)]]


def _ffn_conv(taps, wdw_ref, bdw_ref):
    acc = bdw_ref[...]
    for k, tap in enumerate(taps):
        acc = acc + wdw_ref[k:k + 1, :] * tap[...]
    return acc


def _by_sublane_shift(ext, shifted, tm, offsets, per_chunk):
    first = True
    for b in range(8):
        ks = [(k, o - b) for k, o in enumerate(offsets) if o % 8 == b]
        if not ks:
            continue
        if b:
            shifted[...] = ext[pl.ds(b, shifted.shape[0]), :]
        _chunks(tm, functools.partial(per_chunk, src=shifted if b else ext, ks=ks, first=first))
        first = False


def _ffn_fwd(hb, xin, wg, wdw, bdw, wd, gn, tm, tps, ffn_taps, name, target=None, comm=None):
    T, D = xin.shape
    ns, _, nc = wg.shape
    F = ns * nc // 2
    hf = HALO_FFN
    hh = 2 * HALO_FFN
    final = target is not None

    def body(*refs):
        if final:
            (h_ref, hh_ref, x_ref, wg_ref, wdw_ref, bdw_ref, wd_ref, gn_ref, tg_ref,
             up_ref, gb_ref, dx_ref, dxb_ref, loss_ref, dgn_ref, aext, realigned) = refs
        else:
            (h_ref, hh_ref, x_ref, wg_ref, wdw_ref, bdw_ref, wd_ref, gn_ref,
             up_ref, gb_ref, xo_ref, ho_ref, aext, realigned) = refs
        i = pl.program_id(0)
        hv = h_ref[...]
        for s in range(ns):
            up_ref[:, s * nc:(s + 1) * nc] = _dot(hv, wg_ref[s])
        halo = jnp.concatenate([_dot(hh_ref[...], wg_ref[s]) for s in range(ns // 2)], axis=1)[hh - hf:hh, :]
        aext[0:hf, :] = jnp.where(i % tps == 0, 0.0, halo)
        aext[hf:hf + tm, :] = up_ref[:, 0:F]
        acv = _ffn_conv(_ffn_taps(aext, realigned, tm, ffn_taps), wdw_ref, bdw_ref)
        gb = (acv * _sig(acv) * up_ref[:, F:2 * F]).astype(BF16)
        gb_ref[...] = gb.T
        xo = x_ref[...] + _dot(gb, wd_ref[...])
        if not final:
            xo_ref[...] = xo
            ho_ref[...] = _rms(xo, gn_ref[...])[0]
        else:
            out, xh, r = _rms(xo, gn_ref[...])
            diff = out - tg_ref[...]
            part = 0.5 / D * _colsum(diff * diff)
            dx, dg = _rms_bwd(diff * (1.0 / D), xh, r, gn_ref[...])
            dx_ref[...] = dx
            dxb_ref[...] = dx.astype(BF16)
            _accumulate(loss_ref, part, i == 0)
            _accumulate(dgn_ref, dg, i == 0)

    in_specs = [_rows(tm, D), _prev_halo(tm, hh, D), _rows(tm, D), _full(wg.shape), _full(wdw.shape), _full((1, F)),
                _full((F, D)), _full((1, D))]
    args = [hb, hb, xin, wg, wdw, bdw, wd, gn]
    if final:
        in_specs.append(_rows(tm, D))
        args.append(target)
        out_specs = [_rows(tm, 2 * F), _cols(F, tm), _rows(tm, D), _rows(tm, D), _acc_spec(1, D), _acc_spec(1, D)]
        out_shape = [S_((T, 2 * F), F32), S_((T // tm, F, tm), BF16), S_((T, D), F32), S_((T, D), BF16), S_((1, D), F32),
                     S_((1, D), F32)]
    else:
        out_specs = [_rows(tm, 2 * F), _cols(F, tm), _rows(tm, D), _rows(tm, D)]
        out_shape = [S_((T, 2 * F), F32), S_((T // tm, F, tm), BF16), S_((T, D), F32), S_((T, D), F32)]
    return _pcall(body, name=name, grid=(T // tm,), in_specs=in_specs, out_specs=out_specs, out_shape=out_shape,
                  scratch=[pltpu.VMEM((tm + hf, F), F32), pltpu.VMEM((ffn_taps - 1, tm, F), F32)], args=args, comm=comm)


def _pool_counts(i, tps, tm, w):
    pos = (i % tps) * tm + lax.broadcasted_iota(jnp.int32, (tm, 1), 0)
    return jnp.minimum(pos + 1, w).astype(F32)


def _window_sums(src, buf_a, buf_b, cg, causal):
    assert POOL_WINDOWS == (2, 4, 8, 16)
    n = src.shape[0]
    levels = len(POOL_WINDOWS)
    cur, outs = src, []
    for l in range(levels):
        dst = buf_b if l % 2 else buf_a
        cols = slice(l * cg, levels * cg)
        lo, m = 8 * (l + 1), n - 8 * (l + 1)
        if causal:
            dst[lo:n, cols] = cur[lo:n, cols] + cur[pl.ds(lo - (1 << l), m), cols]
        else:
            dst[0:m, cols] = cur[0:m, cols] + cur[pl.ds(1 << l, m), cols]
        outs.append(dst)
        cur = dst
    return outs


def _pool_fwd(h, x2, pw, pb, ps, gn, tm, tps, comm=None):
    T, D = h.shape
    ng, cg, _ = pw.shape
    hp = HALO_POOL

    def body(h_ref, hh_ref, x_ref, pw_ref, pb_ref, ps_ref, gn_ref, pl_ref, xo_ref, hb_ref, hbt_ref, hext, buf_a, buf_b):
        i = pl.program_id(0)
        hext[0:hp, :] = jnp.where(i % tps == 0, 0.0, hh_ref[...])
        hext[hp:hp + tm, :] = h_ref[...]
        sums = _window_sums(hext, buf_a, buf_b, cg, True)
        for g, w in enumerate(POOL_WINDOWS):
            cs = slice(g * cg, (g + 1) * cg)
            pl_ref[:, cs] = (sums[g][hp:hp + tm, cs] / _pool_counts(i, tps, tm, w) - h_ref[:, cs]).astype(BF16)

        pv = pl_ref[...]
        mixed = jnp.concatenate([_dot(pv[:, g * cg:(g + 1) * cg], pw_ref[g]) for g in range(ng)], axis=1)
        xo = x_ref[...] + ps_ref[...] * (mixed + pb_ref[...])
        xo_ref[...] = xo
        hb = _rms(xo, gn_ref[...])[0].astype(BF16)
        hb_ref[...] = hb
        hbt_ref[...] = hb.T

    return _pcall(body, name="pool_fwd", grid=(T // tm,),
                  in_specs=[_rows(tm, D), _prev_halo(tm, hp, D), _rows(tm, D), _full(pw.shape), _full((1, D)), _full((1, D)),
                            _full((1, D))],
                  out_specs=[_rows(tm, D), _rows(tm, D), _rows(tm, D), _cols(D, tm)],
                  out_shape=[S_((T, D), BF16), S_((T, D), F32), S_((T, D), BF16), S_((T // tm, D, tm), BF16)],
                  scratch=[pltpu.VMEM((tm + hp, D), F32)] * 3, args=(h, h, x2, pw, pb, ps, gn), comm=comm)


def _bwd_down(dx, up, wdw, bdw, wd, tm, tps, ffn_taps, name, comm=None):
    T, D = dx.shape
    F = up.shape[1] // 2
    hf = HALO_FFN
    nt = T // tm

    def body(dx_ref, up_ref, uph_ref, wdw_ref, bdw_ref, wd_ref, dac_ref, dgate_ref, wsum_ref, aext, realigned):
        i = pl.program_id(0)
        aext[0:hf, :] = jnp.where(i % tps == 0, 0.0, uph_ref[...])
        aext[hf:hf + tm, :] = up_ref[:, 0:F]
        taps = _ffn_taps(aext, realigned, tm, ffn_taps)
        acv = _ffn_conv(taps, wdw_ref, bdw_ref)
        dg = _dot_nt(dx_ref[...].astype(BF16), wd_ref[...])
        sg = _sig(acv)
        dgate_ref[...] = (dg * acv * sg).astype(BF16)
        dac = dg * up_ref[:, F:2 * F] * (sg * (1.0 + acv * (1.0 - sg)))
        dac_ref[...] = dac
        rows = [_colsum(dac * tap[...]) for tap in taps]
        rows += [_colsum(dac), jnp.zeros((8 - ffn_taps - 1, F), F32)]
        _accumulate(wsum_ref, jnp.concatenate(rows, axis=0), i == 0)

    return _pcall(body, name=name, grid=(nt,),
                  in_specs=[_rows(tm, D), _rows(tm, 2 * F), _prev_halo(tm, hf, F), _full(wdw.shape), _full((1, F)),
                            _full((F, D))],
                  out_specs=[_rows(tm, F), _rows(tm, F, col=1), _acc_spec(8, F)],
                  out_shape=[S_((T, F), F32), S_((T, 2 * F), BF16), S_((8, F), F32)],
                  scratch=[pltpu.VMEM((tm + hf, F), F32), pltpu.VMEM((ffn_taps - 1, tm, F), F32)],
                  args=(dx, up, up, wdw, bdw, wd), comm=comm)


def _bwd_up(dac, dup, wg, wdw, xin, gn, dx, tm, tps, ffn_taps, name, comm=None):
    T, D = xin.shape
    F = dac.shape[1]
    ns, _, nc = wg.shape
    hf = HALO_FFN
    nt = T // tm

    def body(dac_ref, dach_ref, dgate_ref, wg_ref, wdw_ref, x_ref, gn_ref, dx_ref, dact_ref, dxo_ref, dxb_ref, dgn_ref, dext):
        i = pl.program_id(0)
        dext[0:tm, :] = dac_ref[...]
        dext[tm:tm + hf, :] = jnp.where(i % tps == tps - 1, 0.0, dach_ref[...])
        dact = wdw_ref[ffn_taps - 1:ffn_taps, :] * dac_ref[...]
        for k in range(ffn_taps - 1):
            dact = dact + wdw_ref[k:k + 1, :] * dext[pl.ds(ffn_taps - 1 - k, tm), :]
        dact_ref[...] = dact.astype(BF16)

        dh = jnp.zeros((tm, D), F32)
        for s in range(ns):
            src = dact_ref if s < ns // 2 else dgate_ref
            o = (s % (ns // 2)) * nc
            dh = dh + _dot_nt(src[:, o:o + nc], wg_ref[s])
        _, xh, r = _rms(x_ref[...], gn_ref[...])
        dxn, dg = _rms_bwd(dh, xh, r, gn_ref[...])
        dxo = dx_ref[...] + dxn
        dxo_ref[...] = dxo
        dxb_ref[...] = dxo.astype(BF16)
        _accumulate(dgn_ref, dg, i == 0)

    return _pcall(body, name=name, grid=(nt,),
                  in_specs=[_rows(tm, F), _next_halo(tm, hf, F, T), _rows(tm, F, col=1), _full(wg.shape), _full(wdw.shape),
                            _rows(tm, D), _full((1, D)), _rows(tm, D)],
                  out_specs=[_rows(tm, F), _rows(tm, D), _rows(tm, D), _acc_spec(1, D)],
                  out_shape=[S_((T, 2 * F), BF16), S_((T, D), F32), S_((T, D), BF16), S_((1, D), F32)],
                  scratch=[pltpu.VMEM((tm + hf, F), F32)],
                  aliases={2: 0}, args=(dac, dac, dup, wg, wdw, xin, gn, dx), comm=comm)


def _mm_tn(xt, dy, bk, bn, shard_major, name, comm=None):
    nt, K, tt = xt.shape
    T, N = dy.shape
    assert nt * tt == T

    def body(x_ref, dy_ref, o_ref):
        x = jnp.concatenate([x_ref[t] for t in range(nt)], axis=1)
        o_ref[...] = _dot(x, dy_ref[...]).astype(o_ref.dtype)

    if shard_major:
        out_spec = pl.BlockSpec((None, bk, bn), lambda i, j: (j, i, 0))
        out_shape = S_((N // bn, K, bn), BF16)
    else:
        out_spec = pl.BlockSpec((bk, bn), lambda i, j: (i, j))
        out_shape = S_((K, N), BF16)
    return _pcall(body, name=name, grid=(K // bk, N // bn),
                  in_specs=[pl.BlockSpec((nt, bk, tt), lambda i, j: (0, i, 0)), pl.BlockSpec((T, bn), lambda i, j: (0, j))],
                  out_specs=out_spec, out_shape=out_shape, args=(xt, dy), comm=comm)


def _pool_bwd(dx3, plb, x2, pw, pb, ps, gn, tm, tps, comm=None):
    T, D = x2.shape
    ng, cg, _ = pw.shape
    hp = HALO_POOL
    nt = T // tm

    def body(do_ref, doh_ref, pl_ref, x_ref, pw_ref, pb_ref, ps_ref, gn_ref, dxo_ref, dxb_ref, dpw_ref, sm_ref, qext, buf_a,
             buf_b, dh_s):
        i = pl.program_id(0)
        do = do_ref[...]
        dm = do * ps_ref[...]
        dmh = jnp.where(i % tps == tps - 1, 0.0, doh_ref[...]) * ps_ref[...]
        pv = pl_ref[...]
        mixed = jnp.concatenate([_dot(pv[:, g * cg:(g + 1) * cg], pw_ref[g]) for g in range(ng)], axis=1)
        dscale = _colsum(do * (mixed + pb_ref[...]))
        dbias = _colsum(dm)
        dmb = dm.astype(BF16)
        dmhb = dmh.astype(BF16)
        for g, w in enumerate(POOL_WINDOWS):
            cs = slice(g * cg, (g + 1) * cg)
            _accumulate(dpw_ref.at[g], _dot_tn(pv[:, cs], dmb[:, cs]), i == 0)
            dpo = _dot_nt(dmb[:, cs], pw_ref[g])
            dh_s[:, cs] = dpo
            qext[0:tm, cs] = dpo / _pool_counts(i, tps, tm, w)
            qext[tm:tm + hp, cs] = _dot_nt(dmhb[:, cs], pw_ref[g]) * (1.0 / w)
        sums = _window_sums(qext, buf_a, buf_b, cg, False)
        for g in range(ng):
            cs = slice(g * cg, (g + 1) * cg)
            dh_s[:, cs] = sums[g][0:tm, cs] - dh_s[:, cs]

        _, xh, r = _rms(x_ref[...], gn_ref[...])
        dxn, dg = _rms_bwd(dh_s[...], xh, r, gn_ref[...])
        dxo = do + dxn
        dxo_ref[...] = dxo
        dxb_ref[...] = dxo.astype(BF16)
        sm = jnp.concatenate([dscale, dbias, dg, jnp.zeros((5, D), F32)], axis=0)
        _accumulate(sm_ref, sm, i == 0)

    return _pcall(body, name="pool_bwd", grid=(nt,),
                  in_specs=[_rows(tm, D), _next_halo(tm, hp, D, T), _rows(tm, D), _rows(tm, D), _full(pw.shape), _full((1, D)),
                            _full((1, D)), _full((1, D))],
                  out_specs=[_rows(tm, D), _rows(tm, D), pl.BlockSpec((ng, cg, cg), lambda i: (0, 0, 0)), _acc_spec(8, D)],
                  out_shape=[S_((T, D), F32), S_((T, D), BF16), S_((ng, cg, cg), F32), S_((8, D), F32)],
                  scratch=[pltpu.VMEM((tm + hp, D), F32)] * 3 + [pltpu.VMEM((tm, D), F32)],
                  args=(dx3, dx3, plb, x2, pw, pb, ps, gn), comm=comm)


def _conv_bwd1(dx1, c, w2, lng, lnb, tm, comm=None):
    T, D = dx1.shape

    def body(do_ref, c_ref, w2_ref, lng_ref, lnb_ref, dc_ref, sm_ref):
        i = pl.program_id(0)
        do = do_ref[...]
        ds = _dot_nt(do.astype(BF16), w2_ref[...])
        cv = c_ref[...]
        xc = cv - jnp.mean(cv, axis=-1, keepdims=True)
        rstd = lax.rsqrt(jnp.mean(xc * xc, axis=-1, keepdims=True) + LN_EPS)
        nh = xc * rstd
        n = nh * lng_ref[...] + lnb_ref[...]
        sg = _sig(n)
        dn = ds * (sg * (1.0 + n * (1.0 - sg)))
        dnh = dn * lng_ref[...]
        dc = rstd * (dnh - jnp.mean(dnh, axis=-1, keepdims=True) - nh * jnp.mean(dnh * nh, axis=-1, keepdims=True))
        dc_ref[...] = dc
        sm = jnp.concatenate([_colsum(dn * nh), _colsum(dn), _colsum(do), _colsum(dc), jnp.zeros((4, D), F32)], axis=0)
        _accumulate(sm_ref, sm, i == 0)

    return _pcall(body, name="conv_bwd1", grid=(T // tm,),
                  in_specs=[_rows(tm, D), _rows(tm, D), _full((D, D)), _full((1, D)), _full((1, D))],
                  out_specs=[_rows(tm, D), _acc_spec(8, D)],
                  out_shape=[S_((T, D), F32), S_((8, D), F32)], args=(dx1, c, w2, lng, lnb), comm=comm)


def _conv_bwd2(dc, a, x, w1g, wdw, gn, dx1, tm, tps, taps, comm=None):
    T, D = x.shape
    ns, _, nc = w1g.shape
    hc = HALO_CONV
    nt = T // tm
    tp = wdw.shape[0]

    def body(dc_ref, dch_ref, a_ref, ah_ref, x_ref, w_ref, wdw_ref, gn_ref, dx1_ref,
             gx_ref, dab_ref, dw_ref, db1_ref, dgn_ref, uext, dext, shifted, du_s, wacc):
        i = pl.program_id(0)
        ah = ah_ref[...]
        uext[0:hc, :] = jnp.where(i % tps == 0, 0.0, ah[:, :D] * _sig(ah[:, D:]))
        av = a_ref[...]
        sg2 = _sig(av[:, D:])
        uext[hc:hc + tm, :] = av[:, :D] * sg2
        dext[0:tm, :] = dc_ref[...]
        dext[tm:tm + hc, :] = jnp.where(i % tps == tps - 1, 0.0, dch_ref[...])

        @pl.when(i == 0)
        def _():
            wacc[...] = jnp.zeros_like(wacc)

        def wgrad(r0, src, ks, first):
            d = dc_ref[pl.ds(r0, CHUNK), :]
            for k, o in ks:
                wacc[8 * k:8 * k + 8, :] += _fold8(d * src[pl.ds(pl.multiple_of(r0 + o, 8), CHUNK), :])
        _by_sublane_shift(uext, shifted, tm, [hc - (taps - 1) + k for k in range(taps)], wgrad)

        def convt(r0, src, ks, first):
            acc = jnp.zeros((CHUNK, D), F32) if first else du_s[pl.ds(r0, CHUNK), :]
            for k, o in ks:
                acc = acc + wdw_ref[k:k + 1, :] * src[pl.ds(pl.multiple_of(r0 + o, 8), CHUNK), :]
            du_s[pl.ds(r0, CHUNK), :] = acc
        _by_sublane_shift(dext, shifted, tm, [taps - 1 - k for k in range(taps)], convt)

        du = du_s[...]
        da1 = du * sg2
        da2 = du * av[:, :D] * (sg2 * (1.0 - sg2))
        dab_ref[:, 0:D] = da1.astype(BF16)
        dab_ref[:, D:2 * D] = da2.astype(BF16)
        _accumulate(db1_ref, jnp.concatenate([_colsum(da1), _colsum(da2)], axis=0), i == 0)
        dh = jnp.zeros((tm, D), F32)
        for s in range(ns):
            dh = dh + _dot_nt(dab_ref[:, s * nc:(s + 1) * nc], w_ref[s])
        _, xh, r = _rms(x_ref[...], gn_ref[...])
        dxn, dg = _rms_bwd(dh, xh, r, gn_ref[...])
        gx_ref[...] = dx1_ref[...] + dxn
        _accumulate(dgn_ref, dg, i == 0)

        @pl.when(i == nt - 1)
        def _():
            dw_ref[...] = jnp.zeros_like(dw_ref)
            for k in range(taps):
                dw_ref[k:k + 1, :] = _colsum(wacc[8 * k:8 * k + 8, :])

    return _pcall(body, name="conv_bwd2", grid=(nt,),
                  in_specs=[_rows(tm, D), _next_halo(tm, hc, D, T), _rows(tm, 2 * D), _prev_halo(tm, hc, 2 * D), _rows(tm, D),
                            _full(w1g.shape), _full(wdw.shape), _full((1, D)), _rows(tm, D)],
                  out_specs=[_rows(tm, D), _rows(tm, 2 * D), _acc_spec(tp, D), _acc_spec(2, D), _acc_spec(1, D)],
                  out_shape=[S_((T, D), F32), S_((T, 2 * D), BF16), S_((tp, D), F32), S_((2, D), F32), S_((1, D), F32)],
                  scratch=[pltpu.VMEM((tm + hc, D), F32), pltpu.VMEM((tm + hc, D), F32), pltpu.VMEM((tm + hc - 8, D), F32),
                           pltpu.VMEM((tm, D), F32), pltpu.VMEM((8 * taps, D), F32)],
                  args=(dc, dc, a, a, x, w1g, wdw, gn, dx1), comm=comm)


def _place():
    x, y, c = lax.axis_index("x"), lax.axis_index("y"), lax.axis_index("c")
    chips = [(1 - x, y), (x, 1 - y), (1 - x, 1 - y)]
    return x, y, c, chips


class _GatherComm:
    def __init__(self, slots):
        n = len(slots)
        self.inputs = list(slots)
        self.out_shape = [S_(s.shape, s.dtype) for s in slots]
        self.aliases = {i: i for i in range(n)}
        self.sems = [pltpu.SemaphoreType.DMA((n, 3))] * 4
        self.halves = [s.shape[1] // 2 for s in slots]
        self.paired = [s.shape[0] == N_CHIPS // 2 for s in slots]
        self.widths = [s.shape[2] // 2 if p else s.shape[2] for s, p in zip(slots, self.paired)]

    def _copies(self, outs, sems):
        s_ici, r_ici, s_d2d, r_d2d = sems
        x, y, c, chips = _place()
        peers = [(j, chip, 2 * chip[0] + chip[1]) for j, chip in enumerate(chips)]

        def block(i, chip, h):
            rows = pl.ds(pl.multiple_of(h * self.halves[i], 8), self.halves[i])
            if self.paired[i]:
                return outs[i].at[chip // 2, rows, pl.ds(pl.multiple_of((chip % 2) * self.widths[i], LANES), self.widths[i])]
            return outs[i].at[chip, rows]

        def ici(i, j, chip, to):
            blk = block(i, chip, c)
            return pltpu.make_async_remote_copy(src_ref=blk, dst_ref=blk, send_sem=s_ici.at[i, j], recv_sem=r_ici.at[i, j],
                                                device_id=to, device_id_type=MESH)

        def d2d(i, j, chip, h):
            blk = block(i, chip, h)
            return pltpu.make_async_remote_copy(src_ref=blk, dst_ref=blk, send_sem=s_d2d.at[i, j], recv_sem=r_d2d.at[i, j],
                                                device_id=(x, y, 1 - c), device_id_type=MESH)
        return (x, y, c, 2 * x + y), peers, ici, d2d

    def before(self, cin, outs, sems, step, total):
        (x, y, c, q), peers, ici, d2d = self._copies(outs, sems)

        @pl.when(step == 0)
        def _():
            for i in range(len(outs)):
                for j, chip, _ in peers:
                    ici(i, j, q, (*chip, c)).start()

    def after(self, cin, outs, sems, step, total):
        (x, y, c, q), peers, ici, d2d = self._copies(outs, sems)

        @pl.when(step == max(total - 2, 0))
        def _():
            for i in range(len(outs)):
                for j, chip, qj in peers:
                    ici(i, j, qj, (x, y, c)).wait_recv()
                    d2d(i, j, qj, c).start()

        @pl.when(step == total - 1)
        def _():
            for i in range(len(outs)):
                for j, chip, qj in peers:
                    d2d(i, j, qj, 1 - c).wait_recv()
            for i in range(len(outs)):
                for j, chip, qj in peers:
                    ici(i, j, q, (*chip, c)).wait_send()
                    d2d(i, j, qj, c).wait_send()


class _ExchangeComm:
    def __init__(self, sums):
        n = len(sums)
        self.inputs = list(sums)
        self.out_shape = [S_((3,) + s.shape[1:], s.dtype) for s in sums]
        self.aliases = {}
        self.sems = [pltpu.SemaphoreType.DMA((n, 3))] * 2

    def _copies(self, srcs, outs, sems):
        s_sem, r_sem = sems
        x, y, c, chips = _place()
        return [pltpu.make_async_remote_copy(src_ref=srcs[i].at[2 * chip[0] + chip[1]], dst_ref=outs[i].at[j],
                                             send_sem=s_sem.at[i, j], recv_sem=r_sem.at[i, j],
                                             device_id=(*chip, c), device_id_type=MESH)
                for i in range(len(srcs)) for j, chip in enumerate(chips)]

    def before(self, srcs, outs, sems, step, total):
        @pl.when(step == 0)
        def _():
            for cp in self._copies(srcs, outs, sems):
                cp.start()

    def after(self, srcs, outs, sems, step, total):
        @pl.when(step == total - 1)
        def _():
            for cp in self._copies(srcs, outs, sems):
                cp.wait()


def _own_slots(specs):
    n = len(specs)
    shapes, blocks, count = [], [], []
    for a, (w, layer, dtype, pairs) in enumerate(specs):
        _, r, c = w.shape
        br = _row_block(r, c, 2 * BLOCK_BYTES)
        shapes.append(S_((N_CHIPS // 2, r, 2 * c) if pairs else (N_CHIPS, r, c), dtype))
        blocks += [(a, k, k * br, br) for k in range(r // br)]
        count.append(r // br)
    gather = _GatherComm(shapes[:1])
    n_sems = len(gather.sems)

    def body(*refs):
        srcs, outs = refs[:n], refs[n:2 * n]
        inb, outb = refs[2 * n:3 * n], refs[3 * n:4 * n]
        lsem, ssem = refs[4 * n], refs[4 * n + 1]
        gsems = refs[4 * n + 2:4 * n + 2 + n_sems]
        x, y, c, _ = _place()
        q = 2 * x + y

        def load(a, k, r0, br):
            return pltpu.make_async_copy(srcs[a].at[specs[a][1], pl.ds(r0, br)], inb[a].at[k % 2], lsem.at[a, k % 2])

        def store(a, k, r0, br):
            if specs[a][3]:
                wc = specs[a][0].shape[2]
                dst = outs[a].at[q // 2, pl.ds(r0, br), pl.ds(pl.multiple_of((q % 2) * wc, LANES), wc)]
            else:
                dst = outs[a].at[q, pl.ds(r0, br)]
            return pltpu.make_async_copy(outb[a].at[k % 2], dst, ssem.at[a, k % 2])

        load(*blocks[0]).start()
        for b, (a, k, r0, br) in enumerate(blocks):
            if b + 1 < len(blocks):
                load(*blocks[b + 1]).start()
            load(a, k, r0, br).wait()
            if k >= 2:
                store(a, k - 2, r0 - 2 * br, br).wait()
            outb[a][k % 2] = inb[a][k % 2].astype(specs[a][2])
            store(a, k, r0, br).start()
            if a == 0 and k == count[0] - 1:
                for kk in range(max(k - 1, 0), k + 1):
                    store(0, kk, kk * br, br).wait()
                gather.before(None, outs[:1], gsems, 0, 1)
        for a in range(1, n):
            br = blocks[sum(count[:a])][3]
            for kk in range(max(count[a] - 2, 0), count[a]):
                store(a, kk, kk * br, br).wait()
        gather.after(None, outs[:1], gsems, 0, 1)

    any_spec = pl.BlockSpec(memory_space=pl.ANY)
    scratch = ([pltpu.VMEM((2, blocks[sum(count[:a])][3], specs[a][0].shape[2]), F32) for a in range(n)]
               + [pltpu.VMEM((2, blocks[sum(count[:a])][3], specs[a][0].shape[2]), specs[a][2]) for a in range(n)]
               + [pltpu.SemaphoreType.DMA((n, 2)), pltpu.SemaphoreType.DMA((n, 2))] + list(gather.sems))
    outs = pl.pallas_call(body, name="own_slots", in_specs=[any_spec] * n, out_specs=[any_spec] * n, out_shape=shapes,
                          scratch_shapes=scratch,
                          compiler_params=pltpu.CompilerParams(vmem_limit_bytes=VMEM_LIMIT_BYTES))(*[s[0] for s in specs])
    return list(outs)


def _pair_exchange(parts, name):
    n = len(parts)

    def body(*refs):
        srcs, outs = refs[:n], refs[n:2 * n]
        s_sem, r_sem = refs[2 * n:]
        x, y, c, _ = _place()
        cps = []
        for i in range(n):
            for s in range(N_CHIPS):
                cp = pltpu.make_async_remote_copy(src_ref=srcs[i].at[s, 1 - c], dst_ref=outs[i].at[s], send_sem=s_sem.at[i, s],
                                                  recv_sem=r_sem.at[i, s], device_id=(x, y, 1 - c), device_id_type=MESH)
                cp.start()
                cps.append(cp)
        for cp in cps:
            cp.wait()

    any_spec = pl.BlockSpec(memory_space=pl.ANY)
    outs = pl.pallas_call(
        body, name=name, in_specs=[any_spec] * n, out_specs=[any_spec] * n,
        out_shape=[S_((N_CHIPS,) + p.shape[2:], p.dtype) for p in parts],
        scratch_shapes=[pltpu.SemaphoreType.DMA((n, N_CHIPS))] * 2,
    )(*parts)
    return list(outs)


class _PairComm:
    def __init__(self, parts):
        n = len(parts)
        self.inputs = list(parts)
        self.out_shape = [S_((N_CHIPS,) + p.shape[2:], p.dtype) for p in parts]
        self.aliases = {}
        self.sems = [pltpu.SemaphoreType.DMA((n, N_CHIPS))] * 2

    def _copies(self, srcs, outs, sems):
        s_sem, r_sem = sems
        x, y, c, _ = _place()
        return [pltpu.make_async_remote_copy(src_ref=srcs[i].at[s, 1 - c], dst_ref=outs[i].at[s], send_sem=s_sem.at[i, s],
                                             recv_sem=r_sem.at[i, s], device_id=(x, y, 1 - c), device_id_type=MESH)
                for i in range(len(srcs)) for s in range(N_CHIPS)]

    def before(self, srcs, outs, sems, step, total):
        @pl.when(step == 0)
        def _():
            for cp in self._copies(srcs, outs, sems):
                cp.start()

    def after(self, srcs, outs, sems, step, total):
        @pl.when(step == total - 1)
        def _():
            for cp in self._copies(srcs, outs, sems):
                cp.wait()


class _ShareComm:
    def __init__(self, bufs):
        n = len(bufs)
        self.inputs = list(bufs)
        self.out_shape = [S_(b.shape, b.dtype) for b in bufs]
        self.aliases = {i: i for i in range(n)}
        self.sems = [pltpu.SemaphoreType.DMA((n, max(b.shape[0] for b in bufs)))] * 2
        self.layers = [b.shape[0] for b in bufs]

    def _copies(self, outs, sems):
        s_sem, r_sem = sems
        x, y, c, _ = _place()
        cps = []
        for i, nl in enumerate(self.layers):
            for l in range(nl):
                blk = outs[i].at[l, c]
                cps.append(pltpu.make_async_remote_copy(src_ref=blk, dst_ref=blk, send_sem=s_sem.at[i, l],
                                                        recv_sem=r_sem.at[i, l], device_id=(x, y, 1 - c),
                                                        device_id_type=MESH))
        return cps

    def before(self, cin, outs, sems, step, total):
        @pl.when(step == 0)
        def _():
            for cp in self._copies(outs, sems):
                cp.start()

    def after(self, cin, outs, sems, step, total):
        @pl.when(step == total - 1)
        def _():
            for cp in self._copies(outs, sems):
                cp.wait()


class _Both:
    def __init__(self, a, b):
        self.parts = (a, b)
        self.inputs = a.inputs + b.inputs
        self.out_shape = a.out_shape + b.out_shape
        self.aliases = dict(a.aliases)
        self.aliases.update({len(a.inputs) + i: len(a.out_shape) + o for i, o in b.aliases.items()})
        self.sems = a.sems + b.sems

    def _each(self, phase, cin, cout, sems, step, total):
        i = o = s = 0
        for p in self.parts:
            ni, no, ns = len(p.inputs), len(p.out_shape), len(p.sems)
            getattr(p, phase)(cin[i:i + ni], cout[o:o + no], sems[s:s + ns], step, total)
            i, o, s = i + ni, o + no, s + ns

    def before(self, cin, cout, sems, step, total):
        self._each("before", cin, cout, sems, step, total)

    def after(self, cin, cout, sems, step, total):
        self._each("after", cin, cout, sems, step, total)


def _small_allreduce(groups, comm):
    n = len(groups)
    nci, nco = len(comm.inputs), len(comm.out_shape)
    arrays = []
    for grp in groups:
        for p in grp:
            if not isinstance(p, int) and not any(p[0] is a for a in arrays):
                arrays.append(p[0])
    shapes = []
    for grp in groups:
        rows = sum(p if isinstance(p, int) else p[2] - p[1] for p in grp)
        width = next(p[0].shape[1] for p in grp if not isinstance(p, int))
        assert rows % 8 == 0
        shapes.append((rows, width))
    na = len(arrays)

    def body(*refs):
        a_refs, cin = refs[:na], refs[na:na + nci]
        o_refs, cout = refs[na + nci:na + nci + n], refs[na + nci + n:na + nci + n + nco]
        rest = refs[na + nci + n + nco:]
        v_refs, sibs, css, gots = rest[:n], rest[n:2 * n], rest[2 * n:3 * n], rest[3 * n:4 * n]
        s_sem, r_sem = rest[4 * n:4 * n + 2]
        csems = rest[4 * n + 2:]
        comm.before(cin, cout, csems, 0, 1)
        for i, grp in enumerate(groups):
            o = 0
            for p in grp:
                if isinstance(p, int):
                    v_refs[i][o:o + p, :] = jnp.zeros((p, shapes[i][1]), F32)
                    o += p
                else:
                    k = next(t for t, a in enumerate(arrays) if a is p[0])
                    v_refs[i][o:o + p[2] - p[1], :] = a_refs[k][p[1]:p[2], :]
                    o += p[2] - p[1]
        x, y, c, chips = _place()
        q = 2 * x + y
        to_sib = [pltpu.make_async_remote_copy(src_ref=v_refs[i], dst_ref=sibs[i], send_sem=s_sem.at[i, 3], recv_sem=r_sem.at[i, 3],
                                               device_id=(x, y, 1 - c), device_id_type=MESH) for i in range(n)]
        for cp in to_sib:
            cp.start()
        cps = []
        for i in range(n):
            to_sib[i].wait()
            mine, other = v_refs[i][...], sibs[i][...]
            css[i][...] = jnp.where(c == 0, mine, other) + jnp.where(c == 0, other, mine)
            for j, chip in enumerate(chips):
                cp = pltpu.make_async_remote_copy(src_ref=css[i], dst_ref=gots[i].at[j], send_sem=s_sem.at[i, j],
                                                  recv_sem=r_sem.at[i, j], device_id=(*chip, c), device_id_type=MESH)
                cp.start()
                cps.append(cp)
        for cp in cps:
            cp.wait()
        flips = [2, 1, 3]
        for i in range(n):
            total = None
            for k in range(N_CHIPS):
                d = q ^ k
                term = jnp.where(d == 0, css[i][...], 0.0)
                for j in range(3):
                    term = jnp.where(d == flips[j], gots[i][j], term)
                total = term if total is None else total + term
            o_refs[i][...] = total
        comm.after(cin, cout, csems, 0, 1)

    vm, any_spec = pl.BlockSpec(memory_space=pltpu.VMEM), pl.BlockSpec(memory_space=pl.ANY)
    outs = pl.pallas_call(
        body, name="small_allreduce", in_specs=[vm] * na + [any_spec] * nci, out_specs=[vm] * n + [any_spec] * nco,
        out_shape=[S_(s, F32) for s in shapes] + list(comm.out_shape),
        input_output_aliases={na + i: n + o for i, o in comm.aliases.items()},
        scratch_shapes=[pltpu.VMEM(s, F32) for s in shapes] * 3 + [pltpu.VMEM((3,) + s, F32) for s in shapes]
        + [pltpu.SemaphoreType.DMA((n, 4)), pltpu.SemaphoreType.DMA((n, 4))] + list(comm.sems),
    )(*arrays, *comm.inputs)
    return list(outs[:n]), list(outs[n:])


def _row_block(r, c, target_bytes=BLOCK_BYTES):
    br = r
    while br % (2 * BF16_ROWS) == 0 and br * c * 4 > target_bytes:
        br //= 2
    return br


def _chip_sum(part, sib, ids, name):
    _, _, r, c = part.shape
    br = _row_block(r, c, 4 * BLOCK_BYTES)

    def body(ids_ref, p_ref, s_ref, o_ref):
        o_ref[...] = (p_ref[...].astype(F32) + s_ref[...].astype(F32)).astype(BF16)

    return _pcall(body, name=name, grid=(N_CHIPS, r // br), prefetch=1,
                  in_specs=[pl.BlockSpec((None, None, br, c), lambda s, i, ids: (s, ids[1], i, 0)),
                            pl.BlockSpec((None, br, c), lambda s, i, ids: (s, i, 0))],
                  out_specs=pl.BlockSpec((None, br, c), lambda s, i, ids: (s, i, 0)),
                  out_shape=S_((N_CHIPS, r, c), BF16), args=(ids, part, sib))


def _final_sum(part, sib, got, ids, name):
    _, r, c = got.shape
    br = _row_block(r, c, 2 * BLOCK_BYTES)

    def body(ids_ref, p_ref, s_ref, got_ref, o_ref):
        v = p_ref[...].astype(F32) + s_ref[...].astype(F32)
        for j in range(3):
            v = v + got_ref[j].astype(F32)
        o_ref[...] = v

    return _pcall(body, name=name, grid=(r // br,), prefetch=1,
                  in_specs=[pl.BlockSpec((None, None, br, c), lambda i, ids: (ids[0], ids[1], i, 0)),
                            pl.BlockSpec((None, br, c), lambda i, ids: (ids[0], i, 0)),
                            pl.BlockSpec((3, br, c), lambda i, ids: (0, i, 0))],
                  out_specs=pl.BlockSpec((None, None, br, c), lambda i, ids: (0, ids[1], i, 0)),
                  out_shape=S_((1, 2, r, c), F32), args=(ids, part, sib, got))


def _final_sums(items, n_bufs, comm):
    n = len(items)
    nci, nco, ncs = len(comm.inputs), len(comm.out_shape), len(comm.sems)
    shapes = [None] * n_bufs
    blocks, geo = [], []
    for a, (part, sib, got, b, layer, nl) in enumerate(items):
        _, r, c = got.shape
        br = _row_block(r, c)
        shapes[b] = S_((nl, 2, r, c), F32)
        geo.append((br, c))
        blocks += [(a, k, k * br, br) for k in range(r // br)]
    n_src = 5

    def body(*refs):
        srcs = [refs[3 * a:3 * a + 3] for a in range(n)]
        cin = refs[3 * n:3 * n + nci]
        outs = refs[3 * n + nci:3 * n + nci + n_bufs]
        cout = refs[3 * n + nci + n_bufs:3 * n + nci + n_bufs + nco]
        rest = refs[3 * n + nci + n_bufs + nco:]
        inb, outb = rest[:n], rest[n:2 * n]
        lsem, ssem = rest[2 * n], rest[2 * n + 1]
        csems = rest[2 * n + 2:]
        x, y, c, _ = _place()
        q = 2 * x + y
        comm.before(cin, cout, csems, 0, 1)

        def loads(a, k, r0, br):
            part, sib, got = srcs[a]
            rows = pl.ds(r0, br)
            pieces = [part.at[q, c, rows], sib.at[q, rows]] + [got.at[j, rows] for j in range(3)]
            return [pltpu.make_async_copy(p, inb[a].at[k % 2, j], lsem.at[a, k % 2, j]) for j, p in enumerate(pieces)]

        def store(a, k, r0, br):
            _, _, _, b, layer, _ = items[a]
            return pltpu.make_async_copy(outb[a].at[k % 2], outs[b].at[layer, c, pl.ds(r0, br)], ssem.at[a, k % 2])

        for cp in loads(*blocks[0]):
            cp.start()
        for i, (a, k, r0, br) in enumerate(blocks):
            if i + 1 < len(blocks):
                for cp in loads(*blocks[i + 1]):
                    cp.start()
            for cp in loads(a, k, r0, br):
                cp.wait()
            if k >= 2:
                store(a, k - 2, r0 - 2 * br, br).wait()
            v = inb[a][k % 2, 0].astype(F32) + inb[a][k % 2, 1].astype(F32)
            for j in range(2, n_src):
                v = v + inb[a][k % 2, j].astype(F32)
            outb[a][k % 2] = v
            store(a, k, r0, br).start()
        for a in range(n):
            cnt = sum(1 for blk in blocks if blk[0] == a)
            for kk in range(max(cnt - 2, 0), cnt):
                store(a, kk, kk * geo[a][0], geo[a][0]).wait()
        comm.after(cin, cout, csems, 0, 1)

    any_spec = pl.BlockSpec(memory_space=pl.ANY)
    scratch = ([pltpu.VMEM((2, n_src) + g, BF16) for g in geo] + [pltpu.VMEM((2,) + g, F32) for g in geo]
               + [pltpu.SemaphoreType.DMA((n, 2, n_src)), pltpu.SemaphoreType.DMA((n, 2))] + list(comm.sems))
    args = [t for it in items for t in it[:3]] + list(comm.inputs)
    outs = pl.pallas_call(body, name="final_sums", in_specs=[any_spec] * len(args), out_specs=[any_spec] * (n_bufs + nco),
                          out_shape=shapes + list(comm.out_shape), scratch_shapes=scratch,
                          compiler_params=pltpu.CompilerParams(vmem_limit_bytes=VMEM_LIMIT_BYTES))(*args)
    return list(outs[:n_bufs]), list(outs[n_bufs:])


def _adam_math(w, g, m, v):
    c1 = 1.0 / (1.0 - ADAM_B1 ** ADAM_STEP)
    c2 = 1.0 / (1.0 - ADAM_B2 ** ADAM_STEP)
    mn = ADAM_B1 * m + (1.0 - ADAM_B1) * g
    vn = ADAM_B2 * v + (1.0 - ADAM_B2) * (g * g)
    return -ADAM_LR * ((mn * c1) / (jnp.sqrt(vn * c2) + ADAM_EPS) + ADAM_WD * w), mn, vn


def _adamw(w, g, m, v, name):
    nl, r, c = w.shape
    br = _row_block(r, c)

    def body(w_ref, g_ref, m_ref, v_ref, go_ref, d_ref, mo_ref, vo_ref):
        gv = g_ref[...]
        go_ref[...] = gv
        d_ref[...], mo_ref[...], vo_ref[...] = _adam_math(w_ref[...], gv, m_ref[...], v_ref[...])

    blk = pl.BlockSpec((None, br, c), lambda l, i: (l, i, 0))
    return _pcall(body, name=name, grid=(nl, r // br), in_specs=[blk] * 4, out_specs=[blk] * 4,
                  out_shape=[S_((nl, r, c), F32)] * 4, args=(w, g, m, v))


def _adamw_small(rd, rf, sharded_g, ws, ms, vs, rows_d, rows_f, loss_row):
    n = len(ws)
    ns = len(sharded_g)
    rep = [k for k in range(n) if rows_d[k] is not None or rows_f[k] is not None]

    def body(*refs):
        rd_ref, rf_ref = refs[0], refs[1]
        sg = refs[2:2 + ns]
        w_refs, m_refs, v_refs = (refs[2 + ns + t * n:2 + ns + (t + 1) * n] for t in range(3))
        loss_ref, outs = refs[2 + ns + 3 * n], refs[3 + ns + 3 * n:]
        g_out, d_out, m_out, v_out = outs[:len(rep)], outs[len(rep):len(rep) + n], outs[len(rep) + n:len(rep) + 2 * n], \
            outs[len(rep) + 2 * n:]
        loss_ref[...] = jnp.sum(rd_ref[loss_row:loss_row + 1, :], axis=1, keepdims=True)
        si = 0
        for k in range(n):
            shape = w_refs[k].shape
            if rows_d[k] is not None or rows_f[k] is not None:
                src, rws = (rd_ref, rows_d[k]) if rows_d[k] is not None else (rf_ref, rows_f[k])
                axis = 0 if shape[0] == len(rws) else 1
                g = src[rws[0]:rws[0] + 1, :]
                if len(rws) > 1:
                    g = jnp.concatenate([src[r:r + 1, :] for r in rws], axis=axis)
                g_out[rep.index(k)][...] = g
            else:
                g = sg[si][...]
                si += 1
            d_out[k][...], m_out[k][...], v_out[k][...] = _adam_math(w_refs[k][...], g, m_refs[k][...], v_refs[k][...])

    vm = pl.BlockSpec(memory_space=pltpu.VMEM)
    shapes = [S_(w.shape, F32) for w in ws]
    outs = pl.pallas_call(
        body, name="adamw_small", in_specs=[vm] * (2 + ns + 3 * n), out_specs=[vm] * (1 + len(rep) + 3 * n),
        out_shape=[S_((1, 1), F32)] + [shapes[k] for k in rep] + shapes * 3,
    )(rd, rf, *sharded_g, *ws, *ms, *vs)
    g_rep = dict(zip(rep, outs[1:1 + len(rep)]))
    o = outs[1 + len(rep):]
    return outs[0], g_rep, o[:n], o[n:2 * n], o[2 * n:]


def _pack(pieces, rows):
    flat = jnp.concatenate([p.reshape(-1).astype(F32) for p in pieces])
    return jnp.pad(flat, (0, rows * LANES - flat.shape[0])).reshape(rows, LANES)


def _unpack(packed, shapes):
    flat = packed.reshape(-1)
    out, o = [], 0
    for s in shapes:
        n = 1
        for d in s:
            n *= d
        out.append(flat[o:o + n].reshape(s))
        o += n
    return out


def _rows_for(pieces_or_shapes):
    n = 0
    for p in pieces_or_shapes:
        k = 1
        for d in (p if isinstance(p, tuple) else p.shape):
            k *= d
        n += k
    return -(-n // (SUBLANES * LANES)) * SUBLANES


def kernel(x, norm_mix, norm_ffn, conv_w_pw1, conv_b_pw1, conv_w_dw, conv_b_dw, conv_ln_g, conv_ln_b, conv_w_pw2, conv_b_pw2, pool_w, pool_b, pool_scale, ffn_w_up, ffn_w_dw, ffn_b_dw, ffn_w_down, final_norm, loss_target, m_norm_mix, m_norm_ffn, m_conv_w_pw1, m_conv_b_pw1, m_conv_w_dw, m_conv_b_dw, m_conv_ln_g, m_conv_ln_b, m_conv_w_pw2, m_conv_b_pw2, m_pool_w, m_pool_b, m_pool_scale, m_ffn_w_up, m_ffn_w_dw, m_ffn_b_dw, m_ffn_w_down, m_final_norm, v_norm_mix, v_norm_ffn, v_conv_w_pw1, v_conv_b_pw1, v_conv_w_dw, v_conv_b_dw, v_conv_ln_g, v_conv_ln_b, v_conv_w_pw2, v_conv_b_pw2, v_pool_w, v_pool_b, v_pool_scale, v_ffn_w_up, v_ffn_w_dw, v_ffn_b_dw, v_ffn_w_down, v_final_norm):
    nb, seq, D = x.shape
    T = nb * seq
    F = ffn_w_down.shape[1] * N_CHIPS
    taps = conv_w_dw.shape[1]
    ffn_taps = ffn_w_dw.shape[1]
    ng = pool_w.shape[1]
    cg = pool_w.shape[3]
    dsh = D // N_CHIPS
    fsh = F // N_CHIPS
    assert taps - 1 <= HALO_CONV and ffn_taps - 1 <= HALO_FFN and max(POOL_WINDOWS) <= HALO_POOL
    tm = min(TILE_D, seq)
    tf = min(TILE_F, seq)
    assert seq % tm == 0 and seq % tf == 0 and tm % HALO_CONV == 0 and tf % CHUNK == 0
    tps, tpf = seq // tm, seq // tf

    xi, yi, ci = lax.axis_index("x"), lax.axis_index("y"), lax.axis_index("c")
    qi = 2 * xi + yi
    ids = jnp.stack([qi, ci]).astype(jnp.int32)
    x2d = x.reshape(T, D)
    tg2d = loss_target.reshape(T, D)

    small_sharded = [conv_w_dw[0], ffn_w_dw, pool_b, pool_scale]
    srows = 2 * _rows_for(small_sharded)
    w1g, s_pw2, s_small, s_pool, s_up0, s_up1, s_down0, s_down1 = _own_slots([
        (conv_w_pw1, 0, BF16, False), (conv_w_pw2, 0, BF16, False), (_pack(small_sharded, srows)[None], 0, F32, False),
        (pool_w.reshape(1, ng * (cg // N_CHIPS), cg), 0, BF16, False), (ffn_w_up, 0, BF16, True), (ffn_w_up, 1, BF16, True),
        (ffn_w_down, 0, BF16, False), (ffn_w_down, 1, BF16, False)])
    s_up, s_down = [s_up0, s_up1], [s_down0, s_down1]
    hb0t, a, w2g, smallg, pwg = _conv_in(x2d, norm_mix[0:1], w1g, conv_b_pw1, tm, comm=_GatherComm([s_pw2, s_small, s_pool]))
    w2f = w2g.reshape(D, D)
    sm_parts = [_unpack(smallg[s], [(taps, dsh), (2, ffn_taps, fsh), (1, dsh), (1, dsh)]) for s in range(N_CHIPS)]
    wdw_f = jnp.concatenate([p[0] for p in sm_parts], axis=1)
    wdw_f = jnp.pad(wdw_f, ((0, HALO_CONV - taps), (0, 0)))
    fdw_f = jnp.concatenate([p[1] for p in sm_parts], axis=2)
    fdw_f = jnp.pad(fdw_f, ((0, 0), (0, 8 - ffn_taps), (0, 0)))
    pb_f = jnp.concatenate([p[2] for p in sm_parts], axis=1)
    ps_f = jnp.concatenate([p[3] for p in sm_parts], axis=1)

    c, sbt, x1, hb1, hb1t, wup0, wd0 = _conv_mid(a, x2d, wdw_f, conv_b_dw, conv_ln_g, conv_ln_b, w2f, conv_b_pw2,
                                                 norm_ffn[0:1], tm, tps, taps, comm=_GatherComm([s_up[0], s_down[0]]))
    up0, gb0t, x2, h2, wup1, wd1 = _ffn_fwd(hb1, x1, wup0, fdw_f[0], ffn_b_dw[0:1], wd0.reshape(F, D), norm_mix[1:2], tf, tpf,
                                            ffn_taps, "ffn_fwd0", comm=_GatherComm([s_up[1], s_down[1]]))
    pwf = pwg.reshape(N_CHIPS, ng, cg // N_CHIPS, cg).transpose(1, 0, 2, 3).reshape(ng, cg, cg)
    plb, x3, hb3, hb3t = _pool_fwd(h2, x2, pwf, pb_f, ps_f, norm_ffn[1:2], tm, tps)
    wup = [wup0, wup1]
    wdn = [wd0.reshape(F, D), wd1.reshape(F, D)]
    up1, gb1t, dx4, dx4b, loss_part, d_final = _ffn_fwd(hb3, x3, wup[1], fdw_f[1], ffn_b_dw[1:2], wdn[1], final_norm.reshape(1, D),
                                                  tf, tpf, ffn_taps, "ffn_fwd1", target=tg2d)

    fs = {}

    def pair(group):
        return list(group), list(group.values()), _PairComm(list(group.values()))

    def chip_sums(nms, parts, from_sib):
        for nm, p, s in zip(nms, parts, from_sib):
            fs[nm] = (p, s)
        return nms, _ExchangeComm([_chip_sum(p, s, ids, "chip_sum_" + nm) for nm, p, s in zip(nms, parts, from_sib)])

    def arrived(nms, got):
        for nm, g in zip(nms, got):
            fs[nm] = fs[nm] + (g,)

    dac1, dup1, fw1 = _bwd_down(dx4, up1, fdw_f[1], ffn_b_dw[1:2], wdn[1], tf, tpf, ffn_taps, "bwd_down1")
    p_down1 = _mm_tn(gb1t, dx4b, F // 4, D, False, "dw_down1")
    nms, parts, px = pair({"down1": p_down1.reshape(N_CHIPS, 2, fsh // 2, D)})
    dup1, dx3, _, dnf1, *sib = _bwd_up(dac1, dup1, wup[1], fdw_f[1], x3, norm_ffn[1:2], dx4, tf, tpf, ffn_taps, "bwd_up1", comm=px)
    nms, xchg = chip_sums(nms, parts, sib)
    p_up1, *got = _mm_tn(hb3t, dup1, D // 2, 2 * fsh, True, "dw_up1", comm=xchg)
    arrived(nms, got)
    nms, parts, px = pair({"up1": p_up1.reshape(N_CHIPS, 2, D // 2, 2 * fsh)})
    dx2, dx2b, dpw, pool_sm, *sib = _pool_bwd(dx3, plb, x2, pwf, pb_f, ps_f, norm_mix[1:2], tm, tps, comm=px)
    nms, xchg = chip_sums(nms, parts, sib)
    dac0, dup0, fw0, *got = _bwd_down(dx2, up0, fdw_f[0], ffn_b_dw[0:1], wdn[0], tf, tpf, ffn_taps, "bwd_down0", comm=xchg)
    arrived(nms, got)
    p_down0 = _mm_tn(gb0t, dx2b, F // 4, D, False, "dw_down0")
    nms, parts, px = pair({"down0": p_down0.reshape(N_CHIPS, 2, fsh // 2, D)})
    dup0, dx1, dx1b, dnf0, *sib = _bwd_up(dac0, dup0, wup[0], fdw_f[0], x1, norm_ffn[0:1], dx2, tf, tpf, ffn_taps, "bwd_up0", comm=px)
    nms, xchg = chip_sums(nms, parts, sib)
    p_up0, *got = _mm_tn(hb1t, dup0, D // 2, 2 * fsh, True, "dw_up0", comm=xchg)
    arrived(nms, got)
    csh = cg // N_CHIPS
    p_pool = dpw.reshape(2, ng // 2, N_CHIPS, csh, cg).transpose(2, 0, 1, 3, 4).reshape(N_CHIPS, 2, (ng // 2) * csh, cg)
    nms, parts, px = pair({"up0": p_up0.reshape(N_CHIPS, 2, D // 2, 2 * fsh), "pool": p_pool.astype(BF16)})
    dc, conv_sm, *sib = _conv_bwd1(dx1, c, w2f, conv_ln_g, conv_ln_b, tm, comm=px)
    nms, xchg = chip_sums(nms, parts, sib)
    p_pw2 = _mm_tn(sbt, dx1b, D, D, False, "dw_pw2")
    nms2, parts2, px = pair({"pw2": p_pw2.reshape(N_CHIPS, 2, dsh // 2, D)})
    grad_x, dab, dwdw, db1, dnm0, *both = _conv_bwd2(dc, a, x2d, w1g, wdw_f, norm_mix[0:1], dx1, tm, tps, taps,
                                                     comm=_Both(xchg, px))
    arrived(nms, both[:len(nms)])
    nms, xchg = chip_sums(nms2, parts2, both[len(nms):])
    p_pw1, *got = _mm_tn(hb0t, dab, D, 2 * dsh, True, "dw_pw1", comm=xchg)
    arrived(nms, got)
    nms, parts, _ = pair({"pw1": p_pw1.reshape(N_CHIPS, 2, D // 2, 2 * dsh)})
    nms, xchg = chip_sums(nms, parts, _pair_exchange(parts, "grad_pair_exchange_pw1"))
    bufs, got = _final_sums([(*fs["pw2"], 0, 0, 1), (*fs["pool"], 1, 0, 1), (*fs["up0"], 2, 0, 2), (*fs["up1"], 2, 1, 2),
                             (*fs["down0"], 3, 0, 2), (*fs["down1"], 3, 1, 2)], 4, xchg)
    arrived(nms, got)
    bufs.append(_final_sum(*fs["pw1"], ids, "final_sum_pw1"))
    (rd, rf), shared = _small_allreduce([
        [(dnm0, 0, 1), (pool_sm, 2, 3), (dnf0, 0, 1), (dnf1, 0, 1), (db1, 0, 2), (conv_sm, 0, 4), (d_final, 0, 1),
         (pool_sm, 0, 2), (loss_part, 0, 1), 2, (dwdw, 0, dwdw.shape[0])],
        [(fw0, 0, 4), (fw1, 0, 4)]], _ShareComm(bufs))
    g_pw2, g_pool, g_up, g_down, g_pw1 = [b.reshape(b.shape[0], 2 * b.shape[2], b.shape[3]) for b in shared]
    g_wdw = lax.dynamic_slice_in_dim(rd[16:16 + taps], qi * dsh, dsh, axis=1)
    g_ps = lax.dynamic_slice_in_dim(rd[11:12], qi * dsh, dsh, axis=1)
    g_pb = lax.dynamic_slice_in_dim(rd[12:13], qi * dsh, dsh, axis=1)
    g_fdw = lax.dynamic_slice_in_dim(rf.reshape(2, 4, F)[:, :ffn_taps], qi * fsh, fsh, axis=2).reshape(2 * ffn_taps, fsh)

    def big(w, g, m, v, nm):
        return tuple(t.reshape(w.shape) for t in _adamw(w.reshape(g.shape), g, m.reshape(g.shape), v.reshape(g.shape),
                                                        "adamw_" + nm))

    o_pw1 = big(conv_w_pw1, g_pw1, m_conv_w_pw1, v_conv_w_pw1, "pw1")
    o_pw2 = big(conv_w_pw2, g_pw2, m_conv_w_pw2, v_conv_w_pw2, "pw2")
    o_pool = big(pool_w, g_pool, m_pool_w, v_pool_w, "pool")
    o_up = big(ffn_w_up, g_up, m_ffn_w_up, v_ffn_w_up, "up")
    o_down = big(ffn_w_down, g_down, m_ffn_w_down, v_ffn_w_down, "down")

    snames = ["norm_mix", "norm_ffn", "b_pw1", "w_dw", "b_dw", "ln_g", "ln_b", "b_pw2", "pool_b", "pool_scale", "ffn_w_dw",
              "ffn_b_dw", "final_norm"]
    sw = [norm_mix, norm_ffn, conv_b_pw1, conv_w_dw, conv_b_dw, conv_ln_g, conv_ln_b, conv_b_pw2, pool_b, pool_scale,
          ffn_w_dw, ffn_b_dw, final_norm]
    smm = [m_norm_mix, m_norm_ffn, m_conv_b_pw1, m_conv_w_dw, m_conv_b_dw, m_conv_ln_g, m_conv_ln_b, m_conv_b_pw2, m_pool_b,
           m_pool_scale, m_ffn_w_dw, m_ffn_b_dw, m_final_norm]
    svv = [v_norm_mix, v_norm_ffn, v_conv_b_pw1, v_conv_w_dw, v_conv_b_dw, v_conv_ln_g, v_conv_ln_b, v_conv_b_pw2, v_pool_b,
           v_pool_scale, v_ffn_w_dw, v_ffn_b_dw, v_final_norm]
    sshapes = [tuple(w.shape) for w in sw]
    two_d = [(2, D), (2, D), (1, 2 * D), (taps, dsh), (1, D), (1, D), (1, D), (1, D), (1, dsh), (1, dsh), (2 * ffn_taps, fsh),
             (2, F), (1, D)]
    rows_d = [[0, 1], [2, 3], [4, 5], None, [9], [6], [7], [8], None, None, None, None, [10]]
    rows_f = [None] * 11 + [[3, 7], None]
    as2d = lambda ts: [t.reshape(s) for t, s in zip(ts, two_d)]
    loss, g_rep, sd, sm_new, sv_new = _adamw_small(rd, rf, [g_wdw, g_pb, g_ps, g_fdw], as2d(sw), as2d(smm), as2d(svv), rows_d,
                                                   rows_f, loss_row=13)
    sg = dict(g_rep)
    sg.update({3: g_wdw, 8: g_pb, 9: g_ps, 10: g_fdw})
    small_out = {n: tuple(t.reshape(sshapes[k]) for t in (sg[k], sd[k], sm_new[k], sv_new[k])) for k, n in enumerate(snames)}

    order = [small_out["norm_mix"], small_out["norm_ffn"], o_pw1, small_out["b_pw1"], small_out["w_dw"], small_out["b_dw"],
             small_out["ln_g"], small_out["ln_b"], o_pw2, small_out["b_pw2"], o_pool, small_out["pool_b"],
             small_out["pool_scale"], o_up, small_out["ffn_w_dw"], small_out["ffn_b_dw"], o_down, small_out["final_norm"]]
    return (loss[0, 0], grad_x.reshape(nb, seq, D), *[o[0] for o in order], *[o[1] for o in order], *[o[2] for o in order],
            *[o[3] for o in order])
```
